```python
import jax, jax.numpy as jnp
from jax import lax
import numpy as np

D_MODEL = 1024
BATCH = 8
SEQ = 8192
DEPTH = 2

N_MIXERS = 2
N_A = (DEPTH + 1) // 2
N_B = DEPTH // 2
N_META = 16
EXPAND = 2
E_CONV = EXPAND * D_MODEL
CONV_WIDTH = 3
E_MLSTM = EXPAND * D_MODEL
N_HEADS = 4
DV = E_MLSTM // N_HEADS
DK = DV // 2
QK = N_HEADS * DK
CHUNK = 64
RMS_EPS = 1e-6

kernel_name = "hybrid_shortconv_mlstm_interleaved"


def _rmsnorm(x, w):
    xf = x.astype(jnp.float32)
    y = xf * lax.rsqrt(jnp.mean(xf * xf, axis=-1, keepdims=True) + RMS_EPS)
    return (y * w.astype(jnp.float32)).astype(x.dtype)


def _short_conv_mixer(u, w_in, w_conv, w_out):
    proj = u @ w_in
    b_gate, c_gate, xin, z = jnp.split(proj, 4, axis=-1)
    y = lax.conv_general_dilated(
        c_gate * xin, w_conv[:, None, :].astype(u.dtype),
        window_strides=(1,), padding=[(CONV_WIDTH - 1, 0)],
        dimension_numbers=("NWC", "WIO", "NWC"), feature_group_count=E_CONV)
    return (jax.nn.silu(z) * b_gate * y) @ w_out


def _mlstm_chunk(carry, inp):
    c_state, n_state, m_state = carry
    q, k, v, logi, logf = inp
    L = q.shape[-2]
    b = jnp.cumsum(logf, axis=-1)
    causal = jnp.tril(jnp.ones((L, L), dtype=bool))
    log_d = jnp.where(causal, b[..., :, None] - b[..., None, :] + logi[..., None, :], -jnp.inf)
    log_inter = b + m_state[..., None]
    m_row = jnp.maximum(log_inter, jnp.max(log_d, axis=-1))
    d = jnp.exp(log_d - m_row[..., None])
    inter = jnp.exp(log_inter - m_row)
    s = jnp.einsum("bhtd,bhsd->bhts", q, k) * d
    num = jnp.einsum("bhts,bhsv->bhtv", s, v) + inter[..., None] * jnp.einsum("bhtd,bhdv->bhtv", q, c_state)
    den = jnp.sum(s, axis=-1) + inter * jnp.einsum("bhtd,bhd->bht", q, n_state)
    h = num / jnp.maximum(jnp.abs(den), jnp.exp(-m_row))[..., None]
    log_w = b[..., -1:] - b + logi
    m_new = jnp.maximum(b[..., -1] + m_state, jnp.max(log_w, axis=-1))
    decay = jnp.exp(b[..., -1] + m_state - m_new)
    w = jnp.exp(log_w - m_new[..., None])
    c_new = decay[..., None, None] * c_state + jnp.einsum("bhs,bhsd,bhsv->bhdv", w, k, v)
    n_new = decay[..., None] * n_state + jnp.einsum("bhs,bhsd->bhd", w, k)
    return (c_new, n_new, m_new), h


def _mlstm_mixer(u, w_in, gate_b, head_norm_w, w_out):
    bsz, T, _ = u.shape
    proj = u @ w_in
    q, k, v, o, z, g = jnp.split(
        proj, [QK, 2 * QK, 2 * QK + E_MLSTM, 2 * QK + 2 * E_MLSTM, 2 * QK + 3 * E_MLSTM], axis=-1)
    g = g.astype(jnp.float32) + gate_b.astype(jnp.float32)
    logi = jnp.transpose(g[..., :N_HEADS], (0, 2, 1))
    logf = jnp.transpose(jax.nn.log_sigmoid(g[..., N_HEADS:]), (0, 2, 1))

    def heads(a, dh):
        return a.astype(jnp.float32).reshape(bsz, T, N_HEADS, dh).transpose(0, 2, 1, 3)

    q = heads(q, DK) * (DK ** -0.5)
    k = heads(k, DK)
    v = heads(v, DV)
    seqs = (q, k, v, logi, logf)

    def chunks(a):
        real = a[:, :, N_META:]
        nc = real.shape[2] // CHUNK
        return jnp.moveaxis(real.reshape(bsz, N_HEADS, nc, CHUNK, *real.shape[3:]), 2, 0)

    carry0 = (jnp.zeros((bsz, N_HEADS, DK, DV), jnp.float32),
              jnp.zeros((bsz, N_HEADS, DK), jnp.float32),
              jnp.zeros((bsz, N_HEADS), jnp.float32))
    carry, h_meta = _mlstm_chunk(carry0, tuple(a[:, :, :N_META] for a in seqs))
    _, h_real = lax.scan(_mlstm_chunk, carry, tuple(chunks(a) for a in seqs))
    h_real = jnp.moveaxis(h_real, 0, 2).reshape(bsz, N_HEADS, T - N_META, DV)
    h = jnp.concatenate([h_meta, h_real], axis=2).transpose(0, 2, 1, 3)
    h = h * lax.rsqrt(jnp.mean(h * h, axis=-1, keepdims=True) + RMS_EPS)
    h = h * head_norm_w.astype(jnp.float32).reshape(N_HEADS, DV)
    h = h.reshape(bsz, T, E_MLSTM).astype(u.dtype) * jax.nn.sigmoid(o) * jax.nn.silu(z)
    return h @ w_out


def _fwd_setup_inputs(seed: int = 0) -> dict:
    key = jax.random.key(seed)
    ks = jax.random.split(key, 12)
    f32 = jnp.float32
    x = jax.random.normal(ks[0], (BATCH, SEQ, D_MODEL), f32)
    meta_tokens = jax.random.normal(ks[1], (N_META, D_MODEL), f32)
    norm_w = 1.0 + 0.05 * jax.random.normal(ks[2], (DEPTH, D_MODEL), f32)
    conv_in_w = jax.random.normal(ks[3], (N_A, D_MODEL, 4 * E_CONV), f32) * D_MODEL ** -0.5
    conv_w = jax.random.normal(ks[4], (N_A, CONV_WIDTH, E_CONV), f32) * CONV_WIDTH ** -0.5
    conv_out_w = jax.random.normal(ks[5], (N_A, E_CONV, D_MODEL), f32) * E_CONV ** -0.5
    n_in = 2 * QK + 3 * E_MLSTM + 2 * N_HEADS
    mlstm_in_w = jax.random.normal(ks[6], (N_B, D_MODEL, n_in), f32) * D_MODEL ** -0.5
    i_bias = 0.1 * jax.random.normal(ks[7], (N_B, N_HEADS), f32)
    f_bias = jnp.linspace(3.0, 6.0, N_HEADS, dtype=f32)[None, :] + 0.1 * jax.random.normal(ks[8], (N_B, N_HEADS), f32)
    mlstm_gate_b = jnp.concatenate([i_bias, f_bias], axis=-1)
    mlstm_head_norm_w = 1.0 + 0.05 * jax.random.normal(ks[9], (N_B, E_MLSTM), f32)
    mlstm_out_w = jax.random.normal(ks[10], (N_B, E_MLSTM, D_MODEL), f32) * E_MLSTM ** -0.5
    final_norm_w = 1.0 + 0.05 * jax.random.normal(ks[11], (D_MODEL,), f32)
    return {"x": x, "meta_tokens": meta_tokens, "norm_w": norm_w, "conv_in_w": conv_in_w,
            "conv_w": conv_w, "conv_out_w": conv_out_w, "mlstm_in_w": mlstm_in_w,
            "mlstm_gate_b": mlstm_gate_b, "mlstm_head_norm_w": mlstm_head_norm_w,
            "mlstm_out_w": mlstm_out_w, "final_norm_w": final_norm_w}


def _fwd_reference(x, meta_tokens, norm_w, conv_in_w, conv_w, conv_out_w, mlstm_in_w,
              mlstm_gate_b, mlstm_head_norm_w, mlstm_out_w, final_norm_w):
    bsz = x.shape[0]
    meta = jnp.broadcast_to(meta_tokens.astype(x.dtype)[None], (bsz, N_META, D_MODEL))
    h = jnp.concatenate([meta, x], axis=1)
    for i in range(DEPTH):
        u = _rmsnorm(h, norm_w[i])
        j = i // N_MIXERS
        if i % N_MIXERS == 0:
            h = h + _short_conv_mixer(u, conv_in_w[j], conv_w[j], conv_out_w[j])
        else:
            h = h + _mlstm_mixer(u, mlstm_in_w[j], mlstm_gate_b[j], mlstm_head_norm_w[j], mlstm_out_w[j])
    h = _rmsnorm(h, final_norm_w)
    return h[:, N_META:]


import jax as _jax
import jax.numpy as _jnp

TWIN_FORMAT = 'train_step'
FWD_PARAMS = ['x', 'meta_tokens', 'norm_w', 'conv_in_w', 'conv_w', 'conv_out_w', 'mlstm_in_w', 'mlstm_gate_b', 'mlstm_head_norm_w', 'mlstm_out_w', 'final_norm_w']
TWIN_WEIGHTS = ['meta_tokens', 'norm_w', 'conv_in_w', 'conv_w', 'conv_out_w', 'mlstm_in_w', 'mlstm_gate_b', 'mlstm_head_norm_w', 'mlstm_out_w', 'final_norm_w']
TWIN_DIFF_INPUT = 'x'
TWIN_INPUTS = ['x', 'meta_tokens', 'norm_w', 'conv_in_w', 'conv_w', 'conv_out_w', 'mlstm_in_w', 'mlstm_gate_b', 'mlstm_head_norm_w', 'mlstm_out_w', 'final_norm_w', 'loss_target', 'm_meta_tokens', 'm_norm_w', 'm_conv_in_w', 'm_conv_w', 'm_conv_out_w', 'm_mlstm_in_w', 'm_mlstm_gate_b', 'm_mlstm_head_norm_w', 'm_mlstm_out_w', 'm_final_norm_w', 'v_meta_tokens', 'v_norm_w', 'v_conv_in_w', 'v_conv_w', 'v_conv_out_w', 'v_mlstm_in_w', 'v_mlstm_gate_b', 'v_mlstm_head_norm_w', 'v_mlstm_out_w', 'v_final_norm_w']
TWIN_OUTPUTS = ['loss', 'grad_x', 'grad_meta_tokens', 'grad_norm_w', 'grad_conv_in_w', 'grad_conv_w', 'grad_conv_out_w', 'grad_mlstm_in_w', 'grad_mlstm_gate_b', 'grad_mlstm_head_norm_w', 'grad_mlstm_out_w', 'grad_final_norm_w', 'delta_meta_tokens', 'delta_norm_w', 'delta_conv_in_w', 'delta_conv_w', 'delta_conv_out_w', 'delta_mlstm_in_w', 'delta_mlstm_gate_b', 'delta_mlstm_head_norm_w', 'delta_mlstm_out_w', 'delta_final_norm_w', 'new_m_meta_tokens', 'new_m_norm_w', 'new_m_conv_in_w', 'new_m_conv_w', 'new_m_conv_out_w', 'new_m_mlstm_in_w', 'new_m_mlstm_gate_b', 'new_m_mlstm_head_norm_w', 'new_m_mlstm_out_w', 'new_m_final_norm_w', 'new_v_meta_tokens', 'new_v_norm_w', 'new_v_conv_in_w', 'new_v_conv_w', 'new_v_conv_out_w', 'new_v_mlstm_in_w', 'new_v_mlstm_gate_b', 'new_v_mlstm_head_norm_w', 'new_v_mlstm_out_w', 'new_v_final_norm_w']
TWIN_LEAF_KINDS = {'loss': 'loss', 'grad_x': 'grad_x', 'grad_meta_tokens': 'grad_w', 'grad_norm_w': 'grad_w', 'grad_conv_in_w': 'grad_w', 'grad_conv_w': 'grad_w', 'grad_conv_out_w': 'grad_w', 'grad_mlstm_in_w': 'grad_w', 'grad_mlstm_gate_b': 'grad_w', 'grad_mlstm_head_norm_w': 'grad_w', 'grad_mlstm_out_w': 'grad_w', 'grad_final_norm_w': 'grad_w', 'delta_meta_tokens': 'delta_w', 'delta_norm_w': 'delta_w', 'delta_conv_in_w': 'delta_w', 'delta_conv_w': 'delta_w', 'delta_conv_out_w': 'delta_w', 'delta_mlstm_in_w': 'delta_w', 'delta_mlstm_gate_b': 'delta_w', 'delta_mlstm_head_norm_w': 'delta_w', 'delta_mlstm_out_w': 'delta_w', 'delta_final_norm_w': 'delta_w', 'new_m_meta_tokens': 'new_m', 'new_m_norm_w': 'new_m', 'new_m_conv_in_w': 'new_m', 'new_m_conv_w': 'new_m', 'new_m_conv_out_w': 'new_m', 'new_m_mlstm_in_w': 'new_m', 'new_m_mlstm_gate_b': 'new_m', 'new_m_mlstm_head_norm_w': 'new_m', 'new_m_mlstm_out_w': 'new_m', 'new_m_final_norm_w': 'new_m', 'new_v_meta_tokens': 'new_v', 'new_v_norm_w': 'new_v', 'new_v_conv_in_w': 'new_v', 'new_v_conv_w': 'new_v', 'new_v_conv_out_w': 'new_v', 'new_v_mlstm_in_w': 'new_v', 'new_v_mlstm_gate_b': 'new_v', 'new_v_mlstm_head_norm_w': 'new_v', 'new_v_mlstm_out_w': 'new_v', 'new_v_final_norm_w': 'new_v'}


def _forward(args):
    return _fwd_reference(*[args[k] for k in FWD_PARAMS])


def _output_shape():
    def fwd():
        inp = _fwd_setup_inputs(0)
        return _fwd_reference(*[inp[k] for k in FWD_PARAMS])
    out = _jax.eval_shape(fwd)
    return out.shape, out.dtype

N_MICROBATCH = 1
ADAM_LR = 0.001
ADAM_B1 = 0.9
ADAM_B2 = 0.999
ADAM_EPS = 1e-08
ADAM_WD = 0.01
ADAM_STEP = 10
PER_EXAMPLE_BATCH_AXIS = {'x': 0, 'loss_target': 0}
SHARED_INPUTS = []
_WEIGHT_DTYPES = {'meta_tokens': _jnp.float32, 'norm_w': _jnp.float32, 'conv_in_w': _jnp.float32, 'conv_w': _jnp.float32, 'conv_out_w': _jnp.float32, 'mlstm_in_w': _jnp.float32, 'mlstm_gate_b': _jnp.float32, 'mlstm_head_norm_w': _jnp.float32, 'mlstm_out_w': _jnp.float32, 'final_norm_w': _jnp.float32}
MOMENT_SCALE = {'meta_tokens': 9.592117e-03, 'norm_w': 2.357077e-01, 'conv_in_w': 1.048813e-01, 'conv_w': 1.047224e-01, 'conv_out_w': 1.484724e-01, 'mlstm_in_w': 5.533903e-02, 'mlstm_gate_b': 4.906017e-01, 'mlstm_head_norm_w': 4.913607e-02, 'mlstm_out_w': 6.789321e-02, 'final_norm_w': 6.412118e+01}


def _to_microbatches(a, axis):
    t = _jnp.moveaxis(a, axis, 0)
    t = t.reshape((N_MICROBATCH, t.shape[0] // N_MICROBATCH) + t.shape[1:])
    return _jnp.moveaxis(t, 1, axis + 1)


def setup_inputs(seed: int = 0) -> dict:
    inp = _fwd_setup_inputs(seed)
    key = _jax.random.fold_in(_jax.random.key(seed), 7919)
    shape, _ = _output_shape()
    out = dict(inp)
    out["loss_target"] = _jax.random.normal(_jax.random.fold_in(key, 0), shape, _jnp.float32)
    for i, name in enumerate(TWIN_WEIGHTS):
        w = inp[name].astype(_jnp.float32)
        if MOMENT_SCALE is None:
            s = _jnp.sqrt(_jnp.mean(_jnp.square(w)) + 1e-30)
        else:
            s = MOMENT_SCALE[name]
        km, kv = _jax.random.split(_jax.random.fold_in(key, i + 1))
        out[name] = w
        out["m_" + name] = s * _jax.random.normal(km, w.shape, _jnp.float32)
        out["v_" + name] = (s * s) * _jax.random.uniform(kv, w.shape, _jnp.float32, 0.5, 1.5)
    if N_MICROBATCH > 1:
        for name, axis in PER_EXAMPLE_BATCH_AXIS.items():
            out[name] = _to_microbatches(out[name], axis)
    return {'x': out['x'], 'meta_tokens': out['meta_tokens'], 'norm_w': out['norm_w'], 'conv_in_w': out['conv_in_w'], 'conv_w': out['conv_w'], 'conv_out_w': out['conv_out_w'], 'mlstm_in_w': out['mlstm_in_w'], 'mlstm_gate_b': out['mlstm_gate_b'], 'mlstm_head_norm_w': out['mlstm_head_norm_w'], 'mlstm_out_w': out['mlstm_out_w'], 'final_norm_w': out['final_norm_w'], 'loss_target': out['loss_target'], 'm_meta_tokens': out['m_meta_tokens'], 'm_norm_w': out['m_norm_w'], 'm_conv_in_w': out['m_conv_in_w'], 'm_conv_w': out['m_conv_w'], 'm_conv_out_w': out['m_conv_out_w'], 'm_mlstm_in_w': out['m_mlstm_in_w'], 'm_mlstm_gate_b': out['m_mlstm_gate_b'], 'm_mlstm_head_norm_w': out['m_mlstm_head_norm_w'], 'm_mlstm_out_w': out['m_mlstm_out_w'], 'm_final_norm_w': out['m_final_norm_w'], 'v_meta_tokens': out['v_meta_tokens'], 'v_norm_w': out['v_norm_w'], 'v_conv_in_w': out['v_conv_in_w'], 'v_conv_w': out['v_conv_w'], 'v_conv_out_w': out['v_conv_out_w'], 'v_mlstm_in_w': out['v_mlstm_in_w'], 'v_mlstm_gate_b': out['v_mlstm_gate_b'], 'v_mlstm_head_norm_w': out['v_mlstm_head_norm_w'], 'v_mlstm_out_w': out['v_mlstm_out_w'], 'v_final_norm_w': out['v_final_norm_w']}


def _loss(weights, diff, rest, loss_target):
    with _jax.named_scope("forward"):
        args = {**rest, TWIN_DIFF_INPUT: diff, **{k: w.astype(_WEIGHT_DTYPES[k]) for k, w in weights.items()}}
        y = _forward(args)
    with _jax.named_scope("loss_head"):
        err = _jnp.square(y.astype(_jnp.float32) - loss_target)
        return 0.5 * _jnp.sum(_jnp.mean(err, axis=-1)) if err.ndim else 0.5 * err


def _adamw(w, g, m, v):
    m = ADAM_B1 * m + (1.0 - ADAM_B1) * g
    v = ADAM_B2 * v + (1.0 - ADAM_B2) * _jnp.square(g)
    m_hat = m / (1.0 - ADAM_B1 ** ADAM_STEP)
    v_hat = v / (1.0 - ADAM_B2 ** ADAM_STEP)
    delta = -ADAM_LR * (m_hat / (_jnp.sqrt(v_hat) + ADAM_EPS) + ADAM_WD * w)
    return delta, m, v


def reference(x, meta_tokens, norm_w, conv_in_w, conv_w, conv_out_w, mlstm_in_w, mlstm_gate_b, mlstm_head_norm_w, mlstm_out_w, final_norm_w, loss_target, m_meta_tokens, m_norm_w, m_conv_in_w, m_conv_w, m_conv_out_w, m_mlstm_in_w, m_mlstm_gate_b, m_mlstm_head_norm_w, m_mlstm_out_w, m_final_norm_w, v_meta_tokens, v_norm_w, v_conv_in_w, v_conv_w, v_conv_out_w, v_mlstm_in_w, v_mlstm_gate_b, v_mlstm_head_norm_w, v_mlstm_out_w, v_final_norm_w):
    given = dict(x=x, meta_tokens=meta_tokens, norm_w=norm_w, conv_in_w=conv_in_w, conv_w=conv_w, conv_out_w=conv_out_w, mlstm_in_w=mlstm_in_w, mlstm_gate_b=mlstm_gate_b, mlstm_head_norm_w=mlstm_head_norm_w, mlstm_out_w=mlstm_out_w, final_norm_w=final_norm_w, loss_target=loss_target, m_meta_tokens=m_meta_tokens, m_norm_w=m_norm_w, m_conv_in_w=m_conv_in_w, m_conv_w=m_conv_w, m_conv_out_w=m_conv_out_w, m_mlstm_in_w=m_mlstm_in_w, m_mlstm_gate_b=m_mlstm_gate_b, m_mlstm_head_norm_w=m_mlstm_head_norm_w, m_mlstm_out_w=m_mlstm_out_w, m_final_norm_w=m_final_norm_w, v_meta_tokens=v_meta_tokens, v_norm_w=v_norm_w, v_conv_in_w=v_conv_in_w, v_conv_w=v_conv_w, v_conv_out_w=v_conv_out_w, v_mlstm_in_w=v_mlstm_in_w, v_mlstm_gate_b=v_mlstm_gate_b, v_mlstm_head_norm_w=v_mlstm_head_norm_w, v_mlstm_out_w=v_mlstm_out_w, v_final_norm_w=v_final_norm_w)
    weights = {n: given[n] for n in TWIN_WEIGHTS}
    shared = {n: given[n] for n in SHARED_INPUTS}
    per_example = {n: given[n] for n in ['x']}
    grad_fn = _jax.value_and_grad(_loss, argnums=(0, 1))

    def one_microbatch(ex, loss_target):
        ex = dict(ex)
        diff = ex.pop(TWIN_DIFF_INPUT)
        return grad_fn(weights, diff, {**shared, **ex}, loss_target)

    if N_MICROBATCH == 1:
        loss, (grad_w, grad_x) = one_microbatch(per_example, given["loss_target"])
    else:
        def body(carry, xs):
            loss_sum, grad_sum = carry
            l_k, (gw_k, gx_k) = one_microbatch(xs[0], xs[1])
            with _jax.named_scope("update"):
                return (loss_sum + l_k, _jax.tree.map(_jnp.add, grad_sum, gw_k)), gx_k

        init = (_jnp.zeros((), _jnp.float32), _jax.tree.map(_jnp.zeros_like, weights))
        (loss, grad_w), grad_x = _jax.lax.scan(body, init, (per_example, given["loss_target"]))
    with _jax.named_scope("update"):
        delta_w, new_m, new_v = {}, {}, {}
        for n in TWIN_WEIGHTS:
            delta_w[n], new_m[n], new_v[n] = _adamw(weights[n], grad_w[n], given["m_" + n], given["v_" + n])
    return (loss, grad_x, *[grad_w[n] for n in TWIN_WEIGHTS], *[delta_w[n] for n in TWIN_WEIGHTS],
            *[new_m[n] for n in TWIN_WEIGHTS], *[new_v[n] for n in TWIN_WEIGHTS])
```

```python
import functools

import jax
import jax.numpy as jnp
from jax import lax
from jax.experimental import pallas as pl
from jax.experimental.pallas import tpu as pltpu

F32 = jnp.float32
BF16 = jnp.bfloat16
MESH = pl.DeviceIdType.MESH

N_DEV = 8
N_META = 16
N_HEADS = 4
CHUNK = 64
LEAD = 128
PAD_ROWS = LEAD - N_META
RMS_EPS = 1e-6
NEG = -1e30
LANES = 128
VMEM_LIMIT = 48 * 1024 * 1024

ADAM_LR = 0.001
ADAM_B1 = 0.9
ADAM_B2 = 0.999
ADAM_EPS = 1e-08
ADAM_WD = 0.01
ADAM_STEP = 10

HIGHEST = lax.Precision.HIGHEST


def _pick(n, target, mult):
    best = None
    for d in range(mult, min(n, target) + 1, mult):
        if n % d == 0:
            best = d
    return n if best is None else best


def _params(**kw):
    return pltpu.CompilerParams(vmem_limit_bytes=VMEM_LIMIT, **kw)


def _sigmoid(x):
    return 1.0 / (1.0 + jnp.exp(-x))


def _matmul(a, b, *, form, m, n, kdim, tm, tn, tk, out_dtype, name,
            a_spec=None, b_spec=None, out_spec=None, out_shape=None, residual=None):
    ni, nj, nk = m // tm, n // tn, kdim // tk
    assert ni * tm == m and nj * tn == n and nk * tk == kdim, (name, m, n, kdim, tm, tn, tk)
    if form == "nn":
        dn = (((1,), (0,)), ((), ()))
        a_def = pl.BlockSpec((tm, tk), lambda i, j, k: (i, k))
        b_def = pl.BlockSpec((tk, tn), lambda i, j, k: (k, j))
    elif form == "nt":
        dn = (((1,), (1,)), ((), ()))
        a_def = pl.BlockSpec((tm, tk), lambda i, j, k: (i, k))
        b_def = pl.BlockSpec((tn, tk), lambda i, j, k: (j, k))
    else:
        dn = (((0,), (0,)), ((), ()))
        a_def = pl.BlockSpec((tk, tm), lambda i, j, k: (k, i))
        b_def = pl.BlockSpec((tk, tn), lambda i, j, k: (k, j))
    a_spec = a_def if a_spec is None else a_spec
    b_spec = b_def if b_spec is None else b_spec
    o_spec = pl.BlockSpec((tm, tn), lambda i, j, k: (i, j)) if out_spec is None else out_spec
    has_res = residual is not None

    def body(*refs):
        if has_res:
            a_ref, b_ref, r_ref, o_ref = refs[:4]
        else:
            a_ref, b_ref, o_ref = refs[:3]
            r_ref = None
        part = lax.dot_general(a_ref[...].astype(BF16), b_ref[...].astype(BF16), dn,
                               preferred_element_type=F32)

        def finish(acc):
            if has_res:
                acc = acc + r_ref[...].astype(F32)
            o_ref[...] = acc.astype(o_ref.dtype)

        if nk == 1:
            finish(part)
        else:
            acc_ref = refs[-1]
            k = pl.program_id(2)

            @pl.when(k == 0)
            def _():
                acc_ref[...] = part

            @pl.when(k > 0)
            def _():
                acc_ref[...] += part

            @pl.when(k == nk - 1)
            def _():
                finish(acc_ref[...])

    in_specs = [a_spec, b_spec]
    args = [a, b]
    if has_res:
        in_specs.append(pl.BlockSpec((tm, tn), lambda i, j, k: (i, j)))
        args.append(residual)
    return pl.pallas_call(
        body, name=name, grid=(ni, nj, nk),
        in_specs=in_specs, out_specs=o_spec,
        out_shape=jax.ShapeDtypeStruct((m, n) if out_shape is None else out_shape, out_dtype),
        scratch_shapes=[] if nk == 1 else [pltpu.VMEM((tm, tn), F32)],
        compiler_params=_params(),
    )(*args)


def _rms_fwd(h, w, name):
    t, d = h.shape
    tm = _pick(t, 416, 16)

    def body(h_ref, w_ref, u_ref):
        x = h_ref[...]
        r = lax.rsqrt(jnp.mean(x * x, axis=-1, keepdims=True) + RMS_EPS)
        u_ref[...] = ((x * r) * w_ref[...]).astype(BF16)

    return pl.pallas_call(
        body, name=name, grid=(t // tm,),
        in_specs=[pl.BlockSpec((tm, d), lambda i: (i, 0)), pl.BlockSpec((1, d), lambda i: (0, 0))],
        out_specs=pl.BlockSpec((tm, d), lambda i: (i, 0)),
        out_shape=jax.ShapeDtypeStruct((t, d), BF16),
        compiler_params=_params(),
    )(h, w)


def _rms_bwd(h, w, du, dres, name):
    t, d = h.shape
    tm = _pick(t, 416, 16)

    def body(h_ref, w_ref, du_ref, dres_ref, dh_ref, dw_ref):
        i = pl.program_id(0)
        x = h_ref[...]
        g = du_ref[...].astype(F32)
        r = lax.rsqrt(jnp.mean(x * x, axis=-1, keepdims=True) + RMS_EPS)
        gw = g * w_ref[...]
        dot = jnp.mean(gw * x, axis=-1, keepdims=True)
        dh_ref[...] = dres_ref[...] + (r * gw - x * ((r * r * r) * dot))
        part = jnp.sum(g * (x * r), axis=0, keepdims=True)

        @pl.when(i == 0)
        def _():
            dw_ref[...] = jnp.zeros_like(dw_ref)

        dw_ref[...] += jnp.broadcast_to(part, dw_ref.shape)

    row = pl.BlockSpec((tm, d), lambda i: (i, 0))
    return pl.pallas_call(
        body, name=name, grid=(t // tm,),
        in_specs=[row, pl.BlockSpec((1, d), lambda i: (0, 0)), row, row],
        out_specs=[row, pl.BlockSpec((8, d), lambda i: (0, 0))],
        out_shape=[jax.ShapeDtypeStruct((t, d), F32), jax.ShapeDtypeStruct((8, d), F32)],
        compiler_params=_params(),
    )(h, w, du, dres)


def _final_loss(h, w, tgt, name):
    t, d = h.shape
    tm = _pick(t, 416, 16)

    def body(h_ref, w_ref, t_ref, dh_ref, dw_ref, loss_ref):
        i = pl.program_id(0)
        x = h_ref[...]
        r = lax.rsqrt(jnp.mean(x * x, axis=-1, keepdims=True) + RMS_EPS)
        yn = x * r
        y = yn * w_ref[...]
        rows = i * tm + lax.broadcasted_iota(jnp.int32, (tm, 1), 0)
        diff = jnp.where(rows >= LEAD, y - t_ref[...], 0.0)
        tile_loss = 0.5 * jnp.sum(jnp.mean(diff * diff, axis=-1, keepdims=True), axis=0, keepdims=True)
        dy = diff / d
        gw = dy * w_ref[...]
        dot = jnp.mean(gw * x, axis=-1, keepdims=True)
        dh_ref[...] = r * gw - x * ((r * r * r) * dot)
        part = jnp.sum(dy * yn, axis=0, keepdims=True)

        @pl.when(i == 0)
        def _():
            dw_ref[...] = jnp.zeros_like(dw_ref)
            loss_ref[...] = jnp.zeros_like(loss_ref)

        dw_ref[...] += jnp.broadcast_to(part, dw_ref.shape)
        loss_ref[...] += jnp.broadcast_to(tile_loss, loss_ref.shape)

    row = pl.BlockSpec((tm, d), lambda i: (i, 0))
    return pl.pallas_call(
        body, name=name, grid=(t // tm,),
        in_specs=[row, pl.BlockSpec((1, d), lambda i: (0, 0)), row],
        out_specs=[row, pl.BlockSpec((8, d), lambda i: (0, 0)), pl.BlockSpec((8, LANES), lambda i: (0, 0))],
        out_shape=[jax.ShapeDtypeStruct((t, d), F32), jax.ShapeDtypeStruct((8, d), F32),
                   jax.ShapeDtypeStruct((8, LANES), F32)],
        compiler_params=_params(),
    )(h, w, tgt)


def _conv_gate_fwd(p0, conv_w, name):
    t = p0.shape[0]
    ec = p0.shape[1] // 8
    tm = _pick(t, 416, 16)
    nt = t // tm

    def body(p_ref, w_ref, g_ref, y_ref, ext_ref):
        i = pl.program_id(1)
        bg = p_ref[:, 0:ec].astype(F32)
        cg = p_ref[:, ec:2 * ec].astype(F32)
        xin = p_ref[:, 2 * ec:3 * ec].astype(F32)
        z = p_ref[:, 3 * ec:4 * ec].astype(F32)
        a = cg * xin

        @pl.when(i == 0)
        def _():
            ext_ref[0:8, :] = jnp.zeros((8, ec), F32)

        ext_ref[8:8 + tm, :] = a
        y = (w_ref[0:1, :] * ext_ref[6:6 + tm, :] + w_ref[1:2, :] * ext_ref[7:7 + tm, :]
             + w_ref[2:3, :] * a)
        ext_ref[0:8, :] = a[tm - 8:tm, :]
        y_ref[...] = y.astype(BF16)
        g_ref[...] = ((z * _sigmoid(z)) * bg * y).astype(BF16)

    out = pl.BlockSpec((tm, ec), lambda j, i: (i, j))
    return pl.pallas_call(
        body, name=name, grid=(2, nt),
        in_specs=[pl.BlockSpec((tm, 4 * ec), lambda j, i: (i, j)), pl.BlockSpec((3, ec), lambda j, i: (0, j))],
        out_specs=[out, out],
        out_shape=[jax.ShapeDtypeStruct((t, 2 * ec), BF16)] * 2,
        scratch_shapes=[pltpu.VMEM((tm + 8, ec), F32)],
        compiler_params=_params(),
    )(p0, conv_w)


def _conv_gate_bwd(p0, y, dg, conv_w, name):
    t = p0.shape[0]
    ec = p0.shape[1] // 8
    tm = _pick(t, 208, 16)
    nt = t // tm

    def body(p_ref, y_ref, dg_ref, w_ref, dp_ref, dw_ref, ext_ref):
        i = pl.program_id(1)
        bg = p_ref[:, 0:ec].astype(F32)
        cg = p_ref[:, ec:2 * ec].astype(F32)
        xin = p_ref[:, 2 * ec:3 * ec].astype(F32)
        z = p_ref[:, 3 * ec:4 * ec].astype(F32)
        yv = y_ref[...].astype(F32)
        dgv = dg_ref[...].astype(F32)
        sig = _sigmoid(z)
        sz = z * sig
        dp_ref[:, 3 * ec:4 * ec] = (dgv * bg * yv * (sig * (1.0 + z * (1.0 - sig)))).astype(BF16)
        dp_ref[:, 0:ec] = (dgv * sz * yv).astype(BF16)
        dy = dgv * sz * bg

        @pl.when(i == 0)
        def _():
            ext_ref[tm:tm + 8, :] = jnp.zeros((8, ec), F32)
            dw_ref[...] = jnp.zeros_like(dw_ref)

        ext_ref[0:tm, :] = dy
        dy1 = ext_ref[1:tm + 1, :]
        dy2 = ext_ref[2:tm + 2, :]
        ext_ref[tm:tm + 8, :] = dy[0:8, :]
        da = w_ref[0:1, :] * dy2 + w_ref[1:2, :] * dy1 + w_ref[2:3, :] * dy
        dp_ref[:, ec:2 * ec] = (da * xin).astype(BF16)
        dp_ref[:, 2 * ec:3 * ec] = (da * cg).astype(BF16)
        a = cg * xin
        dw_ref[0:1, :] += jnp.sum(a * dy2, axis=0, keepdims=True)
        dw_ref[1:2, :] += jnp.sum(a * dy1, axis=0, keepdims=True)
        dw_ref[2:3, :] += jnp.sum(a * dy, axis=0, keepdims=True)

    rev = lambda j, i: (nt - 1 - i, j)
    return pl.pallas_call(
        body, name=name, grid=(2, nt),
        in_specs=[pl.BlockSpec((tm, 4 * ec), rev), pl.BlockSpec((tm, ec), rev), pl.BlockSpec((tm, ec), rev),
                  pl.BlockSpec((3, ec), lambda j, i: (0, j))],
        out_specs=[pl.BlockSpec((tm, 4 * ec), rev), pl.BlockSpec((8, ec), lambda j, i: (0, j))],
        out_shape=[jax.ShapeDtypeStruct((t, 8 * ec), BF16), jax.ShapeDtypeStruct((8, 2 * ec), F32)],
        scratch_shapes=[pltpu.VMEM((tm + 8, ec), F32)],
        compiler_params=_params(),
    )(p0, y, dg, conv_w)


def _log_sigmoid(x):
    return jnp.minimum(x, 0.0) - jnp.log(1.0 + jnp.exp(-jnp.abs(x)))


def _gates_fwd(graw, graw_t, bias_row, bias_col, name):
    t = graw.shape[0]
    tm = _pick(t, 640, 128)
    cpt = tm // CHUNK
    nh = N_HEADS

    def body(g_ref, gt_ref, br_ref, bc_ref, colf_ref, rowf_ref):
        i = pl.program_id(0)
        gc = g_ref[...] + br_ref[...]
        rows = i * tm + lax.broadcasted_iota(jnp.int32, (tm, 1), 0)
        live = rows >= PAD_ROWS
        lf = jnp.where(live, _log_sigmoid(gc), 0.0)
        li = jnp.where(live, gc, NEG)
        gt = gt_ref[...] + bc_ref[...]
        cols = i * tm + lax.broadcasted_iota(jnp.int32, (1, tm), 1)
        live_t = cols >= PAD_ROWS
        lf_t = jnp.where(live_t, _log_sigmoid(gt), 0.0)
        li_t = jnp.where(live_t, gt, NEG)
        ri = lax.broadcasted_iota(jnp.int32, (CHUNK, CHUNK), 0)
        ci = lax.broadcasted_iota(jnp.int32, (CHUNK, CHUNK), 1)
        lower = (ri >= ci).astype(F32)
        upper = (ri <= ci).astype(F32)
        lane = lax.broadcasted_iota(jnp.int32, (CHUNK, LANES), 1)
        sub = lax.broadcasted_iota(jnp.int32, (8, CHUNK), 0)
        for c in range(cpt):
            sl = slice(c * CHUNK, (c + 1) * CHUNK)
            b_all = jnp.dot(lower, lf[sl, :], precision=HIGHEST, preferred_element_type=F32)
            bt_all = jnp.dot(lf_t[:, sl], upper, precision=HIGHEST, preferred_element_type=F32)
            for h in range(nh):
                b = b_all[:, nh + h:nh + h + 1]
                r = li[sl, h:h + 1] - b
                pre = gc[sl, nh + h:nh + h + 1]
                colf_ref[h, sl, :] = jnp.where(lane == 0, b, jnp.where(lane == 1, r, jnp.where(lane == 2, pre, 0.0)))
                b_row = bt_all[nh + h:nh + h + 1, :]
                r_row = li_t[h:h + 1, sl] - b_row
                rowf_ref[h, c, :, :] = jnp.where(sub == 0, r_row, jnp.where(sub == 1, b_row, 0.0))

    return pl.pallas_call(
        body, name=name, grid=(t // tm,),
        in_specs=[pl.BlockSpec((tm, LANES), lambda i: (i, 0)), pl.BlockSpec((16, tm), lambda i: (0, i)),
                  pl.BlockSpec((1, LANES), lambda i: (0, 0)), pl.BlockSpec((16, 1), lambda i: (0, 0))],
        out_specs=[pl.BlockSpec((nh, tm, LANES), lambda i: (0, i, 0)),
                   pl.BlockSpec((nh, cpt, 8, CHUNK), lambda i: (0, i, 0, 0))],
        out_shape=[jax.ShapeDtypeStruct((nh, t, LANES), F32),
                   jax.ShapeDtypeStruct((nh, t // CHUNK, 8, CHUNK), F32)],
        compiler_params=_params(),
    )(graw, graw_t, bias_row, bias_col)


def _gates_bwd(gstat, colf, name):
    nh, t, _ = gstat.shape
    tm = _pick(t, 640, 128)
    cpt = tm // CHUNK
    nt = t // tm
    nc = t // CHUNK

    def body(gs_ref, nx_ref, colf_ref, dg_ref, db_ref):
        i = pl.program_id(0)
        ri = lax.broadcasted_iota(jnp.int32, (CHUNK, CHUNK), 0)
        ci = lax.broadcasted_iota(jnp.int32, (CHUNK, CHUNK), 1)
        upper = (ri <= ci).astype(F32)
        lane = lax.broadcasted_iota(jnp.int32, (CHUNK, LANES), 1)
        total = jnp.zeros((1, LANES), F32)
        for c in range(cpt):
            sl = slice(c * CHUNK, (c + 1) * CHUNK)
            rows = i * tm + c * CHUNK + lax.broadcasted_iota(jnp.int32, (CHUNK, 1), 0)
            live = rows >= PAD_ROWS
            acc = jnp.zeros((CHUNK, LANES), F32)
            for h in range(nh):
                blk = gs_ref[h, sl, :]
                rev = jnp.dot(upper, blk, precision=HIGHEST, preferred_element_type=F32)
                if c + 1 < cpt:
                    carry = gs_ref[h, (c + 1) * CHUNK:(c + 1) * CHUNK + 1, 2:3]
                else:
                    carry = jnp.where(i == nt - 1, 0.0, nx_ref[h, 0:1, 2:3])
                dlogf = rev[:, 0:1] + carry
                pre = colf_ref[h, sl, 2:3]
                dgf = jnp.where(live, dlogf * (1.0 - _sigmoid(pre)), 0.0)
                dgi = jnp.where(live, blk[:, 1:2], 0.0)
                acc = acc + jnp.where(lane == h, dgi, 0.0) + jnp.where(lane == nh + h, dgf, 0.0)
            dg_ref[sl, :] = acc
            total = total + jnp.sum(acc, axis=0, keepdims=True)

        @pl.when(i == 0)
        def _():
            db_ref[...] = jnp.zeros_like(db_ref)

        db_ref[...] += jnp.broadcast_to(total, db_ref.shape)

    return pl.pallas_call(
        body, name=name, grid=(nt,),
        in_specs=[pl.BlockSpec((nh, tm, LANES), lambda i: (0, i, 0)),
                  pl.BlockSpec((nh, CHUNK, LANES), lambda i: (0, jnp.minimum((i + 1) * cpt, nc - 1), 0)),
                  pl.BlockSpec((nh, tm, LANES), lambda i: (0, i, 0))],
        out_specs=[pl.BlockSpec((tm, LANES), lambda i: (i, 0)), pl.BlockSpec((8, LANES), lambda i: (0, 0))],
        out_shape=[jax.ShapeDtypeStruct((t, LANES), F32), jax.ShapeDtypeStruct((8, LANES), F32)],
        compiler_params=_params(),
    )(gstat, gstat, colf)


NT_DIMS = (((1,), (1,)), ((), ()))
TN_DIMS = (((0,), (0,)), ((), ()))


def _dot(a, b):
    return jnp.dot(a.astype(BF16), b.astype(BF16), preferred_element_type=F32)


def _dot_nt(a, b):
    return lax.dot_general(a.astype(BF16), b.astype(BF16), NT_DIMS, preferred_element_type=F32)


def _dot_tn(a, b):
    return lax.dot_general(a.astype(BF16), b.astype(BF16), TN_DIMS, preferred_element_type=F32)


def _chunk_gates(colf_ref, rowf_ref, m_prev):
    b = colf_ref[:, 0:1]
    rcol = colf_ref[:, 1:2]
    rrow = rowf_ref[0:1, :]
    ri = lax.broadcasted_iota(jnp.int32, (CHUNK, CHUNK), 0)
    ci = lax.broadcasted_iota(jnp.int32, (CHUNK, CHUNK), 1)
    log_d = jnp.where(ri >= ci, b + rrow, NEG)
    m_row = jnp.maximum(b + m_prev, jnp.max(log_d, axis=-1, keepdims=True))
    dmat = jnp.exp(log_d - m_row)
    inter = jnp.exp(b + m_prev - m_row)
    b_last = b[CHUNK - 1:CHUNK, :]
    log_w = rcol + b_last
    m_new = jnp.maximum(b_last + m_prev, jnp.max(log_w, axis=0, keepdims=True))
    decay = jnp.exp(b_last + m_prev - m_new)
    w = jnp.exp(log_w - m_new)
    return m_row, dmat, inter, m_new, decay, w


def _mlstm_fwd(p1, colf, rowf, head_w, dk, dv, name):
    t = p1.shape[0]
    nh = N_HEADS
    nc = t // CHUNK
    hw = 2 * dk + 3 * dv

    def body(p_ref, colf_ref, rowf_ref, w_ref, hm_ref, hh_ref, stat_ref, cs_ref, ns_ref, c_ref, n_ref, m_ref):
        c = pl.program_id(1)

        @pl.when(c == 0)
        def _():
            c_ref[...] = jnp.zeros_like(c_ref)
            n_ref[...] = jnp.zeros_like(n_ref)
            m_ref[...] = jnp.zeros_like(m_ref)

        m_prev = m_ref[...]
        n_prev = n_ref[...]
        c_prev = c_ref[...]
        cs_ref[...] = c_prev.astype(BF16)
        sub = lax.broadcasted_iota(jnp.int32, (8, dk), 0)
        ns_ref[...] = jnp.where(sub == 0, n_prev, jnp.where(sub == 1, m_prev, 0.0))

        q = p_ref[:, 0:dk]
        k = p_ref[:, dk:2 * dk]
        v = p_ref[:, 2 * dk:2 * dk + dv]
        o = p_ref[:, 2 * dk + dv:2 * dk + 2 * dv].astype(F32)
        z = p_ref[:, 2 * dk + 2 * dv:2 * dk + 3 * dv].astype(F32)
        m_row, dmat, inter, m_new, decay, w = _chunk_gates(colf_ref, rowf_ref, m_prev)
        s = _dot_nt(q, k) * dmat
        num = _dot(s, v) + inter * _dot(q, c_prev)
        den = jnp.sum(s, axis=-1, keepdims=True) + inter * jnp.sum(q.astype(F32) * n_prev, axis=-1, keepdims=True)
        denom = jnp.maximum(jnp.abs(den), jnp.exp(-m_row))
        hh = num / denom
        hh_ref[...] = hh
        lane = lax.broadcasted_iota(jnp.int32, (CHUNK, LANES), 1)
        stat_ref[...] = jnp.where(lane == 0, den, 0.0)

        c_ref[...] = decay * c_prev + _dot_tn(k, w * v.astype(F32))
        n_ref[...] = decay * n_prev + jnp.sum(w * k.astype(F32), axis=0, keepdims=True)
        m_ref[...] = m_new

        r = lax.rsqrt(jnp.mean(hh * hh, axis=-1, keepdims=True) + RMS_EPS)
        hn = (hh * r) * w_ref[...]
        hm_ref[...] = (hn * _sigmoid(o) * (z * _sigmoid(z))).astype(BF16)

    return pl.pallas_call(
        body, name=name, grid=(nh, nc),
        in_specs=[pl.BlockSpec((CHUNK, hw), lambda h, c: (c, h)),
                  pl.BlockSpec((None, CHUNK, LANES), lambda h, c: (h, c, 0)),
                  pl.BlockSpec((None, None, 8, CHUNK), lambda h, c: (h, c, 0, 0)),
                  pl.BlockSpec((1, dv), lambda h, c: (0, h))],
        out_specs=[pl.BlockSpec((CHUNK, dv), lambda h, c: (c, h)),
                   pl.BlockSpec((CHUNK, dv), lambda h, c: (c, h)),
                   pl.BlockSpec((None, CHUNK, LANES), lambda h, c: (h, c, 0)),
                   pl.BlockSpec((None, None, dk, dv), lambda h, c: (h, c, 0, 0)),
                   pl.BlockSpec((None, None, 8, dk), lambda h, c: (h, c, 0, 0))],
        out_shape=[jax.ShapeDtypeStruct((t, nh * dv), BF16), jax.ShapeDtypeStruct((t, nh * dv), F32),
                   jax.ShapeDtypeStruct((nh, t, LANES), F32),
                   jax.ShapeDtypeStruct((nh, nc, dk, dv), BF16), jax.ShapeDtypeStruct((nh, nc, 8, dk), F32)],
        scratch_shapes=[pltpu.VMEM((dk, dv), F32), pltpu.VMEM((1, dk), F32), pltpu.VMEM((1, 1), F32)],
        compiler_params=_params(),
    )(p1, colf, rowf, head_w)


def _mlstm_bwd(p1, colf, rowf, head_w, hh, stat, csave, nsave, dhm, dk, dv, name):
    t = p1.shape[0]
    nh = N_HEADS
    nc = t // CHUNK
    hw = 2 * dk + 3 * dv

    def body(p_ref, colf_ref, rowf_ref, w_ref, hh_ref, stat_ref, cs_ref, ns_ref, dhm_ref,
             dp_ref, gs_ref, dw_ref, dc_ref, dn_ref):
        c = pl.program_id(1)

        @pl.when(c == 0)
        def _():
            dc_ref[...] = jnp.zeros_like(dc_ref)
            dn_ref[...] = jnp.zeros_like(dn_ref)
            dw_ref[...] = jnp.zeros_like(dw_ref)

        q = p_ref[:, 0:dk]
        k = p_ref[:, dk:2 * dk]
        v = p_ref[:, 2 * dk:2 * dk + dv]
        o = p_ref[:, 2 * dk + dv:2 * dk + 2 * dv].astype(F32)
        z = p_ref[:, 2 * dk + 2 * dv:2 * dk + 3 * dv].astype(F32)
        qf = q.astype(F32)
        kf = k.astype(F32)
        n_prev = ns_ref[0:1, :]
        m_prev = ns_ref[1:2, 0:1]
        c_prev = cs_ref[...]
        m_row, dmat, inter, m_new, decay, w = _chunk_gates(colf_ref, rowf_ref, m_prev)

        hh = hh_ref[...]
        dhm_v = dhm_ref[...].astype(F32)
        so = _sigmoid(o)
        sg = _sigmoid(z)
        sz = z * sg
        r = lax.rsqrt(jnp.mean(hh * hh, axis=-1, keepdims=True) + RMS_EPS)
        hn = (hh * r) * w_ref[...]
        dhn = dhm_v * so * sz
        dp_ref[:, 2 * dk + dv:2 * dk + 2 * dv] = (dhm_v * hn * sz * (so * (1.0 - so))).astype(BF16)
        dp_ref[:, 2 * dk + 2 * dv:2 * dk + 3 * dv] = (dhm_v * hn * so * (sg * (1.0 + z * (1.0 - sg)))).astype(BF16)
        dw_ref[...] += jnp.broadcast_to(jnp.sum(dhn * (hh * r), axis=0, keepdims=True), dw_ref.shape)
        gwn = dhn * w_ref[...]
        dhh = r * gwn - hh * ((r * r * r) * jnp.mean(gwn * hh, axis=-1, keepdims=True))

        den = stat_ref[:, 0:1]
        floor = jnp.exp(-m_row)
        denom = jnp.maximum(jnp.abs(den), floor)
        dnum = dhh / denom
        hdot = jnp.sum(dhh * hh, axis=-1, keepdims=True)
        dden = jnp.where(jnp.abs(den) > floor, -(hdot / denom) * jnp.sign(den), 0.0)
        s = _dot_nt(q, k) * dmat
        dqk = (_dot_nt(dnum, v) + dden) * dmat
        dc_new = dc_ref[...]
        dn_new = dn_ref[...]
        idd = inter * dden
        dq = _dot(dqk, k) + inter * _dot_nt(dnum, c_prev) + idd * n_prev
        dkv = _dot_tn(dqk, q) + w * (_dot_nt(v, dc_new) + dn_new)
        dvv = _dot_tn(s, dnum) + w * _dot(k, dc_new)
        dp_ref[:, 0:dk] = dq.astype(BF16)
        dp_ref[:, dk:2 * dk] = dkv.astype(BF16)
        dp_ref[:, 2 * dk:2 * dk + dv] = dvv.astype(BF16)
        qdq = jnp.sum(qf * dq, axis=-1, keepdims=True)
        kdk = jnp.sum(kf * dkv, axis=-1, keepdims=True)
        dc_prev = decay * dc_new + _dot_tn(inter * qf, dnum)
        dn_prev = decay * dn_new + jnp.sum(idd * qf, axis=0, keepdims=True)
        dc_ref[...] = dc_prev
        dn_ref[...] = dn_prev
        cross = (jnp.sum(jnp.sum(c_prev.astype(F32) * dc_prev, axis=-1, keepdims=True), axis=0, keepdims=True)
                 + jnp.sum(n_prev * dn_prev, axis=-1, keepdims=True))
        lane = lax.broadcasted_iota(jnp.int32, (CHUNK, LANES), 1)
        gs_ref[...] = jnp.where(lane == 0, qdq - kdk, jnp.where(lane == 1, kdk, jnp.where(lane == 2, cross, 0.0)))

    rc = lambda h, c: (nc - 1 - c, h)
    return pl.pallas_call(
        body, name=name, grid=(nh, nc),
        in_specs=[pl.BlockSpec((CHUNK, hw), rc),
                  pl.BlockSpec((None, CHUNK, LANES), lambda h, c: (h, nc - 1 - c, 0)),
                  pl.BlockSpec((None, None, 8, CHUNK), lambda h, c: (h, nc - 1 - c, 0, 0)),
                  pl.BlockSpec((1, dv), lambda h, c: (0, h)),
                  pl.BlockSpec((CHUNK, dv), rc),
                  pl.BlockSpec((None, CHUNK, LANES), lambda h, c: (h, nc - 1 - c, 0)),
                  pl.BlockSpec((None, None, dk, dv), lambda h, c: (h, nc - 1 - c, 0, 0)),
                  pl.BlockSpec((None, None, 8, dk), lambda h, c: (h, nc - 1 - c, 0, 0)),
                  pl.BlockSpec((CHUNK, dv), rc)],
        out_specs=[pl.BlockSpec((CHUNK, hw), rc),
                   pl.BlockSpec((None, CHUNK, LANES), lambda h, c: (h, nc - 1 - c, 0)),
                   pl.BlockSpec((None, 8, dv), lambda h, c: (h, 0, 0))],
        out_shape=[jax.ShapeDtypeStruct((t, nh * hw), BF16), jax.ShapeDtypeStruct((nh, t, LANES), F32),
                   jax.ShapeDtypeStruct((nh, 8, dv), F32)],
        scratch_shapes=[pltpu.VMEM((dk, dv), F32), pltpu.VMEM((1, dk), F32)],
        compiler_params=_params(),
    )(p1, colf, rowf, head_w, hh, stat, csave, nsave, dhm)


def _position():
    return lax.axis_index("x"), lax.axis_index("y"), lax.axis_index("c")


def _all_gather(shards, name):
    n = len(shards)

    def body(*refs):
        ins, outs = refs[:n], refs[n:2 * n]
        send_sems, recv_sems, local_sems = refs[2 * n:]
        x, y, c = _position()
        me, sibling = (x, y, c), (x, y, 1 - c)
        chips = [(1 - x, y), (x, 1 - y), (1 - x, 1 - y)]

        def copy(a, k, block, to, src=None):
            px, py, pc = block
            dst = outs[a].at[4 * px + 2 * py + pc]
            return pltpu.make_async_remote_copy(
                src_ref=dst if src is None else src, dst_ref=dst,
                send_sem=send_sems.at[a, k], recv_sem=recv_sems.at[a, k],
                device_id=to, device_id_type=MESH)

        mine = [pltpu.make_async_copy(ins[a], outs[a].at[4 * x + 2 * y + c], local_sems.at[a]) for a in range(n)]
        for cp in mine:
            cp.start()
        first = []
        for a in range(n):
            first.append(copy(a, 0, me, sibling, src=ins[a]))
            first += [copy(a, 1 + j, me, (*chip, c), src=ins[a]) for j, chip in enumerate(chips)]
        for cp in first:
            cp.start()
        passed = []
        for j, chip in enumerate(chips):
            for a in range(n):
                copy(a, 1 + j, (*chip, c), me).wait_recv()
                cp = copy(a, 4 + j, (*chip, c), sibling)
                cp.start()
                passed.append(cp)
        for a in range(n):
            copy(a, 0, sibling, me).wait_recv()
            for j, chip in enumerate(chips):
                copy(a, 4 + j, (*chip, 1 - c), me).wait_recv()
        for cp in first + passed:
            cp.wait_send()
        for cp in mine:
            cp.wait()

    any_spec = pl.BlockSpec(memory_space=pl.ANY)
    return pl.pallas_call(
        body, name=name,
        in_specs=[any_spec] * n, out_specs=[any_spec] * n,
        out_shape=[jax.ShapeDtypeStruct((N_DEV,) + s.shape, s.dtype) for s in shards],
        scratch_shapes=[pltpu.SemaphoreType.DMA((n, 7)), pltpu.SemaphoreType.DMA((n, 7)),
                        pltpu.SemaphoreType.DMA((n,))],
        compiler_params=pltpu.CompilerParams(has_side_effects=True),
    )(*shards)


def _scatter_to_owners(fulls, name):
    n = len(fulls)

    def body(*refs):
        ins, outs = refs[:n], refs[n:2 * n]
        send_sems, recv_sems, local_sems = refs[2 * n:]
        x, y, c = _position()
        my_slot = 4 * x + 2 * y + c
        mine = [pltpu.make_async_copy(ins[a].at[my_slot], outs[a].at[my_slot], local_sems.at[a]) for a in range(n)]
        for cp in mine:
            cp.start()
        copies = []
        for kk in (1, 2, 4, 6, 3, 5, 7):
            kx, ky, kc = (kk >> 2) & 1, (kk >> 1) & 1, kk & 1
            px = 1 - x if kx else x
            py = 1 - y if ky else y
            pc = 1 - c if kc else c
            peer_slot = 4 * px + 2 * py + pc
            for a in range(n):
                cp = pltpu.make_async_remote_copy(
                    src_ref=ins[a].at[peer_slot], dst_ref=outs[a].at[my_slot],
                    send_sem=send_sems.at[a, kk - 1], recv_sem=recv_sems.at[a, kk - 1],
                    device_id=(px, py, pc), device_id_type=MESH)
                cp.start()
                copies.append(pltpu.make_async_remote_copy(
                    src_ref=ins[a].at[peer_slot], dst_ref=outs[a].at[peer_slot],
                    send_sem=send_sems.at[a, kk - 1], recv_sem=recv_sems.at[a, kk - 1],
                    device_id=(px, py, pc), device_id_type=MESH))
        for cp in copies:
            cp.wait_send()
            cp.wait_recv()
        for cp in mine:
            cp.wait()

    any_spec = pl.BlockSpec(memory_space=pl.ANY)
    return pl.pallas_call(
        body, name=name,
        in_specs=[any_spec] * n, out_specs=[any_spec] * n,
        out_shape=[jax.ShapeDtypeStruct(f.shape, f.dtype) for f in fulls],
        scratch_shapes=[pltpu.SemaphoreType.DMA((n, 7)), pltpu.SemaphoreType.DMA((n, 7)),
                        pltpu.SemaphoreType.DMA((n,))],
        compiler_params=pltpu.CompilerParams(has_side_effects=True),
    )(*fulls)


def _adamw_math(w, g, m, v):
    m = ADAM_B1 * m + (1.0 - ADAM_B1) * g
    v = ADAM_B2 * v + (1.0 - ADAM_B2) * (g * g)
    m_hat = m / (1.0 - ADAM_B1 ** ADAM_STEP)
    v_hat = v / (1.0 - ADAM_B2 ** ADAM_STEP)
    delta = -ADAM_LR * (m_hat / (jnp.sqrt(v_hat) + ADAM_EPS) + ADAM_WD * w)
    return delta, m, v


def _adamw_sharded(parts, w, m, v, name):
    _, r, c = parts.shape
    tr = _pick(r, 128, 8)

    def body(p_ref, w_ref, m_ref, v_ref, g_ref, d_ref, nm_ref, nv_ref):
        g = p_ref[0].astype(F32)
        for s in range(1, N_DEV):
            g = g + p_ref[s].astype(F32)
        delta, m_new, v_new = _adamw_math(w_ref[...], g, m_ref[...], v_ref[...])
        g_ref[...] = g
        d_ref[...] = delta
        nm_ref[...] = m_new
        nv_ref[...] = v_new

    blk = pl.BlockSpec((tr, c), lambda i: (i, 0))
    return pl.pallas_call(
        body, name=name, grid=(r // tr,),
        in_specs=[pl.BlockSpec((N_DEV, tr, c), lambda i: (0, i, 0)), blk, blk, blk],
        out_specs=[blk] * 4,
        out_shape=[jax.ShapeDtypeStruct((r, c), F32)] * 4,
        compiler_params=_params(),
    )(parts, w, m, v)


def _sum_devices(parts, name):
    _, r, c = parts.shape

    def body(p_ref, o_ref):
        g = p_ref[0]
        for s in range(1, N_DEV):
            g = g + p_ref[s]
        o_ref[...] = g

    return pl.pallas_call(
        body, name=name, out_shape=jax.ShapeDtypeStruct((r, c), F32), compiler_params=_params(),
    )(parts)


def _adamw_small(gs, ws, ms, vs, name):
    n = len(gs)

    def body(*refs):
        g_refs, w_refs, m_refs, v_refs = refs[:n], refs[n:2 * n], refs[2 * n:3 * n], refs[3 * n:4 * n]
        d_refs, nm_refs, nv_refs = refs[4 * n:5 * n], refs[5 * n:6 * n], refs[6 * n:7 * n]
        for a in range(n):
            delta, m_new, v_new = _adamw_math(w_refs[a][...], g_refs[a][...], m_refs[a][...], v_refs[a][...])
            d_refs[a][...] = delta
            nm_refs[a][...] = m_new
            nv_refs[a][...] = v_new

    shapes = [jax.ShapeDtypeStruct(w.shape, F32) for w in ws]
    outs = pl.pallas_call(
        body, name=name, out_shape=shapes * 3, compiler_params=_params(),
    )(*gs, *ws, *ms, *vs)
    return outs[:n], outs[n:2 * n], outs[2 * n:]


def _pad_rows(a, rows):
    return jnp.pad(a, ((0, rows - a.shape[0]), (0, 0)))


def kernel(x, meta_tokens, norm_w, conv_in_w, conv_w, conv_out_w, mlstm_in_w, mlstm_gate_b, mlstm_head_norm_w, mlstm_out_w, final_norm_w, loss_target, m_meta_tokens, m_norm_w, m_conv_in_w, m_conv_w, m_conv_out_w, m_mlstm_in_w, m_mlstm_gate_b, m_mlstm_head_norm_w, m_mlstm_out_w, m_final_norm_w, v_meta_tokens, v_norm_w, v_conv_in_w, v_conv_w, v_conv_out_w, v_mlstm_in_w, v_mlstm_gate_b, v_mlstm_head_norm_w, v_mlstm_out_w, v_final_norm_w):
    seq, d = x.shape[1], x.shape[2]
    t = seq + LEAD
    e = 2 * d
    ec = e // 2
    nh = N_HEADS
    dv = e // nh
    dk = dv // 2
    qk = nh * dk
    hw = 2 * dk + 3 * dv
    n_in = 2 * qk + 3 * e + 2 * nh
    n_in_s = n_in // N_DEV
    me = 4 * lax.axis_index("x") + 2 * lax.axis_index("y") + lax.axis_index("c")
    tm = _pick(t, 832, 16)

    small = jnp.concatenate([
        meta_tokens,
        _pad_rows(conv_w[0].reshape(3 * (e // N_DEV) // LANES, LANES), 8),
        _pad_rows(mlstm_head_norm_w[0].reshape((e // N_DEV) // LANES, LANES), 8),
    ], axis=0) if d // N_DEV == LANES else None
    assert small is not None, "the packed small-weight block assumes d_model / 8 == 128"
    w_ci, w_co, w_mi, w_mo, small_g = _all_gather(
        [conv_in_w[0].astype(BF16), conv_out_w[0].astype(BF16), mlstm_in_w[0].astype(BF16),
         mlstm_out_w[0].astype(BF16), small], "gather_weights")
    w_co = w_co.reshape(e, d)
    w_mo = w_mo.reshape(e, d)
    meta_full = jnp.transpose(small_g[:, 0:N_META, :], (1, 0, 2)).reshape(N_META, d)
    cw_rows = 3 * (e // N_DEV) // LANES
    conv_w_full = jnp.transpose(
        small_g[:, N_META:N_META + cw_rows, :].reshape(N_DEV, 3, e // N_DEV), (1, 0, 2)).reshape(3, e)
    hn_rows = (e // N_DEV) // LANES
    head_w_full = small_g[:, N_META + 8:N_META + 8 + hn_rows, :].reshape(1, e)

    w_nat = jnp.transpose(w_mi, (1, 0, 2)).reshape(d, n_in)
    wq = (w_nat[:, 0:qk] * (dk ** -0.5)).astype(BF16).reshape(d, nh, dk)
    wk = w_nat[:, qk:2 * qk].reshape(d, nh, dk)
    wv = w_nat[:, 2 * qk:2 * qk + e].reshape(d, nh, dv)
    wo = w_nat[:, 2 * qk + e:2 * qk + 2 * e].reshape(d, nh, dv)
    wz = w_nat[:, 2 * qk + 2 * e:2 * qk + 3 * e].reshape(d, nh, dv)
    w_p1 = jnp.concatenate([wq, wk, wv, wo, wz], axis=2).reshape(d, nh * hw)
    w_gate = jnp.pad(w_nat[:, 2 * qk + 3 * e:], ((0, 0), (0, LANES - 2 * nh)))
    w_gate_t = jnp.pad(w_nat[:, 2 * qk + 3 * e:].T, ((0, 16 - 2 * nh), (0, 0)))
    bias_row = jnp.pad(mlstm_gate_b, ((0, 0), (0, LANES - 2 * nh)))
    bias_col = jnp.pad(mlstm_gate_b.T, ((0, 16 - 2 * nh), (0, 0)))

    h0 = jnp.concatenate([jnp.zeros((PAD_ROWS, d), F32), meta_full, x[0]], axis=0)
    tgt = jnp.concatenate([jnp.zeros((LEAD, d), F32), loss_target[0]], axis=0)

    ci_map = lambda blk: 2 * (blk % 4) + blk // 4
    u0 = _rms_fwd(h0, norm_w[0:1], "rms0")
    p0 = _matmul(u0, w_ci, form="nn", m=t, n=8 * ec, kdim=d, tm=tm, tn=ec, tk=d, out_dtype=BF16, name="conv_in",
                 b_spec=pl.BlockSpec((None, d, ec), lambda i, j, k: (ci_map(j), 0, 0)))
    g0, y0 = _conv_gate_fwd(p0, conv_w_full, "conv_gate")
    h1 = _matmul(g0, w_co, form="nn", m=t, n=d, kdim=e, tm=tm, tn=d, tk=e, out_dtype=F32, name="conv_out",
                 residual=h0)

    u1 = _rms_fwd(h1, norm_w[1:2], "rms1")
    p1 = _matmul(u1, w_p1, form="nn", m=t, n=nh * hw, kdim=d, tm=tm, tn=_pick(nh * hw, 1024, LANES), tk=d,
                 out_dtype=BF16, name="mlstm_in")
    graw = _matmul(u1, w_gate, form="nn", m=t, n=LANES, kdim=d, tm=tm, tn=LANES, tk=d, out_dtype=F32,
                   name="gates_col")
    graw_t = _matmul(w_gate_t, u1, form="nt", m=16, n=t, kdim=d, tm=16, tn=_pick(t, 1664, LANES), tk=d,
                     out_dtype=F32, name="gates_row")
    colf, rowf = _gates_fwd(graw, graw_t, bias_row, bias_col, "gates_fwd")
    hm, hh, stat, csave, nsave = _mlstm_fwd(p1, colf, rowf, head_w_full, dk, dv, "mlstm_fwd")
    h2 = _matmul(hm, w_mo, form="nn", m=t, n=d, kdim=e, tm=tm, tn=d, tk=e, out_dtype=F32, name="mlstm_out",
                 residual=h1)

    dh2, dwf, loss_part = _final_loss(h2, final_norm_w.reshape(1, d), tgt, "final_loss")

    dhm = _matmul(dh2, w_mo, form="nt", m=t, n=e, kdim=d, tm=tm, tn=_pick(e, 1024, LANES), tk=d, out_dtype=BF16,
                  name="mlstm_out_dx")
    g_mo = _matmul(hm, dh2, form="tn", m=e, n=d, kdim=t, tm=_pick(e, 1024, LANES), tn=d, tk=tm, out_dtype=BF16,
                   name="mlstm_out_dw")
    dp1, gstat, dhead = _mlstm_bwd(p1, colf, rowf, head_w_full, hh, stat, csave, nsave, dhm, dk, dv, "mlstm_bwd")
    dgates, dbias = _gates_bwd(gstat, colf, "gates_bwd")
    du1 = _matmul(dp1, w_p1, form="nt", m=t, n=d, kdim=nh * hw, tm=tm, tn=d, tk=_pick(nh * hw, 1024, LANES),
                  out_dtype=F32, name="mlstm_in_dx")
    du1 = _matmul(dgates, w_gate, form="nt", m=t, n=d, kdim=LANES, tm=tm, tn=d, tk=LANES, out_dtype=F32,
                  name="gates_dx", residual=du1)
    g_p1 = _matmul(u1, dp1, form="tn", m=d, n=nh * hw, kdim=t, tm=d, tn=_pick(nh * hw, 1024, LANES), tk=tm,
                   out_dtype=BF16, name="mlstm_in_dw")
    g_gate = _matmul(u1, dgates, form="tn", m=d, n=LANES, kdim=t, tm=d, tn=LANES, tk=tm, out_dtype=F32,
                     name="gates_dw")
    dh1, dnw1 = _rms_bwd(h1, norm_w[1:2], du1, dh2, "rms1_bwd")

    dg0 = _matmul(dh1, w_co, form="nt", m=t, n=e, kdim=d, tm=tm, tn=_pick(e, 1024, LANES), tk=d, out_dtype=BF16,
                  name="conv_out_dx")
    g_co = _matmul(g0, dh1, form="tn", m=e, n=d, kdim=t, tm=_pick(e, 1024, LANES), tn=d, tk=tm, out_dtype=BF16,
                   name="conv_out_dw")
    dp0, dconv = _conv_gate_bwd(p0, y0, dg0, conv_w_full, "conv_gate_bwd")
    du0 = _matmul(dp0, w_ci, form="nt", m=t, n=d, kdim=8 * ec, tm=tm, tn=d, tk=ec, out_dtype=F32, name="conv_in_dx",
                  b_spec=pl.BlockSpec((None, d, ec), lambda i, j, k: (ci_map(k), 0, 0)))
    g_ci = _matmul(u0, dp0, form="tn", m=d, n=8 * ec, kdim=t, tm=d, tn=ec, tk=tm, out_dtype=BF16, name="conv_in_dw",
                   out_shape=(N_DEV, d, ec),
                   out_spec=pl.BlockSpec((None, d, ec), lambda i, j, k: (ci_map(j), 0, 0)))
    dh0, dnw0 = _rms_bwd(h0, norm_w[0:1], du0, dh1, "rms0_bwd")
    grad_x = dh0[LEAD:][None]

    gp = g_p1.reshape(d, nh, hw)
    gq = (gp[:, :, 0:dk] * (dk ** -0.5)).astype(BF16).reshape(d, qk)
    gk = gp[:, :, dk:2 * dk].reshape(d, qk)
    gv = gp[:, :, 2 * dk:2 * dk + dv].reshape(d, e)
    go = gp[:, :, 2 * dk + dv:2 * dk + 2 * dv].reshape(d, e)
    gz = gp[:, :, 2 * dk + 2 * dv:].reshape(d, e)
    g_mi = jnp.concatenate([gq, gk, gv, go, gz, g_gate[:, 0:2 * nh].astype(BF16)], axis=1)
    g_mi = jnp.transpose(g_mi.reshape(d, N_DEV, n_in_s), (1, 0, 2))

    r_ci, r_co, r_mi, r_mo = _scatter_to_owners(
        [g_ci, g_co.reshape(N_DEV, e // N_DEV, d), g_mi, g_mo.reshape(N_DEV, e // N_DEV, d)], "scatter_grads")

    row8 = lax.broadcasted_iota(jnp.int32, (8, 1), 0)
    loss_wide = jnp.pad(loss_part, ((0, 0), (0, d - LANES)))
    payload = jnp.concatenate([
        jnp.where(row8 == 0, dnw0, jnp.where(row8 == 1, dnw1, 0.0)),
        jnp.where(row8 == 0, dwf, jnp.where(row8 == 1, loss_wide, 0.0)),
        jnp.where(row8 == 0, jnp.pad(dbias, ((0, 0), (0, d - LANES))), 0.0),
        dh0[PAD_ROWS:LEAD],
        _pad_rows(dconv[0:3].reshape(3 * e // d, d), 8),
        _pad_rows(dhead[:, 0, :].reshape(e // d, d), 8),
    ], axis=0)
    (payload_g,) = _all_gather([payload], "gather_small_grads")
    tot = _sum_devices(payload_g, "sum_small_grads")

    loss = tot[9, 0]
    g_norm = tot[0:2]
    g_final = tot[8]
    g_gate_b = tot[16:17, 0:2 * nh]
    g_meta = lax.dynamic_slice(tot[24:24 + N_META], (0, me * (d // N_DEV)), (N_META, d // N_DEV))
    g_conv_w = lax.dynamic_slice(tot[40:40 + 3 * e // d].reshape(3, e), (0, me * (e // N_DEV)), (3, e // N_DEV))
    g_head = lax.dynamic_slice(tot[48:48 + e // d].reshape(1, e), (0, me * (e // N_DEV)), (1, e // N_DEV))

    g1, d1, nm1, nv1 = _adamw_sharded(r_ci, conv_in_w[0], m_conv_in_w[0], v_conv_in_w[0], "adamw_conv_in")
    g2, d2, nm2, nv2 = _adamw_sharded(r_co, conv_out_w[0], m_conv_out_w[0], v_conv_out_w[0], "adamw_conv_out")
    g3, d3, nm3, nv3 = _adamw_sharded(r_mi, mlstm_in_w[0], m_mlstm_in_w[0], v_mlstm_in_w[0], "adamw_mlstm_in")
    g4, d4, nm4, nv4 = _adamw_sharded(r_mo, mlstm_out_w[0], m_mlstm_out_w[0], v_mlstm_out_w[0], "adamw_mlstm_out")

    small_g = [g_meta, g_norm, g_conv_w, g_gate_b, g_head, g_final.reshape(1, d)]
    small_w = [meta_tokens, norm_w, conv_w[0], mlstm_gate_b, mlstm_head_norm_w, final_norm_w.reshape(1, d)]
    small_m = [m_meta_tokens, m_norm_w, m_conv_w[0], m_mlstm_gate_b, m_mlstm_head_norm_w, m_final_norm_w.reshape(1, d)]
    small_v = [v_meta_tokens, v_norm_w, v_conv_w[0], v_mlstm_gate_b, v_mlstm_head_norm_w, v_final_norm_w.reshape(1, d)]
    sd, snm, snv = _adamw_small(small_g, small_w, small_m, small_v, "adamw_small")

    def order(meta, norm, cin, cw, cout, min_, gb, hn, mout, fin):
        return (meta, norm, cin[None], cw[None], cout[None], min_[None], gb, hn, mout[None], fin.reshape(d))

    grads = order(g_meta, g_norm, g1, g_conv_w, g2, g3, g_gate_b, g_head, g4, g_final)
    deltas = order(sd[0], sd[1], d1, sd[2], d2, d3, sd[3], sd[4], d4, sd[5])
    new_m = order(snm[0], snm[1], nm1, snm[2], nm2, nm3, snm[3], snm[4], nm4, snm[5])
    new_v = order(snv[0], snv[1], nv1, snv[2], nv2, nv3, snv[3], snv[4], nv4, snv[5])
    return (loss, grad_x, *grads, *deltas, *new_m, *new_v)
```

```python
import functools

import jax
import jax.numpy as jnp
from jax import lax
from jax.experimental import pallas as pl
from jax.experimental.pallas import tpu as pltpu

F32 = jnp.float32
BF16 = jnp.bfloat16
MESH = pl.DeviceIdType.MESH

N_DEV = 8
N_META = 16
N_HEADS = 4
CHUNK = 64
LEAD = 128
PAD_ROWS = LEAD - N_META
RMS_EPS = 1e-6
NEG = -1e30
LANES = 128
VMEM_LIMIT = 48 * 1024 * 1024

ADAM_LR = 0.001
ADAM_B1 = 0.9
ADAM_B2 = 0.999
ADAM_EPS = 1e-08
ADAM_WD = 0.01
ADAM_STEP = 10

HIGHEST = lax.Precision.HIGHEST


def _pick(n, target, mult):
    best = None
    for d in range(mult, min(n, target) + 1, mult):
        if n % d == 0:
            best = d
    return n if best is None else best


def _params(**kw):
    return pltpu.CompilerParams(vmem_limit_bytes=VMEM_LIMIT, **kw)


def _sigmoid(x):
    return 1.0 / (1.0 + jnp.exp(-x))


def _matmul(a, b, *, form, m, n, kdim, tm, tn, tk, out_dtype, name,
            a_spec=None, b_spec=None, out_spec=None, out_shape=None, residual=None, exchange=None):
    ni, nj, nk = m // tm, n // tn, kdim // tk
    assert ni * tm == m and nj * tn == n and nk * tk == kdim, (name, m, n, kdim, tm, tn, tk)
    if form == "nn":
        dn = (((1,), (0,)), ((), ()))
        a_def = pl.BlockSpec((tm, tk), lambda i, j, k: (i, k))
        b_def = pl.BlockSpec((tk, tn), lambda i, j, k: (k, j))
    elif form == "nt":
        dn = (((1,), (1,)), ((), ()))
        a_def = pl.BlockSpec((tm, tk), lambda i, j, k: (i, k))
        b_def = pl.BlockSpec((tn, tk), lambda i, j, k: (j, k))
    else:
        dn = (((0,), (0,)), ((), ()))
        a_def = pl.BlockSpec((tk, tm), lambda i, j, k: (k, i))
        b_def = pl.BlockSpec((tk, tn), lambda i, j, k: (k, j))
    a_spec = a_def if a_spec is None else a_spec
    b_spec = b_def if b_spec is None else b_spec
    o_spec = pl.BlockSpec((tm, tn), lambda i, j, k: (i, j)) if out_spec is None else out_spec
    has_res = residual is not None
    ex = exchange
    n_ex_in = 0 if ex is None else len(ex.ins)
    n_ex_out = 0 if ex is None else len(ex.out_shapes)
    n_in = 2 + has_res + n_ex_in
    steps = ni * nj * nk

    def body(*refs):
        a_ref, b_ref = refs[:2]
        r_ref = refs[2] if has_res else None
        o_ref = refs[n_in]
        if ex is not None:
            ex_refs = (refs[n_in - n_ex_in:n_in], refs[n_in + 1:n_in + 1 + n_ex_out], refs[len(refs) - 3:])
            step = (pl.program_id(0) * nj + pl.program_id(1)) * nk + pl.program_id(2)

            @pl.when(step == 0)
            def _():
                ex.start(*ex_refs)

            if ex.mid is not None:
                @pl.when(step == (3 * steps) // 4)
                def _():
                    ex.mid(*ex_refs)

        part = lax.dot_general(a_ref[...].astype(BF16), b_ref[...].astype(BF16), dn,
                               preferred_element_type=F32)

        def finish(acc):
            if has_res:
                acc = acc + r_ref[...].astype(F32)
            o_ref[...] = acc.astype(o_ref.dtype)

        if nk == 1:
            finish(part)
        else:
            acc_ref = refs[n_in + 1 + n_ex_out]
            k = pl.program_id(2)

            @pl.when(k == 0)
            def _():
                acc_ref[...] = part

            @pl.when(k > 0)
            def _():
                acc_ref[...] += part

            @pl.when(k == nk - 1)
            def _():
                finish(acc_ref[...])

        if ex is not None:
            @pl.when(step == steps - 1)
            def _():
                ex.finish(*ex_refs)

    in_specs = [a_spec, b_spec]
    args = [a, b]
    if has_res:
        in_specs.append(pl.BlockSpec((tm, tn), lambda i, j, k: (i, j)))
        args.append(residual)
    out_specs = o_spec
    out_shapes = jax.ShapeDtypeStruct((m, n) if out_shape is None else out_shape, out_dtype)
    scratch = [] if nk == 1 else [pltpu.VMEM((tm, tn), F32)]
    if ex is not None:
        any_spec = pl.BlockSpec(memory_space=pl.ANY)
        in_specs += [any_spec] * n_ex_in
        args += ex.ins
        out_specs = [o_spec] + [any_spec] * n_ex_out
        out_shapes = [out_shapes] + ex.out_shapes
        scratch = scratch + ex.scratch
    return pl.pallas_call(
        body, name=name, grid=(ni, nj, nk),
        in_specs=in_specs, out_specs=out_specs, out_shape=out_shapes,
        scratch_shapes=scratch, compiler_params=_params(),
    )(*args)


def _rms_fwd(h, w, name):
    t, d = h.shape
    tm = _pick(t, 416, 16)

    def body(h_ref, w_ref, u_ref):
        x = h_ref[...]
        r = lax.rsqrt(jnp.mean(x * x, axis=-1, keepdims=True) + RMS_EPS)
        u_ref[...] = ((x * r) * w_ref[...]).astype(BF16)

    return pl.pallas_call(
        body, name=name, grid=(t // tm,),
        in_specs=[pl.BlockSpec((tm, d), lambda i: (i, 0)), pl.BlockSpec((1, d), lambda i: (0, 0))],
        out_specs=pl.BlockSpec((tm, d), lambda i: (i, 0)),
        out_shape=jax.ShapeDtypeStruct((t, d), BF16),
        compiler_params=_params(),
    )(h, w)


def _rms_bwd(h, w, du, dres, name):
    t, d = h.shape
    tm = _pick(t, 416, 16)

    def body(h_ref, w_ref, du_ref, dres_ref, dh_ref, dw_ref):
        i = pl.program_id(0)
        x = h_ref[...]
        g = du_ref[...].astype(F32)
        r = lax.rsqrt(jnp.mean(x * x, axis=-1, keepdims=True) + RMS_EPS)
        gw = g * w_ref[...]
        dot = jnp.mean(gw * x, axis=-1, keepdims=True)
        dh_ref[...] = dres_ref[...] + (r * gw - x * ((r * r * r) * dot))
        part = jnp.sum(g * (x * r), axis=0, keepdims=True)

        @pl.when(i == 0)
        def _():
            dw_ref[...] = jnp.zeros_like(dw_ref)

        dw_ref[...] += jnp.broadcast_to(part, dw_ref.shape)

    row = pl.BlockSpec((tm, d), lambda i: (i, 0))
    return pl.pallas_call(
        body, name=name, grid=(t // tm,),
        in_specs=[row, pl.BlockSpec((1, d), lambda i: (0, 0)), row, row],
        out_specs=[row, pl.BlockSpec((8, d), lambda i: (0, 0))],
        out_shape=[jax.ShapeDtypeStruct((t, d), F32), jax.ShapeDtypeStruct((8, d), F32)],
        compiler_params=_params(),
    )(h, w, du, dres)


def _rms_bwd_first(h, w, du, dres, name):
    t, d = h.shape
    tm = LEAD

    def body(h_ref, w_ref, du_ref, dres_ref, gx_ref, dmeta_ref, dw_ref):
        i = pl.program_id(0)
        x = h_ref[...]
        g = du_ref[...].astype(F32)
        r = lax.rsqrt(jnp.mean(x * x, axis=-1, keepdims=True) + RMS_EPS)
        gw = g * w_ref[...]
        dot = jnp.mean(gw * x, axis=-1, keepdims=True)
        dh = dres_ref[...] + (r * gw - x * ((r * r * r) * dot))
        part = jnp.sum(g * (x * r), axis=0, keepdims=True)

        @pl.when(i == 0)
        def _():
            dw_ref[...] = jnp.zeros_like(dw_ref)
            dmeta_ref[...] = dh[PAD_ROWS:LEAD, :]

        @pl.when(i > 0)
        def _():
            gx_ref[...] = dh

        dw_ref[...] += jnp.broadcast_to(part, dw_ref.shape)

    row = pl.BlockSpec((tm, d), lambda i: (i, 0))
    return pl.pallas_call(
        body, name=name, grid=(t // tm,),
        in_specs=[row, pl.BlockSpec((1, d), lambda i: (0, 0)), row, row],
        out_specs=[pl.BlockSpec((tm, d), lambda i: (jnp.maximum(i - 1, 0), 0)),
                   pl.BlockSpec((N_META, d), lambda i: (0, 0)), pl.BlockSpec((8, d), lambda i: (0, 0))],
        out_shape=[jax.ShapeDtypeStruct((t - LEAD, d), F32), jax.ShapeDtypeStruct((N_META, d), F32),
                   jax.ShapeDtypeStruct((8, d), F32)],
        compiler_params=_params(),
    )(h, w, du, dres)


def _final_loss(h, w, tgt, name):
    t, d = h.shape
    tm = LEAD

    def body(h_ref, w_ref, t_ref, dh_ref, dw_ref, loss_ref):
        i = pl.program_id(0)
        x = h_ref[...]
        r = lax.rsqrt(jnp.mean(x * x, axis=-1, keepdims=True) + RMS_EPS)
        yn = x * r
        y = yn * w_ref[...]
        rows = i * tm + lax.broadcasted_iota(jnp.int32, (tm, 1), 0)
        diff = jnp.where(rows >= LEAD, y - t_ref[...], 0.0)
        tile_loss = 0.5 * jnp.sum(jnp.mean(diff * diff, axis=-1, keepdims=True), axis=0, keepdims=True)
        dy = diff / d
        gw = dy * w_ref[...]
        dot = jnp.mean(gw * x, axis=-1, keepdims=True)
        dh_ref[...] = r * gw - x * ((r * r * r) * dot)
        part = jnp.sum(dy * yn, axis=0, keepdims=True)

        @pl.when(i == 0)
        def _():
            dw_ref[...] = jnp.zeros_like(dw_ref)
            loss_ref[...] = jnp.zeros_like(loss_ref)

        dw_ref[...] += jnp.broadcast_to(part, dw_ref.shape)
        loss_ref[...] += jnp.broadcast_to(tile_loss, loss_ref.shape)

    row = pl.BlockSpec((tm, d), lambda i: (i, 0))
    return pl.pallas_call(
        body, name=name, grid=(t // tm,),
        in_specs=[row, pl.BlockSpec((1, d), lambda i: (0, 0)),
                  pl.BlockSpec((tm, d), lambda i: (jnp.maximum(i - 1, 0), 0))],
        out_specs=[row, pl.BlockSpec((8, d), lambda i: (0, 0)), pl.BlockSpec((8, LANES), lambda i: (0, 0))],
        out_shape=[jax.ShapeDtypeStruct((t, d), F32), jax.ShapeDtypeStruct((8, d), F32),
                   jax.ShapeDtypeStruct((8, LANES), F32)],
        compiler_params=_params(),
    )(h, w, tgt)


def _conv_gate_fwd(p0, conv_w, name):
    t = p0.shape[0]
    ec = p0.shape[1] // 8
    tm = _pick(t, 416, 16)
    nt = t // tm

    def body(p_ref, w_ref, g_ref, y_ref, ext_ref):
        i = pl.program_id(1)
        bg = p_ref[:, 0:ec].astype(F32)
        cg = p_ref[:, ec:2 * ec].astype(F32)
        xin = p_ref[:, 2 * ec:3 * ec].astype(F32)
        z = p_ref[:, 3 * ec:4 * ec].astype(F32)
        a = cg * xin

        @pl.when(i == 0)
        def _():
            ext_ref[0:8, :] = jnp.zeros((8, ec), F32)

        ext_ref[8:8 + tm, :] = a
        y = (w_ref[0:1, :] * ext_ref[6:6 + tm, :] + w_ref[1:2, :] * ext_ref[7:7 + tm, :]
             + w_ref[2:3, :] * a)
        ext_ref[0:8, :] = a[tm - 8:tm, :]
        y_ref[...] = y.astype(BF16)
        g_ref[...] = ((z * _sigmoid(z)) * bg * y).astype(BF16)

    out = pl.BlockSpec((tm, ec), lambda j, i: (i, j))
    return pl.pallas_call(
        body, name=name, grid=(2, nt),
        in_specs=[pl.BlockSpec((tm, 4 * ec), lambda j, i: (i, j)), pl.BlockSpec((3, ec), lambda j, i: (0, j))],
        out_specs=[out, out],
        out_shape=[jax.ShapeDtypeStruct((t, 2 * ec), BF16)] * 2,
        scratch_shapes=[pltpu.VMEM((tm + 8, ec), F32)],
        compiler_params=_params(),
    )(p0, conv_w)


def _conv_gate_bwd(p0, y, dg, conv_w, name):
    t = p0.shape[0]
    ec = p0.shape[1] // 8
    tm = _pick(t, 208, 16)
    nt = t // tm

    def body(p_ref, y_ref, dg_ref, w_ref, dp_ref, dw_ref, ext_ref):
        i = pl.program_id(1)
        bg = p_ref[:, 0:ec].astype(F32)
        cg = p_ref[:, ec:2 * ec].astype(F32)
        xin = p_ref[:, 2 * ec:3 * ec].astype(F32)
        z = p_ref[:, 3 * ec:4 * ec].astype(F32)
        yv = y_ref[...].astype(F32)
        dgv = dg_ref[...].astype(F32)
        sig = _sigmoid(z)
        sz = z * sig
        dp_ref[:, 3 * ec:4 * ec] = (dgv * bg * yv * (sig * (1.0 + z * (1.0 - sig)))).astype(BF16)
        dp_ref[:, 0:ec] = (dgv * sz * yv).astype(BF16)
        dy = dgv * sz * bg

        @pl.when(i == 0)
        def _():
            ext_ref[tm:tm + 8, :] = jnp.zeros((8, ec), F32)
            dw_ref[...] = jnp.zeros_like(dw_ref)

        ext_ref[0:tm, :] = dy
        dy1 = ext_ref[1:tm + 1, :]
        dy2 = ext_ref[2:tm + 2, :]
        ext_ref[tm:tm + 8, :] = dy[0:8, :]
        da = w_ref[0:1, :] * dy2 + w_ref[1:2, :] * dy1 + w_ref[2:3, :] * dy
        dp_ref[:, ec:2 * ec] = (da * xin).astype(BF16)
        dp_ref[:, 2 * ec:3 * ec] = (da * cg).astype(BF16)
        a = cg * xin
        dw_ref[0:1, :] += jnp.sum(a * dy2, axis=0, keepdims=True)
        dw_ref[1:2, :] += jnp.sum(a * dy1, axis=0, keepdims=True)
        dw_ref[2:3, :] += jnp.sum(a * dy, axis=0, keepdims=True)

    rev = lambda j, i: (nt - 1 - i, j)
    return pl.pallas_call(
        body, name=name, grid=(2, nt),
        in_specs=[pl.BlockSpec((tm, 4 * ec), rev), pl.BlockSpec((tm, ec), rev), pl.BlockSpec((tm, ec), rev),
                  pl.BlockSpec((3, ec), lambda j, i: (0, j))],
        out_specs=[pl.BlockSpec((tm, 4 * ec), rev), pl.BlockSpec((8, ec), lambda j, i: (0, j))],
        out_shape=[jax.ShapeDtypeStruct((t, 8 * ec), BF16), jax.ShapeDtypeStruct((8, 2 * ec), F32)],
        scratch_shapes=[pltpu.VMEM((tm + 8, ec), F32)],
        compiler_params=_params(),
    )(p0, y, dg, conv_w)


def _log_sigmoid(x):
    return jnp.minimum(x, 0.0) - jnp.log(1.0 + jnp.exp(-jnp.abs(x)))


def _gates_fwd(graw, graw_t, bias_row, bias_col, name):
    t = graw.shape[0]
    tm = _pick(t, 640, 128)
    cpt = tm // CHUNK
    nh = N_HEADS

    def body(g_ref, gt_ref, br_ref, bc_ref, colf_ref, rowf_ref):
        i = pl.program_id(0)
        gc = g_ref[...] + br_ref[...]
        rows = i * tm + lax.broadcasted_iota(jnp.int32, (tm, 1), 0)
        live = rows >= PAD_ROWS
        lf = jnp.where(live, _log_sigmoid(gc), 0.0)
        li = jnp.where(live, gc, NEG)
        gt = gt_ref[...] + bc_ref[...]
        cols = i * tm + lax.broadcasted_iota(jnp.int32, (1, tm), 1)
        live_t = cols >= PAD_ROWS
        lf_t = jnp.where(live_t, _log_sigmoid(gt), 0.0)
        li_t = jnp.where(live_t, gt, NEG)
        ri = lax.broadcasted_iota(jnp.int32, (CHUNK, CHUNK), 0)
        ci = lax.broadcasted_iota(jnp.int32, (CHUNK, CHUNK), 1)
        lower = (ri >= ci).astype(F32)
        upper = (ri <= ci).astype(F32)
        lane = lax.broadcasted_iota(jnp.int32, (CHUNK, LANES), 1)
        sub = lax.broadcasted_iota(jnp.int32, (8, CHUNK), 0)
        for c in range(cpt):
            sl = slice(c * CHUNK, (c + 1) * CHUNK)
            b_all = jnp.dot(lower, lf[sl, :], precision=HIGHEST, preferred_element_type=F32)
            bt_all = jnp.dot(lf_t[:, sl], upper, precision=HIGHEST, preferred_element_type=F32)
            for h in range(nh):
                b = b_all[:, nh + h:nh + h + 1]
                r = li[sl, h:h + 1] - b
                pre = gc[sl, nh + h:nh + h + 1]
                colf_ref[h, sl, :] = jnp.where(lane == 0, b, jnp.where(lane == 1, r, jnp.where(lane == 2, pre, 0.0)))
                b_row = bt_all[nh + h:nh + h + 1, :]
                r_row = li_t[h:h + 1, sl] - b_row
                rowf_ref[h, c, :, :] = jnp.where(sub == 0, r_row, jnp.where(sub == 1, b_row, 0.0))

    return pl.pallas_call(
        body, name=name, grid=(t // tm,),
        in_specs=[pl.BlockSpec((tm, LANES), lambda i: (i, 0)), pl.BlockSpec((16, tm), lambda i: (0, i)),
                  pl.BlockSpec((1, LANES), lambda i: (0, 0)), pl.BlockSpec((16, 1), lambda i: (0, 0))],
        out_specs=[pl.BlockSpec((nh, tm, LANES), lambda i: (0, i, 0)),
                   pl.BlockSpec((nh, cpt, 8, CHUNK), lambda i: (0, i, 0, 0))],
        out_shape=[jax.ShapeDtypeStruct((nh, t, LANES), F32),
                   jax.ShapeDtypeStruct((nh, t // CHUNK, 8, CHUNK), F32)],
        compiler_params=_params(),
    )(graw, graw_t, bias_row, bias_col)


def _gates_bwd(gstat, colf, name):
    nh, t, _ = gstat.shape
    tm = _pick(t, 640, 128)
    cpt = tm // CHUNK
    nt = t // tm
    nc = t // CHUNK

    def body(gs_ref, nx_ref, colf_ref, dg_ref, db_ref):
        i = pl.program_id(0)
        ri = lax.broadcasted_iota(jnp.int32, (CHUNK, CHUNK), 0)
        ci = lax.broadcasted_iota(jnp.int32, (CHUNK, CHUNK), 1)
        upper = (ri <= ci).astype(F32)
        lane = lax.broadcasted_iota(jnp.int32, (CHUNK, LANES), 1)
        total = jnp.zeros((1, LANES), F32)
        for c in range(cpt):
            sl = slice(c * CHUNK, (c + 1) * CHUNK)
            rows = i * tm + c * CHUNK + lax.broadcasted_iota(jnp.int32, (CHUNK, 1), 0)
            live = rows >= PAD_ROWS
            acc = jnp.zeros((CHUNK, LANES), F32)
            for h in range(nh):
                blk = gs_ref[h, sl, :]
                rev = jnp.dot(upper, blk, precision=HIGHEST, preferred_element_type=F32)
                if c + 1 < cpt:
                    carry = gs_ref[h, (c + 1) * CHUNK:(c + 1) * CHUNK + 1, 2:3]
                else:
                    carry = jnp.where(i == nt - 1, 0.0, nx_ref[h, 0:1, 2:3])
                dlogf = rev[:, 0:1] + carry
                pre = colf_ref[h, sl, 2:3]
                dgf = jnp.where(live, dlogf * (1.0 - _sigmoid(pre)), 0.0)
                dgi = jnp.where(live, blk[:, 1:2], 0.0)
                acc = acc + jnp.where(lane == h, dgi, 0.0) + jnp.where(lane == nh + h, dgf, 0.0)
            dg_ref[sl, :] = acc
            total = total + jnp.sum(acc, axis=0, keepdims=True)

        @pl.when(i == 0)
        def _():
            db_ref[...] = jnp.zeros_like(db_ref)

        db_ref[...] += jnp.broadcast_to(total, db_ref.shape)

    return pl.pallas_call(
        body, name=name, grid=(nt,),
        in_specs=[pl.BlockSpec((nh, tm, LANES), lambda i: (0, i, 0)),
                  pl.BlockSpec((nh, CHUNK, LANES), lambda i: (0, jnp.minimum((i + 1) * cpt, nc - 1), 0)),
                  pl.BlockSpec((nh, tm, LANES), lambda i: (0, i, 0))],
        out_specs=[pl.BlockSpec((tm, LANES), lambda i: (i, 0)), pl.BlockSpec((8, LANES), lambda i: (0, 0))],
        out_shape=[jax.ShapeDtypeStruct((t, LANES), F32), jax.ShapeDtypeStruct((8, LANES), F32)],
        compiler_params=_params(),
    )(gstat, gstat, colf)


NT_DIMS = (((1,), (1,)), ((), ()))
TN_DIMS = (((0,), (0,)), ((), ()))


def _dot(a, b):
    return jnp.dot(a.astype(BF16), b.astype(BF16), preferred_element_type=F32)


def _dot_nt(a, b):
    return lax.dot_general(a.astype(BF16), b.astype(BF16), NT_DIMS, preferred_element_type=F32)


def _dot_tn(a, b):
    return lax.dot_general(a.astype(BF16), b.astype(BF16), TN_DIMS, preferred_element_type=F32)


def _chunk_gates(colf_ref, rowf_ref, m_prev):
    b = colf_ref[:, 0:1]
    rcol = colf_ref[:, 1:2]
    rrow = rowf_ref[0:1, :]
    ri = lax.broadcasted_iota(jnp.int32, (CHUNK, CHUNK), 0)
    ci = lax.broadcasted_iota(jnp.int32, (CHUNK, CHUNK), 1)
    log_d = jnp.where(ri >= ci, b + rrow, NEG)
    m_row = jnp.maximum(b + m_prev, jnp.max(log_d, axis=-1, keepdims=True))
    dmat = jnp.exp(log_d - m_row)
    inter = jnp.exp(b + m_prev - m_row)
    b_last = b[CHUNK - 1:CHUNK, :]
    log_w = rcol + b_last
    m_new = jnp.maximum(b_last + m_prev, jnp.max(log_w, axis=0, keepdims=True))
    decay = jnp.exp(b_last + m_prev - m_new)
    w = jnp.exp(log_w - m_new)
    return m_row, dmat, inter, m_new, decay, w


def _mlstm_fwd(p1, colf, rowf, head_w, dk, dv, name):
    t = p1.shape[0]
    nh = N_HEADS
    nc = t // CHUNK
    hw = 2 * dk + 3 * dv

    def body(p_ref, colf_ref, rowf_ref, w_ref, hm_ref, hh_ref, stat_ref, cs_ref, ns_ref, c_ref, n_ref, m_ref):
        c = pl.program_id(0)

        @pl.when(c == 0)
        def _():
            c_ref[...] = jnp.zeros_like(c_ref)
            n_ref[...] = jnp.zeros_like(n_ref)
            m_ref[...] = jnp.zeros_like(m_ref)

        for h in range(nh):
            cols = pl.ds(h * dv, dv)
            head(p_ref.at[:, pl.ds(h * hw, hw)], colf_ref.at[h], rowf_ref.at[h], w_ref.at[:, cols],
                 hm_ref.at[:, cols], hh_ref.at[:, cols], stat_ref.at[h], cs_ref.at[h], ns_ref.at[h],
                 c_ref.at[h], n_ref.at[h], m_ref.at[h])

    def head(p_ref, colf_ref, rowf_ref, w_ref, hm_ref, hh_ref, stat_ref, cs_ref, ns_ref, c_ref, n_ref, m_ref):
        m_prev = m_ref[...]
        n_prev = n_ref[...]
        c_prev = c_ref[...]
        cs_ref[...] = c_prev.astype(BF16)
        sub = lax.broadcasted_iota(jnp.int32, (8, dk), 0)
        ns_ref[...] = jnp.where(sub == 0, n_prev, jnp.where(sub == 1, m_prev, 0.0))

        q = p_ref[:, 0:dk]
        k = p_ref[:, dk:2 * dk]
        v = p_ref[:, 2 * dk:2 * dk + dv]
        o = p_ref[:, 2 * dk + dv:2 * dk + 2 * dv].astype(F32)
        z = p_ref[:, 2 * dk + 2 * dv:2 * dk + 3 * dv].astype(F32)
        m_row, dmat, inter, m_new, decay, w = _chunk_gates(colf_ref, rowf_ref, m_prev)
        s = _dot_nt(q, k) * dmat
        num = _dot(s, v) + inter * _dot(q, c_prev)
        den = jnp.sum(s, axis=-1, keepdims=True) + inter * jnp.sum(q.astype(F32) * n_prev, axis=-1, keepdims=True)
        denom = jnp.maximum(jnp.abs(den), jnp.exp(-m_row))
        hh = num / denom
        hh_ref[...] = hh
        lane = lax.broadcasted_iota(jnp.int32, (CHUNK, LANES), 1)
        stat_ref[...] = jnp.where(lane == 0, den, 0.0)

        c_ref[...] = decay * c_prev + _dot_tn(k, w * v.astype(F32))
        n_ref[...] = decay * n_prev + jnp.sum(w * k.astype(F32), axis=0, keepdims=True)
        m_ref[...] = m_new

        r = lax.rsqrt(jnp.mean(hh * hh, axis=-1, keepdims=True) + RMS_EPS)
        hn = (hh * r) * w_ref[...]
        hm_ref[...] = (hn * _sigmoid(o) * (z * _sigmoid(z))).astype(BF16)

    return pl.pallas_call(
        body, name=name, grid=(nc,),
        in_specs=[pl.BlockSpec((CHUNK, nh * hw), lambda c: (c, 0)),
                  pl.BlockSpec((nh, CHUNK, LANES), lambda c: (0, c, 0)),
                  pl.BlockSpec((nh, None, 8, CHUNK), lambda c: (0, c, 0, 0)),
                  pl.BlockSpec((1, nh * dv), lambda c: (0, 0))],
        out_specs=[pl.BlockSpec((CHUNK, nh * dv), lambda c: (c, 0)),
                   pl.BlockSpec((CHUNK, nh * dv), lambda c: (c, 0)),
                   pl.BlockSpec((nh, CHUNK, LANES), lambda c: (0, c, 0)),
                   pl.BlockSpec((nh, None, dk, dv), lambda c: (0, c, 0, 0)),
                   pl.BlockSpec((nh, None, 8, dk), lambda c: (0, c, 0, 0))],
        out_shape=[jax.ShapeDtypeStruct((t, nh * dv), BF16), jax.ShapeDtypeStruct((t, nh * dv), F32),
                   jax.ShapeDtypeStruct((nh, t, LANES), F32),
                   jax.ShapeDtypeStruct((nh, nc, dk, dv), BF16), jax.ShapeDtypeStruct((nh, nc, 8, dk), F32)],
        scratch_shapes=[pltpu.VMEM((nh, dk, dv), F32), pltpu.VMEM((nh, 1, dk), F32), pltpu.VMEM((nh, 1, 1), F32)],
        compiler_params=_params(),
    )(p1, colf, rowf, head_w)


def _mlstm_bwd(p1, colf, rowf, head_w, hh, stat, csave, nsave, dhm, dk, dv, name):
    t = p1.shape[0]
    nh = N_HEADS
    nc = t // CHUNK
    hw = 2 * dk + 3 * dv

    def body(p_ref, colf_ref, rowf_ref, w_ref, hh_ref, stat_ref, cs_ref, ns_ref, dhm_ref,
             dp_ref, gs_ref, dw_ref, dc_ref, dn_ref):
        c = pl.program_id(0)

        @pl.when(c == 0)
        def _():
            dc_ref[...] = jnp.zeros_like(dc_ref)
            dn_ref[...] = jnp.zeros_like(dn_ref)
            dw_ref[...] = jnp.zeros_like(dw_ref)

        for h in range(nh):
            cols = pl.ds(h * dv, dv)
            head(p_ref.at[:, pl.ds(h * hw, hw)], colf_ref.at[h], rowf_ref.at[h], w_ref.at[:, cols],
                 hh_ref.at[:, cols], stat_ref.at[h], cs_ref.at[h], ns_ref.at[h], dhm_ref.at[:, cols],
                 dp_ref.at[:, pl.ds(h * hw, hw)], gs_ref.at[h], dw_ref.at[h], dc_ref.at[h], dn_ref.at[h])

    def head(p_ref, colf_ref, rowf_ref, w_ref, hh_ref, stat_ref, cs_ref, ns_ref, dhm_ref,
             dp_ref, gs_ref, dw_ref, dc_ref, dn_ref):
        q = p_ref[:, 0:dk]
        k = p_ref[:, dk:2 * dk]
        v = p_ref[:, 2 * dk:2 * dk + dv]
        o = p_ref[:, 2 * dk + dv:2 * dk + 2 * dv].astype(F32)
        z = p_ref[:, 2 * dk + 2 * dv:2 * dk + 3 * dv].astype(F32)
        qf = q.astype(F32)
        kf = k.astype(F32)
        n_prev = ns_ref[0:1, :]
        m_prev = ns_ref[1:2, 0:1]
        c_prev = cs_ref[...]
        m_row, dmat, inter, m_new, decay, w = _chunk_gates(colf_ref, rowf_ref, m_prev)

        hh = hh_ref[...]
        dhm_v = dhm_ref[...].astype(F32)
        so = _sigmoid(o)
        sg = _sigmoid(z)
        sz = z * sg
        r = lax.rsqrt(jnp.mean(hh * hh, axis=-1, keepdims=True) + RMS_EPS)
        hn = (hh * r) * w_ref[...]
        dhn = dhm_v * so * sz
        dp_ref[:, 2 * dk + dv:2 * dk + 2 * dv] = (dhm_v * hn * sz * (so * (1.0 - so))).astype(BF16)
        dp_ref[:, 2 * dk + 2 * dv:2 * dk + 3 * dv] = (dhm_v * hn * so * (sg * (1.0 + z * (1.0 - sg)))).astype(BF16)
        dw_ref[...] += jnp.broadcast_to(jnp.sum(dhn * (hh * r), axis=0, keepdims=True), dw_ref.shape)
        gwn = dhn * w_ref[...]
        dhh = r * gwn - hh * ((r * r * r) * jnp.mean(gwn * hh, axis=-1, keepdims=True))

        den = stat_ref[:, 0:1]
        floor = jnp.exp(-m_row)
        denom = jnp.maximum(jnp.abs(den), floor)
        dnum = dhh / denom
        hdot = jnp.sum(dhh * hh, axis=-1, keepdims=True)
        dden = jnp.where(jnp.abs(den) > floor, -(hdot / denom) * jnp.sign(den), 0.0)
        s = _dot_nt(q, k) * dmat
        dqk = (_dot_nt(dnum, v) + dden) * dmat
        dc_new = dc_ref[...]
        dn_new = dn_ref[...]
        idd = inter * dden
        dq = _dot(dqk, k) + inter * _dot_nt(dnum, c_prev) + idd * n_prev
        dkv = _dot_tn(dqk, q) + w * (_dot_nt(v, dc_new) + dn_new)
        dvv = _dot_tn(s, dnum) + w * _dot(k, dc_new)
        dp_ref[:, 0:dk] = dq.astype(BF16)
        dp_ref[:, dk:2 * dk] = dkv.astype(BF16)
        dp_ref[:, 2 * dk:2 * dk + dv] = dvv.astype(BF16)
        qdq = jnp.sum(qf * dq, axis=-1, keepdims=True)
        kdk = jnp.sum(kf * dkv, axis=-1, keepdims=True)
        dc_prev = decay * dc_new + _dot_tn(inter * qf, dnum)
        dn_prev = decay * dn_new + jnp.sum(idd * qf, axis=0, keepdims=True)
        dc_ref[...] = dc_prev
        dn_ref[...] = dn_prev
        cross = (jnp.sum(jnp.sum(c_prev.astype(F32) * dc_prev, axis=-1, keepdims=True), axis=0, keepdims=True)
                 + jnp.sum(n_prev * dn_prev, axis=-1, keepdims=True))
        lane = lax.broadcasted_iota(jnp.int32, (CHUNK, LANES), 1)
        gs_ref[...] = jnp.where(lane == 0, qdq - kdk, jnp.where(lane == 1, kdk, jnp.where(lane == 2, cross, 0.0)))

    rc = lambda c: (nc - 1 - c, 0)
    rc3 = lambda c: (0, nc - 1 - c, 0)
    rc4 = lambda c: (0, nc - 1 - c, 0, 0)
    return pl.pallas_call(
        body, name=name, grid=(nc,),
        in_specs=[pl.BlockSpec((CHUNK, nh * hw), rc),
                  pl.BlockSpec((nh, CHUNK, LANES), rc3),
                  pl.BlockSpec((nh, None, 8, CHUNK), rc4),
                  pl.BlockSpec((1, nh * dv), lambda c: (0, 0)),
                  pl.BlockSpec((CHUNK, nh * dv), rc),
                  pl.BlockSpec((nh, CHUNK, LANES), rc3),
                  pl.BlockSpec((nh, None, dk, dv), rc4),
                  pl.BlockSpec((nh, None, 8, dk), rc4),
                  pl.BlockSpec((CHUNK, nh * dv), rc)],
        out_specs=[pl.BlockSpec((CHUNK, nh * hw), rc),
                   pl.BlockSpec((nh, CHUNK, LANES), rc3),
                   pl.BlockSpec((nh, 8, dv), lambda c: (0, 0, 0))],
        out_shape=[jax.ShapeDtypeStruct((t, nh * hw), BF16), jax.ShapeDtypeStruct((nh, t, LANES), F32),
                   jax.ShapeDtypeStruct((nh, 8, dv), F32)],
        scratch_shapes=[pltpu.VMEM((nh, dk, dv), F32), pltpu.VMEM((nh, 1, dk), F32)],
        compiler_params=_params(),
    )(p1, colf, rowf, head_w, hh, stat, csave, nsave, dhm)


def _position():
    return lax.axis_index("x"), lax.axis_index("y"), lax.axis_index("c")


class _Exchange:
    def __init__(self, ins, out_shapes, start, mid, finish):
        n = len(ins)
        self.ins, self.out_shapes = list(ins), list(out_shapes)
        self.start, self.mid, self.finish = start, mid, finish
        self.scratch = [pltpu.SemaphoreType.DMA((n, 7)), pltpu.SemaphoreType.DMA((n, 7)),
                        pltpu.SemaphoreType.DMA((n,))]


def _gather_exchange(shards):
    n = len(shards)

    def plan(ins, outs, sems):
        send_sems, recv_sems, local_sems = sems
        x, y, c = _position()
        me, sibling = (x, y, c), (x, y, 1 - c)
        chips = [(1 - x, y), (x, 1 - y), (1 - x, 1 - y)]

        def copy(a, k, block, to, src=None):
            px, py, pc = block
            dst = outs[a].at[4 * px + 2 * py + pc]
            return pltpu.make_async_remote_copy(
                src_ref=dst if src is None else src, dst_ref=dst,
                send_sem=send_sems.at[a, k], recv_sem=recv_sems.at[a, k],
                device_id=to, device_id_type=MESH)

        mine = [pltpu.make_async_copy(ins[a], outs[a].at[4 * x + 2 * y + c], local_sems.at[a]) for a in range(n)]
        first = []
        for a in range(n):
            first.append(copy(a, 0, me, sibling, src=ins[a]))
            first += [copy(a, 1 + j, me, (*chip, c), src=ins[a]) for j, chip in enumerate(chips)]
        ici_in = [copy(a, 1 + j, (*chip, c), me) for j, chip in enumerate(chips) for a in range(n)]
        passed = [copy(a, 4 + j, (*chip, c), sibling) for j, chip in enumerate(chips) for a in range(n)]
        d2d_in = [copy(a, 0, sibling, me) for a in range(n)]
        d2d_in += [copy(a, 4 + j, (*chip, 1 - c), me) for j, chip in enumerate(chips) for a in range(n)]
        return mine, first, ici_in, passed, d2d_in

    def start(ins, outs, sems):
        mine, first, _, _, _ = plan(ins, outs, sems)
        for cp in mine + first:
            cp.start()

    def mid(ins, outs, sems):
        _, _, ici_in, passed, _ = plan(ins, outs, sems)
        for arrived, onward in zip(ici_in, passed):
            arrived.wait_recv()
            onward.start()

    def finish(ins, outs, sems):
        mine, first, _, passed, d2d_in = plan(ins, outs, sems)
        for cp in d2d_in:
            cp.wait_recv()
        for cp in first + passed:
            cp.wait_send()
        for cp in mine:
            cp.wait()

    shapes = [jax.ShapeDtypeStruct((N_DEV,) + s.shape, s.dtype) for s in shards]
    return _Exchange(shards, shapes, start, mid, finish)


def _scatter_exchange(fulls):
    n = len(fulls)

    def plan(ins, outs, sems):
        send_sems, recv_sems, local_sems = sems
        x, y, c = _position()
        my_slot = 4 * x + 2 * y + c
        mine = [pltpu.make_async_copy(ins[a].at[my_slot], outs[a].at[my_slot], local_sems.at[a]) for a in range(n)]
        sends, arrivals = [], []
        for kk in (1, 2, 4, 6, 3, 5, 7):
            kx, ky, kc = (kk >> 2) & 1, (kk >> 1) & 1, kk & 1
            px = 1 - x if kx else x
            py = 1 - y if ky else y
            pc = 1 - c if kc else c
            peer_slot = 4 * px + 2 * py + pc
            for a in range(n):
                pair = dict(send_sem=send_sems.at[a, kk - 1], recv_sem=recv_sems.at[a, kk - 1],
                            device_id=(px, py, pc), device_id_type=MESH)
                sends.append(pltpu.make_async_remote_copy(
                    src_ref=ins[a].at[peer_slot], dst_ref=outs[a].at[my_slot], **pair))
                arrivals.append(pltpu.make_async_remote_copy(
                    src_ref=ins[a].at[peer_slot], dst_ref=outs[a].at[peer_slot], **pair))
        return mine, sends, arrivals

    def start(ins, outs, sems):
        mine, sends, _ = plan(ins, outs, sems)
        for cp in mine + sends:
            cp.start()

    def finish(ins, outs, sems):
        mine, sends, arrivals = plan(ins, outs, sems)
        for cp in arrivals:
            cp.wait_recv()
        for cp in sends:
            cp.wait_send()
        for cp in mine:
            cp.wait()

    shapes = [jax.ShapeDtypeStruct(f.shape, f.dtype) for f in fulls]
    return _Exchange(fulls, shapes, start, None, finish)


def _run_exchange(ex, name):
    n_in, n_out = len(ex.ins), len(ex.out_shapes)

    def body(*refs):
        ins, outs, sems = refs[:n_in], refs[n_in:n_in + n_out], refs[n_in + n_out:]
        ex.start(ins, outs, sems)
        if ex.mid is not None:
            ex.mid(ins, outs, sems)
        ex.finish(ins, outs, sems)

    any_spec = pl.BlockSpec(memory_space=pl.ANY)
    return pl.pallas_call(
        body, name=name,
        in_specs=[any_spec] * n_in, out_specs=[any_spec] * n_out,
        out_shape=ex.out_shapes, scratch_shapes=ex.scratch,
    )(*ex.ins)


def _adamw_math(w, g, m, v):
    m = ADAM_B1 * m + (1.0 - ADAM_B1) * g
    v = ADAM_B2 * v + (1.0 - ADAM_B2) * (g * g)
    m_hat = m / (1.0 - ADAM_B1 ** ADAM_STEP)
    v_hat = v / (1.0 - ADAM_B2 ** ADAM_STEP)
    delta = -ADAM_LR * (m_hat / (jnp.sqrt(v_hat) + ADAM_EPS) + ADAM_WD * w)
    return delta, m, v


def _adamw_sharded(parts, w, m, v, name):
    _, r, c = parts.shape
    tr = _pick(r, 128, 8)

    def body(p_ref, w_ref, m_ref, v_ref, g_ref, d_ref, nm_ref, nv_ref):
        g = p_ref[0].astype(F32)
        for s in range(1, N_DEV):
            g = g + p_ref[s].astype(F32)
        delta, m_new, v_new = _adamw_math(w_ref[...], g, m_ref[...], v_ref[...])
        g_ref[...] = g
        d_ref[...] = delta
        nm_ref[...] = m_new
        nv_ref[...] = v_new

    blk = pl.BlockSpec((tr, c), lambda i: (i, 0))
    return pl.pallas_call(
        body, name=name, grid=(r // tr,),
        in_specs=[pl.BlockSpec((N_DEV, tr, c), lambda i: (0, i, 0)), blk, blk, blk],
        out_specs=[blk] * 4,
        out_shape=[jax.ShapeDtypeStruct((r, c), F32)] * 4,
        compiler_params=_params(),
    )(parts, w, m, v)


def _sum_devices(parts, name):
    _, r, c = parts.shape

    def body(p_ref, o_ref):
        g = p_ref[0]
        for s in range(1, N_DEV):
            g = g + p_ref[s]
        o_ref[...] = g

    return pl.pallas_call(
        body, name=name, out_shape=jax.ShapeDtypeStruct((r, c), F32), compiler_params=_params(),
    )(parts)


def _adamw_small(gs, ws, ms, vs, name):
    n = len(gs)

    def body(*refs):
        g_refs, w_refs, m_refs, v_refs = refs[:n], refs[n:2 * n], refs[2 * n:3 * n], refs[3 * n:4 * n]
        d_refs, nm_refs, nv_refs = refs[4 * n:5 * n], refs[5 * n:6 * n], refs[6 * n:7 * n]
        for a in range(n):
            delta, m_new, v_new = _adamw_math(w_refs[a][...], g_refs[a][...], m_refs[a][...], v_refs[a][...])
            d_refs[a][...] = delta
            nm_refs[a][...] = m_new
            nv_refs[a][...] = v_new

    shapes = [jax.ShapeDtypeStruct(w.shape, F32) for w in ws]
    outs = pl.pallas_call(
        body, name=name, out_shape=shapes * 3, compiler_params=_params(),
    )(*gs, *ws, *ms, *vs)
    return outs[:n], outs[n:2 * n], outs[2 * n:]


def _pad_rows(a, rows):
    return jnp.pad(a, ((0, rows - a.shape[0]), (0, 0)))


def kernel(x, meta_tokens, norm_w, conv_in_w, conv_w, conv_out_w, mlstm_in_w, mlstm_gate_b, mlstm_head_norm_w, mlstm_out_w, final_norm_w, loss_target, m_meta_tokens, m_norm_w, m_conv_in_w, m_conv_w, m_conv_out_w, m_mlstm_in_w, m_mlstm_gate_b, m_mlstm_head_norm_w, m_mlstm_out_w, m_final_norm_w, v_meta_tokens, v_norm_w, v_conv_in_w, v_conv_w, v_conv_out_w, v_mlstm_in_w, v_mlstm_gate_b, v_mlstm_head_norm_w, v_mlstm_out_w, v_final_norm_w):
    seq, d = x.shape[1], x.shape[2]
    t = seq + LEAD
    e = 2 * d
    ec = e // 2
    nh = N_HEADS
    dv = e // nh
    dk = dv // 2
    qk = nh * dk
    hw = 2 * dk + 3 * dv
    n_in = 2 * qk + 3 * e + 2 * nh
    n_in_s = n_in // N_DEV
    me = 4 * lax.axis_index("x") + 2 * lax.axis_index("y") + lax.axis_index("c")
    tm = _pick(t, 832, 16)

    small = jnp.concatenate([
        meta_tokens,
        _pad_rows(conv_w[0].reshape(3 * (e // N_DEV) // LANES, LANES), 8),
        _pad_rows(mlstm_head_norm_w[0].reshape((e // N_DEV) // LANES, LANES), 8),
    ], axis=0) if d // N_DEV == LANES else None
    assert small is not None, "the packed small-weight block assumes d_model / 8 == 128"
    w_ci, w_co, small_g = _run_exchange(
        _gather_exchange([conv_in_w[0].astype(BF16), conv_out_w[0].astype(BF16), small]), "gather_weights")
    w_co = w_co.reshape(e, d)
    meta_full = jnp.transpose(small_g[:, 0:N_META, :], (1, 0, 2)).reshape(N_META, d)
    cw_rows = 3 * (e // N_DEV) // LANES
    conv_w_full = jnp.transpose(
        small_g[:, N_META:N_META + cw_rows, :].reshape(N_DEV, 3, e // N_DEV), (1, 0, 2)).reshape(3, e)
    hn_rows = (e // N_DEV) // LANES
    head_w_full = small_g[:, N_META + 8:N_META + 8 + hn_rows, :].reshape(1, e)

    h0 = jnp.concatenate([jnp.zeros((PAD_ROWS, d), F32), meta_full, x[0]], axis=0)
    tgt = loss_target[0]

    ci_map = lambda blk: 2 * (blk % 4) + blk // 4
    u0 = _rms_fwd(h0, norm_w[0:1], "rms0")
    p0, w_mi, w_mo = _matmul(
        u0, w_ci, form="nn", m=t, n=8 * ec, kdim=d, tm=tm, tn=ec, tk=d, out_dtype=BF16, name="conv_in",
        b_spec=pl.BlockSpec((None, d, ec), lambda i, j, k: (ci_map(j), 0, 0)),
        exchange=_gather_exchange([mlstm_in_w[0].astype(BF16), mlstm_out_w[0].astype(BF16)]))
    w_mo = w_mo.reshape(e, d)
    g0, y0 = _conv_gate_fwd(p0, conv_w_full, "conv_gate")
    h1 = _matmul(g0, w_co, form="nn", m=t, n=d, kdim=e, tm=tm, tn=d, tk=e, out_dtype=F32, name="conv_out",
                 residual=h0)

    w_nat = jnp.transpose(w_mi, (1, 0, 2)).reshape(d, n_in)
    wq = (w_nat[:, 0:qk] * (dk ** -0.5)).astype(BF16).reshape(d, nh, dk)
    wk = w_nat[:, qk:2 * qk].reshape(d, nh, dk)
    wv = w_nat[:, 2 * qk:2 * qk + e].reshape(d, nh, dv)
    wo = w_nat[:, 2 * qk + e:2 * qk + 2 * e].reshape(d, nh, dv)
    wz = w_nat[:, 2 * qk + 2 * e:2 * qk + 3 * e].reshape(d, nh, dv)
    w_p1 = jnp.concatenate([wq, wk, wv, wo, wz], axis=2).reshape(d, nh * hw)
    w_gate = jnp.pad(w_nat[:, 2 * qk + 3 * e:], ((0, 0), (0, LANES - 2 * nh)))
    w_gate_t = jnp.pad(w_nat[:, 2 * qk + 3 * e:].T, ((0, 16 - 2 * nh), (0, 0)))
    bias_row = jnp.pad(mlstm_gate_b, ((0, 0), (0, LANES - 2 * nh)))
    bias_col = jnp.pad(mlstm_gate_b.T, ((0, 16 - 2 * nh), (0, 0)))

    u1 = _rms_fwd(h1, norm_w[1:2], "rms1")
    p1 = _matmul(u1, w_p1, form="nn", m=t, n=nh * hw, kdim=d, tm=tm, tn=_pick(nh * hw, 1024, LANES), tk=d,
                 out_dtype=BF16, name="mlstm_in")
    graw = _matmul(u1, w_gate, form="nn", m=t, n=LANES, kdim=d, tm=tm, tn=LANES, tk=d, out_dtype=F32,
                   name="gates_col")
    graw_t = _matmul(w_gate_t, u1, form="nt", m=16, n=t, kdim=d, tm=16, tn=_pick(t, 1664, LANES), tk=d,
                     out_dtype=F32, name="gates_row")
    colf, rowf = _gates_fwd(graw, graw_t, bias_row, bias_col, "gates_fwd")
    hm, hh, stat, csave, nsave = _mlstm_fwd(p1, colf, rowf, head_w_full, dk, dv, "mlstm_fwd")
    h2 = _matmul(hm, w_mo, form="nn", m=t, n=d, kdim=e, tm=tm, tn=d, tk=e, out_dtype=F32, name="mlstm_out",
                 residual=h1)

    dh2, dwf, loss_part = _final_loss(h2, final_norm_w.reshape(1, d), tgt, "final_loss")

    dhm = _matmul(dh2, w_mo, form="nt", m=t, n=e, kdim=d, tm=tm, tn=_pick(e, 1024, LANES), tk=d, out_dtype=BF16,
                  name="mlstm_out_dx")
    g_mo = _matmul(hm, dh2, form="tn", m=e, n=d, kdim=t, tm=_pick(e, 1024, LANES), tn=d, tk=tm, out_dtype=BF16,
                   name="mlstm_out_dw")
    dp1, gstat, dhead = _mlstm_bwd(p1, colf, rowf, head_w_full, hh, stat, csave, nsave, dhm, dk, dv, "mlstm_bwd")
    dgates, dbias = _gates_bwd(gstat, colf, "gates_bwd")
    du1 = _matmul(dp1, w_p1, form="nt", m=t, n=d, kdim=nh * hw, tm=tm, tn=d, tk=_pick(nh * hw, 1024, LANES),
                  out_dtype=F32, name="mlstm_in_dx")
    du1 = _matmul(dgates, w_gate, form="nt", m=t, n=d, kdim=LANES, tm=tm, tn=d, tk=LANES, out_dtype=F32,
                  name="gates_dx", residual=du1)
    g_p1 = _matmul(u1, dp1, form="tn", m=d, n=nh * hw, kdim=t, tm=d, tn=_pick(nh * hw, 1024, LANES), tk=tm,
                   out_dtype=BF16, name="mlstm_in_dw")
    g_gate = _matmul(u1, dgates, form="tn", m=d, n=LANES, kdim=t, tm=d, tn=LANES, tk=tm, out_dtype=F32,
                     name="gates_dw")
    dh1, dnw1 = _rms_bwd(h1, norm_w[1:2], du1, dh2, "rms1_bwd")

    gp = g_p1.reshape(d, nh, hw)
    gq = (gp[:, :, 0:dk] * (dk ** -0.5)).astype(BF16).reshape(d, qk)
    gk = gp[:, :, dk:2 * dk].reshape(d, qk)
    gv = gp[:, :, 2 * dk:2 * dk + dv].reshape(d, e)
    go = gp[:, :, 2 * dk + dv:2 * dk + 2 * dv].reshape(d, e)
    gz = gp[:, :, 2 * dk + 2 * dv:].reshape(d, e)
    g_mi = jnp.concatenate([gq, gk, gv, go, gz, g_gate[:, 0:2 * nh].astype(BF16)], axis=1)
    g_mi = jnp.transpose(g_mi.reshape(d, N_DEV, n_in_s), (1, 0, 2))

    dg0 =_matmul(dh1, w_co, form="nt", m=t, n=e, kdim=d, tm=tm, tn=_pick(e, 1024, LANES), tk=d, out_dtype=BF16,
                  name="conv_out_dx")
    g_co = _matmul(g0, dh1, form="tn", m=e, n=d, kdim=t, tm=_pick(e, 1024, LANES), tn=d, tk=tm, out_dtype=BF16,
                   name="conv_out_dw")
    dp0, dconv = _conv_gate_bwd(p0, y0, dg0, conv_w_full, "conv_gate_bwd")
    g_ci, r_co, r_mi, r_mo = _matmul(
        u0, dp0, form="tn", m=d, n=8 * ec, kdim=t, tm=d, tn=ec, tk=tm, out_dtype=BF16, name="conv_in_dw",
        out_shape=(N_DEV, d, ec), out_spec=pl.BlockSpec((None, d, ec), lambda i, j, k: (ci_map(j), 0, 0)),
        exchange=_scatter_exchange([g_co.reshape(N_DEV, e // N_DEV, d), g_mi, g_mo.reshape(N_DEV, e // N_DEV, d)]))
    du0, r_ci = _matmul(
        dp0, w_ci, form="nt", m=t, n=d, kdim=8 * ec, tm=tm, tn=d, tk=ec, out_dtype=F32, name="conv_in_dx",
        b_spec=pl.BlockSpec((None, d, ec), lambda i, j, k: (ci_map(k), 0, 0)),
        exchange=_scatter_exchange([g_ci]))
    grad_x, dmeta, dnw0 = _rms_bwd_first(h0, norm_w[0:1], du0, dh1, "rms0_bwd")
    grad_x = grad_x[None]

    row8 = lax.broadcasted_iota(jnp.int32, (8, 1), 0)
    loss_wide = jnp.pad(loss_part, ((0, 0), (0, d - LANES)))
    payload = jnp.concatenate([
        jnp.where(row8 == 0, dnw0, jnp.where(row8 == 1, dnw1, 0.0)),
        jnp.where(row8 == 0, dwf, jnp.where(row8 == 1, loss_wide, 0.0)),
        jnp.where(row8 == 0, jnp.pad(dbias, ((0, 0), (0, d - LANES))), 0.0),
        dmeta,
        _pad_rows(dconv[0:3].reshape(3 * e // d, d), 8),
        _pad_rows(dhead[:, 0, :].reshape(e // d, d), 8),
    ], axis=0)
    (payload_g,) = _run_exchange(_gather_exchange([payload]), "gather_small_grads")
    tot = _sum_devices(payload_g, "sum_small_grads")

    loss = tot[9, 0]
    g_norm = tot[0:2]
    g_final = tot[8]
    g_gate_b = tot[16:17, 0:2 * nh]
    g_meta = lax.dynamic_slice(tot[24:24 + N_META], (0, me * (d // N_DEV)), (N_META, d // N_DEV))
    g_conv_w = lax.dynamic_slice(tot[40:40 + 3 * e // d].reshape(3, e), (0, me * (e // N_DEV)), (3, e // N_DEV))
    g_head = lax.dynamic_slice(tot[48:48 + e // d].reshape(1, e), (0, me * (e // N_DEV)), (1, e // N_DEV))

    g1, d1, nm1, nv1 = _adamw_sharded(r_ci, conv_in_w[0], m_conv_in_w[0], v_conv_in_w[0], "adamw_conv_in")
    g2, d2, nm2, nv2 = _adamw_sharded(r_co, conv_out_w[0], m_conv_out_w[0], v_conv_out_w[0], "adamw_conv_out")
    g3, d3, nm3, nv3 = _adamw_sharded(r_mi, mlstm_in_w[0], m_mlstm_in_w[0], v_mlstm_in_w[0], "adamw_mlstm_in")
    g4, d4, nm4, nv4 = _adamw_sharded(r_mo, mlstm_out_w[0], m_mlstm_out_w[0], v_mlstm_out_w[0], "adamw_mlstm_out")

    small_g = [g_meta, g_norm, g_conv_w, g_gate_b, g_head, g_final.reshape(1, d)]
    small_w = [meta_tokens, norm_w, conv_w[0], mlstm_gate_b, mlstm_head_norm_w, final_norm_w.reshape(1, d)]
    small_m = [m_meta_tokens, m_norm_w, m_conv_w[0], m_mlstm_gate_b, m_mlstm_head_norm_w, m_final_norm_w.reshape(1, d)]
    small_v = [v_meta_tokens, v_norm_w, v_conv_w[0], v_mlstm_gate_b, v_mlstm_head_norm_w, v_final_norm_w.reshape(1, d)]
    sd, snm, snv = _adamw_small(small_g, small_w, small_m, small_v, "adamw_small")

    def order(meta, norm, cin, cw, cout, min_, gb, hn, mout, fin):
        return (meta, norm, cin[None], cw[None], cout[None], min_[None], gb, hn, mout[None], fin.reshape(d))

    grads = order(g_meta, g_norm, g1, g_conv_w, g2, g3, g_gate_b, g_head, g4, g_final)
    deltas = order(sd[0], sd[1], d1, sd[2], d2, d3, sd[3], sd[4], d4, sd[5])
    new_m = order(snm[0], snm[1], nm1, snm[2], nm2, nm3, snm[3], snm[4], nm4, snm[5])
    new_v = order(snv[0], snv[1], nv1, snv[2], nv2, nv3, snv[3], snv[4], nv4, snv[5])
    return (loss, grad_x, *grads, *deltas, *new_m, *new_v)
```

```python
import functools

import jax
import jax.numpy as jnp
from jax import lax
from jax.experimental import pallas as pl
from jax.experimental.pallas import tpu as pltpu

F32 = jnp.float32
BF16 = jnp.bfloat16
MESH = pl.DeviceIdType.MESH

N_DEV = 8
N_META = 16
N_HEADS = 4
CHUNK = 64
CHUNKS_PER_STEP = 2
LEAD = 128
PAD_ROWS = LEAD - N_META
RMS_EPS = 1e-6
NEG = -1e30
LANES = 128
VMEM_LIMIT = 48 * 1024 * 1024

ADAM_LR = 0.001
ADAM_B1 = 0.9
ADAM_B2 = 0.999
ADAM_EPS = 1e-08
ADAM_WD = 0.01
ADAM_STEP = 10

HIGHEST = lax.Precision.HIGHEST


def _pick(n, target, mult):
    best = None
    for d in range(mult, min(n, target) + 1, mult):
        if n % d == 0:
            best = d
    return n if best is None else best


def _params(**kw):
    return pltpu.CompilerParams(vmem_limit_bytes=VMEM_LIMIT, **kw)


def _sigmoid(x):
    return 0.5 * jnp.tanh(0.5 * x) + 0.5


def _matmul(a, b, *, form, m, n, kdim, tm, tn, tk, out_dtype, name,
            a_spec=None, b_spec=None, out_spec=None, out_shape=None, residual=None, exchange=None):
    ni, nj, nk = m // tm, n // tn, kdim // tk
    assert ni * tm == m and nj * tn == n and nk * tk == kdim, (name, m, n, kdim, tm, tn, tk)
    if form == "nn":
        dn = (((1,), (0,)), ((), ()))
        a_def = pl.BlockSpec((tm, tk), lambda i, j, k: (i, k))
        b_def = pl.BlockSpec((tk, tn), lambda i, j, k: (k, j))
    elif form == "nt":
        dn = (((1,), (1,)), ((), ()))
        a_def = pl.BlockSpec((tm, tk), lambda i, j, k: (i, k))
        b_def = pl.BlockSpec((tn, tk), lambda i, j, k: (j, k))
    else:
        dn = (((0,), (0,)), ((), ()))
        a_def = pl.BlockSpec((tk, tm), lambda i, j, k: (k, i))
        b_def = pl.BlockSpec((tk, tn), lambda i, j, k: (k, j))
    a_spec = a_def if a_spec is None else a_spec
    b_spec = b_def if b_spec is None else b_spec
    o_spec = pl.BlockSpec((tm, tn), lambda i, j, k: (i, j)) if out_spec is None else out_spec
    has_res = residual is not None
    ex = exchange
    n_ex_in = 0 if ex is None else len(ex.ins)
    n_ex_out = 0 if ex is None else len(ex.out_shapes)
    n_in = 2 + has_res + n_ex_in
    steps = ni * nj * nk

    def body(*refs):
        a_ref, b_ref = refs[:2]
        r_ref = refs[2] if has_res else None
        o_ref = refs[n_in]
        if ex is not None:
            ex_refs = (refs[n_in - n_ex_in:n_in], refs[n_in + 1:n_in + 1 + n_ex_out], refs[len(refs) - 3:])
            step = (pl.program_id(0) * nj + pl.program_id(1)) * nk + pl.program_id(2)

            @pl.when(step == 0)
            def _():
                ex.start(*ex_refs)

            if ex.mid is not None:
                @pl.when(step == (3 * steps) // 4)
                def _():
                    ex.mid(*ex_refs)

        def product():
            return lax.dot_general(a_ref[...].astype(BF16), b_ref[...].astype(BF16), dn,
                                   preferred_element_type=F32)

        def finish(acc):
            if has_res:
                acc = acc + r_ref[...].astype(F32)
            o_ref[...] = acc.astype(o_ref.dtype)

        if nk == 1:
            finish(product())
        else:
            acc_ref = refs[n_in + 1 + n_ex_out]
            k = pl.program_id(2)

            @pl.when(k == 0)
            def _():
                acc_ref[...] = jnp.zeros_like(acc_ref)

            acc_ref[...] += product()

            @pl.when(k == nk - 1)
            def _():
                finish(acc_ref[...])

        if ex is not None:
            @pl.when(step == steps - 1)
            def _():
                ex.finish(*ex_refs)

    in_specs = [a_spec, b_spec]
    args = [a, b]
    if has_res:
        in_specs.append(pl.BlockSpec((tm, tn), lambda i, j, k: (i, j)))
        args.append(residual)
    out_specs = o_spec
    out_shapes = jax.ShapeDtypeStruct((m, n) if out_shape is None else out_shape, out_dtype)
    scratch = [] if nk == 1 else [pltpu.VMEM((tm, tn), F32)]
    if ex is not None:
        any_spec = pl.BlockSpec(memory_space=pl.ANY)
        in_specs += [any_spec] * n_ex_in
        args += ex.ins
        out_specs = [o_spec] + [any_spec] * n_ex_out
        out_shapes = [out_shapes] + ex.out_shapes
        scratch = scratch + ex.scratch
    return pl.pallas_call(
        body, name=name, grid=(ni, nj, nk),
        in_specs=in_specs, out_specs=out_specs, out_shape=out_shapes,
        scratch_shapes=scratch, compiler_params=_params(),
    )(*args)


def _rms_fwd(h, w, name):
    t, d = h.shape
    tm = _pick(t, 416, 16)

    def body(h_ref, w_ref, u_ref):
        x = h_ref[...]
        r = lax.rsqrt(jnp.mean(x * x, axis=-1, keepdims=True) + RMS_EPS)
        u_ref[...] = ((x * r) * w_ref[...]).astype(BF16)

    return pl.pallas_call(
        body, name=name, grid=(t // tm,),
        in_specs=[pl.BlockSpec((tm, d), lambda i: (i, 0)), pl.BlockSpec((1, d), lambda i: (0, 0))],
        out_specs=pl.BlockSpec((tm, d), lambda i: (i, 0)),
        out_shape=jax.ShapeDtypeStruct((t, d), BF16),
        compiler_params=_params(),
    )(h, w)


def _rms_bwd(h, w, du, dres, name):
    t, d = h.shape
    tm = _pick(t, 416, 16)

    def body(h_ref, w_ref, du_ref, dres_ref, dh_ref, dw_ref):
        i = pl.program_id(0)
        x = h_ref[...]
        g = du_ref[...].astype(F32)
        r = lax.rsqrt(jnp.mean(x * x, axis=-1, keepdims=True) + RMS_EPS)
        gw = g * w_ref[...]
        dot = jnp.mean(gw * x, axis=-1, keepdims=True)
        dh_ref[...] = dres_ref[...] + (r * gw - x * ((r * r * r) * dot))
        part = jnp.sum(g * (x * r), axis=0, keepdims=True)

        @pl.when(i == 0)
        def _():
            dw_ref[...] = jnp.zeros_like(dw_ref)

        dw_ref[...] += jnp.broadcast_to(part, dw_ref.shape)

    row = pl.BlockSpec((tm, d), lambda i: (i, 0))
    return pl.pallas_call(
        body, name=name, grid=(t // tm,),
        in_specs=[row, pl.BlockSpec((1, d), lambda i: (0, 0)), row, row],
        out_specs=[row, pl.BlockSpec((8, d), lambda i: (0, 0))],
        out_shape=[jax.ShapeDtypeStruct((t, d), F32), jax.ShapeDtypeStruct((8, d), F32)],
        compiler_params=_params(),
    )(h, w, du, dres)


def _rms_bwd_first(h, w, du, dres, name):
    t, d = h.shape
    tm = LEAD

    def body(h_ref, w_ref, du_ref, dres_ref, gx_ref, dmeta_ref, dw_ref):
        i = pl.program_id(0)
        x = h_ref[...]
        g = du_ref[...].astype(F32)
        r = lax.rsqrt(jnp.mean(x * x, axis=-1, keepdims=True) + RMS_EPS)
        gw = g * w_ref[...]
        dot = jnp.mean(gw * x, axis=-1, keepdims=True)
        dh = dres_ref[...] + (r * gw - x * ((r * r * r) * dot))
        part = jnp.sum(g * (x * r), axis=0, keepdims=True)

        @pl.when(i == 0)
        def _():
            dw_ref[...] = jnp.zeros_like(dw_ref)
            dmeta_ref[...] = dh[PAD_ROWS:LEAD, :]

        @pl.when(i > 0)
        def _():
            gx_ref[...] = dh

        dw_ref[...] += jnp.broadcast_to(part, dw_ref.shape)

    row = pl.BlockSpec((tm, d), lambda i: (i, 0))
    return pl.pallas_call(
        body, name=name, grid=(t // tm,),
        in_specs=[row, pl.BlockSpec((1, d), lambda i: (0, 0)), row, row],
        out_specs=[pl.BlockSpec((tm, d), lambda i: (jnp.maximum(i - 1, 0), 0)),
                   pl.BlockSpec((N_META, d), lambda i: (0, 0)), pl.BlockSpec((8, d), lambda i: (0, 0))],
        out_shape=[jax.ShapeDtypeStruct((t - LEAD, d), F32), jax.ShapeDtypeStruct((N_META, d), F32),
                   jax.ShapeDtypeStruct((8, d), F32)],
        compiler_params=_params(),
    )(h, w, du, dres)


def _final_loss(h, w, tgt, name):
    t, d = h.shape
    tm = LEAD

    def body(h_ref, w_ref, t_ref, dh_ref, dw_ref, loss_ref):
        i = pl.program_id(0)
        x = h_ref[...]
        r = lax.rsqrt(jnp.mean(x * x, axis=-1, keepdims=True) + RMS_EPS)
        yn = x * r
        y = yn * w_ref[...]
        rows = i * tm + lax.broadcasted_iota(jnp.int32, (tm, 1), 0)
        diff = jnp.where(rows >= LEAD, y - t_ref[...], 0.0)
        tile_loss = 0.5 * jnp.sum(jnp.mean(diff * diff, axis=-1, keepdims=True), axis=0, keepdims=True)
        dy = diff / d
        gw = dy * w_ref[...]
        dot = jnp.mean(gw * x, axis=-1, keepdims=True)
        dh_ref[...] = r * gw - x * ((r * r * r) * dot)
        part = jnp.sum(dy * yn, axis=0, keepdims=True)

        @pl.when(i == 0)
        def _():
            dw_ref[...] = jnp.zeros_like(dw_ref)
            loss_ref[...] = jnp.zeros_like(loss_ref)

        dw_ref[...] += jnp.broadcast_to(part, dw_ref.shape)
        loss_ref[...] += jnp.broadcast_to(tile_loss, loss_ref.shape)

    row = pl.BlockSpec((tm, d), lambda i: (i, 0))
    return pl.pallas_call(
        body, name=name, grid=(t // tm,),
        in_specs=[row, pl.BlockSpec((1, d), lambda i: (0, 0)),
                  pl.BlockSpec((tm, d), lambda i: (jnp.maximum(i - 1, 0), 0))],
        out_specs=[row, pl.BlockSpec((8, d), lambda i: (0, 0)), pl.BlockSpec((8, LANES), lambda i: (0, 0))],
        out_shape=[jax.ShapeDtypeStruct((t, d), F32), jax.ShapeDtypeStruct((8, d), F32),
                   jax.ShapeDtypeStruct((8, LANES), F32)],
        compiler_params=_params(),
    )(h, w, tgt)


def _conv_gate_fwd(p0, conv_w, name):
    t = p0.shape[0]
    ec = p0.shape[1] // 8
    tm = _pick(t, 416, 16)
    nt = t // tm

    def body(p_ref, w_ref, g_ref, y_ref, ext_ref):
        i = pl.program_id(1)
        bg = p_ref[:, 0:ec].astype(F32)
        cg = p_ref[:, ec:2 * ec].astype(F32)
        xin = p_ref[:, 2 * ec:3 * ec].astype(F32)
        z = p_ref[:, 3 * ec:4 * ec].astype(F32)
        a = cg * xin

        @pl.when(i == 0)
        def _():
            ext_ref[0:8, :] = jnp.zeros((8, ec), F32)

        ext_ref[8:8 + tm, :] = a
        y = (w_ref[0:1, :] * ext_ref[6:6 + tm, :] + w_ref[1:2, :] * ext_ref[7:7 + tm, :]
             + w_ref[2:3, :] * a)
        ext_ref[0:8, :] = a[tm - 8:tm, :]
        y_ref[...] = y.astype(BF16)
        g_ref[...] = ((z * _sigmoid(z)) * bg * y).astype(BF16)

    out = pl.BlockSpec((tm, ec), lambda j, i: (i, j))
    return pl.pallas_call(
        body, name=name, grid=(2, nt),
        in_specs=[pl.BlockSpec((tm, 4 * ec), lambda j, i: (i, j)), pl.BlockSpec((3, ec), lambda j, i: (0, j))],
        out_specs=[out, out],
        out_shape=[jax.ShapeDtypeStruct((t, 2 * ec), BF16)] * 2,
        scratch_shapes=[pltpu.VMEM((tm + 8, ec), F32)],
        compiler_params=_params(),
    )(p0, conv_w)


def _conv_gate_bwd(p0, y, dg, conv_w, name):
    t = p0.shape[0]
    ec = p0.shape[1] // 8
    tm = _pick(t, 208, 16)
    nt = t // tm

    def body(p_ref, y_ref, dg_ref, w_ref, dp_ref, dw_ref, ext_ref):
        i = pl.program_id(1)
        bg = p_ref[:, 0:ec].astype(F32)
        cg = p_ref[:, ec:2 * ec].astype(F32)
        xin = p_ref[:, 2 * ec:3 * ec].astype(F32)
        z = p_ref[:, 3 * ec:4 * ec].astype(F32)
        yv = y_ref[...].astype(F32)
        dgv = dg_ref[...].astype(F32)
        sig = _sigmoid(z)
        sz = z * sig
        dp_ref[:, 3 * ec:4 * ec] = (dgv * bg * yv * (sig * (1.0 + z * (1.0 - sig)))).astype(BF16)
        dp_ref[:, 0:ec] = (dgv * sz * yv).astype(BF16)
        dy = dgv * sz * bg

        @pl.when(i == 0)
        def _():
            ext_ref[tm:tm + 8, :] = jnp.zeros((8, ec), F32)
            dw_ref[...] = jnp.zeros_like(dw_ref)

        ext_ref[0:tm, :] = dy
        dy1 = ext_ref[1:tm + 1, :]
        dy2 = ext_ref[2:tm + 2, :]
        ext_ref[tm:tm + 8, :] = dy[0:8, :]
        da = w_ref[0:1, :] * dy2 + w_ref[1:2, :] * dy1 + w_ref[2:3, :] * dy
        dp_ref[:, ec:2 * ec] = (da * xin).astype(BF16)
        dp_ref[:, 2 * ec:3 * ec] = (da * cg).astype(BF16)
        a = cg * xin
        dw_ref[0:1, :] += jnp.sum(a * dy2, axis=0, keepdims=True)
        dw_ref[1:2, :] += jnp.sum(a * dy1, axis=0, keepdims=True)
        dw_ref[2:3, :] += jnp.sum(a * dy, axis=0, keepdims=True)

    rev = lambda j, i: (nt - 1 - i, j)
    return pl.pallas_call(
        body, name=name, grid=(2, nt),
        in_specs=[pl.BlockSpec((tm, 4 * ec), rev), pl.BlockSpec((tm, ec), rev), pl.BlockSpec((tm, ec), rev),
                  pl.BlockSpec((3, ec), lambda j, i: (0, j))],
        out_specs=[pl.BlockSpec((tm, 4 * ec), rev), pl.BlockSpec((8, ec), lambda j, i: (0, j))],
        out_shape=[jax.ShapeDtypeStruct((t, 8 * ec), BF16), jax.ShapeDtypeStruct((8, 2 * ec), F32)],
        scratch_shapes=[pltpu.VMEM((tm + 8, ec), F32)],
        compiler_params=_params(),
    )(p0, y, dg, conv_w)


def _log_sigmoid(x):
    return jnp.minimum(x, 0.0) - jnp.log(1.0 + jnp.exp(-jnp.abs(x)))


def _gates_fwd(graw, graw_t, bias_row, bias_col, name):
    t = graw.shape[0]
    tm = _pick(t, 640, 128)
    cpt = tm // CHUNK
    nh = N_HEADS

    def body(g_ref, gt_ref, br_ref, bc_ref, colf_ref, rowf_ref):
        i = pl.program_id(0)
        gc = g_ref[...] + br_ref[...]
        rows = i * tm + lax.broadcasted_iota(jnp.int32, (tm, 1), 0)
        live = rows >= PAD_ROWS
        lf = jnp.where(live, _log_sigmoid(gc), 0.0)
        li = jnp.where(live, gc, NEG)
        gt = gt_ref[...] + bc_ref[...]
        cols = i * tm + lax.broadcasted_iota(jnp.int32, (1, tm), 1)
        live_t = cols >= PAD_ROWS
        lf_t = jnp.where(live_t, _log_sigmoid(gt), 0.0)
        li_t = jnp.where(live_t, gt, NEG)
        ri = lax.broadcasted_iota(jnp.int32, (CHUNK, CHUNK), 0)
        ci = lax.broadcasted_iota(jnp.int32, (CHUNK, CHUNK), 1)
        lower = (ri >= ci).astype(F32)
        upper = (ri <= ci).astype(F32)
        lane = lax.broadcasted_iota(jnp.int32, (CHUNK, LANES), 1)
        sub = lax.broadcasted_iota(jnp.int32, (8, CHUNK), 0)
        for c in range(cpt):
            sl = slice(c * CHUNK, (c + 1) * CHUNK)
            b_all = jnp.dot(lower, lf[sl, :], precision=HIGHEST, preferred_element_type=F32)
            bt_all = jnp.dot(lf_t[:, sl], upper, precision=HIGHEST, preferred_element_type=F32)
            for h in range(nh):
                b = b_all[:, nh + h:nh + h + 1]
                r = li[sl, h:h + 1] - b
                pre = gc[sl, nh + h:nh + h + 1]
                colf_ref[h, sl, :] = jnp.where(lane == 0, b, jnp.where(lane == 1, r, jnp.where(lane == 2, pre, 0.0)))
                b_row = bt_all[nh + h:nh + h + 1, :]
                r_row = li_t[h:h + 1, sl] - b_row
                rowf_ref[h, c, :, :] = jnp.where(sub == 0, r_row, jnp.where(sub == 1, b_row, 0.0))

    return pl.pallas_call(
        body, name=name, grid=(t // tm,),
        in_specs=[pl.BlockSpec((tm, LANES), lambda i: (i, 0)), pl.BlockSpec((16, tm), lambda i: (0, i)),
                  pl.BlockSpec((1, LANES), lambda i: (0, 0)), pl.BlockSpec((16, 1), lambda i: (0, 0))],
        out_specs=[pl.BlockSpec((nh, tm, LANES), lambda i: (0, i, 0)),
                   pl.BlockSpec((nh, cpt, 8, CHUNK), lambda i: (0, i, 0, 0))],
        out_shape=[jax.ShapeDtypeStruct((nh, t, LANES), F32),
                   jax.ShapeDtypeStruct((nh, t // CHUNK, 8, CHUNK), F32)],
        compiler_params=_params(),
    )(graw, graw_t, bias_row, bias_col)


def _gates_bwd(gstat, colf, name):
    nh, t, _ = gstat.shape
    tm = _pick(t, 640, 128)
    cpt = tm // CHUNK
    nt = t // tm
    nc = t // CHUNK

    def body(gs_ref, nx_ref, colf_ref, dg_ref, db_ref):
        i = pl.program_id(0)
        ri = lax.broadcasted_iota(jnp.int32, (CHUNK, CHUNK), 0)
        ci = lax.broadcasted_iota(jnp.int32, (CHUNK, CHUNK), 1)
        upper = (ri <= ci).astype(F32)
        lane = lax.broadcasted_iota(jnp.int32, (CHUNK, LANES), 1)
        total = jnp.zeros((1, LANES), F32)
        for c in range(cpt):
            sl = slice(c * CHUNK, (c + 1) * CHUNK)
            rows = i * tm + c * CHUNK + lax.broadcasted_iota(jnp.int32, (CHUNK, 1), 0)
            live = rows >= PAD_ROWS
            acc = jnp.zeros((CHUNK, LANES), F32)
            for h in range(nh):
                blk = gs_ref[h, sl, :]
                rev = jnp.dot(upper, blk, precision=HIGHEST, preferred_element_type=F32)
                if c + 1 < cpt:
                    carry = gs_ref[h, (c + 1) * CHUNK:(c + 1) * CHUNK + 1, 2:3]
                else:
                    carry = jnp.where(i == nt - 1, 0.0, nx_ref[h, 0:1, 2:3])
                dlogf = rev[:, 0:1] + carry
                pre = colf_ref[h, sl, 2:3]
                dgf = jnp.where(live, dlogf * (1.0 - _sigmoid(pre)), 0.0)
                dgi = jnp.where(live, blk[:, 1:2], 0.0)
                acc = acc + jnp.where(lane == h, dgi, 0.0) + jnp.where(lane == nh + h, dgf, 0.0)
            dg_ref[sl, :] = acc
            total = total + jnp.sum(acc, axis=0, keepdims=True)

        @pl.when(i == 0)
        def _():
            db_ref[...] = jnp.zeros_like(db_ref)

        db_ref[...] += jnp.broadcast_to(total, db_ref.shape)

    return pl.pallas_call(
        body, name=name, grid=(nt,),
        in_specs=[pl.BlockSpec((nh, tm, LANES), lambda i: (0, i, 0)),
                  pl.BlockSpec((nh, CHUNK, LANES), lambda i: (0, jnp.minimum((i + 1) * cpt, nc - 1), 0)),
                  pl.BlockSpec((nh, tm, LANES), lambda i: (0, i, 0))],
        out_specs=[pl.BlockSpec((tm, LANES), lambda i: (i, 0)), pl.BlockSpec((8, LANES), lambda i: (0, 0))],
        out_shape=[jax.ShapeDtypeStruct((t, LANES), F32), jax.ShapeDtypeStruct((8, LANES), F32)],
        compiler_params=_params(),
    )(gstat, gstat, colf)


NT_DIMS = (((1,), (1,)), ((), ()))
TN_DIMS = (((0,), (0,)), ((), ()))


def _dot(a, b):
    return jnp.dot(a.astype(BF16), b.astype(BF16), preferred_element_type=F32)


def _dot_nt(a, b):
    return lax.dot_general(a.astype(BF16), b.astype(BF16), NT_DIMS, preferred_element_type=F32)


def _dot_tn(a, b):
    return lax.dot_general(a.astype(BF16), b.astype(BF16), TN_DIMS, preferred_element_type=F32)


def _chunk_gates(colf_ref, rowf_ref, m_prev):
    b = colf_ref[:, 0:1]
    rcol = colf_ref[:, 1:2]
    rrow = rowf_ref[0:1, :]
    ri = lax.broadcasted_iota(jnp.int32, (CHUNK, CHUNK), 0)
    ci = lax.broadcasted_iota(jnp.int32, (CHUNK, CHUNK), 1)
    log_d = jnp.where(ri >= ci, b + rrow, NEG)
    m_row = jnp.maximum(b + m_prev, jnp.max(log_d, axis=-1, keepdims=True))
    dmat = jnp.exp(log_d - m_row)
    inter = jnp.exp(b + m_prev - m_row)
    b_last = b[CHUNK - 1:CHUNK, :]
    log_w = rcol + b_last
    m_new = jnp.maximum(b_last + m_prev, jnp.max(log_w, axis=0, keepdims=True))
    decay = jnp.exp(b_last + m_prev - m_new)
    w = jnp.exp(log_w - m_new)
    return m_row, dmat, inter, m_new, decay, w


def _mlstm_fwd(p1, colf, rowf, head_w, dk, dv, name):
    t = p1.shape[0]
    nh = N_HEADS
    nc = t // CHUNK
    hw = 2 * dk + 3 * dv
    cps = CHUNKS_PER_STEP if nc % CHUNKS_PER_STEP == 0 else 1
    rows_per_step = cps * CHUNK

    def body(p_ref, colf_ref, rowf_ref, w_ref, hm_ref, hh_ref, stat_ref, cs_ref, ns_ref, c_ref, n_ref, m_ref):
        c = pl.program_id(0)

        @pl.when(c == 0)
        def _():
            c_ref[...] = jnp.zeros_like(c_ref)
            n_ref[...] = jnp.zeros_like(n_ref)
            m_ref[...] = jnp.zeros_like(m_ref)

        for cc in range(cps):
            rows = pl.ds(cc * CHUNK, CHUNK)
            for h in range(nh):
                cols = pl.ds(h * dv, dv)
                head(p_ref.at[rows, pl.ds(h * hw, hw)], colf_ref.at[h, rows], rowf_ref.at[h, cc], w_ref.at[:, cols],
                     hm_ref.at[rows, cols], hh_ref.at[rows, cols], stat_ref.at[h, rows], cs_ref.at[h, cc],
                     ns_ref.at[h, cc], c_ref.at[h], n_ref.at[h], m_ref.at[h])

    def head(p_ref, colf_ref, rowf_ref, w_ref, hm_ref, hh_ref, stat_ref, cs_ref, ns_ref, c_ref, n_ref, m_ref):
        m_prev = m_ref[...]
        n_prev = n_ref[...]
        c_prev = c_ref[...]
        cs_ref[...] = c_prev.astype(BF16)
        sub = lax.broadcasted_iota(jnp.int32, (8, dk), 0)
        ns_ref[...] = jnp.where(sub == 0, n_prev, jnp.where(sub == 1, m_prev, 0.0))

        q = p_ref[:, 0:dk]
        k = p_ref[:, dk:2 * dk]
        v = p_ref[:, 2 * dk:2 * dk + dv]
        o = p_ref[:, 2 * dk + dv:2 * dk + 2 * dv].astype(F32)
        z = p_ref[:, 2 * dk + 2 * dv:2 * dk + 3 * dv].astype(F32)
        m_row, dmat, inter, m_new, decay, w = _chunk_gates(colf_ref, rowf_ref, m_prev)
        s = _dot_nt(q, k) * dmat
        num = _dot(s, v) + inter * _dot(q, c_prev)
        den = jnp.sum(s, axis=-1, keepdims=True) + inter * jnp.sum(q.astype(F32) * n_prev, axis=-1, keepdims=True)
        denom = jnp.maximum(jnp.abs(den), jnp.exp(-m_row))
        hh = num * (1.0 / denom)
        hh_ref[...] = hh
        lane = lax.broadcasted_iota(jnp.int32, (CHUNK, LANES), 1)
        stat_ref[...] = jnp.where(lane == 0, den, 0.0)

        c_ref[...] = decay * c_prev + _dot_tn(k, w * v.astype(F32))
        n_ref[...] = decay * n_prev + jnp.sum(w * k.astype(F32), axis=0, keepdims=True)
        m_ref[...] = m_new

        r = lax.rsqrt(jnp.mean(hh * hh, axis=-1, keepdims=True) + RMS_EPS)
        hn = (hh * r) * w_ref[...]
        hm_ref[...] = (hn * _sigmoid(o) * (z * _sigmoid(z))).astype(BF16)

    return pl.pallas_call(
        body, name=name, grid=(nc // cps,),
        in_specs=[pl.BlockSpec((rows_per_step, nh * hw), lambda c: (c, 0)),
                  pl.BlockSpec((nh, rows_per_step, LANES), lambda c: (0, c, 0)),
                  pl.BlockSpec((nh, cps, 8, CHUNK), lambda c: (0, c, 0, 0)),
                  pl.BlockSpec((1, nh * dv), lambda c: (0, 0))],
        out_specs=[pl.BlockSpec((rows_per_step, nh * dv), lambda c: (c, 0)),
                   pl.BlockSpec((rows_per_step, nh * dv), lambda c: (c, 0)),
                   pl.BlockSpec((nh, rows_per_step, LANES), lambda c: (0, c, 0)),
                   pl.BlockSpec((nh, cps, dk, dv), lambda c: (0, c, 0, 0)),
                   pl.BlockSpec((nh, cps, 8, dk), lambda c: (0, c, 0, 0))],
        out_shape=[jax.ShapeDtypeStruct((t, nh * dv), BF16), jax.ShapeDtypeStruct((t, nh * dv), F32),
                   jax.ShapeDtypeStruct((nh, t, LANES), F32),
                   jax.ShapeDtypeStruct((nh, nc, dk, dv), BF16), jax.ShapeDtypeStruct((nh, nc, 8, dk), F32)],
        scratch_shapes=[pltpu.VMEM((nh, dk, dv), F32), pltpu.VMEM((nh, 1, dk), F32), pltpu.VMEM((nh, 1, 1), F32)],
        compiler_params=_params(),
    )(p1, colf, rowf, head_w)


def _mlstm_bwd(p1, colf, rowf, head_w, hh, stat, csave, nsave, dhm, dk, dv, name):
    t = p1.shape[0]
    nh = N_HEADS
    nc = t // CHUNK
    hw = 2 * dk + 3 * dv
    cps = CHUNKS_PER_STEP if nc % CHUNKS_PER_STEP == 0 else 1
    rows_per_step = cps * CHUNK

    def body(p_ref, colf_ref, rowf_ref, w_ref, hh_ref, stat_ref, cs_ref, ns_ref, dhm_ref,
             dp_ref, gs_ref, dw_ref, dc_ref, dn_ref):
        c = pl.program_id(0)

        @pl.when(c == 0)
        def _():
            dc_ref[...] = jnp.zeros_like(dc_ref)
            dn_ref[...] = jnp.zeros_like(dn_ref)
            dw_ref[...] = jnp.zeros_like(dw_ref)

        for cc in reversed(range(cps)):
            rows = pl.ds(cc * CHUNK, CHUNK)
            for h in range(nh):
                cols = pl.ds(h * dv, dv)
                head(p_ref.at[rows, pl.ds(h * hw, hw)], colf_ref.at[h, rows], rowf_ref.at[h, cc], w_ref.at[:, cols],
                     hh_ref.at[rows, cols], stat_ref.at[h, rows], cs_ref.at[h, cc], ns_ref.at[h, cc],
                     dhm_ref.at[rows, cols], dp_ref.at[rows, pl.ds(h * hw, hw)], gs_ref.at[h, rows],
                     dw_ref.at[h], dc_ref.at[h], dn_ref.at[h])

    def head(p_ref, colf_ref, rowf_ref, w_ref, hh_ref, stat_ref, cs_ref, ns_ref, dhm_ref,
             dp_ref, gs_ref, dw_ref, dc_ref, dn_ref):
        q = p_ref[:, 0:dk]
        k = p_ref[:, dk:2 * dk]
        v = p_ref[:, 2 * dk:2 * dk + dv]
        o = p_ref[:, 2 * dk + dv:2 * dk + 2 * dv].astype(F32)
        z = p_ref[:, 2 * dk + 2 * dv:2 * dk + 3 * dv].astype(F32)
        qf = q.astype(F32)
        kf = k.astype(F32)
        n_prev = ns_ref[0:1, :]
        m_prev = ns_ref[1:2, 0:1]
        c_prev = cs_ref[...]
        m_row, dmat, inter, m_new, decay, w = _chunk_gates(colf_ref, rowf_ref, m_prev)

        hh = hh_ref[...]
        dhm_v = dhm_ref[...].astype(F32)
        so = _sigmoid(o)
        sg = _sigmoid(z)
        sz = z * sg
        r = lax.rsqrt(jnp.mean(hh * hh, axis=-1, keepdims=True) + RMS_EPS)
        hn = (hh * r) * w_ref[...]
        dhn = dhm_v * so * sz
        dp_ref[:, 2 * dk + dv:2 * dk + 2 * dv] = (dhm_v * hn * sz * (so * (1.0 - so))).astype(BF16)
        dp_ref[:, 2 * dk + 2 * dv:2 * dk + 3 * dv] = (dhm_v * hn * so * (sg * (1.0 + z * (1.0 - sg)))).astype(BF16)
        dw_ref[...] += jnp.broadcast_to(jnp.sum(dhn * (hh * r), axis=0, keepdims=True), dw_ref.shape)
        gwn = dhn * w_ref[...]
        dhh = r * gwn - hh * ((r * r * r) * jnp.mean(gwn * hh, axis=-1, keepdims=True))

        den = stat_ref[:, 0:1]
        floor = jnp.exp(-m_row)
        denom = jnp.maximum(jnp.abs(den), floor)
        inv = 1.0 / denom
        dnum = dhh * inv
        hdot = jnp.sum(dhh * hh, axis=-1, keepdims=True)
        dden = jnp.where(jnp.abs(den) > floor, -(hdot * inv) * jnp.sign(den), 0.0)
        s = _dot_nt(q, k) * dmat
        dqk = (_dot_nt(dnum, v) + dden) * dmat
        dc_new = dc_ref[...]
        dn_new = dn_ref[...]
        idd = inter * dden
        dq = _dot(dqk, k) + inter * _dot_nt(dnum, c_prev) + idd * n_prev
        dkv = _dot_tn(dqk, q) + w * (_dot_nt(v, dc_new) + dn_new)
        dvv = _dot_tn(s, dnum) + w * _dot(k, dc_new)
        dp_ref[:, 0:dk] = dq.astype(BF16)
        dp_ref[:, dk:2 * dk] = dkv.astype(BF16)
        dp_ref[:, 2 * dk:2 * dk + dv] = dvv.astype(BF16)
        qdq = jnp.sum(qf * dq, axis=-1, keepdims=True)
        kdk = jnp.sum(kf * dkv, axis=-1, keepdims=True)
        dc_prev = decay * dc_new + _dot_tn(inter * qf, dnum)
        dn_prev = decay * dn_new + jnp.sum(idd * qf, axis=0, keepdims=True)
        dc_ref[...] = dc_prev
        dn_ref[...] = dn_prev
        cross = (jnp.sum(jnp.sum(c_prev.astype(F32) * dc_prev, axis=-1, keepdims=True), axis=0, keepdims=True)
                 + jnp.sum(n_prev * dn_prev, axis=-1, keepdims=True))
        lane = lax.broadcasted_iota(jnp.int32, (CHUNK, LANES), 1)
        gs_ref[...] = jnp.where(lane == 0, qdq - kdk, jnp.where(lane == 1, kdk, jnp.where(lane == 2, cross, 0.0)))

    ns = nc // cps
    rc = lambda c: (ns - 1 - c, 0)
    rc3 = lambda c: (0, ns - 1 - c, 0)
    rc4 = lambda c: (0, ns - 1 - c, 0, 0)
    return pl.pallas_call(
        body, name=name, grid=(ns,),
        in_specs=[pl.BlockSpec((rows_per_step, nh * hw), rc),
                  pl.BlockSpec((nh, rows_per_step, LANES), rc3),
                  pl.BlockSpec((nh, cps, 8, CHUNK), rc4),
                  pl.BlockSpec((1, nh * dv), lambda c: (0, 0)),
                  pl.BlockSpec((rows_per_step, nh * dv), rc),
                  pl.BlockSpec((nh, rows_per_step, LANES), rc3),
                  pl.BlockSpec((nh, cps, dk, dv), rc4),
                  pl.BlockSpec((nh, cps, 8, dk), rc4),
                  pl.BlockSpec((rows_per_step, nh * dv), rc)],
        out_specs=[pl.BlockSpec((rows_per_step, nh * hw), rc),
                   pl.BlockSpec((nh, rows_per_step, LANES), rc3),
                   pl.BlockSpec((nh, 8, dv), lambda c: (0, 0, 0))],
        out_shape=[jax.ShapeDtypeStruct((t, nh * hw), BF16), jax.ShapeDtypeStruct((nh, t, LANES), F32),
                   jax.ShapeDtypeStruct((nh, 8, dv), F32)],
        scratch_shapes=[pltpu.VMEM((nh, dk, dv), F32), pltpu.VMEM((nh, 1, dk), F32)],
        compiler_params=_params(),
    )(p1, colf, rowf, head_w, hh, stat, csave, nsave, dhm)


def _position():
    return lax.axis_index("x"), lax.axis_index("y"), lax.axis_index("c")


class _Exchange:
    def __init__(self, ins, out_shapes, start, mid, finish):
        n = len(ins)
        self.ins, self.out_shapes = list(ins), list(out_shapes)
        self.start, self.mid, self.finish = start, mid, finish
        self.scratch = [pltpu.SemaphoreType.DMA((n, 7)), pltpu.SemaphoreType.DMA((n, 7)),
                        pltpu.SemaphoreType.DMA((n,))]


def _gather_exchange(shards):
    n = len(shards)

    def plan(ins, outs, sems):
        send_sems, recv_sems, local_sems = sems
        x, y, c = _position()
        me, sibling = (x, y, c), (x, y, 1 - c)
        chips = [(1 - x, y), (x, 1 - y), (1 - x, 1 - y)]

        def copy(a, k, block, to, src=None):
            px, py, pc = block
            dst = outs[a].at[4 * px + 2 * py + pc]
            return pltpu.make_async_remote_copy(
                src_ref=dst if src is None else src, dst_ref=dst,
                send_sem=send_sems.at[a, k], recv_sem=recv_sems.at[a, k],
                device_id=to, device_id_type=MESH)

        def mine():
            return [pltpu.make_async_copy(ins[a], outs[a].at[4 * x + 2 * y + c], local_sems.at[a])
                    for a in range(n)]

        def first():
            out = []
            for a in range(n):
                out.append(copy(a, 0, me, sibling, src=ins[a]))
                out += [copy(a, 1 + j, me, (*chip, c), src=ins[a]) for j, chip in enumerate(chips)]
            return out

        def ici_in():
            return [copy(a, 1 + j, (*chip, c), me) for j, chip in enumerate(chips) for a in range(n)]

        def passed():
            return [copy(a, 4 + j, (*chip, c), sibling) for j, chip in enumerate(chips) for a in range(n)]

        def d2d_in():
            return ([copy(a, 0, sibling, me) for a in range(n)]
                    + [copy(a, 4 + j, (*chip, 1 - c), me) for j, chip in enumerate(chips) for a in range(n)])

        return mine, first, ici_in, passed, d2d_in

    def start(ins, outs, sems):
        mine, first, _, _, _ = plan(ins, outs, sems)
        for cp in mine() + first():
            cp.start()

    def mid(ins, outs, sems):
        _, _, ici_in, passed, _ = plan(ins, outs, sems)
        for arrived, onward in zip(ici_in(), passed()):
            arrived.wait_recv()
            onward.start()

    def finish(ins, outs, sems):
        mine, first, _, passed, d2d_in = plan(ins, outs, sems)
        for cp in d2d_in():
            cp.wait_recv()
        for cp in first() + passed():
            cp.wait_send()
        for cp in mine():
            cp.wait()

    shapes = [jax.ShapeDtypeStruct((N_DEV,) + s.shape, s.dtype) for s in shards]
    return _Exchange(shards, shapes, start, mid, finish)


def _scatter_exchange(fulls):
    n = len(fulls)

    def plan(ins, outs, sems):
        send_sems, recv_sems, local_sems = sems
        x, y, c = _position()
        my_slot = 4 * x + 2 * y + c

        def mine():
            return [pltpu.make_async_copy(ins[a].at[my_slot], outs[a].at[my_slot], local_sems.at[a])
                    for a in range(n)]

        def remote(arriving):
            out = []
            for kk in (1, 2, 4, 6, 3, 5, 7):
                kx, ky, kc = (kk >> 2) & 1, (kk >> 1) & 1, kk & 1
                px = 1 - x if kx else x
                py = 1 - y if ky else y
                pc = 1 - c if kc else c
                peer_slot = 4 * px + 2 * py + pc
                for a in range(n):
                    out.append(pltpu.make_async_remote_copy(
                        src_ref=ins[a].at[peer_slot], dst_ref=outs[a].at[peer_slot if arriving else my_slot],
                        send_sem=send_sems.at[a, kk - 1], recv_sem=recv_sems.at[a, kk - 1],
                        device_id=(px, py, pc), device_id_type=MESH))
            return out

        return mine, remote

    def start(ins, outs, sems):
        mine, remote = plan(ins, outs, sems)
        for cp in mine() + remote(False):
            cp.start()

    def finish(ins, outs, sems):
        mine, remote = plan(ins, outs, sems)
        for cp in remote(True):
            cp.wait_recv()
        for cp in remote(False):
            cp.wait_send()
        for cp in mine():
            cp.wait()

    shapes = [jax.ShapeDtypeStruct(f.shape, f.dtype) for f in fulls]
    return _Exchange(fulls, shapes, start, None, finish)


def _run_exchange(ex, name):
    n_in, n_out = len(ex.ins), len(ex.out_shapes)

    def body(*refs):
        ins, outs, sems = refs[:n_in], refs[n_in:n_in + n_out], refs[n_in + n_out:]
        ex.start(ins, outs, sems)
        if ex.mid is not None:
            ex.mid(ins, outs, sems)
        ex.finish(ins, outs, sems)

    any_spec = pl.BlockSpec(memory_space=pl.ANY)
    return pl.pallas_call(
        body, name=name,
        in_specs=[any_spec] * n_in, out_specs=[any_spec] * n_out,
        out_shape=ex.out_shapes, scratch_shapes=ex.scratch,
    )(*ex.ins)


def _adamw_math(w, g, m, v):
    m = ADAM_B1 * m + (1.0 - ADAM_B1) * g
    v = ADAM_B2 * v + (1.0 - ADAM_B2) * (g * g)
    m_hat = m / (1.0 - ADAM_B1 ** ADAM_STEP)
    v_hat = v / (1.0 - ADAM_B2 ** ADAM_STEP)
    delta = -ADAM_LR * (m_hat / (jnp.sqrt(v_hat) + ADAM_EPS) + ADAM_WD * w)
    return delta, m, v


def _adamw_sharded(parts, w, m, v, name):
    _, r, c = parts.shape
    tr = _pick(r, 128, 8)

    def body(p_ref, w_ref, m_ref, v_ref, g_ref, d_ref, nm_ref, nv_ref):
        g = p_ref[0].astype(F32)
        for s in range(1, N_DEV):
            g = g + p_ref[s].astype(F32)
        delta, m_new, v_new = _adamw_math(w_ref[...], g, m_ref[...], v_ref[...])
        g_ref[...] = g
        d_ref[...] = delta
        nm_ref[...] = m_new
        nv_ref[...] = v_new

    blk = pl.BlockSpec((tr, c), lambda i: (i, 0))
    return pl.pallas_call(
        body, name=name, grid=(r // tr,),
        in_specs=[pl.BlockSpec((N_DEV, tr, c), lambda i: (0, i, 0)), blk, blk, blk],
        out_specs=[blk] * 4,
        out_shape=[jax.ShapeDtypeStruct((r, c), F32)] * 4,
        compiler_params=_params(),
    )(parts, w, m, v)


def _sum_devices(parts, name):
    _, r, c = parts.shape

    def body(p_ref, o_ref):
        g = p_ref[0]
        for s in range(1, N_DEV):
            g = g + p_ref[s]
        o_ref[...] = g

    return pl.pallas_call(
        body, name=name, out_shape=jax.ShapeDtypeStruct((r, c), F32), compiler_params=_params(),
    )(parts)


def _adamw_small(gs, ws, ms, vs, name):
    n = len(gs)

    def body(*refs):
        g_refs, w_refs, m_refs, v_refs = refs[:n], refs[n:2 * n], refs[2 * n:3 * n], refs[3 * n:4 * n]
        d_refs, nm_refs, nv_refs = refs[4 * n:5 * n], refs[5 * n:6 * n], refs[6 * n:7 * n]
        for a in range(n):
            delta, m_new, v_new = _adamw_math(w_refs[a][...], g_refs[a][...], m_refs[a][...], v_refs[a][...])
            d_refs[a][...] = delta
            nm_refs[a][...] = m_new
            nv_refs[a][...] = v_new

    shapes = [jax.ShapeDtypeStruct(w.shape, F32) for w in ws]
    outs = pl.pallas_call(
        body, name=name, out_shape=shapes * 3, compiler_params=_params(),
    )(*gs, *ws, *ms, *vs)
    return outs[:n], outs[n:2 * n], outs[2 * n:]


def _pad_rows(a, rows):
    return jnp.pad(a, ((0, rows - a.shape[0]), (0, 0)))


def kernel(x, meta_tokens, norm_w, conv_in_w, conv_w, conv_out_w, mlstm_in_w, mlstm_gate_b, mlstm_head_norm_w, mlstm_out_w, final_norm_w, loss_target, m_meta_tokens, m_norm_w, m_conv_in_w, m_conv_w, m_conv_out_w, m_mlstm_in_w, m_mlstm_gate_b, m_mlstm_head_norm_w, m_mlstm_out_w, m_final_norm_w, v_meta_tokens, v_norm_w, v_conv_in_w, v_conv_w, v_conv_out_w, v_mlstm_in_w, v_mlstm_gate_b, v_mlstm_head_norm_w, v_mlstm_out_w, v_final_norm_w):
    seq, d = x.shape[1], x.shape[2]
    t = seq + LEAD
    e = 2 * d
    ec = e // 2
    nh = N_HEADS
    dv = e // nh
    dk = dv // 2
    qk = nh * dk
    hw = 2 * dk + 3 * dv
    n_in = 2 * qk + 3 * e + 2 * nh
    n_in_s = n_in // N_DEV
    me = 4 * lax.axis_index("x") + 2 * lax.axis_index("y") + lax.axis_index("c")
    tm = _pick(t, 832, 16)
    tkw = _pick(t, 2080, 16)

    small = jnp.concatenate([
        meta_tokens,
        _pad_rows(conv_w[0].reshape(3 * (e // N_DEV) // LANES, LANES), 8),
        _pad_rows(mlstm_head_norm_w[0].reshape((e // N_DEV) // LANES, LANES), 8),
    ], axis=0) if d // N_DEV == LANES else None
    assert small is not None, "the packed small-weight block assumes d_model / 8 == 128"
    w_ci, w_co, small_g = _run_exchange(
        _gather_exchange([conv_in_w[0].astype(BF16), conv_out_w[0].astype(BF16), small]), "gather_weights")
    w_co = w_co.reshape(e, d)
    meta_full = jnp.transpose(small_g[:, 0:N_META, :], (1, 0, 2)).reshape(N_META, d)
    cw_rows = 3 * (e // N_DEV) // LANES
    conv_w_full = jnp.transpose(
        small_g[:, N_META:N_META + cw_rows, :].reshape(N_DEV, 3, e // N_DEV), (1, 0, 2)).reshape(3, e)
    hn_rows = (e // N_DEV) // LANES
    head_w_full = small_g[:, N_META + 8:N_META + 8 + hn_rows, :].reshape(1, e)

    h0 = jnp.concatenate([jnp.zeros((PAD_ROWS, d), F32), meta_full, x[0]], axis=0)
    tgt = loss_target[0]

    ci_map = lambda blk: 2 * (blk % 4) + blk // 4
    u0 = _rms_fwd(h0, norm_w[0:1], "rms0")
    p0, w_mi, w_mo = _matmul(
        u0, w_ci, form="nn", m=t, n=8 * ec, kdim=d, tm=tm, tn=ec, tk=d, out_dtype=BF16, name="conv_in",
        b_spec=pl.BlockSpec((None, d, ec), lambda i, j, k: (ci_map(j), 0, 0)),
        exchange=_gather_exchange([mlstm_in_w[0].astype(BF16), mlstm_out_w[0].astype(BF16)]))
    w_mo = w_mo.reshape(e, d)
    g0, y0 = _conv_gate_fwd(p0, conv_w_full, "conv_gate")
    h1 = _matmul(g0, w_co, form="nn", m=t, n=d, kdim=e, tm=tm, tn=d, tk=e, out_dtype=F32, name="conv_out",
                 residual=h0)

    w_nat = jnp.transpose(w_mi, (1, 0, 2)).reshape(d, n_in)
    wq = (w_nat[:, 0:qk] * (dk ** -0.5)).astype(BF16).reshape(d, nh, dk)
    wk = w_nat[:, qk:2 * qk].reshape(d, nh, dk)
    wv = w_nat[:, 2 * qk:2 * qk + e].reshape(d, nh, dv)
    wo = w_nat[:, 2 * qk + e:2 * qk + 2 * e].reshape(d, nh, dv)
    wz = w_nat[:, 2 * qk + 2 * e:2 * qk + 3 * e].reshape(d, nh, dv)
    w_p1 = jnp.concatenate([wq, wk, wv, wo, wz], axis=2).reshape(d, nh * hw)
    w_gate = jnp.pad(w_nat[:, 2 * qk + 3 * e:], ((0, 0), (0, LANES - 2 * nh)))
    w_gate_t = jnp.pad(w_nat[:, 2 * qk + 3 * e:].T, ((0, 16 - 2 * nh), (0, 0)))
    bias_row = jnp.pad(mlstm_gate_b, ((0, 0), (0, LANES - 2 * nh)))
    bias_col = jnp.pad(mlstm_gate_b.T, ((0, 16 - 2 * nh), (0, 0)))

    u1 = _rms_fwd(h1, norm_w[1:2], "rms1")
    p1 = _matmul(u1, w_p1, form="nn", m=t, n=nh * hw, kdim=d, tm=tm, tn=_pick(nh * hw, 1024, LANES), tk=d,
                 out_dtype=BF16, name="mlstm_in")
    graw = _matmul(u1, w_gate, form="nn", m=t, n=LANES, kdim=d, tm=tm, tn=LANES, tk=d, out_dtype=F32,
                   name="gates_col")
    graw_t = _matmul(w_gate_t, u1, form="nt", m=16, n=t, kdim=d, tm=16, tn=_pick(t, 1664, LANES), tk=d,
                     out_dtype=F32, name="gates_row")
    colf, rowf = _gates_fwd(graw, graw_t, bias_row, bias_col, "gates_fwd")
    hm, hh, stat, csave, nsave = _mlstm_fwd(p1, colf, rowf, head_w_full, dk, dv, "mlstm_fwd")
    h2 = _matmul(hm, w_mo, form="nn", m=t, n=d, kdim=e, tm=tm, tn=d, tk=e, out_dtype=F32, name="mlstm_out",
                 residual=h1)

    dh2, dwf, loss_part = _final_loss(h2, final_norm_w.reshape(1, d), tgt, "final_loss")

    dhm = _matmul(dh2, w_mo, form="nt", m=t, n=e, kdim=d, tm=tm, tn=_pick(e, 1024, LANES), tk=d, out_dtype=BF16,
                  name="mlstm_out_dx")
    g_mo = _matmul(hm, dh2, form="tn", m=e, n=d, kdim=t, tm=_pick(e, 1024, LANES), tn=d, tk=tkw, out_dtype=BF16,
                   name="mlstm_out_dw")
    dp1, gstat, dhead = _mlstm_bwd(p1, colf, rowf, head_w_full, hh, stat, csave, nsave, dhm, dk, dv, "mlstm_bwd")
    dgates, dbias = _gates_bwd(gstat, colf, "gates_bwd")
    du1 = _matmul(dp1, w_p1, form="nt", m=t, n=d, kdim=nh * hw, tm=tm, tn=d, tk=_pick(nh * hw, 2048, LANES),
                  out_dtype=F32, name="mlstm_in_dx")
    du1 = _matmul(dgates, w_gate, form="nt", m=t, n=d, kdim=LANES, tm=tm, tn=d, tk=LANES, out_dtype=F32,
                  name="gates_dx", residual=du1)
    g_p1 = _matmul(u1, dp1, form="tn", m=d, n=nh * hw, kdim=t, tm=d, tn=_pick(nh * hw, 1024, LANES), tk=tkw,
                   out_dtype=BF16, name="mlstm_in_dw")
    g_gate = _matmul(u1, dgates, form="tn", m=d, n=LANES, kdim=t, tm=d, tn=LANES, tk=tkw, out_dtype=F32,
                     name="gates_dw")
    dh1, dnw1 = _rms_bwd(h1, norm_w[1:2], du1, dh2, "rms1_bwd")

    gp = g_p1.reshape(d, nh, hw)
    gq = (gp[:, :, 0:dk] * (dk ** -0.5)).astype(BF16).reshape(d, qk)
    gk = gp[:, :, dk:2 * dk].reshape(d, qk)
    gv = gp[:, :, 2 * dk:2 * dk + dv].reshape(d, e)
    go = gp[:, :, 2 * dk + dv:2 * dk + 2 * dv].reshape(d, e)
    gz = gp[:, :, 2 * dk + 2 * dv:].reshape(d, e)
    g_mi = jnp.concatenate([gq, gk, gv, go, gz, g_gate[:, 0:2 * nh].astype(BF16)], axis=1)
    g_mi = jnp.transpose(g_mi.reshape(d, N_DEV, n_in_s), (1, 0, 2))

    dg0 =_matmul(dh1, w_co, form="nt", m=t, n=e, kdim=d, tm=tm, tn=_pick(e, 1024, LANES), tk=d, out_dtype=BF16,
                  name="conv_out_dx")
    g_co = _matmul(g0, dh1, form="tn", m=e, n=d, kdim=t, tm=_pick(e, 1024, LANES), tn=d, tk=tkw, out_dtype=BF16,
                   name="conv_out_dw")
    dp0, dconv = _conv_gate_bwd(p0, y0, dg0, conv_w_full, "conv_gate_bwd")
    g_ci, r_co, r_mi, r_mo = _matmul(
        u0, dp0, form="tn", m=d, n=8 * ec, kdim=t, tm=d, tn=ec, tk=tkw, out_dtype=BF16, name="conv_in_dw",
        out_shape=(N_DEV, d, ec), out_spec=pl.BlockSpec((None, d, ec), lambda i, j, k: (ci_map(j), 0, 0)),
        exchange=_scatter_exchange([g_co.reshape(N_DEV, e // N_DEV, d), g_mi, g_mo.reshape(N_DEV, e // N_DEV, d)]))
    du0, r_ci = _matmul(
        dp0, w_ci, form="nt", m=t, n=d, kdim=8 * ec, tm=tm, tn=d, tk=ec, out_dtype=F32, name="conv_in_dx",
        b_spec=pl.BlockSpec((None, d, ec), lambda i, j, k: (ci_map(k), 0, 0)),
        exchange=_scatter_exchange([g_ci]))
    grad_x, dmeta, dnw0 = _rms_bwd_first(h0, norm_w[0:1], du0, dh1, "rms0_bwd")
    grad_x = grad_x[None]

    row8 = lax.broadcasted_iota(jnp.int32, (8, 1), 0)
    loss_wide = jnp.pad(loss_part, ((0, 0), (0, d - LANES)))
    payload = jnp.concatenate([
        jnp.where(row8 == 0, dnw0, jnp.where(row8 == 1, dnw1, 0.0)),
        jnp.where(row8 == 0, dwf, jnp.where(row8 == 1, loss_wide, 0.0)),
        jnp.where(row8 == 0, jnp.pad(dbias, ((0, 0), (0, d - LANES))), 0.0),
        dmeta,
        _pad_rows(dconv[0:3].reshape(3 * e // d, d), 8),
        _pad_rows(dhead[:, 0, :].reshape(e // d, d), 8),
    ], axis=0)
    (payload_g,) = _run_exchange(_gather_exchange([payload]), "gather_small_grads")
    tot = _sum_devices(payload_g, "sum_small_grads")

    loss = tot[9, 0]
    g_norm = tot[0:2]
    g_final = tot[8]
    g_gate_b = tot[16:17, 0:2 * nh]
    g_meta = lax.dynamic_slice(tot[24:24 + N_META], (0, me * (d // N_DEV)), (N_META, d // N_DEV))
    g_conv_w = lax.dynamic_slice(tot[40:40 + 3 * e // d].reshape(3, e), (0, me * (e // N_DEV)), (3, e // N_DEV))
    g_head = lax.dynamic_slice(tot[48:48 + e // d].reshape(1, e), (0, me * (e // N_DEV)), (1, e // N_DEV))

    g1, d1, nm1, nv1 = _adamw_sharded(r_ci, conv_in_w[0], m_conv_in_w[0], v_conv_in_w[0], "adamw_conv_in")
    g2, d2, nm2, nv2 = _adamw_sharded(r_co, conv_out_w[0], m_conv_out_w[0], v_conv_out_w[0], "adamw_conv_out")
    g3, d3, nm3, nv3 = _adamw_sharded(r_mi, mlstm_in_w[0], m_mlstm_in_w[0], v_mlstm_in_w[0], "adamw_mlstm_in")
    g4, d4, nm4, nv4 = _adamw_sharded(r_mo, mlstm_out_w[0], m_mlstm_out_w[0], v_mlstm_out_w[0], "adamw_mlstm_out")

    small_g = [g_meta, g_norm, g_conv_w, g_gate_b, g_head, g_final.reshape(1, d)]
    small_w = [meta_tokens, norm_w, conv_w[0], mlstm_gate_b, mlstm_head_norm_w, final_norm_w.reshape(1, d)]
    small_m = [m_meta_tokens, m_norm_w, m_conv_w[0], m_mlstm_gate_b, m_mlstm_head_norm_w, m_final_norm_w.reshape(1, d)]
    small_v = [v_meta_tokens, v_norm_w, v_conv_w[0], v_mlstm_gate_b, v_mlstm_head_norm_w, v_final_norm_w.reshape(1, d)]
    sd, snm, snv = _adamw_small(small_g, small_w, small_m, small_v, "adamw_small")

    def order(meta, norm, cin, cw, cout, min_, gb, hn, mout, fin):
        return (meta, norm, cin[None], cw[None], cout[None], min_[None], gb, hn, mout[None], fin.reshape(d))

    grads = order(g_meta, g_norm, g1, g_conv_w, g2, g3, g_gate_b, g_head, g4, g_final)
    deltas = order(sd[0], sd[1], d1, sd[2], d2, d3, sd[3], sd[4], d4, sd[5])
    new_m = order(snm[0], snm[1], nm1, snm[2], nm2, nm3, snm[3], snm[4], nm4, snm[5])
    new_v = order(snv[0], snv[1], nv1, snv[2], nv2, nv3, snv[3], snv[4], nv4, snv[5])
    return (loss, grad_x, *grads, *deltas, *new_m, *new_v)
```

```python
import functools

import jax
import jax.numpy as jnp
from jax import lax
from jax.experimental import pallas as pl
from jax.experimental.pallas import tpu as pltpu

F32 = jnp.float32
BF16 = jnp.bfloat16
MESH = pl.DeviceIdType.MESH

N_DEV = 8
N_META = 16
N_HEADS = 4
CHUNK = 64
CHUNKS_PER_STEP = 2
LEAD = 128
PAD_ROWS = LEAD - N_META
RMS_EPS = 1e-6
NEG = -1e30
LANES = 128
VMEM_LIMIT = 48 * 1024 * 1024

ADAM_LR = 0.001
ADAM_B1 = 0.9
ADAM_B2 = 0.999
ADAM_EPS = 1e-08
ADAM_WD = 0.01
ADAM_STEP = 10

HIGHEST = lax.Precision.HIGHEST


def _pick(n, target, mult):
    best = None
    for d in range(mult, min(n, target) + 1, mult):
        if n % d == 0:
            best = d
    return n if best is None else best


def _params(**kw):
    return pltpu.CompilerParams(vmem_limit_bytes=VMEM_LIMIT, **kw)


def _sigmoid(x):
    return 0.5 * jnp.tanh(0.5 * x) + 0.5


def _call(body, *, name, grid, in_specs, out_specs, out_shape, args, scratch_shapes=(), exchange=None):
    if exchange is None:
        return pl.pallas_call(
            body, name=name, grid=grid, in_specs=list(in_specs), out_specs=out_specs, out_shape=out_shape,
            scratch_shapes=list(scratch_shapes), compiler_params=_params())(*args)
    ex = exchange
    single = not isinstance(out_shape, (list, tuple))
    shapes = [out_shape] if single else list(out_shape)
    specs = [out_specs] if single else list(out_specs)
    n_in, n_out, n_scr = len(in_specs), len(shapes), len(scratch_shapes)
    n_ex_in, n_ex_out = len(ex.ins), len(ex.out_shapes)
    steps = 1
    for size in grid:
        steps *= size

    def wrapped(*refs):
        own_in, ex_in = refs[:n_in], refs[n_in:n_in + n_ex_in]
        at = n_in + n_ex_in
        own_out, ex_out = refs[at:at + n_out], refs[at + n_out:at + n_out + n_ex_out]
        at += n_out + n_ex_out
        own_scr, ex_scr = refs[at:at + n_scr], refs[at + n_scr:]
        step = 0
        for axis, size in enumerate(grid):
            step = step * size + pl.program_id(axis)

        @pl.when(step == 0)
        def _():
            ex.start(ex_in, ex_out, ex_scr)

        if ex.mid is not None:
            @pl.when(step == (3 * steps) // 4)
            def _():
                ex.mid(ex_in, ex_out, ex_scr)

        body(*own_in, *own_out, *own_scr)

        @pl.when(step == steps - 1)
        def _():
            ex.finish(ex_in, ex_out, ex_scr)

    any_spec = pl.BlockSpec(memory_space=pl.ANY)
    res = pl.pallas_call(
        wrapped, name=name, grid=grid,
        in_specs=list(in_specs) + [any_spec] * n_ex_in, out_specs=specs + [any_spec] * n_ex_out,
        out_shape=shapes + ex.out_shapes, scratch_shapes=list(scratch_shapes) + ex.scratch,
        compiler_params=_params())(*args, *ex.ins)
    return (res[0] if single else res[:n_out]), res[n_out:]


def _matmul(a, b, *, form, m, n, kdim, tm, tn, tk, out_dtype, name,
            a_spec=None, b_spec=None, out_spec=None, out_shape=None, residual=None, exchange=None):
    ni, nj, nk = m // tm, n // tn, kdim // tk
    assert ni * tm == m and nj * tn == n and nk * tk == kdim, (name, m, n, kdim, tm, tn, tk)
    if form == "nn":
        dn = (((1,), (0,)), ((), ()))
        a_def = pl.BlockSpec((tm, tk), lambda i, j, k: (i, k))
        b_def = pl.BlockSpec((tk, tn), lambda i, j, k: (k, j))
    elif form == "nt":
        dn = (((1,), (1,)), ((), ()))
        a_def = pl.BlockSpec((tm, tk), lambda i, j, k: (i, k))
        b_def = pl.BlockSpec((tn, tk), lambda i, j, k: (j, k))
    else:
        dn = (((0,), (0,)), ((), ()))
        a_def = pl.BlockSpec((tk, tm), lambda i, j, k: (k, i))
        b_def = pl.BlockSpec((tk, tn), lambda i, j, k: (k, j))
    a_spec = a_def if a_spec is None else a_spec
    b_spec = b_def if b_spec is None else b_spec
    o_spec = pl.BlockSpec((tm, tn), lambda i, j, k: (i, j)) if out_spec is None else out_spec
    has_res = residual is not None

    def body(*refs):
        a_ref, b_ref = refs[:2]
        r_ref = refs[2] if has_res else None
        o_ref = refs[2 + has_res]

        def product():
            return lax.dot_general(a_ref[...].astype(BF16), b_ref[...].astype(BF16), dn,
                                   preferred_element_type=F32)

        def finish(acc):
            if has_res:
                acc = acc + r_ref[...].astype(F32)
            o_ref[...] = acc.astype(o_ref.dtype)

        if nk == 1:
            finish(product())
        else:
            acc_ref = refs[3 + has_res]
            k = pl.program_id(2)

            @pl.when(k == 0)
            def _():
                acc_ref[...] = jnp.zeros_like(acc_ref)

            acc_ref[...] += product()

            @pl.when(k == nk - 1)
            def _():
                finish(acc_ref[...])

    in_specs = [a_spec, b_spec]
    args = [a, b]
    if has_res:
        in_specs.append(pl.BlockSpec((tm, tn), lambda i, j, k: (i, j)))
        args.append(residual)
    return _call(
        body, name=name, grid=(ni, nj, nk), in_specs=in_specs, out_specs=o_spec,
        out_shape=jax.ShapeDtypeStruct((m, n) if out_shape is None else out_shape, out_dtype),
        scratch_shapes=[] if nk == 1 else [pltpu.VMEM((tm, tn), F32)], args=args, exchange=exchange)


def _rms_fwd(h, w, name):
    t, d = h.shape
    tm = _pick(t, 416, 16)

    def body(h_ref, w_ref, u_ref):
        x = h_ref[...]
        r = lax.rsqrt(jnp.mean(x * x, axis=-1, keepdims=True) + RMS_EPS)
        u_ref[...] = ((x * r) * w_ref[...]).astype(BF16)

    return pl.pallas_call(
        body, name=name, grid=(t // tm,),
        in_specs=[pl.BlockSpec((tm, d), lambda i: (i, 0)), pl.BlockSpec((1, d), lambda i: (0, 0))],
        out_specs=pl.BlockSpec((tm, d), lambda i: (i, 0)),
        out_shape=jax.ShapeDtypeStruct((t, d), BF16),
        compiler_params=_params(),
    )(h, w)


def _rms_bwd(h, w, du, dres, name):
    t, d = h.shape
    tm = _pick(t, 416, 16)

    def body(h_ref, w_ref, du_ref, dres_ref, dh_ref, dw_ref):
        i = pl.program_id(0)
        x = h_ref[...]
        g = du_ref[...].astype(F32)
        r = lax.rsqrt(jnp.mean(x * x, axis=-1, keepdims=True) + RMS_EPS)
        gw = g * w_ref[...]
        dot = jnp.mean(gw * x, axis=-1, keepdims=True)
        dh_ref[...] = dres_ref[...] + (r * gw - x * ((r * r * r) * dot))
        part = jnp.sum(g * (x * r), axis=0, keepdims=True)

        @pl.when(i == 0)
        def _():
            dw_ref[...] = jnp.zeros_like(dw_ref)

        dw_ref[...] += jnp.broadcast_to(part, dw_ref.shape)

    row = pl.BlockSpec((tm, d), lambda i: (i, 0))
    return pl.pallas_call(
        body, name=name, grid=(t // tm,),
        in_specs=[row, pl.BlockSpec((1, d), lambda i: (0, 0)), row, row],
        out_specs=[row, pl.BlockSpec((8, d), lambda i: (0, 0))],
        out_shape=[jax.ShapeDtypeStruct((t, d), F32), jax.ShapeDtypeStruct((8, d), F32)],
        compiler_params=_params(),
    )(h, w, du, dres)


def _rms_bwd_first(h, w, du, dres, name):
    t, d = h.shape
    tm = LEAD

    def body(h_ref, w_ref, du_ref, dres_ref, gx_ref, dmeta_ref, dw_ref):
        i = pl.program_id(0)
        x = h_ref[...]
        g = du_ref[...].astype(F32)
        r = lax.rsqrt(jnp.mean(x * x, axis=-1, keepdims=True) + RMS_EPS)
        gw = g * w_ref[...]
        dot = jnp.mean(gw * x, axis=-1, keepdims=True)
        dh = dres_ref[...] + (r * gw - x * ((r * r * r) * dot))
        part = jnp.sum(g * (x * r), axis=0, keepdims=True)

        @pl.when(i == 0)
        def _():
            dw_ref[...] = jnp.zeros_like(dw_ref)
            dmeta_ref[...] = dh[PAD_ROWS:LEAD, :]

        @pl.when(i > 0)
        def _():
            gx_ref[...] = dh

        dw_ref[...] += jnp.broadcast_to(part, dw_ref.shape)

    row = pl.BlockSpec((tm, d), lambda i: (i, 0))
    return pl.pallas_call(
        body, name=name, grid=(t // tm,),
        in_specs=[row, pl.BlockSpec((1, d), lambda i: (0, 0)), row, row],
        out_specs=[pl.BlockSpec((tm, d), lambda i: (jnp.maximum(i - 1, 0), 0)),
                   pl.BlockSpec((N_META, d), lambda i: (0, 0)), pl.BlockSpec((8, d), lambda i: (0, 0))],
        out_shape=[jax.ShapeDtypeStruct((t - LEAD, d), F32), jax.ShapeDtypeStruct((N_META, d), F32),
                   jax.ShapeDtypeStruct((8, d), F32)],
        compiler_params=_params(),
    )(h, w, du, dres)


def _final_loss(h, w, tgt, name):
    t, d = h.shape
    tm = LEAD

    def body(h_ref, w_ref, t_ref, dh_ref, dw_ref, loss_ref):
        i = pl.program_id(0)
        x = h_ref[...]
        r = lax.rsqrt(jnp.mean(x * x, axis=-1, keepdims=True) + RMS_EPS)
        yn = x * r
        y = yn * w_ref[...]
        rows = i * tm + lax.broadcasted_iota(jnp.int32, (tm, 1), 0)
        diff = jnp.where(rows >= LEAD, y - t_ref[...], 0.0)
        tile_loss = 0.5 * jnp.sum(jnp.mean(diff * diff, axis=-1, keepdims=True), axis=0, keepdims=True)
        dy = diff / d
        gw = dy * w_ref[...]
        dot = jnp.mean(gw * x, axis=-1, keepdims=True)
        dh_ref[...] = r * gw - x * ((r * r * r) * dot)
        part = jnp.sum(dy * yn, axis=0, keepdims=True)

        @pl.when(i == 0)
        def _():
            dw_ref[...] = jnp.zeros_like(dw_ref)
            loss_ref[...] = jnp.zeros_like(loss_ref)

        dw_ref[...] += jnp.broadcast_to(part, dw_ref.shape)
        loss_ref[...] += jnp.broadcast_to(tile_loss, loss_ref.shape)

    row = pl.BlockSpec((tm, d), lambda i: (i, 0))
    return pl.pallas_call(
        body, name=name, grid=(t // tm,),
        in_specs=[row, pl.BlockSpec((1, d), lambda i: (0, 0)),
                  pl.BlockSpec((tm, d), lambda i: (jnp.maximum(i - 1, 0), 0))],
        out_specs=[row, pl.BlockSpec((8, d), lambda i: (0, 0)), pl.BlockSpec((8, LANES), lambda i: (0, 0))],
        out_shape=[jax.ShapeDtypeStruct((t, d), F32), jax.ShapeDtypeStruct((8, d), F32),
                   jax.ShapeDtypeStruct((8, LANES), F32)],
        compiler_params=_params(),
    )(h, w, tgt)


def _conv_gate_fwd(p0, conv_w, name):
    t = p0.shape[0]
    ec = p0.shape[1] // 8
    tm = _pick(t, 416, 16)
    nt = t // tm

    def body(p_ref, w_ref, g_ref, y_ref, ext_ref):
        i = pl.program_id(1)
        bg = p_ref[:, 0:ec].astype(F32)
        cg = p_ref[:, ec:2 * ec].astype(F32)
        xin = p_ref[:, 2 * ec:3 * ec].astype(F32)
        z = p_ref[:, 3 * ec:4 * ec].astype(F32)
        a = cg * xin

        @pl.when(i == 0)
        def _():
            ext_ref[0:8, :] = jnp.zeros((8, ec), F32)

        ext_ref[8:8 + tm, :] = a
        y = (w_ref[0:1, :] * ext_ref[6:6 + tm, :] + w_ref[1:2, :] * ext_ref[7:7 + tm, :]
             + w_ref[2:3, :] * a)
        ext_ref[0:8, :] = a[tm - 8:tm, :]
        y_ref[...] = y.astype(BF16)
        g_ref[...] = ((z * _sigmoid(z)) * bg * y).astype(BF16)

    out = pl.BlockSpec((tm, ec), lambda j, i: (i, j))
    return pl.pallas_call(
        body, name=name, grid=(2, nt),
        in_specs=[pl.BlockSpec((tm, 4 * ec), lambda j, i: (i, j)), pl.BlockSpec((3, ec), lambda j, i: (0, j))],
        out_specs=[out, out],
        out_shape=[jax.ShapeDtypeStruct((t, 2 * ec), BF16)] * 2,
        scratch_shapes=[pltpu.VMEM((tm + 8, ec), F32)],
        compiler_params=_params(),
    )(p0, conv_w)


def _conv_gate_bwd(p0, y, dg, conv_w, name, exchange=None):
    t = p0.shape[0]
    ec = p0.shape[1] // 8
    tm = _pick(t, 208, 16)
    nt = t // tm

    def body(p_ref, y_ref, dg_ref, w_ref, dp_ref, dw_ref, ext_ref):
        i = pl.program_id(1)
        bg = p_ref[:, 0:ec].astype(F32)
        cg = p_ref[:, ec:2 * ec].astype(F32)
        xin = p_ref[:, 2 * ec:3 * ec].astype(F32)
        z = p_ref[:, 3 * ec:4 * ec].astype(F32)
        yv = y_ref[...].astype(F32)
        dgv = dg_ref[...].astype(F32)
        sig = _sigmoid(z)
        sz = z * sig
        dp_ref[:, 3 * ec:4 * ec] = (dgv * bg * yv * (sig * (1.0 + z * (1.0 - sig)))).astype(BF16)
        dp_ref[:, 0:ec] = (dgv * sz * yv).astype(BF16)
        dy = dgv * sz * bg

        @pl.when(i == 0)
        def _():
            ext_ref[tm:tm + 8, :] = jnp.zeros((8, ec), F32)
            dw_ref[...] = jnp.zeros_like(dw_ref)

        ext_ref[0:tm, :] = dy
        dy1 = ext_ref[1:tm + 1, :]
        dy2 = ext_ref[2:tm + 2, :]
        ext_ref[tm:tm + 8, :] = dy[0:8, :]
        da = w_ref[0:1, :] * dy2 + w_ref[1:2, :] * dy1 + w_ref[2:3, :] * dy
        dp_ref[:, ec:2 * ec] = (da * xin).astype(BF16)
        dp_ref[:, 2 * ec:3 * ec] = (da * cg).astype(BF16)
        a = cg * xin
        dw_ref[0:1, :] += jnp.sum(a * dy2, axis=0, keepdims=True)
        dw_ref[1:2, :] += jnp.sum(a * dy1, axis=0, keepdims=True)
        dw_ref[2:3, :] += jnp.sum(a * dy, axis=0, keepdims=True)

    rev = lambda j, i: (nt - 1 - i, j)
    return _call(
        body, name=name, grid=(2, nt),
        in_specs=[pl.BlockSpec((tm, 4 * ec), rev), pl.BlockSpec((tm, ec), rev), pl.BlockSpec((tm, ec), rev),
                  pl.BlockSpec((3, ec), lambda j, i: (0, j))],
        out_specs=[pl.BlockSpec((tm, 4 * ec), rev), pl.BlockSpec((8, ec), lambda j, i: (0, j))],
        out_shape=[jax.ShapeDtypeStruct((t, 8 * ec), BF16), jax.ShapeDtypeStruct((8, 2 * ec), F32)],
        scratch_shapes=[pltpu.VMEM((tm + 8, ec), F32)], args=(p0, y, dg, conv_w), exchange=exchange)


def _log_sigmoid(x):
    return jnp.minimum(x, 0.0) - jnp.log(1.0 + jnp.exp(-jnp.abs(x)))


def _gates_fwd(graw, graw_t, bias_row, bias_col, name):
    t = graw.shape[0]
    tm = _pick(t, 640, 128)
    cpt = tm // CHUNK
    nh = N_HEADS

    def body(g_ref, gt_ref, br_ref, bc_ref, colf_ref, rowf_ref):
        i = pl.program_id(0)
        gc = g_ref[...] + br_ref[...]
        rows = i * tm + lax.broadcasted_iota(jnp.int32, (tm, 1), 0)
        live = rows >= PAD_ROWS
        lf = jnp.where(live, _log_sigmoid(gc), 0.0)
        li = jnp.where(live, gc, NEG)
        gt = gt_ref[...] + bc_ref[...]
        cols = i * tm + lax.broadcasted_iota(jnp.int32, (1, tm), 1)
        live_t = cols >= PAD_ROWS
        lf_t = jnp.where(live_t, _log_sigmoid(gt), 0.0)
        li_t = jnp.where(live_t, gt, NEG)
        ri = lax.broadcasted_iota(jnp.int32, (CHUNK, CHUNK), 0)
        ci = lax.broadcasted_iota(jnp.int32, (CHUNK, CHUNK), 1)
        lower = (ri >= ci).astype(F32)
        upper = (ri <= ci).astype(F32)
        lane = lax.broadcasted_iota(jnp.int32, (CHUNK, LANES), 1)
        sub = lax.broadcasted_iota(jnp.int32, (8, CHUNK), 0)
        for c in range(cpt):
            sl = slice(c * CHUNK, (c + 1) * CHUNK)
            b_all = jnp.dot(lower, lf[sl, :], precision=HIGHEST, preferred_element_type=F32)
            bt_all = jnp.dot(lf_t[:, sl], upper, precision=HIGHEST, preferred_element_type=F32)
            for h in range(nh):
                b = b_all[:, nh + h:nh + h + 1]
                r = li[sl, h:h + 1] - b
                pre = gc[sl, nh + h:nh + h + 1]
                colf_ref[h, sl, :] = jnp.where(lane == 0, b, jnp.where(lane == 1, r, jnp.where(lane == 2, pre, 0.0)))
                b_row = bt_all[nh + h:nh + h + 1, :]
                r_row = li_t[h:h + 1, sl] - b_row
                rowf_ref[h, c, :, :] = jnp.where(sub == 0, r_row, jnp.where(sub == 1, b_row, 0.0))

    return pl.pallas_call(
        body, name=name, grid=(t // tm,),
        in_specs=[pl.BlockSpec((tm, LANES), lambda i: (i, 0)), pl.BlockSpec((16, tm), lambda i: (0, i)),
                  pl.BlockSpec((1, LANES), lambda i: (0, 0)), pl.BlockSpec((16, 1), lambda i: (0, 0))],
        out_specs=[pl.BlockSpec((nh, tm, LANES), lambda i: (0, i, 0)),
                   pl.BlockSpec((nh, cpt, 8, CHUNK), lambda i: (0, i, 0, 0))],
        out_shape=[jax.ShapeDtypeStruct((nh, t, LANES), F32),
                   jax.ShapeDtypeStruct((nh, t // CHUNK, 8, CHUNK), F32)],
        compiler_params=_params(),
    )(graw, graw_t, bias_row, bias_col)


def _gates_bwd(gstat, colf, name):
    nh, t, _ = gstat.shape
    tm = _pick(t, 640, 128)
    cpt = tm // CHUNK
    nt = t // tm
    nc = t // CHUNK

    def body(gs_ref, nx_ref, colf_ref, dg_ref, db_ref):
        i = pl.program_id(0)
        ri = lax.broadcasted_iota(jnp.int32, (CHUNK, CHUNK), 0)
        ci = lax.broadcasted_iota(jnp.int32, (CHUNK, CHUNK), 1)
        upper = (ri <= ci).astype(F32)
        lane = lax.broadcasted_iota(jnp.int32, (CHUNK, LANES), 1)
        total = jnp.zeros((1, LANES), F32)
        for c in range(cpt):
            sl = slice(c * CHUNK, (c + 1) * CHUNK)
            rows = i * tm + c * CHUNK + lax.broadcasted_iota(jnp.int32, (CHUNK, 1), 0)
            live = rows >= PAD_ROWS
            acc = jnp.zeros((CHUNK, LANES), F32)
            for h in range(nh):
                blk = gs_ref[h, sl, :]
                rev = jnp.dot(upper, blk, precision=HIGHEST, preferred_element_type=F32)
                if c + 1 < cpt:
                    carry = gs_ref[h, (c + 1) * CHUNK:(c + 1) * CHUNK + 1, 2:3]
                else:
                    carry = jnp.where(i == nt - 1, 0.0, nx_ref[h, 0:1, 2:3])
                dlogf = rev[:, 0:1] + carry
                pre = colf_ref[h, sl, 2:3]
                dgf = jnp.where(live, dlogf * (1.0 - _sigmoid(pre)), 0.0)
                dgi = jnp.where(live, blk[:, 1:2], 0.0)
                acc = acc + jnp.where(lane == h, dgi, 0.0) + jnp.where(lane == nh + h, dgf, 0.0)
            dg_ref[sl, :] = acc
            total = total + jnp.sum(acc, axis=0, keepdims=True)

        @pl.when(i == 0)
        def _():
            db_ref[...] = jnp.zeros_like(db_ref)

        db_ref[...] += jnp.broadcast_to(total, db_ref.shape)

    return pl.pallas_call(
        body, name=name, grid=(nt,),
        in_specs=[pl.BlockSpec((nh, tm, LANES), lambda i: (0, i, 0)),
                  pl.BlockSpec((nh, CHUNK, LANES), lambda i: (0, jnp.minimum((i + 1) * cpt, nc - 1), 0)),
                  pl.BlockSpec((nh, tm, LANES), lambda i: (0, i, 0))],
        out_specs=[pl.BlockSpec((tm, LANES), lambda i: (i, 0)), pl.BlockSpec((8, LANES), lambda i: (0, 0))],
        out_shape=[jax.ShapeDtypeStruct((t, LANES), F32), jax.ShapeDtypeStruct((8, LANES), F32)],
        compiler_params=_params(),
    )(gstat, gstat, colf)


NT_DIMS = (((1,), (1,)), ((), ()))
TN_DIMS = (((0,), (0,)), ((), ()))


def _dot(a, b):
    return jnp.dot(a.astype(BF16), b.astype(BF16), preferred_element_type=F32)


def _dot_nt(a, b):
    return lax.dot_general(a.astype(BF16), b.astype(BF16), NT_DIMS, preferred_element_type=F32)


def _dot_tn(a, b):
    return lax.dot_general(a.astype(BF16), b.astype(BF16), TN_DIMS, preferred_element_type=F32)


def _chunk_gates(colf_ref, rowf_ref, m_prev):
    b = colf_ref[:, 0:1]
    rcol = colf_ref[:, 1:2]
    rrow = rowf_ref[0:1, :]
    ri = lax.broadcasted_iota(jnp.int32, (CHUNK, CHUNK), 0)
    ci = lax.broadcasted_iota(jnp.int32, (CHUNK, CHUNK), 1)
    log_d = jnp.where(ri >= ci, b + rrow, NEG)
    m_row = jnp.maximum(b + m_prev, jnp.max(log_d, axis=-1, keepdims=True))
    dmat = jnp.exp(log_d - m_row)
    inter = jnp.exp(b + m_prev - m_row)
    b_last = b[CHUNK - 1:CHUNK, :]
    log_w = rcol + b_last
    m_new = jnp.maximum(b_last + m_prev, jnp.max(log_w, axis=0, keepdims=True))
    decay = jnp.exp(b_last + m_prev - m_new)
    w = jnp.exp(log_w - m_new)
    return m_row, dmat, inter, m_new, decay, w


def _mlstm_fwd(p1, colf, rowf, head_w, dk, dv, name):
    t = p1.shape[0]
    nh = N_HEADS
    nc = t // CHUNK
    hw = 2 * dk + 3 * dv
    cps = CHUNKS_PER_STEP if nc % CHUNKS_PER_STEP == 0 else 1
    rows_per_step = cps * CHUNK

    def body(p_ref, colf_ref, rowf_ref, w_ref, hm_ref, hh_ref, stat_ref, cs_ref, ns_ref, c_ref, n_ref, m_ref):
        c = pl.program_id(0)

        @pl.when(c == 0)
        def _():
            c_ref[...] = jnp.zeros_like(c_ref)
            n_ref[...] = jnp.zeros_like(n_ref)
            m_ref[...] = jnp.zeros_like(m_ref)

        for cc in range(cps):
            rows = pl.ds(cc * CHUNK, CHUNK)
            for h in range(nh):
                cols = pl.ds(h * dv, dv)
                head(p_ref.at[rows, pl.ds(h * hw, hw)], colf_ref.at[h, rows], rowf_ref.at[h, cc], w_ref.at[:, cols],
                     hm_ref.at[rows, cols], hh_ref.at[rows, cols], stat_ref.at[h, rows], cs_ref.at[h, cc],
                     ns_ref.at[h, cc], c_ref.at[h], n_ref.at[h], m_ref.at[h])

    def head(p_ref, colf_ref, rowf_ref, w_ref, hm_ref, hh_ref, stat_ref, cs_ref, ns_ref, c_ref, n_ref, m_ref):
        m_prev = m_ref[...]
        n_prev = n_ref[...]
        c_prev = c_ref[...]
        cs_ref[...] = c_prev.astype(BF16)
        sub = lax.broadcasted_iota(jnp.int32, (8, dk), 0)
        ns_ref[...] = jnp.where(sub == 0, n_prev, jnp.where(sub == 1, m_prev, 0.0))

        q = p_ref[:, 0:dk]
        k = p_ref[:, dk:2 * dk]
        v = p_ref[:, 2 * dk:2 * dk + dv]
        o = p_ref[:, 2 * dk + dv:2 * dk + 2 * dv].astype(F32)
        z = p_ref[:, 2 * dk + 2 * dv:2 * dk + 3 * dv].astype(F32)
        m_row, dmat, inter, m_new, decay, w = _chunk_gates(colf_ref, rowf_ref, m_prev)
        s = _dot_nt(q, k) * dmat
        num = _dot(s, v) + inter * _dot(q, c_prev)
        den = jnp.sum(s, axis=-1, keepdims=True) + inter * jnp.sum(q.astype(F32) * n_prev, axis=-1, keepdims=True)
        denom = jnp.maximum(jnp.abs(den), jnp.exp(-m_row))
        hh = num * (1.0 / denom)
        hh_ref[...] = hh
        lane = lax.broadcasted_iota(jnp.int32, (CHUNK, LANES), 1)
        stat_ref[...] = jnp.where(lane == 0, den, 0.0)

        c_ref[...] = decay * c_prev + _dot_tn(k, w * v.astype(F32))
        n_ref[...] = decay * n_prev + jnp.sum(w * k.astype(F32), axis=0, keepdims=True)
        m_ref[...] = m_new

        r = lax.rsqrt(jnp.mean(hh * hh, axis=-1, keepdims=True) + RMS_EPS)
        hn = (hh * r) * w_ref[...]
        hm_ref[...] = (hn * _sigmoid(o) * (z * _sigmoid(z))).astype(BF16)

    return pl.pallas_call(
        body, name=name, grid=(nc // cps,),
        in_specs=[pl.BlockSpec((rows_per_step, nh * hw), lambda c: (c, 0)),
                  pl.BlockSpec((nh, rows_per_step, LANES), lambda c: (0, c, 0)),
                  pl.BlockSpec((nh, cps, 8, CHUNK), lambda c: (0, c, 0, 0)),
                  pl.BlockSpec((1, nh * dv), lambda c: (0, 0))],
        out_specs=[pl.BlockSpec((rows_per_step, nh * dv), lambda c: (c, 0)),
                   pl.BlockSpec((rows_per_step, nh * dv), lambda c: (c, 0)),
                   pl.BlockSpec((nh, rows_per_step, LANES), lambda c: (0, c, 0)),
                   pl.BlockSpec((nh, cps, dk, dv), lambda c: (0, c, 0, 0)),
                   pl.BlockSpec((nh, cps, 8, dk), lambda c: (0, c, 0, 0))],
        out_shape=[jax.ShapeDtypeStruct((t, nh * dv), BF16), jax.ShapeDtypeStruct((t, nh * dv), F32),
                   jax.ShapeDtypeStruct((nh, t, LANES), F32),
                   jax.ShapeDtypeStruct((nh, nc, dk, dv), BF16), jax.ShapeDtypeStruct((nh, nc, 8, dk), F32)],
        scratch_shapes=[pltpu.VMEM((nh, dk, dv), F32), pltpu.VMEM((nh, 1, dk), F32), pltpu.VMEM((nh, 1, 1), F32)],
        compiler_params=_params(),
    )(p1, colf, rowf, head_w)


def _mlstm_bwd(p1, colf, rowf, head_w, hh, stat, csave, nsave, dhm, dk, dv, name, exchange=None):
    t = p1.shape[0]
    nh = N_HEADS
    nc = t // CHUNK
    hw = 2 * dk + 3 * dv
    cps = CHUNKS_PER_STEP if nc % CHUNKS_PER_STEP == 0 else 1
    rows_per_step = cps * CHUNK

    def body(p_ref, colf_ref, rowf_ref, w_ref, hh_ref, stat_ref, cs_ref, ns_ref, dhm_ref,
             dp_ref, gs_ref, dw_ref, dc_ref, dn_ref):
        c = pl.program_id(0)

        @pl.when(c == 0)
        def _():
            dc_ref[...] = jnp.zeros_like(dc_ref)
            dn_ref[...] = jnp.zeros_like(dn_ref)
            dw_ref[...] = jnp.zeros_like(dw_ref)

        for cc in reversed(range(cps)):
            rows = pl.ds(cc * CHUNK, CHUNK)
            for h in range(nh):
                cols = pl.ds(h * dv, dv)
                head(p_ref.at[rows, pl.ds(h * hw, hw)], colf_ref.at[h, rows], rowf_ref.at[h, cc], w_ref.at[:, cols],
                     hh_ref.at[rows, cols], stat_ref.at[h, rows], cs_ref.at[h, cc], ns_ref.at[h, cc],
                     dhm_ref.at[rows, cols], dp_ref.at[rows, pl.ds(h * hw, hw)], gs_ref.at[h, rows],
                     dw_ref.at[h], dc_ref.at[h], dn_ref.at[h])

    def head(p_ref, colf_ref, rowf_ref, w_ref, hh_ref, stat_ref, cs_ref, ns_ref, dhm_ref,
             dp_ref, gs_ref, dw_ref, dc_ref, dn_ref):
        q = p_ref[:, 0:dk]
        k = p_ref[:, dk:2 * dk]
        v = p_ref[:, 2 * dk:2 * dk + dv]
        o = p_ref[:, 2 * dk + dv:2 * dk + 2 * dv].astype(F32)
        z = p_ref[:, 2 * dk + 2 * dv:2 * dk + 3 * dv].astype(F32)
        qf = q.astype(F32)
        kf = k.astype(F32)
        n_prev = ns_ref[0:1, :]
        m_prev = ns_ref[1:2, 0:1]
        c_prev = cs_ref[...]
        m_row, dmat, inter, m_new, decay, w = _chunk_gates(colf_ref, rowf_ref, m_prev)

        hh = hh_ref[...]
        dhm_v = dhm_ref[...].astype(F32)
        so = _sigmoid(o)
        sg = _sigmoid(z)
        sz = z * sg
        r = lax.rsqrt(jnp.mean(hh * hh, axis=-1, keepdims=True) + RMS_EPS)
        hn = (hh * r) * w_ref[...]
        dhn = dhm_v * so * sz
        dp_ref[:, 2 * dk + dv:2 * dk + 2 * dv] = (dhm_v * hn * sz * (so * (1.0 - so))).astype(BF16)
        dp_ref[:, 2 * dk + 2 * dv:2 * dk + 3 * dv] = (dhm_v * hn * so * (sg * (1.0 + z * (1.0 - sg)))).astype(BF16)
        dw_ref[...] += jnp.broadcast_to(jnp.sum(dhn * (hh * r), axis=0, keepdims=True), dw_ref.shape)
        gwn = dhn * w_ref[...]
        dhh = r * gwn - hh * ((r * r * r) * jnp.mean(gwn * hh, axis=-1, keepdims=True))

        den = stat_ref[:, 0:1]
        floor = jnp.exp(-m_row)
        denom = jnp.maximum(jnp.abs(den), floor)
        inv = 1.0 / denom
        dnum = dhh * inv
        hdot = jnp.sum(dhh * hh, axis=-1, keepdims=True)
        dden = jnp.where(jnp.abs(den) > floor, -(hdot * inv) * jnp.sign(den), 0.0)
        s = _dot_nt(q, k) * dmat
        dqk = (_dot_nt(dnum, v) + dden) * dmat
        dc_new = dc_ref[...]
        dn_new = dn_ref[...]
        idd = inter * dden
        dq = _dot(dqk, k) + inter * _dot_nt(dnum, c_prev) + idd * n_prev
        dkv = _dot_tn(dqk, q) + w * (_dot_nt(v, dc_new) + dn_new)
        dvv = _dot_tn(s, dnum) + w * _dot(k, dc_new)
        dp_ref[:, 0:dk] = dq.astype(BF16)
        dp_ref[:, dk:2 * dk] = dkv.astype(BF16)
        dp_ref[:, 2 * dk:2 * dk + dv] = dvv.astype(BF16)
        qdq = jnp.sum(qf * dq, axis=-1, keepdims=True)
        kdk = jnp.sum(kf * dkv, axis=-1, keepdims=True)
        dc_prev = decay * dc_new + _dot_tn(inter * qf, dnum)
        dn_prev = decay * dn_new + jnp.sum(idd * qf, axis=0, keepdims=True)
        dc_ref[...] = dc_prev
        dn_ref[...] = dn_prev
        cross = (jnp.sum(jnp.sum(c_prev.astype(F32) * dc_prev, axis=-1, keepdims=True), axis=0, keepdims=True)
                 + jnp.sum(n_prev * dn_prev, axis=-1, keepdims=True))
        lane = lax.broadcasted_iota(jnp.int32, (CHUNK, LANES), 1)
        gs_ref[...] = jnp.where(lane == 0, qdq - kdk, jnp.where(lane == 1, kdk, jnp.where(lane == 2, cross, 0.0)))

    ns = nc // cps
    rc = lambda c: (ns - 1 - c, 0)
    rc3 = lambda c: (0, ns - 1 - c, 0)
    rc4 = lambda c: (0, ns - 1 - c, 0, 0)
    return _call(
        body, name=name, grid=(ns,),
        in_specs=[pl.BlockSpec((rows_per_step, nh * hw), rc),
                  pl.BlockSpec((nh, rows_per_step, LANES), rc3),
                  pl.BlockSpec((nh, cps, 8, CHUNK), rc4),
                  pl.BlockSpec((1, nh * dv), lambda c: (0, 0)),
                  pl.BlockSpec((rows_per_step, nh * dv), rc),
                  pl.BlockSpec((nh, rows_per_step, LANES), rc3),
                  pl.BlockSpec((nh, cps, dk, dv), rc4),
                  pl.BlockSpec((nh, cps, 8, dk), rc4),
                  pl.BlockSpec((rows_per_step, nh * dv), rc)],
        out_specs=[pl.BlockSpec((rows_per_step, nh * hw), rc),
                   pl.BlockSpec((nh, rows_per_step, LANES), rc3),
                   pl.BlockSpec((nh, 8, dv), lambda c: (0, 0, 0))],
        out_shape=[jax.ShapeDtypeStruct((t, nh * hw), BF16), jax.ShapeDtypeStruct((nh, t, LANES), F32),
                   jax.ShapeDtypeStruct((nh, 8, dv), F32)],
        scratch_shapes=[pltpu.VMEM((nh, dk, dv), F32), pltpu.VMEM((nh, 1, dk), F32)],
        args=(p1, colf, rowf, head_w, hh, stat, csave, nsave, dhm), exchange=exchange)


def _position():
    return lax.axis_index("x"), lax.axis_index("y"), lax.axis_index("c")


class _Exchange:
    def __init__(self, ins, out_shapes, start, mid, finish):
        n = len(ins)
        self.ins, self.out_shapes = list(ins), list(out_shapes)
        self.start, self.mid, self.finish = start, mid, finish
        self.scratch = [pltpu.SemaphoreType.DMA((n, 7)), pltpu.SemaphoreType.DMA((n, 7)),
                        pltpu.SemaphoreType.DMA((n,))]


def _gather_exchange(shards):
    n = len(shards)

    def plan(ins, outs, sems):
        send_sems, recv_sems, local_sems = sems
        x, y, c = _position()
        me, sibling = (x, y, c), (x, y, 1 - c)
        chips = [(1 - x, y), (x, 1 - y), (1 - x, 1 - y)]

        def copy(a, k, block, to, src=None):
            px, py, pc = block
            dst = outs[a].at[4 * px + 2 * py + pc]
            return pltpu.make_async_remote_copy(
                src_ref=dst if src is None else src, dst_ref=dst,
                send_sem=send_sems.at[a, k], recv_sem=recv_sems.at[a, k],
                device_id=to, device_id_type=MESH)

        def mine():
            return [pltpu.make_async_copy(ins[a], outs[a].at[4 * x + 2 * y + c], local_sems.at[a])
                    for a in range(n)]

        def first():
            out = []
            for a in range(n):
                out.append(copy(a, 0, me, sibling, src=ins[a]))
                out += [copy(a, 1 + j, me, (*chip, c), src=ins[a]) for j, chip in enumerate(chips)]
            return out

        def ici_in():
            return [copy(a, 1 + j, (*chip, c), me) for j, chip in enumerate(chips) for a in range(n)]

        def passed():
            return [copy(a, 4 + j, (*chip, c), sibling) for j, chip in enumerate(chips) for a in range(n)]

        def d2d_in():
            return ([copy(a, 0, sibling, me) for a in range(n)]
                    + [copy(a, 4 + j, (*chip, 1 - c), me) for j, chip in enumerate(chips) for a in range(n)])

        return mine, first, ici_in, passed, d2d_in

    def start(ins, outs, sems):
        mine, first, _, _, _ = plan(ins, outs, sems)
        for cp in mine() + first():
            cp.start()

    def mid(ins, outs, sems):
        _, _, ici_in, passed, _ = plan(ins, outs, sems)
        for arrived, onward in zip(ici_in(), passed()):
            arrived.wait_recv()
            onward.start()

    def finish(ins, outs, sems):
        mine, first, _, passed, d2d_in = plan(ins, outs, sems)
        for cp in d2d_in():
            cp.wait_recv()
        for cp in first() + passed():
            cp.wait_send()
        for cp in mine():
            cp.wait()

    shapes = [jax.ShapeDtypeStruct((N_DEV,) + s.shape, s.dtype) for s in shards]
    return _Exchange(shards, shapes, start, mid, finish)


def _scatter_exchange(fulls):
    n = len(fulls)

    def plan(ins, outs, sems):
        send_sems, recv_sems, local_sems = sems
        x, y, c = _position()
        my_slot = 4 * x + 2 * y + c

        def mine():
            return [pltpu.make_async_copy(ins[a].at[my_slot], outs[a].at[my_slot], local_sems.at[a])
                    for a in range(n)]

        def remote(arriving):
            out = []
            for kk in (1, 2, 4, 6, 3, 5, 7):
                kx, ky, kc = (kk >> 2) & 1, (kk >> 1) & 1, kk & 1
                px = 1 - x if kx else x
                py = 1 - y if ky else y
                pc = 1 - c if kc else c
                peer_slot = 4 * px + 2 * py + pc
                for a in range(n):
                    out.append(pltpu.make_async_remote_copy(
                        src_ref=ins[a].at[peer_slot], dst_ref=outs[a].at[peer_slot if arriving else my_slot],
                        send_sem=send_sems.at[a, kk - 1], recv_sem=recv_sems.at[a, kk - 1],
                        device_id=(px, py, pc), device_id_type=MESH))
            return out

        return mine, remote

    def start(ins, outs, sems):
        mine, remote = plan(ins, outs, sems)
        for cp in mine() + remote(False):
            cp.start()

    def finish(ins, outs, sems):
        mine, remote = plan(ins, outs, sems)
        for cp in remote(True):
            cp.wait_recv()
        for cp in remote(False):
            cp.wait_send()
        for cp in mine():
            cp.wait()

    shapes = [jax.ShapeDtypeStruct(f.shape, f.dtype) for f in fulls]
    return _Exchange(fulls, shapes, start, None, finish)


def _run_exchange(ex, name):
    n_in, n_out = len(ex.ins), len(ex.out_shapes)

    def body(*refs):
        ins, outs, sems = refs[:n_in], refs[n_in:n_in + n_out], refs[n_in + n_out:]
        ex.start(ins, outs, sems)
        if ex.mid is not None:
            ex.mid(ins, outs, sems)
        ex.finish(ins, outs, sems)

    any_spec = pl.BlockSpec(memory_space=pl.ANY)
    return pl.pallas_call(
        body, name=name,
        in_specs=[any_spec] * n_in, out_specs=[any_spec] * n_out,
        out_shape=ex.out_shapes, scratch_shapes=ex.scratch,
    )(*ex.ins)


def _natural_to_head_major(n0, dk, dv):
    nh = N_HEADS
    qk, e, hw = nh * dk, nh * dv, 2 * dk + 3 * dv
    if n0 < qk:
        h, off = divmod(n0, dk)
        return h * hw + off, True
    n1 = n0 - qk
    if n1 < qk:
        h, off = divmod(n1, dk)
        return h * hw + dk + off, False
    part, n3 = divmod(n1 - qk, e)
    h, off = divmod(n3, dv)
    return h * hw + 2 * dk + part * dv + off, False


def _mlstm_weight_layout(w_blocks, dk, dv, name):
    _, d, blk = w_blocks.shape
    nh = N_HEADS
    hw = 2 * dk + 3 * dv
    n_tiles = nh * hw // LANES
    tiles_per_block = d // LANES
    tr = _pick(d, 128, 16)
    scale = dk ** -0.5

    def body(w_ref, p_ref, g_ref):
        lane = lax.broadcasted_iota(jnp.int32, (tr, LANES), 1)

        def tile(s, r):
            return w_ref[s, :, r * LANES:(r + 1) * LANES].astype(F32)

        def last_col(s):
            return w_ref[s, :, d:d + 1].astype(F32)

        for tn in range(n_tiles):
            s, r = divmod(tn, tiles_per_block)
            if s == 0:
                val = tile(0, r)
            elif r == 0:
                from_prev = pltpu.roll(tile(s - 1, tiles_per_block - 1), s - 1, 1)
                from_here = pltpu.roll(tile(s, 0), s, 1)
                val = jnp.where(lane < s - 1, from_prev, jnp.where(lane == s - 1, last_col(s - 1), from_here))
            else:
                slab = jnp.concatenate([tile(s, r - 1), tile(s, r)], axis=1)
                val = pltpu.roll(slab, s, 1)[:, LANES:]
            at, is_q = _natural_to_head_major(tn * LANES, dk, dv)
            if is_q:
                val = val * scale
            p_ref[:, at:at + LANES] = val.astype(BF16)
        n_gate = 2 * nh
        s = N_DEV - 1
        gates = pltpu.roll(tile(s, tiles_per_block - 1), n_gate - 1, 1)
        g_ref[...] = jnp.where(lane < n_gate - 1, gates, jnp.where(lane == n_gate - 1, last_col(s), 0.0)).astype(BF16)

    return pl.pallas_call(
        body, name=name, grid=(d // tr,),
        in_specs=[pl.BlockSpec((N_DEV, tr, blk), lambda i: (0, i, 0))],
        out_specs=[pl.BlockSpec((tr, nh * hw), lambda i: (i, 0)), pl.BlockSpec((tr, LANES), lambda i: (i, 0))],
        out_shape=[jax.ShapeDtypeStruct((d, nh * hw), BF16), jax.ShapeDtypeStruct((d, LANES), BF16)],
        compiler_params=_params(),
    )(w_blocks)


def _mlstm_grad_layout(g_p1, g_gate, dk, dv, name):
    d = g_p1.shape[0]
    nh = N_HEADS
    hw = 2 * dk + 3 * dv
    n_tiles = nh * hw // LANES
    tiles_per_block = d // LANES
    tr = _pick(d, 128, 16)
    scale = dk ** -0.5

    def body(p_ref, g_ref, o_ref):
        def natural(tn):
            if tn == n_tiles:
                return g_ref[...]
            at, is_q = _natural_to_head_major(tn * LANES, dk, dv)
            val = p_ref[:, at:at + LANES].astype(F32)
            return val * scale if is_q else val

        for s in range(N_DEV):
            for r in range(tiles_per_block):
                tn = s * tiles_per_block + r
                if s == 0:
                    val = natural(tn)
                else:
                    slab = jnp.concatenate([natural(tn), natural(tn + 1)], axis=1)
                    val = pltpu.roll(slab, 2 * LANES - s, 1)[:, :LANES]
                o_ref[s, :, r * LANES:(r + 1) * LANES] = val.astype(BF16)
            o_ref[s, :, d:d + 1] = natural((s + 1) * tiles_per_block)[:, s:s + 1].astype(BF16)

    return pl.pallas_call(
        body, name=name, grid=(d // tr,),
        in_specs=[pl.BlockSpec((tr, nh * hw), lambda i: (i, 0)), pl.BlockSpec((tr, LANES), lambda i: (i, 0))],
        out_specs=pl.BlockSpec((N_DEV, tr, d + 1), lambda i: (0, i, 0)),
        out_shape=jax.ShapeDtypeStruct((N_DEV, d, d + 1), BF16),
        compiler_params=_params(),
    )(g_p1, g_gate)


def _adamw_math(w, g, m, v):
    m = ADAM_B1 * m + (1.0 - ADAM_B1) * g
    v = ADAM_B2 * v + (1.0 - ADAM_B2) * (g * g)
    m_hat = m / (1.0 - ADAM_B1 ** ADAM_STEP)
    v_hat = v / (1.0 - ADAM_B2 ** ADAM_STEP)
    delta = -ADAM_LR * (m_hat / (jnp.sqrt(v_hat) + ADAM_EPS) + ADAM_WD * w)
    return delta, m, v


def _adamw_sharded(parts, w, m, v, name):
    _, r, c = parts.shape
    tr = _pick(r, 128, 8)

    def body(p_ref, w_ref, m_ref, v_ref, g_ref, d_ref, nm_ref, nv_ref):
        g = p_ref[0].astype(F32)
        for s in range(1, N_DEV):
            g = g + p_ref[s].astype(F32)
        delta, m_new, v_new = _adamw_math(w_ref[...], g, m_ref[...], v_ref[...])
        g_ref[...] = g
        d_ref[...] = delta
        nm_ref[...] = m_new
        nv_ref[...] = v_new

    blk = pl.BlockSpec((tr, c), lambda i: (i, 0))
    return pl.pallas_call(
        body, name=name, grid=(r // tr,),
        in_specs=[pl.BlockSpec((N_DEV, tr, c), lambda i: (0, i, 0)), blk, blk, blk],
        out_specs=[blk] * 4,
        out_shape=[jax.ShapeDtypeStruct((r, c), F32)] * 4,
        compiler_params=_params(),
    )(parts, w, m, v)


def _sum_devices(parts, name):
    _, r, c = parts.shape

    def body(p_ref, o_ref):
        g = p_ref[0]
        for s in range(1, N_DEV):
            g = g + p_ref[s]
        o_ref[...] = g

    return pl.pallas_call(
        body, name=name, out_shape=jax.ShapeDtypeStruct((r, c), F32), compiler_params=_params(),
    )(parts)


def _adamw_small(gs, ws, ms, vs, name):
    n = len(gs)

    def body(*refs):
        g_refs, w_refs, m_refs, v_refs = refs[:n], refs[n:2 * n], refs[2 * n:3 * n], refs[3 * n:4 * n]
        d_refs, nm_refs, nv_refs = refs[4 * n:5 * n], refs[5 * n:6 * n], refs[6 * n:7 * n]
        for a in range(n):
            delta, m_new, v_new = _adamw_math(w_refs[a][...], g_refs[a][...], m_refs[a][...], v_refs[a][...])
            d_refs[a][...] = delta
            nm_refs[a][...] = m_new
            nv_refs[a][...] = v_new

    shapes = [jax.ShapeDtypeStruct(w.shape, F32) for w in ws]
    outs = pl.pallas_call(
        body, name=name, out_shape=shapes * 3, compiler_params=_params(),
    )(*gs, *ws, *ms, *vs)
    return outs[:n], outs[n:2 * n], outs[2 * n:]


def _pad_rows(a, rows):
    return jnp.pad(a, ((0, rows - a.shape[0]), (0, 0)))


def kernel(x, meta_tokens, norm_w, conv_in_w, conv_w, conv_out_w, mlstm_in_w, mlstm_gate_b, mlstm_head_norm_w, mlstm_out_w, final_norm_w, loss_target, m_meta_tokens, m_norm_w, m_conv_in_w, m_conv_w, m_conv_out_w, m_mlstm_in_w, m_mlstm_gate_b, m_mlstm_head_norm_w, m_mlstm_out_w, m_final_norm_w, v_meta_tokens, v_norm_w, v_conv_in_w, v_conv_w, v_conv_out_w, v_mlstm_in_w, v_mlstm_gate_b, v_mlstm_head_norm_w, v_mlstm_out_w, v_final_norm_w):
    seq, d = x.shape[1], x.shape[2]
    t = seq + LEAD
    e = 2 * d
    ec = e // 2
    nh = N_HEADS
    dv = e // nh
    dk = dv // 2
    qk = nh * dk
    hw = 2 * dk + 3 * dv
    n_in = 2 * qk + 3 * e + 2 * nh
    n_in_s = n_in // N_DEV
    me = 4 * lax.axis_index("x") + 2 * lax.axis_index("y") + lax.axis_index("c")
    tm = _pick(t, 832, 16)
    tm_in = _pick(t, 1664, 16)
    tkw = _pick(t, 2080, 16)

    small = jnp.concatenate([
        meta_tokens,
        _pad_rows(conv_w[0].reshape(3 * (e // N_DEV) // LANES, LANES), 8),
        _pad_rows(mlstm_head_norm_w[0].reshape((e // N_DEV) // LANES, LANES), 8),
    ], axis=0) if d // N_DEV == LANES else None
    assert small is not None, "the packed small-weight block assumes d_model / 8 == 128"
    w_ci, w_co, small_g = _run_exchange(
        _gather_exchange([conv_in_w[0].astype(BF16), conv_out_w[0].astype(BF16), small]), "gather_weights")
    w_co = w_co.reshape(e, d)
    meta_full = jnp.transpose(small_g[:, 0:N_META, :], (1, 0, 2)).reshape(N_META, d)
    cw_rows = 3 * (e // N_DEV) // LANES
    conv_w_full = jnp.transpose(
        small_g[:, N_META:N_META + cw_rows, :].reshape(N_DEV, 3, e // N_DEV), (1, 0, 2)).reshape(3, e)
    hn_rows = (e // N_DEV) // LANES
    head_w_full = small_g[:, N_META + 8:N_META + 8 + hn_rows, :].reshape(1, e)

    h0 = jnp.concatenate([jnp.zeros((PAD_ROWS, d), F32), meta_full, x[0]], axis=0)
    tgt = loss_target[0]

    ci_map = lambda blk: 2 * (blk % 4) + blk // 4
    u0 = _rms_fwd(h0, norm_w[0:1], "rms0")
    p0, (w_mi, w_mo) = _matmul(
        u0, w_ci, form="nn", m=t, n=8 * ec, kdim=d, tm=tm_in, tn=ec, tk=d, out_dtype=BF16, name="conv_in",
        b_spec=pl.BlockSpec((None, d, ec), lambda i, j, k: (ci_map(j), 0, 0)),
        exchange=_gather_exchange([mlstm_in_w[0].astype(BF16), mlstm_out_w[0].astype(BF16)]))
    w_mo = w_mo.reshape(e, d)
    g0, y0 = _conv_gate_fwd(p0, conv_w_full, "conv_gate")
    h1 = _matmul(g0, w_co, form="nn", m=t, n=d, kdim=e, tm=tm, tn=d, tk=e, out_dtype=F32, name="conv_out",
                 residual=h0)

    assert n_in_s == d + 1 and w_mi.shape == (N_DEV, d, d + 1)
    w_p1, w_gate = _mlstm_weight_layout(w_mi, dk, dv, "mlstm_w_layout")
    w_gate_t = jnp.transpose(w_gate[:, 0:16])
    bias_row = jnp.pad(mlstm_gate_b, ((0, 0), (0, LANES - 2 * nh)))
    bias_col = jnp.pad(mlstm_gate_b.T, ((0, 16 - 2 * nh), (0, 0)))

    u1 = _rms_fwd(h1, norm_w[1:2], "rms1")
    p1 = _matmul(u1, w_p1, form="nn", m=t, n=nh * hw, kdim=d, tm=tm_in, tn=_pick(nh * hw, 1024, LANES), tk=d,
                 out_dtype=BF16, name="mlstm_in")
    graw = _matmul(u1, w_gate, form="nn", m=t, n=LANES, kdim=d, tm=tm, tn=LANES, tk=d, out_dtype=F32,
                   name="gates_col")
    graw_t = _matmul(w_gate_t, u1, form="nt", m=16, n=t, kdim=d, tm=16, tn=_pick(t, 1664, LANES), tk=d,
                     out_dtype=F32, name="gates_row")
    colf, rowf = _gates_fwd(graw, graw_t, bias_row, bias_col, "gates_fwd")
    hm, hh, stat, csave, nsave = _mlstm_fwd(p1, colf, rowf, head_w_full, dk, dv, "mlstm_fwd")
    h2 = _matmul(hm, w_mo, form="nn", m=t, n=d, kdim=e, tm=tm, tn=d, tk=e, out_dtype=F32, name="mlstm_out",
                 residual=h1)

    dh2, dwf, loss_part = _final_loss(h2, final_norm_w.reshape(1, d), tgt, "final_loss")

    dhm = _matmul(dh2, w_mo, form="nt", m=t, n=e, kdim=d, tm=tm, tn=_pick(e, 1024, LANES), tk=d, out_dtype=BF16,
                  name="mlstm_out_dx")
    g_mo = _matmul(hm, dh2, form="tn", m=e, n=d, kdim=t, tm=_pick(e, 1024, LANES), tn=d, tk=tkw, out_dtype=BF16,
                   name="mlstm_out_dw")
    (dp1, gstat, dhead), (r_mo,) = _mlstm_bwd(
        p1, colf, rowf, head_w_full, hh, stat, csave, nsave, dhm, dk, dv, "mlstm_bwd",
        exchange=_scatter_exchange([g_mo.reshape(N_DEV, e // N_DEV, d)]))
    dgates, dbias = _gates_bwd(gstat, colf, "gates_bwd")
    du1 = _matmul(dp1, w_p1, form="nt", m=t, n=d, kdim=nh * hw, tm=tm, tn=d, tk=_pick(nh * hw, 2048, LANES),
                  out_dtype=F32, name="mlstm_in_dx")
    du1 = _matmul(dgates, w_gate, form="nt", m=t, n=d, kdim=LANES, tm=tm, tn=d, tk=LANES, out_dtype=F32,
                  name="gates_dx", residual=du1)
    g_p1 = _matmul(u1, dp1, form="tn", m=d, n=nh * hw, kdim=t, tm=d, tn=_pick(nh * hw, 1024, LANES), tk=tkw,
                   out_dtype=BF16, name="mlstm_in_dw")
    g_gate = _matmul(u1, dgates, form="tn", m=d, n=LANES, kdim=t, tm=d, tn=LANES, tk=tkw, out_dtype=F32,
                     name="gates_dw")
    dh1, dnw1 = _rms_bwd(h1, norm_w[1:2], du1, dh2, "rms1_bwd")

    g_mi = _mlstm_grad_layout(g_p1, g_gate, dk, dv, "mlstm_g_layout")

    dg0 =_matmul(dh1, w_co, form="nt", m=t, n=e, kdim=d, tm=tm, tn=_pick(e, 1024, LANES), tk=d, out_dtype=BF16,
                  name="conv_out_dx")
    g_co = _matmul(g0, dh1, form="tn", m=e, n=d, kdim=t, tm=_pick(e, 1024, LANES), tn=d, tk=tkw, out_dtype=BF16,
                   name="conv_out_dw")
    (dp0, dconv), (r_mi,) = _conv_gate_bwd(p0, y0, dg0, conv_w_full, "conv_gate_bwd",
                                           exchange=_scatter_exchange([g_mi]))
    g_ci, (r_co,) = _matmul(
        u0, dp0, form="tn", m=d, n=8 * ec, kdim=t, tm=d, tn=ec, tk=tkw, out_dtype=BF16, name="conv_in_dw",
        out_shape=(N_DEV, d, ec), out_spec=pl.BlockSpec((None, d, ec), lambda i, j, k: (ci_map(j), 0, 0)),
        exchange=_scatter_exchange([g_co.reshape(N_DEV, e // N_DEV, d)]))
    du0, (r_ci,) = _matmul(
        dp0, w_ci, form="nt", m=t, n=d, kdim=8 * ec, tm=tm, tn=d, tk=ec, out_dtype=F32, name="conv_in_dx",
        b_spec=pl.BlockSpec((None, d, ec), lambda i, j, k: (ci_map(k), 0, 0)),
        exchange=_scatter_exchange([g_ci]))
    grad_x, dmeta, dnw0 = _rms_bwd_first(h0, norm_w[0:1], du0, dh1, "rms0_bwd")
    grad_x = grad_x[None]

    row8 = lax.broadcasted_iota(jnp.int32, (8, 1), 0)
    loss_wide = jnp.pad(loss_part, ((0, 0), (0, d - LANES)))
    payload = jnp.concatenate([
        jnp.where(row8 == 0, dnw0, jnp.where(row8 == 1, dnw1, 0.0)),
        jnp.where(row8 == 0, dwf, jnp.where(row8 == 1, loss_wide, 0.0)),
        jnp.where(row8 == 0, jnp.pad(dbias, ((0, 0), (0, d - LANES))), 0.0),
        dmeta,
        _pad_rows(dconv[0:3].reshape(3 * e // d, d), 8),
        _pad_rows(dhead[:, 0, :].reshape(e // d, d), 8),
    ], axis=0)
    (payload_g,) = _run_exchange(_gather_exchange([payload]), "gather_small_grads")
    tot = _sum_devices(payload_g, "sum_small_grads")

    loss = tot[9, 0]
    g_norm = tot[0:2]
    g_final = tot[8]
    g_gate_b = tot[16:17, 0:2 * nh]
    g_meta = lax.dynamic_slice(tot[24:24 + N_META], (0, me * (d // N_DEV)), (N_META, d // N_DEV))
    g_conv_w = lax.dynamic_slice(tot[40:40 + 3 * e // d].reshape(3, e), (0, me * (e // N_DEV)), (3, e // N_DEV))
    g_head = lax.dynamic_slice(tot[48:48 + e // d].reshape(1, e), (0, me * (e // N_DEV)), (1, e // N_DEV))

    g1, d1, nm1, nv1 = _adamw_sharded(r_ci, conv_in_w[0], m_conv_in_w[0], v_conv_in_w[0], "adamw_conv_in")
    g2, d2, nm2, nv2 = _adamw_sharded(r_co, conv_out_w[0], m_conv_out_w[0], v_conv_out_w[0], "adamw_conv_out")
    g3, d3, nm3, nv3 = _adamw_sharded(r_mi, mlstm_in_w[0], m_mlstm_in_w[0], v_mlstm_in_w[0], "adamw_mlstm_in")
    g4, d4, nm4, nv4 = _adamw_sharded(r_mo, mlstm_out_w[0], m_mlstm_out_w[0], v_mlstm_out_w[0], "adamw_mlstm_out")

    small_g = [g_meta, g_norm, g_conv_w, g_gate_b, g_head, g_final.reshape(1, d)]
    small_w = [meta_tokens, norm_w, conv_w[0], mlstm_gate_b, mlstm_head_norm_w, final_norm_w.reshape(1, d)]
    small_m = [m_meta_tokens, m_norm_w, m_conv_w[0], m_mlstm_gate_b, m_mlstm_head_norm_w, m_final_norm_w.reshape(1, d)]
    small_v = [v_meta_tokens, v_norm_w, v_conv_w[0], v_mlstm_gate_b, v_mlstm_head_norm_w, v_final_norm_w.reshape(1, d)]
    sd, snm, snv = _adamw_small(small_g, small_w, small_m, small_v, "adamw_small")

    def order(meta, norm, cin, cw, cout, min_, gb, hn, mout, fin):
        return (meta, norm, cin[None], cw[None], cout[None], min_[None], gb, hn, mout[None], fin.reshape(d))

    grads = order(g_meta, g_norm, g1, g_conv_w, g2, g3, g_gate_b, g_head, g4, g_final)
    deltas = order(sd[0], sd[1], d1, sd[2], d2, d3, sd[3], sd[4], d4, sd[5])
    new_m = order(snm[0], snm[1], nm1, snm[2], nm2, nm3, snm[3], snm[4], nm4, snm[5])
    new_v = order(snv[0], snv[1], nv1, snv[2], nv2, nv3, snv[3], snv[4], nv4, snv[5])
    return (loss, grad_x, *grads, *deltas, *new_m, *new_v)
```

```python
import functools

import jax
import jax.numpy as jnp
from jax import lax
from jax.experimental import pallas as pl
from jax.experimental.pallas import tpu as pltpu

F32 = jnp.float32
BF16 = jnp.bfloat16
MESH = pl.DeviceIdType.MESH

N_DEV = 8
N_META = 16
N_HEADS = 4
CHUNK = 64
CHUNKS_PER_STEP = 2
LEAD = 128
PAD_ROWS = LEAD - N_META
RMS_EPS = 1e-6
NEG = -1e30
LANES = 128
VMEM_LIMIT = 48 * 1024 * 1024

ADAM_LR = 0.001
ADAM_B1 = 0.9
ADAM_B2 = 0.999
ADAM_EPS = 1e-08
ADAM_WD = 0.01
ADAM_STEP = 10

HIGHEST = lax.Precision.HIGHEST


def _pick(n, target, mult):
    best = None
    for d in range(mult, min(n, target) + 1, mult):
        if n % d == 0:
            best = d
    return n if best is None else best


def _params(**kw):
    return pltpu.CompilerParams(vmem_limit_bytes=VMEM_LIMIT, **kw)


def _sigmoid(x):
    return 0.5 * jnp.tanh(0.5 * x) + 0.5


def _call(body, *, name, grid, in_specs, out_specs, out_shape, args, scratch_shapes=(), exchange=None):
    if exchange is None:
        return pl.pallas_call(
            body, name=name, grid=grid, in_specs=list(in_specs), out_specs=out_specs, out_shape=out_shape,
            scratch_shapes=list(scratch_shapes), compiler_params=_params())(*args)
    ex = exchange
    single = not isinstance(out_shape, (list, tuple))
    shapes = [out_shape] if single else list(out_shape)
    specs = [out_specs] if single else list(out_specs)
    n_in, n_out, n_scr = len(in_specs), len(shapes), len(scratch_shapes)
    n_ex_in, n_ex_out = len(ex.ins), len(ex.out_shapes)
    steps = 1
    for size in grid:
        steps *= size

    def wrapped(*refs):
        own_in, ex_in = refs[:n_in], refs[n_in:n_in + n_ex_in]
        at = n_in + n_ex_in
        own_out, ex_out = refs[at:at + n_out], refs[at + n_out:at + n_out + n_ex_out]
        at += n_out + n_ex_out
        own_scr, ex_scr = refs[at:at + n_scr], refs[at + n_scr:]
        step = 0
        for axis, size in enumerate(grid):
            step = step * size + pl.program_id(axis)

        @pl.when(step == 0)
        def _():
            ex.start(ex_in, ex_out, ex_scr)

        if ex.mid is not None:
            @pl.when(step == (3 * steps) // 4)
            def _():
                ex.mid(ex_in, ex_out, ex_scr)

        body(*own_in, *own_out, *own_scr)

        @pl.when(step == steps - 1)
        def _():
            ex.finish(ex_in, ex_out, ex_scr)

    any_spec = pl.BlockSpec(memory_space=pl.ANY)
    res = pl.pallas_call(
        wrapped, name=name, grid=grid,
        in_specs=list(in_specs) + [any_spec] * n_ex_in, out_specs=specs + [any_spec] * n_ex_out,
        out_shape=shapes + ex.out_shapes, scratch_shapes=list(scratch_shapes) + ex.scratch,
        compiler_params=_params())(*args, *ex.ins)
    return (res[0] if single else res[:n_out]), res[n_out:]


def _matmul(a, b, *, form, m, n, kdim, tm, tn, tk, out_dtype, name,
            a_spec=None, b_spec=None, out_spec=None, out_shape=None, residual=None, exchange=None):
    ni, nj, nk = m // tm, n // tn, kdim // tk
    assert ni * tm == m and nj * tn == n and nk * tk == kdim, (name, m, n, kdim, tm, tn, tk)
    if form == "nn":
        dn = (((1,), (0,)), ((), ()))
        a_def = pl.BlockSpec((tm, tk), lambda i, j, k: (i, k))
        b_def = pl.BlockSpec((tk, tn), lambda i, j, k: (k, j))
    elif form == "nt":
        dn = (((1,), (1,)), ((), ()))
        a_def = pl.BlockSpec((tm, tk), lambda i, j, k: (i, k))
        b_def = pl.BlockSpec((tn, tk), lambda i, j, k: (j, k))
    else:
        dn = (((0,), (0,)), ((), ()))
        a_def = pl.BlockSpec((tk, tm), lambda i, j, k: (k, i))
        b_def = pl.BlockSpec((tk, tn), lambda i, j, k: (k, j))
    a_spec = a_def if a_spec is None else a_spec
    b_spec = b_def if b_spec is None else b_spec
    o_spec = pl.BlockSpec((tm, tn), lambda i, j, k: (i, j)) if out_spec is None else out_spec
    has_res = residual is not None

    def body(*refs):
        a_ref, b_ref = refs[:2]
        r_ref = refs[2] if has_res else None
        o_ref = refs[2 + has_res]

        def product():
            return lax.dot_general(a_ref[...].astype(BF16), b_ref[...].astype(BF16), dn,
                                   preferred_element_type=F32)

        def finish(acc):
            if has_res:
                acc = acc + r_ref[...].astype(F32)
            o_ref[...] = acc.astype(o_ref.dtype)

        if nk == 1:
            finish(product())
        else:
            acc_ref = refs[3 + has_res]
            k = pl.program_id(2)

            @pl.when(k == 0)
            def _():
                acc_ref[...] = jnp.zeros_like(acc_ref)

            acc_ref[...] += product()

            @pl.when(k == nk - 1)
            def _():
                finish(acc_ref[...])

    in_specs = [a_spec, b_spec]
    args = [a, b]
    if has_res:
        in_specs.append(pl.BlockSpec((tm, tn), lambda i, j, k: (i, j)))
        args.append(residual)
    return _call(
        body, name=name, grid=(ni, nj, nk), in_specs=in_specs, out_specs=o_spec,
        out_shape=jax.ShapeDtypeStruct((m, n) if out_shape is None else out_shape, out_dtype),
        scratch_shapes=[] if nk == 1 else [pltpu.VMEM((tm, tn), F32)], args=args, exchange=exchange)


def _rms_fwd(h, w, name, exchange=None):
    t, d = h.shape
    tm = _pick(t, 416, 16)

    def body(h_ref, w_ref, u_ref):
        x = h_ref[...]
        r = lax.rsqrt(jnp.mean(x * x, axis=-1, keepdims=True) + RMS_EPS)
        u_ref[...] = ((x * r) * w_ref[...]).astype(BF16)

    return _call(
        body, name=name, grid=(t // tm,),
        in_specs=[pl.BlockSpec((tm, d), lambda i: (i, 0)), pl.BlockSpec((1, d), lambda i: (0, 0))],
        out_specs=pl.BlockSpec((tm, d), lambda i: (i, 0)),
        out_shape=jax.ShapeDtypeStruct((t, d), BF16), args=(h, w), exchange=exchange)


def _rms_bwd(h, w, du, dres, name):
    t, d = h.shape
    tm = _pick(t, 416, 16)

    def body(h_ref, w_ref, du_ref, dres_ref, dh_ref, dw_ref):
        i = pl.program_id(0)
        x = h_ref[...]
        g = du_ref[...].astype(F32)
        r = lax.rsqrt(jnp.mean(x * x, axis=-1, keepdims=True) + RMS_EPS)
        gw = g * w_ref[...]
        dot = jnp.mean(gw * x, axis=-1, keepdims=True)
        dh_ref[...] = dres_ref[...] + (r * gw - x * ((r * r * r) * dot))
        part = jnp.sum(g * (x * r), axis=0, keepdims=True)

        @pl.when(i == 0)
        def _():
            dw_ref[...] = jnp.zeros_like(dw_ref)

        dw_ref[...] += jnp.broadcast_to(part, dw_ref.shape)

    row = pl.BlockSpec((tm, d), lambda i: (i, 0))
    return pl.pallas_call(
        body, name=name, grid=(t // tm,),
        in_specs=[row, pl.BlockSpec((1, d), lambda i: (0, 0)), row, row],
        out_specs=[row, pl.BlockSpec((8, d), lambda i: (0, 0))],
        out_shape=[jax.ShapeDtypeStruct((t, d), F32), jax.ShapeDtypeStruct((8, d), F32)],
        compiler_params=_params(),
    )(h, w, du, dres)


def _rms_bwd_first(h, w, du, dres, name):
    t, d = h.shape
    tm = LEAD

    def body(h_ref, w_ref, du_ref, dres_ref, gx_ref, dmeta_ref, dw_ref):
        i = pl.program_id(0)
        x = h_ref[...]
        g = du_ref[...].astype(F32)
        r = lax.rsqrt(jnp.mean(x * x, axis=-1, keepdims=True) + RMS_EPS)
        gw = g * w_ref[...]
        dot = jnp.mean(gw * x, axis=-1, keepdims=True)
        dh = dres_ref[...] + (r * gw - x * ((r * r * r) * dot))
        part = jnp.sum(g * (x * r), axis=0, keepdims=True)

        @pl.when(i == 0)
        def _():
            dw_ref[...] = jnp.zeros_like(dw_ref)
            dmeta_ref[...] = dh[PAD_ROWS:LEAD, :]

        @pl.when(i > 0)
        def _():
            gx_ref[...] = dh

        dw_ref[...] += jnp.broadcast_to(part, dw_ref.shape)

    row = pl.BlockSpec((tm, d), lambda i: (i, 0))
    return pl.pallas_call(
        body, name=name, grid=(t // tm,),
        in_specs=[row, pl.BlockSpec((1, d), lambda i: (0, 0)), row, row],
        out_specs=[pl.BlockSpec((tm, d), lambda i: (jnp.maximum(i - 1, 0), 0)),
                   pl.BlockSpec((N_META, d), lambda i: (0, 0)), pl.BlockSpec((8, d), lambda i: (0, 0))],
        out_shape=[jax.ShapeDtypeStruct((t - LEAD, d), F32), jax.ShapeDtypeStruct((N_META, d), F32),
                   jax.ShapeDtypeStruct((8, d), F32)],
        compiler_params=_params(),
    )(h, w, du, dres)


def _final_loss(h, w, tgt, name):
    t, d = h.shape
    sub = LEAD
    per = _pick(t // sub, 5, 1)
    tm = per * sub

    def body(h_ref, w_ref, *rest):
        t_refs, (dh_ref, dw_ref, loss_ref) = rest[:per], rest[per:]
        i = pl.program_id(0)

        @pl.when(i == 0)
        def _():
            dw_ref[...] = jnp.zeros_like(dw_ref)
            loss_ref[...] = jnp.zeros_like(loss_ref)

        for q in range(per):
            sl = slice(q * sub, (q + 1) * sub)
            x = h_ref[sl, :]
            r = lax.rsqrt(jnp.mean(x * x, axis=-1, keepdims=True) + RMS_EPS)
            yn = x * r
            y = yn * w_ref[...]
            rows = i * tm + q * sub + lax.broadcasted_iota(jnp.int32, (sub, 1), 0)
            diff = jnp.where(rows >= LEAD, y - t_refs[q][...], 0.0)
            tile_loss = 0.5 * jnp.sum(jnp.mean(diff * diff, axis=-1, keepdims=True), axis=0, keepdims=True)
            dy = diff / d
            gw = dy * w_ref[...]
            dot = jnp.mean(gw * x, axis=-1, keepdims=True)
            dh_ref[sl, :] = r * gw - x * ((r * r * r) * dot)
            dw_ref[...] += jnp.broadcast_to(jnp.sum(dy * yn, axis=0, keepdims=True), dw_ref.shape)
            loss_ref[...] += jnp.broadcast_to(tile_loss, loss_ref.shape)

    row = pl.BlockSpec((tm, d), lambda i: (i, 0))
    piece = [pl.BlockSpec((sub, d), functools.partial(lambda i, q: (jnp.maximum(i * per + q - 1, 0), 0), q=q))
             for q in range(per)]
    return pl.pallas_call(
        body, name=name, grid=(t // tm,),
        in_specs=[row, pl.BlockSpec((1, d), lambda i: (0, 0))] + piece,
        out_specs=[row, pl.BlockSpec((8, d), lambda i: (0, 0)), pl.BlockSpec((8, LANES), lambda i: (0, 0))],
        out_shape=[jax.ShapeDtypeStruct((t, d), F32), jax.ShapeDtypeStruct((8, d), F32),
                   jax.ShapeDtypeStruct((8, LANES), F32)],
        compiler_params=_params(),
    )(h, w, *([tgt] * per))


def _conv_gate_fwd(p0, conv_w, name):
    t = p0.shape[0]
    ec = p0.shape[1] // 8
    tm = _pick(t, 416, 16)
    nt = t // tm

    def body(p_ref, w_ref, g_ref, y_ref, ext_ref):
        i = pl.program_id(1)
        bg = p_ref[:, 0:ec].astype(F32)
        cg = p_ref[:, ec:2 * ec].astype(F32)
        xin = p_ref[:, 2 * ec:3 * ec].astype(F32)
        z = p_ref[:, 3 * ec:4 * ec].astype(F32)
        a = cg * xin

        @pl.when(i == 0)
        def _():
            ext_ref[0:8, :] = jnp.zeros((8, ec), F32)

        ext_ref[8:8 + tm, :] = a
        y = (w_ref[0:1, :] * ext_ref[6:6 + tm, :] + w_ref[1:2, :] * ext_ref[7:7 + tm, :]
             + w_ref[2:3, :] * a)
        ext_ref[0:8, :] = a[tm - 8:tm, :]
        y_ref[...] = y.astype(BF16)
        g_ref[...] = ((z * _sigmoid(z)) * bg * y).astype(BF16)

    out = pl.BlockSpec((tm, ec), lambda j, i: (i, j))
    return pl.pallas_call(
        body, name=name, grid=(2, nt),
        in_specs=[pl.BlockSpec((tm, 4 * ec), lambda j, i: (i, j)), pl.BlockSpec((3, ec), lambda j, i: (0, j))],
        out_specs=[out, out],
        out_shape=[jax.ShapeDtypeStruct((t, 2 * ec), BF16)] * 2,
        scratch_shapes=[pltpu.VMEM((tm + 8, ec), F32)],
        compiler_params=_params(),
    )(p0, conv_w)


def _conv_gate_bwd(p0, y, dg, conv_w, name, exchange=None):
    t = p0.shape[0]
    ec = p0.shape[1] // 8
    tm = _pick(t, 208, 16)
    nt = t // tm

    def body(p_ref, y_ref, dg_ref, w_ref, dp_ref, dw_ref, ext_ref):
        i = pl.program_id(1)
        bg = p_ref[:, 0:ec].astype(F32)
        cg = p_ref[:, ec:2 * ec].astype(F32)
        xin = p_ref[:, 2 * ec:3 * ec].astype(F32)
        z = p_ref[:, 3 * ec:4 * ec].astype(F32)
        yv = y_ref[...].astype(F32)
        dgv = dg_ref[...].astype(F32)
        sig = _sigmoid(z)
        sz = z * sig
        dp_ref[:, 3 * ec:4 * ec] = (dgv * bg * yv * (sig * (1.0 + z * (1.0 - sig)))).astype(BF16)
        dp_ref[:, 0:ec] = (dgv * sz * yv).astype(BF16)
        dy = dgv * sz * bg

        @pl.when(i == 0)
        def _():
            ext_ref[tm:tm + 8, :] = jnp.zeros((8, ec), F32)
            dw_ref[...] = jnp.zeros_like(dw_ref)

        ext_ref[0:tm, :] = dy
        dy1 = ext_ref[1:tm + 1, :]
        dy2 = ext_ref[2:tm + 2, :]
        ext_ref[tm:tm + 8, :] = dy[0:8, :]
        da = w_ref[0:1, :] * dy2 + w_ref[1:2, :] * dy1 + w_ref[2:3, :] * dy
        dp_ref[:, ec:2 * ec] = (da * xin).astype(BF16)
        dp_ref[:, 2 * ec:3 * ec] = (da * cg).astype(BF16)
        a = cg * xin
        dw_ref[0:1, :] += jnp.sum(a * dy2, axis=0, keepdims=True)
        dw_ref[1:2, :] += jnp.sum(a * dy1, axis=0, keepdims=True)
        dw_ref[2:3, :] += jnp.sum(a * dy, axis=0, keepdims=True)

    rev = lambda j, i: (nt - 1 - i, j)
    return _call(
        body, name=name, grid=(2, nt),
        in_specs=[pl.BlockSpec((tm, 4 * ec), rev), pl.BlockSpec((tm, ec), rev), pl.BlockSpec((tm, ec), rev),
                  pl.BlockSpec((3, ec), lambda j, i: (0, j))],
        out_specs=[pl.BlockSpec((tm, 4 * ec), rev), pl.BlockSpec((8, ec), lambda j, i: (0, j))],
        out_shape=[jax.ShapeDtypeStruct((t, 8 * ec), BF16), jax.ShapeDtypeStruct((8, 2 * ec), F32)],
        scratch_shapes=[pltpu.VMEM((tm + 8, ec), F32)], args=(p0, y, dg, conv_w), exchange=exchange)


def _log_sigmoid(x):
    return jnp.minimum(x, 0.0) - jnp.log(1.0 + jnp.exp(-jnp.abs(x)))


def _gates_fwd(graw, graw_t, bias_row, bias_col, name):
    t = graw.shape[0]
    tm = _pick(t, 640, 128)
    cpt = tm // CHUNK
    nh = N_HEADS

    def body(g_ref, gt_ref, br_ref, bc_ref, colf_ref, rowf_ref):
        i = pl.program_id(0)
        gc = g_ref[...] + br_ref[...]
        rows = i * tm + lax.broadcasted_iota(jnp.int32, (tm, 1), 0)
        live = rows >= PAD_ROWS
        lf = jnp.where(live, _log_sigmoid(gc), 0.0)
        li = jnp.where(live, gc, NEG)
        gt = gt_ref[...] + bc_ref[...]
        cols = i * tm + lax.broadcasted_iota(jnp.int32, (1, tm), 1)
        live_t = cols >= PAD_ROWS
        lf_t = jnp.where(live_t, _log_sigmoid(gt), 0.0)
        li_t = jnp.where(live_t, gt, NEG)
        ri = lax.broadcasted_iota(jnp.int32, (CHUNK, CHUNK), 0)
        ci = lax.broadcasted_iota(jnp.int32, (CHUNK, CHUNK), 1)
        lower = (ri >= ci).astype(F32)
        upper = (ri <= ci).astype(F32)
        lane = lax.broadcasted_iota(jnp.int32, (CHUNK, LANES), 1)
        sub = lax.broadcasted_iota(jnp.int32, (8, CHUNK), 0)
        for c in range(cpt):
            sl = slice(c * CHUNK, (c + 1) * CHUNK)
            b_all = jnp.dot(lower, lf[sl, :], precision=HIGHEST, preferred_element_type=F32)
            bt_all = jnp.dot(lf_t[:, sl], upper, precision=HIGHEST, preferred_element_type=F32)
            for h in range(nh):
                b = b_all[:, nh + h:nh + h + 1]
                r = li[sl, h:h + 1] - b
                pre = gc[sl, nh + h:nh + h + 1]
                colf_ref[h, sl, :] = jnp.where(lane == 0, b, jnp.where(lane == 1, r, jnp.where(lane == 2, pre, 0.0)))
                b_row = bt_all[nh + h:nh + h + 1, :]
                r_row = li_t[h:h + 1, sl] - b_row
                rowf_ref[h, c, :, :] = jnp.where(sub == 0, r_row, jnp.where(sub == 1, b_row, 0.0))

    return pl.pallas_call(
        body, name=name, grid=(t // tm,),
        in_specs=[pl.BlockSpec((tm, LANES), lambda i: (i, 0)), pl.BlockSpec((16, tm), lambda i: (0, i)),
                  pl.BlockSpec((1, LANES), lambda i: (0, 0)), pl.BlockSpec((16, 1), lambda i: (0, 0))],
        out_specs=[pl.BlockSpec((nh, tm, LANES), lambda i: (0, i, 0)),
                   pl.BlockSpec((nh, cpt, 8, CHUNK), lambda i: (0, i, 0, 0))],
        out_shape=[jax.ShapeDtypeStruct((nh, t, LANES), F32),
                   jax.ShapeDtypeStruct((nh, t // CHUNK, 8, CHUNK), F32)],
        compiler_params=_params(),
    )(graw, graw_t, bias_row, bias_col)


def _gates_bwd(gstat, colf, dp1, name):
    nh, t, _ = gstat.shape
    tm = _pick(t, 640, 128)
    cpt = tm // CHUNK
    nt = t // tm
    nc = t // CHUNK
    gate_block = dp1.shape[1] // LANES - 1

    def body(gs_ref, nx_ref, colf_ref, dp_any, dg_ref, db_ref):
        i = pl.program_id(0)
        ri = lax.broadcasted_iota(jnp.int32, (CHUNK, CHUNK), 0)
        ci = lax.broadcasted_iota(jnp.int32, (CHUNK, CHUNK), 1)
        upper = (ri <= ci).astype(F32)
        lane = lax.broadcasted_iota(jnp.int32, (CHUNK, LANES), 1)
        total = jnp.zeros((1, LANES), F32)
        for c in range(cpt):
            sl = slice(c * CHUNK, (c + 1) * CHUNK)
            rows = i * tm + c * CHUNK + lax.broadcasted_iota(jnp.int32, (CHUNK, 1), 0)
            live = rows >= PAD_ROWS
            acc = jnp.zeros((CHUNK, LANES), F32)
            for h in range(nh):
                blk = gs_ref[h, sl, :]
                rev = jnp.dot(upper, blk, precision=HIGHEST, preferred_element_type=F32)
                if c + 1 < cpt:
                    carry = gs_ref[h, (c + 1) * CHUNK:(c + 1) * CHUNK + 1, 2:3]
                else:
                    carry = jnp.where(i == nt - 1, 0.0, nx_ref[h, 0:1, 2:3])
                dlogf = rev[:, 0:1] + carry
                pre = colf_ref[h, sl, 2:3]
                dgf = jnp.where(live, dlogf * (1.0 - _sigmoid(pre)), 0.0)
                dgi = jnp.where(live, blk[:, 1:2], 0.0)
                acc = acc + jnp.where(lane == h, dgi, 0.0) + jnp.where(lane == nh + h, dgf, 0.0)
            dg_ref[sl, :] = acc.astype(BF16)
            total = total + jnp.sum(acc, axis=0, keepdims=True)

        @pl.when(i == 0)
        def _():
            db_ref[...] = jnp.zeros_like(db_ref)

        db_ref[...] += jnp.broadcast_to(total, db_ref.shape)

    return pl.pallas_call(
        body, name=name, grid=(nt,),
        in_specs=[pl.BlockSpec((nh, tm, LANES), lambda i: (0, i, 0)),
                  pl.BlockSpec((nh, CHUNK, LANES), lambda i: (0, jnp.minimum((i + 1) * cpt, nc - 1), 0)),
                  pl.BlockSpec((nh, tm, LANES), lambda i: (0, i, 0)),
                  pl.BlockSpec(memory_space=pl.ANY)],
        out_specs=[pl.BlockSpec((tm, LANES), lambda i: (i, gate_block)), pl.BlockSpec((8, LANES), lambda i: (0, 0))],
        out_shape=[jax.ShapeDtypeStruct(dp1.shape, dp1.dtype), jax.ShapeDtypeStruct((8, LANES), F32)],
        input_output_aliases={3: 0},
        compiler_params=_params(),
    )(gstat, gstat, colf, dp1)


NT_DIMS = (((1,), (1,)), ((), ()))
TN_DIMS = (((0,), (0,)), ((), ()))


def _dot(a, b):
    return jnp.dot(a.astype(BF16), b.astype(BF16), preferred_element_type=F32)


def _dot_nt(a, b):
    return lax.dot_general(a.astype(BF16), b.astype(BF16), NT_DIMS, preferred_element_type=F32)


def _dot_tn(a, b):
    return lax.dot_general(a.astype(BF16), b.astype(BF16), TN_DIMS, preferred_element_type=F32)


def _chunk_gates(colf_ref, rowf_ref, m_prev):
    b = colf_ref[:, 0:1]
    rcol = colf_ref[:, 1:2]
    rrow = rowf_ref[0:1, :]
    ri = lax.broadcasted_iota(jnp.int32, (CHUNK, CHUNK), 0)
    ci = lax.broadcasted_iota(jnp.int32, (CHUNK, CHUNK), 1)
    log_d = jnp.where(ri >= ci, b + rrow, NEG)
    m_row = jnp.maximum(b + m_prev, jnp.max(log_d, axis=-1, keepdims=True))
    dmat = jnp.exp(log_d - m_row)
    inter = jnp.exp(b + m_prev - m_row)
    b_last = b[CHUNK - 1:CHUNK, :]
    log_w = rcol + b_last
    m_new = jnp.maximum(b_last + m_prev, jnp.max(log_w, axis=0, keepdims=True))
    decay = jnp.exp(b_last + m_prev - m_new)
    w = jnp.exp(log_w - m_new)
    return m_row, dmat, inter, m_new, decay, w


def _mlstm_fwd(p1, colf, rowf, head_w, dk, dv, name):
    t = p1.shape[0]
    nh = N_HEADS
    nc = t // CHUNK
    hw = 2 * dk + 3 * dv
    cps = CHUNKS_PER_STEP if nc % CHUNKS_PER_STEP == 0 else 1
    rows_per_step = cps * CHUNK

    def body(p_ref, colf_ref, rowf_ref, w_ref, hm_ref, hh_ref, stat_ref, cs_ref, ns_ref, c_ref, n_ref, m_ref):
        c = pl.program_id(0)

        @pl.when(c == 0)
        def _():
            c_ref[...] = jnp.zeros_like(c_ref)
            n_ref[...] = jnp.zeros_like(n_ref)
            m_ref[...] = jnp.zeros_like(m_ref)

        for cc in range(cps):
            rows = pl.ds(cc * CHUNK, CHUNK)
            for h in range(nh):
                cols = pl.ds(h * dv, dv)
                head(p_ref.at[rows, pl.ds(h * hw, hw)], colf_ref.at[h, rows], rowf_ref.at[h, cc], w_ref.at[:, cols],
                     hm_ref.at[rows, cols], hh_ref.at[rows, cols], stat_ref.at[h, rows], cs_ref.at[h, cc],
                     ns_ref.at[h, cc], c_ref.at[h], n_ref.at[h], m_ref.at[h])

    def head(p_ref, colf_ref, rowf_ref, w_ref, hm_ref, hh_ref, stat_ref, cs_ref, ns_ref, c_ref, n_ref, m_ref):
        m_prev = m_ref[...]
        n_prev = n_ref[...]
        c_prev = c_ref[...]
        cs_ref[...] = c_prev.astype(BF16)
        sub = lax.broadcasted_iota(jnp.int32, (8, dk), 0)
        ns_ref[...] = jnp.where(sub == 0, n_prev, jnp.where(sub == 1, m_prev, 0.0))

        q = p_ref[:, 0:dk]
        k = p_ref[:, dk:2 * dk]
        v = p_ref[:, 2 * dk:2 * dk + dv]
        o = p_ref[:, 2 * dk + dv:2 * dk + 2 * dv].astype(F32)
        z = p_ref[:, 2 * dk + 2 * dv:2 * dk + 3 * dv].astype(F32)
        m_row, dmat, inter, m_new, decay, w = _chunk_gates(colf_ref, rowf_ref, m_prev)
        s = _dot_nt(q, k) * dmat
        num = _dot(s, v) + inter * _dot(q, c_prev)
        den = jnp.sum(s, axis=-1, keepdims=True) + inter * jnp.sum(q.astype(F32) * n_prev, axis=-1, keepdims=True)
        denom = jnp.maximum(jnp.abs(den), jnp.exp(-m_row))
        hh = num * (1.0 / denom)
        hh_ref[...] = hh
        lane = lax.broadcasted_iota(jnp.int32, (CHUNK, LANES), 1)
        stat_ref[...] = jnp.where(lane == 0, den, 0.0)

        c_ref[...] = decay * c_prev + _dot_tn(k, w * v.astype(F32))
        n_ref[...] = decay * n_prev + jnp.sum(w * k.astype(F32), axis=0, keepdims=True)
        m_ref[...] = m_new

        r = lax.rsqrt(jnp.mean(hh * hh, axis=-1, keepdims=True) + RMS_EPS)
        hn = (hh * r) * w_ref[...]
        hm_ref[...] = (hn * _sigmoid(o) * (z * _sigmoid(z))).astype(BF16)

    return pl.pallas_call(
        body, name=name, grid=(nc // cps,),
        in_specs=[pl.BlockSpec((rows_per_step, nh * hw), lambda c: (c, 0)),
                  pl.BlockSpec((nh, rows_per_step, LANES), lambda c: (0, c, 0)),
                  pl.BlockSpec((nh, cps, 8, CHUNK), lambda c: (0, c, 0, 0)),
                  pl.BlockSpec((1, nh * dv), lambda c: (0, 0))],
        out_specs=[pl.BlockSpec((rows_per_step, nh * dv), lambda c: (c, 0)),
                   pl.BlockSpec((rows_per_step, nh * dv), lambda c: (c, 0)),
                   pl.BlockSpec((nh, rows_per_step, LANES), lambda c: (0, c, 0)),
                   pl.BlockSpec((nh, cps, dk, dv), lambda c: (0, c, 0, 0)),
                   pl.BlockSpec((nh, cps, 8, dk), lambda c: (0, c, 0, 0))],
        out_shape=[jax.ShapeDtypeStruct((t, nh * dv), BF16), jax.ShapeDtypeStruct((t, nh * dv), F32),
                   jax.ShapeDtypeStruct((nh, t, LANES), F32),
                   jax.ShapeDtypeStruct((nh, nc, dk, dv), BF16), jax.ShapeDtypeStruct((nh, nc, 8, dk), F32)],
        scratch_shapes=[pltpu.VMEM((nh, dk, dv), F32), pltpu.VMEM((nh, 1, dk), F32), pltpu.VMEM((nh, 1, 1), F32)],
        compiler_params=_params(),
    )(p1, colf, rowf, head_w)


def _mlstm_bwd(p1, colf, rowf, head_w, hh, stat, csave, nsave, dhm, dk, dv, name, exchange=None):
    t = p1.shape[0]
    nh = N_HEADS
    nc = t // CHUNK
    hw = 2 * dk + 3 * dv
    cps = CHUNKS_PER_STEP if nc % CHUNKS_PER_STEP == 0 else 1
    rows_per_step = cps * CHUNK

    def body(p_ref, colf_ref, rowf_ref, w_ref, hh_ref, stat_ref, cs_ref, ns_ref, dhm_ref,
             dp_ref, gs_ref, dw_ref, dc_ref, dn_ref):
        c = pl.program_id(0)

        @pl.when(c == 0)
        def _():
            dc_ref[...] = jnp.zeros_like(dc_ref)
            dn_ref[...] = jnp.zeros_like(dn_ref)
            dw_ref[...] = jnp.zeros_like(dw_ref)

        for cc in reversed(range(cps)):
            rows = pl.ds(cc * CHUNK, CHUNK)
            for h in range(nh):
                cols = pl.ds(h * dv, dv)
                head(p_ref.at[rows, pl.ds(h * hw, hw)], colf_ref.at[h, rows], rowf_ref.at[h, cc], w_ref.at[:, cols],
                     hh_ref.at[rows, cols], stat_ref.at[h, rows], cs_ref.at[h, cc], ns_ref.at[h, cc],
                     dhm_ref.at[rows, cols], dp_ref.at[rows, pl.ds(h * hw, hw)], gs_ref.at[h, rows],
                     dw_ref.at[h], dc_ref.at[h], dn_ref.at[h])

    def head(p_ref, colf_ref, rowf_ref, w_ref, hh_ref, stat_ref, cs_ref, ns_ref, dhm_ref,
             dp_ref, gs_ref, dw_ref, dc_ref, dn_ref):
        q = p_ref[:, 0:dk]
        k = p_ref[:, dk:2 * dk]
        v = p_ref[:, 2 * dk:2 * dk + dv]
        o = p_ref[:, 2 * dk + dv:2 * dk + 2 * dv].astype(F32)
        z = p_ref[:, 2 * dk + 2 * dv:2 * dk + 3 * dv].astype(F32)
        qf = q.astype(F32)
        kf = k.astype(F32)
        n_prev = ns_ref[0:1, :]
        m_prev = ns_ref[1:2, 0:1]
        c_prev = cs_ref[...]
        m_row, dmat, inter, m_new, decay, w = _chunk_gates(colf_ref, rowf_ref, m_prev)

        hh = hh_ref[...]
        dhm_v = dhm_ref[...].astype(F32)
        so = _sigmoid(o)
        sg = _sigmoid(z)
        sz = z * sg
        r = lax.rsqrt(jnp.mean(hh * hh, axis=-1, keepdims=True) + RMS_EPS)
        hn = (hh * r) * w_ref[...]
        dhn = dhm_v * so * sz
        dp_ref[:, 2 * dk + dv:2 * dk + 2 * dv] = (dhm_v * hn * sz * (so * (1.0 - so))).astype(BF16)
        dp_ref[:, 2 * dk + 2 * dv:2 * dk + 3 * dv] = (dhm_v * hn * so * (sg * (1.0 + z * (1.0 - sg)))).astype(BF16)
        dw_ref[...] += jnp.broadcast_to(jnp.sum(dhn * (hh * r), axis=0, keepdims=True), dw_ref.shape)
        gwn = dhn * w_ref[...]
        dhh = r * gwn - hh * ((r * r * r) * jnp.mean(gwn * hh, axis=-1, keepdims=True))

        den = stat_ref[:, 0:1]
        floor = jnp.exp(-m_row)
        denom = jnp.maximum(jnp.abs(den), floor)
        inv = 1.0 / denom
        dnum = dhh * inv
        hdot = jnp.sum(dhh * hh, axis=-1, keepdims=True)
        dden = jnp.where(jnp.abs(den) > floor, -(hdot * inv) * jnp.sign(den), 0.0)
        s = _dot_nt(q, k) * dmat
        dqk = (_dot_nt(dnum, v) + dden) * dmat
        dc_new = dc_ref[...]
        dn_new = dn_ref[...]
        idd = inter * dden
        dq = _dot(dqk, k) + inter * _dot_nt(dnum, c_prev) + idd * n_prev
        dkv = _dot_tn(dqk, q) + w * (_dot_nt(v, dc_new) + dn_new)
        dvv = _dot_tn(s, dnum) + w * _dot(k, dc_new)
        dp_ref[:, 0:dk] = dq.astype(BF16)
        dp_ref[:, dk:2 * dk] = dkv.astype(BF16)
        dp_ref[:, 2 * dk:2 * dk + dv] = dvv.astype(BF16)
        qdq = jnp.sum(qf * dq, axis=-1, keepdims=True)
        kdk = jnp.sum(kf * dkv, axis=-1, keepdims=True)
        dc_prev = decay * dc_new + _dot_tn(inter * qf, dnum)
        dn_prev = decay * dn_new + jnp.sum(idd * qf, axis=0, keepdims=True)
        dc_ref[...] = dc_prev
        dn_ref[...] = dn_prev
        cross = (jnp.sum(jnp.sum(c_prev.astype(F32) * dc_prev, axis=-1, keepdims=True), axis=0, keepdims=True)
                 + jnp.sum(n_prev * dn_prev, axis=-1, keepdims=True))
        lane = lax.broadcasted_iota(jnp.int32, (CHUNK, LANES), 1)
        gs_ref[...] = jnp.where(lane == 0, qdq - kdk, jnp.where(lane == 1, kdk, jnp.where(lane == 2, cross, 0.0)))

    ns = nc // cps
    rc = lambda c: (ns - 1 - c, 0)
    rc3 = lambda c: (0, ns - 1 - c, 0)
    rc4 = lambda c: (0, ns - 1 - c, 0, 0)
    return _call(
        body, name=name, grid=(ns,),
        in_specs=[pl.BlockSpec((rows_per_step, nh * hw), rc),
                  pl.BlockSpec((nh, rows_per_step, LANES), rc3),
                  pl.BlockSpec((nh, cps, 8, CHUNK), rc4),
                  pl.BlockSpec((1, nh * dv), lambda c: (0, 0)),
                  pl.BlockSpec((rows_per_step, nh * dv), rc),
                  pl.BlockSpec((nh, rows_per_step, LANES), rc3),
                  pl.BlockSpec((nh, cps, dk, dv), rc4),
                  pl.BlockSpec((nh, cps, 8, dk), rc4),
                  pl.BlockSpec((rows_per_step, nh * dv), rc)],
        out_specs=[pl.BlockSpec((rows_per_step, nh * hw), rc),
                   pl.BlockSpec((nh, rows_per_step, LANES), rc3),
                   pl.BlockSpec((nh, 8, dv), lambda c: (0, 0, 0))],
        out_shape=[jax.ShapeDtypeStruct((t, nh * hw + LANES), BF16), jax.ShapeDtypeStruct((nh, t, LANES), F32),
                   jax.ShapeDtypeStruct((nh, 8, dv), F32)],
        scratch_shapes=[pltpu.VMEM((nh, dk, dv), F32), pltpu.VMEM((nh, 1, dk), F32)],
        args=(p1, colf, rowf, head_w, hh, stat, csave, nsave, dhm), exchange=exchange)


def _position():
    return lax.axis_index("x"), lax.axis_index("y"), lax.axis_index("c")


class _Exchange:
    def __init__(self, ins, out_shapes, start, mid, finish):
        n = len(ins)
        self.ins, self.out_shapes = list(ins), list(out_shapes)
        self.start, self.mid, self.finish = start, mid, finish
        self.scratch = [pltpu.SemaphoreType.DMA((n, 7)), pltpu.SemaphoreType.DMA((n, 7)),
                        pltpu.SemaphoreType.DMA((n,))]


def _gather_exchange(shards):
    n = len(shards)

    def plan(ins, outs, sems):
        send_sems, recv_sems, local_sems = sems
        x, y, c = _position()
        me, sibling = (x, y, c), (x, y, 1 - c)
        chips = [(1 - x, y), (x, 1 - y), (1 - x, 1 - y)]

        def copy(a, k, block, to, src=None):
            px, py, pc = block
            dst = outs[a].at[4 * px + 2 * py + pc]
            return pltpu.make_async_remote_copy(
                src_ref=dst if src is None else src, dst_ref=dst,
                send_sem=send_sems.at[a, k], recv_sem=recv_sems.at[a, k],
                device_id=to, device_id_type=MESH)

        def mine():
            return [pltpu.make_async_copy(ins[a], outs[a].at[4 * x + 2 * y + c], local_sems.at[a])
                    for a in range(n)]

        def first():
            out = []
            for a in range(n):
                out.append(copy(a, 0, me, sibling, src=ins[a]))
                out += [copy(a, 1 + j, me, (*chip, c), src=ins[a]) for j, chip in enumerate(chips)]
            return out

        def ici_in():
            return [copy(a, 1 + j, (*chip, c), me) for j, chip in enumerate(chips) for a in range(n)]

        def passed():
            return [copy(a, 4 + j, (*chip, c), sibling) for j, chip in enumerate(chips) for a in range(n)]

        def d2d_in():
            return ([copy(a, 0, sibling, me) for a in range(n)]
                    + [copy(a, 4 + j, (*chip, 1 - c), me) for j, chip in enumerate(chips) for a in range(n)])

        return mine, first, ici_in, passed, d2d_in

    def start(ins, outs, sems):
        mine, first, _, _, _ = plan(ins, outs, sems)
        for cp in mine() + first():
            cp.start()

    def mid(ins, outs, sems):
        _, _, ici_in, passed, _ = plan(ins, outs, sems)
        for arrived, onward in zip(ici_in(), passed()):
            arrived.wait_recv()
            onward.start()

    def finish(ins, outs, sems):
        mine, first, _, passed, d2d_in = plan(ins, outs, sems)
        for cp in d2d_in():
            cp.wait_recv()
        for cp in first() + passed():
            cp.wait_send()
        for cp in mine():
            cp.wait()

    shapes = [jax.ShapeDtypeStruct((N_DEV,) + s.shape, s.dtype) for s in shards]
    return _Exchange(shards, shapes, start, mid, finish)


def _scatter_exchange(fulls):
    n = len(fulls)

    def plan(ins, outs, sems):
        send_sems, recv_sems, local_sems = sems
        x, y, c = _position()
        my_slot = 4 * x + 2 * y + c

        def mine():
            return [pltpu.make_async_copy(ins[a].at[my_slot], outs[a].at[my_slot], local_sems.at[a])
                    for a in range(n)]

        def remote(arriving):
            out = []
            for kk in (1, 2, 4, 6, 3, 5, 7):
                kx, ky, kc = (kk >> 2) & 1, (kk >> 1) & 1, kk & 1
                px = 1 - x if kx else x
                py = 1 - y if ky else y
                pc = 1 - c if kc else c
                peer_slot = 4 * px + 2 * py + pc
                for a in range(n):
                    out.append(pltpu.make_async_remote_copy(
                        src_ref=ins[a].at[peer_slot], dst_ref=outs[a].at[peer_slot if arriving else my_slot],
                        send_sem=send_sems.at[a, kk - 1], recv_sem=recv_sems.at[a, kk - 1],
                        device_id=(px, py, pc), device_id_type=MESH))
            return out

        return mine, remote

    def start(ins, outs, sems):
        mine, remote = plan(ins, outs, sems)
        for cp in mine() + remote(False):
            cp.start()

    def finish(ins, outs, sems):
        mine, remote = plan(ins, outs, sems)
        for cp in remote(True):
            cp.wait_recv()
        for cp in remote(False):
            cp.wait_send()
        for cp in mine():
            cp.wait()

    shapes = [jax.ShapeDtypeStruct(f.shape, f.dtype) for f in fulls]
    return _Exchange(fulls, shapes, start, None, finish)


def _run_exchange(ex, name):
    n_in, n_out = len(ex.ins), len(ex.out_shapes)

    def body(*refs):
        ins, outs, sems = refs[:n_in], refs[n_in:n_in + n_out], refs[n_in + n_out:]
        ex.start(ins, outs, sems)
        if ex.mid is not None:
            ex.mid(ins, outs, sems)
        ex.finish(ins, outs, sems)

    any_spec = pl.BlockSpec(memory_space=pl.ANY)
    return pl.pallas_call(
        body, name=name,
        in_specs=[any_spec] * n_in, out_specs=[any_spec] * n_out,
        out_shape=ex.out_shapes, scratch_shapes=ex.scratch,
    )(*ex.ins)


def _natural_to_head_major(n0, dk, dv):
    nh = N_HEADS
    qk, e, hw = nh * dk, nh * dv, 2 * dk + 3 * dv
    if n0 < qk:
        h, off = divmod(n0, dk)
        return h * hw + off, True
    n1 = n0 - qk
    if n1 < qk:
        h, off = divmod(n1, dk)
        return h * hw + dk + off, False
    part, n3 = divmod(n1 - qk, e)
    h, off = divmod(n3, dv)
    return h * hw + 2 * dk + part * dv + off, False


def _mlstm_weight_layout(w_blocks, dk, dv, name):
    _, d, blk = w_blocks.shape
    nh = N_HEADS
    hw = 2 * dk + 3 * dv
    n_tiles = nh * hw // LANES
    tiles_per_block = d // LANES
    tr = _pick(d, 128, 16)
    scale = dk ** -0.5

    def body(w_ref, p_ref, g_ref):
        lane = lax.broadcasted_iota(jnp.int32, (tr, LANES), 1)

        def tile(s, r):
            return w_ref[s, :, r * LANES:(r + 1) * LANES].astype(F32)

        def last_col(s):
            return w_ref[s, :, d:d + 1].astype(F32)

        for tn in range(n_tiles):
            s, r = divmod(tn, tiles_per_block)
            if s == 0:
                val = tile(0, r)
            elif r == 0:
                from_prev = pltpu.roll(tile(s - 1, tiles_per_block - 1), s - 1, 1)
                from_here = pltpu.roll(tile(s, 0), s, 1)
                val = jnp.where(lane < s - 1, from_prev, jnp.where(lane == s - 1, last_col(s - 1), from_here))
            else:
                slab = jnp.concatenate([tile(s, r - 1), tile(s, r)], axis=1)
                val = pltpu.roll(slab, s, 1)[:, LANES:]
            at, is_q = _natural_to_head_major(tn * LANES, dk, dv)
            if is_q:
                val = val * scale
            p_ref[:, at:at + LANES] = val.astype(BF16)
        n_gate = 2 * nh
        s = N_DEV - 1
        gates = pltpu.roll(tile(s, tiles_per_block - 1), n_gate - 1, 1)
        gates = jnp.where(lane < n_gate - 1, gates, jnp.where(lane == n_gate - 1, last_col(s), 0.0)).astype(BF16)
        g_ref[...] = gates
        p_ref[:, nh * hw:nh * hw + LANES] = gates

    return pl.pallas_call(
        body, name=name, grid=(d // tr,),
        in_specs=[pl.BlockSpec((N_DEV, tr, blk), lambda i: (0, i, 0))],
        out_specs=[pl.BlockSpec((tr, nh * hw + LANES), lambda i: (i, 0)), pl.BlockSpec((tr, LANES), lambda i: (i, 0))],
        out_shape=[jax.ShapeDtypeStruct((d, nh * hw + LANES), BF16), jax.ShapeDtypeStruct((d, LANES), BF16)],
        compiler_params=_params(),
    )(w_blocks)


def _mlstm_grad_layout(g_p1, dk, dv, name):
    d = g_p1.shape[0]
    nh = N_HEADS
    hw = 2 * dk + 3 * dv
    n_tiles = nh * hw // LANES
    tiles_per_block = d // LANES
    tr = _pick(d, 128, 16)
    scale = dk ** -0.5

    def body(p_ref, o_ref):
        def natural(tn):
            if tn == n_tiles:
                return p_ref[:, nh * hw:nh * hw + LANES].astype(F32)
            at, is_q = _natural_to_head_major(tn * LANES, dk, dv)
            val = p_ref[:, at:at + LANES].astype(F32)
            return val * scale if is_q else val

        for s in range(N_DEV):
            for r in range(tiles_per_block):
                tn = s * tiles_per_block + r
                if s == 0:
                    val = natural(tn)
                else:
                    slab = jnp.concatenate([natural(tn), natural(tn + 1)], axis=1)
                    val = pltpu.roll(slab, 2 * LANES - s, 1)[:, :LANES]
                o_ref[s, :, r * LANES:(r + 1) * LANES] = val.astype(BF16)
            o_ref[s, :, d:d + 1] = natural((s + 1) * tiles_per_block)[:, s:s + 1].astype(BF16)

    return pl.pallas_call(
        body, name=name, grid=(d // tr,),
        in_specs=[pl.BlockSpec((tr, nh * hw + LANES), lambda i: (i, 0))],
        out_specs=pl.BlockSpec((N_DEV, tr, d + 1), lambda i: (0, i, 0)),
        out_shape=jax.ShapeDtypeStruct((N_DEV, d, d + 1), BF16),
        compiler_params=_params(),
    )(g_p1)


def _adamw_math(w, g, m, v):
    m = ADAM_B1 * m + (1.0 - ADAM_B1) * g
    v = ADAM_B2 * v + (1.0 - ADAM_B2) * (g * g)
    m_hat = m / (1.0 - ADAM_B1 ** ADAM_STEP)
    v_hat = v / (1.0 - ADAM_B2 ** ADAM_STEP)
    delta = -ADAM_LR * (m_hat / (jnp.sqrt(v_hat) + ADAM_EPS) + ADAM_WD * w)
    return delta, m, v


def _adamw_sharded(parts, w, m, v, name):
    _, r, c = parts.shape
    tr = _pick(r, 128, 8)

    def body(p_ref, w_ref, m_ref, v_ref, g_ref, d_ref, nm_ref, nv_ref):
        g = p_ref[0].astype(F32)
        for s in range(1, N_DEV):
            g = g + p_ref[s].astype(F32)
        delta, m_new, v_new = _adamw_math(w_ref[...], g, m_ref[...], v_ref[...])
        g_ref[...] = g
        d_ref[...] = delta
        nm_ref[...] = m_new
        nv_ref[...] = v_new

    blk = pl.BlockSpec((tr, c), lambda i: (i, 0))
    return pl.pallas_call(
        body, name=name, grid=(r // tr,),
        in_specs=[pl.BlockSpec((N_DEV, tr, c), lambda i: (0, i, 0)), blk, blk, blk],
        out_specs=[blk] * 4,
        out_shape=[jax.ShapeDtypeStruct((r, c), F32)] * 4,
        compiler_params=_params(),
    )(parts, w, m, v)


def _sum_devices(parts, name):
    _, r, c = parts.shape

    def body(p_ref, o_ref):
        g = p_ref[0]
        for s in range(1, N_DEV):
            g = g + p_ref[s]
        o_ref[...] = g

    return pl.pallas_call(
        body, name=name, out_shape=jax.ShapeDtypeStruct((r, c), F32), compiler_params=_params(),
    )(parts)


def _adamw_small(gs, ws, ms, vs, name):
    n = len(gs)

    def body(*refs):
        g_refs, w_refs, m_refs, v_refs = refs[:n], refs[n:2 * n], refs[2 * n:3 * n], refs[3 * n:4 * n]
        d_refs, nm_refs, nv_refs = refs[4 * n:5 * n], refs[5 * n:6 * n], refs[6 * n:7 * n]
        for a in range(n):
            delta, m_new, v_new = _adamw_math(w_refs[a][...], g_refs[a][...], m_refs[a][...], v_refs[a][...])
            d_refs[a][...] = delta
            nm_refs[a][...] = m_new
            nv_refs[a][...] = v_new

    shapes = [jax.ShapeDtypeStruct(w.shape, F32) for w in ws]
    outs = pl.pallas_call(
        body, name=name, out_shape=shapes * 3, compiler_params=_params(),
    )(*gs, *ws, *ms, *vs)
    return outs[:n], outs[n:2 * n], outs[2 * n:]


def _pad_rows(a, rows):
    return jnp.pad(a, ((0, rows - a.shape[0]), (0, 0)))


def kernel(x, meta_tokens, norm_w, conv_in_w, conv_w, conv_out_w, mlstm_in_w, mlstm_gate_b, mlstm_head_norm_w, mlstm_out_w, final_norm_w, loss_target, m_meta_tokens, m_norm_w, m_conv_in_w, m_conv_w, m_conv_out_w, m_mlstm_in_w, m_mlstm_gate_b, m_mlstm_head_norm_w, m_mlstm_out_w, m_final_norm_w, v_meta_tokens, v_norm_w, v_conv_in_w, v_conv_w, v_conv_out_w, v_mlstm_in_w, v_mlstm_gate_b, v_mlstm_head_norm_w, v_mlstm_out_w, v_final_norm_w):
    seq, d = x.shape[1], x.shape[2]
    t = seq + LEAD
    e = 2 * d
    ec = e // 2
    nh = N_HEADS
    dv = e // nh
    dk = dv // 2
    qk = nh * dk
    hw = 2 * dk + 3 * dv
    n_in = 2 * qk + 3 * e + 2 * nh
    n_in_s = n_in // N_DEV
    me = 4 * lax.axis_index("x") + 2 * lax.axis_index("y") + lax.axis_index("c")
    tm = _pick(t, 832, 16)
    tm_in = _pick(t, 1664, 16)
    tkw = _pick(t, 2080, 16)

    small = jnp.concatenate([
        meta_tokens,
        _pad_rows(conv_w[0].reshape(3 * (e // N_DEV) // LANES, LANES), 8),
        _pad_rows(mlstm_head_norm_w[0].reshape((e // N_DEV) // LANES, LANES), 8),
    ], axis=0) if d // N_DEV == LANES else None
    assert small is not None, "the packed small-weight block assumes d_model / 8 == 128"
    (small_g,) = _run_exchange(_gather_exchange([small]), "gather_small_weights")
    meta_full = jnp.transpose(small_g[:, 0:N_META, :], (1, 0, 2)).reshape(N_META, d)
    cw_rows = 3 * (e // N_DEV) // LANES
    conv_w_full = jnp.transpose(
        small_g[:, N_META:N_META + cw_rows, :].reshape(N_DEV, 3, e // N_DEV), (1, 0, 2)).reshape(3, e)
    hn_rows = (e // N_DEV) // LANES
    head_w_full = small_g[:, N_META + 8:N_META + 8 + hn_rows, :].reshape(1, e)

    h0 = jnp.concatenate([jnp.zeros((PAD_ROWS, d), F32), meta_full, x[0]], axis=0)
    tgt = loss_target[0]

    ci_map = lambda blk: 2 * (blk % 4) + blk // 4
    u0, (w_ci,) = _rms_fwd(h0, norm_w[0:1], "rms0", exchange=_gather_exchange([conv_in_w[0].astype(BF16)]))
    p0, (w_co, w_mi, w_mo) = _matmul(
        u0, w_ci, form="nn", m=t, n=8 * ec, kdim=d, tm=tm_in, tn=ec, tk=d, out_dtype=BF16, name="conv_in",
        b_spec=pl.BlockSpec((None, d, ec), lambda i, j, k: (ci_map(j), 0, 0)),
        exchange=_gather_exchange([conv_out_w[0].astype(BF16), mlstm_in_w[0].astype(BF16),
                                   mlstm_out_w[0].astype(BF16)]))
    w_co = w_co.reshape(e, d)
    w_mo = w_mo.reshape(e, d)
    g0, y0 = _conv_gate_fwd(p0, conv_w_full, "conv_gate")
    h1 = _matmul(g0, w_co, form="nn", m=t, n=d, kdim=e, tm=tm, tn=d, tk=e, out_dtype=F32, name="conv_out",
                 residual=h0)

    assert n_in_s == d + 1 and w_mi.shape == (N_DEV, d, d + 1)
    w_p1, w_gate = _mlstm_weight_layout(w_mi, dk, dv, "mlstm_w_layout")
    w_gate_t = jnp.transpose(w_gate[:, 0:16])
    bias_row = jnp.pad(mlstm_gate_b, ((0, 0), (0, LANES - 2 * nh)))
    bias_col = jnp.pad(mlstm_gate_b.T, ((0, 16 - 2 * nh), (0, 0)))

    u1 = _rms_fwd(h1, norm_w[1:2], "rms1")
    p1 = _matmul(u1, w_p1, form="nn", m=t, n=nh * hw, kdim=d, tm=tm_in, tn=_pick(nh * hw, 1024, LANES), tk=d,
                 out_dtype=BF16, name="mlstm_in")
    graw = _matmul(u1, w_gate, form="nn", m=t, n=LANES, kdim=d, tm=tm, tn=LANES, tk=d, out_dtype=F32,
                   name="gates_col")
    graw_t = _matmul(w_gate_t, u1, form="nt", m=16, n=t, kdim=d, tm=16, tn=_pick(t, 1664, LANES), tk=d,
                     out_dtype=F32, name="gates_row")
    colf, rowf = _gates_fwd(graw, graw_t, bias_row, bias_col, "gates_fwd")
    hm, hh, stat, csave, nsave = _mlstm_fwd(p1, colf, rowf, head_w_full, dk, dv, "mlstm_fwd")
    h2 = _matmul(hm, w_mo, form="nn", m=t, n=d, kdim=e, tm=tm, tn=d, tk=e, out_dtype=F32, name="mlstm_out",
                 residual=h1)

    dh2, dwf, loss_part = _final_loss(h2, final_norm_w.reshape(1, d), tgt, "final_loss")

    dhm = _matmul(dh2, w_mo, form="nt", m=t, n=e, kdim=d, tm=tm, tn=_pick(e, 2048, LANES), tk=d, out_dtype=BF16,
                  name="mlstm_out_dx")
    g_mo = _matmul(hm, dh2, form="tn", m=e, n=d, kdim=t, tm=_pick(e, 1024, LANES), tn=d, tk=tkw, out_dtype=BF16,
                   name="mlstm_out_dw")
    (dp1, gstat, dhead), (r_mo,) = _mlstm_bwd(
        p1, colf, rowf, head_w_full, hh, stat, csave, nsave, dhm, dk, dv, "mlstm_bwd",
        exchange=_scatter_exchange([g_mo.reshape(N_DEV, e // N_DEV, d)]))
    dp1, dbias = _gates_bwd(gstat, colf, dp1, "gates_bwd")
    n_p1 = nh * hw + LANES
    du1 = _matmul(dp1, w_p1, form="nt", m=t, n=d, kdim=n_p1, tm=tm, tn=d, tk=_pick(n_p1, 2048, LANES),
                  out_dtype=F32, name="mlstm_in_dx")
    g_p1 = _matmul(u1, dp1, form="tn", m=d, n=n_p1, kdim=t, tm=d, tn=_pick(n_p1, 2048, LANES), tk=tkw,
                   out_dtype=BF16, name="mlstm_in_dw")
    dh1, dnw1 = _rms_bwd(h1, norm_w[1:2], du1, dh2, "rms1_bwd")

    g_mi = _mlstm_grad_layout(g_p1, dk, dv, "mlstm_g_layout")

    dg0 =_matmul(dh1, w_co, form="nt", m=t, n=e, kdim=d, tm=tm, tn=_pick(e, 2048, LANES), tk=d, out_dtype=BF16,
                  name="conv_out_dx")
    g_co = _matmul(g0, dh1, form="tn", m=e, n=d, kdim=t, tm=_pick(e, 1024, LANES), tn=d, tk=tkw, out_dtype=BF16,
                   name="conv_out_dw")
    (dp0, dconv), (r_mi,) = _conv_gate_bwd(p0, y0, dg0, conv_w_full, "conv_gate_bwd",
                                           exchange=_scatter_exchange([g_mi]))
    g_ci, (r_co,) = _matmul(
        u0, dp0, form="tn", m=d, n=8 * ec, kdim=t, tm=d, tn=ec, tk=tkw, out_dtype=BF16, name="conv_in_dw",
        out_shape=(N_DEV, d, ec), out_spec=pl.BlockSpec((None, d, ec), lambda i, j, k: (ci_map(j), 0, 0)),
        exchange=_scatter_exchange([g_co.reshape(N_DEV, e // N_DEV, d)]))
    du0, (r_ci,) = _matmul(
        dp0, w_ci, form="nt", m=t, n=d, kdim=8 * ec, tm=tm, tn=d, tk=ec, out_dtype=F32, name="conv_in_dx",
        b_spec=pl.BlockSpec((None, d, ec), lambda i, j, k: (ci_map(k), 0, 0)),
        exchange=_scatter_exchange([g_ci]))
    grad_x, dmeta, dnw0 = _rms_bwd_first(h0, norm_w[0:1], du0, dh1, "rms0_bwd")
    grad_x = grad_x[None]

    row8 = lax.broadcasted_iota(jnp.int32, (8, 1), 0)
    loss_wide = jnp.pad(loss_part, ((0, 0), (0, d - LANES)))
    payload = jnp.concatenate([
        jnp.where(row8 == 0, dnw0, jnp.where(row8 == 1, dnw1, 0.0)),
        jnp.where(row8 == 0, dwf, jnp.where(row8 == 1, loss_wide, 0.0)),
        jnp.where(row8 == 0, jnp.pad(dbias, ((0, 0), (0, d - LANES))), 0.0),
        dmeta,
        _pad_rows(dconv[0:3].reshape(3 * e // d, d), 8),
        _pad_rows(dhead[:, 0, :].reshape(e // d, d), 8),
    ], axis=0)
    (payload_g,) = _run_exchange(_gather_exchange([payload]), "gather_small_grads")
    tot = _sum_devices(payload_g, "sum_small_grads")

    loss = tot[9, 0]
    g_norm = tot[0:2]
    g_final = tot[8]
    g_gate_b = tot[16:17, 0:2 * nh]
    g_meta = lax.dynamic_slice(tot[24:24 + N_META], (0, me * (d // N_DEV)), (N_META, d // N_DEV))
    g_conv_w = lax.dynamic_slice(tot[40:40 + 3 * e // d].reshape(3, e), (0, me * (e // N_DEV)), (3, e // N_DEV))
    g_head = lax.dynamic_slice(tot[48:48 + e // d].reshape(1, e), (0, me * (e // N_DEV)), (1, e // N_DEV))

    g1, d1, nm1, nv1 = _adamw_sharded(r_ci, conv_in_w[0], m_conv_in_w[0], v_conv_in_w[0], "adamw_conv_in")
    g2, d2, nm2, nv2 = _adamw_sharded(r_co, conv_out_w[0], m_conv_out_w[0], v_conv_out_w[0], "adamw_conv_out")
    g3, d3, nm3, nv3 = _adamw_sharded(r_mi, mlstm_in_w[0], m_mlstm_in_w[0], v_mlstm_in_w[0], "adamw_mlstm_in")
    g4, d4, nm4, nv4 = _adamw_sharded(r_mo, mlstm_out_w[0], m_mlstm_out_w[0], v_mlstm_out_w[0], "adamw_mlstm_out")

    small_g = [g_meta, g_norm, g_conv_w, g_gate_b, g_head, g_final.reshape(1, d)]
    small_w = [meta_tokens, norm_w, conv_w[0], mlstm_gate_b, mlstm_head_norm_w, final_norm_w.reshape(1, d)]
    small_m = [m_meta_tokens, m_norm_w, m_conv_w[0], m_mlstm_gate_b, m_mlstm_head_norm_w, m_final_norm_w.reshape(1, d)]
    small_v = [v_meta_tokens, v_norm_w, v_conv_w[0], v_mlstm_gate_b, v_mlstm_head_norm_w, v_final_norm_w.reshape(1, d)]
    sd, snm, snv = _adamw_small(small_g, small_w, small_m, small_v, "adamw_small")

    def order(meta, norm, cin, cw, cout, min_, gb, hn, mout, fin):
        return (meta, norm, cin[None], cw[None], cout[None], min_[None], gb, hn, mout[None], fin.reshape(d))

    grads = order(g_meta, g_norm, g1, g_conv_w, g2, g3, g_gate_b, g_head, g4, g_final)
    deltas = order(sd[0], sd[1], d1, sd[2], d2, d3, sd[3], sd[4], d4, sd[5])
    new_m = order(snm[0], snm[1], nm1, snm[2], nm2, nm3, snm[3], snm[4], nm4, snm[5])
    new_v = order(snv[0], snv[1], nv1, snv[2], nv2, nv3, snv[3], snv[4], nv4, snv[5])
    return (loss, grad_x, *grads, *deltas, *new_m, *new_v)
```

```python
import functools

import jax
import jax.numpy as jnp
from jax import lax
from jax.experimental import pallas as pl
from jax.experimental.pallas import tpu as pltpu

F32 = jnp.float32
BF16 = jnp.bfloat16
MESH = pl.DeviceIdType.MESH

N_DEV = 8
N_META = 16
N_HEADS = 4
CHUNK = 64
CHUNKS_PER_STEP = 2
LEAD = 128
PAD_ROWS = LEAD - N_META
RMS_EPS = 1e-6
NEG = -1e30
LANES = 128
VMEM_LIMIT = 48 * 1024 * 1024

ADAM_LR = 0.001
ADAM_B1 = 0.9
ADAM_B2 = 0.999
ADAM_EPS = 1e-08
ADAM_WD = 0.01
ADAM_STEP = 10

HIGHEST = lax.Precision.HIGHEST


def _pick(n, target, mult):
    best = None
    for d in range(mult, min(n, target) + 1, mult):
        if n % d == 0:
            best = d
    return n if best is None else best


def _params(**kw):
    return pltpu.CompilerParams(vmem_limit_bytes=VMEM_LIMIT, **kw)


def _sigmoid(x):
    return 0.5 * jnp.tanh(0.5 * x) + 0.5


def _call(body, *, name, grid, in_specs, out_specs, out_shape, args, scratch_shapes=(), aliases=None,
          exchange=None):
    aliases = {} if aliases is None else aliases
    if exchange is None:
        return pl.pallas_call(
            body, name=name, grid=grid, in_specs=list(in_specs), out_specs=out_specs, out_shape=out_shape,
            scratch_shapes=list(scratch_shapes), input_output_aliases=aliases,
            compiler_params=_params())(*args)
    ex = exchange
    single = not isinstance(out_shape, (list, tuple))
    shapes = [out_shape] if single else list(out_shape)
    specs = [out_specs] if single else list(out_specs)
    n_in, n_out, n_scr = len(in_specs), len(shapes), len(scratch_shapes)
    n_ex_in, n_ex_out = len(ex.ins), len(ex.out_shapes)
    steps = 1
    for size in grid:
        steps *= size

    def wrapped(*refs):
        own_in, ex_in = refs[:n_in], refs[n_in:n_in + n_ex_in]
        at = n_in + n_ex_in
        own_out, ex_out = refs[at:at + n_out], refs[at + n_out:at + n_out + n_ex_out]
        at += n_out + n_ex_out
        own_scr, ex_scr = refs[at:at + n_scr], refs[at + n_scr:]
        step = 0
        for axis, size in enumerate(grid):
            step = step * size + pl.program_id(axis)

        @pl.when(step == 0)
        def _():
            ex.start(ex_in, ex_out, ex_scr)

        if ex.mid is not None:
            @pl.when(step == (3 * steps) // 4)
            def _():
                ex.mid(ex_in, ex_out, ex_scr)

        body(*own_in, *own_out, *own_scr)

        @pl.when(step == steps - 1)
        def _():
            ex.finish(ex_in, ex_out, ex_scr)

    any_spec = pl.BlockSpec(memory_space=pl.ANY)
    res = pl.pallas_call(
        wrapped, name=name, grid=grid,
        in_specs=list(in_specs) + [any_spec] * n_ex_in, out_specs=specs + [any_spec] * n_ex_out,
        out_shape=shapes + ex.out_shapes, scratch_shapes=list(scratch_shapes) + ex.scratch,
        input_output_aliases=aliases, compiler_params=_params())(*args, *ex.ins)
    return (res[0] if single else res[:n_out]), res[n_out:]


def _matmul(a, b, *, form, m, n, kdim, tm, tn, tk, out_dtype, name,
            a_spec=None, b_spec=None, out_spec=None, out_shape=None, residual=None, exchange=None):
    ni, nj, nk = m // tm, n // tn, kdim // tk
    assert ni * tm == m and nj * tn == n and nk * tk == kdim, (name, m, n, kdim, tm, tn, tk)
    if form == "nn":
        dn = (((1,), (0,)), ((), ()))
        a_def = pl.BlockSpec((tm, tk), lambda i, j, k: (i, k))
        b_def = pl.BlockSpec((tk, tn), lambda i, j, k: (k, j))
    elif form == "nt":
        dn = (((1,), (1,)), ((), ()))
        a_def = pl.BlockSpec((tm, tk), lambda i, j, k: (i, k))
        b_def = pl.BlockSpec((tn, tk), lambda i, j, k: (j, k))
    else:
        dn = (((0,), (0,)), ((), ()))
        a_def = pl.BlockSpec((tk, tm), lambda i, j, k: (k, i))
        b_def = pl.BlockSpec((tk, tn), lambda i, j, k: (k, j))
    a_spec = a_def if a_spec is None else a_spec
    b_spec = b_def if b_spec is None else b_spec
    o_spec = pl.BlockSpec((tm, tn), lambda i, j, k: (i, j)) if out_spec is None else out_spec
    has_res = residual is not None

    def body(*refs):
        a_ref, b_ref = refs[:2]
        r_ref = refs[2] if has_res else None
        o_ref = refs[2 + has_res]

        def product():
            return lax.dot_general(a_ref[...].astype(BF16), b_ref[...].astype(BF16), dn,
                                   preferred_element_type=F32)

        def finish(acc):
            if has_res:
                acc = acc + r_ref[...].astype(F32)
            o_ref[...] = acc.astype(o_ref.dtype)

        if nk == 1:
            finish(product())
        else:
            acc_ref = refs[3 + has_res]
            k = pl.program_id(2)

            @pl.when(k == 0)
            def _():
                acc_ref[...] = jnp.zeros_like(acc_ref)

            acc_ref[...] += product()

            @pl.when(k == nk - 1)
            def _():
                finish(acc_ref[...])

    in_specs = [a_spec, b_spec]
    args = [a, b]
    if has_res:
        in_specs.append(pl.BlockSpec((tm, tn), lambda i, j, k: (i, j)))
        args.append(residual)
    return _call(
        body, name=name, grid=(ni, nj, nk), in_specs=in_specs, out_specs=o_spec,
        out_shape=jax.ShapeDtypeStruct((m, n) if out_shape is None else out_shape, out_dtype),
        scratch_shapes=[] if nk == 1 else [pltpu.VMEM((tm, tn), F32)], args=args, exchange=exchange)


def _rms_fwd(h, w, name, exchange=None):
    t, d = h.shape
    tm = _pick(t, 416, 16)

    def body(h_ref, w_ref, u_ref):
        x = h_ref[...]
        r = lax.rsqrt(jnp.mean(x * x, axis=-1, keepdims=True) + RMS_EPS)
        u_ref[...] = ((x * r) * w_ref[...]).astype(BF16)

    return _call(
        body, name=name, grid=(t // tm,),
        in_specs=[pl.BlockSpec((tm, d), lambda i: (i, 0)), pl.BlockSpec((1, d), lambda i: (0, 0))],
        out_specs=pl.BlockSpec((tm, d), lambda i: (i, 0)),
        out_shape=jax.ShapeDtypeStruct((t, d), BF16), args=(h, w), exchange=exchange)


def _rms_bwd(h, w, du, dres, name):
    t, d = h.shape
    tm = _pick(t, 416, 16)

    def body(h_ref, w_ref, du_ref, dres_ref, dh_ref, dw_ref):
        i = pl.program_id(0)
        x = h_ref[...]
        g = du_ref[...].astype(F32)
        r = lax.rsqrt(jnp.mean(x * x, axis=-1, keepdims=True) + RMS_EPS)
        gw = g * w_ref[...]
        dot = jnp.mean(gw * x, axis=-1, keepdims=True)
        dh_ref[...] = dres_ref[...] + (r * gw - x * ((r * r * r) * dot))
        part = jnp.sum(g * (x * r), axis=0, keepdims=True)

        @pl.when(i == 0)
        def _():
            dw_ref[...] = jnp.zeros_like(dw_ref)

        dw_ref[...] += jnp.broadcast_to(part, dw_ref.shape)

    row = pl.BlockSpec((tm, d), lambda i: (i, 0))
    return pl.pallas_call(
        body, name=name, grid=(t // tm,),
        in_specs=[row, pl.BlockSpec((1, d), lambda i: (0, 0)), row, row],
        out_specs=[row, pl.BlockSpec((8, d), lambda i: (0, 0))],
        out_shape=[jax.ShapeDtypeStruct((t, d), F32), jax.ShapeDtypeStruct((8, d), F32)],
        compiler_params=_params(),
    )(h, w, du, dres)


def _rms_bwd_first(h, w, du, dres, name):
    t, d = h.shape
    tm = LEAD

    def body(h_ref, w_ref, du_ref, dres_ref, gx_ref, dmeta_ref, dw_ref):
        i = pl.program_id(0)
        x = h_ref[...]
        g = du_ref[...].astype(F32)
        r = lax.rsqrt(jnp.mean(x * x, axis=-1, keepdims=True) + RMS_EPS)
        gw = g * w_ref[...]
        dot = jnp.mean(gw * x, axis=-1, keepdims=True)
        dh = dres_ref[...] + (r * gw - x * ((r * r * r) * dot))
        part = jnp.sum(g * (x * r), axis=0, keepdims=True)

        @pl.when(i == 0)
        def _():
            dw_ref[...] = jnp.zeros_like(dw_ref)
            dmeta_ref[...] = dh[PAD_ROWS:LEAD, :]

        @pl.when(i > 0)
        def _():
            gx_ref[...] = dh

        dw_ref[...] += jnp.broadcast_to(part, dw_ref.shape)

    row = pl.BlockSpec((tm, d), lambda i: (i, 0))
    return pl.pallas_call(
        body, name=name, grid=(t // tm,),
        in_specs=[row, pl.BlockSpec((1, d), lambda i: (0, 0)), row, row],
        out_specs=[pl.BlockSpec((tm, d), lambda i: (jnp.maximum(i - 1, 0), 0)),
                   pl.BlockSpec((N_META, d), lambda i: (0, 0)), pl.BlockSpec((8, d), lambda i: (0, 0))],
        out_shape=[jax.ShapeDtypeStruct((t - LEAD, d), F32), jax.ShapeDtypeStruct((N_META, d), F32),
                   jax.ShapeDtypeStruct((8, d), F32)],
        compiler_params=_params(),
    )(h, w, du, dres)


def _final_loss(h, w, tgt, name):
    t, d = h.shape
    sub = LEAD
    per = _pick(t // sub, 5, 1)
    tm = per * sub

    def body(h_ref, w_ref, *rest):
        t_refs, (dh_ref, dw_ref, loss_ref) = rest[:per], rest[per:]
        i = pl.program_id(0)

        @pl.when(i == 0)
        def _():
            dw_ref[...] = jnp.zeros_like(dw_ref)
            loss_ref[...] = jnp.zeros_like(loss_ref)

        for q in range(per):
            sl = slice(q * sub, (q + 1) * sub)
            x = h_ref[sl, :]
            r = lax.rsqrt(jnp.mean(x * x, axis=-1, keepdims=True) + RMS_EPS)
            yn = x * r
            y = yn * w_ref[...]
            rows = i * tm + q * sub + lax.broadcasted_iota(jnp.int32, (sub, 1), 0)
            diff = jnp.where(rows >= LEAD, y - t_refs[q][...], 0.0)
            tile_loss = 0.5 * jnp.sum(jnp.mean(diff * diff, axis=-1, keepdims=True), axis=0, keepdims=True)
            dy = diff / d
            gw = dy * w_ref[...]
            dot = jnp.mean(gw * x, axis=-1, keepdims=True)
            dh_ref[sl, :] = r * gw - x * ((r * r * r) * dot)
            dw_ref[...] += jnp.broadcast_to(jnp.sum(dy * yn, axis=0, keepdims=True), dw_ref.shape)
            loss_ref[...] += jnp.broadcast_to(tile_loss, loss_ref.shape)

    row = pl.BlockSpec((tm, d), lambda i: (i, 0))
    piece = [pl.BlockSpec((sub, d), functools.partial(lambda i, q: (jnp.maximum(i * per + q - 1, 0), 0), q=q))
             for q in range(per)]
    return pl.pallas_call(
        body, name=name, grid=(t // tm,),
        in_specs=[row, pl.BlockSpec((1, d), lambda i: (0, 0))] + piece,
        out_specs=[row, pl.BlockSpec((8, d), lambda i: (0, 0)), pl.BlockSpec((8, LANES), lambda i: (0, 0))],
        out_shape=[jax.ShapeDtypeStruct((t, d), F32), jax.ShapeDtypeStruct((8, d), F32),
                   jax.ShapeDtypeStruct((8, LANES), F32)],
        compiler_params=_params(),
    )(h, w, *([tgt] * per))


def _conv_gate_fwd(p0, conv_w, name, exchange=None):
    t = p0.shape[0]
    ec = p0.shape[1] // 8
    tm = _pick(t, 416, 16)
    nt = t // tm

    def body(p_ref, w_ref, g_ref, y_ref, ext_ref):
        i = pl.program_id(1)
        bg = p_ref[:, 0:ec].astype(F32)
        cg = p_ref[:, ec:2 * ec].astype(F32)
        xin = p_ref[:, 2 * ec:3 * ec].astype(F32)
        z = p_ref[:, 3 * ec:4 * ec].astype(F32)
        a = cg * xin

        @pl.when(i == 0)
        def _():
            ext_ref[0:8, :] = jnp.zeros((8, ec), F32)

        ext_ref[8:8 + tm, :] = a
        y = (w_ref[0:1, :] * ext_ref[6:6 + tm, :] + w_ref[1:2, :] * ext_ref[7:7 + tm, :]
             + w_ref[2:3, :] * a)
        ext_ref[0:8, :] = a[tm - 8:tm, :]
        y_ref[...] = y.astype(BF16)
        g_ref[...] = ((z * _sigmoid(z)) * bg * y).astype(BF16)

    out = pl.BlockSpec((tm, ec), lambda j, i: (i, j))
    return _call(
        body, name=name, grid=(2, nt),
        in_specs=[pl.BlockSpec((tm, 4 * ec), lambda j, i: (i, j)), pl.BlockSpec((3, ec), lambda j, i: (0, j))],
        out_specs=[out, out],
        out_shape=[jax.ShapeDtypeStruct((t, 2 * ec), BF16)] * 2,
        scratch_shapes=[pltpu.VMEM((tm + 8, ec), F32)], args=(p0, conv_w), exchange=exchange)


def _conv_gate_bwd(p0, y, dg, conv_w, name, exchange=None):
    t = p0.shape[0]
    ec = p0.shape[1] // 8
    tm = _pick(t, 208, 16)
    nt = t // tm

    def body(p_ref, y_ref, dg_ref, w_ref, dp_ref, dw_ref, ext_ref):
        i = pl.program_id(1)
        bg = p_ref[:, 0:ec].astype(F32)
        cg = p_ref[:, ec:2 * ec].astype(F32)
        xin = p_ref[:, 2 * ec:3 * ec].astype(F32)
        z = p_ref[:, 3 * ec:4 * ec].astype(F32)
        yv = y_ref[...].astype(F32)
        dgv = dg_ref[...].astype(F32)
        sig = _sigmoid(z)
        sz = z * sig
        dp_ref[:, 3 * ec:4 * ec] = (dgv * bg * yv * (sig * (1.0 + z * (1.0 - sig)))).astype(BF16)
        dp_ref[:, 0:ec] = (dgv * sz * yv).astype(BF16)
        dy = dgv * sz * bg

        @pl.when(i == 0)
        def _():
            ext_ref[tm:tm + 8, :] = jnp.zeros((8, ec), F32)
            dw_ref[...] = jnp.zeros_like(dw_ref)

        ext_ref[0:tm, :] = dy
        dy1 = ext_ref[1:tm + 1, :]
        dy2 = ext_ref[2:tm + 2, :]
        ext_ref[tm:tm + 8, :] = dy[0:8, :]
        da = w_ref[0:1, :] * dy2 + w_ref[1:2, :] * dy1 + w_ref[2:3, :] * dy
        dp_ref[:, ec:2 * ec] = (da * xin).astype(BF16)
        dp_ref[:, 2 * ec:3 * ec] = (da * cg).astype(BF16)
        a = cg * xin
        dw_ref[0:1, :] += jnp.sum(a * dy2, axis=0, keepdims=True)
        dw_ref[1:2, :] += jnp.sum(a * dy1, axis=0, keepdims=True)
        dw_ref[2:3, :] += jnp.sum(a * dy, axis=0, keepdims=True)

    rev = lambda j, i: (nt - 1 - i, j)
    return _call(
        body, name=name, grid=(2, nt),
        in_specs=[pl.BlockSpec((tm, 4 * ec), rev), pl.BlockSpec((tm, ec), rev), pl.BlockSpec((tm, ec), rev),
                  pl.BlockSpec((3, ec), lambda j, i: (0, j))],
        out_specs=[pl.BlockSpec((tm, 4 * ec), rev), pl.BlockSpec((8, ec), lambda j, i: (0, j))],
        out_shape=[jax.ShapeDtypeStruct((t, 8 * ec), BF16), jax.ShapeDtypeStruct((8, 2 * ec), F32)],
        scratch_shapes=[pltpu.VMEM((tm + 8, ec), F32)], args=(p0, y, dg, conv_w), exchange=exchange)


def _log_sigmoid(x):
    return jnp.minimum(x, 0.0) - jnp.log(1.0 + jnp.exp(-jnp.abs(x)))


def _gates_fwd(graw, graw_t, bias_row, bias_col, name):
    t = graw.shape[0]
    tm = _pick(t, 640, 128)
    cpt = tm // CHUNK
    nh = N_HEADS

    def body(g_ref, gt_ref, br_ref, bc_ref, colf_ref, rowf_ref):
        i = pl.program_id(0)
        gc = g_ref[...] + br_ref[...]
        rows = i * tm + lax.broadcasted_iota(jnp.int32, (tm, 1), 0)
        live = rows >= PAD_ROWS
        lf = jnp.where(live, _log_sigmoid(gc), 0.0)
        li = jnp.where(live, gc, NEG)
        gt = gt_ref[...] + bc_ref[...]
        cols = i * tm + lax.broadcasted_iota(jnp.int32, (1, tm), 1)
        live_t = cols >= PAD_ROWS
        lf_t = jnp.where(live_t, _log_sigmoid(gt), 0.0)
        li_t = jnp.where(live_t, gt, NEG)
        ri = lax.broadcasted_iota(jnp.int32, (CHUNK, CHUNK), 0)
        ci = lax.broadcasted_iota(jnp.int32, (CHUNK, CHUNK), 1)
        lower = (ri >= ci).astype(F32)
        upper = (ri <= ci).astype(F32)
        lane = lax.broadcasted_iota(jnp.int32, (CHUNK, LANES), 1)
        sub = lax.broadcasted_iota(jnp.int32, (8, CHUNK), 0)
        for c in range(cpt):
            sl = slice(c * CHUNK, (c + 1) * CHUNK)
            b_all = jnp.dot(lower, lf[sl, :], precision=HIGHEST, preferred_element_type=F32)
            bt_all = jnp.dot(lf_t[:, sl], upper, precision=HIGHEST, preferred_element_type=F32)
            for h in range(nh):
                b = b_all[:, nh + h:nh + h + 1]
                r = li[sl, h:h + 1] - b
                pre = gc[sl, nh + h:nh + h + 1]
                colf_ref[h, sl, :] = jnp.where(lane == 0, b, jnp.where(lane == 1, r, jnp.where(lane == 2, pre, 0.0)))
                b_row = bt_all[nh + h:nh + h + 1, :]
                r_row = li_t[h:h + 1, sl] - b_row
                rowf_ref[h, c, :, :] = jnp.where(sub == 0, r_row, jnp.where(sub == 1, b_row, 0.0))

    return pl.pallas_call(
        body, name=name, grid=(t // tm,),
        in_specs=[pl.BlockSpec((tm, LANES), lambda i: (i, 0)), pl.BlockSpec((16, tm), lambda i: (0, i)),
                  pl.BlockSpec((1, LANES), lambda i: (0, 0)), pl.BlockSpec((16, 1), lambda i: (0, 0))],
        out_specs=[pl.BlockSpec((nh, tm, LANES), lambda i: (0, i, 0)),
                   pl.BlockSpec((nh, cpt, 8, CHUNK), lambda i: (0, i, 0, 0))],
        out_shape=[jax.ShapeDtypeStruct((nh, t, LANES), F32),
                   jax.ShapeDtypeStruct((nh, t // CHUNK, 8, CHUNK), F32)],
        compiler_params=_params(),
    )(graw, graw_t, bias_row, bias_col)


def _gates_bwd(gstat, colf, dp1, name):
    nh, t, _ = gstat.shape
    tm = _pick(t, 640, 128)
    cpt = tm // CHUNK
    nt = t // tm
    nc = t // CHUNK
    gate_block = dp1.shape[1] // LANES - 1

    def body(gs_ref, nx_ref, colf_ref, dp_any, dg_ref, db_ref):
        i = pl.program_id(0)
        ri = lax.broadcasted_iota(jnp.int32, (CHUNK, CHUNK), 0)
        ci = lax.broadcasted_iota(jnp.int32, (CHUNK, CHUNK), 1)
        upper = (ri <= ci).astype(F32)
        lane = lax.broadcasted_iota(jnp.int32, (CHUNK, LANES), 1)
        total = jnp.zeros((1, LANES), F32)
        for c in range(cpt):
            sl = slice(c * CHUNK, (c + 1) * CHUNK)
            rows = i * tm + c * CHUNK + lax.broadcasted_iota(jnp.int32, (CHUNK, 1), 0)
            live = rows >= PAD_ROWS
            acc = jnp.zeros((CHUNK, LANES), F32)
            for h in range(nh):
                blk = gs_ref[h, sl, :]
                rev = jnp.dot(upper, blk, precision=HIGHEST, preferred_element_type=F32)
                if c + 1 < cpt:
                    carry = gs_ref[h, (c + 1) * CHUNK:(c + 1) * CHUNK + 1, 2:3]
                else:
                    carry = jnp.where(i == nt - 1, 0.0, nx_ref[h, 0:1, 2:3])
                dlogf = rev[:, 0:1] + carry
                pre = colf_ref[h, sl, 2:3]
                dgf = jnp.where(live, dlogf * (1.0 - _sigmoid(pre)), 0.0)
                dgi = jnp.where(live, blk[:, 1:2], 0.0)
                acc = acc + jnp.where(lane == h, dgi, 0.0) + jnp.where(lane == nh + h, dgf, 0.0)
            dg_ref[sl, :] = acc.astype(BF16)
            total = total + jnp.sum(acc, axis=0, keepdims=True)

        @pl.when(i == 0)
        def _():
            db_ref[...] = jnp.zeros_like(db_ref)

        db_ref[...] += jnp.broadcast_to(total, db_ref.shape)

    return pl.pallas_call(
        body, name=name, grid=(nt,),
        in_specs=[pl.BlockSpec((nh, tm, LANES), lambda i: (0, i, 0)),
                  pl.BlockSpec((nh, CHUNK, LANES), lambda i: (0, jnp.minimum((i + 1) * cpt, nc - 1), 0)),
                  pl.BlockSpec((nh, tm, LANES), lambda i: (0, i, 0)),
                  pl.BlockSpec(memory_space=pl.ANY)],
        out_specs=[pl.BlockSpec((tm, LANES), lambda i: (i, gate_block)), pl.BlockSpec((8, LANES), lambda i: (0, 0))],
        out_shape=[jax.ShapeDtypeStruct(dp1.shape, dp1.dtype), jax.ShapeDtypeStruct((8, LANES), F32)],
        input_output_aliases={3: 0},
        compiler_params=_params(),
    )(gstat, gstat, colf, dp1)


NT_DIMS = (((1,), (1,)), ((), ()))
TN_DIMS = (((0,), (0,)), ((), ()))


def _dot(a, b):
    return jnp.dot(a.astype(BF16), b.astype(BF16), preferred_element_type=F32)


def _dot_nt(a, b):
    return lax.dot_general(a.astype(BF16), b.astype(BF16), NT_DIMS, preferred_element_type=F32)


def _dot_tn(a, b):
    return lax.dot_general(a.astype(BF16), b.astype(BF16), TN_DIMS, preferred_element_type=F32)


def _chunk_gates(colf_ref, rowf_ref, m_prev):
    b = colf_ref[:, 0:1]
    rcol = colf_ref[:, 1:2]
    rrow = rowf_ref[0:1, :]
    ri = lax.broadcasted_iota(jnp.int32, (CHUNK, CHUNK), 0)
    ci = lax.broadcasted_iota(jnp.int32, (CHUNK, CHUNK), 1)
    log_d = jnp.where(ri >= ci, b + rrow, NEG)
    m_row = jnp.maximum(b + m_prev, jnp.max(log_d, axis=-1, keepdims=True))
    dmat = jnp.exp(log_d - m_row)
    inter = jnp.exp(b + m_prev - m_row)
    b_last = b[CHUNK - 1:CHUNK, :]
    log_w = rcol + b_last
    m_new = jnp.maximum(b_last + m_prev, jnp.max(log_w, axis=0, keepdims=True))
    decay = jnp.exp(b_last + m_prev - m_new)
    w = jnp.exp(log_w - m_new)
    return m_row, dmat, inter, m_new, decay, w


def _mlstm_fwd(p1, colf, rowf, dk, dv, name):
    t = p1.shape[0]
    nh = N_HEADS
    nc = t // CHUNK
    hw = 2 * dk + dv
    cps = CHUNKS_PER_STEP if nc % CHUNKS_PER_STEP == 0 else 1
    rows_per_step = cps * CHUNK

    def body(p_ref, colf_ref, rowf_ref, hh_ref, stat_ref, cs_ref, ns_ref, c_ref, n_ref, m_ref):
        c = pl.program_id(0)

        @pl.when(c == 0)
        def _():
            c_ref[...] = jnp.zeros_like(c_ref)
            n_ref[...] = jnp.zeros_like(n_ref)
            m_ref[...] = jnp.zeros_like(m_ref)

        for cc in range(cps):
            rows = pl.ds(cc * CHUNK, CHUNK)
            for h in range(nh):
                cols = pl.ds(h * dv, dv)
                head(p_ref.at[rows, pl.ds(h * hw, hw)], colf_ref.at[h, rows], rowf_ref.at[h, cc],
                     hh_ref.at[rows, cols], stat_ref.at[h, rows], cs_ref.at[h, cc],
                     ns_ref.at[h, cc], c_ref.at[h], n_ref.at[h], m_ref.at[h])

    def head(p_ref, colf_ref, rowf_ref, hh_ref, stat_ref, cs_ref, ns_ref, c_ref, n_ref, m_ref):
        m_prev = m_ref[...]
        n_prev = n_ref[...]
        c_prev = c_ref[...]
        cs_ref[...] = c_prev.astype(BF16)
        sub = lax.broadcasted_iota(jnp.int32, (8, dk), 0)
        ns_ref[...] = jnp.where(sub == 0, n_prev, jnp.where(sub == 1, m_prev, 0.0))

        q = p_ref[:, 0:dk]
        k = p_ref[:, dk:2 * dk]
        v = p_ref[:, 2 * dk:2 * dk + dv]
        m_row, dmat, inter, m_new, decay, w = _chunk_gates(colf_ref, rowf_ref, m_prev)
        s = _dot_nt(q, k) * dmat
        num = _dot(s, v) + inter * _dot(q, c_prev)
        den = jnp.sum(s, axis=-1, keepdims=True) + inter * jnp.sum(q.astype(F32) * n_prev, axis=-1, keepdims=True)
        denom = jnp.maximum(jnp.abs(den), jnp.exp(-m_row))
        hh_ref[...] = num * (1.0 / denom)
        lane = lax.broadcasted_iota(jnp.int32, (CHUNK, LANES), 1)
        stat_ref[...] = jnp.where(lane == 0, den, 0.0)

        wk = w * k.astype(F32)
        c_ref[...] = decay * c_prev + _dot_tn(wk, v)
        n_ref[...] = decay * n_prev + jnp.sum(wk, axis=0, keepdims=True)
        m_ref[...] = m_new

    return pl.pallas_call(
        body, name=name, grid=(nc // cps,),
        in_specs=[pl.BlockSpec((rows_per_step, nh * hw), lambda c: (c, 0)),
                  pl.BlockSpec((nh, rows_per_step, LANES), lambda c: (0, c, 0)),
                  pl.BlockSpec((nh, cps, 8, CHUNK), lambda c: (0, c, 0, 0))],
        out_specs=[pl.BlockSpec((rows_per_step, nh * dv), lambda c: (c, 0)),
                   pl.BlockSpec((nh, rows_per_step, LANES), lambda c: (0, c, 0)),
                   pl.BlockSpec((nh, cps, dk, dv), lambda c: (0, c, 0, 0)),
                   pl.BlockSpec((nh, cps, 8, dk), lambda c: (0, c, 0, 0))],
        out_shape=[jax.ShapeDtypeStruct((t, nh * dv), F32), jax.ShapeDtypeStruct((nh, t, LANES), F32),
                   jax.ShapeDtypeStruct((nh, nc, dk, dv), BF16), jax.ShapeDtypeStruct((nh, nc, 8, dk), F32)],
        scratch_shapes=[pltpu.VMEM((nh, dk, dv), F32), pltpu.VMEM((nh, 1, dk), F32), pltpu.VMEM((nh, 1, 1), F32)],
        compiler_params=_params(),
    )(p1, colf, rowf)


def _head_gate_fwd(hh, p1, head_w, dv, name):
    t = hh.shape[0]
    nh = N_HEADS
    e = nh * dv
    tm = _pick(t, 416, 16)

    def body(hh_ref, oz_ref, w_ref, hm_ref):
        for h in range(nh):
            cols = slice(h * dv, (h + 1) * dv)
            x = hh_ref[:, cols]
            o = oz_ref[:, 2 * h * dv:(2 * h + 1) * dv].astype(F32)
            z = oz_ref[:, (2 * h + 1) * dv:(2 * h + 2) * dv].astype(F32)
            r = lax.rsqrt(jnp.mean(x * x, axis=-1, keepdims=True) + RMS_EPS)
            hn = (x * r) * w_ref[:, cols]
            hm_ref[:, cols] = (hn * _sigmoid(o) * (z * _sigmoid(z))).astype(BF16)

    return pl.pallas_call(
        body, name=name, grid=(t // tm,),
        in_specs=[pl.BlockSpec((tm, e), lambda i: (i, 0)), pl.BlockSpec((tm, 2 * e), lambda i: (i, 1)),
                  pl.BlockSpec((1, e), lambda i: (0, 0))],
        out_specs=pl.BlockSpec((tm, e), lambda i: (i, 0)),
        out_shape=jax.ShapeDtypeStruct((t, e), BF16),
        compiler_params=_params(),
    )(hh, p1, head_w)


def _mlstm_bwd(p1, colf, rowf, head_w, hh, stat, csave, nsave, dhm, n_cols, dk, dv, name, exchange=None):
    t = p1.shape[0]
    nh = N_HEADS
    nc = t // CHUNK
    hw = 2 * dk + dv
    cps = CHUNKS_PER_STEP if nc % CHUNKS_PER_STEP == 0 else 1
    rows_per_step = cps * CHUNK

    def body(p_ref, colf_ref, rowf_ref, w_ref, hh_ref, stat_ref, cs_ref, ns_ref, dhm_ref,
             dp_ref, gs_ref, dw_ref, dc_ref, dn_ref):
        c = pl.program_id(0)

        @pl.when(c == 0)
        def _():
            dc_ref[...] = jnp.zeros_like(dc_ref)
            dn_ref[...] = jnp.zeros_like(dn_ref)
            dw_ref[...] = jnp.zeros_like(dw_ref)

        for cc in reversed(range(cps)):
            rows = pl.ds(cc * CHUNK, CHUNK)
            for h in range(nh):
                cols = pl.ds(h * dv, dv)
                oz = pl.ds(nh * hw + h * 2 * dv, 2 * dv)
                head(p_ref.at[rows, pl.ds(h * hw, hw)], p_ref.at[rows, oz], colf_ref.at[h, rows], rowf_ref.at[h, cc],
                     w_ref.at[:, cols], hh_ref.at[rows, cols], stat_ref.at[h, rows], cs_ref.at[h, cc],
                     ns_ref.at[h, cc], dhm_ref.at[rows, cols], dp_ref.at[rows, pl.ds(h * hw, hw)],
                     dp_ref.at[rows, oz], gs_ref.at[h, rows], dw_ref.at[h], dc_ref.at[h], dn_ref.at[h])

    def head(p_ref, oz_ref, colf_ref, rowf_ref, w_ref, hh_ref, stat_ref, cs_ref, ns_ref, dhm_ref,
             dp_ref, doz_ref, gs_ref, dw_ref, dc_ref, dn_ref):
        q = p_ref[:, 0:dk]
        k = p_ref[:, dk:2 * dk]
        v = p_ref[:, 2 * dk:2 * dk + dv]
        o = oz_ref[:, 0:dv].astype(F32)
        z = oz_ref[:, dv:2 * dv].astype(F32)
        qf = q.astype(F32)
        kf = k.astype(F32)
        n_prev = ns_ref[0:1, :]
        m_prev = ns_ref[1:2, 0:1]
        c_prev = cs_ref[...]
        m_row, dmat, inter, m_new, decay, w = _chunk_gates(colf_ref, rowf_ref, m_prev)

        hh = hh_ref[...]
        dhm_v = dhm_ref[...].astype(F32)
        so = _sigmoid(o)
        sg = _sigmoid(z)
        sz = z * sg
        r = lax.rsqrt(jnp.mean(hh * hh, axis=-1, keepdims=True) + RMS_EPS)
        hn = (hh * r) * w_ref[...]
        dhn = dhm_v * so * sz
        doz_ref[:, 0:dv] = (dhm_v * hn * sz * (so * (1.0 - so))).astype(BF16)
        doz_ref[:, dv:2 * dv] = (dhm_v * hn * so * (sg * (1.0 + z * (1.0 - sg)))).astype(BF16)
        dw_ref[...] += jnp.broadcast_to(jnp.sum(dhn * (hh * r), axis=0, keepdims=True), dw_ref.shape)
        gwn = dhn * w_ref[...]
        dhh = r * gwn - hh * ((r * r * r) * jnp.mean(gwn * hh, axis=-1, keepdims=True))

        den = stat_ref[:, 0:1]
        floor = jnp.exp(-m_row)
        denom = jnp.maximum(jnp.abs(den), floor)
        inv = 1.0 / denom
        dnum = dhh * inv
        hdot = jnp.sum(dhh * hh, axis=-1, keepdims=True)
        dden = jnp.where(jnp.abs(den) > floor, -(hdot * inv) * jnp.sign(den), 0.0)
        s = _dot_nt(q, k) * dmat
        dqk = (_dot_nt(dnum, v) + dden) * dmat
        dc_new = dc_ref[...]
        dn_new = dn_ref[...]
        idd = inter * dden
        dq = _dot(dqk, k) + inter * _dot_nt(dnum, c_prev) + idd * n_prev
        dkv = _dot_tn(dqk, q) + w * (_dot_nt(v, dc_new) + dn_new)
        dvv = _dot_tn(s, dnum) + w * _dot(k, dc_new)
        dp_ref[:, 0:dk] = dq.astype(BF16)
        dp_ref[:, dk:2 * dk] = dkv.astype(BF16)
        dp_ref[:, 2 * dk:2 * dk + dv] = dvv.astype(BF16)
        qdq = jnp.sum(qf * dq, axis=-1, keepdims=True)
        kdk = jnp.sum(kf * dkv, axis=-1, keepdims=True)
        dc_prev = decay * dc_new + _dot_tn(inter * qf, dnum)
        dn_prev = decay * dn_new + jnp.sum(idd * qf, axis=0, keepdims=True)
        dc_ref[...] = dc_prev
        dn_ref[...] = dn_prev
        cross = (jnp.sum(jnp.sum(c_prev.astype(F32) * dc_prev, axis=-1, keepdims=True), axis=0, keepdims=True)
                 + jnp.sum(n_prev * dn_prev, axis=-1, keepdims=True))
        lane = lax.broadcasted_iota(jnp.int32, (CHUNK, LANES), 1)
        gs_ref[...] = jnp.where(lane == 0, qdq - kdk, jnp.where(lane == 1, kdk, jnp.where(lane == 2, cross, 0.0)))

    ns = nc // cps
    rc = lambda c: (ns - 1 - c, 0)
    rc3 = lambda c: (0, ns - 1 - c, 0)
    rc4 = lambda c: (0, ns - 1 - c, 0, 0)
    return _call(
        body, name=name, grid=(ns,),
        in_specs=[pl.BlockSpec((rows_per_step, nh * (hw + 2 * dv)), rc),
                  pl.BlockSpec((nh, rows_per_step, LANES), rc3),
                  pl.BlockSpec((nh, cps, 8, CHUNK), rc4),
                  pl.BlockSpec((1, nh * dv), lambda c: (0, 0)),
                  pl.BlockSpec((rows_per_step, nh * dv), rc),
                  pl.BlockSpec((nh, rows_per_step, LANES), rc3),
                  pl.BlockSpec((nh, cps, dk, dv), rc4),
                  pl.BlockSpec((nh, cps, 8, dk), rc4),
                  pl.BlockSpec((rows_per_step, nh * dv), rc)],
        out_specs=[pl.BlockSpec((rows_per_step, nh * (hw + 2 * dv)), rc),
                   pl.BlockSpec((nh, rows_per_step, LANES), rc3),
                   pl.BlockSpec((nh, 8, dv), lambda c: (0, 0, 0))],
        out_shape=[jax.ShapeDtypeStruct((t, n_cols), BF16), jax.ShapeDtypeStruct((nh, t, LANES), F32),
                   jax.ShapeDtypeStruct((nh, 8, dv), F32)],
        scratch_shapes=[pltpu.VMEM((nh, dk, dv), F32), pltpu.VMEM((nh, 1, dk), F32)],
        args=(p1, colf, rowf, head_w, hh, stat, csave, nsave, dhm), exchange=exchange)


def _position():
    return lax.axis_index("x"), lax.axis_index("y"), lax.axis_index("c")


class _Exchange:
    def __init__(self, ins, out_shapes, start, mid, finish):
        n = len(ins)
        self.ins, self.out_shapes = list(ins), list(out_shapes)
        self.start, self.mid, self.finish = start, mid, finish
        self.scratch = [pltpu.SemaphoreType.DMA((n, 7)), pltpu.SemaphoreType.DMA((n, 7)),
                        pltpu.SemaphoreType.DMA((n,))]


def _gather_exchange(shards):
    n = len(shards)

    def plan(ins, outs, sems):
        send_sems, recv_sems, local_sems = sems
        x, y, c = _position()
        me, sibling = (x, y, c), (x, y, 1 - c)
        chips = [(1 - x, y), (x, 1 - y), (1 - x, 1 - y)]

        def copy(a, k, block, to, src=None):
            px, py, pc = block
            dst = outs[a].at[4 * px + 2 * py + pc]
            return pltpu.make_async_remote_copy(
                src_ref=dst if src is None else src, dst_ref=dst,
                send_sem=send_sems.at[a, k], recv_sem=recv_sems.at[a, k],
                device_id=to, device_id_type=MESH)

        def mine():
            return [pltpu.make_async_copy(ins[a], outs[a].at[4 * x + 2 * y + c], local_sems.at[a])
                    for a in range(n)]

        def first():
            out = []
            for a in range(n):
                out.append(copy(a, 0, me, sibling, src=ins[a]))
                out += [copy(a, 1 + j, me, (*chip, c), src=ins[a]) for j, chip in enumerate(chips)]
            return out

        def ici_in():
            return [copy(a, 1 + j, (*chip, c), me) for j, chip in enumerate(chips) for a in range(n)]

        def passed():
            return [copy(a, 4 + j, (*chip, c), sibling) for j, chip in enumerate(chips) for a in range(n)]

        def d2d_in():
            return ([copy(a, 0, sibling, me) for a in range(n)]
                    + [copy(a, 4 + j, (*chip, 1 - c), me) for j, chip in enumerate(chips) for a in range(n)])

        return mine, first, ici_in, passed, d2d_in

    def start(ins, outs, sems):
        mine, first, _, _, _ = plan(ins, outs, sems)
        for cp in mine() + first():
            cp.start()

    def mid(ins, outs, sems):
        _, _, ici_in, passed, _ = plan(ins, outs, sems)
        for arrived, onward in zip(ici_in(), passed()):
            arrived.wait_recv()
            onward.start()

    def finish(ins, outs, sems):
        mine, first, _, passed, d2d_in = plan(ins, outs, sems)
        for cp in d2d_in():
            cp.wait_recv()
        for cp in first() + passed():
            cp.wait_send()
        for cp in mine():
            cp.wait()

    shapes = [jax.ShapeDtypeStruct((N_DEV,) + s.shape, s.dtype) for s in shards]
    return _Exchange(shards, shapes, start, mid, finish)


def _scatter_exchange(fulls):
    n = len(fulls)

    def plan(ins, outs, sems):
        send_sems, recv_sems, local_sems = sems
        x, y, c = _position()
        my_slot = 4 * x + 2 * y + c

        def mine():
            return [pltpu.make_async_copy(ins[a].at[my_slot], outs[a].at[my_slot], local_sems.at[a])
                    for a in range(n)]

        def remote(arriving):
            out = []
            for kk in (1, 2, 4, 6, 3, 5, 7):
                kx, ky, kc = (kk >> 2) & 1, (kk >> 1) & 1, kk & 1
                px = 1 - x if kx else x
                py = 1 - y if ky else y
                pc = 1 - c if kc else c
                peer_slot = 4 * px + 2 * py + pc
                for a in range(n):
                    out.append(pltpu.make_async_remote_copy(
                        src_ref=ins[a].at[peer_slot], dst_ref=outs[a].at[peer_slot if arriving else my_slot],
                        send_sem=send_sems.at[a, kk - 1], recv_sem=recv_sems.at[a, kk - 1],
                        device_id=(px, py, pc), device_id_type=MESH))
            return out

        return mine, remote

    def start(ins, outs, sems):
        mine, remote = plan(ins, outs, sems)
        for cp in mine() + remote(False):
            cp.start()

    def finish(ins, outs, sems):
        mine, remote = plan(ins, outs, sems)
        for cp in remote(True):
            cp.wait_recv()
        for cp in remote(False):
            cp.wait_send()
        for cp in mine():
            cp.wait()

    shapes = [jax.ShapeDtypeStruct(f.shape, f.dtype) for f in fulls]
    return _Exchange(fulls, shapes, start, None, finish)


def _run_exchange(ex, name):
    n_in, n_out = len(ex.ins), len(ex.out_shapes)

    def body(*refs):
        ins, outs, sems = refs[:n_in], refs[n_in:n_in + n_out], refs[n_in + n_out:]
        ex.start(ins, outs, sems)
        if ex.mid is not None:
            ex.mid(ins, outs, sems)
        ex.finish(ins, outs, sems)

    any_spec = pl.BlockSpec(memory_space=pl.ANY)
    return pl.pallas_call(
        body, name=name,
        in_specs=[any_spec] * n_in, out_specs=[any_spec] * n_out,
        out_shape=ex.out_shapes, scratch_shapes=ex.scratch,
    )(*ex.ins)


def _natural_to_head_major(n0, dk, dv):
    nh = N_HEADS
    qk, e, hw = nh * dk, nh * dv, 2 * dk + dv
    if n0 < qk:
        h, off = divmod(n0, dk)
        return h * hw + off, True
    n1 = n0 - qk
    if n1 < qk:
        h, off = divmod(n1, dk)
        return h * hw + dk + off, False
    part, n3 = divmod(n1 - qk, e)
    h, off = divmod(n3, dv)
    if part == 0:
        return h * hw + 2 * dk + off, False
    return nh * hw + h * 2 * dv + (part - 1) * dv + off, False


def _mlstm_weight_layout(w_blocks, dk, dv, name):
    _, d, blk = w_blocks.shape
    nh = N_HEADS
    hw = 2 * dk + 3 * dv
    n_tiles = nh * hw // LANES
    tiles_per_block = d // LANES
    tr = _pick(d, 128, 16)
    scale = dk ** -0.5

    def body(w_ref, p_ref, g_ref):
        lane = lax.broadcasted_iota(jnp.int32, (tr, LANES), 1)

        def tile(s, r):
            return w_ref[s, :, r * LANES:(r + 1) * LANES].astype(F32)

        def last_col(s):
            return w_ref[s, :, d:d + 1].astype(F32)

        for tn in range(n_tiles):
            s, r = divmod(tn, tiles_per_block)
            if s == 0:
                val = tile(0, r)
            elif r == 0:
                from_prev = pltpu.roll(tile(s - 1, tiles_per_block - 1), s - 1, 1)
                from_here = pltpu.roll(tile(s, 0), s, 1)
                val = jnp.where(lane < s - 1, from_prev, jnp.where(lane == s - 1, last_col(s - 1), from_here))
            else:
                slab = jnp.concatenate([tile(s, r - 1), tile(s, r)], axis=1)
                val = pltpu.roll(slab, s, 1)[:, LANES:]
            at, is_q = _natural_to_head_major(tn * LANES, dk, dv)
            if is_q:
                val = val * scale
            p_ref[:, at:at + LANES] = val.astype(BF16)
        n_gate = 2 * nh
        s = N_DEV - 1
        gates = pltpu.roll(tile(s, tiles_per_block - 1), n_gate - 1, 1)
        gates = jnp.where(lane < n_gate - 1, gates, jnp.where(lane == n_gate - 1, last_col(s), 0.0)).astype(BF16)
        g_ref[...] = gates
        p_ref[:, nh * hw:nh * hw + LANES] = gates

    return pl.pallas_call(
        body, name=name, grid=(d // tr,),
        in_specs=[pl.BlockSpec((N_DEV, tr, blk), lambda i: (0, i, 0))],
        out_specs=[pl.BlockSpec((tr, nh * hw + LANES), lambda i: (i, 0)), pl.BlockSpec((tr, LANES), lambda i: (i, 0))],
        out_shape=[jax.ShapeDtypeStruct((d, nh * hw + LANES), BF16), jax.ShapeDtypeStruct((d, LANES), BF16)],
        compiler_params=_params(),
    )(w_blocks)


def _mlstm_grad_layout(g_p1, dk, dv, name):
    d = g_p1.shape[0]
    nh = N_HEADS
    hw = 2 * dk + 3 * dv
    n_tiles = nh * hw // LANES
    tiles_per_block = d // LANES
    tr = _pick(d, 128, 16)
    scale = dk ** -0.5

    def body(p_ref, o_ref):
        def natural(tn):
            if tn == n_tiles:
                return p_ref[:, nh * hw:nh * hw + LANES].astype(F32)
            at, is_q = _natural_to_head_major(tn * LANES, dk, dv)
            val = p_ref[:, at:at + LANES].astype(F32)
            return val * scale if is_q else val

        for s in range(N_DEV):
            for r in range(tiles_per_block):
                tn = s * tiles_per_block + r
                if s == 0:
                    val = natural(tn)
                else:
                    slab = jnp.concatenate([natural(tn), natural(tn + 1)], axis=1)
                    val = pltpu.roll(slab, 2 * LANES - s, 1)[:, :LANES]
                o_ref[s, :, r * LANES:(r + 1) * LANES] = val.astype(BF16)
            o_ref[s, :, d:d + 1] = natural((s + 1) * tiles_per_block)[:, s:s + 1].astype(BF16)

    return pl.pallas_call(
        body, name=name, grid=(d // tr,),
        in_specs=[pl.BlockSpec((tr, nh * hw + LANES), lambda i: (i, 0))],
        out_specs=pl.BlockSpec((N_DEV, tr, d + 1), lambda i: (0, i, 0)),
        out_shape=jax.ShapeDtypeStruct((N_DEV, d, d + 1), BF16),
        compiler_params=_params(),
    )(g_p1)


def _adamw_math(w, g, m, v):
    m = ADAM_B1 * m + (1.0 - ADAM_B1) * g
    v = ADAM_B2 * v + (1.0 - ADAM_B2) * (g * g)
    m_hat = m / (1.0 - ADAM_B1 ** ADAM_STEP)
    v_hat = v / (1.0 - ADAM_B2 ** ADAM_STEP)
    delta = -ADAM_LR * (m_hat / (jnp.sqrt(v_hat) + ADAM_EPS) + ADAM_WD * w)
    return delta, m, v


def _adamw_sharded(parts, w, m, v, name):
    _, r, c = parts.shape
    tr = _pick(r, 128, 8)

    def body(p_ref, w_ref, m_ref, v_ref, g_ref, d_ref, nm_ref, nv_ref):
        g = p_ref[0].astype(F32)
        for s in range(1, N_DEV):
            g = g + p_ref[s].astype(F32)
        delta, m_new, v_new = _adamw_math(w_ref[...], g, m_ref[...], v_ref[...])
        g_ref[...] = g
        d_ref[...] = delta
        nm_ref[...] = m_new
        nv_ref[...] = v_new

    blk = pl.BlockSpec((tr, c), lambda i: (i, 0))
    return pl.pallas_call(
        body, name=name, grid=(r // tr,),
        in_specs=[pl.BlockSpec((N_DEV, tr, c), lambda i: (0, i, 0)), blk, blk, blk],
        out_specs=[blk] * 4,
        out_shape=[jax.ShapeDtypeStruct((r, c), F32)] * 4,
        compiler_params=_params(),
    )(parts, w, m, v)


def _sum_devices(parts, name):
    _, r, c = parts.shape

    def body(p_ref, o_ref):
        g = p_ref[0]
        for s in range(1, N_DEV):
            g = g + p_ref[s]
        o_ref[...] = g

    return pl.pallas_call(
        body, name=name, out_shape=jax.ShapeDtypeStruct((r, c), F32), compiler_params=_params(),
    )(parts)


def _adamw_small(gs, ws, ms, vs, name):
    n = len(gs)

    def body(*refs):
        g_refs, w_refs, m_refs, v_refs = refs[:n], refs[n:2 * n], refs[2 * n:3 * n], refs[3 * n:4 * n]
        d_refs, nm_refs, nv_refs = refs[4 * n:5 * n], refs[5 * n:6 * n], refs[6 * n:7 * n]
        for a in range(n):
            delta, m_new, v_new = _adamw_math(w_refs[a][...], g_refs[a][...], m_refs[a][...], v_refs[a][...])
            d_refs[a][...] = delta
            nm_refs[a][...] = m_new
            nv_refs[a][...] = v_new

    shapes = [jax.ShapeDtypeStruct(w.shape, F32) for w in ws]
    outs = pl.pallas_call(
        body, name=name, out_shape=shapes * 3, compiler_params=_params(),
    )(*gs, *ws, *ms, *vs)
    return outs[:n], outs[n:2 * n], outs[2 * n:]


def _pad_rows(a, rows):
    return jnp.pad(a, ((0, rows - a.shape[0]), (0, 0)))


def kernel(x, meta_tokens, norm_w, conv_in_w, conv_w, conv_out_w, mlstm_in_w, mlstm_gate_b, mlstm_head_norm_w, mlstm_out_w, final_norm_w, loss_target, m_meta_tokens, m_norm_w, m_conv_in_w, m_conv_w, m_conv_out_w, m_mlstm_in_w, m_mlstm_gate_b, m_mlstm_head_norm_w, m_mlstm_out_w, m_final_norm_w, v_meta_tokens, v_norm_w, v_conv_in_w, v_conv_w, v_conv_out_w, v_mlstm_in_w, v_mlstm_gate_b, v_mlstm_head_norm_w, v_mlstm_out_w, v_final_norm_w):
    seq, d = x.shape[1], x.shape[2]
    t = seq + LEAD
    e = 2 * d
    ec = e // 2
    nh = N_HEADS
    dv = e // nh
    dk = dv // 2
    qk = nh * dk
    hw = 2 * dk + 3 * dv
    n_in = 2 * qk + 3 * e + 2 * nh
    n_in_s = n_in // N_DEV
    me = 4 * lax.axis_index("x") + 2 * lax.axis_index("y") + lax.axis_index("c")
    tm = _pick(t, 832, 16)
    tm_in = _pick(t, 1664, 16)
    tkw = _pick(t, 2080, 16)

    small = jnp.concatenate([
        meta_tokens,
        _pad_rows(conv_w[0].reshape(3 * (e // N_DEV) // LANES, LANES), 8),
        _pad_rows(mlstm_head_norm_w[0].reshape((e // N_DEV) // LANES, LANES), 8),
    ], axis=0) if d // N_DEV == LANES else None
    assert small is not None, "the packed small-weight block assumes d_model / 8 == 128"
    (small_g,) = _run_exchange(_gather_exchange([small]), "gather_small_weights")
    meta_full = jnp.transpose(small_g[:, 0:N_META, :], (1, 0, 2)).reshape(N_META, d)
    cw_rows = 3 * (e // N_DEV) // LANES
    conv_w_full = jnp.transpose(
        small_g[:, N_META:N_META + cw_rows, :].reshape(N_DEV, 3, e // N_DEV), (1, 0, 2)).reshape(3, e)
    hn_rows = (e // N_DEV) // LANES
    head_w_full = small_g[:, N_META + 8:N_META + 8 + hn_rows, :].reshape(1, e)

    h0 = jnp.concatenate([jnp.zeros((PAD_ROWS, d), F32), meta_full, x[0]], axis=0)
    tgt = loss_target[0]

    ci_map = lambda blk: 2 * (blk % 4) + blk // 4
    u0, (w_ci,) = _rms_fwd(h0, norm_w[0:1], "rms0", exchange=_gather_exchange([conv_in_w[0].astype(BF16)]))
    p0, (w_co, w_mi) = _matmul(
        u0, w_ci, form="nn", m=t, n=8 * ec, kdim=d, tm=tm_in, tn=ec, tk=d, out_dtype=BF16, name="conv_in",
        b_spec=pl.BlockSpec((None, d, ec), lambda i, j, k: (ci_map(j), 0, 0)),
        exchange=_gather_exchange([conv_out_w[0].astype(BF16), mlstm_in_w[0].astype(BF16)]))
    w_co = w_co.reshape(e, d)
    (g0, y0), (w_mo,) = _conv_gate_fwd(p0, conv_w_full, "conv_gate",
                                       exchange=_gather_exchange([mlstm_out_w[0].astype(BF16)]))
    w_mo = w_mo.reshape(e, d)
    h1 = _matmul(g0, w_co, form="nn", m=t, n=d, kdim=e, tm=tm, tn=d, tk=e, out_dtype=F32, name="conv_out",
                 residual=h0)

    assert n_in_s == d + 1 and w_mi.shape == (N_DEV, d, d + 1)
    w_p1, w_gate = _mlstm_weight_layout(w_mi, dk, dv, "mlstm_w_layout")
    w_gate_t = jnp.transpose(w_gate[:, 0:16])
    bias_row = jnp.pad(mlstm_gate_b, ((0, 0), (0, LANES - 2 * nh)))
    bias_col = jnp.pad(mlstm_gate_b.T, ((0, 16 - 2 * nh), (0, 0)))

    u1 = _rms_fwd(h1, norm_w[1:2], "rms1")
    p1 = _matmul(u1, w_p1, form="nn", m=t, n=nh * hw, kdim=d, tm=tm_in, tn=_pick(nh * hw, 1024, LANES), tk=d,
                 out_dtype=BF16, name="mlstm_in")
    graw = _matmul(u1, w_gate, form="nn", m=t, n=LANES, kdim=d, tm=tm, tn=LANES, tk=d, out_dtype=F32,
                   name="gates_col")
    graw_t = _matmul(w_gate_t, u1, form="nt", m=16, n=t, kdim=d, tm=16, tn=_pick(t, 1664, LANES), tk=d,
                     out_dtype=F32, name="gates_row")
    colf, rowf = _gates_fwd(graw, graw_t, bias_row, bias_col, "gates_fwd")
    hh, stat, csave, nsave = _mlstm_fwd(p1, colf, rowf, dk, dv, "mlstm_fwd")
    hm = _head_gate_fwd(hh, p1, head_w_full, dv, "head_gate")
    h2 =_matmul(hm, w_mo, form="nn", m=t, n=d, kdim=e, tm=tm, tn=d, tk=e, out_dtype=F32, name="mlstm_out",
                 residual=h1)

    dh2, dwf, loss_part = _final_loss(h2, final_norm_w.reshape(1, d), tgt, "final_loss")

    dhm = _matmul(dh2, w_mo, form="nt", m=t, n=e, kdim=d, tm=tm, tn=_pick(e, 2048, LANES), tk=d, out_dtype=BF16,
                  name="mlstm_out_dx")
    g_mo = _matmul(hm, dh2, form="tn", m=e, n=d, kdim=t, tm=_pick(e, 1024, LANES), tn=d, tk=tkw, out_dtype=BF16,
                   name="mlstm_out_dw")
    n_p1 = nh * hw + LANES
    (dp1, gstat, dhead), (r_mo,) = _mlstm_bwd(
        p1, colf, rowf, head_w_full, hh, stat, csave, nsave, dhm, n_p1, dk, dv, "mlstm_bwd",
        exchange=_scatter_exchange([g_mo.reshape(N_DEV, e // N_DEV, d)]))
    dp1, dbias = _gates_bwd(gstat, colf, dp1, "gates_bwd")
    du1 = _matmul(dp1, w_p1, form="nt", m=t, n=d, kdim=n_p1, tm=tm, tn=d, tk=_pick(n_p1, 2048, LANES),
                  out_dtype=F32, name="mlstm_in_dx")
    g_p1 = _matmul(u1, dp1, form="tn", m=d, n=n_p1, kdim=t, tm=d, tn=_pick(n_p1, 2048, LANES), tk=tkw,
                   out_dtype=BF16, name="mlstm_in_dw")
    dh1, dnw1 = _rms_bwd(h1, norm_w[1:2], du1, dh2, "rms1_bwd")

    g_mi = _mlstm_grad_layout(g_p1, dk, dv, "mlstm_g_layout")

    dg0 =_matmul(dh1, w_co, form="nt", m=t, n=e, kdim=d, tm=tm, tn=_pick(e, 2048, LANES), tk=d, out_dtype=BF16,
                  name="conv_out_dx")
    g_co = _matmul(g0, dh1, form="tn", m=e, n=d, kdim=t, tm=_pick(e, 1024, LANES), tn=d, tk=tkw, out_dtype=BF16,
                   name="conv_out_dw")
    (dp0, dconv), (r_mi,) = _conv_gate_bwd(p0, y0, dg0, conv_w_full, "conv_gate_bwd",
                                           exchange=_scatter_exchange([g_mi]))
    g_ci, (r_co,) = _matmul(
        u0, dp0, form="tn", m=d, n=8 * ec, kdim=t, tm=d, tn=ec, tk=tkw, out_dtype=BF16, name="conv_in_dw",
        out_shape=(N_DEV, d, ec), out_spec=pl.BlockSpec((None, d, ec), lambda i, j, k: (ci_map(j), 0, 0)),
        exchange=_scatter_exchange([g_co.reshape(N_DEV, e // N_DEV, d)]))
    du0, (r_ci,) = _matmul(
        dp0, w_ci, form="nt", m=t, n=d, kdim=8 * ec, tm=tm, tn=d, tk=ec, out_dtype=F32, name="conv_in_dx",
        b_spec=pl.BlockSpec((None, d, ec), lambda i, j, k: (ci_map(k), 0, 0)),
        exchange=_scatter_exchange([g_ci]))
    grad_x, dmeta, dnw0 = _rms_bwd_first(h0, norm_w[0:1], du0, dh1, "rms0_bwd")
    grad_x = grad_x[None]

    row8 = lax.broadcasted_iota(jnp.int32, (8, 1), 0)
    loss_wide = jnp.pad(loss_part, ((0, 0), (0, d - LANES)))
    payload = jnp.concatenate([
        jnp.where(row8 == 0, dnw0, jnp.where(row8 == 1, dnw1, 0.0)),
        jnp.where(row8 == 0, dwf, jnp.where(row8 == 1, loss_wide, 0.0)),
        jnp.where(row8 == 0, jnp.pad(dbias, ((0, 0), (0, d - LANES))), 0.0),
        dmeta,
        _pad_rows(dconv[0:3].reshape(3 * e // d, d), 8),
        _pad_rows(dhead[:, 0, :].reshape(e // d, d), 8),
    ], axis=0)
    (payload_g,) = _run_exchange(_gather_exchange([payload]), "gather_small_grads")
    tot = _sum_devices(payload_g, "sum_small_grads")

    loss = tot[9, 0]
    g_norm = tot[0:2]
    g_final = tot[8]
    g_gate_b = tot[16:17, 0:2 * nh]
    g_meta = lax.dynamic_slice(tot[24:24 + N_META], (0, me * (d // N_DEV)), (N_META, d // N_DEV))
    g_conv_w = lax.dynamic_slice(tot[40:40 + 3 * e // d].reshape(3, e), (0, me * (e // N_DEV)), (3, e // N_DEV))
    g_head = lax.dynamic_slice(tot[48:48 + e // d].reshape(1, e), (0, me * (e // N_DEV)), (1, e // N_DEV))

    g1, d1, nm1, nv1 = _adamw_sharded(r_ci, conv_in_w[0], m_conv_in_w[0], v_conv_in_w[0], "adamw_conv_in")
    g2, d2, nm2, nv2 = _adamw_sharded(r_co, conv_out_w[0], m_conv_out_w[0], v_conv_out_w[0], "adamw_conv_out")
    g3, d3, nm3, nv3 = _adamw_sharded(r_mi, mlstm_in_w[0], m_mlstm_in_w[0], v_mlstm_in_w[0], "adamw_mlstm_in")
    g4, d4, nm4, nv4 = _adamw_sharded(r_mo, mlstm_out_w[0], m_mlstm_out_w[0], v_mlstm_out_w[0], "adamw_mlstm_out")

    small_g = [g_meta, g_norm, g_conv_w, g_gate_b, g_head, g_final.reshape(1, d)]
    small_w = [meta_tokens, norm_w, conv_w[0], mlstm_gate_b, mlstm_head_norm_w, final_norm_w.reshape(1, d)]
    small_m = [m_meta_tokens, m_norm_w, m_conv_w[0], m_mlstm_gate_b, m_mlstm_head_norm_w, m_final_norm_w.reshape(1, d)]
    small_v = [v_meta_tokens, v_norm_w, v_conv_w[0], v_mlstm_gate_b, v_mlstm_head_norm_w, v_final_norm_w.reshape(1, d)]
    sd, snm, snv = _adamw_small(small_g, small_w, small_m, small_v, "adamw_small")

    def order(meta, norm, cin, cw, cout, min_, gb, hn, mout, fin):
        return (meta, norm, cin[None], cw[None], cout[None], min_[None], gb, hn, mout[None], fin.reshape(d))

    grads = order(g_meta, g_norm, g1, g_conv_w, g2, g3, g_gate_b, g_head, g4, g_final)
    deltas = order(sd[0], sd[1], d1, sd[2], d2, d3, sd[3], sd[4], d4, sd[5])
    new_m = order(snm[0], snm[1], nm1, snm[2], nm2, nm3, snm[3], snm[4], nm4, snm[5])
    new_v = order(snv[0], snv[1], nv1, snv[2], nv2, nv3, snv[3], snv[4], nv4, snv[5])
    return (loss, grad_x, *grads, *deltas, *new_m, *new_v)
```

```python
import functools

import jax
import jax.numpy as jnp
from jax import lax
from jax.experimental import pallas as pl
from jax.experimental.pallas import tpu as pltpu

F32 = jnp.float32
BF16 = jnp.bfloat16
MESH = pl.DeviceIdType.MESH

N_DEV = 8
N_META = 16
N_HEADS = 4
CHUNK = 64
CHUNKS_PER_STEP = 2
LEAD = 128
PAD_ROWS = LEAD - N_META
RMS_EPS = 1e-6
NEG = -1e30
LANES = 128
VMEM_LIMIT = 48 * 1024 * 1024

ADAM_LR = 0.001
ADAM_B1 = 0.9
ADAM_B2 = 0.999
ADAM_EPS = 1e-08
ADAM_WD = 0.01
ADAM_STEP = 10

HIGHEST = lax.Precision.HIGHEST


def _pick(n, target, mult):
    best = None
    for d in range(mult, min(n, target) + 1, mult):
        if n % d == 0:
            best = d
    return n if best is None else best


def _params(**kw):
    return pltpu.CompilerParams(vmem_limit_bytes=VMEM_LIMIT, **kw)


def _sigmoid(x):
    return 0.5 * jnp.tanh(0.5 * x) + 0.5


STRIP_ROWS = 16
STRIP_COLS = 256


def _strips(n_rows, n_cols):
    for c0 in range(0, n_cols, STRIP_COLS):
        for r0 in range(0, n_rows, STRIP_ROWS):
            yield slice(r0, r0 + STRIP_ROWS), slice(c0, min(c0 + STRIP_COLS, n_cols))


def _shift(sl, by):
    return slice(sl.start + by, sl.stop + by)


def _call(body, *, name, grid, in_specs, out_specs, out_shape, args, scratch_shapes=(), aliases=None,
          exchange=None):
    aliases = {} if aliases is None else aliases
    if exchange is None:
        return pl.pallas_call(
            body, name=name, grid=grid, in_specs=list(in_specs), out_specs=out_specs, out_shape=out_shape,
            scratch_shapes=list(scratch_shapes), input_output_aliases=aliases,
            compiler_params=_params())(*args)
    ex = exchange
    single = not isinstance(out_shape, (list, tuple))
    shapes = [out_shape] if single else list(out_shape)
    specs = [out_specs] if single else list(out_specs)
    n_in, n_out, n_scr = len(in_specs), len(shapes), len(scratch_shapes)
    n_ex_in, n_ex_out = len(ex.ins), len(ex.out_shapes)
    steps = 1
    for size in grid:
        steps *= size

    def wrapped(*refs):
        own_in, ex_in = refs[:n_in], refs[n_in:n_in + n_ex_in]
        at = n_in + n_ex_in
        own_out, ex_out = refs[at:at + n_out], refs[at + n_out:at + n_out + n_ex_out]
        at += n_out + n_ex_out
        own_scr, ex_scr = refs[at:at + n_scr], refs[at + n_scr:]
        step = 0
        for axis, size in enumerate(grid):
            step = step * size + pl.program_id(axis)

        @pl.when(step == 0)
        def _():
            ex.start(ex_in, ex_out, ex_scr)

        if ex.mid is not None:
            @pl.when(step == (3 * steps) // 4)
            def _():
                ex.mid(ex_in, ex_out, ex_scr)

        body(*own_in, *own_out, *own_scr)

        @pl.when(step == steps - 1)
        def _():
            ex.finish(ex_in, ex_out, ex_scr)

    any_spec = pl.BlockSpec(memory_space=pl.ANY)
    res = pl.pallas_call(
        wrapped, name=name, grid=grid,
        in_specs=list(in_specs) + [any_spec] * n_ex_in, out_specs=specs + [any_spec] * n_ex_out,
        out_shape=shapes + ex.out_shapes, scratch_shapes=list(scratch_shapes) + ex.scratch,
        input_output_aliases=aliases, compiler_params=_params())(*args, *ex.ins)
    return (res[0] if single else res[:n_out]), res[n_out:]


def _matmul(a, b, *, form, m, n, kdim, tm, tn, tk, out_dtype, name,
            a_spec=None, b_spec=None, out_spec=None, out_shape=None, residual=None, exchange=None):
    ni, nj, nk = m // tm, n // tn, kdim // tk
    assert ni * tm == m and nj * tn == n and nk * tk == kdim, (name, m, n, kdim, tm, tn, tk)
    if form == "nn":
        dn = (((1,), (0,)), ((), ()))
        a_def = pl.BlockSpec((tm, tk), lambda i, j, k: (i, k))
        b_def = pl.BlockSpec((tk, tn), lambda i, j, k: (k, j))
    elif form == "nt":
        dn = (((1,), (1,)), ((), ()))
        a_def = pl.BlockSpec((tm, tk), lambda i, j, k: (i, k))
        b_def = pl.BlockSpec((tn, tk), lambda i, j, k: (j, k))
    else:
        dn = (((0,), (0,)), ((), ()))
        a_def = pl.BlockSpec((tk, tm), lambda i, j, k: (k, i))
        b_def = pl.BlockSpec((tk, tn), lambda i, j, k: (k, j))
    a_spec = a_def if a_spec is None else a_spec
    b_spec = b_def if b_spec is None else b_spec
    o_spec = pl.BlockSpec((tm, tn), lambda i, j, k: (i, j)) if out_spec is None else out_spec
    has_res = residual is not None

    def body(*refs):
        a_ref, b_ref = refs[:2]
        r_ref = refs[2] if has_res else None
        o_ref = refs[2 + has_res]

        def product():
            return lax.dot_general(a_ref[...].astype(BF16), b_ref[...].astype(BF16), dn,
                                   preferred_element_type=F32)

        def finish(acc):
            if has_res:
                acc = acc + r_ref[...].astype(F32)
            o_ref[...] = acc.astype(o_ref.dtype)

        if nk == 1:
            finish(product())
        else:
            acc_ref = refs[3 + has_res]
            k = pl.program_id(2)

            @pl.when(k == 0)
            def _():
                acc_ref[...] = jnp.zeros_like(acc_ref)

            acc_ref[...] += product()

            @pl.when(k == nk - 1)
            def _():
                finish(acc_ref[...])

    in_specs = [a_spec, b_spec]
    args = [a, b]
    if has_res:
        in_specs.append(pl.BlockSpec((tm, tn), lambda i, j, k: (i, j)))
        args.append(residual)
    return _call(
        body, name=name, grid=(ni, nj, nk), in_specs=in_specs, out_specs=o_spec,
        out_shape=jax.ShapeDtypeStruct((m, n) if out_shape is None else out_shape, out_dtype),
        scratch_shapes=[] if nk == 1 else [pltpu.VMEM((tm, tn), F32)], args=args, exchange=exchange)


def _rms_fwd(h, w, name, exchange=None):
    t, d = h.shape
    tm = _pick(t, 416, 16)

    def body(h_ref, w_ref, u_ref):
        x = h_ref[...]
        r = lax.rsqrt(jnp.mean(x * x, axis=-1, keepdims=True) + RMS_EPS)
        u_ref[...] = ((x * r) * w_ref[...]).astype(BF16)

    return _call(
        body, name=name, grid=(t // tm,),
        in_specs=[pl.BlockSpec((tm, d), lambda i: (i, 0)), pl.BlockSpec((1, d), lambda i: (0, 0))],
        out_specs=pl.BlockSpec((tm, d), lambda i: (i, 0)),
        out_shape=jax.ShapeDtypeStruct((t, d), BF16), args=(h, w), exchange=exchange)


def _rms_bwd(h, w, du, dres, name):
    t, d = h.shape
    tm = _pick(t, 416, 16)

    def body(h_ref, w_ref, du_ref, dres_ref, dh_ref, dw_ref):
        i = pl.program_id(0)
        x = h_ref[...]
        g = du_ref[...].astype(F32)
        r = lax.rsqrt(jnp.mean(x * x, axis=-1, keepdims=True) + RMS_EPS)
        gw = g * w_ref[...]
        dot = jnp.mean(gw * x, axis=-1, keepdims=True)
        dh_ref[...] = dres_ref[...] + (r * gw - x * ((r * r * r) * dot))
        part = jnp.sum(g * (x * r), axis=0, keepdims=True)

        @pl.when(i == 0)
        def _():
            dw_ref[...] = jnp.zeros_like(dw_ref)

        dw_ref[...] += jnp.broadcast_to(part, dw_ref.shape)

    row = pl.BlockSpec((tm, d), lambda i: (i, 0))
    return pl.pallas_call(
        body, name=name, grid=(t // tm,),
        in_specs=[row, pl.BlockSpec((1, d), lambda i: (0, 0)), row, row],
        out_specs=[row, pl.BlockSpec((8, d), lambda i: (0, 0))],
        out_shape=[jax.ShapeDtypeStruct((t, d), F32), jax.ShapeDtypeStruct((8, d), F32)],
        compiler_params=_params(),
    )(h, w, du, dres)


def _rms_bwd_first(h, w, du, dres, name):
    t, d = h.shape
    tm = LEAD

    def body(h_ref, w_ref, du_ref, dres_ref, gx_ref, dmeta_ref, dw_ref):
        i = pl.program_id(0)
        x = h_ref[...]
        g = du_ref[...].astype(F32)
        r = lax.rsqrt(jnp.mean(x * x, axis=-1, keepdims=True) + RMS_EPS)
        gw = g * w_ref[...]
        dot = jnp.mean(gw * x, axis=-1, keepdims=True)
        dh = dres_ref[...] + (r * gw - x * ((r * r * r) * dot))
        part = jnp.sum(g * (x * r), axis=0, keepdims=True)

        @pl.when(i == 0)
        def _():
            dw_ref[...] = jnp.zeros_like(dw_ref)
            dmeta_ref[...] = dh[PAD_ROWS:LEAD, :]

        @pl.when(i > 0)
        def _():
            gx_ref[...] = dh

        dw_ref[...] += jnp.broadcast_to(part, dw_ref.shape)

    row = pl.BlockSpec((tm, d), lambda i: (i, 0))
    return pl.pallas_call(
        body, name=name, grid=(t // tm,),
        in_specs=[row, pl.BlockSpec((1, d), lambda i: (0, 0)), row, row],
        out_specs=[pl.BlockSpec((tm, d), lambda i: (jnp.maximum(i - 1, 0), 0)),
                   pl.BlockSpec((N_META, d), lambda i: (0, 0)), pl.BlockSpec((8, d), lambda i: (0, 0))],
        out_shape=[jax.ShapeDtypeStruct((t - LEAD, d), F32), jax.ShapeDtypeStruct((N_META, d), F32),
                   jax.ShapeDtypeStruct((8, d), F32)],
        compiler_params=_params(),
    )(h, w, du, dres)


def _final_loss(h, w, tgt, name):
    t, d = h.shape
    sub = LEAD
    per = _pick(t // sub, 5, 1)
    tm = per * sub

    def body(h_ref, w_ref, *rest):
        t_refs, (dh_ref, dw_ref, loss_ref) = rest[:per], rest[per:]
        i = pl.program_id(0)

        @pl.when(i == 0)
        def _():
            dw_ref[...] = jnp.zeros_like(dw_ref)
            loss_ref[...] = jnp.zeros_like(loss_ref)

        for q in range(per):
            sl = slice(q * sub, (q + 1) * sub)
            x = h_ref[sl, :]
            r = lax.rsqrt(jnp.mean(x * x, axis=-1, keepdims=True) + RMS_EPS)
            yn = x * r
            y = yn * w_ref[...]
            rows = i * tm + q * sub + lax.broadcasted_iota(jnp.int32, (sub, 1), 0)
            diff = jnp.where(rows >= LEAD, y - t_refs[q][...], 0.0)
            tile_loss = 0.5 * jnp.sum(jnp.mean(diff * diff, axis=-1, keepdims=True), axis=0, keepdims=True)
            dy = diff / d
            gw = dy * w_ref[...]
            dot = jnp.mean(gw * x, axis=-1, keepdims=True)
            dh_ref[sl, :] = r * gw - x * ((r * r * r) * dot)
            dw_ref[...] += jnp.broadcast_to(jnp.sum(dy * yn, axis=0, keepdims=True), dw_ref.shape)
            loss_ref[...] += jnp.broadcast_to(tile_loss, loss_ref.shape)

    row = pl.BlockSpec((tm, d), lambda i: (i, 0))
    piece = [pl.BlockSpec((sub, d), functools.partial(lambda i, q: (jnp.maximum(i * per + q - 1, 0), 0), q=q))
             for q in range(per)]
    return pl.pallas_call(
        body, name=name, grid=(t // tm,),
        in_specs=[row, pl.BlockSpec((1, d), lambda i: (0, 0))] + piece,
        out_specs=[row, pl.BlockSpec((8, d), lambda i: (0, 0)), pl.BlockSpec((8, LANES), lambda i: (0, 0))],
        out_shape=[jax.ShapeDtypeStruct((t, d), F32), jax.ShapeDtypeStruct((8, d), F32),
                   jax.ShapeDtypeStruct((8, LANES), F32)],
        compiler_params=_params(),
    )(h, w, *([tgt] * per))


def _conv_gate_fwd(p0, conv_w, name, exchange=None):
    t = p0.shape[0]
    ec = p0.shape[1] // 8
    tm = _pick(t, 416, 16)
    nt = t // tm

    def body(p_ref, w_ref, g_ref, y_ref, ext_ref):
        i = pl.program_id(1)

        @pl.when(i == 0)
        def _():
            ext_ref[0:8, :] = jnp.zeros((8, ec), F32)

        for rows, cols in _strips(tm, ec):
            cg = p_ref[rows, _shift(cols, ec)].astype(F32)
            xin = p_ref[rows, _shift(cols, 2 * ec)].astype(F32)
            ext_ref[_shift(rows, 8), cols] = cg * xin
        for rows, cols in _strips(tm, ec):
            y = (w_ref[0:1, cols] * ext_ref[_shift(rows, 6), cols] + w_ref[1:2, cols] * ext_ref[_shift(rows, 7), cols]
                 + w_ref[2:3, cols] * ext_ref[_shift(rows, 8), cols])
            bg = p_ref[rows, cols].astype(F32)
            z = p_ref[rows, _shift(cols, 3 * ec)].astype(F32)
            y_ref[rows, cols] = y.astype(BF16)
            g_ref[rows, cols] = ((z * _sigmoid(z)) * bg * y).astype(BF16)
        ext_ref[0:8, :] = ext_ref[tm:tm + 8, :]

    out = pl.BlockSpec((tm, ec), lambda j, i: (i, j))
    return _call(
        body, name=name, grid=(2, nt),
        in_specs=[pl.BlockSpec((tm, 4 * ec), lambda j, i: (i, j)), pl.BlockSpec((3, ec), lambda j, i: (0, j))],
        out_specs=[out, out],
        out_shape=[jax.ShapeDtypeStruct((t, 2 * ec), BF16)] * 2,
        scratch_shapes=[pltpu.VMEM((tm + 8, ec), F32)], args=(p0, conv_w), exchange=exchange)


def _conv_gate_bwd(p0, y, dg, conv_w, name, exchange=None):
    t = p0.shape[0]
    ec = p0.shape[1] // 8
    tm = _pick(t, 208, 16)
    nt = t // tm

    def body(p_ref, y_ref, dg_ref, w_ref, dp_ref, dw_ref, ext_ref):
        i = pl.program_id(1)

        @pl.when(i == 0)
        def _():
            ext_ref[tm:tm + 8, :] = jnp.zeros((8, ec), F32)
            dw_ref[...] = jnp.zeros_like(dw_ref)

        for rows, cols in _strips(tm, ec):
            bg = p_ref[rows, cols].astype(F32)
            z = p_ref[rows, _shift(cols, 3 * ec)].astype(F32)
            yv = y_ref[rows, cols].astype(F32)
            dgv = dg_ref[rows, cols].astype(F32)
            sig = _sigmoid(z)
            sz = z * sig
            dp_ref[rows, _shift(cols, 3 * ec)] = (dgv * bg * yv * (sig * (1.0 + z * (1.0 - sig)))).astype(BF16)
            dp_ref[rows, cols] = (dgv * sz * yv).astype(BF16)
            ext_ref[rows, cols] = dgv * sz * bg
        acc = None
        for rows, cols in _strips(tm, ec):
            if rows.start == 0:
                acc = [jnp.zeros((STRIP_ROWS, cols.stop - cols.start), F32) for _ in range(3)]
            dy = ext_ref[rows, cols]
            dy1 = ext_ref[_shift(rows, 1), cols]
            dy2 = ext_ref[_shift(rows, 2), cols]
            cg = p_ref[rows, _shift(cols, ec)].astype(F32)
            xin = p_ref[rows, _shift(cols, 2 * ec)].astype(F32)
            da = w_ref[0:1, cols] * dy2 + w_ref[1:2, cols] * dy1 + w_ref[2:3, cols] * dy
            dp_ref[rows, _shift(cols, ec)] = (da * xin).astype(BF16)
            dp_ref[rows, _shift(cols, 2 * ec)] = (da * cg).astype(BF16)
            a = cg * xin
            acc = [acc[0] + a * dy2, acc[1] + a * dy1, acc[2] + a * dy]
            if rows.stop == tm:
                for tap in range(3):
                    dw_ref[tap:tap + 1, cols] += jnp.sum(acc[tap], axis=0, keepdims=True)
        ext_ref[tm:tm + 8, :] = ext_ref[0:8, :]

    rev = lambda j, i: (nt - 1 - i, j)
    return _call(
        body, name=name, grid=(2, nt),
        in_specs=[pl.BlockSpec((tm, 4 * ec), rev), pl.BlockSpec((tm, ec), rev), pl.BlockSpec((tm, ec), rev),
                  pl.BlockSpec((3, ec), lambda j, i: (0, j))],
        out_specs=[pl.BlockSpec((tm, 4 * ec), rev), pl.BlockSpec((8, ec), lambda j, i: (0, j))],
        out_shape=[jax.ShapeDtypeStruct((t, 8 * ec), BF16), jax.ShapeDtypeStruct((8, 2 * ec), F32)],
        scratch_shapes=[pltpu.VMEM((tm + 8, ec), F32)], args=(p0, y, dg, conv_w), exchange=exchange)


def _log_sigmoid(x):
    return jnp.minimum(x, 0.0) - jnp.log(1.0 + jnp.exp(-jnp.abs(x)))


def _gates_fwd(graw, graw_t, bias_row, bias_col, name):
    t = graw.shape[0]
    tm = _pick(t, 640, 128)
    cpt = tm // CHUNK
    nh = N_HEADS

    def body(g_ref, gt_ref, br_ref, bc_ref, colf_ref, rowf_ref):
        i = pl.program_id(0)
        gc = g_ref[...] + br_ref[...]
        rows = i * tm + lax.broadcasted_iota(jnp.int32, (tm, 1), 0)
        live = rows >= PAD_ROWS
        lf = jnp.where(live, _log_sigmoid(gc), 0.0)
        li = jnp.where(live, gc, NEG)
        gt = gt_ref[...] + bc_ref[...]
        cols = i * tm + lax.broadcasted_iota(jnp.int32, (1, tm), 1)
        live_t = cols >= PAD_ROWS
        lf_t = jnp.where(live_t, _log_sigmoid(gt), 0.0)
        li_t = jnp.where(live_t, gt, NEG)
        ri = lax.broadcasted_iota(jnp.int32, (CHUNK, CHUNK), 0)
        ci = lax.broadcasted_iota(jnp.int32, (CHUNK, CHUNK), 1)
        lower = (ri >= ci).astype(F32)
        upper = (ri <= ci).astype(F32)
        lane = lax.broadcasted_iota(jnp.int32, (CHUNK, LANES), 1)
        sub = lax.broadcasted_iota(jnp.int32, (8, CHUNK), 0)
        for c in range(cpt):
            sl = slice(c * CHUNK, (c + 1) * CHUNK)
            b_all = jnp.dot(lower, lf[sl, :], precision=HIGHEST, preferred_element_type=F32)
            bt_all = jnp.dot(lf_t[:, sl], upper, precision=HIGHEST, preferred_element_type=F32)
            for h in range(nh):
                b = b_all[:, nh + h:nh + h + 1]
                r = li[sl, h:h + 1] - b
                pre = gc[sl, nh + h:nh + h + 1]
                colf_ref[h, sl, :] = jnp.where(lane == 0, b, jnp.where(lane == 1, r, jnp.where(lane == 2, pre, 0.0)))
                b_row = bt_all[nh + h:nh + h + 1, :]
                r_row = li_t[h:h + 1, sl] - b_row
                rowf_ref[h, c, :, :] = jnp.where(sub == 0, r_row, jnp.where(sub == 1, b_row, 0.0))

    return pl.pallas_call(
        body, name=name, grid=(t // tm,),
        in_specs=[pl.BlockSpec((tm, LANES), lambda i: (i, 0)), pl.BlockSpec((16, tm), lambda i: (0, i)),
                  pl.BlockSpec((1, LANES), lambda i: (0, 0)), pl.BlockSpec((16, 1), lambda i: (0, 0))],
        out_specs=[pl.BlockSpec((nh, tm, LANES), lambda i: (0, i, 0)),
                   pl.BlockSpec((nh, cpt, 8, CHUNK), lambda i: (0, i, 0, 0))],
        out_shape=[jax.ShapeDtypeStruct((nh, t, LANES), F32),
                   jax.ShapeDtypeStruct((nh, t // CHUNK, 8, CHUNK), F32)],
        compiler_params=_params(),
    )(graw, graw_t, bias_row, bias_col)


def _gates_bwd(gstat, colf, dp1, name):
    nh, t, _ = gstat.shape
    tm = _pick(t, 640, 128)
    cpt = tm // CHUNK
    nt = t // tm
    nc = t // CHUNK
    gate_block = dp1.shape[1] // LANES - 1

    def body(gs_ref, nx_ref, colf_ref, dp_any, dg_ref, db_ref):
        i = pl.program_id(0)
        ri = lax.broadcasted_iota(jnp.int32, (CHUNK, CHUNK), 0)
        ci = lax.broadcasted_iota(jnp.int32, (CHUNK, CHUNK), 1)
        upper = (ri <= ci).astype(F32)
        lane = lax.broadcasted_iota(jnp.int32, (CHUNK, LANES), 1)
        total = jnp.zeros((1, LANES), F32)
        for c in range(cpt):
            sl = slice(c * CHUNK, (c + 1) * CHUNK)
            rows = i * tm + c * CHUNK + lax.broadcasted_iota(jnp.int32, (CHUNK, 1), 0)
            live = rows >= PAD_ROWS
            acc = jnp.zeros((CHUNK, LANES), F32)
            for h in range(nh):
                blk = gs_ref[h, sl, :]
                rev = jnp.dot(upper, blk, precision=HIGHEST, preferred_element_type=F32)
                if c + 1 < cpt:
                    carry = gs_ref[h, (c + 1) * CHUNK:(c + 1) * CHUNK + 1, 2:3]
                else:
                    carry = jnp.where(i == nt - 1, 0.0, nx_ref[h, 0:1, 2:3])
                dlogf = rev[:, 0:1] + carry
                pre = colf_ref[h, sl, 2:3]
                dgf = jnp.where(live, dlogf * (1.0 - _sigmoid(pre)), 0.0)
                dgi = jnp.where(live, blk[:, 1:2], 0.0)
                acc = acc + jnp.where(lane == h, dgi, 0.0) + jnp.where(lane == nh + h, dgf, 0.0)
            dg_ref[sl, :] = acc.astype(BF16)
            total = total + jnp.sum(acc, axis=0, keepdims=True)

        @pl.when(i == 0)
        def _():
            db_ref[...] = jnp.zeros_like(db_ref)

        db_ref[...] += jnp.broadcast_to(total, db_ref.shape)

    return pl.pallas_call(
        body, name=name, grid=(nt,),
        in_specs=[pl.BlockSpec((nh, tm, LANES), lambda i: (0, i, 0)),
                  pl.BlockSpec((nh, CHUNK, LANES), lambda i: (0, jnp.minimum((i + 1) * cpt, nc - 1), 0)),
                  pl.BlockSpec((nh, tm, LANES), lambda i: (0, i, 0)),
                  pl.BlockSpec(memory_space=pl.ANY)],
        out_specs=[pl.BlockSpec((tm, LANES), lambda i: (i, gate_block)), pl.BlockSpec((8, LANES), lambda i: (0, 0))],
        out_shape=[jax.ShapeDtypeStruct(dp1.shape, dp1.dtype), jax.ShapeDtypeStruct((8, LANES), F32)],
        input_output_aliases={3: 0},
        compiler_params=_params(),
    )(gstat, gstat, colf, dp1)


NT_DIMS = (((1,), (1,)), ((), ()))
TN_DIMS = (((0,), (0,)), ((), ()))


def _dot(a, b):
    return jnp.dot(a.astype(BF16), b.astype(BF16), preferred_element_type=F32)


def _dot_nt(a, b):
    return lax.dot_general(a.astype(BF16), b.astype(BF16), NT_DIMS, preferred_element_type=F32)


def _dot_tn(a, b):
    return lax.dot_general(a.astype(BF16), b.astype(BF16), TN_DIMS, preferred_element_type=F32)


def _chunk_gates(colf_ref, rowf_ref, m_prev):
    b = colf_ref[:, 0:1]
    rcol = colf_ref[:, 1:2]
    rrow = rowf_ref[0:1, :]
    ri = lax.broadcasted_iota(jnp.int32, (CHUNK, CHUNK), 0)
    ci = lax.broadcasted_iota(jnp.int32, (CHUNK, CHUNK), 1)
    log_d = jnp.where(ri >= ci, b + rrow, NEG)
    m_row = jnp.maximum(b + m_prev, jnp.max(log_d, axis=-1, keepdims=True))
    dmat = jnp.exp(log_d - m_row)
    inter = jnp.exp(b + m_prev - m_row)
    b_last = b[CHUNK - 1:CHUNK, :]
    log_w = rcol + b_last
    m_new = jnp.maximum(b_last + m_prev, jnp.max(log_w, axis=0, keepdims=True))
    decay = jnp.exp(b_last + m_prev - m_new)
    w = jnp.exp(log_w - m_new)
    return m_row, dmat, inter, m_new, decay, w


def _mlstm_fwd(p1, colf, rowf, dk, dv, name):
    t = p1.shape[0]
    nh = N_HEADS
    nc = t // CHUNK
    hw = 2 * dk + dv
    cps = CHUNKS_PER_STEP if nc % CHUNKS_PER_STEP == 0 else 1
    rows_per_step = cps * CHUNK

    def body(p_ref, colf_ref, rowf_ref, hh_ref, stat_ref, cs_ref, ns_ref, c_ref, n_ref, m_ref):
        c = pl.program_id(0)

        @pl.when(c == 0)
        def _():
            c_ref[...] = jnp.zeros_like(c_ref)
            n_ref[...] = jnp.zeros_like(n_ref)
            m_ref[...] = jnp.zeros_like(m_ref)

        for cc in range(cps):
            rows = pl.ds(cc * CHUNK, CHUNK)
            for h in range(nh):
                cols = pl.ds(h * dv, dv)
                head(p_ref.at[rows, pl.ds(h * hw, hw)], colf_ref.at[h, rows], rowf_ref.at[h, cc],
                     hh_ref.at[rows, cols], stat_ref.at[h, rows], cs_ref.at[h, cc],
                     ns_ref.at[h, cc], c_ref.at[h], n_ref.at[h], m_ref.at[h])

    def head(p_ref, colf_ref, rowf_ref, hh_ref, stat_ref, cs_ref, ns_ref, c_ref, n_ref, m_ref):
        m_prev = m_ref[...]
        n_prev = n_ref[...]
        c_prev = c_ref[...]
        cs_ref[...] = c_prev.astype(BF16)
        sub = lax.broadcasted_iota(jnp.int32, (8, dk), 0)
        ns_ref[...] = jnp.where(sub == 0, n_prev, jnp.where(sub == 1, m_prev, 0.0))

        q = p_ref[:, 0:dk]
        k = p_ref[:, dk:2 * dk]
        v = p_ref[:, 2 * dk:2 * dk + dv]
        m_row, dmat, inter, m_new, decay, w = _chunk_gates(colf_ref, rowf_ref, m_prev)
        s = _dot_nt(q, k) * dmat
        num = _dot(s, v) + inter * _dot(q, c_prev)
        den = jnp.sum(s, axis=-1, keepdims=True) + inter * jnp.sum(q.astype(F32) * n_prev, axis=-1, keepdims=True)
        denom = jnp.maximum(jnp.abs(den), jnp.exp(-m_row))
        hh_ref[...] = num * (1.0 / denom)
        lane = lax.broadcasted_iota(jnp.int32, (CHUNK, LANES), 1)
        stat_ref[...] = jnp.where(lane == 0, den, 0.0)

        wk = w * k.astype(F32)
        c_ref[...] = decay * c_prev + _dot_tn(wk, v)
        n_ref[...] = decay * n_prev + jnp.sum(wk, axis=0, keepdims=True)
        m_ref[...] = m_new

    return pl.pallas_call(
        body, name=name, grid=(nc // cps,),
        in_specs=[pl.BlockSpec((rows_per_step, nh * hw), lambda c: (c, 0)),
                  pl.BlockSpec((nh, rows_per_step, LANES), lambda c: (0, c, 0)),
                  pl.BlockSpec((nh, cps, 8, CHUNK), lambda c: (0, c, 0, 0))],
        out_specs=[pl.BlockSpec((rows_per_step, nh * dv), lambda c: (c, 0)),
                   pl.BlockSpec((nh, rows_per_step, LANES), lambda c: (0, c, 0)),
                   pl.BlockSpec((nh, cps, dk, dv), lambda c: (0, c, 0, 0)),
                   pl.BlockSpec((nh, cps, 8, dk), lambda c: (0, c, 0, 0))],
        out_shape=[jax.ShapeDtypeStruct((t, nh * dv), F32), jax.ShapeDtypeStruct((nh, t, LANES), F32),
                   jax.ShapeDtypeStruct((nh, nc, dk, dv), BF16), jax.ShapeDtypeStruct((nh, nc, 8, dk), F32)],
        scratch_shapes=[pltpu.VMEM((nh, dk, dv), F32), pltpu.VMEM((nh, 1, dk), F32), pltpu.VMEM((nh, 1, 1), F32)],
        compiler_params=_params(),
    )(p1, colf, rowf)


def _head_gate_fwd(hh, p1, head_w, dv, name):
    t = hh.shape[0]
    nh = N_HEADS
    e = nh * dv
    tm = _pick(t, 416, 16)

    def body(hh_ref, oz_ref, w_ref, hm_ref):
        for h in range(nh):
            cols = slice(h * dv, (h + 1) * dv)
            x = hh_ref[:, cols]
            o = oz_ref[:, 2 * h * dv:(2 * h + 1) * dv].astype(F32)
            z = oz_ref[:, (2 * h + 1) * dv:(2 * h + 2) * dv].astype(F32)
            r = lax.rsqrt(jnp.mean(x * x, axis=-1, keepdims=True) + RMS_EPS)
            hn = (x * r) * w_ref[:, cols]
            hm_ref[:, cols] = (hn * _sigmoid(o) * (z * _sigmoid(z))).astype(BF16)

    return pl.pallas_call(
        body, name=name, grid=(t // tm,),
        in_specs=[pl.BlockSpec((tm, e), lambda i: (i, 0)), pl.BlockSpec((tm, 2 * e), lambda i: (i, 1)),
                  pl.BlockSpec((1, e), lambda i: (0, 0))],
        out_specs=pl.BlockSpec((tm, e), lambda i: (i, 0)),
        out_shape=jax.ShapeDtypeStruct((t, e), BF16),
        compiler_params=_params(),
    )(hh, p1, head_w)


def _mlstm_bwd(p1, colf, rowf, head_w, hh, stat, csave, nsave, dhm, n_cols, dk, dv, name, exchange=None):
    t = p1.shape[0]
    nh = N_HEADS
    nc = t // CHUNK
    hw = 2 * dk + dv
    cps = CHUNKS_PER_STEP if nc % CHUNKS_PER_STEP == 0 else 1
    rows_per_step = cps * CHUNK

    def body(p_ref, colf_ref, rowf_ref, w_ref, hh_ref, stat_ref, cs_ref, ns_ref, dhm_ref,
             dp_ref, gs_ref, dw_ref, dc_ref, dn_ref):
        c = pl.program_id(0)

        @pl.when(c == 0)
        def _():
            dc_ref[...] = jnp.zeros_like(dc_ref)
            dn_ref[...] = jnp.zeros_like(dn_ref)
            dw_ref[...] = jnp.zeros_like(dw_ref)

        for cc in reversed(range(cps)):
            rows = pl.ds(cc * CHUNK, CHUNK)
            for h in range(nh):
                cols = pl.ds(h * dv, dv)
                oz = pl.ds(nh * hw + h * 2 * dv, 2 * dv)
                head(p_ref.at[rows, pl.ds(h * hw, hw)], p_ref.at[rows, oz], colf_ref.at[h, rows], rowf_ref.at[h, cc],
                     w_ref.at[:, cols], hh_ref.at[rows, cols], stat_ref.at[h, rows], cs_ref.at[h, cc],
                     ns_ref.at[h, cc], dhm_ref.at[rows, cols], dp_ref.at[rows, pl.ds(h * hw, hw)],
                     dp_ref.at[rows, oz], gs_ref.at[h, rows], dw_ref.at[h], dc_ref.at[h], dn_ref.at[h])

    def head(p_ref, oz_ref, colf_ref, rowf_ref, w_ref, hh_ref, stat_ref, cs_ref, ns_ref, dhm_ref,
             dp_ref, doz_ref, gs_ref, dw_ref, dc_ref, dn_ref):
        q = p_ref[:, 0:dk]
        k = p_ref[:, dk:2 * dk]
        v = p_ref[:, 2 * dk:2 * dk + dv]
        o = oz_ref[:, 0:dv].astype(F32)
        z = oz_ref[:, dv:2 * dv].astype(F32)
        qf = q.astype(F32)
        kf = k.astype(F32)
        n_prev = ns_ref[0:1, :]
        m_prev = ns_ref[1:2, 0:1]
        c_prev = cs_ref[...]
        m_row, dmat, inter, m_new, decay, w = _chunk_gates(colf_ref, rowf_ref, m_prev)

        hh = hh_ref[...]
        dhm_v = dhm_ref[...].astype(F32)
        so = _sigmoid(o)
        sg = _sigmoid(z)
        sz = z * sg
        r = lax.rsqrt(jnp.mean(hh * hh, axis=-1, keepdims=True) + RMS_EPS)
        hn = (hh * r) * w_ref[...]
        dhn = dhm_v * so * sz
        doz_ref[:, 0:dv] = (dhm_v * hn * sz * (so * (1.0 - so))).astype(BF16)
        doz_ref[:, dv:2 * dv] = (dhm_v * hn * so * (sg * (1.0 + z * (1.0 - sg)))).astype(BF16)
        dw_ref[...] += jnp.broadcast_to(jnp.sum(dhn * (hh * r), axis=0, keepdims=True), dw_ref.shape)
        gwn = dhn * w_ref[...]
        dhh = r * gwn - hh * ((r * r * r) * jnp.mean(gwn * hh, axis=-1, keepdims=True))

        den = stat_ref[:, 0:1]
        floor = jnp.exp(-m_row)
        denom = jnp.maximum(jnp.abs(den), floor)
        inv = 1.0 / denom
        dnum = dhh * inv
        hdot = jnp.sum(dhh * hh, axis=-1, keepdims=True)
        dden = jnp.where(jnp.abs(den) > floor, -(hdot * inv) * jnp.sign(den), 0.0)
        s = _dot_nt(q, k) * dmat
        dqk = (_dot_nt(dnum, v) + dden) * dmat
        dc_new = dc_ref[...]
        dn_new = dn_ref[...]
        idd = inter * dden
        dq = _dot(dqk, k) + inter * _dot_nt(dnum, c_prev) + idd * n_prev
        dkv = _dot_tn(dqk, q) + w * (_dot_nt(v, dc_new) + dn_new)
        dvv = _dot_tn(s, dnum) + w * _dot(k, dc_new)
        dp_ref[:, 0:dk] = dq.astype(BF16)
        dp_ref[:, dk:2 * dk] = dkv.astype(BF16)
        dp_ref[:, 2 * dk:2 * dk + dv] = dvv.astype(BF16)
        qdq = jnp.sum(qf * dq, axis=-1, keepdims=True)
        kdk = jnp.sum(kf * dkv, axis=-1, keepdims=True)
        dc_prev = decay * dc_new + _dot_tn(inter * qf, dnum)
        dn_prev = decay * dn_new + jnp.sum(idd * qf, axis=0, keepdims=True)
        dc_ref[...] = dc_prev
        dn_ref[...] = dn_prev
        cross = (jnp.sum(jnp.sum(c_prev.astype(F32) * dc_prev, axis=-1, keepdims=True), axis=0, keepdims=True)
                 + jnp.sum(n_prev * dn_prev, axis=-1, keepdims=True))
        lane = lax.broadcasted_iota(jnp.int32, (CHUNK, LANES), 1)
        gs_ref[...] = jnp.where(lane == 0, qdq - kdk, jnp.where(lane == 1, kdk, jnp.where(lane == 2, cross, 0.0)))

    ns = nc // cps
    rc = lambda c: (ns - 1 - c, 0)
    rc3 = lambda c: (0, ns - 1 - c, 0)
    rc4 = lambda c: (0, ns - 1 - c, 0, 0)
    return _call(
        body, name=name, grid=(ns,),
        in_specs=[pl.BlockSpec((rows_per_step, nh * (hw + 2 * dv)), rc),
                  pl.BlockSpec((nh, rows_per_step, LANES), rc3),
                  pl.BlockSpec((nh, cps, 8, CHUNK), rc4),
                  pl.BlockSpec((1, nh * dv), lambda c: (0, 0)),
                  pl.BlockSpec((rows_per_step, nh * dv), rc),
                  pl.BlockSpec((nh, rows_per_step, LANES), rc3),
                  pl.BlockSpec((nh, cps, dk, dv), rc4),
                  pl.BlockSpec((nh, cps, 8, dk), rc4),
                  pl.BlockSpec((rows_per_step, nh * dv), rc)],
        out_specs=[pl.BlockSpec((rows_per_step, nh * (hw + 2 * dv)), rc),
                   pl.BlockSpec((nh, rows_per_step, LANES), rc3),
                   pl.BlockSpec((nh, 8, dv), lambda c: (0, 0, 0))],
        out_shape=[jax.ShapeDtypeStruct((t, n_cols), BF16), jax.ShapeDtypeStruct((nh, t, LANES), F32),
                   jax.ShapeDtypeStruct((nh, 8, dv), F32)],
        scratch_shapes=[pltpu.VMEM((nh, dk, dv), F32), pltpu.VMEM((nh, 1, dk), F32)],
        args=(p1, colf, rowf, head_w, hh, stat, csave, nsave, dhm), exchange=exchange)


def _position():
    return lax.axis_index("x"), lax.axis_index("y"), lax.axis_index("c")


class _Exchange:
    def __init__(self, ins, out_shapes, start, mid, finish):
        n = len(ins)
        self.ins, self.out_shapes = list(ins), list(out_shapes)
        self.start, self.mid, self.finish = start, mid, finish
        self.scratch = [pltpu.SemaphoreType.DMA((n, 7)), pltpu.SemaphoreType.DMA((n, 7)),
                        pltpu.SemaphoreType.DMA((n,))]


def _gather_exchange(shards):
    n = len(shards)

    def plan(ins, outs, sems):
        send_sems, recv_sems, local_sems = sems
        x, y, c = _position()
        me, sibling = (x, y, c), (x, y, 1 - c)
        chips = [(1 - x, y), (x, 1 - y), (1 - x, 1 - y)]

        def copy(a, k, block, to, src=None):
            px, py, pc = block
            dst = outs[a].at[4 * px + 2 * py + pc]
            return pltpu.make_async_remote_copy(
                src_ref=dst if src is None else src, dst_ref=dst,
                send_sem=send_sems.at[a, k], recv_sem=recv_sems.at[a, k],
                device_id=to, device_id_type=MESH)

        def mine():
            return [pltpu.make_async_copy(ins[a], outs[a].at[4 * x + 2 * y + c], local_sems.at[a])
                    for a in range(n)]

        def first():
            out = []
            for a in range(n):
                out.append(copy(a, 0, me, sibling, src=ins[a]))
                out += [copy(a, 1 + j, me, (*chip, c), src=ins[a]) for j, chip in enumerate(chips)]
            return out

        def ici_in():
            return [copy(a, 1 + j, (*chip, c), me) for j, chip in enumerate(chips) for a in range(n)]

        def passed():
            return [copy(a, 4 + j, (*chip, c), sibling) for j, chip in enumerate(chips) for a in range(n)]

        def d2d_in():
            return ([copy(a, 0, sibling, me) for a in range(n)]
                    + [copy(a, 4 + j, (*chip, 1 - c), me) for j, chip in enumerate(chips) for a in range(n)])

        return mine, first, ici_in, passed, d2d_in

    def start(ins, outs, sems):
        mine, first, _, _, _ = plan(ins, outs, sems)
        for cp in mine() + first():
            cp.start()

    def mid(ins, outs, sems):
        _, _, ici_in, passed, _ = plan(ins, outs, sems)
        for arrived, onward in zip(ici_in(), passed()):
            arrived.wait_recv()
            onward.start()

    def finish(ins, outs, sems):
        mine, first, _, passed, d2d_in = plan(ins, outs, sems)
        for cp in d2d_in():
            cp.wait_recv()
        for cp in first() + passed():
            cp.wait_send()
        for cp in mine():
            cp.wait()

    shapes = [jax.ShapeDtypeStruct((N_DEV,) + s.shape, s.dtype) for s in shards]
    return _Exchange(shards, shapes, start, mid, finish)


def _scatter_exchange(fulls):
    n = len(fulls)

    def plan(ins, outs, sems):
        send_sems, recv_sems, local_sems = sems
        x, y, c = _position()
        my_slot = 4 * x + 2 * y + c

        def mine():
            return [pltpu.make_async_copy(ins[a].at[my_slot], outs[a].at[my_slot], local_sems.at[a])
                    for a in range(n)]

        def remote(arriving):
            out = []
            for kk in (1, 2, 4, 6, 3, 5, 7):
                kx, ky, kc = (kk >> 2) & 1, (kk >> 1) & 1, kk & 1
                px = 1 - x if kx else x
                py = 1 - y if ky else y
                pc = 1 - c if kc else c
                peer_slot = 4 * px + 2 * py + pc
                for a in range(n):
                    out.append(pltpu.make_async_remote_copy(
                        src_ref=ins[a].at[peer_slot], dst_ref=outs[a].at[peer_slot if arriving else my_slot],
                        send_sem=send_sems.at[a, kk - 1], recv_sem=recv_sems.at[a, kk - 1],
                        device_id=(px, py, pc), device_id_type=MESH))
            return out

        return mine, remote

    def start(ins, outs, sems):
        mine, remote = plan(ins, outs, sems)
        for cp in mine() + remote(False):
            cp.start()

    def finish(ins, outs, sems):
        mine, remote = plan(ins, outs, sems)
        for cp in remote(True):
            cp.wait_recv()
        for cp in remote(False):
            cp.wait_send()
        for cp in mine():
            cp.wait()

    shapes = [jax.ShapeDtypeStruct(f.shape, f.dtype) for f in fulls]
    return _Exchange(fulls, shapes, start, None, finish)


def _run_exchange(ex, name):
    n_in, n_out = len(ex.ins), len(ex.out_shapes)

    def body(*refs):
        ins, outs, sems = refs[:n_in], refs[n_in:n_in + n_out], refs[n_in + n_out:]
        ex.start(ins, outs, sems)
        if ex.mid is not None:
            ex.mid(ins, outs, sems)
        ex.finish(ins, outs, sems)

    any_spec = pl.BlockSpec(memory_space=pl.ANY)
    return pl.pallas_call(
        body, name=name,
        in_specs=[any_spec] * n_in, out_specs=[any_spec] * n_out,
        out_shape=ex.out_shapes, scratch_shapes=ex.scratch,
    )(*ex.ins)


def _natural_to_head_major(n0, dk, dv):
    nh = N_HEADS
    qk, e, hw = nh * dk, nh * dv, 2 * dk + dv
    if n0 < qk:
        h, off = divmod(n0, dk)
        return h * hw + off, True
    n1 = n0 - qk
    if n1 < qk:
        h, off = divmod(n1, dk)
        return h * hw + dk + off, False
    part, n3 = divmod(n1 - qk, e)
    h, off = divmod(n3, dv)
    if part == 0:
        return h * hw + 2 * dk + off, False
    return nh * hw + h * 2 * dv + (part - 1) * dv + off, False


def _mlstm_weight_layout(w_blocks, dk, dv, name):
    _, d, blk = w_blocks.shape
    nh = N_HEADS
    hw = 2 * dk + 3 * dv
    n_tiles = nh * hw // LANES
    tiles_per_block = d // LANES
    tr = _pick(d, 128, 16)
    scale = dk ** -0.5

    def body(w_ref, p_ref, g_ref):
        lane = lax.broadcasted_iota(jnp.int32, (tr, LANES), 1)

        def tile(s, r):
            return w_ref[s, :, r * LANES:(r + 1) * LANES].astype(F32)

        def last_col(s):
            return w_ref[s, :, d:d + 1].astype(F32)

        for tn in range(n_tiles):
            s, r = divmod(tn, tiles_per_block)
            if s == 0:
                val = tile(0, r)
            elif r == 0:
                from_prev = pltpu.roll(tile(s - 1, tiles_per_block - 1), s - 1, 1)
                from_here = pltpu.roll(tile(s, 0), s, 1)
                val = jnp.where(lane < s - 1, from_prev, jnp.where(lane == s - 1, last_col(s - 1), from_here))
            else:
                slab = jnp.concatenate([tile(s, r - 1), tile(s, r)], axis=1)
                val = pltpu.roll(slab, s, 1)[:, LANES:]
            at, is_q = _natural_to_head_major(tn * LANES, dk, dv)
            if is_q:
                val = val * scale
            p_ref[:, at:at + LANES] = val.astype(BF16)
        n_gate = 2 * nh
        s = N_DEV - 1
        gates = pltpu.roll(tile(s, tiles_per_block - 1), n_gate - 1, 1)
        gates = jnp.where(lane < n_gate - 1, gates, jnp.where(lane == n_gate - 1, last_col(s), 0.0)).astype(BF16)
        g_ref[...] = gates
        p_ref[:, nh * hw:nh * hw + LANES] = gates

    return pl.pallas_call(
        body, name=name, grid=(d // tr,),
        in_specs=[pl.BlockSpec((N_DEV, tr, blk), lambda i: (0, i, 0))],
        out_specs=[pl.BlockSpec((tr, nh * hw + LANES), lambda i: (i, 0)), pl.BlockSpec((tr, LANES), lambda i: (i, 0))],
        out_shape=[jax.ShapeDtypeStruct((d, nh * hw + LANES), BF16), jax.ShapeDtypeStruct((d, LANES), BF16)],
        compiler_params=_params(),
    )(w_blocks)


def _mlstm_grad_layout(g_p1, dk, dv, name):
    d = g_p1.shape[0]
    nh = N_HEADS
    hw = 2 * dk + 3 * dv
    n_tiles = nh * hw // LANES
    tiles_per_block = d // LANES
    tr = _pick(d, 128, 16)
    scale = dk ** -0.5

    def body(p_ref, o_ref):
        def natural(tn):
            if tn == n_tiles:
                return p_ref[:, nh * hw:nh * hw + LANES].astype(F32)
            at, is_q = _natural_to_head_major(tn * LANES, dk, dv)
            val = p_ref[:, at:at + LANES].astype(F32)
            return val * scale if is_q else val

        for s in range(N_DEV):
            for r in range(tiles_per_block):
                tn = s * tiles_per_block + r
                if s == 0:
                    val = natural(tn)
                else:
                    slab = jnp.concatenate([natural(tn), natural(tn + 1)], axis=1)
                    val = pltpu.roll(slab, 2 * LANES - s, 1)[:, :LANES]
                o_ref[s, :, r * LANES:(r + 1) * LANES] = val.astype(BF16)
            o_ref[s, :, d:d + 1] = natural((s + 1) * tiles_per_block)[:, s:s + 1].astype(BF16)

    return pl.pallas_call(
        body, name=name, grid=(d // tr,),
        in_specs=[pl.BlockSpec((tr, nh * hw + LANES), lambda i: (i, 0))],
        out_specs=pl.BlockSpec((N_DEV, tr, d + 1), lambda i: (0, i, 0)),
        out_shape=jax.ShapeDtypeStruct((N_DEV, d, d + 1), BF16),
        compiler_params=_params(),
    )(g_p1)


def _adamw_math(w, g, m, v):
    m = ADAM_B1 * m + (1.0 - ADAM_B1) * g
    v = ADAM_B2 * v + (1.0 - ADAM_B2) * (g * g)
    m_hat = m / (1.0 - ADAM_B1 ** ADAM_STEP)
    v_hat = v / (1.0 - ADAM_B2 ** ADAM_STEP)
    delta = -ADAM_LR * (m_hat / (jnp.sqrt(v_hat) + ADAM_EPS) + ADAM_WD * w)
    return delta, m, v


def _adamw_sharded(parts, w, m, v, name):
    _, r, c = parts.shape
    tr = _pick(r, 128, 8)

    def body(p_ref, w_ref, m_ref, v_ref, g_ref, d_ref, nm_ref, nv_ref):
        g = p_ref[0].astype(F32)
        for s in range(1, N_DEV):
            g = g + p_ref[s].astype(F32)
        delta, m_new, v_new = _adamw_math(w_ref[...], g, m_ref[...], v_ref[...])
        g_ref[...] = g
        d_ref[...] = delta
        nm_ref[...] = m_new
        nv_ref[...] = v_new

    blk = pl.BlockSpec((tr, c), lambda i: (i, 0))
    return pl.pallas_call(
        body, name=name, grid=(r // tr,),
        in_specs=[pl.BlockSpec((N_DEV, tr, c), lambda i: (0, i, 0)), blk, blk, blk],
        out_specs=[blk] * 4,
        out_shape=[jax.ShapeDtypeStruct((r, c), F32)] * 4,
        compiler_params=_params(),
    )(parts, w, m, v)


def _sum_devices(parts, name):
    _, r, c = parts.shape

    def body(p_ref, o_ref):
        g = p_ref[0]
        for s in range(1, N_DEV):
            g = g + p_ref[s]
        o_ref[...] = g

    return pl.pallas_call(
        body, name=name, out_shape=jax.ShapeDtypeStruct((r, c), F32), compiler_params=_params(),
    )(parts)


def _adamw_small(gs, ws, ms, vs, name):
    n = len(gs)

    def body(*refs):
        g_refs, w_refs, m_refs, v_refs = refs[:n], refs[n:2 * n], refs[2 * n:3 * n], refs[3 * n:4 * n]
        d_refs, nm_refs, nv_refs = refs[4 * n:5 * n], refs[5 * n:6 * n], refs[6 * n:7 * n]
        for a in range(n):
            delta, m_new, v_new = _adamw_math(w_refs[a][...], g_refs[a][...], m_refs[a][...], v_refs[a][...])
            d_refs[a][...] = delta
            nm_refs[a][...] = m_new
            nv_refs[a][...] = v_new

    shapes = [jax.ShapeDtypeStruct(w.shape, F32) for w in ws]
    outs = pl.pallas_call(
        body, name=name, out_shape=shapes * 3, compiler_params=_params(),
    )(*gs, *ws, *ms, *vs)
    return outs[:n], outs[n:2 * n], outs[2 * n:]


def _pad_rows(a, rows):
    return jnp.pad(a, ((0, rows - a.shape[0]), (0, 0)))


def kernel(x, meta_tokens, norm_w, conv_in_w, conv_w, conv_out_w, mlstm_in_w, mlstm_gate_b, mlstm_head_norm_w, mlstm_out_w, final_norm_w, loss_target, m_meta_tokens, m_norm_w, m_conv_in_w, m_conv_w, m_conv_out_w, m_mlstm_in_w, m_mlstm_gate_b, m_mlstm_head_norm_w, m_mlstm_out_w, m_final_norm_w, v_meta_tokens, v_norm_w, v_conv_in_w, v_conv_w, v_conv_out_w, v_mlstm_in_w, v_mlstm_gate_b, v_mlstm_head_norm_w, v_mlstm_out_w, v_final_norm_w):
    seq, d = x.shape[1], x.shape[2]
    t = seq + LEAD
    e = 2 * d
    ec = e // 2
    nh = N_HEADS
    dv = e // nh
    dk = dv // 2
    qk = nh * dk
    hw = 2 * dk + 3 * dv
    n_in = 2 * qk + 3 * e + 2 * nh
    n_in_s = n_in // N_DEV
    me = 4 * lax.axis_index("x") + 2 * lax.axis_index("y") + lax.axis_index("c")
    tm = _pick(t, 832, 16)
    tm_in = _pick(t, 1664, 16)
    tkw = _pick(t, 2080, 16)

    small = jnp.concatenate([
        meta_tokens,
        _pad_rows(conv_w[0].reshape(3 * (e // N_DEV) // LANES, LANES), 8),
        _pad_rows(mlstm_head_norm_w[0].reshape((e // N_DEV) // LANES, LANES), 8),
    ], axis=0) if d // N_DEV == LANES else None
    assert small is not None, "the packed small-weight block assumes d_model / 8 == 128"
    (small_g,) = _run_exchange(_gather_exchange([small]), "gather_small_weights")
    meta_full = jnp.transpose(small_g[:, 0:N_META, :], (1, 0, 2)).reshape(N_META, d)
    cw_rows = 3 * (e // N_DEV) // LANES
    conv_w_full = jnp.transpose(
        small_g[:, N_META:N_META + cw_rows, :].reshape(N_DEV, 3, e // N_DEV), (1, 0, 2)).reshape(3, e)
    hn_rows = (e // N_DEV) // LANES
    head_w_full = small_g[:, N_META + 8:N_META + 8 + hn_rows, :].reshape(1, e)

    h0 = jnp.concatenate([jnp.zeros((PAD_ROWS, d), F32), meta_full, x[0]], axis=0)
    tgt = loss_target[0]

    ci_map = lambda blk: 2 * (blk % 4) + blk // 4
    u0, (w_ci,) = _rms_fwd(h0, norm_w[0:1], "rms0", exchange=_gather_exchange([conv_in_w[0].astype(BF16)]))
    p0, (w_co, w_mi) = _matmul(
        u0, w_ci, form="nn", m=t, n=8 * ec, kdim=d, tm=tm_in, tn=ec, tk=d, out_dtype=BF16, name="conv_in",
        b_spec=pl.BlockSpec((None, d, ec), lambda i, j, k: (ci_map(j), 0, 0)),
        exchange=_gather_exchange([conv_out_w[0].astype(BF16), mlstm_in_w[0].astype(BF16)]))
    w_co = w_co.reshape(e, d)
    (g0, y0), (w_mo,) = _conv_gate_fwd(p0, conv_w_full, "conv_gate",
                                       exchange=_gather_exchange([mlstm_out_w[0].astype(BF16)]))
    w_mo = w_mo.reshape(e, d)
    h1 = _matmul(g0, w_co, form="nn", m=t, n=d, kdim=e, tm=tm, tn=d, tk=e, out_dtype=F32, name="conv_out",
                 residual=h0)

    assert n_in_s == d + 1 and w_mi.shape == (N_DEV, d, d + 1)
    w_p1, w_gate = _mlstm_weight_layout(w_mi, dk, dv, "mlstm_w_layout")
    w_gate_t = jnp.transpose(w_gate[:, 0:16])
    bias_row = jnp.pad(mlstm_gate_b, ((0, 0), (0, LANES - 2 * nh)))
    bias_col = jnp.pad(mlstm_gate_b.T, ((0, 16 - 2 * nh), (0, 0)))

    u1 = _rms_fwd(h1, norm_w[1:2], "rms1")
    p1 = _matmul(u1, w_p1, form="nn", m=t, n=nh * hw, kdim=d, tm=tm_in, tn=_pick(nh * hw, 1024, LANES), tk=d,
                 out_dtype=BF16, name="mlstm_in")
    graw = _matmul(u1, w_gate, form="nn", m=t, n=LANES, kdim=d, tm=tm, tn=LANES, tk=d, out_dtype=F32,
                   name="gates_col")
    graw_t = _matmul(w_gate_t, u1, form="nt", m=16, n=t, kdim=d, tm=16, tn=_pick(t, 1664, LANES), tk=d,
                     out_dtype=F32, name="gates_row")
    colf, rowf = _gates_fwd(graw, graw_t, bias_row, bias_col, "gates_fwd")
    hh, stat, csave, nsave = _mlstm_fwd(p1, colf, rowf, dk, dv, "mlstm_fwd")
    hm = _head_gate_fwd(hh, p1, head_w_full, dv, "head_gate")
    h2 =_matmul(hm, w_mo, form="nn", m=t, n=d, kdim=e, tm=tm, tn=d, tk=e, out_dtype=F32, name="mlstm_out",
                 residual=h1)

    dh2, dwf, loss_part = _final_loss(h2, final_norm_w.reshape(1, d), tgt, "final_loss")

    dhm = _matmul(dh2, w_mo, form="nt", m=t, n=e, kdim=d, tm=tm, tn=_pick(e, 2048, LANES), tk=d, out_dtype=BF16,
                  name="mlstm_out_dx")
    g_mo = _matmul(hm, dh2, form="tn", m=e, n=d, kdim=t, tm=_pick(e, 1024, LANES), tn=d, tk=tkw, out_dtype=BF16,
                   name="mlstm_out_dw")
    n_p1 = nh * hw + LANES
    (dp1, gstat, dhead), (r_mo,) = _mlstm_bwd(
        p1, colf, rowf, head_w_full, hh, stat, csave, nsave, dhm, n_p1, dk, dv, "mlstm_bwd",
        exchange=_scatter_exchange([g_mo.reshape(N_DEV, e // N_DEV, d)]))
    dp1, dbias = _gates_bwd(gstat, colf, dp1, "gates_bwd")
    du1 = _matmul(dp1, w_p1, form="nt", m=t, n=d, kdim=n_p1, tm=tm, tn=d, tk=_pick(n_p1, 2048, LANES),
                  out_dtype=F32, name="mlstm_in_dx")
    g_p1 = _matmul(u1, dp1, form="tn", m=d, n=n_p1, kdim=t, tm=d, tn=_pick(n_p1, 2048, LANES), tk=tkw,
                   out_dtype=BF16, name="mlstm_in_dw")
    dh1, dnw1 = _rms_bwd(h1, norm_w[1:2], du1, dh2, "rms1_bwd")

    g_mi = _mlstm_grad_layout(g_p1, dk, dv, "mlstm_g_layout")

    dg0 =_matmul(dh1, w_co, form="nt", m=t, n=e, kdim=d, tm=tm, tn=_pick(e, 2048, LANES), tk=d, out_dtype=BF16,
                  name="conv_out_dx")
    g_co = _matmul(g0, dh1, form="tn", m=e, n=d, kdim=t, tm=_pick(e, 1024, LANES), tn=d, tk=tkw, out_dtype=BF16,
                   name="conv_out_dw")
    (dp0, dconv), (r_mi,) = _conv_gate_bwd(p0, y0, dg0, conv_w_full, "conv_gate_bwd",
                                           exchange=_scatter_exchange([g_mi]))
    g_ci, (r_co,) = _matmul(
        u0, dp0, form="tn", m=d, n=8 * ec, kdim=t, tm=d, tn=ec, tk=tkw, out_dtype=BF16, name="conv_in_dw",
        out_shape=(N_DEV, d, ec), out_spec=pl.BlockSpec((None, d, ec), lambda i, j, k: (ci_map(j), 0, 0)),
        exchange=_scatter_exchange([g_co.reshape(N_DEV, e // N_DEV, d)]))
    du0, (r_ci,) = _matmul(
        dp0, w_ci, form="nt", m=t, n=d, kdim=8 * ec, tm=tm, tn=d, tk=ec, out_dtype=F32, name="conv_in_dx",
        b_spec=pl.BlockSpec((None, d, ec), lambda i, j, k: (ci_map(k), 0, 0)),
        exchange=_scatter_exchange([g_ci]))
    grad_x, dmeta, dnw0 = _rms_bwd_first(h0, norm_w[0:1], du0, dh1, "rms0_bwd")
    grad_x = grad_x[None]

    row8 = lax.broadcasted_iota(jnp.int32, (8, 1), 0)
    loss_wide = jnp.pad(loss_part, ((0, 0), (0, d - LANES)))
    payload = jnp.concatenate([
        jnp.where(row8 == 0, dnw0, jnp.where(row8 == 1, dnw1, 0.0)),
        jnp.where(row8 == 0, dwf, jnp.where(row8 == 1, loss_wide, 0.0)),
        jnp.where(row8 == 0, jnp.pad(dbias, ((0, 0), (0, d - LANES))), 0.0),
        dmeta,
        _pad_rows(dconv[0:3].reshape(3 * e // d, d), 8),
        _pad_rows(dhead[:, 0, :].reshape(e // d, d), 8),
    ], axis=0)
    (payload_g,) = _run_exchange(_gather_exchange([payload]), "gather_small_grads")
    tot = _sum_devices(payload_g, "sum_small_grads")

    loss = tot[9, 0]
    g_norm = tot[0:2]
    g_final = tot[8]
    g_gate_b = tot[16:17, 0:2 * nh]
    g_meta = lax.dynamic_slice(tot[24:24 + N_META], (0, me * (d // N_DEV)), (N_META, d // N_DEV))
    g_conv_w = lax.dynamic_slice(tot[40:40 + 3 * e // d].reshape(3, e), (0, me * (e // N_DEV)), (3, e // N_DEV))
    g_head = lax.dynamic_slice(tot[48:48 + e // d].reshape(1, e), (0, me * (e // N_DEV)), (1, e // N_DEV))

    g1, d1, nm1, nv1 = _adamw_sharded(r_ci, conv_in_w[0], m_conv_in_w[0], v_conv_in_w[0], "adamw_conv_in")
    g2, d2, nm2, nv2 = _adamw_sharded(r_co, conv_out_w[0], m_conv_out_w[0], v_conv_out_w[0], "adamw_conv_out")
    g3, d3, nm3, nv3 = _adamw_sharded(r_mi, mlstm_in_w[0], m_mlstm_in_w[0], v_mlstm_in_w[0], "adamw_mlstm_in")
    g4, d4, nm4, nv4 = _adamw_sharded(r_mo, mlstm_out_w[0], m_mlstm_out_w[0], v_mlstm_out_w[0], "adamw_mlstm_out")

    small_g = [g_meta, g_norm, g_conv_w, g_gate_b, g_head, g_final.reshape(1, d)]
    small_w = [meta_tokens, norm_w, conv_w[0], mlstm_gate_b, mlstm_head_norm_w, final_norm_w.reshape(1, d)]
    small_m = [m_meta_tokens, m_norm_w, m_conv_w[0], m_mlstm_gate_b, m_mlstm_head_norm_w, m_final_norm_w.reshape(1, d)]
    small_v = [v_meta_tokens, v_norm_w, v_conv_w[0], v_mlstm_gate_b, v_mlstm_head_norm_w, v_final_norm_w.reshape(1, d)]
    sd, snm, snv = _adamw_small(small_g, small_w, small_m, small_v, "adamw_small")

    def order(meta, norm, cin, cw, cout, min_, gb, hn, mout, fin):
        return (meta, norm, cin[None], cw[None], cout[None], min_[None], gb, hn, mout[None], fin.reshape(d))

    grads = order(g_meta, g_norm, g1, g_conv_w, g2, g3, g_gate_b, g_head, g4, g_final)
    deltas = order(sd[0], sd[1], d1, sd[2], d2, d3, sd[3], sd[4], d4, sd[5])
    new_m = order(snm[0], snm[1], nm1, snm[2], nm2, nm3, snm[3], snm[4], nm4, snm[5])
    new_v = order(snv[0], snv[1], nv1, snv[2], nv2, nv3, snv[3], snv[4], nv4, snv[5])
    return (loss, grad_x, *grads, *deltas, *new_m, *new_v)
```

```python
import functools

import jax
import jax.numpy as jnp
from jax import lax
from jax.experimental import pallas as pl
from jax.experimental.pallas import tpu as pltpu

F32 = jnp.float32
BF16 = jnp.bfloat16
MESH = pl.DeviceIdType.MESH

N_DEV = 8
N_META = 16
N_HEADS = 4
CHUNK = 64
CHUNKS_PER_STEP = 2
LEAD = 128
PAD_ROWS = LEAD - N_META
RMS_EPS = 1e-6
NEG = -1e30
LANES = 128
VMEM_LIMIT = 48 * 1024 * 1024

ADAM_LR = 0.001
ADAM_B1 = 0.9
ADAM_B2 = 0.999
ADAM_EPS = 1e-08
ADAM_WD = 0.01
ADAM_STEP = 10

HIGHEST = lax.Precision.HIGHEST


def _pick(n, target, mult):
    best = None
    for d in range(mult, min(n, target) + 1, mult):
        if n % d == 0:
            best = d
    return n if best is None else best


def _params(**kw):
    return pltpu.CompilerParams(vmem_limit_bytes=VMEM_LIMIT, **kw)


def _sigmoid(x):
    return 0.5 * jnp.tanh(0.5 * x) + 0.5


STRIP_ROWS = 16
STRIP_COLS = 256


def _strips(n_rows, n_cols):
    for c0 in range(0, n_cols, STRIP_COLS):
        for r0 in range(0, n_rows, STRIP_ROWS):
            yield slice(r0, r0 + STRIP_ROWS), slice(c0, min(c0 + STRIP_COLS, n_cols))


def _shift(sl, by):
    return slice(sl.start + by, sl.stop + by)


def _call(body, *, name, grid, in_specs, out_specs, out_shape, args, scratch_shapes=(), aliases=None,
          exchange=None):
    aliases = {} if aliases is None else aliases
    if exchange is None:
        return pl.pallas_call(
            body, name=name, grid=grid, in_specs=list(in_specs), out_specs=out_specs, out_shape=out_shape,
            scratch_shapes=list(scratch_shapes), input_output_aliases=aliases,
            compiler_params=_params())(*args)
    ex = exchange
    single = not isinstance(out_shape, (list, tuple))
    shapes = [out_shape] if single else list(out_shape)
    specs = [out_specs] if single else list(out_specs)
    n_in, n_out, n_scr = len(in_specs), len(shapes), len(scratch_shapes)
    n_ex_in, n_ex_out = len(ex.ins), len(ex.out_shapes)
    steps = 1
    for size in grid:
        steps *= size

    def wrapped(*refs):
        own_in, ex_in = refs[:n_in], refs[n_in:n_in + n_ex_in]
        at = n_in + n_ex_in
        own_out, ex_out = refs[at:at + n_out], refs[at + n_out:at + n_out + n_ex_out]
        at += n_out + n_ex_out
        own_scr, ex_scr = refs[at:at + n_scr], refs[at + n_scr:]
        step = 0
        for axis, size in enumerate(grid):
            step = step * size + pl.program_id(axis)

        @pl.when(step == 0)
        def _():
            ex.start(ex_in, ex_out, ex_scr)

        if ex.mid is not None:
            @pl.when(step == (3 * steps) // 4)
            def _():
                ex.mid(ex_in, ex_out, ex_scr)

        body(*own_in, *own_out, *own_scr)

        @pl.when(step == steps - 1)
        def _():
            ex.finish(ex_in, ex_out, ex_scr)

    any_spec = pl.BlockSpec(memory_space=pl.ANY)
    res = pl.pallas_call(
        wrapped, name=name, grid=grid,
        in_specs=list(in_specs) + [any_spec] * n_ex_in, out_specs=specs + [any_spec] * n_ex_out,
        out_shape=shapes + ex.out_shapes, scratch_shapes=list(scratch_shapes) + ex.scratch,
        input_output_aliases=aliases, compiler_params=_params())(*args, *ex.ins)
    return (res[0] if single else res[:n_out]), res[n_out:]


def _matmul(a, b, *, form, m, n, kdim, tm, tn, tk, out_dtype, name,
            a_spec=None, b_spec=None, out_spec=None, out_shape=None, residual=None, exchange=None):
    ni, nj, nk = m // tm, n // tn, kdim // tk
    assert ni * tm == m and nj * tn == n and nk * tk == kdim, (name, m, n, kdim, tm, tn, tk)
    if form == "nn":
        dn = (((1,), (0,)), ((), ()))
        a_def = pl.BlockSpec((tm, tk), lambda i, j, k: (i, k))
        b_def = pl.BlockSpec((tk, tn), lambda i, j, k: (k, j))
    elif form == "nt":
        dn = (((1,), (1,)), ((), ()))
        a_def = pl.BlockSpec((tm, tk), lambda i, j, k: (i, k))
        b_def = pl.BlockSpec((tn, tk), lambda i, j, k: (j, k))
    else:
        dn = (((0,), (0,)), ((), ()))
        a_def = pl.BlockSpec((tk, tm), lambda i, j, k: (k, i))
        b_def = pl.BlockSpec((tk, tn), lambda i, j, k: (k, j))
    a_spec = a_def if a_spec is None else a_spec
    b_spec = b_def if b_spec is None else b_spec
    o_spec = pl.BlockSpec((tm, tn), lambda i, j, k: (i, j)) if out_spec is None else out_spec
    has_res = residual is not None

    def body(*refs):
        a_ref, b_ref = refs[:2]
        r_ref = refs[2] if has_res else None
        o_ref = refs[2 + has_res]

        def product():
            return lax.dot_general(a_ref[...].astype(BF16), b_ref[...].astype(BF16), dn,
                                   preferred_element_type=F32)

        def finish(acc):
            if has_res:
                acc = acc + r_ref[...].astype(F32)
            o_ref[...] = acc.astype(o_ref.dtype)

        if nk == 1:
            finish(product())
        else:
            acc_ref = refs[3 + has_res]
            k = pl.program_id(2)

            @pl.when(k == 0)
            def _():
                acc_ref[...] = jnp.zeros_like(acc_ref)

            acc_ref[...] += product()

            @pl.when(k == nk - 1)
            def _():
                finish(acc_ref[...])

    in_specs = [a_spec, b_spec]
    args = [a, b]
    if has_res:
        in_specs.append(pl.BlockSpec((tm, tn), lambda i, j, k: (i, j)))
        args.append(residual)
    return _call(
        body, name=name, grid=(ni, nj, nk), in_specs=in_specs, out_specs=o_spec,
        out_shape=jax.ShapeDtypeStruct((m, n) if out_shape is None else out_shape, out_dtype),
        scratch_shapes=[] if nk == 1 else [pltpu.VMEM((tm, tn), F32)], args=args, exchange=exchange)


def _rms_fwd(h, w, name, exchange=None):
    t, d = h.shape
    tm = _pick(t, 416, 16)

    def body(h_ref, w_ref, u_ref):
        x = h_ref[...]
        r = lax.rsqrt(jnp.mean(x * x, axis=-1, keepdims=True) + RMS_EPS)
        u_ref[...] = ((x * r) * w_ref[...]).astype(BF16)

    return _call(
        body, name=name, grid=(t // tm,),
        in_specs=[pl.BlockSpec((tm, d), lambda i: (i, 0)), pl.BlockSpec((1, d), lambda i: (0, 0))],
        out_specs=pl.BlockSpec((tm, d), lambda i: (i, 0)),
        out_shape=jax.ShapeDtypeStruct((t, d), BF16), args=(h, w), exchange=exchange)


def _rms_bwd(h, w, du, dres, name):
    t, d = h.shape
    tm = _pick(t, 416, 16)

    def body(h_ref, w_ref, du_ref, dres_ref, dh_ref, dw_ref):
        i = pl.program_id(0)
        x = h_ref[...]
        g = du_ref[...].astype(F32)
        r = lax.rsqrt(jnp.mean(x * x, axis=-1, keepdims=True) + RMS_EPS)
        gw = g * w_ref[...]
        dot = jnp.mean(gw * x, axis=-1, keepdims=True)
        dh_ref[...] = dres_ref[...] + (r * gw - x * ((r * r * r) * dot))
        part = jnp.sum(g * (x * r), axis=0, keepdims=True)

        @pl.when(i == 0)
        def _():
            dw_ref[...] = jnp.zeros_like(dw_ref)

        dw_ref[...] += jnp.broadcast_to(part, dw_ref.shape)

    row = pl.BlockSpec((tm, d), lambda i: (i, 0))
    return pl.pallas_call(
        body, name=name, grid=(t // tm,),
        in_specs=[row, pl.BlockSpec((1, d), lambda i: (0, 0)), row, row],
        out_specs=[row, pl.BlockSpec((8, d), lambda i: (0, 0))],
        out_shape=[jax.ShapeDtypeStruct((t, d), F32), jax.ShapeDtypeStruct((8, d), F32)],
        compiler_params=_params(),
    )(h, w, du, dres)


def _rms_bwd_first(h, w, du, dres, name):
    t, d = h.shape
    tm = LEAD

    def body(h_ref, w_ref, du_ref, dres_ref, gx_ref, dmeta_ref, dw_ref):
        i = pl.program_id(0)
        x = h_ref[...]
        g = du_ref[...].astype(F32)
        r = lax.rsqrt(jnp.mean(x * x, axis=-1, keepdims=True) + RMS_EPS)
        gw = g * w_ref[...]
        dot = jnp.mean(gw * x, axis=-1, keepdims=True)
        dh = dres_ref[...] + (r * gw - x * ((r * r * r) * dot))
        part = jnp.sum(g * (x * r), axis=0, keepdims=True)

        @pl.when(i == 0)
        def _():
            dw_ref[...] = jnp.zeros_like(dw_ref)
            dmeta_ref[...] = dh[PAD_ROWS:LEAD, :]

        @pl.when(i > 0)
        def _():
            gx_ref[...] = dh

        dw_ref[...] += jnp.broadcast_to(part, dw_ref.shape)

    row = pl.BlockSpec((tm, d), lambda i: (i, 0))
    return pl.pallas_call(
        body, name=name, grid=(t // tm,),
        in_specs=[row, pl.BlockSpec((1, d), lambda i: (0, 0)), row, row],
        out_specs=[pl.BlockSpec((tm, d), lambda i: (jnp.maximum(i - 1, 0), 0)),
                   pl.BlockSpec((N_META, d), lambda i: (0, 0)), pl.BlockSpec((8, d), lambda i: (0, 0))],
        out_shape=[jax.ShapeDtypeStruct((t - LEAD, d), F32), jax.ShapeDtypeStruct((N_META, d), F32),
                   jax.ShapeDtypeStruct((8, d), F32)],
        compiler_params=_params(),
    )(h, w, du, dres)


def _final_loss(h, w, tgt, name):
    t, d = h.shape
    sub = LEAD
    per = _pick(t // sub, 5, 1)
    tm = per * sub

    def body(h_ref, w_ref, *rest):
        t_refs, (dh_ref, dw_ref, loss_ref) = rest[:per], rest[per:]
        i = pl.program_id(0)

        @pl.when(i == 0)
        def _():
            dw_ref[...] = jnp.zeros_like(dw_ref)
            loss_ref[...] = jnp.zeros_like(loss_ref)

        for q in range(per):
            sl = slice(q * sub, (q + 1) * sub)
            x = h_ref[sl, :]
            r = lax.rsqrt(jnp.mean(x * x, axis=-1, keepdims=True) + RMS_EPS)
            yn = x * r
            y = yn * w_ref[...]
            rows = i * tm + q * sub + lax.broadcasted_iota(jnp.int32, (sub, 1), 0)
            diff = jnp.where(rows >= LEAD, y - t_refs[q][...], 0.0)
            tile_loss = 0.5 * jnp.sum(jnp.mean(diff * diff, axis=-1, keepdims=True), axis=0, keepdims=True)
            dy = diff / d
            gw = dy * w_ref[...]
            dot = jnp.mean(gw * x, axis=-1, keepdims=True)
            dh_ref[sl, :] = r * gw - x * ((r * r * r) * dot)
            dw_ref[...] += jnp.broadcast_to(jnp.sum(dy * yn, axis=0, keepdims=True), dw_ref.shape)
            loss_ref[...] += jnp.broadcast_to(tile_loss, loss_ref.shape)

    row = pl.BlockSpec((tm, d), lambda i: (i, 0))
    piece = [pl.BlockSpec((sub, d), functools.partial(lambda i, q: (jnp.maximum(i * per + q - 1, 0), 0), q=q))
             for q in range(per)]
    return pl.pallas_call(
        body, name=name, grid=(t // tm,),
        in_specs=[row, pl.BlockSpec((1, d), lambda i: (0, 0))] + piece,
        out_specs=[row, pl.BlockSpec((8, d), lambda i: (0, 0)), pl.BlockSpec((8, LANES), lambda i: (0, 0))],
        out_shape=[jax.ShapeDtypeStruct((t, d), F32), jax.ShapeDtypeStruct((8, d), F32),
                   jax.ShapeDtypeStruct((8, LANES), F32)],
        compiler_params=_params(),
    )(h, w, *([tgt] * per))


def _conv_gate_fwd(p0, conv_w, name, exchange=None):
    t = p0.shape[0]
    ec = p0.shape[1] // 8
    tm = _pick(t, 416, 16)
    nt = t // tm

    def body(p_ref, w_ref, g_ref, y_ref, ext_ref):
        i = pl.program_id(1)

        @pl.when(i == 0)
        def _():
            ext_ref[0:8, :] = jnp.zeros((8, ec), F32)

        for rows, cols in _strips(tm, ec):
            cg = p_ref[rows, _shift(cols, ec)].astype(F32)
            xin = p_ref[rows, _shift(cols, 2 * ec)].astype(F32)
            ext_ref[_shift(rows, 8), cols] = cg * xin
        for rows, cols in _strips(tm, ec):
            y = (w_ref[0:1, cols] * ext_ref[_shift(rows, 6), cols] + w_ref[1:2, cols] * ext_ref[_shift(rows, 7), cols]
                 + w_ref[2:3, cols] * ext_ref[_shift(rows, 8), cols])
            bg = p_ref[rows, cols].astype(F32)
            z = p_ref[rows, _shift(cols, 3 * ec)].astype(F32)
            y_ref[rows, cols] = y.astype(BF16)
            g_ref[rows, cols] = ((z * _sigmoid(z)) * bg * y).astype(BF16)
        ext_ref[0:8, :] = ext_ref[tm:tm + 8, :]

    out = pl.BlockSpec((tm, ec), lambda j, i: (i, j))
    return _call(
        body, name=name, grid=(2, nt),
        in_specs=[pl.BlockSpec((tm, 4 * ec), lambda j, i: (i, j)), pl.BlockSpec((3, ec), lambda j, i: (0, j))],
        out_specs=[out, out],
        out_shape=[jax.ShapeDtypeStruct((t, 2 * ec), BF16)] * 2,
        scratch_shapes=[pltpu.VMEM((tm + 8, ec), F32)], args=(p0, conv_w), exchange=exchange)


def _conv_gate_bwd(p0, y, dg, conv_w, name, exchange=None):
    t = p0.shape[0]
    ec = p0.shape[1] // 8
    tm = _pick(t, 208, 16)
    nt = t // tm

    def body(p_ref, y_ref, dg_ref, w_ref, dp_ref, dw_ref, ext_ref):
        i = pl.program_id(1)

        @pl.when(i == 0)
        def _():
            ext_ref[tm:tm + 8, :] = jnp.zeros((8, ec), F32)
            dw_ref[...] = jnp.zeros_like(dw_ref)

        for rows, cols in _strips(tm, ec):
            bg = p_ref[rows, cols].astype(F32)
            z = p_ref[rows, _shift(cols, 3 * ec)].astype(F32)
            yv = y_ref[rows, cols].astype(F32)
            dgv = dg_ref[rows, cols].astype(F32)
            sig = _sigmoid(z)
            sz = z * sig
            dp_ref[rows, _shift(cols, 3 * ec)] = (dgv * bg * yv * (sig * (1.0 + z * (1.0 - sig)))).astype(BF16)
            dp_ref[rows, cols] = (dgv * sz * yv).astype(BF16)
            ext_ref[rows, cols] = dgv * sz * bg
        acc = None
        for rows, cols in _strips(tm, ec):
            if rows.start == 0:
                acc = [jnp.zeros((STRIP_ROWS, cols.stop - cols.start), F32) for _ in range(3)]
            dy = ext_ref[rows, cols]
            dy1 = ext_ref[_shift(rows, 1), cols]
            dy2 = ext_ref[_shift(rows, 2), cols]
            cg = p_ref[rows, _shift(cols, ec)].astype(F32)
            xin = p_ref[rows, _shift(cols, 2 * ec)].astype(F32)
            da = w_ref[0:1, cols] * dy2 + w_ref[1:2, cols] * dy1 + w_ref[2:3, cols] * dy
            dp_ref[rows, _shift(cols, ec)] = (da * xin).astype(BF16)
            dp_ref[rows, _shift(cols, 2 * ec)] = (da * cg).astype(BF16)
            a = cg * xin
            acc = [acc[0] + a * dy2, acc[1] + a * dy1, acc[2] + a * dy]
            if rows.stop == tm:
                for tap in range(3):
                    dw_ref[tap:tap + 1, cols] += jnp.sum(acc[tap], axis=0, keepdims=True)
        ext_ref[tm:tm + 8, :] = ext_ref[0:8, :]

    rev = lambda j, i: (nt - 1 - i, j)
    return _call(
        body, name=name, grid=(2, nt),
        in_specs=[pl.BlockSpec((tm, 4 * ec), rev), pl.BlockSpec((tm, ec), rev), pl.BlockSpec((tm, ec), rev),
                  pl.BlockSpec((3, ec), lambda j, i: (0, j))],
        out_specs=[pl.BlockSpec((tm, 4 * ec), rev), pl.BlockSpec((8, ec), lambda j, i: (0, j))],
        out_shape=[jax.ShapeDtypeStruct((t, 8 * ec), BF16), jax.ShapeDtypeStruct((8, 2 * ec), F32)],
        scratch_shapes=[pltpu.VMEM((tm + 8, ec), F32)], args=(p0, y, dg, conv_w), exchange=exchange)


def _log_sigmoid(x):
    return jnp.minimum(x, 0.0) - jnp.log(1.0 + jnp.exp(-jnp.abs(x)))


def _gates_fwd(graw, graw_t, bias_row, bias_col, name):
    t = graw.shape[0]
    tm = _pick(t, 640, 128)
    cpt = tm // CHUNK
    nh = N_HEADS

    def body(g_ref, gt_ref, br_ref, bc_ref, colf_ref, rowf_ref):
        i = pl.program_id(0)
        gc = g_ref[...] + br_ref[...]
        rows = i * tm + lax.broadcasted_iota(jnp.int32, (tm, 1), 0)
        live = rows >= PAD_ROWS
        lf = jnp.where(live, _log_sigmoid(gc), 0.0)
        li = jnp.where(live, gc, NEG)
        gt = gt_ref[...] + bc_ref[...]
        cols = i * tm + lax.broadcasted_iota(jnp.int32, (1, tm), 1)
        live_t = cols >= PAD_ROWS
        lf_t = jnp.where(live_t, _log_sigmoid(gt), 0.0)
        li_t = jnp.where(live_t, gt, NEG)
        ri = lax.broadcasted_iota(jnp.int32, (CHUNK, CHUNK), 0)
        ci = lax.broadcasted_iota(jnp.int32, (CHUNK, CHUNK), 1)
        lower = (ri >= ci).astype(F32)
        upper = (ri <= ci).astype(F32)
        lane = lax.broadcasted_iota(jnp.int32, (CHUNK, LANES), 1)
        sub = lax.broadcasted_iota(jnp.int32, (8, CHUNK), 0)
        for c in range(cpt):
            sl = slice(c * CHUNK, (c + 1) * CHUNK)
            b_all = jnp.dot(lower, lf[sl, :], precision=HIGHEST, preferred_element_type=F32)
            bt_all = jnp.dot(lf_t[:, sl], upper, precision=HIGHEST, preferred_element_type=F32)
            for h in range(nh):
                b = b_all[:, nh + h:nh + h + 1]
                r = li[sl, h:h + 1] - b
                pre = gc[sl, nh + h:nh + h + 1]
                colf_ref[h, sl, :] = jnp.where(lane == 0, b, jnp.where(lane == 1, r, jnp.where(lane == 2, pre, 0.0)))
                b_row = bt_all[nh + h:nh + h + 1, :]
                r_row = li_t[h:h + 1, sl] - b_row
                rowf_ref[h, c, :, :] = jnp.where(sub == 0, r_row, jnp.where(sub == 1, b_row, 0.0))

    return pl.pallas_call(
        body, name=name, grid=(t // tm,),
        in_specs=[pl.BlockSpec((tm, LANES), lambda i: (i, 0)), pl.BlockSpec((16, tm), lambda i: (0, i)),
                  pl.BlockSpec((1, LANES), lambda i: (0, 0)), pl.BlockSpec((16, 1), lambda i: (0, 0))],
        out_specs=[pl.BlockSpec((nh, tm, LANES), lambda i: (0, i, 0)),
                   pl.BlockSpec((nh, cpt, 8, CHUNK), lambda i: (0, i, 0, 0))],
        out_shape=[jax.ShapeDtypeStruct((nh, t, LANES), F32),
                   jax.ShapeDtypeStruct((nh, t // CHUNK, 8, CHUNK), F32)],
        compiler_params=_params(),
    )(graw, graw_t, bias_row, bias_col)


def _gates_bwd(gstat, colf, dp1, name):
    nh, t, _ = gstat.shape
    tm = _pick(t, 640, 128)
    cpt = tm // CHUNK
    nt = t // tm
    nc = t // CHUNK
    gate_block = dp1.shape[1] // LANES - 1

    def body(gs_ref, nx_ref, colf_ref, dp_any, dg_ref, db_ref):
        i = pl.program_id(0)
        ri = lax.broadcasted_iota(jnp.int32, (CHUNK, CHUNK), 0)
        ci = lax.broadcasted_iota(jnp.int32, (CHUNK, CHUNK), 1)
        upper = (ri <= ci).astype(F32)
        lane = lax.broadcasted_iota(jnp.int32, (CHUNK, LANES), 1)
        total = jnp.zeros((1, LANES), F32)
        for c in range(cpt):
            sl = slice(c * CHUNK, (c + 1) * CHUNK)
            rows = i * tm + c * CHUNK + lax.broadcasted_iota(jnp.int32, (CHUNK, 1), 0)
            live = rows >= PAD_ROWS
            acc = jnp.zeros((CHUNK, LANES), F32)
            for h in range(nh):
                blk = gs_ref[h, sl, :]
                rev = jnp.dot(upper, blk, precision=HIGHEST, preferred_element_type=F32)
                if c + 1 < cpt:
                    carry = gs_ref[h, (c + 1) * CHUNK:(c + 1) * CHUNK + 1, 2:3]
                else:
                    carry = jnp.where(i == nt - 1, 0.0, nx_ref[h, 0:1, 2:3])
                dlogf = rev[:, 0:1] + carry
                pre = colf_ref[h, sl, 2:3]
                dgf = jnp.where(live, dlogf * (1.0 - _sigmoid(pre)), 0.0)
                dgi = jnp.where(live, blk[:, 1:2], 0.0)
                acc = acc + jnp.where(lane == h, dgi, 0.0) + jnp.where(lane == nh + h, dgf, 0.0)
            dg_ref[sl, :] = acc.astype(BF16)
            total = total + jnp.sum(acc, axis=0, keepdims=True)

        @pl.when(i == 0)
        def _():
            db_ref[...] = jnp.zeros_like(db_ref)

        db_ref[...] += jnp.broadcast_to(total, db_ref.shape)

    return pl.pallas_call(
        body, name=name, grid=(nt,),
        in_specs=[pl.BlockSpec((nh, tm, LANES), lambda i: (0, i, 0)),
                  pl.BlockSpec((nh, CHUNK, LANES), lambda i: (0, jnp.minimum((i + 1) * cpt, nc - 1), 0)),
                  pl.BlockSpec((nh, tm, LANES), lambda i: (0, i, 0)),
                  pl.BlockSpec(memory_space=pl.ANY)],
        out_specs=[pl.BlockSpec((tm, LANES), lambda i: (i, gate_block)), pl.BlockSpec((8, LANES), lambda i: (0, 0))],
        out_shape=[jax.ShapeDtypeStruct(dp1.shape, dp1.dtype), jax.ShapeDtypeStruct((8, LANES), F32)],
        input_output_aliases={3: 0},
        compiler_params=_params(),
    )(gstat, gstat, colf, dp1)


NT_DIMS = (((1,), (1,)), ((), ()))
TN_DIMS = (((0,), (0,)), ((), ()))


def _dot(a, b):
    return jnp.dot(a.astype(BF16), b.astype(BF16), preferred_element_type=F32)


def _dot_nt(a, b):
    return lax.dot_general(a.astype(BF16), b.astype(BF16), NT_DIMS, preferred_element_type=F32)


def _dot_tn(a, b):
    return lax.dot_general(a.astype(BF16), b.astype(BF16), TN_DIMS, preferred_element_type=F32)


def _chunk_gates(colf_ref, rowf_ref, m_prev):
    b = colf_ref[:, 0:1]
    rcol = colf_ref[:, 1:2]
    rrow = rowf_ref[0:1, :]
    ri = lax.broadcasted_iota(jnp.int32, (CHUNK, CHUNK), 0)
    ci = lax.broadcasted_iota(jnp.int32, (CHUNK, CHUNK), 1)
    log_d = jnp.where(ri >= ci, b + rrow, NEG)
    m_row = jnp.maximum(b + m_prev, jnp.max(log_d, axis=-1, keepdims=True))
    dmat = jnp.exp(log_d - m_row)
    inter = jnp.exp(b + m_prev - m_row)
    b_last = b[CHUNK - 1:CHUNK, :]
    log_w = rcol + b_last
    m_new = jnp.maximum(b_last + m_prev, jnp.max(log_w, axis=0, keepdims=True))
    decay = jnp.exp(b_last + m_prev - m_new)
    w = jnp.exp(log_w - m_new)
    return m_row, dmat, inter, m_new, decay, w


def _mlstm_fwd(p1, colf, rowf, dk, dv, name):
    t = p1.shape[0]
    nh = N_HEADS
    nc = t // CHUNK
    hw = 2 * dk + dv
    cps = CHUNKS_PER_STEP if nc % CHUNKS_PER_STEP == 0 else 1
    rows_per_step = cps * CHUNK

    def body(p_ref, colf_ref, rowf_ref, hh_ref, stat_ref, cs_ref, ns_ref, *state):
        c_refs, n_refs, m_refs = state[:nh], state[nh:2 * nh], state[2 * nh:]
        c = pl.program_id(0)

        @pl.when(c == 0)
        def _():
            for ref in state:
                ref[...] = jnp.zeros_like(ref)

        for cc in range(cps):
            rows = pl.ds(cc * CHUNK, CHUNK)
            for h in range(nh):
                head(p_ref.at[rows, pl.ds(h * hw, hw)], colf_ref.at[h, rows], rowf_ref.at[h, cc],
                     hh_ref.at[rows, pl.ds(h * dv, dv)], stat_ref.at[h, rows], cs_ref.at[h, cc],
                     ns_ref.at[h, cc], c_refs[h], n_refs[h], m_refs[h])

    def head(p_ref, colf_ref, rowf_ref, hh_ref, stat_ref, cs_ref, ns_ref, c_ref, n_ref, m_ref):
        m_prev = m_ref[...]
        n_prev = n_ref[...]
        c_prev = c_ref[...]
        cs_ref[...] = c_prev.astype(BF16)
        sub = lax.broadcasted_iota(jnp.int32, (8, dk), 0)
        ns_ref[...] = jnp.where(sub == 0, n_prev, jnp.where(sub == 1, m_prev, 0.0))

        q = p_ref[:, 0:dk]
        k = p_ref[:, dk:2 * dk]
        v = p_ref[:, 2 * dk:2 * dk + dv]
        m_row, dmat, inter, m_new, decay, w = _chunk_gates(colf_ref, rowf_ref, m_prev)
        s = _dot_nt(q, k) * dmat
        num = _dot(s, v) + inter * _dot(q, c_prev)
        den = jnp.sum(s, axis=-1, keepdims=True) + inter * jnp.sum(q.astype(F32) * n_prev, axis=-1, keepdims=True)
        denom = jnp.maximum(jnp.abs(den), jnp.exp(-m_row))
        hh_ref[...] = num * (1.0 / denom)
        lane = lax.broadcasted_iota(jnp.int32, (CHUNK, LANES), 1)
        stat_ref[...] = jnp.where(lane == 0, den, 0.0)

        wk = w * k.astype(F32)
        c_ref[...] = decay * c_prev + _dot_tn(wk, v)
        n_ref[...] = decay * n_prev + jnp.sum(wk, axis=0, keepdims=True)
        m_ref[...] = m_new

    return pl.pallas_call(
        body, name=name, grid=(nc // cps,),
        in_specs=[pl.BlockSpec((rows_per_step, nh * hw), lambda c: (c, 0)),
                  pl.BlockSpec((nh, rows_per_step, LANES), lambda c: (0, c, 0)),
                  pl.BlockSpec((nh, cps, 8, CHUNK), lambda c: (0, c, 0, 0))],
        out_specs=[pl.BlockSpec((rows_per_step, nh * dv), lambda c: (c, 0)),
                   pl.BlockSpec((nh, rows_per_step, LANES), lambda c: (0, c, 0)),
                   pl.BlockSpec((nh, cps, dk, dv), lambda c: (0, c, 0, 0)),
                   pl.BlockSpec((nh, cps, 8, dk), lambda c: (0, c, 0, 0))],
        out_shape=[jax.ShapeDtypeStruct((t, nh * dv), F32), jax.ShapeDtypeStruct((nh, t, LANES), F32),
                   jax.ShapeDtypeStruct((nh, nc, dk, dv), BF16), jax.ShapeDtypeStruct((nh, nc, 8, dk), F32)],
        scratch_shapes=([pltpu.VMEM((dk, dv), F32)] * nh + [pltpu.VMEM((1, dk), F32)] * nh
                        + [pltpu.VMEM((1, 1), F32)] * nh),
        compiler_params=_params(),
    )(p1, colf, rowf)


def _head_gate_fwd(hh, p1, head_w, dv, name):
    t = hh.shape[0]
    nh = N_HEADS
    e = nh * dv
    tm = _pick(t, 416, 16)

    def body(hh_ref, oz_ref, w_ref, hm_ref):
        for h in range(nh):
            cols = slice(h * dv, (h + 1) * dv)
            x = hh_ref[:, cols]
            o = oz_ref[:, 2 * h * dv:(2 * h + 1) * dv].astype(F32)
            z = oz_ref[:, (2 * h + 1) * dv:(2 * h + 2) * dv].astype(F32)
            r = lax.rsqrt(jnp.mean(x * x, axis=-1, keepdims=True) + RMS_EPS)
            hn = (x * r) * w_ref[:, cols]
            hm_ref[:, cols] = (hn * _sigmoid(o) * (z * _sigmoid(z))).astype(BF16)

    return pl.pallas_call(
        body, name=name, grid=(t // tm,),
        in_specs=[pl.BlockSpec((tm, e), lambda i: (i, 0)), pl.BlockSpec((tm, 2 * e), lambda i: (i, 1)),
                  pl.BlockSpec((1, e), lambda i: (0, 0))],
        out_specs=pl.BlockSpec((tm, e), lambda i: (i, 0)),
        out_shape=jax.ShapeDtypeStruct((t, e), BF16),
        compiler_params=_params(),
    )(hh, p1, head_w)


def _mlstm_bwd(p1, colf, rowf, head_w, hh, stat, csave, nsave, dhm, n_cols, dk, dv, name, exchange=None):
    t = p1.shape[0]
    nh = N_HEADS
    nc = t // CHUNK
    hw = 2 * dk + dv
    cps = CHUNKS_PER_STEP if nc % CHUNKS_PER_STEP == 0 else 1
    rows_per_step = cps * CHUNK

    def body(p_ref, colf_ref, rowf_ref, w_ref, hh_ref, stat_ref, cs_ref, ns_ref, dhm_ref,
             dp_ref, gs_ref, dw_ref, *state):
        dc_refs, dn_refs, dwacc_refs = state[:nh], state[nh:2 * nh], state[2 * nh:]
        c = pl.program_id(0)

        @pl.when(c == 0)
        def _():
            for ref in state:
                ref[...] = jnp.zeros_like(ref)

        for cc in reversed(range(cps)):
            rows = pl.ds(cc * CHUNK, CHUNK)
            for h in range(nh):
                cols = pl.ds(h * dv, dv)
                oz = pl.ds(nh * hw + h * 2 * dv, 2 * dv)
                head(p_ref.at[rows, pl.ds(h * hw, hw)], p_ref.at[rows, oz], colf_ref.at[h, rows], rowf_ref.at[h, cc],
                     w_ref.at[:, cols], hh_ref.at[rows, cols], stat_ref.at[h, rows], cs_ref.at[h, cc],
                     ns_ref.at[h, cc], dhm_ref.at[rows, cols], dp_ref.at[rows, pl.ds(h * hw, hw)],
                     dp_ref.at[rows, oz], gs_ref.at[h, rows], dwacc_refs[h], dc_refs[h], dn_refs[h])

        @pl.when(c == nc // cps - 1)
        def _():
            for h in range(nh):
                dw_ref[h] = dwacc_refs[h][...]

    def head(p_ref, oz_ref, colf_ref, rowf_ref, w_ref, hh_ref, stat_ref, cs_ref, ns_ref, dhm_ref,
             dp_ref, doz_ref, gs_ref, dw_ref, dc_ref, dn_ref):
        q = p_ref[:, 0:dk]
        k = p_ref[:, dk:2 * dk]
        v = p_ref[:, 2 * dk:2 * dk + dv]
        o = oz_ref[:, 0:dv].astype(F32)
        z = oz_ref[:, dv:2 * dv].astype(F32)
        qf = q.astype(F32)
        kf = k.astype(F32)
        n_prev = ns_ref[0:1, :]
        m_prev = ns_ref[1:2, 0:1]
        c_prev = cs_ref[...]
        m_row, dmat, inter, m_new, decay, w = _chunk_gates(colf_ref, rowf_ref, m_prev)

        hh = hh_ref[...]
        dhm_v = dhm_ref[...].astype(F32)
        so = _sigmoid(o)
        sg = _sigmoid(z)
        sz = z * sg
        r = lax.rsqrt(jnp.mean(hh * hh, axis=-1, keepdims=True) + RMS_EPS)
        hn = (hh * r) * w_ref[...]
        dhn = dhm_v * so * sz
        doz_ref[:, 0:dv] = (dhm_v * hn * sz * (so * (1.0 - so))).astype(BF16)
        doz_ref[:, dv:2 * dv] = (dhm_v * hn * so * (sg * (1.0 + z * (1.0 - sg)))).astype(BF16)
        dw_ref[...] += jnp.broadcast_to(jnp.sum(dhn * (hh * r), axis=0, keepdims=True), dw_ref.shape)
        gwn = dhn * w_ref[...]
        dhh = r * gwn - hh * ((r * r * r) * jnp.mean(gwn * hh, axis=-1, keepdims=True))

        den = stat_ref[:, 0:1]
        floor = jnp.exp(-m_row)
        denom = jnp.maximum(jnp.abs(den), floor)
        inv = 1.0 / denom
        dnum = dhh * inv
        hdot = jnp.sum(dhh * hh, axis=-1, keepdims=True)
        dden = jnp.where(jnp.abs(den) > floor, -(hdot * inv) * jnp.sign(den), 0.0)
        s = _dot_nt(q, k) * dmat
        dqk = (_dot_nt(dnum, v) + dden) * dmat
        dc_new = dc_ref[...]
        dn_new = dn_ref[...]
        idd = inter * dden
        dq = _dot(dqk, k) + inter * _dot_nt(dnum, c_prev) + idd * n_prev
        dkv = _dot_tn(dqk, q) + w * (_dot_nt(v, dc_new) + dn_new)
        dvv = _dot_tn(s, dnum) + w * _dot(k, dc_new)
        dp_ref[:, 0:dk] = dq.astype(BF16)
        dp_ref[:, dk:2 * dk] = dkv.astype(BF16)
        dp_ref[:, 2 * dk:2 * dk + dv] = dvv.astype(BF16)
        qdq = jnp.sum(qf * dq, axis=-1, keepdims=True)
        kdk = jnp.sum(kf * dkv, axis=-1, keepdims=True)
        dc_prev = decay * dc_new + _dot_tn(inter * qf, dnum)
        dn_prev = decay * dn_new + jnp.sum(idd * qf, axis=0, keepdims=True)
        dc_ref[...] = dc_prev
        dn_ref[...] = dn_prev
        cross = (jnp.sum(jnp.sum(c_prev.astype(F32) * dc_prev, axis=-1, keepdims=True), axis=0, keepdims=True)
                 + jnp.sum(n_prev * dn_prev, axis=-1, keepdims=True))
        lane = lax.broadcasted_iota(jnp.int32, (CHUNK, LANES), 1)
        gs_ref[...] = jnp.where(lane == 0, qdq - kdk, jnp.where(lane == 1, kdk, jnp.where(lane == 2, cross, 0.0)))

    ns = nc // cps
    rc = lambda c: (ns - 1 - c, 0)
    rc3 = lambda c: (0, ns - 1 - c, 0)
    rc4 = lambda c: (0, ns - 1 - c, 0, 0)
    return _call(
        body, name=name, grid=(ns,),
        in_specs=[pl.BlockSpec((rows_per_step, nh * (hw + 2 * dv)), rc),
                  pl.BlockSpec((nh, rows_per_step, LANES), rc3),
                  pl.BlockSpec((nh, cps, 8, CHUNK), rc4),
                  pl.BlockSpec((1, nh * dv), lambda c: (0, 0)),
                  pl.BlockSpec((rows_per_step, nh * dv), rc),
                  pl.BlockSpec((nh, rows_per_step, LANES), rc3),
                  pl.BlockSpec((nh, cps, dk, dv), rc4),
                  pl.BlockSpec((nh, cps, 8, dk), rc4),
                  pl.BlockSpec((rows_per_step, nh * dv), rc)],
        out_specs=[pl.BlockSpec((rows_per_step, nh * (hw + 2 * dv)), rc),
                   pl.BlockSpec((nh, rows_per_step, LANES), rc3),
                   pl.BlockSpec((nh, 8, dv), lambda c: (0, 0, 0))],
        out_shape=[jax.ShapeDtypeStruct((t, n_cols), BF16), jax.ShapeDtypeStruct((nh, t, LANES), F32),
                   jax.ShapeDtypeStruct((nh, 8, dv), F32)],
        scratch_shapes=([pltpu.VMEM((dk, dv), F32)] * nh + [pltpu.VMEM((1, dk), F32)] * nh
                        + [pltpu.VMEM((8, dv), F32)] * nh),
        args=(p1, colf, rowf, head_w, hh, stat, csave, nsave, dhm), exchange=exchange)


def _position():
    return lax.axis_index("x"), lax.axis_index("y"), lax.axis_index("c")


class _Exchange:
    def __init__(self, ins, out_shapes, start, mid, finish):
        n = len(ins)
        self.ins, self.out_shapes = list(ins), list(out_shapes)
        self.start, self.mid, self.finish = start, mid, finish
        self.scratch = [pltpu.SemaphoreType.DMA((n, 7)), pltpu.SemaphoreType.DMA((n, 7)),
                        pltpu.SemaphoreType.DMA((n,))]


def _gather_exchange(shards):
    n = len(shards)

    def plan(ins, outs, sems):
        send_sems, recv_sems, local_sems = sems
        x, y, c = _position()
        me, sibling = (x, y, c), (x, y, 1 - c)
        chips = [(1 - x, y), (x, 1 - y), (1 - x, 1 - y)]

        def copy(a, k, block, to, src=None):
            px, py, pc = block
            dst = outs[a].at[4 * px + 2 * py + pc]
            return pltpu.make_async_remote_copy(
                src_ref=dst if src is None else src, dst_ref=dst,
                send_sem=send_sems.at[a, k], recv_sem=recv_sems.at[a, k],
                device_id=to, device_id_type=MESH)

        def mine():
            return [pltpu.make_async_copy(ins[a], outs[a].at[4 * x + 2 * y + c], local_sems.at[a])
                    for a in range(n)]

        def first():
            out = []
            for a in range(n):
                out.append(copy(a, 0, me, sibling, src=ins[a]))
                out += [copy(a, 1 + j, me, (*chip, c), src=ins[a]) for j, chip in enumerate(chips)]
            return out

        def ici_in():
            return [copy(a, 1 + j, (*chip, c), me) for j, chip in enumerate(chips) for a in range(n)]

        def passed():
            return [copy(a, 4 + j, (*chip, c), sibling) for j, chip in enumerate(chips) for a in range(n)]

        def d2d_in():
            return ([copy(a, 0, sibling, me) for a in range(n)]
                    + [copy(a, 4 + j, (*chip, 1 - c), me) for j, chip in enumerate(chips) for a in range(n)])

        return mine, first, ici_in, passed, d2d_in

    def start(ins, outs, sems):
        mine, first, _, _, _ = plan(ins, outs, sems)
        for cp in mine() + first():
            cp.start()

    def mid(ins, outs, sems):
        _, _, ici_in, passed, _ = plan(ins, outs, sems)
        for arrived, onward in zip(ici_in(), passed()):
            arrived.wait_recv()
            onward.start()

    def finish(ins, outs, sems):
        mine, first, _, passed, d2d_in = plan(ins, outs, sems)
        for cp in d2d_in():
            cp.wait_recv()
        for cp in first() + passed():
            cp.wait_send()
        for cp in mine():
            cp.wait()

    shapes = [jax.ShapeDtypeStruct((N_DEV,) + s.shape, s.dtype) for s in shards]
    return _Exchange(shards, shapes, start, mid, finish)


def _scatter_exchange(fulls):
    n = len(fulls)

    def plan(ins, outs, sems):
        send_sems, recv_sems, local_sems = sems
        x, y, c = _position()
        my_slot = 4 * x + 2 * y + c

        def mine():
            return [pltpu.make_async_copy(ins[a].at[my_slot], outs[a].at[my_slot], local_sems.at[a])
                    for a in range(n)]

        def remote(arriving):
            out = []
            for kk in (1, 2, 4, 6, 3, 5, 7):
                kx, ky, kc = (kk >> 2) & 1, (kk >> 1) & 1, kk & 1
                px = 1 - x if kx else x
                py = 1 - y if ky else y
                pc = 1 - c if kc else c
                peer_slot = 4 * px + 2 * py + pc
                for a in range(n):
                    out.append(pltpu.make_async_remote_copy(
                        src_ref=ins[a].at[peer_slot], dst_ref=outs[a].at[peer_slot if arriving else my_slot],
                        send_sem=send_sems.at[a, kk - 1], recv_sem=recv_sems.at[a, kk - 1],
                        device_id=(px, py, pc), device_id_type=MESH))
            return out

        return mine, remote

    def start(ins, outs, sems):
        mine, remote = plan(ins, outs, sems)
        for cp in mine() + remote(False):
            cp.start()

    def finish(ins, outs, sems):
        mine, remote = plan(ins, outs, sems)
        for cp in remote(True):
            cp.wait_recv()
        for cp in remote(False):
            cp.wait_send()
        for cp in mine():
            cp.wait()

    shapes = [jax.ShapeDtypeStruct(f.shape, f.dtype) for f in fulls]
    return _Exchange(fulls, shapes, start, None, finish)


def _run_exchange(ex, name):
    n_in, n_out = len(ex.ins), len(ex.out_shapes)

    def body(*refs):
        ins, outs, sems = refs[:n_in], refs[n_in:n_in + n_out], refs[n_in + n_out:]
        ex.start(ins, outs, sems)
        if ex.mid is not None:
            ex.mid(ins, outs, sems)
        ex.finish(ins, outs, sems)

    any_spec = pl.BlockSpec(memory_space=pl.ANY)
    return pl.pallas_call(
        body, name=name,
        in_specs=[any_spec] * n_in, out_specs=[any_spec] * n_out,
        out_shape=ex.out_shapes, scratch_shapes=ex.scratch,
    )(*ex.ins)


def _natural_to_head_major(n0, dk, dv):
    nh = N_HEADS
    qk, e, hw = nh * dk, nh * dv, 2 * dk + dv
    if n0 < qk:
        h, off = divmod(n0, dk)
        return h * hw + off, True
    n1 = n0 - qk
    if n1 < qk:
        h, off = divmod(n1, dk)
        return h * hw + dk + off, False
    part, n3 = divmod(n1 - qk, e)
    h, off = divmod(n3, dv)
    if part == 0:
        return h * hw + 2 * dk + off, False
    return nh * hw + h * 2 * dv + (part - 1) * dv + off, False


def _mlstm_weight_layout(w_blocks, dk, dv, name):
    _, d, blk = w_blocks.shape
    nh = N_HEADS
    hw = 2 * dk + 3 * dv
    n_tiles = nh * hw // LANES
    tiles_per_block = d // LANES
    tr = _pick(d, 128, 16)
    scale = dk ** -0.5

    def body(w_ref, p_ref, g_ref):
        lane = lax.broadcasted_iota(jnp.int32, (tr, LANES), 1)

        def tile(s, r):
            return w_ref[s, :, r * LANES:(r + 1) * LANES].astype(F32)

        def last_col(s):
            return w_ref[s, :, d:d + 1].astype(F32)

        for tn in range(n_tiles):
            s, r = divmod(tn, tiles_per_block)
            if s == 0:
                val = tile(0, r)
            elif r == 0:
                from_prev = pltpu.roll(tile(s - 1, tiles_per_block - 1), s - 1, 1)
                from_here = pltpu.roll(tile(s, 0), s, 1)
                val = jnp.where(lane < s - 1, from_prev, jnp.where(lane == s - 1, last_col(s - 1), from_here))
            else:
                slab = jnp.concatenate([tile(s, r - 1), tile(s, r)], axis=1)
                val = pltpu.roll(slab, s, 1)[:, LANES:]
            at, is_q = _natural_to_head_major(tn * LANES, dk, dv)
            if is_q:
                val = val * scale
            p_ref[:, at:at + LANES] = val.astype(BF16)
        n_gate = 2 * nh
        s = N_DEV - 1
        gates = pltpu.roll(tile(s, tiles_per_block - 1), n_gate - 1, 1)
        gates = jnp.where(lane < n_gate - 1, gates, jnp.where(lane == n_gate - 1, last_col(s), 0.0)).astype(BF16)
        g_ref[...] = gates
        p_ref[:, nh * hw:nh * hw + LANES] = gates

    return pl.pallas_call(
        body, name=name, grid=(d // tr,),
        in_specs=[pl.BlockSpec((N_DEV, tr, blk), lambda i: (0, i, 0))],
        out_specs=[pl.BlockSpec((tr, nh * hw + LANES), lambda i: (i, 0)), pl.BlockSpec((tr, LANES), lambda i: (i, 0))],
        out_shape=[jax.ShapeDtypeStruct((d, nh * hw + LANES), BF16), jax.ShapeDtypeStruct((d, LANES), BF16)],
        compiler_params=_params(),
    )(w_blocks)


def _mlstm_grad_layout(g_p1, dk, dv, name):
    d = g_p1.shape[0]
    nh = N_HEADS
    hw = 2 * dk + 3 * dv
    n_tiles = nh * hw // LANES
    tiles_per_block = d // LANES
    tr = _pick(d, 128, 16)
    scale = dk ** -0.5

    def body(p_ref, o_ref):
        def natural(tn):
            if tn == n_tiles:
                return p_ref[:, nh * hw:nh * hw + LANES].astype(F32)
            at, is_q = _natural_to_head_major(tn * LANES, dk, dv)
            val = p_ref[:, at:at + LANES].astype(F32)
            return val * scale if is_q else val

        for s in range(N_DEV):
            for r in range(tiles_per_block):
                tn = s * tiles_per_block + r
                if s == 0:
                    val = natural(tn)
                else:
                    slab = jnp.concatenate([natural(tn), natural(tn + 1)], axis=1)
                    val = pltpu.roll(slab, 2 * LANES - s, 1)[:, :LANES]
                o_ref[s, :, r * LANES:(r + 1) * LANES] = val.astype(BF16)
            o_ref[s, :, d:d + 1] = natural((s + 1) * tiles_per_block)[:, s:s + 1].astype(BF16)

    return pl.pallas_call(
        body, name=name, grid=(d // tr,),
        in_specs=[pl.BlockSpec((tr, nh * hw + LANES), lambda i: (i, 0))],
        out_specs=pl.BlockSpec((N_DEV, tr, d + 1), lambda i: (0, i, 0)),
        out_shape=jax.ShapeDtypeStruct((N_DEV, d, d + 1), BF16),
        compiler_params=_params(),
    )(g_p1)


def _adamw_math(w, g, m, v):
    m = ADAM_B1 * m + (1.0 - ADAM_B1) * g
    v = ADAM_B2 * v + (1.0 - ADAM_B2) * (g * g)
    m_hat = m / (1.0 - ADAM_B1 ** ADAM_STEP)
    v_hat = v / (1.0 - ADAM_B2 ** ADAM_STEP)
    delta = -ADAM_LR * (m_hat / (jnp.sqrt(v_hat) + ADAM_EPS) + ADAM_WD * w)
    return delta, m, v


def _adamw_sharded(parts, w, m, v, name):
    _, r, c = parts.shape
    tr = _pick(r, 128, 8)

    def body(p_ref, w_ref, m_ref, v_ref, g_ref, d_ref, nm_ref, nv_ref):
        g = p_ref[0].astype(F32)
        for s in range(1, N_DEV):
            g = g + p_ref[s].astype(F32)
        delta, m_new, v_new = _adamw_math(w_ref[...], g, m_ref[...], v_ref[...])
        g_ref[...] = g
        d_ref[...] = delta
        nm_ref[...] = m_new
        nv_ref[...] = v_new

    blk = pl.BlockSpec((tr, c), lambda i: (i, 0))
    return pl.pallas_call(
        body, name=name, grid=(r // tr,),
        in_specs=[pl.BlockSpec((N_DEV, tr, c), lambda i: (0, i, 0)), blk, blk, blk],
        out_specs=[blk] * 4,
        out_shape=[jax.ShapeDtypeStruct((r, c), F32)] * 4,
        compiler_params=_params(),
    )(parts, w, m, v)


def _sum_devices(parts, name):
    _, r, c = parts.shape

    def body(p_ref, o_ref):
        g = p_ref[0]
        for s in range(1, N_DEV):
            g = g + p_ref[s]
        o_ref[...] = g

    return pl.pallas_call(
        body, name=name, out_shape=jax.ShapeDtypeStruct((r, c), F32), compiler_params=_params(),
    )(parts)


def _adamw_small(gs, ws, ms, vs, name):
    n = len(gs)

    def body(*refs):
        g_refs, w_refs, m_refs, v_refs = refs[:n], refs[n:2 * n], refs[2 * n:3 * n], refs[3 * n:4 * n]
        d_refs, nm_refs, nv_refs = refs[4 * n:5 * n], refs[5 * n:6 * n], refs[6 * n:7 * n]
        for a in range(n):
            delta, m_new, v_new = _adamw_math(w_refs[a][...], g_refs[a][...], m_refs[a][...], v_refs[a][...])
            d_refs[a][...] = delta
            nm_refs[a][...] = m_new
            nv_refs[a][...] = v_new

    shapes = [jax.ShapeDtypeStruct(w.shape, F32) for w in ws]
    outs = pl.pallas_call(
        body, name=name, out_shape=shapes * 3, compiler_params=_params(),
    )(*gs, *ws, *ms, *vs)
    return outs[:n], outs[n:2 * n], outs[2 * n:]


def _pad_rows(a, rows):
    return jnp.pad(a, ((0, rows - a.shape[0]), (0, 0)))


def kernel(x, meta_tokens, norm_w, conv_in_w, conv_w, conv_out_w, mlstm_in_w, mlstm_gate_b, mlstm_head_norm_w, mlstm_out_w, final_norm_w, loss_target, m_meta_tokens, m_norm_w, m_conv_in_w, m_conv_w, m_conv_out_w, m_mlstm_in_w, m_mlstm_gate_b, m_mlstm_head_norm_w, m_mlstm_out_w, m_final_norm_w, v_meta_tokens, v_norm_w, v_conv_in_w, v_conv_w, v_conv_out_w, v_mlstm_in_w, v_mlstm_gate_b, v_mlstm_head_norm_w, v_mlstm_out_w, v_final_norm_w):
    seq, d = x.shape[1], x.shape[2]
    t = seq + LEAD
    e = 2 * d
    ec = e // 2
    nh = N_HEADS
    dv = e // nh
    dk = dv // 2
    qk = nh * dk
    hw = 2 * dk + 3 * dv
    n_in = 2 * qk + 3 * e + 2 * nh
    n_in_s = n_in // N_DEV
    me = 4 * lax.axis_index("x") + 2 * lax.axis_index("y") + lax.axis_index("c")
    tm = _pick(t, 832, 16)
    tm_in = _pick(t, 1664, 16)
    tkw = _pick(t, 2080, 16)

    small = jnp.concatenate([
        meta_tokens,
        _pad_rows(conv_w[0].reshape(3 * (e // N_DEV) // LANES, LANES), 8),
        _pad_rows(mlstm_head_norm_w[0].reshape((e // N_DEV) // LANES, LANES), 8),
    ], axis=0) if d // N_DEV == LANES else None
    assert small is not None, "the packed small-weight block assumes d_model / 8 == 128"
    (small_g,) = _run_exchange(_gather_exchange([small]), "gather_small_weights")
    meta_full = jnp.transpose(small_g[:, 0:N_META, :], (1, 0, 2)).reshape(N_META, d)
    cw_rows = 3 * (e // N_DEV) // LANES
    conv_w_full = jnp.transpose(
        small_g[:, N_META:N_META + cw_rows, :].reshape(N_DEV, 3, e // N_DEV), (1, 0, 2)).reshape(3, e)
    hn_rows = (e // N_DEV) // LANES
    head_w_full = small_g[:, N_META + 8:N_META + 8 + hn_rows, :].reshape(1, e)

    h0 = jnp.concatenate([jnp.zeros((PAD_ROWS, d), F32), meta_full, x[0]], axis=0)
    tgt = loss_target[0]

    ci_map = lambda blk: 2 * (blk % 4) + blk // 4
    u0, (w_ci,) = _rms_fwd(h0, norm_w[0:1], "rms0", exchange=_gather_exchange([conv_in_w[0].astype(BF16)]))
    p0, (w_co, w_mi) = _matmul(
        u0, w_ci, form="nn", m=t, n=8 * ec, kdim=d, tm=tm_in, tn=ec, tk=d, out_dtype=BF16, name="conv_in",
        b_spec=pl.BlockSpec((None, d, ec), lambda i, j, k: (ci_map(j), 0, 0)),
        exchange=_gather_exchange([conv_out_w[0].astype(BF16), mlstm_in_w[0].astype(BF16)]))
    w_co = w_co.reshape(e, d)
    (g0, y0), (w_mo,) = _conv_gate_fwd(p0, conv_w_full, "conv_gate",
                                       exchange=_gather_exchange([mlstm_out_w[0].astype(BF16)]))
    w_mo = w_mo.reshape(e, d)
    h1 = _matmul(g0, w_co, form="nn", m=t, n=d, kdim=e, tm=tm, tn=d, tk=e, out_dtype=F32, name="conv_out",
                 residual=h0)

    assert n_in_s == d + 1 and w_mi.shape == (N_DEV, d, d + 1)
    w_p1, w_gate = _mlstm_weight_layout(w_mi, dk, dv, "mlstm_w_layout")
    w_gate_t = jnp.transpose(w_gate[:, 0:16])
    bias_row = jnp.pad(mlstm_gate_b, ((0, 0), (0, LANES - 2 * nh)))
    bias_col = jnp.pad(mlstm_gate_b.T, ((0, 16 - 2 * nh), (0, 0)))

    u1 = _rms_fwd(h1, norm_w[1:2], "rms1")
    p1 = _matmul(u1, w_p1, form="nn", m=t, n=nh * hw, kdim=d, tm=tm_in, tn=_pick(nh * hw, 1024, LANES), tk=d,
                 out_dtype=BF16, name="mlstm_in")
    graw = _matmul(u1, w_gate, form="nn", m=t, n=LANES, kdim=d, tm=tm, tn=LANES, tk=d, out_dtype=F32,
                   name="gates_col")
    graw_t = _matmul(w_gate_t, u1, form="nt", m=16, n=t, kdim=d, tm=16, tn=_pick(t, 1664, LANES), tk=d,
                     out_dtype=F32, name="gates_row")
    colf, rowf = _gates_fwd(graw, graw_t, bias_row, bias_col, "gates_fwd")
    hh, stat, csave, nsave = _mlstm_fwd(p1, colf, rowf, dk, dv, "mlstm_fwd")
    hm = _head_gate_fwd(hh, p1, head_w_full, dv, "head_gate")
    h2 =_matmul(hm, w_mo, form="nn", m=t, n=d, kdim=e, tm=tm, tn=d, tk=e, out_dtype=F32, name="mlstm_out",
                 residual=h1)

    dh2, dwf, loss_part = _final_loss(h2, final_norm_w.reshape(1, d), tgt, "final_loss")

    dhm = _matmul(dh2, w_mo, form="nt", m=t, n=e, kdim=d, tm=tm, tn=_pick(e, 2048, LANES), tk=d, out_dtype=BF16,
                  name="mlstm_out_dx")
    g_mo = _matmul(hm, dh2, form="tn", m=e, n=d, kdim=t, tm=_pick(e, 1024, LANES), tn=d, tk=tkw, out_dtype=BF16,
                   name="mlstm_out_dw")
    n_p1 = nh * hw + LANES
    (dp1, gstat, dhead), (r_mo,) = _mlstm_bwd(
        p1, colf, rowf, head_w_full, hh, stat, csave, nsave, dhm, n_p1, dk, dv, "mlstm_bwd",
        exchange=_scatter_exchange([g_mo.reshape(N_DEV, e // N_DEV, d)]))
    dp1, dbias = _gates_bwd(gstat, colf, dp1, "gates_bwd")
    du1 = _matmul(dp1, w_p1, form="nt", m=t, n=d, kdim=n_p1, tm=tm, tn=d, tk=_pick(n_p1, 2048, LANES),
                  out_dtype=F32, name="mlstm_in_dx")
    g_p1 = _matmul(u1, dp1, form="tn", m=d, n=n_p1, kdim=t, tm=d, tn=_pick(n_p1, 2048, LANES), tk=tkw,
                   out_dtype=BF16, name="mlstm_in_dw")
    dh1, dnw1 = _rms_bwd(h1, norm_w[1:2], du1, dh2, "rms1_bwd")

    g_mi = _mlstm_grad_layout(g_p1, dk, dv, "mlstm_g_layout")

    dg0 =_matmul(dh1, w_co, form="nt", m=t, n=e, kdim=d, tm=tm, tn=_pick(e, 2048, LANES), tk=d, out_dtype=BF16,
                  name="conv_out_dx")
    g_co = _matmul(g0, dh1, form="tn", m=e, n=d, kdim=t, tm=_pick(e, 1024, LANES), tn=d, tk=tkw, out_dtype=BF16,
                   name="conv_out_dw")
    (dp0, dconv), (r_co,) = _conv_gate_bwd(p0, y0, dg0, conv_w_full, "conv_gate_bwd",
                                           exchange=_scatter_exchange([g_co.reshape(N_DEV, e // N_DEV, d)]))
    g_ci, (r_mi,) = _matmul(
        u0, dp0, form="tn", m=d, n=8 * ec, kdim=t, tm=d, tn=ec, tk=tkw, out_dtype=BF16, name="conv_in_dw",
        out_shape=(N_DEV, d, ec), out_spec=pl.BlockSpec((None, d, ec), lambda i, j, k: (ci_map(j), 0, 0)),
        exchange=_scatter_exchange([g_mi]))
    du0, (r_ci,) = _matmul(
        dp0, w_ci, form="nt", m=t, n=d, kdim=8 * ec, tm=tm, tn=d, tk=ec, out_dtype=F32, name="conv_in_dx",
        b_spec=pl.BlockSpec((None, d, ec), lambda i, j, k: (ci_map(k), 0, 0)),
        exchange=_scatter_exchange([g_ci]))
    grad_x, dmeta, dnw0 = _rms_bwd_first(h0, norm_w[0:1], du0, dh1, "rms0_bwd")
    grad_x = grad_x[None]

    row8 = lax.broadcasted_iota(jnp.int32, (8, 1), 0)
    loss_wide = jnp.pad(loss_part, ((0, 0), (0, d - LANES)))
    payload = jnp.concatenate([
        jnp.where(row8 == 0, dnw0, jnp.where(row8 == 1, dnw1, 0.0)),
        jnp.where(row8 == 0, dwf, jnp.where(row8 == 1, loss_wide, 0.0)),
        jnp.where(row8 == 0, jnp.pad(dbias, ((0, 0), (0, d - LANES))), 0.0),
        dmeta,
        _pad_rows(dconv[0:3].reshape(3 * e // d, d), 8),
        _pad_rows(dhead[:, 0, :].reshape(e // d, d), 8),
    ], axis=0)
    (payload_g,) = _run_exchange(_gather_exchange([payload]), "gather_small_grads")
    tot = _sum_devices(payload_g, "sum_small_grads")

    loss = tot[9, 0]
    g_norm = tot[0:2]
    g_final = tot[8]
    g_gate_b = tot[16:17, 0:2 * nh]
    g_meta = lax.dynamic_slice(tot[24:24 + N_META], (0, me * (d // N_DEV)), (N_META, d // N_DEV))
    g_conv_w = lax.dynamic_slice(tot[40:40 + 3 * e // d].reshape(3, e), (0, me * (e // N_DEV)), (3, e // N_DEV))
    g_head = lax.dynamic_slice(tot[48:48 + e // d].reshape(1, e), (0, me * (e // N_DEV)), (1, e // N_DEV))

    g1, d1, nm1, nv1 = _adamw_sharded(r_ci, conv_in_w[0], m_conv_in_w[0], v_conv_in_w[0], "adamw_conv_in")
    g2, d2, nm2, nv2 = _adamw_sharded(r_co, conv_out_w[0], m_conv_out_w[0], v_conv_out_w[0], "adamw_conv_out")
    g3, d3, nm3, nv3 = _adamw_sharded(r_mi, mlstm_in_w[0], m_mlstm_in_w[0], v_mlstm_in_w[0], "adamw_mlstm_in")
    g4, d4, nm4, nv4 = _adamw_sharded(r_mo, mlstm_out_w[0], m_mlstm_out_w[0], v_mlstm_out_w[0], "adamw_mlstm_out")

    small_g = [g_meta, g_norm, g_conv_w, g_gate_b, g_head, g_final.reshape(1, d)]
    small_w = [meta_tokens, norm_w, conv_w[0], mlstm_gate_b, mlstm_head_norm_w, final_norm_w.reshape(1, d)]
    small_m = [m_meta_tokens, m_norm_w, m_conv_w[0], m_mlstm_gate_b, m_mlstm_head_norm_w, m_final_norm_w.reshape(1, d)]
    small_v = [v_meta_tokens, v_norm_w, v_conv_w[0], v_mlstm_gate_b, v_mlstm_head_norm_w, v_final_norm_w.reshape(1, d)]
    sd, snm, snv = _adamw_small(small_g, small_w, small_m, small_v, "adamw_small")

    def order(meta, norm, cin, cw, cout, min_, gb, hn, mout, fin):
        return (meta, norm, cin[None], cw[None], cout[None], min_[None], gb, hn, mout[None], fin.reshape(d))

    grads = order(g_meta, g_norm, g1, g_conv_w, g2, g3, g_gate_b, g_head, g4, g_final)
    deltas = order(sd[0], sd[1], d1, sd[2], d2, d3, sd[3], sd[4], d4, sd[5])
    new_m = order(snm[0], snm[1], nm1, snm[2], nm2, nm3, snm[3], snm[4], nm4, snm[5])
    new_v = order(snv[0], snv[1], nv1, snv[2], nv2, nv3, snv[3], snv[4], nv4, snv[5])
    return (loss, grad_x, *grads, *deltas, *new_m, *new_v)
```

```python
import functools

import jax
import jax.numpy as jnp
from jax import lax
from jax.experimental import pallas as pl
from jax.experimental.pallas import tpu as pltpu

F32 = jnp.float32
BF16 = jnp.bfloat16
MESH = pl.DeviceIdType.MESH

N_DEV = 8
N_META = 16
N_HEADS = 4
CHUNK = 64
CHUNKS_PER_STEP = 2
LEAD = 128
PAD_ROWS = LEAD - N_META
RMS_EPS = 1e-6
NEG = -1e30
LANES = 128
VMEM_LIMIT = 48 * 1024 * 1024

ADAM_LR = 0.001
ADAM_B1 = 0.9
ADAM_B2 = 0.999
ADAM_EPS = 1e-08
ADAM_WD = 0.01
ADAM_STEP = 10

HIGHEST = lax.Precision.HIGHEST


def _pick(n, target, mult):
    best = None
    for d in range(mult, min(n, target) + 1, mult):
        if n % d == 0:
            best = d
    return n if best is None else best


def _params(**kw):
    return pltpu.CompilerParams(vmem_limit_bytes=VMEM_LIMIT, **kw)


def _sigmoid(x):
    return 0.5 * jnp.tanh(0.5 * x) + 0.5


STRIP_ROWS = 16
STRIP_COLS = 256


def _strips(n_rows, n_cols):
    for c0 in range(0, n_cols, STRIP_COLS):
        for r0 in range(0, n_rows, STRIP_ROWS):
            yield slice(r0, r0 + STRIP_ROWS), slice(c0, min(c0 + STRIP_COLS, n_cols))


def _shift(sl, by):
    return slice(sl.start + by, sl.stop + by)


def _call(body, *, name, grid, in_specs, out_specs, out_shape, args, scratch_shapes=(), aliases=None,
          exchange=None):
    aliases = {} if aliases is None else aliases
    if exchange is None:
        return pl.pallas_call(
            body, name=name, grid=grid, in_specs=list(in_specs), out_specs=out_specs, out_shape=out_shape,
            scratch_shapes=list(scratch_shapes), input_output_aliases=aliases,
            compiler_params=_params())(*args)
    ex = exchange
    single = not isinstance(out_shape, (list, tuple))
    shapes = [out_shape] if single else list(out_shape)
    specs = [out_specs] if single else list(out_specs)
    n_in, n_out, n_scr = len(in_specs), len(shapes), len(scratch_shapes)
    n_ex_in, n_ex_out = len(ex.ins), len(ex.out_shapes)
    steps = 1
    for size in grid:
        steps *= size

    def wrapped(*refs):
        own_in, ex_in = refs[:n_in], refs[n_in:n_in + n_ex_in]
        at = n_in + n_ex_in
        own_out, ex_out = refs[at:at + n_out], refs[at + n_out:at + n_out + n_ex_out]
        at += n_out + n_ex_out
        own_scr, ex_scr = refs[at:at + n_scr], refs[at + n_scr:]
        step = 0
        for axis, size in enumerate(grid):
            step = step * size + pl.program_id(axis)

        @pl.when(step == 0)
        def _():
            ex.start(ex_in, ex_out, ex_scr)

        if ex.mid is not None:
            @pl.when(step == (3 * steps) // 4)
            def _():
                ex.mid(ex_in, ex_out, ex_scr)

        body(*own_in, *own_out, *own_scr)

        @pl.when(step == steps - 1)
        def _():
            ex.finish(ex_in, ex_out, ex_scr)

    any_spec = pl.BlockSpec(memory_space=pl.ANY)
    res = pl.pallas_call(
        wrapped, name=name, grid=grid,
        in_specs=list(in_specs) + [any_spec] * n_ex_in, out_specs=specs + [any_spec] * n_ex_out,
        out_shape=shapes + ex.out_shapes, scratch_shapes=list(scratch_shapes) + ex.scratch,
        input_output_aliases=aliases, compiler_params=_params())(*args, *ex.ins)
    return (res[0] if single else res[:n_out]), res[n_out:]


def _matmul(a, b, *, form, m, n, kdim, tm, tn, tk, out_dtype, name,
            a_spec=None, b_spec=None, out_spec=None, out_shape=None, residual=None, exchange=None):
    ni, nj, nk = m // tm, n // tn, kdim // tk
    assert ni * tm == m and nj * tn == n and nk * tk == kdim, (name, m, n, kdim, tm, tn, tk)
    if form == "nn":
        dn = (((1,), (0,)), ((), ()))
        a_def = pl.BlockSpec((tm, tk), lambda i, j, k: (i, k))
        b_def = pl.BlockSpec((tk, tn), lambda i, j, k: (k, j))
    elif form == "nt":
        dn = (((1,), (1,)), ((), ()))
        a_def = pl.BlockSpec((tm, tk), lambda i, j, k: (i, k))
        b_def = pl.BlockSpec((tn, tk), lambda i, j, k: (j, k))
    else:
        dn = (((0,), (0,)), ((), ()))
        a_def = pl.BlockSpec((tk, tm), lambda i, j, k: (k, i))
        b_def = pl.BlockSpec((tk, tn), lambda i, j, k: (k, j))
    a_spec = a_def if a_spec is None else a_spec
    b_spec = b_def if b_spec is None else b_spec
    o_spec = pl.BlockSpec((tm, tn), lambda i, j, k: (i, j)) if out_spec is None else out_spec
    has_res = residual is not None

    def body(*refs):
        a_ref, b_ref = refs[:2]
        r_ref = refs[2] if has_res else None
        o_ref = refs[2 + has_res]

        def product():
            return lax.dot_general(a_ref[...].astype(BF16), b_ref[...].astype(BF16), dn,
                                   preferred_element_type=F32)

        def finish(acc):
            if has_res:
                acc = acc + r_ref[...].astype(F32)
            o_ref[...] = acc.astype(o_ref.dtype)

        if nk == 1:
            finish(product())
        else:
            acc_ref = refs[3 + has_res]
            k = pl.program_id(2)

            @pl.when(k == 0)
            def _():
                acc_ref[...] = jnp.zeros_like(acc_ref)

            acc_ref[...] += product()

            @pl.when(k == nk - 1)
            def _():
                finish(acc_ref[...])

    in_specs = [a_spec, b_spec]
    args = [a, b]
    if has_res:
        in_specs.append(pl.BlockSpec((tm, tn), lambda i, j, k: (i, j)))
        args.append(residual)
    return _call(
        body, name=name, grid=(ni, nj, nk), in_specs=in_specs, out_specs=o_spec,
        out_shape=jax.ShapeDtypeStruct((m, n) if out_shape is None else out_shape, out_dtype),
        scratch_shapes=[] if nk == 1 else [pltpu.VMEM((tm, tn), F32)], args=args, exchange=exchange)


def _rms_fwd(h, w, name, exchange=None):
    t, d = h.shape
    tm = _pick(t, 416, 16)

    def body(h_ref, w_ref, u_ref):
        x = h_ref[...]
        r = lax.rsqrt(jnp.mean(x * x, axis=-1, keepdims=True) + RMS_EPS)
        u_ref[...] = ((x * r) * w_ref[...]).astype(BF16)

    return _call(
        body, name=name, grid=(t // tm,),
        in_specs=[pl.BlockSpec((tm, d), lambda i: (i, 0)), pl.BlockSpec((1, d), lambda i: (0, 0))],
        out_specs=pl.BlockSpec((tm, d), lambda i: (i, 0)),
        out_shape=jax.ShapeDtypeStruct((t, d), BF16), args=(h, w), exchange=exchange)


def _rms_first(x, meta, w, name, exchange=None):
    seq, d = x.shape
    t = seq + LEAD
    tm = LEAD

    def body(x_ref, meta_ref, w_ref, h_ref, u_ref):
        i = pl.program_id(0)

        @pl.when(i == 0)
        def _():
            h_ref[0:PAD_ROWS, :] = jnp.zeros((PAD_ROWS, d), F32)
            h_ref[PAD_ROWS:LEAD, :] = meta_ref[...]

        @pl.when(i > 0)
        def _():
            h_ref[...] = x_ref[...]

        hv = h_ref[...]
        r = lax.rsqrt(jnp.mean(hv * hv, axis=-1, keepdims=True) + RMS_EPS)
        u_ref[...] = ((hv * r) * w_ref[...]).astype(BF16)

    row = pl.BlockSpec((tm, d), lambda i: (i, 0))
    return _call(
        body, name=name, grid=(t // tm,),
        in_specs=[pl.BlockSpec((tm, d), lambda i: (jnp.maximum(i - 1, 0), 0)),
                  pl.BlockSpec((N_META, d), lambda i: (0, 0)), pl.BlockSpec((1, d), lambda i: (0, 0))],
        out_specs=[row, row],
        out_shape=[jax.ShapeDtypeStruct((t, d), F32), jax.ShapeDtypeStruct((t, d), BF16)],
        args=(x, meta, w), exchange=exchange)


def _rms_bwd(h, w, du, dres, name):
    t, d = h.shape
    tm = _pick(t, 416, 16)

    def body(h_ref, w_ref, du_ref, dres_ref, dh_ref, dw_ref):
        i = pl.program_id(0)
        x = h_ref[...]
        g = du_ref[...].astype(F32)
        r = lax.rsqrt(jnp.mean(x * x, axis=-1, keepdims=True) + RMS_EPS)
        gw = g * w_ref[...]
        dot = jnp.mean(gw * x, axis=-1, keepdims=True)
        dh_ref[...] = dres_ref[...] + (r * gw - x * ((r * r * r) * dot))
        part = jnp.sum(g * (x * r), axis=0, keepdims=True)

        @pl.when(i == 0)
        def _():
            dw_ref[...] = jnp.zeros_like(dw_ref)

        dw_ref[...] += jnp.broadcast_to(part, dw_ref.shape)

    row = pl.BlockSpec((tm, d), lambda i: (i, 0))
    return pl.pallas_call(
        body, name=name, grid=(t // tm,),
        in_specs=[row, pl.BlockSpec((1, d), lambda i: (0, 0)), row, row],
        out_specs=[row, pl.BlockSpec((8, d), lambda i: (0, 0))],
        out_shape=[jax.ShapeDtypeStruct((t, d), F32), jax.ShapeDtypeStruct((8, d), F32)],
        compiler_params=_params(),
    )(h, w, du, dres)


def _rms_bwd_first(h, w, du, dres, name):
    t, d = h.shape
    tm = LEAD

    def body(h_ref, w_ref, du_ref, dres_ref, gx_ref, dmeta_ref, dw_ref):
        i = pl.program_id(0)
        x = h_ref[...]
        g = du_ref[...].astype(F32)
        r = lax.rsqrt(jnp.mean(x * x, axis=-1, keepdims=True) + RMS_EPS)
        gw = g * w_ref[...]
        dot = jnp.mean(gw * x, axis=-1, keepdims=True)
        dh = dres_ref[...] + (r * gw - x * ((r * r * r) * dot))
        part = jnp.sum(g * (x * r), axis=0, keepdims=True)

        @pl.when(i == 0)
        def _():
            dw_ref[...] = jnp.zeros_like(dw_ref)
            dmeta_ref[...] = dh[PAD_ROWS:LEAD, :]

        @pl.when(i > 0)
        def _():
            gx_ref[...] = dh

        dw_ref[...] += jnp.broadcast_to(part, dw_ref.shape)

    row = pl.BlockSpec((tm, d), lambda i: (i, 0))
    return pl.pallas_call(
        body, name=name, grid=(t // tm,),
        in_specs=[row, pl.BlockSpec((1, d), lambda i: (0, 0)), row, row],
        out_specs=[pl.BlockSpec((tm, d), lambda i: (jnp.maximum(i - 1, 0), 0)),
                   pl.BlockSpec((N_META, d), lambda i: (0, 0)), pl.BlockSpec((8, d), lambda i: (0, 0))],
        out_shape=[jax.ShapeDtypeStruct((t - LEAD, d), F32), jax.ShapeDtypeStruct((N_META, d), F32),
                   jax.ShapeDtypeStruct((8, d), F32)],
        compiler_params=_params(),
    )(h, w, du, dres)


def _final_loss(h, w, tgt, name):
    t, d = h.shape
    sub = LEAD
    per = _pick(t // sub, 5, 1)
    tm = per * sub

    def body(h_ref, w_ref, *rest):
        t_refs, (dh_ref, dw_ref, loss_ref) = rest[:per], rest[per:]
        i = pl.program_id(0)

        @pl.when(i == 0)
        def _():
            dw_ref[...] = jnp.zeros_like(dw_ref)
            loss_ref[...] = jnp.zeros_like(loss_ref)

        for q in range(per):
            sl = slice(q * sub, (q + 1) * sub)
            x = h_ref[sl, :]
            r = lax.rsqrt(jnp.mean(x * x, axis=-1, keepdims=True) + RMS_EPS)
            yn = x * r
            y = yn * w_ref[...]
            rows = i * tm + q * sub + lax.broadcasted_iota(jnp.int32, (sub, 1), 0)
            diff = jnp.where(rows >= LEAD, y - t_refs[q][...], 0.0)
            tile_loss = 0.5 * jnp.sum(jnp.mean(diff * diff, axis=-1, keepdims=True), axis=0, keepdims=True)
            dy = diff / d
            gw = dy * w_ref[...]
            dot = jnp.mean(gw * x, axis=-1, keepdims=True)
            dh_ref[sl, :] = r * gw - x * ((r * r * r) * dot)
            dw_ref[...] += jnp.broadcast_to(jnp.sum(dy * yn, axis=0, keepdims=True), dw_ref.shape)
            loss_ref[...] += jnp.broadcast_to(tile_loss, loss_ref.shape)

    row = pl.BlockSpec((tm, d), lambda i: (i, 0))
    piece = [pl.BlockSpec((sub, d), functools.partial(lambda i, q: (jnp.maximum(i * per + q - 1, 0), 0), q=q))
             for q in range(per)]
    return pl.pallas_call(
        body, name=name, grid=(t // tm,),
        in_specs=[row, pl.BlockSpec((1, d), lambda i: (0, 0))] + piece,
        out_specs=[row, pl.BlockSpec((8, d), lambda i: (0, 0)), pl.BlockSpec((8, LANES), lambda i: (0, 0))],
        out_shape=[jax.ShapeDtypeStruct((t, d), F32), jax.ShapeDtypeStruct((8, d), F32),
                   jax.ShapeDtypeStruct((8, LANES), F32)],
        compiler_params=_params(),
    )(h, w, *([tgt] * per))


def _conv_gate_fwd(p0, conv_w, name, exchange=None):
    t = p0.shape[0]
    ec = p0.shape[1] // 8
    tm = _pick(t, 416, 16)
    nt = t // tm

    def body(p_ref, w_ref, g_ref, y_ref, ext_ref):
        i = pl.program_id(1)

        @pl.when(i == 0)
        def _():
            ext_ref[0:8, :] = jnp.zeros((8, ec), F32)

        for rows, cols in _strips(tm, ec):
            cg = p_ref[rows, _shift(cols, ec)].astype(F32)
            xin = p_ref[rows, _shift(cols, 2 * ec)].astype(F32)
            ext_ref[_shift(rows, 8), cols] = cg * xin
        for rows, cols in _strips(tm, ec):
            y = (w_ref[0:1, cols] * ext_ref[_shift(rows, 6), cols] + w_ref[1:2, cols] * ext_ref[_shift(rows, 7), cols]
                 + w_ref[2:3, cols] * ext_ref[_shift(rows, 8), cols])
            bg = p_ref[rows, cols].astype(F32)
            z = p_ref[rows, _shift(cols, 3 * ec)].astype(F32)
            y_ref[rows, cols] = y.astype(BF16)
            g_ref[rows, cols] = ((z * _sigmoid(z)) * bg * y).astype(BF16)
        ext_ref[0:8, :] = ext_ref[tm:tm + 8, :]

    out = pl.BlockSpec((tm, ec), lambda j, i: (i, j))
    return _call(
        body, name=name, grid=(2, nt),
        in_specs=[pl.BlockSpec((tm, 4 * ec), lambda j, i: (i, j)), pl.BlockSpec((3, ec), lambda j, i: (0, j))],
        out_specs=[out, out],
        out_shape=[jax.ShapeDtypeStruct((t, 2 * ec), BF16)] * 2,
        scratch_shapes=[pltpu.VMEM((tm + 8, ec), F32)], args=(p0, conv_w), exchange=exchange)


def _conv_gate_bwd(p0, y, dg, conv_w, name, exchange=None):
    t = p0.shape[0]
    ec = p0.shape[1] // 8
    tm = _pick(t, 208, 16)
    nt = t // tm

    def body(p_ref, y_ref, dg_ref, w_ref, dp_ref, dw_ref, ext_ref):
        i = pl.program_id(1)

        @pl.when(i == 0)
        def _():
            ext_ref[tm:tm + 8, :] = jnp.zeros((8, ec), F32)
            dw_ref[...] = jnp.zeros_like(dw_ref)

        for rows, cols in _strips(tm, ec):
            bg = p_ref[rows, cols].astype(F32)
            z = p_ref[rows, _shift(cols, 3 * ec)].astype(F32)
            yv = y_ref[rows, cols].astype(F32)
            dgv = dg_ref[rows, cols].astype(F32)
            sig = _sigmoid(z)
            sz = z * sig
            dp_ref[rows, _shift(cols, 3 * ec)] = (dgv * bg * yv * (sig * (1.0 + z * (1.0 - sig)))).astype(BF16)
            dp_ref[rows, cols] = (dgv * sz * yv).astype(BF16)
            ext_ref[rows, cols] = dgv * sz * bg
        acc = None
        for rows, cols in _strips(tm, ec):
            if rows.start == 0:
                acc = [jnp.zeros((STRIP_ROWS, cols.stop - cols.start), F32) for _ in range(3)]
            dy = ext_ref[rows, cols]
            dy1 = ext_ref[_shift(rows, 1), cols]
            dy2 = ext_ref[_shift(rows, 2), cols]
            cg = p_ref[rows, _shift(cols, ec)].astype(F32)
            xin = p_ref[rows, _shift(cols, 2 * ec)].astype(F32)
            da = w_ref[0:1, cols] * dy2 + w_ref[1:2, cols] * dy1 + w_ref[2:3, cols] * dy
            dp_ref[rows, _shift(cols, ec)] = (da * xin).astype(BF16)
            dp_ref[rows, _shift(cols, 2 * ec)] = (da * cg).astype(BF16)
            a = cg * xin
            acc = [acc[0] + a * dy2, acc[1] + a * dy1, acc[2] + a * dy]
            if rows.stop == tm:
                for tap in range(3):
                    dw_ref[tap:tap + 1, cols] += jnp.sum(acc[tap], axis=0, keepdims=True)
        ext_ref[tm:tm + 8, :] = ext_ref[0:8, :]

    rev = lambda j, i: (nt - 1 - i, j)
    return _call(
        body, name=name, grid=(2, nt),
        in_specs=[pl.BlockSpec((tm, 4 * ec), rev), pl.BlockSpec((tm, ec), rev), pl.BlockSpec((tm, ec), rev),
                  pl.BlockSpec((3, ec), lambda j, i: (0, j))],
        out_specs=[pl.BlockSpec((tm, 4 * ec), rev), pl.BlockSpec((8, ec), lambda j, i: (0, j))],
        out_shape=[jax.ShapeDtypeStruct((t, 8 * ec), BF16), jax.ShapeDtypeStruct((8, 2 * ec), F32)],
        scratch_shapes=[pltpu.VMEM((tm + 8, ec), F32)], args=(p0, y, dg, conv_w), exchange=exchange)


def _log_sigmoid(x):
    return jnp.minimum(x, 0.0) - jnp.log(1.0 + jnp.exp(-jnp.abs(x)))


def _gates_fwd(graw, graw_t, bias_row, bias_col, name):
    t = graw.shape[0]
    tm = _pick(t, 640, 128)
    cpt = tm // CHUNK
    nh = N_HEADS

    def body(g_ref, gt_ref, br_ref, bc_ref, colf_ref, rowf_ref):
        i = pl.program_id(0)
        gc = g_ref[...] + br_ref[...]
        rows = i * tm + lax.broadcasted_iota(jnp.int32, (tm, 1), 0)
        live = rows >= PAD_ROWS
        lf = jnp.where(live, _log_sigmoid(gc), 0.0)
        li = jnp.where(live, gc, NEG)
        gt = gt_ref[...] + bc_ref[...]
        cols = i * tm + lax.broadcasted_iota(jnp.int32, (1, tm), 1)
        live_t = cols >= PAD_ROWS
        lf_t = jnp.where(live_t, _log_sigmoid(gt), 0.0)
        li_t = jnp.where(live_t, gt, NEG)
        ri = lax.broadcasted_iota(jnp.int32, (CHUNK, CHUNK), 0)
        ci = lax.broadcasted_iota(jnp.int32, (CHUNK, CHUNK), 1)
        lower = (ri >= ci).astype(F32)
        upper = (ri <= ci).astype(F32)
        lane = lax.broadcasted_iota(jnp.int32, (CHUNK, LANES), 1)
        sub = lax.broadcasted_iota(jnp.int32, (8, CHUNK), 0)
        for c in range(cpt):
            sl = slice(c * CHUNK, (c + 1) * CHUNK)
            b_all = jnp.dot(lower, lf[sl, :], precision=HIGHEST, preferred_element_type=F32)
            bt_all = jnp.dot(lf_t[:, sl], upper, precision=HIGHEST, preferred_element_type=F32)
            for h in range(nh):
                b = b_all[:, nh + h:nh + h + 1]
                r = li[sl, h:h + 1] - b
                pre = gc[sl, nh + h:nh + h + 1]
                colf_ref[h, sl, :] = jnp.where(lane == 0, b, jnp.where(lane == 1, r, jnp.where(lane == 2, pre, 0.0)))
                b_row = bt_all[nh + h:nh + h + 1, :]
                r_row = li_t[h:h + 1, sl] - b_row
                rowf_ref[h, c, :, 0:CHUNK] = jnp.where(sub == 0, r_row, jnp.where(sub == 1, b_row, 0.0))
                rowf_ref[h, c, :, CHUNK:LANES] = jnp.zeros((8, LANES - CHUNK), F32)

    return pl.pallas_call(
        body, name=name, grid=(t // tm,),
        in_specs=[pl.BlockSpec((tm, LANES), lambda i: (i, 0)), pl.BlockSpec((16, tm), lambda i: (0, i)),
                  pl.BlockSpec((1, LANES), lambda i: (0, 0)), pl.BlockSpec((16, 1), lambda i: (0, 0))],
        out_specs=[pl.BlockSpec((nh, tm, LANES), lambda i: (0, i, 0)),
                   pl.BlockSpec((nh, cpt, 8, LANES), lambda i: (0, i, 0, 0))],
        out_shape=[jax.ShapeDtypeStruct((nh, t, LANES), F32),
                   jax.ShapeDtypeStruct((nh, t // CHUNK, 8, LANES), F32)],
        compiler_params=_params(),
    )(graw, graw_t, bias_row, bias_col)


def _gates_bwd(gstat, colf, dp1, name):
    nh, t, _ = gstat.shape
    tm = _pick(t, 640, 128)
    cpt = tm // CHUNK
    nt = t // tm
    nc = t // CHUNK
    gate_block = dp1.shape[1] // LANES - 1

    def body(gs_ref, nx_ref, colf_ref, dp_any, dg_ref, db_ref):
        i = pl.program_id(0)
        ri = lax.broadcasted_iota(jnp.int32, (CHUNK, CHUNK), 0)
        ci = lax.broadcasted_iota(jnp.int32, (CHUNK, CHUNK), 1)
        upper = (ri <= ci).astype(F32)
        lane = lax.broadcasted_iota(jnp.int32, (CHUNK, LANES), 1)
        total = jnp.zeros((1, LANES), F32)
        for c in range(cpt):
            sl = slice(c * CHUNK, (c + 1) * CHUNK)
            rows = i * tm + c * CHUNK + lax.broadcasted_iota(jnp.int32, (CHUNK, 1), 0)
            live = rows >= PAD_ROWS
            acc = jnp.zeros((CHUNK, LANES), F32)
            for h in range(nh):
                blk = gs_ref[h, sl, :]
                rev = jnp.dot(upper, blk, precision=HIGHEST, preferred_element_type=F32)
                if c + 1 < cpt:
                    carry = gs_ref[h, (c + 1) * CHUNK:(c + 1) * CHUNK + 1, 2:3]
                else:
                    carry = jnp.where(i == nt - 1, 0.0, nx_ref[h, 0:1, 2:3])
                dlogf = rev[:, 0:1] + carry
                pre = colf_ref[h, sl, 2:3]
                dgf = jnp.where(live, dlogf * (1.0 - _sigmoid(pre)), 0.0)
                dgi = jnp.where(live, blk[:, 1:2], 0.0)
                acc = acc + jnp.where(lane == h, dgi, 0.0) + jnp.where(lane == nh + h, dgf, 0.0)
            dg_ref[sl, :] = acc.astype(BF16)
            total = total + jnp.sum(acc, axis=0, keepdims=True)

        @pl.when(i == 0)
        def _():
            db_ref[...] = jnp.zeros_like(db_ref)

        db_ref[...] += jnp.broadcast_to(total, db_ref.shape)

    return pl.pallas_call(
        body, name=name, grid=(nt,),
        in_specs=[pl.BlockSpec((nh, tm, LANES), lambda i: (0, i, 0)),
                  pl.BlockSpec((nh, CHUNK, LANES), lambda i: (0, jnp.minimum((i + 1) * cpt, nc - 1), 0)),
                  pl.BlockSpec((nh, tm, LANES), lambda i: (0, i, 0)),
                  pl.BlockSpec(memory_space=pl.ANY)],
        out_specs=[pl.BlockSpec((tm, LANES), lambda i: (i, gate_block)), pl.BlockSpec((8, LANES), lambda i: (0, 0))],
        out_shape=[jax.ShapeDtypeStruct(dp1.shape, dp1.dtype), jax.ShapeDtypeStruct((8, LANES), F32)],
        input_output_aliases={3: 0},
        compiler_params=_params(),
    )(gstat, gstat, colf, dp1)


NT_DIMS = (((1,), (1,)), ((), ()))
TN_DIMS = (((0,), (0,)), ((), ()))


def _dot(a, b):
    return jnp.dot(a.astype(BF16), b.astype(BF16), preferred_element_type=F32)


def _dot_nt(a, b):
    return lax.dot_general(a.astype(BF16), b.astype(BF16), NT_DIMS, preferred_element_type=F32)


def _dot_tn(a, b):
    return lax.dot_general(a.astype(BF16), b.astype(BF16), TN_DIMS, preferred_element_type=F32)


def _chunk_gates(colf_ref, rowf_ref, m_prev, width=CHUNK):
    b = colf_ref[:, 0:1]
    rcol = colf_ref[:, 1:2]
    rrow = rowf_ref[0:1, 0:width]
    ri = lax.broadcasted_iota(jnp.int32, (CHUNK, width), 0)
    ci = lax.broadcasted_iota(jnp.int32, (CHUNK, width), 1)
    log_d = jnp.where(ri >= ci, b + rrow, NEG)
    m_row = jnp.maximum(b + m_prev, jnp.max(log_d, axis=-1, keepdims=True))
    dmat = jnp.exp(log_d - m_row)
    inter = jnp.exp(b + m_prev - m_row)
    b_last = b[CHUNK - 1:CHUNK, :]
    log_w = rcol + b_last
    m_new = jnp.maximum(b_last + m_prev, jnp.max(log_w, axis=0, keepdims=True))
    decay = jnp.exp(b_last + m_prev - m_new)
    w = jnp.exp(log_w - m_new)
    return m_row, dmat, inter, m_new, decay, w


def _mlstm_fwd(p1, colf, rowf, dk, dv, name):
    t = p1.shape[0]
    nh = N_HEADS
    nc = t // CHUNK
    hw = 2 * dk + dv
    cps = CHUNKS_PER_STEP if nc % CHUNKS_PER_STEP == 0 else 1
    rows_per_step = cps * CHUNK

    def body(p_ref, colf_ref, rowf_ref, hh_ref, stat_ref, cs_ref, ns_ref, *state):
        c_refs, n_refs, m_refs = state[:nh], state[nh:2 * nh], state[2 * nh:]
        c = pl.program_id(0)

        @pl.when(c == 0)
        def _():
            for ref in state:
                ref[...] = jnp.zeros_like(ref)

        for cc in range(cps):
            rows = pl.ds(cc * CHUNK, CHUNK)
            for h in range(nh):
                head(p_ref.at[rows, pl.ds(h * hw, hw)], colf_ref.at[h, rows], rowf_ref.at[h, cc],
                     hh_ref.at[rows, pl.ds(h * dv, dv)], stat_ref.at[h, rows], cs_ref.at[h, cc],
                     ns_ref.at[h, cc], c_refs[h], n_refs[h], m_refs[h])

    def head(p_ref, colf_ref, rowf_ref, hh_ref, stat_ref, cs_ref, ns_ref, c_ref, n_ref, m_ref):
        m_prev = m_ref[...]
        n_prev = n_ref[...]
        c_prev = c_ref[...]
        cs_ref[...] = c_prev.astype(BF16)
        sub = lax.broadcasted_iota(jnp.int32, (8, dk), 0)
        ns_ref[...] = jnp.where(sub == 0, n_prev, jnp.where(sub == 1, m_prev, 0.0))

        q = p_ref[:, 0:dk]
        k = p_ref[:, dk:2 * dk]
        v = p_ref[:, 2 * dk:2 * dk + dv]
        m_row, dmat, inter, m_new, decay, w = _chunk_gates(colf_ref, rowf_ref, m_prev)
        s = _dot_nt(q, k) * dmat
        num = _dot(s, v) + inter * _dot(q, c_prev)
        den = jnp.sum(s, axis=-1, keepdims=True) + inter * jnp.sum(q.astype(F32) * n_prev, axis=-1, keepdims=True)
        denom = jnp.maximum(jnp.abs(den), jnp.exp(-m_row))
        hh_ref[...] = num * (1.0 / denom)
        lane = lax.broadcasted_iota(jnp.int32, (CHUNK, LANES), 1)
        stat_ref[...] = jnp.where(lane == 0, den, 0.0)

        wk = w * k.astype(F32)
        c_ref[...] = decay * c_prev + _dot_tn(wk, v)
        n_ref[...] = decay * n_prev + jnp.sum(wk, axis=0, keepdims=True)
        m_ref[...] = m_new

    return pl.pallas_call(
        body, name=name, grid=(nc // cps,),
        in_specs=[pl.BlockSpec((rows_per_step, nh * hw), lambda c: (c, 0)),
                  pl.BlockSpec((nh, rows_per_step, LANES), lambda c: (0, c, 0)),
                  pl.BlockSpec((nh, cps, 8, LANES), lambda c: (0, c, 0, 0))],
        out_specs=[pl.BlockSpec((rows_per_step, nh * dv), lambda c: (c, 0)),
                   pl.BlockSpec((nh, rows_per_step, LANES), lambda c: (0, c, 0)),
                   pl.BlockSpec((nh, cps, dk, dv), lambda c: (0, c, 0, 0)),
                   pl.BlockSpec((nh, cps, 8, dk), lambda c: (0, c, 0, 0))],
        out_shape=[jax.ShapeDtypeStruct((t, nh * dv), F32), jax.ShapeDtypeStruct((nh, t, LANES), F32),
                   jax.ShapeDtypeStruct((nh, nc, dk, dv), BF16), jax.ShapeDtypeStruct((nh, nc, 8, dk), F32)],
        scratch_shapes=([pltpu.VMEM((dk, dv), F32)] * nh + [pltpu.VMEM((1, dk), F32)] * nh
                        + [pltpu.VMEM((1, 1), F32)] * nh),
        compiler_params=_params(),
    )(p1, colf, rowf)


def _head_gate_fwd(hh, p1, head_w, dv, name):
    t = hh.shape[0]
    nh = N_HEADS
    e = nh * dv
    tm = _pick(t, 416, 16)

    def body(hh_ref, oz_ref, w_ref, hm_ref):
        for h in range(nh):
            cols = slice(h * dv, (h + 1) * dv)
            x = hh_ref[:, cols]
            o = oz_ref[:, 2 * h * dv:(2 * h + 1) * dv].astype(F32)
            z = oz_ref[:, (2 * h + 1) * dv:(2 * h + 2) * dv].astype(F32)
            r = lax.rsqrt(jnp.mean(x * x, axis=-1, keepdims=True) + RMS_EPS)
            hn = (x * r) * w_ref[:, cols]
            hm_ref[:, cols] = (hn * _sigmoid(o) * (z * _sigmoid(z))).astype(BF16)

    return pl.pallas_call(
        body, name=name, grid=(t // tm,),
        in_specs=[pl.BlockSpec((tm, e), lambda i: (i, 0)), pl.BlockSpec((tm, 2 * e), lambda i: (i, 1)),
                  pl.BlockSpec((1, e), lambda i: (0, 0))],
        out_specs=pl.BlockSpec((tm, e), lambda i: (i, 0)),
        out_shape=jax.ShapeDtypeStruct((t, e), BF16),
        compiler_params=_params(),
    )(hh, p1, head_w)


def _mlstm_bwd(p1, colf, rowf, head_w, hh, stat, csave, nsave, dhm, n_cols, dk, dv, name, exchange=None):
    t = p1.shape[0]
    nh = N_HEADS
    nc = t // CHUNK
    hw = 2 * dk + dv
    cps = CHUNKS_PER_STEP if nc % CHUNKS_PER_STEP == 0 else 1
    rows_per_step = cps * CHUNK

    def body(p_ref, colf_ref, rowf_ref, w_ref, hh_ref, stat_ref, cs_ref, ns_ref, dhm_ref,
             dp_ref, gs_ref, dw_ref, *state):
        dc_refs, dn_refs, dwacc_refs = state[:nh], state[nh:2 * nh], state[2 * nh:]
        c = pl.program_id(0)

        @pl.when(c == 0)
        def _():
            for ref in state:
                ref[...] = jnp.zeros_like(ref)

        for cc in reversed(range(cps)):
            rows = pl.ds(cc * CHUNK, CHUNK)
            for h in range(nh):
                cols = pl.ds(h * dv, dv)
                oz = pl.ds(nh * hw + h * 2 * dv, 2 * dv)
                head(p_ref.at[rows, pl.ds(h * hw, hw)], p_ref.at[rows, oz], colf_ref.at[h, rows], rowf_ref.at[h, cc],
                     w_ref.at[:, cols], hh_ref.at[rows, cols], stat_ref.at[h, rows], cs_ref.at[h, cc],
                     ns_ref.at[h, cc], dhm_ref.at[rows, cols], dp_ref.at[rows, pl.ds(h * hw, hw)],
                     dp_ref.at[rows, oz], gs_ref.at[h, rows], dwacc_refs[h], dc_refs[h], dn_refs[h])

        @pl.when(c == nc // cps - 1)
        def _():
            for h in range(nh):
                dw_ref[h] = dwacc_refs[h][...]

    def head(p_ref, oz_ref, colf_ref, rowf_ref, w_ref, hh_ref, stat_ref, cs_ref, ns_ref, dhm_ref,
             dp_ref, doz_ref, gs_ref, dw_ref, dc_ref, dn_ref):
        q = p_ref[:, 0:dk]
        k = p_ref[:, dk:2 * dk]
        v = p_ref[:, 2 * dk:2 * dk + dv]
        o = oz_ref[:, 0:dv].astype(F32)
        z = oz_ref[:, dv:2 * dv].astype(F32)
        qf = q.astype(F32)
        kf = k.astype(F32)
        n_prev = ns_ref[0:1, :]
        m_prev = ns_ref[1:2, 0:1]
        c_prev = cs_ref[...]
        m_row, dmat, inter, m_new, decay, w = _chunk_gates(colf_ref, rowf_ref, m_prev)

        hh = hh_ref[...]
        dhm_v = dhm_ref[...].astype(F32)
        so = _sigmoid(o)
        sg = _sigmoid(z)
        sz = z * sg
        r = lax.rsqrt(jnp.mean(hh * hh, axis=-1, keepdims=True) + RMS_EPS)
        hn = (hh * r) * w_ref[...]
        dhn = dhm_v * so * sz
        doz_ref[:, 0:dv] = (dhm_v * hn * sz * (so * (1.0 - so))).astype(BF16)
        doz_ref[:, dv:2 * dv] = (dhm_v * hn * so * (sg * (1.0 + z * (1.0 - sg)))).astype(BF16)
        dw_ref[...] += jnp.broadcast_to(jnp.sum(dhn * (hh * r), axis=0, keepdims=True), dw_ref.shape)
        gwn = dhn * w_ref[...]
        dhh = r * gwn - hh * ((r * r * r) * jnp.mean(gwn * hh, axis=-1, keepdims=True))

        den = stat_ref[:, 0:1]
        floor = jnp.exp(-m_row)
        denom = jnp.maximum(jnp.abs(den), floor)
        inv = 1.0 / denom
        dnum = dhh * inv
        hdot = jnp.sum(dhh * hh, axis=-1, keepdims=True)
        dden = jnp.where(jnp.abs(den) > floor, -(hdot * inv) * jnp.sign(den), 0.0)
        s = _dot_nt(q, k) * dmat
        dqk = (_dot_nt(dnum, v) + dden) * dmat
        dc_new = dc_ref[...]
        dn_new = dn_ref[...]
        idd = inter * dden
        dq = _dot(dqk, k) + inter * _dot_nt(dnum, c_prev) + idd * n_prev
        dkv = _dot_tn(dqk, q) + w * (_dot_nt(v, dc_new) + dn_new)
        dvv = _dot_tn(s, dnum) + w * _dot(k, dc_new)
        dp_ref[:, 0:dk] = dq.astype(BF16)
        dp_ref[:, dk:2 * dk] = dkv.astype(BF16)
        dp_ref[:, 2 * dk:2 * dk + dv] = dvv.astype(BF16)
        qdq = jnp.sum(qf * dq, axis=-1, keepdims=True)
        kdk = jnp.sum(kf * dkv, axis=-1, keepdims=True)
        dc_prev = decay * dc_new + _dot_tn(inter * qf, dnum)
        dn_prev = decay * dn_new + jnp.sum(idd * qf, axis=0, keepdims=True)
        dc_ref[...] = dc_prev
        dn_ref[...] = dn_prev
        cross = (jnp.sum(jnp.sum(c_prev.astype(F32) * dc_prev, axis=-1, keepdims=True), axis=0, keepdims=True)
                 + jnp.sum(n_prev * dn_prev, axis=-1, keepdims=True))
        lane = lax.broadcasted_iota(jnp.int32, (CHUNK, LANES), 1)
        gs_ref[...] = jnp.where(lane == 0, qdq - kdk, jnp.where(lane == 1, kdk, jnp.where(lane == 2, cross, 0.0)))

    ns = nc // cps
    rc = lambda c: (ns - 1 - c, 0)
    rc3 = lambda c: (0, ns - 1 - c, 0)
    rc4 = lambda c: (0, ns - 1 - c, 0, 0)
    return _call(
        body, name=name, grid=(ns,),
        in_specs=[pl.BlockSpec((rows_per_step, nh * (hw + 2 * dv)), rc),
                  pl.BlockSpec((nh, rows_per_step, LANES), rc3),
                  pl.BlockSpec((nh, cps, 8, LANES), rc4),
                  pl.BlockSpec((1, nh * dv), lambda c: (0, 0)),
                  pl.BlockSpec((rows_per_step, nh * dv), rc),
                  pl.BlockSpec((nh, rows_per_step, LANES), rc3),
                  pl.BlockSpec((nh, cps, dk, dv), rc4),
                  pl.BlockSpec((nh, cps, 8, dk), rc4),
                  pl.BlockSpec((rows_per_step, nh * dv), rc)],
        out_specs=[pl.BlockSpec((rows_per_step, nh * (hw + 2 * dv)), rc),
                   pl.BlockSpec((nh, rows_per_step, LANES), rc3),
                   pl.BlockSpec((nh, 8, dv), lambda c: (0, 0, 0))],
        out_shape=[jax.ShapeDtypeStruct((t, n_cols), BF16), jax.ShapeDtypeStruct((nh, t, LANES), F32),
                   jax.ShapeDtypeStruct((nh, 8, dv), F32)],
        scratch_shapes=([pltpu.VMEM((dk, dv), F32)] * nh + [pltpu.VMEM((1, dk), F32)] * nh
                        + [pltpu.VMEM((8, dv), F32)] * nh),
        args=(p1, colf, rowf, head_w, hh, stat, csave, nsave, dhm), exchange=exchange)


def _position():
    return lax.axis_index("x"), lax.axis_index("y"), lax.axis_index("c")


class _Exchange:
    def __init__(self, ins, out_shapes, start, mid, finish):
        n = len(ins)
        self.ins, self.out_shapes = list(ins), list(out_shapes)
        self.start, self.mid, self.finish = start, mid, finish
        self.scratch = [pltpu.SemaphoreType.DMA((n, 7)), pltpu.SemaphoreType.DMA((n, 7)),
                        pltpu.SemaphoreType.DMA((n,))]


def _gather_exchange(shards):
    n = len(shards)

    def plan(ins, outs, sems):
        send_sems, recv_sems, local_sems = sems
        x, y, c = _position()
        me, sibling = (x, y, c), (x, y, 1 - c)
        chips = [(1 - x, y), (x, 1 - y), (1 - x, 1 - y)]

        def copy(a, k, block, to, src=None):
            px, py, pc = block
            dst = outs[a].at[4 * px + 2 * py + pc]
            return pltpu.make_async_remote_copy(
                src_ref=dst if src is None else src, dst_ref=dst,
                send_sem=send_sems.at[a, k], recv_sem=recv_sems.at[a, k],
                device_id=to, device_id_type=MESH)

        def mine():
            return [pltpu.make_async_copy(ins[a], outs[a].at[4 * x + 2 * y + c], local_sems.at[a])
                    for a in range(n)]

        def first():
            out = []
            for a in range(n):
                out.append(copy(a, 0, me, sibling, src=ins[a]))
                out += [copy(a, 1 + j, me, (*chip, c), src=ins[a]) for j, chip in enumerate(chips)]
            return out

        def ici_in():
            return [copy(a, 1 + j, (*chip, c), me) for j, chip in enumerate(chips) for a in range(n)]

        def passed():
            return [copy(a, 4 + j, (*chip, c), sibling) for j, chip in enumerate(chips) for a in range(n)]

        def d2d_in():
            return ([copy(a, 0, sibling, me) for a in range(n)]
                    + [copy(a, 4 + j, (*chip, 1 - c), me) for j, chip in enumerate(chips) for a in range(n)])

        return mine, first, ici_in, passed, d2d_in

    def start(ins, outs, sems):
        mine, first, _, _, _ = plan(ins, outs, sems)
        for cp in mine() + first():
            cp.start()

    def mid(ins, outs, sems):
        _, _, ici_in, passed, _ = plan(ins, outs, sems)
        for arrived, onward in zip(ici_in(), passed()):
            arrived.wait_recv()
            onward.start()

    def finish(ins, outs, sems):
        mine, first, _, passed, d2d_in = plan(ins, outs, sems)
        for cp in d2d_in():
            cp.wait_recv()
        for cp in first() + passed():
            cp.wait_send()
        for cp in mine():
            cp.wait()

    shapes = [jax.ShapeDtypeStruct((N_DEV,) + s.shape, s.dtype) for s in shards]
    return _Exchange(shards, shapes, start, mid, finish)


def _scatter_exchange(fulls):
    n = len(fulls)

    def plan(ins, outs, sems):
        send_sems, recv_sems, local_sems = sems
        x, y, c = _position()
        my_slot = 4 * x + 2 * y + c

        def mine():
            return [pltpu.make_async_copy(ins[a].at[my_slot], outs[a].at[my_slot], local_sems.at[a])
                    for a in range(n)]

        def remote(arriving):
            out = []
            for kk in (1, 2, 4, 6, 3, 5, 7):
                kx, ky, kc = (kk >> 2) & 1, (kk >> 1) & 1, kk & 1
                px = 1 - x if kx else x
                py = 1 - y if ky else y
                pc = 1 - c if kc else c
                peer_slot = 4 * px + 2 * py + pc
                for a in range(n):
                    out.append(pltpu.make_async_remote_copy(
                        src_ref=ins[a].at[peer_slot], dst_ref=outs[a].at[peer_slot if arriving else my_slot],
                        send_sem=send_sems.at[a, kk - 1], recv_sem=recv_sems.at[a, kk - 1],
                        device_id=(px, py, pc), device_id_type=MESH))
            return out

        return mine, remote

    def start(ins, outs, sems):
        mine, remote = plan(ins, outs, sems)
        for cp in mine() + remote(False):
            cp.start()

    def finish(ins, outs, sems):
        mine, remote = plan(ins, outs, sems)
        for cp in remote(True):
            cp.wait_recv()
        for cp in remote(False):
            cp.wait_send()
        for cp in mine():
            cp.wait()

    shapes = [jax.ShapeDtypeStruct(f.shape, f.dtype) for f in fulls]
    return _Exchange(fulls, shapes, start, None, finish)


def _run_exchange(ex, name):
    n_in, n_out = len(ex.ins), len(ex.out_shapes)

    def body(*refs):
        ins, outs, sems = refs[:n_in], refs[n_in:n_in + n_out], refs[n_in + n_out:]
        ex.start(ins, outs, sems)
        if ex.mid is not None:
            ex.mid(ins, outs, sems)
        ex.finish(ins, outs, sems)

    any_spec = pl.BlockSpec(memory_space=pl.ANY)
    return pl.pallas_call(
        body, name=name,
        in_specs=[any_spec] * n_in, out_specs=[any_spec] * n_out,
        out_shape=ex.out_shapes, scratch_shapes=ex.scratch,
    )(*ex.ins)


def _natural_to_head_major(n0, dk, dv):
    nh = N_HEADS
    qk, e, hw = nh * dk, nh * dv, 2 * dk + dv
    if n0 < qk:
        h, off = divmod(n0, dk)
        return h * hw + off, True
    n1 = n0 - qk
    if n1 < qk:
        h, off = divmod(n1, dk)
        return h * hw + dk + off, False
    part, n3 = divmod(n1 - qk, e)
    h, off = divmod(n3, dv)
    if part == 0:
        return h * hw + 2 * dk + off, False
    return nh * hw + h * 2 * dv + (part - 1) * dv + off, False


def _mlstm_weight_layout(w_blocks, dk, dv, name):
    _, d, blk = w_blocks.shape
    nh = N_HEADS
    hw = 2 * dk + 3 * dv
    n_tiles = nh * hw // LANES
    tiles_per_block = d // LANES
    tr = _pick(d, 128, 16)
    scale = dk ** -0.5

    def body(w_ref, p_ref, g_ref):
        lane = lax.broadcasted_iota(jnp.int32, (tr, LANES), 1)

        def tile(s, r):
            return w_ref[s, :, r * LANES:(r + 1) * LANES].astype(F32)

        def last_col(s):
            return w_ref[s, :, d:d + 1].astype(F32)

        for tn in range(n_tiles):
            s, r = divmod(tn, tiles_per_block)
            if s == 0:
                val = tile(0, r)
            elif r == 0:
                from_prev = pltpu.roll(tile(s - 1, tiles_per_block - 1), s - 1, 1)
                from_here = pltpu.roll(tile(s, 0), s, 1)
                val = jnp.where(lane < s - 1, from_prev, jnp.where(lane == s - 1, last_col(s - 1), from_here))
            else:
                slab = jnp.concatenate([tile(s, r - 1), tile(s, r)], axis=1)
                val = pltpu.roll(slab, s, 1)[:, LANES:]
            at, is_q = _natural_to_head_major(tn * LANES, dk, dv)
            if is_q:
                val = val * scale
            p_ref[:, at:at + LANES] = val.astype(BF16)
        n_gate = 2 * nh
        s = N_DEV - 1
        gates = pltpu.roll(tile(s, tiles_per_block - 1), n_gate - 1, 1)
        gates = jnp.where(lane < n_gate - 1, gates, jnp.where(lane == n_gate - 1, last_col(s), 0.0)).astype(BF16)
        g_ref[...] = gates
        p_ref[:, nh * hw:nh * hw + LANES] = gates

    return pl.pallas_call(
        body, name=name, grid=(d // tr,),
        in_specs=[pl.BlockSpec((N_DEV, tr, blk), lambda i: (0, i, 0))],
        out_specs=[pl.BlockSpec((tr, nh * hw + LANES), lambda i: (i, 0)), pl.BlockSpec((tr, LANES), lambda i: (i, 0))],
        out_shape=[jax.ShapeDtypeStruct((d, nh * hw + LANES), BF16), jax.ShapeDtypeStruct((d, LANES), BF16)],
        compiler_params=_params(),
    )(w_blocks)


def _mlstm_grad_layout(g_p1, row0, n_rows, dk, dv, name):
    d = g_p1.shape[0]
    nh = N_HEADS
    hw = 2 * dk + 3 * dv
    n_tiles = nh * hw // LANES
    tiles_per_block = d // LANES
    tr = _pick(d, 128, 16)
    scale = dk ** -0.5

    def body(p_ref, o_ref):
        def natural(tn):
            if tn == n_tiles:
                return p_ref[:, nh * hw:nh * hw + LANES].astype(F32)
            at, is_q = _natural_to_head_major(tn * LANES, dk, dv)
            val = p_ref[:, at:at + LANES].astype(F32)
            return val * scale if is_q else val

        for s in range(N_DEV):
            for r in range(tiles_per_block):
                tn = s * tiles_per_block + r
                if s == 0:
                    val = natural(tn)
                else:
                    slab = jnp.concatenate([natural(tn), natural(tn + 1)], axis=1)
                    val = pltpu.roll(slab, 2 * LANES - s, 1)[:, :LANES]
                o_ref[s, :, r * LANES:(r + 1) * LANES] = val.astype(BF16)
            o_ref[s, :, d:d + 1] = natural((s + 1) * tiles_per_block)[:, s:s + 1].astype(BF16)

    first = row0 // tr
    return pl.pallas_call(
        body, name=name, grid=(n_rows // tr,),
        in_specs=[pl.BlockSpec((tr, nh * hw + LANES), lambda i: (i + first, 0))],
        out_specs=pl.BlockSpec((N_DEV, tr, d + 1), lambda i: (0, i, 0)),
        out_shape=jax.ShapeDtypeStruct((N_DEV, n_rows, d + 1), BF16),
        compiler_params=_params(),
    )(g_p1)


def _adamw_math(w, g, m, v):
    m = ADAM_B1 * m + (1.0 - ADAM_B1) * g
    v = ADAM_B2 * v + (1.0 - ADAM_B2) * (g * g)
    m_hat = m / (1.0 - ADAM_B1 ** ADAM_STEP)
    v_hat = v / (1.0 - ADAM_B2 ** ADAM_STEP)
    delta = -ADAM_LR * (m_hat / (jnp.sqrt(v_hat) + ADAM_EPS) + ADAM_WD * w)
    return delta, m, v


def _adamw_sharded(parts, w, m, v, name):
    n_parts = len(parts)
    _, rk, c = parts[0].shape
    r = n_parts * rk
    tr = _pick(rk, 128, 8)
    per = rk // tr

    def body(*refs):
        p_refs = refs[:n_parts]
        w_ref, m_ref, v_ref, g_ref, d_ref, nm_ref, nv_ref, gsum_ref = refs[n_parts:]
        i = pl.program_id(0)
        for k, p_ref in enumerate(p_refs):
            @pl.when(i // per == k)
            def _(p_ref=p_ref):
                g = p_ref[0].astype(F32)
                for s in range(1, N_DEV):
                    g = g + p_ref[s].astype(F32)
                gsum_ref[...] = g
        g = gsum_ref[...]
        delta, m_new, v_new = _adamw_math(w_ref[...], g, m_ref[...], v_ref[...])
        g_ref[...] = g
        d_ref[...] = delta
        nm_ref[...] = m_new
        nv_ref[...] = v_new

    def part_spec(k):
        return pl.BlockSpec((N_DEV, tr, c), lambda i: (0, jnp.clip(i - k * per, 0, per - 1), 0))

    blk = pl.BlockSpec((tr, c), lambda i: (i, 0))
    return pl.pallas_call(
        body, name=name, grid=(r // tr,),
        in_specs=[part_spec(k) for k in range(n_parts)] + [blk, blk, blk],
        out_specs=[blk] * 4,
        out_shape=[jax.ShapeDtypeStruct((r, c), F32)] * 4,
        scratch_shapes=[pltpu.VMEM((tr, c), F32)],
        compiler_params=_params(),
    )(*parts, w, m, v)


def _sum_devices(parts, name):
    _, r, c = parts.shape

    def body(p_ref, o_ref):
        g = p_ref[0]
        for s in range(1, N_DEV):
            g = g + p_ref[s]
        o_ref[...] = g

    return pl.pallas_call(
        body, name=name, out_shape=jax.ShapeDtypeStruct((r, c), F32), compiler_params=_params(),
    )(parts)


def _adamw_small(gs, ws, ms, vs, name):
    n = len(gs)

    def body(*refs):
        g_refs, w_refs, m_refs, v_refs = refs[:n], refs[n:2 * n], refs[2 * n:3 * n], refs[3 * n:4 * n]
        d_refs, nm_refs, nv_refs = refs[4 * n:5 * n], refs[5 * n:6 * n], refs[6 * n:7 * n]
        for a in range(n):
            delta, m_new, v_new = _adamw_math(w_refs[a][...], g_refs[a][...], m_refs[a][...], v_refs[a][...])
            d_refs[a][...] = delta
            nm_refs[a][...] = m_new
            nv_refs[a][...] = v_new

    shapes = [jax.ShapeDtypeStruct(w.shape, F32) for w in ws]
    outs = pl.pallas_call(
        body, name=name, out_shape=shapes * 3, compiler_params=_params(),
    )(*gs, *ws, *ms, *vs)
    return outs[:n], outs[n:2 * n], outs[2 * n:]


def _pad_rows(a, rows):
    return jnp.pad(a, ((0, rows - a.shape[0]), (0, 0)))


def kernel(x, meta_tokens, norm_w, conv_in_w, conv_w, conv_out_w, mlstm_in_w, mlstm_gate_b, mlstm_head_norm_w, mlstm_out_w, final_norm_w, loss_target, m_meta_tokens, m_norm_w, m_conv_in_w, m_conv_w, m_conv_out_w, m_mlstm_in_w, m_mlstm_gate_b, m_mlstm_head_norm_w, m_mlstm_out_w, m_final_norm_w, v_meta_tokens, v_norm_w, v_conv_in_w, v_conv_w, v_conv_out_w, v_mlstm_in_w, v_mlstm_gate_b, v_mlstm_head_norm_w, v_mlstm_out_w, v_final_norm_w):
    seq, d = x.shape[1], x.shape[2]
    t = seq + LEAD
    e = 2 * d
    ec = e // 2
    nh = N_HEADS
    dv = e // nh
    dk = dv // 2
    qk = nh * dk
    hw = 2 * dk + 3 * dv
    n_in = 2 * qk + 3 * e + 2 * nh
    n_in_s = n_in // N_DEV
    me = 4 * lax.axis_index("x") + 2 * lax.axis_index("y") + lax.axis_index("c")
    tm = _pick(t, 832, 16)
    tm_in = _pick(t, 1664, 16)
    tkw = _pick(t, 2080, 16)

    small = jnp.concatenate([
        meta_tokens,
        _pad_rows(conv_w[0].reshape(3 * (e // N_DEV) // LANES, LANES), 8),
        _pad_rows(mlstm_head_norm_w[0].reshape((e // N_DEV) // LANES, LANES), 8),
    ], axis=0) if d // N_DEV == LANES else None
    assert small is not None, "the packed small-weight block assumes d_model / 8 == 128"
    (small_g,) = _run_exchange(_gather_exchange([small]), "gather_small_weights")
    meta_full = jnp.transpose(small_g[:, 0:N_META, :], (1, 0, 2)).reshape(N_META, d)
    cw_rows = 3 * (e // N_DEV) // LANES
    conv_w_full = jnp.transpose(
        small_g[:, N_META:N_META + cw_rows, :].reshape(N_DEV, 3, e // N_DEV), (1, 0, 2)).reshape(3, e)
    hn_rows = (e // N_DEV) // LANES
    head_w_full = small_g[:, N_META + 8:N_META + 8 + hn_rows, :].reshape(1, e)

    tgt = loss_target[0]

    ci_map = lambda blk: 2 * (blk % 4) + blk // 4
    (h0, u0), (w_ci,) = _rms_first(x[0], meta_full, norm_w[0:1], "rms0",
                                   exchange=_gather_exchange([conv_in_w[0].astype(BF16)]))
    p0, (w_co, w_mi) = _matmul(
        u0, w_ci, form="nn", m=t, n=8 * ec, kdim=d, tm=tm_in, tn=ec, tk=d, out_dtype=BF16, name="conv_in",
        b_spec=pl.BlockSpec((None, d, ec), lambda i, j, k: (ci_map(j), 0, 0)),
        exchange=_gather_exchange([conv_out_w[0].astype(BF16), mlstm_in_w[0].astype(BF16)]))
    w_co = w_co.reshape(e, d)
    (g0, y0), (w_mo,) = _conv_gate_fwd(p0, conv_w_full, "conv_gate",
                                       exchange=_gather_exchange([mlstm_out_w[0].astype(BF16)]))
    w_mo = w_mo.reshape(e, d)
    h1 = _matmul(g0, w_co, form="nn", m=t, n=d, kdim=e, tm=tm, tn=d, tk=e, out_dtype=F32, name="conv_out",
                 residual=h0)

    assert n_in_s == d + 1 and w_mi.shape == (N_DEV, d, d + 1)
    w_p1, w_gate = _mlstm_weight_layout(w_mi, dk, dv, "mlstm_w_layout")
    w_gate_t = jnp.transpose(w_gate[:, 0:16])
    bias_row = jnp.pad(mlstm_gate_b, ((0, 0), (0, LANES - 2 * nh)))
    bias_col = jnp.pad(mlstm_gate_b.T, ((0, 16 - 2 * nh), (0, 0)))

    u1 = _rms_fwd(h1, norm_w[1:2], "rms1")
    p1 = _matmul(u1, w_p1, form="nn", m=t, n=nh * hw, kdim=d, tm=tm_in, tn=_pick(nh * hw, 1024, LANES), tk=d,
                 out_dtype=BF16, name="mlstm_in")
    graw = _matmul(u1, w_gate, form="nn", m=t, n=LANES, kdim=d, tm=tm, tn=LANES, tk=d, out_dtype=F32,
                   name="gates_col")
    graw_t = _matmul(w_gate_t, u1, form="nt", m=16, n=t, kdim=d, tm=16, tn=_pick(t, 1664, LANES), tk=d,
                     out_dtype=F32, name="gates_row")
    colf, rowf = _gates_fwd(graw, graw_t, bias_row, bias_col, "gates_fwd")
    hh, stat, csave, nsave = _mlstm_fwd(p1, colf, rowf, dk, dv, "mlstm_fwd")
    hm = _head_gate_fwd(hh, p1, head_w_full, dv, "head_gate")
    h2 =_matmul(hm, w_mo, form="nn", m=t, n=d, kdim=e, tm=tm, tn=d, tk=e, out_dtype=F32, name="mlstm_out",
                 residual=h1)

    dh2, dwf, loss_part = _final_loss(h2, final_norm_w.reshape(1, d), tgt, "final_loss")

    dhm = _matmul(dh2, w_mo, form="nt", m=t, n=e, kdim=d, tm=tm, tn=_pick(e, 2048, LANES), tk=d, out_dtype=BF16,
                  name="mlstm_out_dx")
    g_mo = _matmul(hm, dh2, form="tn", m=e, n=d, kdim=t, tm=_pick(e, 1024, LANES), tn=d, tk=tkw, out_dtype=BF16,
                   name="mlstm_out_dw")
    n_p1 = nh * hw + LANES
    (dp1, gstat, dhead), (r_mo,) = _mlstm_bwd(
        p1, colf, rowf, head_w_full, hh, stat, csave, nsave, dhm, n_p1, dk, dv, "mlstm_bwd",
        exchange=_scatter_exchange([g_mo.reshape(N_DEV, e // N_DEV, d)]))
    dp1, dbias = _gates_bwd(gstat, colf, dp1, "gates_bwd")
    du1 = _matmul(dp1, w_p1, form="nt", m=t, n=d, kdim=n_p1, tm=tm, tn=d, tk=_pick(n_p1, 2048, LANES),
                  out_dtype=F32, name="mlstm_in_dx")
    g_p1 = _matmul(u1, dp1, form="tn", m=d, n=n_p1, kdim=t, tm=d, tn=_pick(n_p1, 2048, LANES), tk=tkw,
                   out_dtype=BF16, name="mlstm_in_dw")
    dh1, dnw1 = _rms_bwd(h1, norm_w[1:2], du1, dh2, "rms1_bwd")

    g_mi_a = _mlstm_grad_layout(g_p1, 0, d // 2, dk, dv, "mlstm_g_layout_a")
    g_mi_b = _mlstm_grad_layout(g_p1, d // 2, d // 2, dk, dv, "mlstm_g_layout_b")

    dg0 =_matmul(dh1, w_co, form="nt", m=t, n=e, kdim=d, tm=tm, tn=_pick(e, 2048, LANES), tk=d, out_dtype=BF16,
                  name="conv_out_dx")
    g_co = _matmul(g0, dh1, form="tn", m=e, n=d, kdim=t, tm=_pick(e, 1024, LANES), tn=d, tk=tkw, out_dtype=BF16,
                   name="conv_out_dw")
    (dp0, dconv), (r_co, r_mi_a) = _conv_gate_bwd(
        p0, y0, dg0, conv_w_full, "conv_gate_bwd",
        exchange=_scatter_exchange([g_co.reshape(N_DEV, e // N_DEV, d), g_mi_a]))
    g_ci, (r_mi_b,) = _matmul(
        u0, dp0, form="tn", m=d, n=8 * ec, kdim=t, tm=d, tn=ec, tk=tkw, out_dtype=BF16, name="conv_in_dw",
        out_shape=(N_DEV, d, ec), out_spec=pl.BlockSpec((None, d, ec), lambda i, j, k: (ci_map(j), 0, 0)),
        exchange=_scatter_exchange([g_mi_b]))
    du0, (r_ci,) = _matmul(
        dp0, w_ci, form="nt", m=t, n=d, kdim=8 * ec, tm=tm, tn=d, tk=ec, out_dtype=F32, name="conv_in_dx",
        b_spec=pl.BlockSpec((None, d, ec), lambda i, j, k: (ci_map(k), 0, 0)),
        exchange=_scatter_exchange([g_ci]))
    grad_x, dmeta, dnw0 = _rms_bwd_first(h0, norm_w[0:1], du0, dh1, "rms0_bwd")
    grad_x = grad_x[None]

    row8 = lax.broadcasted_iota(jnp.int32, (8, 1), 0)
    loss_wide = jnp.pad(loss_part, ((0, 0), (0, d - LANES)))
    payload = jnp.concatenate([
        jnp.where(row8 == 0, dnw0, jnp.where(row8 == 1, dnw1, 0.0)),
        jnp.where(row8 == 0, dwf, jnp.where(row8 == 1, loss_wide, 0.0)),
        jnp.where(row8 == 0, jnp.pad(dbias, ((0, 0), (0, d - LANES))), 0.0),
        dmeta,
        _pad_rows(dconv[0:3].reshape(3 * e // d, d), 8),
        _pad_rows(dhead[:, 0, :].reshape(e // d, d), 8),
    ], axis=0)
    (payload_g,) = _run_exchange(_gather_exchange([payload]), "gather_small_grads")
    tot = _sum_devices(payload_g, "sum_small_grads")

    loss = tot[9, 0]
    g_norm = tot[0:2]
    g_final = tot[8]
    g_gate_b = tot[16:17, 0:2 * nh]
    g_meta = lax.dynamic_slice(tot[24:24 + N_META], (0, me * (d // N_DEV)), (N_META, d // N_DEV))
    g_conv_w = lax.dynamic_slice(tot[40:40 + 3 * e // d].reshape(3, e), (0, me * (e // N_DEV)), (3, e // N_DEV))
    g_head = lax.dynamic_slice(tot[48:48 + e // d].reshape(1, e), (0, me * (e // N_DEV)), (1, e // N_DEV))

    g1, d1, nm1, nv1 = _adamw_sharded([r_ci], conv_in_w[0], m_conv_in_w[0], v_conv_in_w[0], "adamw_conv_in")
    g2, d2, nm2, nv2 = _adamw_sharded([r_co], conv_out_w[0], m_conv_out_w[0], v_conv_out_w[0], "adamw_conv_out")
    g3, d3, nm3, nv3 = _adamw_sharded([r_mi_a, r_mi_b], mlstm_in_w[0], m_mlstm_in_w[0], v_mlstm_in_w[0],
                                      "adamw_mlstm_in")
    g4, d4, nm4, nv4 = _adamw_sharded([r_mo], mlstm_out_w[0], m_mlstm_out_w[0], v_mlstm_out_w[0],
                                      "adamw_mlstm_out")

    small_g = [g_meta, g_norm, g_conv_w, g_gate_b, g_head, g_final.reshape(1, d)]
    small_w = [meta_tokens, norm_w, conv_w[0], mlstm_gate_b, mlstm_head_norm_w, final_norm_w.reshape(1, d)]
    small_m = [m_meta_tokens, m_norm_w, m_conv_w[0], m_mlstm_gate_b, m_mlstm_head_norm_w, m_final_norm_w.reshape(1, d)]
    small_v = [v_meta_tokens, v_norm_w, v_conv_w[0], v_mlstm_gate_b, v_mlstm_head_norm_w, v_final_norm_w.reshape(1, d)]
    sd, snm, snv = _adamw_small(small_g, small_w, small_m, small_v, "adamw_small")

    def order(meta, norm, cin, cw, cout, min_, gb, hn, mout, fin):
        return (meta, norm, cin[None], cw[None], cout[None], min_[None], gb, hn, mout[None], fin.reshape(d))

    grads = order(g_meta, g_norm, g1, g_conv_w, g2, g3, g_gate_b, g_head, g4, g_final)
    deltas = order(sd[0], sd[1], d1, sd[2], d2, d3, sd[3], sd[4], d4, sd[5])
    new_m = order(snm[0], snm[1], nm1, snm[2], nm2, nm3, snm[3], snm[4], nm4, snm[5])
    new_v = order(snv[0], snv[1], nv1, snv[2], nv2, nv3, snv[3], snv[4], nv4, snv[5])
    return (loss, grad_x, *grads, *deltas, *new_m, *new_v)
```

```python
import functools

import jax
import jax.numpy as jnp
from jax import lax
from jax.experimental import pallas as pl
from jax.experimental.pallas import tpu as pltpu

F32 = jnp.float32
BF16 = jnp.bfloat16
MESH = pl.DeviceIdType.MESH

N_DEV = 8
N_META = 16
N_HEADS = 4
CHUNK = 64
CHUNKS_PER_STEP = 2
LEAD = 128
PAD_ROWS = LEAD - N_META
RMS_EPS = 1e-6
NEG = -1e30
LANES = 128
VMEM_LIMIT = 48 * 1024 * 1024

ADAM_LR = 0.001
ADAM_B1 = 0.9
ADAM_B2 = 0.999
ADAM_EPS = 1e-08
ADAM_WD = 0.01
ADAM_STEP = 10

HIGHEST = lax.Precision.HIGHEST


def _pick(n, target, mult):
    best = None
    for d in range(mult, min(n, target) + 1, mult):
        if n % d == 0:
            best = d
    return n if best is None else best


def _params(**kw):
    return pltpu.CompilerParams(vmem_limit_bytes=VMEM_LIMIT, **kw)


def _sigmoid(x):
    return 0.5 * jnp.tanh(0.5 * x) + 0.5


STRIP_ROWS = 16
STRIP_COLS = 256


def _strips(n_rows, n_cols):
    for c0 in range(0, n_cols, STRIP_COLS):
        for r0 in range(0, n_rows, STRIP_ROWS):
            yield slice(r0, r0 + STRIP_ROWS), slice(c0, min(c0 + STRIP_COLS, n_cols))


def _shift(sl, by):
    return slice(sl.start + by, sl.stop + by)


def _call(body, *, name, grid, in_specs, out_specs, out_shape, args, scratch_shapes=(), aliases=None,
          exchange=None):
    aliases = {} if aliases is None else aliases
    if exchange is None:
        return pl.pallas_call(
            body, name=name, grid=grid, in_specs=list(in_specs), out_specs=out_specs, out_shape=out_shape,
            scratch_shapes=list(scratch_shapes), input_output_aliases=aliases,
            compiler_params=_params())(*args)
    ex = exchange
    single = not isinstance(out_shape, (list, tuple))
    shapes = [out_shape] if single else list(out_shape)
    specs = [out_specs] if single else list(out_specs)
    n_in, n_out, n_scr = len(in_specs), len(shapes), len(scratch_shapes)
    n_ex_in, n_ex_out = len(ex.ins), len(ex.out_shapes)
    steps = 1
    for size in grid:
        steps *= size

    def wrapped(*refs):
        own_in, ex_in = refs[:n_in], refs[n_in:n_in + n_ex_in]
        at = n_in + n_ex_in
        own_out, ex_out = refs[at:at + n_out], refs[at + n_out:at + n_out + n_ex_out]
        at += n_out + n_ex_out
        own_scr, ex_scr = refs[at:at + n_scr], refs[at + n_scr:]
        step = 0
        for axis, size in enumerate(grid):
            step = step * size + pl.program_id(axis)

        @pl.when(step == 0)
        def _():
            ex.start(ex_in, ex_out, ex_scr)

        if ex.mid is not None:
            @pl.when(step == (3 * steps) // 4)
            def _():
                ex.mid(ex_in, ex_out, ex_scr)

        body(*own_in, *own_out, *own_scr)

        @pl.when(step == steps - 1)
        def _():
            ex.finish(ex_in, ex_out, ex_scr)

    any_spec = pl.BlockSpec(memory_space=pl.ANY)
    res = pl.pallas_call(
        wrapped, name=name, grid=grid,
        in_specs=list(in_specs) + [any_spec] * n_ex_in, out_specs=specs + [any_spec] * n_ex_out,
        out_shape=shapes + ex.out_shapes, scratch_shapes=list(scratch_shapes) + ex.scratch,
        input_output_aliases=aliases, compiler_params=_params())(*args, *ex.ins)
    return (res[0] if single else res[:n_out]), res[n_out:]


def _matmul(a, b, *, form, m, n, kdim, tm, tn, tk, out_dtype, name,
            a_spec=None, b_spec=None, out_spec=None, out_shape=None, residual=None, exchange=None):
    ni, nj, nk = m // tm, n // tn, kdim // tk
    assert ni * tm == m and nj * tn == n and nk * tk == kdim, (name, m, n, kdim, tm, tn, tk)
    if form == "nn":
        dn = (((1,), (0,)), ((), ()))
        a_def = pl.BlockSpec((tm, tk), lambda i, j, k: (i, k))
        b_def = pl.BlockSpec((tk, tn), lambda i, j, k: (k, j))
    elif form == "nt":
        dn = (((1,), (1,)), ((), ()))
        a_def = pl.BlockSpec((tm, tk), lambda i, j, k: (i, k))
        b_def = pl.BlockSpec((tn, tk), lambda i, j, k: (j, k))
    else:
        dn = (((0,), (0,)), ((), ()))
        a_def = pl.BlockSpec((tk, tm), lambda i, j, k: (k, i))
        b_def = pl.BlockSpec((tk, tn), lambda i, j, k: (k, j))
    a_spec = a_def if a_spec is None else a_spec
    b_spec = b_def if b_spec is None else b_spec
    o_spec = pl.BlockSpec((tm, tn), lambda i, j, k: (i, j)) if out_spec is None else out_spec
    has_res = residual is not None

    def body(*refs):
        a_ref, b_ref = refs[:2]
        r_ref = refs[2] if has_res else None
        o_ref = refs[2 + has_res]

        def product():
            return lax.dot_general(a_ref[...].astype(BF16), b_ref[...].astype(BF16), dn,
                                   preferred_element_type=F32)

        def finish(acc):
            if has_res:
                acc = acc + r_ref[...].astype(F32)
            o_ref[...] = acc.astype(o_ref.dtype)

        if nk == 1:
            finish(product())
        else:
            acc_ref = refs[3 + has_res]
            k = pl.program_id(2)

            @pl.when(k == 0)
            def _():
                acc_ref[...] = jnp.zeros_like(acc_ref)

            acc_ref[...] += product()

            @pl.when(k == nk - 1)
            def _():
                finish(acc_ref[...])

    in_specs = [a_spec, b_spec]
    args = [a, b]
    if has_res:
        in_specs.append(pl.BlockSpec((tm, tn), lambda i, j, k: (i, j)))
        args.append(residual)
    return _call(
        body, name=name, grid=(ni, nj, nk), in_specs=in_specs, out_specs=o_spec,
        out_shape=jax.ShapeDtypeStruct((m, n) if out_shape is None else out_shape, out_dtype),
        scratch_shapes=[] if nk == 1 else [pltpu.VMEM((tm, tn), F32)], args=args, exchange=exchange)


def _rms_fwd(h, w, name, exchange=None):
    t, d = h.shape
    tm = _pick(t, 416, 16)

    def body(h_ref, w_ref, u_ref):
        x = h_ref[...]
        r = lax.rsqrt(jnp.mean(x * x, axis=-1, keepdims=True) + RMS_EPS)
        u_ref[...] = ((x * r) * w_ref[...]).astype(BF16)

    return _call(
        body, name=name, grid=(t // tm,),
        in_specs=[pl.BlockSpec((tm, d), lambda i: (i, 0)), pl.BlockSpec((1, d), lambda i: (0, 0))],
        out_specs=pl.BlockSpec((tm, d), lambda i: (i, 0)),
        out_shape=jax.ShapeDtypeStruct((t, d), BF16), args=(h, w), exchange=exchange)


def _rms_first(x, meta, w, name, exchange=None):
    seq, d = x.shape
    t = seq + LEAD
    tm = LEAD

    def body(x_ref, meta_ref, w_ref, h_ref, u_ref):
        i = pl.program_id(0)

        @pl.when(i == 0)
        def _():
            h_ref[0:PAD_ROWS, :] = jnp.zeros((PAD_ROWS, d), F32)
            h_ref[PAD_ROWS:LEAD, :] = meta_ref[...]

        @pl.when(i > 0)
        def _():
            h_ref[...] = x_ref[...]

        hv = h_ref[...]
        r = lax.rsqrt(jnp.mean(hv * hv, axis=-1, keepdims=True) + RMS_EPS)
        u_ref[...] = ((hv * r) * w_ref[...]).astype(BF16)

    row = pl.BlockSpec((tm, d), lambda i: (i, 0))
    return _call(
        body, name=name, grid=(t // tm,),
        in_specs=[pl.BlockSpec((tm, d), lambda i: (jnp.maximum(i - 1, 0), 0)),
                  pl.BlockSpec((N_META, d), lambda i: (0, 0)), pl.BlockSpec((1, d), lambda i: (0, 0))],
        out_specs=[row, row],
        out_shape=[jax.ShapeDtypeStruct((t, d), F32), jax.ShapeDtypeStruct((t, d), BF16)],
        args=(x, meta, w), exchange=exchange)


def _rms_bwd(h, w, du, dres, name):
    t, d = h.shape
    tm = _pick(t, 416, 16)

    def body(h_ref, w_ref, du_ref, dres_ref, dh_ref, dw_ref):
        i = pl.program_id(0)
        x = h_ref[...]
        g = du_ref[...].astype(F32)
        r = lax.rsqrt(jnp.mean(x * x, axis=-1, keepdims=True) + RMS_EPS)
        gw = g * w_ref[...]
        dot = jnp.mean(gw * x, axis=-1, keepdims=True)
        dh_ref[...] = dres_ref[...] + (r * gw - x * ((r * r * r) * dot))
        part = jnp.sum(g * (x * r), axis=0, keepdims=True)

        @pl.when(i == 0)
        def _():
            dw_ref[...] = jnp.zeros_like(dw_ref)

        dw_ref[...] += jnp.broadcast_to(part, dw_ref.shape)

    row = pl.BlockSpec((tm, d), lambda i: (i, 0))
    return pl.pallas_call(
        body, name=name, grid=(t // tm,),
        in_specs=[row, pl.BlockSpec((1, d), lambda i: (0, 0)), row, row],
        out_specs=[row, pl.BlockSpec((8, d), lambda i: (0, 0))],
        out_shape=[jax.ShapeDtypeStruct((t, d), F32), jax.ShapeDtypeStruct((8, d), F32)],
        compiler_params=_params(),
    )(h, w, du, dres)


def _rms_bwd_first(h, w, du, dres, name):
    t, d = h.shape
    sub = LEAD
    per = _pick((t - LEAD) // sub, 4, 1)
    tmo = per * sub
    n_out = (t - LEAD) // tmo
    n_blocks = t // sub

    def body(*refs):
        w_ref = refs[0]
        h_refs, du_refs, dres_refs = refs[1:1 + per], refs[1 + per:1 + 2 * per], refs[1 + 2 * per:1 + 3 * per]
        gx_ref, dmeta_ref, dw_ref = refs[1 + 3 * per:]
        i = pl.program_id(0)

        def piece(q):
            x = h_refs[q][...]
            g = du_refs[q][...].astype(F32)
            r = lax.rsqrt(jnp.mean(x * x, axis=-1, keepdims=True) + RMS_EPS)
            gw = g * w_ref[...]
            dot = jnp.mean(gw * x, axis=-1, keepdims=True)
            dh = dres_refs[q][...] + (r * gw - x * ((r * r * r) * dot))
            return dh, jnp.sum(g * (x * r), axis=0, keepdims=True)

        @pl.when(i == 0)
        def _():
            dh, part = piece(0)
            dmeta_ref[...] = dh[PAD_ROWS:LEAD, :]
            dw_ref[...] = jnp.broadcast_to(part, dw_ref.shape)

        @pl.when(i > 0)
        def _():
            for q in range(per):
                dh, part = piece(q)
                gx_ref[q * sub:(q + 1) * sub, :] = dh
                dw_ref[...] += jnp.broadcast_to(part, dw_ref.shape)

    def piece_spec(q):
        return pl.BlockSpec((sub, d), lambda i: (jnp.clip(1 + per * (i - 1) + q, 0, n_blocks - 1), 0))

    pieces = [piece_spec(q) for q in range(per)]
    return pl.pallas_call(
        body, name=name, grid=(n_out + 1,),
        in_specs=[pl.BlockSpec((1, d), lambda i: (0, 0))] + pieces * 3,
        out_specs=[pl.BlockSpec((tmo, d), lambda i: (jnp.maximum(i - 1, 0), 0)),
                   pl.BlockSpec((N_META, d), lambda i: (0, 0)), pl.BlockSpec((8, d), lambda i: (0, 0))],
        out_shape=[jax.ShapeDtypeStruct((t - LEAD, d), F32), jax.ShapeDtypeStruct((N_META, d), F32),
                   jax.ShapeDtypeStruct((8, d), F32)],
        compiler_params=_params(),
    )(w, *([h] * per), *([du] * per), *([dres] * per))


def _final_loss(h, w, tgt, name):
    t, d = h.shape
    sub = LEAD
    per = _pick(t // sub, 5, 1)
    tm = per * sub

    def body(h_ref, w_ref, *rest):
        t_refs, (dh_ref, dw_ref, loss_ref) = rest[:per], rest[per:]
        i = pl.program_id(0)

        @pl.when(i == 0)
        def _():
            dw_ref[...] = jnp.zeros_like(dw_ref)
            loss_ref[...] = jnp.zeros_like(loss_ref)

        for q in range(per):
            sl = slice(q * sub, (q + 1) * sub)
            x = h_ref[sl, :]
            r = lax.rsqrt(jnp.mean(x * x, axis=-1, keepdims=True) + RMS_EPS)
            yn = x * r
            y = yn * w_ref[...]
            rows = i * tm + q * sub + lax.broadcasted_iota(jnp.int32, (sub, 1), 0)
            diff = jnp.where(rows >= LEAD, y - t_refs[q][...], 0.0)
            tile_loss = 0.5 * jnp.sum(jnp.mean(diff * diff, axis=-1, keepdims=True), axis=0, keepdims=True)
            dy = diff / d
            gw = dy * w_ref[...]
            dot = jnp.mean(gw * x, axis=-1, keepdims=True)
            dh_ref[sl, :] = r * gw - x * ((r * r * r) * dot)
            dw_ref[...] += jnp.broadcast_to(jnp.sum(dy * yn, axis=0, keepdims=True), dw_ref.shape)
            loss_ref[...] += jnp.broadcast_to(tile_loss, loss_ref.shape)

    row = pl.BlockSpec((tm, d), lambda i: (i, 0))
    piece = [pl.BlockSpec((sub, d), functools.partial(lambda i, q: (jnp.maximum(i * per + q - 1, 0), 0), q=q))
             for q in range(per)]
    return pl.pallas_call(
        body, name=name, grid=(t // tm,),
        in_specs=[row, pl.BlockSpec((1, d), lambda i: (0, 0))] + piece,
        out_specs=[row, pl.BlockSpec((8, d), lambda i: (0, 0)), pl.BlockSpec((8, LANES), lambda i: (0, 0))],
        out_shape=[jax.ShapeDtypeStruct((t, d), F32), jax.ShapeDtypeStruct((8, d), F32),
                   jax.ShapeDtypeStruct((8, LANES), F32)],
        compiler_params=_params(),
    )(h, w, *([tgt] * per))


def _conv_gate_fwd(p0, conv_w, name, exchange=None):
    t = p0.shape[0]
    ec = p0.shape[1] // 8
    tm = _pick(t, 416, 16)
    nt = t // tm

    def body(p_ref, w_ref, g_ref, y_ref, ext_ref):
        i = pl.program_id(1)

        @pl.when(i == 0)
        def _():
            ext_ref[0:8, :] = jnp.zeros((8, ec), F32)

        for rows, cols in _strips(tm, ec):
            cg = p_ref[rows, _shift(cols, ec)].astype(F32)
            xin = p_ref[rows, _shift(cols, 2 * ec)].astype(F32)
            ext_ref[_shift(rows, 8), cols] = cg * xin
        for rows, cols in _strips(tm, ec):
            y = (w_ref[0:1, cols] * ext_ref[_shift(rows, 6), cols] + w_ref[1:2, cols] * ext_ref[_shift(rows, 7), cols]
                 + w_ref[2:3, cols] * ext_ref[_shift(rows, 8), cols])
            bg = p_ref[rows, cols].astype(F32)
            z = p_ref[rows, _shift(cols, 3 * ec)].astype(F32)
            y_ref[rows, cols] = y.astype(BF16)
            g_ref[rows, cols] = ((z * _sigmoid(z)) * bg * y).astype(BF16)
        ext_ref[0:8, :] = ext_ref[tm:tm + 8, :]

    out = pl.BlockSpec((tm, ec), lambda j, i: (i, j))
    return _call(
        body, name=name, grid=(2, nt),
        in_specs=[pl.BlockSpec((tm, 4 * ec), lambda j, i: (i, j)), pl.BlockSpec((3, ec), lambda j, i: (0, j))],
        out_specs=[out, out],
        out_shape=[jax.ShapeDtypeStruct((t, 2 * ec), BF16)] * 2,
        scratch_shapes=[pltpu.VMEM((tm + 8, ec), F32)], args=(p0, conv_w), exchange=exchange)


def _conv_gate_bwd(p0, y, dg, conv_w, name, exchange=None):
    t = p0.shape[0]
    ec = p0.shape[1] // 8
    tm = _pick(t, 208, 16)
    nt = t // tm

    def body(p_ref, y_ref, dg_ref, w_ref, dp_ref, dw_ref, ext_ref):
        i = pl.program_id(1)

        @pl.when(i == 0)
        def _():
            ext_ref[tm:tm + 8, :] = jnp.zeros((8, ec), F32)
            dw_ref[...] = jnp.zeros_like(dw_ref)

        for rows, cols in _strips(tm, ec):
            bg = p_ref[rows, cols].astype(F32)
            z = p_ref[rows, _shift(cols, 3 * ec)].astype(F32)
            yv = y_ref[rows, cols].astype(F32)
            dgv = dg_ref[rows, cols].astype(F32)
            sig = _sigmoid(z)
            sz = z * sig
            dp_ref[rows, _shift(cols, 3 * ec)] = (dgv * bg * yv * (sig * (1.0 + z * (1.0 - sig)))).astype(BF16)
            dp_ref[rows, cols] = (dgv * sz * yv).astype(BF16)
            ext_ref[rows, cols] = dgv * sz * bg
        acc = None
        for rows, cols in _strips(tm, ec):
            if rows.start == 0:
                acc = [jnp.zeros((STRIP_ROWS, cols.stop - cols.start), F32) for _ in range(3)]
            dy = ext_ref[rows, cols]
            dy1 = ext_ref[_shift(rows, 1), cols]
            dy2 = ext_ref[_shift(rows, 2), cols]
            cg = p_ref[rows, _shift(cols, ec)].astype(F32)
            xin = p_ref[rows, _shift(cols, 2 * ec)].astype(F32)
            da = w_ref[0:1, cols] * dy2 + w_ref[1:2, cols] * dy1 + w_ref[2:3, cols] * dy
            dp_ref[rows, _shift(cols, ec)] = (da * xin).astype(BF16)
            dp_ref[rows, _shift(cols, 2 * ec)] = (da * cg).astype(BF16)
            a = cg * xin
            acc = [acc[0] + a * dy2, acc[1] + a * dy1, acc[2] + a * dy]
            if rows.stop == tm:
                for tap in range(3):
                    dw_ref[tap:tap + 1, cols] += jnp.sum(acc[tap], axis=0, keepdims=True)
        ext_ref[tm:tm + 8, :] = ext_ref[0:8, :]

    rev = lambda j, i: (nt - 1 - i, j)
    return _call(
        body, name=name, grid=(2, nt),
        in_specs=[pl.BlockSpec((tm, 4 * ec), rev), pl.BlockSpec((tm, ec), rev), pl.BlockSpec((tm, ec), rev),
                  pl.BlockSpec((3, ec), lambda j, i: (0, j))],
        out_specs=[pl.BlockSpec((tm, 4 * ec), rev), pl.BlockSpec((8, ec), lambda j, i: (0, j))],
        out_shape=[jax.ShapeDtypeStruct((t, 8 * ec), BF16), jax.ShapeDtypeStruct((8, 2 * ec), F32)],
        scratch_shapes=[pltpu.VMEM((tm + 8, ec), F32)], args=(p0, y, dg, conv_w), exchange=exchange)


def _log_sigmoid(x):
    return jnp.minimum(x, 0.0) - jnp.log(1.0 + jnp.exp(-jnp.abs(x)))


def _gates_fwd(graw, graw_t, bias_row, bias_col, name):
    t = graw.shape[0]
    tm = _pick(t, 640, 128)
    cpt = tm // CHUNK
    nh = N_HEADS

    def body(g_ref, gt_ref, br_ref, bc_ref, colf_ref, rowf_ref):
        i = pl.program_id(0)
        gc = g_ref[...] + br_ref[...]
        rows = i * tm + lax.broadcasted_iota(jnp.int32, (tm, 1), 0)
        live = rows >= PAD_ROWS
        lf = jnp.where(live, _log_sigmoid(gc), 0.0)
        li = jnp.where(live, gc, NEG)
        gt = gt_ref[...] + bc_ref[...]
        cols = i * tm + lax.broadcasted_iota(jnp.int32, (1, tm), 1)
        live_t = cols >= PAD_ROWS
        lf_t = jnp.where(live_t, _log_sigmoid(gt), 0.0)
        li_t = jnp.where(live_t, gt, NEG)
        ri = lax.broadcasted_iota(jnp.int32, (CHUNK, CHUNK), 0)
        ci = lax.broadcasted_iota(jnp.int32, (CHUNK, CHUNK), 1)
        lower = (ri >= ci).astype(F32)
        upper = (ri <= ci).astype(F32)
        lane = lax.broadcasted_iota(jnp.int32, (CHUNK, LANES), 1)
        sub = lax.broadcasted_iota(jnp.int32, (8, CHUNK), 0)
        for c in range(cpt):
            sl = slice(c * CHUNK, (c + 1) * CHUNK)
            b_all = jnp.dot(lower, lf[sl, :], precision=HIGHEST, preferred_element_type=F32)
            bt_all = jnp.dot(lf_t[:, sl], upper, precision=HIGHEST, preferred_element_type=F32)
            for h in range(nh):
                b = b_all[:, nh + h:nh + h + 1]
                r = li[sl, h:h + 1] - b
                pre = gc[sl, nh + h:nh + h + 1]
                colf_ref[h, sl, :] = jnp.where(lane == 0, b, jnp.where(lane == 1, r, jnp.where(lane == 2, pre, 0.0)))
                b_row = bt_all[nh + h:nh + h + 1, :]
                r_row = li_t[h:h + 1, sl] - b_row
                rowf_ref[h, c, :, 0:CHUNK] = jnp.where(sub == 0, r_row, jnp.where(sub == 1, b_row, 0.0))
                rowf_ref[h, c, :, CHUNK:LANES] = jnp.zeros((8, LANES - CHUNK), F32)

    return pl.pallas_call(
        body, name=name, grid=(t // tm,),
        in_specs=[pl.BlockSpec((tm, LANES), lambda i: (i, 0)), pl.BlockSpec((16, tm), lambda i: (0, i)),
                  pl.BlockSpec((1, LANES), lambda i: (0, 0)), pl.BlockSpec((16, 1), lambda i: (0, 0))],
        out_specs=[pl.BlockSpec((nh, tm, LANES), lambda i: (0, i, 0)),
                   pl.BlockSpec((nh, cpt, 8, LANES), lambda i: (0, i, 0, 0))],
        out_shape=[jax.ShapeDtypeStruct((nh, t, LANES), F32),
                   jax.ShapeDtypeStruct((nh, t // CHUNK, 8, LANES), F32)],
        compiler_params=_params(),
    )(graw, graw_t, bias_row, bias_col)


def _gates_bwd(gstat, colf, dp1, name):
    nh, t, _ = gstat.shape
    tm = _pick(t, 640, 128)
    cpt = tm // CHUNK
    nt = t // tm
    nc = t // CHUNK
    gate_block = dp1.shape[1] // LANES - 1

    def body(gs_ref, nx_ref, colf_ref, dp_any, dg_ref, db_ref):
        i = pl.program_id(0)
        ri = lax.broadcasted_iota(jnp.int32, (CHUNK, CHUNK), 0)
        ci = lax.broadcasted_iota(jnp.int32, (CHUNK, CHUNK), 1)
        upper = (ri <= ci).astype(F32)
        lane = lax.broadcasted_iota(jnp.int32, (CHUNK, LANES), 1)
        total = jnp.zeros((1, LANES), F32)
        for c in range(cpt):
            sl = slice(c * CHUNK, (c + 1) * CHUNK)
            rows = i * tm + c * CHUNK + lax.broadcasted_iota(jnp.int32, (CHUNK, 1), 0)
            live = rows >= PAD_ROWS
            acc = jnp.zeros((CHUNK, LANES), F32)
            for h in range(nh):
                blk = gs_ref[h, sl, :]
                rev = jnp.dot(upper, blk, precision=HIGHEST, preferred_element_type=F32)
                if c + 1 < cpt:
                    carry = gs_ref[h, (c + 1) * CHUNK:(c + 1) * CHUNK + 1, 2:3]
                else:
                    carry = jnp.where(i == nt - 1, 0.0, nx_ref[h, 0:1, 2:3])
                dlogf = rev[:, 0:1] + carry
                pre = colf_ref[h, sl, 2:3]
                dgf = jnp.where(live, dlogf * (1.0 - _sigmoid(pre)), 0.0)
                dgi = jnp.where(live, blk[:, 1:2], 0.0)
                acc = acc + jnp.where(lane == h, dgi, 0.0) + jnp.where(lane == nh + h, dgf, 0.0)
            dg_ref[sl, :] = acc.astype(BF16)
            total = total + jnp.sum(acc, axis=0, keepdims=True)

        @pl.when(i == 0)
        def _():
            db_ref[...] = jnp.zeros_like(db_ref)

        db_ref[...] += jnp.broadcast_to(total, db_ref.shape)

    return pl.pallas_call(
        body, name=name, grid=(nt,),
        in_specs=[pl.BlockSpec((nh, tm, LANES), lambda i: (0, i, 0)),
                  pl.BlockSpec((nh, CHUNK, LANES), lambda i: (0, jnp.minimum((i + 1) * cpt, nc - 1), 0)),
                  pl.BlockSpec((nh, tm, LANES), lambda i: (0, i, 0)),
                  pl.BlockSpec(memory_space=pl.ANY)],
        out_specs=[pl.BlockSpec((tm, LANES), lambda i: (i, gate_block)), pl.BlockSpec((8, LANES), lambda i: (0, 0))],
        out_shape=[jax.ShapeDtypeStruct(dp1.shape, dp1.dtype), jax.ShapeDtypeStruct((8, LANES), F32)],
        input_output_aliases={3: 0},
        compiler_params=_params(),
    )(gstat, gstat, colf, dp1)


NT_DIMS = (((1,), (1,)), ((), ()))
TN_DIMS = (((0,), (0,)), ((), ()))


def _dot(a, b):
    return jnp.dot(a.astype(BF16), b.astype(BF16), preferred_element_type=F32)


def _dot_nt(a, b):
    return lax.dot_general(a.astype(BF16), b.astype(BF16), NT_DIMS, preferred_element_type=F32)


def _dot_tn(a, b):
    return lax.dot_general(a.astype(BF16), b.astype(BF16), TN_DIMS, preferred_element_type=F32)


def _chunk_gates(colf_ref, rowf_ref, m_prev, width=CHUNK):
    b = colf_ref[:, 0:1]
    rcol = colf_ref[:, 1:2]
    rrow = rowf_ref[0:1, 0:width]
    ri = lax.broadcasted_iota(jnp.int32, (CHUNK, width), 0)
    ci = lax.broadcasted_iota(jnp.int32, (CHUNK, width), 1)
    log_d = jnp.where(ri >= ci, b + rrow, NEG)
    m_row = jnp.maximum(b + m_prev, jnp.max(log_d, axis=-1, keepdims=True))
    dmat = jnp.exp(log_d - m_row)
    inter = jnp.exp(b + m_prev - m_row)
    b_last = b[CHUNK - 1:CHUNK, :]
    log_w = rcol + b_last
    m_new = jnp.maximum(b_last + m_prev, jnp.max(log_w, axis=0, keepdims=True))
    decay = jnp.exp(b_last + m_prev - m_new)
    w = jnp.exp(log_w - m_new)
    return m_row, dmat, inter, m_new, decay, w


def _mlstm_fwd(p1, colf, rowf, dk, dv, name):
    t = p1.shape[0]
    nh = N_HEADS
    nc = t // CHUNK
    hw = 2 * dk + dv
    cps = CHUNKS_PER_STEP if nc % CHUNKS_PER_STEP == 0 else 1
    rows_per_step = cps * CHUNK

    def body(p_ref, colf_ref, rowf_ref, hh_ref, stat_ref, cs_ref, ns_ref, *state):
        c_refs, n_refs, m_refs = state[:nh], state[nh:2 * nh], state[2 * nh:]
        c = pl.program_id(0)

        @pl.when(c == 0)
        def _():
            for ref in state:
                ref[...] = jnp.zeros_like(ref)

        for cc in range(cps):
            rows = pl.ds(cc * CHUNK, CHUNK)
            for h in range(nh):
                head(p_ref.at[rows, pl.ds(h * hw, hw)], colf_ref.at[h, rows], rowf_ref.at[h, cc],
                     hh_ref.at[rows, pl.ds(h * dv, dv)], stat_ref.at[h, rows], cs_ref.at[h, cc],
                     ns_ref.at[h, cc], c_refs[h], n_refs[h], m_refs[h])

    def head(p_ref, colf_ref, rowf_ref, hh_ref, stat_ref, cs_ref, ns_ref, c_ref, n_ref, m_ref):
        m_prev = m_ref[...]
        n_prev = n_ref[...]
        c_prev = c_ref[...]
        cs_ref[...] = c_prev.astype(BF16)
        sub = lax.broadcasted_iota(jnp.int32, (8, dk), 0)
        ns_ref[...] = jnp.where(sub == 0, n_prev, jnp.where(sub == 1, m_prev, 0.0))

        q = p_ref[:, 0:dk]
        k = p_ref[:, dk:2 * dk]
        v = p_ref[:, 2 * dk:2 * dk + dv]
        m_row, dmat, inter, m_new, decay, w = _chunk_gates(colf_ref, rowf_ref, m_prev)
        s = _dot_nt(q, k) * dmat
        num = _dot(s, v) + inter * _dot(q, c_prev)
        den = jnp.sum(s, axis=-1, keepdims=True) + inter * jnp.sum(q.astype(F32) * n_prev, axis=-1, keepdims=True)
        denom = jnp.maximum(jnp.abs(den), jnp.exp(-m_row))
        hh_ref[...] = num * (1.0 / denom)
        lane = lax.broadcasted_iota(jnp.int32, (CHUNK, LANES), 1)
        stat_ref[...] = jnp.where(lane == 0, den, 0.0)

        wk = w * k.astype(F32)
        c_ref[...] = decay * c_prev + _dot_tn(wk, v)
        n_ref[...] = decay * n_prev + jnp.sum(wk, axis=0, keepdims=True)
        m_ref[...] = m_new

    return pl.pallas_call(
        body, name=name, grid=(nc // cps,),
        in_specs=[pl.BlockSpec((rows_per_step, nh * hw), lambda c: (c, 0)),
                  pl.BlockSpec((nh, rows_per_step, LANES), lambda c: (0, c, 0)),
                  pl.BlockSpec((nh, cps, 8, LANES), lambda c: (0, c, 0, 0))],
        out_specs=[pl.BlockSpec((rows_per_step, nh * dv), lambda c: (c, 0)),
                   pl.BlockSpec((nh, rows_per_step, LANES), lambda c: (0, c, 0)),
                   pl.BlockSpec((nh, cps, dk, dv), lambda c: (0, c, 0, 0)),
                   pl.BlockSpec((nh, cps, 8, dk), lambda c: (0, c, 0, 0))],
        out_shape=[jax.ShapeDtypeStruct((t, nh * dv), F32), jax.ShapeDtypeStruct((nh, t, LANES), F32),
                   jax.ShapeDtypeStruct((nh, nc, dk, dv), BF16), jax.ShapeDtypeStruct((nh, nc, 8, dk), F32)],
        scratch_shapes=([pltpu.VMEM((dk, dv), F32)] * nh + [pltpu.VMEM((1, dk), F32)] * nh
                        + [pltpu.VMEM((1, 1), F32)] * nh),
        compiler_params=_params(),
    )(p1, colf, rowf)


def _head_gate_fwd(hh, p1, head_w, dv, name):
    t = hh.shape[0]
    nh = N_HEADS
    e = nh * dv
    tm = _pick(t, 416, 16)

    def body(hh_ref, oz_ref, w_ref, hm_ref):
        for h in range(nh):
            cols = slice(h * dv, (h + 1) * dv)
            x = hh_ref[:, cols]
            o = oz_ref[:, 2 * h * dv:(2 * h + 1) * dv].astype(F32)
            z = oz_ref[:, (2 * h + 1) * dv:(2 * h + 2) * dv].astype(F32)
            r = lax.rsqrt(jnp.mean(x * x, axis=-1, keepdims=True) + RMS_EPS)
            hn = (x * r) * w_ref[:, cols]
            hm_ref[:, cols] = (hn * _sigmoid(o) * (z * _sigmoid(z))).astype(BF16)

    return pl.pallas_call(
        body, name=name, grid=(t // tm,),
        in_specs=[pl.BlockSpec((tm, e), lambda i: (i, 0)), pl.BlockSpec((tm, 2 * e), lambda i: (i, 1)),
                  pl.BlockSpec((1, e), lambda i: (0, 0))],
        out_specs=pl.BlockSpec((tm, e), lambda i: (i, 0)),
        out_shape=jax.ShapeDtypeStruct((t, e), BF16),
        compiler_params=_params(),
    )(hh, p1, head_w)


def _mlstm_bwd(p1, colf, rowf, head_w, hh, stat, csave, nsave, dhm, n_cols, dk, dv, name, exchange=None):
    t = p1.shape[0]
    nh = N_HEADS
    nc = t // CHUNK
    hw = 2 * dk + dv
    cps = CHUNKS_PER_STEP if nc % CHUNKS_PER_STEP == 0 else 1
    rows_per_step = cps * CHUNK

    def body(p_ref, colf_ref, rowf_ref, w_ref, hh_ref, stat_ref, cs_ref, ns_ref, dhm_ref,
             dp_ref, gs_ref, dw_ref, *state):
        dc_refs, dn_refs, dwacc_refs = state[:nh], state[nh:2 * nh], state[2 * nh:]
        c = pl.program_id(0)

        @pl.when(c == 0)
        def _():
            for ref in state:
                ref[...] = jnp.zeros_like(ref)

        for cc in reversed(range(cps)):
            rows = pl.ds(cc * CHUNK, CHUNK)
            for h in range(nh):
                cols = pl.ds(h * dv, dv)
                oz = pl.ds(nh * hw + h * 2 * dv, 2 * dv)
                head(p_ref.at[rows, pl.ds(h * hw, hw)], p_ref.at[rows, oz], colf_ref.at[h, rows], rowf_ref.at[h, cc],
                     w_ref.at[:, cols], hh_ref.at[rows, cols], stat_ref.at[h, rows], cs_ref.at[h, cc],
                     ns_ref.at[h, cc], dhm_ref.at[rows, cols], dp_ref.at[rows, pl.ds(h * hw, hw)],
                     dp_ref.at[rows, oz], gs_ref.at[h, rows], dwacc_refs[h], dc_refs[h], dn_refs[h])

        @pl.when(c == nc // cps - 1)
        def _():
            for h in range(nh):
                dw_ref[h] = dwacc_refs[h][...]

    def head(p_ref, oz_ref, colf_ref, rowf_ref, w_ref, hh_ref, stat_ref, cs_ref, ns_ref, dhm_ref,
             dp_ref, doz_ref, gs_ref, dw_ref, dc_ref, dn_ref):
        q = p_ref[:, 0:dk]
        k = p_ref[:, dk:2 * dk]
        v = p_ref[:, 2 * dk:2 * dk + dv]
        o = oz_ref[:, 0:dv].astype(F32)
        z = oz_ref[:, dv:2 * dv].astype(F32)
        qf = q.astype(F32)
        kf = k.astype(F32)
        n_prev = ns_ref[0:1, :]
        m_prev = ns_ref[1:2, 0:1]
        c_prev = cs_ref[...]
        m_row, dmat, inter, m_new, decay, w = _chunk_gates(colf_ref, rowf_ref, m_prev)

        hh = hh_ref[...]
        dhm_v = dhm_ref[...].astype(F32)
        so = _sigmoid(o)
        sg = _sigmoid(z)
        sz = z * sg
        r = lax.rsqrt(jnp.mean(hh * hh, axis=-1, keepdims=True) + RMS_EPS)
        hn = (hh * r) * w_ref[...]
        dhn = dhm_v * so * sz
        doz_ref[:, 0:dv] = (dhm_v * hn * sz * (so * (1.0 - so))).astype(BF16)
        doz_ref[:, dv:2 * dv] = (dhm_v * hn * so * (sg * (1.0 + z * (1.0 - sg)))).astype(BF16)
        dw_ref[...] += jnp.broadcast_to(jnp.sum(dhn * (hh * r), axis=0, keepdims=True), dw_ref.shape)
        gwn = dhn * w_ref[...]
        dhh = r * gwn - hh * ((r * r * r) * jnp.mean(gwn * hh, axis=-1, keepdims=True))

        den = stat_ref[:, 0:1]
        floor = jnp.exp(-m_row)
        denom = jnp.maximum(jnp.abs(den), floor)
        inv = 1.0 / denom
        dnum = dhh * inv
        hdot = jnp.sum(dhh * hh, axis=-1, keepdims=True)
        dden = jnp.where(jnp.abs(den) > floor, -(hdot * inv) * jnp.sign(den), 0.0)
        s = _dot_nt(q, k) * dmat
        dqk = (_dot_nt(dnum, v) + dden) * dmat
        dc_new = dc_ref[...]
        dn_new = dn_ref[...]
        idd = inter * dden
        dq = _dot(dqk, k) + inter * _dot_nt(dnum, c_prev) + idd * n_prev
        dkv = _dot_tn(dqk, q) + w * (_dot_nt(v, dc_new) + dn_new)
        dvv = _dot_tn(s, dnum) + w * _dot(k, dc_new)
        dp_ref[:, 0:dk] = dq.astype(BF16)
        dp_ref[:, dk:2 * dk] = dkv.astype(BF16)
        dp_ref[:, 2 * dk:2 * dk + dv] = dvv.astype(BF16)
        qdq = jnp.sum(qf * dq, axis=-1, keepdims=True)
        kdk = jnp.sum(kf * dkv, axis=-1, keepdims=True)
        dc_prev = decay * dc_new + _dot_tn(inter * qf, dnum)
        dn_prev = decay * dn_new + jnp.sum(idd * qf, axis=0, keepdims=True)
        dc_ref[...] = dc_prev
        dn_ref[...] = dn_prev
        cross = (jnp.sum(jnp.sum(c_prev.astype(F32) * dc_prev, axis=-1, keepdims=True), axis=0, keepdims=True)
                 + jnp.sum(n_prev * dn_prev, axis=-1, keepdims=True))
        lane = lax.broadcasted_iota(jnp.int32, (CHUNK, LANES), 1)
        gs_ref[...] = jnp.where(lane == 0, qdq - kdk, jnp.where(lane == 1, kdk, jnp.where(lane == 2, cross, 0.0)))

    ns = nc // cps
    rc = lambda c: (ns - 1 - c, 0)
    rc3 = lambda c: (0, ns - 1 - c, 0)
    rc4 = lambda c: (0, ns - 1 - c, 0, 0)
    return _call(
        body, name=name, grid=(ns,),
        in_specs=[pl.BlockSpec((rows_per_step, nh * (hw + 2 * dv)), rc),
                  pl.BlockSpec((nh, rows_per_step, LANES), rc3),
                  pl.BlockSpec((nh, cps, 8, LANES), rc4),
                  pl.BlockSpec((1, nh * dv), lambda c: (0, 0)),
                  pl.BlockSpec((rows_per_step, nh * dv), rc),
                  pl.BlockSpec((nh, rows_per_step, LANES), rc3),
                  pl.BlockSpec((nh, cps, dk, dv), rc4),
                  pl.BlockSpec((nh, cps, 8, dk), rc4),
                  pl.BlockSpec((rows_per_step, nh * dv), rc)],
        out_specs=[pl.BlockSpec((rows_per_step, nh * (hw + 2 * dv)), rc),
                   pl.BlockSpec((nh, rows_per_step, LANES), rc3),
                   pl.BlockSpec((nh, 8, dv), lambda c: (0, 0, 0))],
        out_shape=[jax.ShapeDtypeStruct((t, n_cols), BF16), jax.ShapeDtypeStruct((nh, t, LANES), F32),
                   jax.ShapeDtypeStruct((nh, 8, dv), F32)],
        scratch_shapes=([pltpu.VMEM((dk, dv), F32)] * nh + [pltpu.VMEM((1, dk), F32)] * nh
                        + [pltpu.VMEM((8, dv), F32)] * nh),
        args=(p1, colf, rowf, head_w, hh, stat, csave, nsave, dhm), exchange=exchange)


def _position():
    return lax.axis_index("x"), lax.axis_index("y"), lax.axis_index("c")


class _Exchange:
    def __init__(self, ins, out_shapes, start, mid, finish):
        n = len(ins)
        self.ins, self.out_shapes = list(ins), list(out_shapes)
        self.start, self.mid, self.finish = start, mid, finish
        self.scratch = [pltpu.SemaphoreType.DMA((n, 7)), pltpu.SemaphoreType.DMA((n, 7)),
                        pltpu.SemaphoreType.DMA((n,))]


def _gather_exchange(shards):
    n = len(shards)

    def plan(ins, outs, sems):
        send_sems, recv_sems, local_sems = sems
        x, y, c = _position()
        me, sibling = (x, y, c), (x, y, 1 - c)
        chips = [(1 - x, y), (x, 1 - y), (1 - x, 1 - y)]

        def copy(a, k, block, to, src=None):
            px, py, pc = block
            dst = outs[a].at[4 * px + 2 * py + pc]
            return pltpu.make_async_remote_copy(
                src_ref=dst if src is None else src, dst_ref=dst,
                send_sem=send_sems.at[a, k], recv_sem=recv_sems.at[a, k],
                device_id=to, device_id_type=MESH)

        def mine():
            return [pltpu.make_async_copy(ins[a], outs[a].at[4 * x + 2 * y + c], local_sems.at[a])
                    for a in range(n)]

        def first():
            out = []
            for a in range(n):
                out.append(copy(a, 0, me, sibling, src=ins[a]))
                out += [copy(a, 1 + j, me, (*chip, c), src=ins[a]) for j, chip in enumerate(chips)]
            return out

        def ici_in():
            return [copy(a, 1 + j, (*chip, c), me) for j, chip in enumerate(chips) for a in range(n)]

        def passed():
            return [copy(a, 4 + j, (*chip, c), sibling) for j, chip in enumerate(chips) for a in range(n)]

        def d2d_in():
            return ([copy(a, 0, sibling, me) for a in range(n)]
                    + [copy(a, 4 + j, (*chip, 1 - c), me) for j, chip in enumerate(chips) for a in range(n)])

        return mine, first, ici_in, passed, d2d_in

    def start(ins, outs, sems):
        mine, first, _, _, _ = plan(ins, outs, sems)
        for cp in mine() + first():
            cp.start()

    def mid(ins, outs, sems):
        _, _, ici_in, passed, _ = plan(ins, outs, sems)
        for arrived, onward in zip(ici_in(), passed()):
            arrived.wait_recv()
            onward.start()

    def finish(ins, outs, sems):
        mine, first, _, passed, d2d_in = plan(ins, outs, sems)
        for cp in d2d_in():
            cp.wait_recv()
        for cp in first() + passed():
            cp.wait_send()
        for cp in mine():
            cp.wait()

    shapes = [jax.ShapeDtypeStruct((N_DEV,) + s.shape, s.dtype) for s in shards]
    return _Exchange(shards, shapes, start, mid, finish)


def _scatter_exchange(fulls):
    n = len(fulls)

    def plan(ins, outs, sems):
        send_sems, recv_sems, local_sems = sems
        x, y, c = _position()
        my_slot = 4 * x + 2 * y + c

        def mine():
            return [pltpu.make_async_copy(ins[a].at[my_slot], outs[a].at[my_slot], local_sems.at[a])
                    for a in range(n)]

        def remote(arriving):
            out = []
            for kk in (1, 2, 4, 6, 3, 5, 7):
                kx, ky, kc = (kk >> 2) & 1, (kk >> 1) & 1, kk & 1
                px = 1 - x if kx else x
                py = 1 - y if ky else y
                pc = 1 - c if kc else c
                peer_slot = 4 * px + 2 * py + pc
                for a in range(n):
                    out.append(pltpu.make_async_remote_copy(
                        src_ref=ins[a].at[peer_slot], dst_ref=outs[a].at[peer_slot if arriving else my_slot],
                        send_sem=send_sems.at[a, kk - 1], recv_sem=recv_sems.at[a, kk - 1],
                        device_id=(px, py, pc), device_id_type=MESH))
            return out

        return mine, remote

    def start(ins, outs, sems):
        mine, remote = plan(ins, outs, sems)
        for cp in mine() + remote(False):
            cp.start()

    def finish(ins, outs, sems):
        mine, remote = plan(ins, outs, sems)
        for cp in remote(True):
            cp.wait_recv()
        for cp in remote(False):
            cp.wait_send()
        for cp in mine():
            cp.wait()

    shapes = [jax.ShapeDtypeStruct(f.shape, f.dtype) for f in fulls]
    return _Exchange(fulls, shapes, start, None, finish)


def _run_exchange(ex, name):
    n_in, n_out = len(ex.ins), len(ex.out_shapes)

    def body(*refs):
        ins, outs, sems = refs[:n_in], refs[n_in:n_in + n_out], refs[n_in + n_out:]
        ex.start(ins, outs, sems)
        if ex.mid is not None:
            ex.mid(ins, outs, sems)
        ex.finish(ins, outs, sems)

    any_spec = pl.BlockSpec(memory_space=pl.ANY)
    return pl.pallas_call(
        body, name=name,
        in_specs=[any_spec] * n_in, out_specs=[any_spec] * n_out,
        out_shape=ex.out_shapes, scratch_shapes=ex.scratch,
    )(*ex.ins)


def _natural_to_head_major(n0, dk, dv):
    nh = N_HEADS
    qk, e, hw = nh * dk, nh * dv, 2 * dk + dv
    if n0 < qk:
        h, off = divmod(n0, dk)
        return h * hw + off, True
    n1 = n0 - qk
    if n1 < qk:
        h, off = divmod(n1, dk)
        return h * hw + dk + off, False
    part, n3 = divmod(n1 - qk, e)
    h, off = divmod(n3, dv)
    if part == 0:
        return h * hw + 2 * dk + off, False
    return nh * hw + h * 2 * dv + (part - 1) * dv + off, False


def _mlstm_weight_layout(w_blocks, dk, dv, name):
    _, d, blk = w_blocks.shape
    nh = N_HEADS
    hw = 2 * dk + 3 * dv
    n_tiles = nh * hw // LANES
    tiles_per_block = d // LANES
    tr = _pick(d, 128, 16)
    scale = dk ** -0.5

    def body(w_ref, p_ref, g_ref):
        lane = lax.broadcasted_iota(jnp.int32, (tr, LANES), 1)

        def tile(s, r):
            return w_ref[s, :, r * LANES:(r + 1) * LANES].astype(F32)

        def last_col(s):
            return w_ref[s, :, d:d + 1].astype(F32)

        for tn in range(n_tiles):
            s, r = divmod(tn, tiles_per_block)
            if s == 0:
                val = tile(0, r)
            elif r == 0:
                from_prev = pltpu.roll(tile(s - 1, tiles_per_block - 1), s - 1, 1)
                from_here = pltpu.roll(tile(s, 0), s, 1)
                val = jnp.where(lane < s - 1, from_prev, jnp.where(lane == s - 1, last_col(s - 1), from_here))
            else:
                slab = jnp.concatenate([tile(s, r - 1), tile(s, r)], axis=1)
                val = pltpu.roll(slab, s, 1)[:, LANES:]
            at, is_q = _natural_to_head_major(tn * LANES, dk, dv)
            if is_q:
                val = val * scale
            p_ref[:, at:at + LANES] = val.astype(BF16)
        n_gate = 2 * nh
        s = N_DEV - 1
        gates = pltpu.roll(tile(s, tiles_per_block - 1), n_gate - 1, 1)
        gates = jnp.where(lane < n_gate - 1, gates, jnp.where(lane == n_gate - 1, last_col(s), 0.0)).astype(BF16)
        g_ref[...] = gates
        p_ref[:, nh * hw:nh * hw + LANES] = gates

    return pl.pallas_call(
        body, name=name, grid=(d // tr,),
        in_specs=[pl.BlockSpec((N_DEV, tr, blk), lambda i: (0, i, 0))],
        out_specs=[pl.BlockSpec((tr, nh * hw + LANES), lambda i: (i, 0)), pl.BlockSpec((tr, LANES), lambda i: (i, 0))],
        out_shape=[jax.ShapeDtypeStruct((d, nh * hw + LANES), BF16), jax.ShapeDtypeStruct((d, LANES), BF16)],
        compiler_params=_params(),
    )(w_blocks)


def _mlstm_grad_layout(g_p1, row0, n_rows, dk, dv, name):
    d = g_p1.shape[0]
    nh = N_HEADS
    hw = 2 * dk + 3 * dv
    n_tiles = nh * hw // LANES
    tiles_per_block = d // LANES
    tr = _pick(d, 128, 16)
    scale = dk ** -0.5

    def body(p_ref, o_ref):
        def natural(tn):
            if tn == n_tiles:
                return p_ref[:, nh * hw:nh * hw + LANES].astype(F32)
            at, is_q = _natural_to_head_major(tn * LANES, dk, dv)
            val = p_ref[:, at:at + LANES].astype(F32)
            return val * scale if is_q else val

        for s in range(N_DEV):
            for r in range(tiles_per_block):
                tn = s * tiles_per_block + r
                if s == 0:
                    val = natural(tn)
                else:
                    slab = jnp.concatenate([natural(tn), natural(tn + 1)], axis=1)
                    val = pltpu.roll(slab, 2 * LANES - s, 1)[:, :LANES]
                o_ref[s, :, r * LANES:(r + 1) * LANES] = val.astype(BF16)
            o_ref[s, :, d:d + 1] = natural((s + 1) * tiles_per_block)[:, s:s + 1].astype(BF16)

    first = row0 // tr
    return pl.pallas_call(
        body, name=name, grid=(n_rows // tr,),
        in_specs=[pl.BlockSpec((tr, nh * hw + LANES), lambda i: (i + first, 0))],
        out_specs=pl.BlockSpec((N_DEV, tr, d + 1), lambda i: (0, i, 0)),
        out_shape=jax.ShapeDtypeStruct((N_DEV, n_rows, d + 1), BF16),
        compiler_params=_params(),
    )(g_p1)


def _adamw_math(w, g, m, v):
    m = ADAM_B1 * m + (1.0 - ADAM_B1) * g
    v = ADAM_B2 * v + (1.0 - ADAM_B2) * (g * g)
    m_hat = m / (1.0 - ADAM_B1 ** ADAM_STEP)
    v_hat = v / (1.0 - ADAM_B2 ** ADAM_STEP)
    delta = -ADAM_LR * (m_hat / (jnp.sqrt(v_hat) + ADAM_EPS) + ADAM_WD * w)
    return delta, m, v


def _adamw_sharded(parts, w, m, v, name):
    n_parts = len(parts)
    _, rk, c = parts[0].shape
    r = n_parts * rk
    tr = _pick(rk, 128, 8)
    per = rk // tr

    def body(*refs):
        p_refs = refs[:n_parts]
        w_ref, m_ref, v_ref, g_ref, d_ref, nm_ref, nv_ref, gsum_ref = refs[n_parts:]
        i = pl.program_id(0)
        for k, p_ref in enumerate(p_refs):
            @pl.when(i // per == k)
            def _(p_ref=p_ref):
                g = p_ref[0].astype(F32)
                for s in range(1, N_DEV):
                    g = g + p_ref[s].astype(F32)
                gsum_ref[...] = g
        g = gsum_ref[...]
        delta, m_new, v_new = _adamw_math(w_ref[...], g, m_ref[...], v_ref[...])
        g_ref[...] = g
        d_ref[...] = delta
        nm_ref[...] = m_new
        nv_ref[...] = v_new

    def part_spec(k):
        return pl.BlockSpec((N_DEV, tr, c), lambda i: (0, jnp.clip(i - k * per, 0, per - 1), 0))

    blk = pl.BlockSpec((tr, c), lambda i: (i, 0))
    return pl.pallas_call(
        body, name=name, grid=(r // tr,),
        in_specs=[part_spec(k) for k in range(n_parts)] + [blk, blk, blk],
        out_specs=[blk] * 4,
        out_shape=[jax.ShapeDtypeStruct((r, c), F32)] * 4,
        scratch_shapes=[pltpu.VMEM((tr, c), F32)],
        compiler_params=_params(),
    )(*parts, w, m, v)


def _sum_devices(parts, name):
    _, r, c = parts.shape

    def body(p_ref, o_ref):
        g = p_ref[0]
        for s in range(1, N_DEV):
            g = g + p_ref[s]
        o_ref[...] = g

    return pl.pallas_call(
        body, name=name, out_shape=jax.ShapeDtypeStruct((r, c), F32), compiler_params=_params(),
    )(parts)


def _adamw_small(gs, ws, ms, vs, name):
    n = len(gs)

    def body(*refs):
        g_refs, w_refs, m_refs, v_refs = refs[:n], refs[n:2 * n], refs[2 * n:3 * n], refs[3 * n:4 * n]
        d_refs, nm_refs, nv_refs = refs[4 * n:5 * n], refs[5 * n:6 * n], refs[6 * n:7 * n]
        for a in range(n):
            delta, m_new, v_new = _adamw_math(w_refs[a][...], g_refs[a][...], m_refs[a][...], v_refs[a][...])
            d_refs[a][...] = delta
            nm_refs[a][...] = m_new
            nv_refs[a][...] = v_new

    shapes = [jax.ShapeDtypeStruct(w.shape, F32) for w in ws]
    outs = pl.pallas_call(
        body, name=name, out_shape=shapes * 3, compiler_params=_params(),
    )(*gs, *ws, *ms, *vs)
    return outs[:n], outs[n:2 * n], outs[2 * n:]


def _pad_rows(a, rows):
    return jnp.pad(a, ((0, rows - a.shape[0]), (0, 0)))


def kernel(x, meta_tokens, norm_w, conv_in_w, conv_w, conv_out_w, mlstm_in_w, mlstm_gate_b, mlstm_head_norm_w, mlstm_out_w, final_norm_w, loss_target, m_meta_tokens, m_norm_w, m_conv_in_w, m_conv_w, m_conv_out_w, m_mlstm_in_w, m_mlstm_gate_b, m_mlstm_head_norm_w, m_mlstm_out_w, m_final_norm_w, v_meta_tokens, v_norm_w, v_conv_in_w, v_conv_w, v_conv_out_w, v_mlstm_in_w, v_mlstm_gate_b, v_mlstm_head_norm_w, v_mlstm_out_w, v_final_norm_w):
    seq, d = x.shape[1], x.shape[2]
    t = seq + LEAD
    e = 2 * d
    ec = e // 2
    nh = N_HEADS
    dv = e // nh
    dk = dv // 2
    qk = nh * dk
    hw = 2 * dk + 3 * dv
    n_in = 2 * qk + 3 * e + 2 * nh
    n_in_s = n_in // N_DEV
    me = 4 * lax.axis_index("x") + 2 * lax.axis_index("y") + lax.axis_index("c")
    tm = _pick(t, 832, 16)
    tm_in = _pick(t, 1664, 16)
    tkw = _pick(t, 2080, 16)

    small = jnp.concatenate([
        meta_tokens,
        _pad_rows(conv_w[0].reshape(3 * (e // N_DEV) // LANES, LANES), 8),
        _pad_rows(mlstm_head_norm_w[0].reshape((e // N_DEV) // LANES, LANES), 8),
    ], axis=0) if d // N_DEV == LANES else None
    assert small is not None, "the packed small-weight block assumes d_model / 8 == 128"
    (small_g,) = _run_exchange(_gather_exchange([small]), "gather_small_weights")
    meta_full = jnp.transpose(small_g[:, 0:N_META, :], (1, 0, 2)).reshape(N_META, d)
    cw_rows = 3 * (e // N_DEV) // LANES
    conv_w_full = jnp.transpose(
        small_g[:, N_META:N_META + cw_rows, :].reshape(N_DEV, 3, e // N_DEV), (1, 0, 2)).reshape(3, e)
    hn_rows = (e // N_DEV) // LANES
    head_w_full = small_g[:, N_META + 8:N_META + 8 + hn_rows, :].reshape(1, e)

    tgt = loss_target[0]

    ci_map = lambda blk: 2 * (blk % 4) + blk // 4
    (h0, u0), (w_ci,) = _rms_first(x[0], meta_full, norm_w[0:1], "rms0",
                                   exchange=_gather_exchange([conv_in_w[0].astype(BF16)]))
    p0, (w_co, w_mi) = _matmul(
        u0, w_ci, form="nn", m=t, n=8 * ec, kdim=d, tm=tm_in, tn=ec, tk=d, out_dtype=BF16, name="conv_in",
        b_spec=pl.BlockSpec((None, d, ec), lambda i, j, k: (ci_map(j), 0, 0)),
        exchange=_gather_exchange([conv_out_w[0].astype(BF16), mlstm_in_w[0].astype(BF16)]))
    w_co = w_co.reshape(e, d)
    (g0, y0), (w_mo,) = _conv_gate_fwd(p0, conv_w_full, "conv_gate",
                                       exchange=_gather_exchange([mlstm_out_w[0].astype(BF16)]))
    w_mo = w_mo.reshape(e, d)
    h1 = _matmul(g0, w_co, form="nn", m=t, n=d, kdim=e, tm=tm, tn=d, tk=e, out_dtype=F32, name="conv_out",
                 residual=h0)

    assert n_in_s == d + 1 and w_mi.shape == (N_DEV, d, d + 1)
    w_p1, w_gate = _mlstm_weight_layout(w_mi, dk, dv, "mlstm_w_layout")
    w_gate_t = jnp.transpose(w_gate[:, 0:16])
    bias_row = jnp.pad(mlstm_gate_b, ((0, 0), (0, LANES - 2 * nh)))
    bias_col = jnp.pad(mlstm_gate_b.T, ((0, 16 - 2 * nh), (0, 0)))

    u1 = _rms_fwd(h1, norm_w[1:2], "rms1")
    p1 = _matmul(u1, w_p1, form="nn", m=t, n=nh * hw, kdim=d, tm=tm_in, tn=_pick(nh * hw, 1024, LANES), tk=d,
                 out_dtype=BF16, name="mlstm_in")
    graw = _matmul(u1, w_gate, form="nn", m=t, n=LANES, kdim=d, tm=tm, tn=LANES, tk=d, out_dtype=F32,
                   name="gates_col")
    graw_t = _matmul(w_gate_t, u1, form="nt", m=16, n=t, kdim=d, tm=16, tn=_pick(t, 1664, LANES), tk=d,
                     out_dtype=F32, name="gates_row")
    colf, rowf = _gates_fwd(graw, graw_t, bias_row, bias_col, "gates_fwd")
    hh, stat, csave, nsave = _mlstm_fwd(p1, colf, rowf, dk, dv, "mlstm_fwd")
    hm = _head_gate_fwd(hh, p1, head_w_full, dv, "head_gate")
    h2 =_matmul(hm, w_mo, form="nn", m=t, n=d, kdim=e, tm=tm, tn=d, tk=e, out_dtype=F32, name="mlstm_out",
                 residual=h1)

    dh2, dwf, loss_part = _final_loss(h2, final_norm_w.reshape(1, d), tgt, "final_loss")

    dhm = _matmul(dh2, w_mo, form="nt", m=t, n=e, kdim=d, tm=tm, tn=_pick(e, 2048, LANES), tk=d, out_dtype=BF16,
                  name="mlstm_out_dx")
    g_mo = _matmul(hm, dh2, form="tn", m=e, n=d, kdim=t, tm=_pick(e, 1024, LANES), tn=d, tk=tkw, out_dtype=BF16,
                   name="mlstm_out_dw")
    n_p1 = nh * hw + LANES
    (dp1, gstat, dhead), (r_mo,) = _mlstm_bwd(
        p1, colf, rowf, head_w_full, hh, stat, csave, nsave, dhm, n_p1, dk, dv, "mlstm_bwd",
        exchange=_scatter_exchange([g_mo.reshape(N_DEV, e // N_DEV, d)]))
    dp1, dbias = _gates_bwd(gstat, colf, dp1, "gates_bwd")
    du1 = _matmul(dp1, w_p1, form="nt", m=t, n=d, kdim=n_p1, tm=tm_in, tn=d, tk=_pick(n_p1, 2048, LANES),
                  out_dtype=F32, name="mlstm_in_dx")
    g_p1 = _matmul(u1, dp1, form="tn", m=d, n=n_p1, kdim=t, tm=d, tn=_pick(n_p1, 2048, LANES), tk=tkw,
                   out_dtype=BF16, name="mlstm_in_dw")
    dh1, dnw1 = _rms_bwd(h1, norm_w[1:2], du1, dh2, "rms1_bwd")

    g_mi_a = _mlstm_grad_layout(g_p1, 0, d // 2, dk, dv, "mlstm_g_layout_a")
    g_mi_b = _mlstm_grad_layout(g_p1, d // 2, d // 2, dk, dv, "mlstm_g_layout_b")

    dg0 =_matmul(dh1, w_co, form="nt", m=t, n=e, kdim=d, tm=tm, tn=_pick(e, 2048, LANES), tk=d, out_dtype=BF16,
                  name="conv_out_dx")
    g_co = _matmul(g0, dh1, form="tn", m=e, n=d, kdim=t, tm=_pick(e, 1024, LANES), tn=d, tk=tkw, out_dtype=BF16,
                   name="conv_out_dw")
    (dp0, dconv), (r_co, r_mi_a) = _conv_gate_bwd(
        p0, y0, dg0, conv_w_full, "conv_gate_bwd",
        exchange=_scatter_exchange([g_co.reshape(N_DEV, e // N_DEV, d), g_mi_a]))
    g_ci, (r_mi_b,) = _matmul(
        u0, dp0, form="tn", m=d, n=8 * ec, kdim=t, tm=d, tn=ec, tk=tkw, out_dtype=BF16, name="conv_in_dw",
        out_shape=(N_DEV, d, ec), out_spec=pl.BlockSpec((None, d, ec), lambda i, j, k: (ci_map(j), 0, 0)),
        exchange=_scatter_exchange([g_mi_b]))
    du0, (r_ci,) = _matmul(
        dp0, w_ci, form="nt", m=t, n=d, kdim=8 * ec, tm=tm_in, tn=d, tk=ec, out_dtype=F32, name="conv_in_dx",
        b_spec=pl.BlockSpec((None, d, ec), lambda i, j, k: (ci_map(k), 0, 0)),
        exchange=_scatter_exchange([g_ci]))
    grad_x, dmeta, dnw0 = _rms_bwd_first(h0, norm_w[0:1], du0, dh1, "rms0_bwd")
    grad_x = grad_x[None]

    row8 = lax.broadcasted_iota(jnp.int32, (8, 1), 0)
    loss_wide = jnp.pad(loss_part, ((0, 0), (0, d - LANES)))
    payload = jnp.concatenate([
        jnp.where(row8 == 0, dnw0, jnp.where(row8 == 1, dnw1, 0.0)),
        jnp.where(row8 == 0, dwf, jnp.where(row8 == 1, loss_wide, 0.0)),
        jnp.where(row8 == 0, jnp.pad(dbias, ((0, 0), (0, d - LANES))), 0.0),
        dmeta,
        _pad_rows(dconv[0:3].reshape(3 * e // d, d), 8),
        _pad_rows(dhead[:, 0, :].reshape(e // d, d), 8),
    ], axis=0)
    (payload_g,) = _run_exchange(_gather_exchange([payload]), "gather_small_grads")
    tot = _sum_devices(payload_g, "sum_small_grads")

    loss = tot[9, 0]
    g_norm = tot[0:2]
    g_final = tot[8]
    g_gate_b = tot[16:17, 0:2 * nh]
    g_meta = lax.dynamic_slice(tot[24:24 + N_META], (0, me * (d // N_DEV)), (N_META, d // N_DEV))
    g_conv_w = lax.dynamic_slice(tot[40:40 + 3 * e // d].reshape(3, e), (0, me * (e // N_DEV)), (3, e // N_DEV))
    g_head = lax.dynamic_slice(tot[48:48 + e // d].reshape(1, e), (0, me * (e // N_DEV)), (1, e // N_DEV))

    g1, d1, nm1, nv1 = _adamw_sharded([r_ci], conv_in_w[0], m_conv_in_w[0], v_conv_in_w[0], "adamw_conv_in")
    g2, d2, nm2, nv2 = _adamw_sharded([r_co], conv_out_w[0], m_conv_out_w[0], v_conv_out_w[0], "adamw_conv_out")
    g3, d3, nm3, nv3 = _adamw_sharded([r_mi_a, r_mi_b], mlstm_in_w[0], m_mlstm_in_w[0], v_mlstm_in_w[0],
                                      "adamw_mlstm_in")
    g4, d4, nm4, nv4 = _adamw_sharded([r_mo], mlstm_out_w[0], m_mlstm_out_w[0], v_mlstm_out_w[0],
                                      "adamw_mlstm_out")

    small_g = [g_meta, g_norm, g_conv_w, g_gate_b, g_head, g_final.reshape(1, d)]
    small_w = [meta_tokens, norm_w, conv_w[0], mlstm_gate_b, mlstm_head_norm_w, final_norm_w.reshape(1, d)]
    small_m = [m_meta_tokens, m_norm_w, m_conv_w[0], m_mlstm_gate_b, m_mlstm_head_norm_w, m_final_norm_w.reshape(1, d)]
    small_v = [v_meta_tokens, v_norm_w, v_conv_w[0], v_mlstm_gate_b, v_mlstm_head_norm_w, v_final_norm_w.reshape(1, d)]
    sd, snm, snv = _adamw_small(small_g, small_w, small_m, small_v, "adamw_small")

    def order(meta, norm, cin, cw, cout, min_, gb, hn, mout, fin):
        return (meta, norm, cin[None], cw[None], cout[None], min_[None], gb, hn, mout[None], fin.reshape(d))

    grads = order(g_meta, g_norm, g1, g_conv_w, g2, g3, g_gate_b, g_head, g4, g_final)
    deltas = order(sd[0], sd[1], d1, sd[2], d2, d3, sd[3], sd[4], d4, sd[5])
    new_m = order(snm[0], snm[1], nm1, snm[2], nm2, nm3, snm[3], snm[4], nm4, snm[5])
    new_v = order(snv[0], snv[1], nv1, snv[2], nv2, nv3, snv[3], snv[4], nv4, snv[5])
    return (loss, grad_x, *grads, *deltas, *new_m, *new_v)
```

```python
import functools

import jax
import jax.numpy as jnp
from jax import lax
from jax.experimental import pallas as pl
from jax.experimental.pallas import tpu as pltpu

F32 = jnp.float32
BF16 = jnp.bfloat16
MESH = pl.DeviceIdType.MESH

N_DEV = 8
N_META = 16
N_HEADS = 4
CHUNK = 64
CHUNKS_PER_STEP = 2
LEAD = 128
PAD_ROWS = LEAD - N_META
RMS_EPS = 1e-6
NEG = -1e30
LANES = 128
VMEM_LIMIT = 48 * 1024 * 1024

ADAM_LR = 0.001
ADAM_B1 = 0.9
ADAM_B2 = 0.999
ADAM_EPS = 1e-08
ADAM_WD = 0.01
ADAM_STEP = 10

HIGHEST = lax.Precision.HIGHEST


def _pick(n, target, mult):
    best = None
    for d in range(mult, min(n, target) + 1, mult):
        if n % d == 0:
            best = d
    return n if best is None else best


def _params(**kw):
    return pltpu.CompilerParams(vmem_limit_bytes=VMEM_LIMIT, **kw)


def _sigmoid(x):
    return 0.5 * jnp.tanh(0.5 * x) + 0.5


STRIP_ROWS = 16
STRIP_COLS = 256


def _strips(n_rows, n_cols):
    for c0 in range(0, n_cols, STRIP_COLS):
        for r0 in range(0, n_rows, STRIP_ROWS):
            yield slice(r0, r0 + STRIP_ROWS), slice(c0, min(c0 + STRIP_COLS, n_cols))


def _shift(sl, by):
    return slice(sl.start + by, sl.stop + by)


def _call(body, *, name, grid, in_specs, out_specs, out_shape, args, scratch_shapes=(), aliases=None,
          exchange=None):
    aliases = {} if aliases is None else aliases
    if exchange is None:
        return pl.pallas_call(
            body, name=name, grid=grid, in_specs=list(in_specs), out_specs=out_specs, out_shape=out_shape,
            scratch_shapes=list(scratch_shapes), input_output_aliases=aliases,
            compiler_params=_params())(*args)
    ex = exchange
    single = not isinstance(out_shape, (list, tuple))
    shapes = [out_shape] if single else list(out_shape)
    specs = [out_specs] if single else list(out_specs)
    n_in, n_out, n_scr = len(in_specs), len(shapes), len(scratch_shapes)
    n_ex_in, n_ex_out = len(ex.ins), len(ex.out_shapes)
    steps = 1
    for size in grid:
        steps *= size

    def wrapped(*refs):
        own_in, ex_in = refs[:n_in], refs[n_in:n_in + n_ex_in]
        at = n_in + n_ex_in
        own_out, ex_out = refs[at:at + n_out], refs[at + n_out:at + n_out + n_ex_out]
        at += n_out + n_ex_out
        own_scr, ex_scr = refs[at:at + n_scr], refs[at + n_scr:]
        step = 0
        for axis, size in enumerate(grid):
            step = step * size + pl.program_id(axis)

        @pl.when(step == 0)
        def _():
            ex.start(ex_in, ex_out, ex_scr)

        if ex.mid is not None:
            @pl.when(step == (3 * steps) // 4)
            def _():
                ex.mid(ex_in, ex_out, ex_scr)

        body(*own_in, *own_out, *own_scr)

        @pl.when(step == steps - 1)
        def _():
            ex.finish(ex_in, ex_out, ex_scr)

    any_spec = pl.BlockSpec(memory_space=pl.ANY)
    res = pl.pallas_call(
        wrapped, name=name, grid=grid,
        in_specs=list(in_specs) + [any_spec] * n_ex_in, out_specs=specs + [any_spec] * n_ex_out,
        out_shape=shapes + ex.out_shapes, scratch_shapes=list(scratch_shapes) + ex.scratch,
        input_output_aliases=aliases, compiler_params=_params())(*args, *ex.ins)
    return (res[0] if single else res[:n_out]), res[n_out:]


def _matmul(a, b, *, form, m, n, kdim, tm, tn, tk, out_dtype, name,
            a_spec=None, b_spec=None, out_spec=None, out_shape=None, residual=None, exchange=None):
    ni, nj, nk = m // tm, n // tn, kdim // tk
    assert ni * tm == m and nj * tn == n and nk * tk == kdim, (name, m, n, kdim, tm, tn, tk)
    if form == "nn":
        dn = (((1,), (0,)), ((), ()))
        a_def = pl.BlockSpec((tm, tk), lambda i, j, k: (i, k))
        b_def = pl.BlockSpec((tk, tn), lambda i, j, k: (k, j))
    elif form == "nt":
        dn = (((1,), (1,)), ((), ()))
        a_def = pl.BlockSpec((tm, tk), lambda i, j, k: (i, k))
        b_def = pl.BlockSpec((tn, tk), lambda i, j, k: (j, k))
    else:
        dn = (((0,), (0,)), ((), ()))
        a_def = pl.BlockSpec((tk, tm), lambda i, j, k: (k, i))
        b_def = pl.BlockSpec((tk, tn), lambda i, j, k: (k, j))
    a_spec = a_def if a_spec is None else a_spec
    b_spec = b_def if b_spec is None else b_spec
    o_spec = pl.BlockSpec((tm, tn), lambda i, j, k: (i, j)) if out_spec is None else out_spec
    has_res = residual is not None

    def body(*refs):
        a_ref, b_ref = refs[:2]
        r_ref = refs[2] if has_res else None
        o_ref = refs[2 + has_res]

        def product():
            return lax.dot_general(a_ref[...].astype(BF16), b_ref[...].astype(BF16), dn,
                                   preferred_element_type=F32)

        def finish(acc):
            if has_res:
                acc = acc + r_ref[...].astype(F32)
            o_ref[...] = acc.astype(o_ref.dtype)

        if nk == 1:
            finish(product())
        else:
            acc_ref = refs[3 + has_res]
            k = pl.program_id(2)

            @pl.when(k == 0)
            def _():
                acc_ref[...] = jnp.zeros_like(acc_ref)

            acc_ref[...] += product()

            @pl.when(k == nk - 1)
            def _():
                finish(acc_ref[...])

    in_specs = [a_spec, b_spec]
    args = [a, b]
    if has_res:
        in_specs.append(pl.BlockSpec((tm, tn), lambda i, j, k: (i, j)))
        args.append(residual)
    return _call(
        body, name=name, grid=(ni, nj, nk), in_specs=in_specs, out_specs=o_spec,
        out_shape=jax.ShapeDtypeStruct((m, n) if out_shape is None else out_shape, out_dtype),
        scratch_shapes=[] if nk == 1 else [pltpu.VMEM((tm, tn), F32)], args=args, exchange=exchange)


def _rms_fwd(h, w, name, exchange=None):
    t, d = h.shape
    tm = _pick(t, 416, 16)

    def body(h_ref, w_ref, u_ref):
        x = h_ref[...]
        r = lax.rsqrt(jnp.mean(x * x, axis=-1, keepdims=True) + RMS_EPS)
        u_ref[...] = ((x * r) * w_ref[...]).astype(BF16)

    return _call(
        body, name=name, grid=(t // tm,),
        in_specs=[pl.BlockSpec((tm, d), lambda i: (i, 0)), pl.BlockSpec((1, d), lambda i: (0, 0))],
        out_specs=pl.BlockSpec((tm, d), lambda i: (i, 0)),
        out_shape=jax.ShapeDtypeStruct((t, d), BF16), args=(h, w), exchange=exchange)


def _rms_first(x, meta, w, name, exchange=None):
    seq, d = x.shape
    t = seq + LEAD
    tm = LEAD

    def body(x_ref, meta_ref, w_ref, h_ref, u_ref):
        i = pl.program_id(0)

        @pl.when(i == 0)
        def _():
            h_ref[0:PAD_ROWS, :] = jnp.zeros((PAD_ROWS, d), F32)
            h_ref[PAD_ROWS:LEAD, :] = meta_ref[...]

        @pl.when(i > 0)
        def _():
            h_ref[...] = x_ref[...]

        hv = h_ref[...]
        r = lax.rsqrt(jnp.mean(hv * hv, axis=-1, keepdims=True) + RMS_EPS)
        u_ref[...] = ((hv * r) * w_ref[...]).astype(BF16)

    row = pl.BlockSpec((tm, d), lambda i: (i, 0))
    return _call(
        body, name=name, grid=(t // tm,),
        in_specs=[pl.BlockSpec((tm, d), lambda i: (jnp.maximum(i - 1, 0), 0)),
                  pl.BlockSpec((N_META, d), lambda i: (0, 0)), pl.BlockSpec((1, d), lambda i: (0, 0))],
        out_specs=[row, row],
        out_shape=[jax.ShapeDtypeStruct((t, d), F32), jax.ShapeDtypeStruct((t, d), BF16)],
        args=(x, meta, w), exchange=exchange)


def _rms_bwd(h, w, du, dres, name):
    t, d = h.shape
    tm = _pick(t, 416, 16)

    def body(h_ref, w_ref, du_ref, dres_ref, dh_ref, dw_ref):
        i = pl.program_id(0)
        x = h_ref[...]
        g = du_ref[...].astype(F32)
        r = lax.rsqrt(jnp.mean(x * x, axis=-1, keepdims=True) + RMS_EPS)
        gw = g * w_ref[...]
        dot = jnp.mean(gw * x, axis=-1, keepdims=True)
        dh_ref[...] = dres_ref[...] + (r * gw - x * ((r * r * r) * dot))
        part = jnp.sum(g * (x * r), axis=0, keepdims=True)

        @pl.when(i == 0)
        def _():
            dw_ref[...] = jnp.zeros_like(dw_ref)

        dw_ref[...] += jnp.broadcast_to(part, dw_ref.shape)

    row = pl.BlockSpec((tm, d), lambda i: (i, 0))
    return pl.pallas_call(
        body, name=name, grid=(t // tm,),
        in_specs=[row, pl.BlockSpec((1, d), lambda i: (0, 0)), row, row],
        out_specs=[row, pl.BlockSpec((8, d), lambda i: (0, 0))],
        out_shape=[jax.ShapeDtypeStruct((t, d), F32), jax.ShapeDtypeStruct((8, d), F32)],
        compiler_params=_params(),
    )(h, w, du, dres)


def _rms_bwd_first(h, w, du, dres, name):
    t, d = h.shape
    sub = LEAD
    per = _pick((t - LEAD) // sub, 4, 1)
    tmo = per * sub
    n_out = (t - LEAD) // tmo
    n_blocks = t // sub

    def body(*refs):
        w_ref = refs[0]
        h_refs, du_refs, dres_refs = refs[1:1 + per], refs[1 + per:1 + 2 * per], refs[1 + 2 * per:1 + 3 * per]
        gx_ref, dmeta_ref, dw_ref = refs[1 + 3 * per:]
        i = pl.program_id(0)

        def piece(q):
            x = h_refs[q][...]
            g = du_refs[q][...].astype(F32)
            r = lax.rsqrt(jnp.mean(x * x, axis=-1, keepdims=True) + RMS_EPS)
            gw = g * w_ref[...]
            dot = jnp.mean(gw * x, axis=-1, keepdims=True)
            dh = dres_refs[q][...] + (r * gw - x * ((r * r * r) * dot))
            return dh, jnp.sum(g * (x * r), axis=0, keepdims=True)

        @pl.when(i == 0)
        def _():
            dh, part = piece(0)
            dmeta_ref[...] = dh[PAD_ROWS:LEAD, :]
            dw_ref[...] = jnp.broadcast_to(part, dw_ref.shape)

        @pl.when(i > 0)
        def _():
            for q in range(per):
                dh, part = piece(q)
                gx_ref[q * sub:(q + 1) * sub, :] = dh
                dw_ref[...] += jnp.broadcast_to(part, dw_ref.shape)

    def piece_spec(q):
        return pl.BlockSpec((sub, d), lambda i: (jnp.clip(1 + per * (i - 1) + q, 0, n_blocks - 1), 0))

    pieces = [piece_spec(q) for q in range(per)]
    return pl.pallas_call(
        body, name=name, grid=(n_out + 1,),
        in_specs=[pl.BlockSpec((1, d), lambda i: (0, 0))] + pieces * 3,
        out_specs=[pl.BlockSpec((tmo, d), lambda i: (jnp.maximum(i - 1, 0), 0)),
                   pl.BlockSpec((N_META, d), lambda i: (0, 0)), pl.BlockSpec((8, d), lambda i: (0, 0))],
        out_shape=[jax.ShapeDtypeStruct((t - LEAD, d), F32), jax.ShapeDtypeStruct((N_META, d), F32),
                   jax.ShapeDtypeStruct((8, d), F32)],
        compiler_params=_params(),
    )(w, *([h] * per), *([du] * per), *([dres] * per))


def _final_loss(h, w, tgt, name):
    t, d = h.shape
    sub = LEAD
    per = _pick(t // sub, 5, 1)
    tm = per * sub

    def body(h_ref, w_ref, *rest):
        t_refs, (dh_ref, dw_ref, loss_ref) = rest[:per], rest[per:]
        i = pl.program_id(0)

        @pl.when(i == 0)
        def _():
            dw_ref[...] = jnp.zeros_like(dw_ref)
            loss_ref[...] = jnp.zeros_like(loss_ref)

        for q in range(per):
            sl = slice(q * sub, (q + 1) * sub)
            x = h_ref[sl, :]
            r = lax.rsqrt(jnp.mean(x * x, axis=-1, keepdims=True) + RMS_EPS)
            yn = x * r
            y = yn * w_ref[...]
            rows = i * tm + q * sub + lax.broadcasted_iota(jnp.int32, (sub, 1), 0)
            diff = jnp.where(rows >= LEAD, y - t_refs[q][...], 0.0)
            tile_loss = 0.5 * jnp.sum(jnp.mean(diff * diff, axis=-1, keepdims=True), axis=0, keepdims=True)
            dy = diff / d
            gw = dy * w_ref[...]
            dot = jnp.mean(gw * x, axis=-1, keepdims=True)
            dh_ref[sl, :] = r * gw - x * ((r * r * r) * dot)
            dw_ref[...] += jnp.broadcast_to(jnp.sum(dy * yn, axis=0, keepdims=True), dw_ref.shape)
            loss_ref[...] += jnp.broadcast_to(tile_loss, loss_ref.shape)

    row = pl.BlockSpec((tm, d), lambda i: (i, 0))
    piece = [pl.BlockSpec((sub, d), functools.partial(lambda i, q: (jnp.maximum(i * per + q - 1, 0), 0), q=q))
             for q in range(per)]
    return pl.pallas_call(
        body, name=name, grid=(t // tm,),
        in_specs=[row, pl.BlockSpec((1, d), lambda i: (0, 0))] + piece,
        out_specs=[row, pl.BlockSpec((8, d), lambda i: (0, 0)), pl.BlockSpec((8, LANES), lambda i: (0, 0))],
        out_shape=[jax.ShapeDtypeStruct((t, d), F32), jax.ShapeDtypeStruct((8, d), F32),
                   jax.ShapeDtypeStruct((8, LANES), F32)],
        compiler_params=_params(),
    )(h, w, *([tgt] * per))


def _conv_gate_fwd(p0, conv_w, name, exchange=None):
    t = p0.shape[0]
    ec = p0.shape[1] // 8
    tm = _pick(t, 416, 16)
    nt = t // tm

    def body(p_ref, w_ref, g_ref, y_ref, ext_ref):
        i = pl.program_id(1)

        @pl.when(i == 0)
        def _():
            ext_ref[0:8, :] = jnp.zeros((8, ec), F32)

        for rows, cols in _strips(tm, ec):
            cg = p_ref[rows, _shift(cols, ec)].astype(F32)
            xin = p_ref[rows, _shift(cols, 2 * ec)].astype(F32)
            ext_ref[_shift(rows, 8), cols] = cg * xin
        for rows, cols in _strips(tm, ec):
            y = (w_ref[0:1, cols] * ext_ref[_shift(rows, 6), cols] + w_ref[1:2, cols] * ext_ref[_shift(rows, 7), cols]
                 + w_ref[2:3, cols] * ext_ref[_shift(rows, 8), cols])
            bg = p_ref[rows, cols].astype(F32)
            z = p_ref[rows, _shift(cols, 3 * ec)].astype(F32)
            y_ref[rows, cols] = y.astype(BF16)
            g_ref[rows, cols] = ((z * _sigmoid(z)) * bg * y).astype(BF16)
        ext_ref[0:8, :] = ext_ref[tm:tm + 8, :]

    out = pl.BlockSpec((tm, ec), lambda j, i: (i, j))
    return _call(
        body, name=name, grid=(2, nt),
        in_specs=[pl.BlockSpec((tm, 4 * ec), lambda j, i: (i, j)), pl.BlockSpec((3, ec), lambda j, i: (0, j))],
        out_specs=[out, out],
        out_shape=[jax.ShapeDtypeStruct((t, 2 * ec), BF16)] * 2,
        scratch_shapes=[pltpu.VMEM((tm + 8, ec), F32)], args=(p0, conv_w), exchange=exchange)


def _conv_gate_bwd(p0, y, dg, conv_w, name, exchange=None):
    t = p0.shape[0]
    ec = p0.shape[1] // 8
    tm = _pick(t, 416, 16)
    nt = t // tm

    def body(p_ref, y_ref, dg_ref, w_ref, dp_ref, dw_ref, ext_ref):
        i = pl.program_id(1)

        @pl.when(i == 0)
        def _():
            ext_ref[tm:tm + 8, :] = jnp.zeros((8, ec), F32)
            dw_ref[...] = jnp.zeros_like(dw_ref)

        for rows, cols in _strips(tm, ec):
            bg = p_ref[rows, cols].astype(F32)
            z = p_ref[rows, _shift(cols, 3 * ec)].astype(F32)
            yv = y_ref[rows, cols].astype(F32)
            dgv = dg_ref[rows, cols].astype(F32)
            sig = _sigmoid(z)
            sz = z * sig
            dp_ref[rows, _shift(cols, 3 * ec)] = (dgv * bg * yv * (sig * (1.0 + z * (1.0 - sig)))).astype(BF16)
            dp_ref[rows, cols] = (dgv * sz * yv).astype(BF16)
            ext_ref[rows, cols] = dgv * sz * bg
        acc = None
        for rows, cols in _strips(tm, ec):
            if rows.start == 0:
                acc = [jnp.zeros((STRIP_ROWS, cols.stop - cols.start), F32) for _ in range(3)]
            dy = ext_ref[rows, cols]
            dy1 = ext_ref[_shift(rows, 1), cols]
            dy2 = ext_ref[_shift(rows, 2), cols]
            cg = p_ref[rows, _shift(cols, ec)].astype(F32)
            xin = p_ref[rows, _shift(cols, 2 * ec)].astype(F32)
            da = w_ref[0:1, cols] * dy2 + w_ref[1:2, cols] * dy1 + w_ref[2:3, cols] * dy
            dp_ref[rows, _shift(cols, ec)] = (da * xin).astype(BF16)
            dp_ref[rows, _shift(cols, 2 * ec)] = (da * cg).astype(BF16)
            a = cg * xin
            acc = [acc[0] + a * dy2, acc[1] + a * dy1, acc[2] + a * dy]
            if rows.stop == tm:
                for tap in range(3):
                    dw_ref[tap:tap + 1, cols] += jnp.sum(acc[tap], axis=0, keepdims=True)
        ext_ref[tm:tm + 8, :] = ext_ref[0:8, :]

    rev = lambda j, i: (nt - 1 - i, j)
    return _call(
        body, name=name, grid=(2, nt),
        in_specs=[pl.BlockSpec((tm, 4 * ec), rev), pl.BlockSpec((tm, ec), rev), pl.BlockSpec((tm, ec), rev),
                  pl.BlockSpec((3, ec), lambda j, i: (0, j))],
        out_specs=[pl.BlockSpec((tm, 4 * ec), rev), pl.BlockSpec((8, ec), lambda j, i: (0, j))],
        out_shape=[jax.ShapeDtypeStruct((t, 8 * ec), BF16), jax.ShapeDtypeStruct((8, 2 * ec), F32)],
        scratch_shapes=[pltpu.VMEM((tm + 8, ec), F32)], args=(p0, y, dg, conv_w), exchange=exchange)


def _log_sigmoid(x):
    return jnp.minimum(x, 0.0) - jnp.log(1.0 + jnp.exp(-jnp.abs(x)))


def _gates_fwd(graw, graw_t, bias_row, bias_col, name):
    t = graw.shape[0]
    tm = _pick(t, 640, 128)
    cpt = tm // CHUNK
    nh = N_HEADS

    def body(g_ref, gt_ref, br_ref, bc_ref, colf_ref, rowf_ref):
        i = pl.program_id(0)
        gc = g_ref[...] + br_ref[...]
        rows = i * tm + lax.broadcasted_iota(jnp.int32, (tm, 1), 0)
        live = rows >= PAD_ROWS
        lf = jnp.where(live, _log_sigmoid(gc), 0.0)
        li = jnp.where(live, gc, NEG)
        gt = gt_ref[...] + bc_ref[...]
        cols = i * tm + lax.broadcasted_iota(jnp.int32, (1, tm), 1)
        live_t = cols >= PAD_ROWS
        lf_t = jnp.where(live_t, _log_sigmoid(gt), 0.0)
        li_t = jnp.where(live_t, gt, NEG)
        ri = lax.broadcasted_iota(jnp.int32, (CHUNK, CHUNK), 0)
        ci = lax.broadcasted_iota(jnp.int32, (CHUNK, CHUNK), 1)
        lower = (ri >= ci).astype(F32)
        upper = (ri <= ci).astype(F32)
        lane = lax.broadcasted_iota(jnp.int32, (CHUNK, LANES), 1)
        sub = lax.broadcasted_iota(jnp.int32, (8, CHUNK), 0)
        for c in range(cpt):
            sl = slice(c * CHUNK, (c + 1) * CHUNK)
            b_all = jnp.dot(lower, lf[sl, :], precision=HIGHEST, preferred_element_type=F32)
            bt_all = jnp.dot(lf_t[:, sl], upper, precision=HIGHEST, preferred_element_type=F32)
            for h in range(nh):
                b = b_all[:, nh + h:nh + h + 1]
                r = li[sl, h:h + 1] - b
                pre = gc[sl, nh + h:nh + h + 1]
                colf_ref[h, sl, :] = jnp.where(lane == 0, b, jnp.where(lane == 1, r, jnp.where(lane == 2, pre, 0.0)))
                b_row = bt_all[nh + h:nh + h + 1, :]
                r_row = li_t[h:h + 1, sl] - b_row
                rowf_ref[h, c, :, 0:CHUNK] = jnp.where(sub == 0, r_row, jnp.where(sub == 1, b_row, 0.0))
                rowf_ref[h, c, :, CHUNK:LANES] = jnp.zeros((8, LANES - CHUNK), F32)

    return pl.pallas_call(
        body, name=name, grid=(t // tm,),
        in_specs=[pl.BlockSpec((tm, LANES), lambda i: (i, 0)), pl.BlockSpec((16, tm), lambda i: (0, i)),
                  pl.BlockSpec((1, LANES), lambda i: (0, 0)), pl.BlockSpec((16, 1), lambda i: (0, 0))],
        out_specs=[pl.BlockSpec((nh, tm, LANES), lambda i: (0, i, 0)),
                   pl.BlockSpec((nh, cpt, 8, LANES), lambda i: (0, i, 0, 0))],
        out_shape=[jax.ShapeDtypeStruct((nh, t, LANES), F32),
                   jax.ShapeDtypeStruct((nh, t // CHUNK, 8, LANES), F32)],
        compiler_params=_params(),
    )(graw, graw_t, bias_row, bias_col)


def _gates_bwd(gstat, colf, dp1, name):
    nh, t, _ = gstat.shape
    tm = _pick(t, 640, 128)
    cpt = tm // CHUNK
    nt = t // tm
    nc = t // CHUNK
    gate_block = dp1.shape[1] // LANES - 1

    def body(gs_ref, nx_ref, colf_ref, dp_any, dg_ref, db_ref):
        i = pl.program_id(0)
        ri = lax.broadcasted_iota(jnp.int32, (CHUNK, CHUNK), 0)
        ci = lax.broadcasted_iota(jnp.int32, (CHUNK, CHUNK), 1)
        upper = (ri <= ci).astype(F32)
        lane = lax.broadcasted_iota(jnp.int32, (CHUNK, LANES), 1)
        total = jnp.zeros((1, LANES), F32)
        for c in range(cpt):
            sl = slice(c * CHUNK, (c + 1) * CHUNK)
            rows = i * tm + c * CHUNK + lax.broadcasted_iota(jnp.int32, (CHUNK, 1), 0)
            live = rows >= PAD_ROWS
            acc = jnp.zeros((CHUNK, LANES), F32)
            for h in range(nh):
                blk = gs_ref[h, sl, :]
                rev = jnp.dot(upper, blk, precision=HIGHEST, preferred_element_type=F32)
                if c + 1 < cpt:
                    carry = gs_ref[h, (c + 1) * CHUNK:(c + 1) * CHUNK + 1, 2:3]
                else:
                    carry = jnp.where(i == nt - 1, 0.0, nx_ref[h, 0:1, 2:3])
                dlogf = rev[:, 0:1] + carry
                pre = colf_ref[h, sl, 2:3]
                dgf = jnp.where(live, dlogf * (1.0 - _sigmoid(pre)), 0.0)
                dgi = jnp.where(live, blk[:, 1:2], 0.0)
                acc = acc + jnp.where(lane == h, dgi, 0.0) + jnp.where(lane == nh + h, dgf, 0.0)
            dg_ref[sl, :] = acc.astype(BF16)
            total = total + jnp.sum(acc, axis=0, keepdims=True)

        @pl.when(i == 0)
        def _():
            db_ref[...] = jnp.zeros_like(db_ref)

        db_ref[...] += jnp.broadcast_to(total, db_ref.shape)

    return pl.pallas_call(
        body, name=name, grid=(nt,),
        in_specs=[pl.BlockSpec((nh, tm, LANES), lambda i: (0, i, 0)),
                  pl.BlockSpec((nh, CHUNK, LANES), lambda i: (0, jnp.minimum((i + 1) * cpt, nc - 1), 0)),
                  pl.BlockSpec((nh, tm, LANES), lambda i: (0, i, 0)),
                  pl.BlockSpec(memory_space=pl.ANY)],
        out_specs=[pl.BlockSpec((tm, LANES), lambda i: (i, gate_block)), pl.BlockSpec((8, LANES), lambda i: (0, 0))],
        out_shape=[jax.ShapeDtypeStruct(dp1.shape, dp1.dtype), jax.ShapeDtypeStruct((8, LANES), F32)],
        input_output_aliases={3: 0},
        compiler_params=_params(),
    )(gstat, gstat, colf, dp1)


NT_DIMS = (((1,), (1,)), ((), ()))
TN_DIMS = (((0,), (0,)), ((), ()))


def _dot(a, b):
    return jnp.dot(a.astype(BF16), b.astype(BF16), preferred_element_type=F32)


def _dot_nt(a, b):
    return lax.dot_general(a.astype(BF16), b.astype(BF16), NT_DIMS, preferred_element_type=F32)


def _dot_tn(a, b):
    return lax.dot_general(a.astype(BF16), b.astype(BF16), TN_DIMS, preferred_element_type=F32)


def _chunk_gates(colf_ref, rowf_ref, m_prev, width=CHUNK):
    b = colf_ref[:, 0:1]
    rcol = colf_ref[:, 1:2]
    rrow = rowf_ref[0:1, 0:width]
    ri = lax.broadcasted_iota(jnp.int32, (CHUNK, width), 0)
    ci = lax.broadcasted_iota(jnp.int32, (CHUNK, width), 1)
    log_d = jnp.where(ri >= ci, b + rrow, NEG)
    m_row = jnp.maximum(b + m_prev, jnp.max(log_d, axis=-1, keepdims=True))
    dmat = jnp.exp(log_d - m_row)
    inter = jnp.exp(b + m_prev - m_row)
    b_last = b[CHUNK - 1:CHUNK, :]
    log_w = rcol + b_last
    m_new = jnp.maximum(b_last + m_prev, jnp.max(log_w, axis=0, keepdims=True))
    decay = jnp.exp(b_last + m_prev - m_new)
    w = jnp.exp(log_w - m_new)
    return m_row, dmat, inter, m_new, decay, w


def _mlstm_fwd(p1, colf, rowf, dk, dv, name):
    t = p1.shape[0]
    nh = N_HEADS
    nc = t // CHUNK
    hw = 2 * dk + dv
    cps = CHUNKS_PER_STEP if nc % CHUNKS_PER_STEP == 0 else 1
    rows_per_step = cps * CHUNK

    def body(p_ref, colf_ref, rowf_ref, hh_ref, stat_ref, cs_ref, ns_ref, *state):
        c_refs, n_refs, m_refs = state[:nh], state[nh:2 * nh], state[2 * nh:]
        c = pl.program_id(0)

        @pl.when(c == 0)
        def _():
            for ref in state:
                ref[...] = jnp.zeros_like(ref)

        for cc in range(cps):
            rows = pl.ds(cc * CHUNK, CHUNK)
            for h in range(nh):
                head(p_ref.at[rows, pl.ds(h * hw, hw)], colf_ref.at[h, rows], rowf_ref.at[h, cc],
                     hh_ref.at[rows, pl.ds(h * dv, dv)], stat_ref.at[h, rows], cs_ref.at[h, cc],
                     ns_ref.at[h, cc], c_refs[h], n_refs[h], m_refs[h])

    def head(p_ref, colf_ref, rowf_ref, hh_ref, stat_ref, cs_ref, ns_ref, c_ref, n_ref, m_ref):
        m_prev = m_ref[...]
        n_prev = n_ref[...]
        c_prev = c_ref[...]
        cs_ref[...] = c_prev.astype(BF16)
        sub = lax.broadcasted_iota(jnp.int32, (8, dk), 0)
        ns_ref[...] = jnp.where(sub == 0, n_prev, jnp.where(sub == 1, m_prev, 0.0))

        q = p_ref[:, 0:dk]
        k = p_ref[:, dk:2 * dk]
        v = p_ref[:, 2 * dk:2 * dk + dv]
        m_row, dmat, inter, m_new, decay, w = _chunk_gates(colf_ref, rowf_ref, m_prev)
        s = _dot_nt(q, k) * dmat
        num = _dot(s, v) + inter * _dot(q, c_prev)
        den = jnp.sum(s, axis=-1, keepdims=True) + inter * jnp.sum(q.astype(F32) * n_prev, axis=-1, keepdims=True)
        denom = jnp.maximum(jnp.abs(den), jnp.exp(-m_row))
        hh_ref[...] = num * (1.0 / denom)
        lane = lax.broadcasted_iota(jnp.int32, (CHUNK, LANES), 1)
        stat_ref[...] = jnp.where(lane == 0, den, 0.0)

        wk = w * k.astype(F32)
        c_ref[...] = decay * c_prev + _dot_tn(wk, v)
        n_ref[...] = decay * n_prev + jnp.sum(wk, axis=0, keepdims=True)
        m_ref[...] = m_new

    return pl.pallas_call(
        body, name=name, grid=(nc // cps,),
        in_specs=[pl.BlockSpec((rows_per_step, nh * hw), lambda c: (c, 0)),
                  pl.BlockSpec((nh, rows_per_step, LANES), lambda c: (0, c, 0)),
                  pl.BlockSpec((nh, cps, 8, LANES), lambda c: (0, c, 0, 0))],
        out_specs=[pl.BlockSpec((rows_per_step, nh * dv), lambda c: (c, 0)),
                   pl.BlockSpec((nh, rows_per_step, LANES), lambda c: (0, c, 0)),
                   pl.BlockSpec((nh, cps, dk, dv), lambda c: (0, c, 0, 0)),
                   pl.BlockSpec((nh, cps, 8, dk), lambda c: (0, c, 0, 0))],
        out_shape=[jax.ShapeDtypeStruct((t, nh * dv), F32), jax.ShapeDtypeStruct((nh, t, LANES), F32),
                   jax.ShapeDtypeStruct((nh, nc, dk, dv), BF16), jax.ShapeDtypeStruct((nh, nc, 8, dk), F32)],
        scratch_shapes=([pltpu.VMEM((dk, dv), F32)] * nh + [pltpu.VMEM((1, dk), F32)] * nh
                        + [pltpu.VMEM((1, 1), F32)] * nh),
        compiler_params=_params(),
    )(p1, colf, rowf)


def _head_gate_fwd(hh, p1, head_w, dv, name):
    t = hh.shape[0]
    nh = N_HEADS
    e = nh * dv
    tm = _pick(t, 416, 16)

    def body(hh_ref, oz_ref, w_ref, hm_ref):
        for h in range(nh):
            cols = slice(h * dv, (h + 1) * dv)
            x = hh_ref[:, cols]
            o = oz_ref[:, 2 * h * dv:(2 * h + 1) * dv].astype(F32)
            z = oz_ref[:, (2 * h + 1) * dv:(2 * h + 2) * dv].astype(F32)
            r = lax.rsqrt(jnp.mean(x * x, axis=-1, keepdims=True) + RMS_EPS)
            hn = (x * r) * w_ref[:, cols]
            hm_ref[:, cols] = (hn * _sigmoid(o) * (z * _sigmoid(z))).astype(BF16)

    return pl.pallas_call(
        body, name=name, grid=(t // tm,),
        in_specs=[pl.BlockSpec((tm, e), lambda i: (i, 0)), pl.BlockSpec((tm, 2 * e), lambda i: (i, 1)),
                  pl.BlockSpec((1, e), lambda i: (0, 0))],
        out_specs=pl.BlockSpec((tm, e), lambda i: (i, 0)),
        out_shape=jax.ShapeDtypeStruct((t, e), BF16),
        compiler_params=_params(),
    )(hh, p1, head_w)


def _mlstm_bwd(p1, colf, rowf, head_w, hh, stat, csave, nsave, dhm, n_cols, dk, dv, name, exchange=None):
    t = p1.shape[0]
    nh = N_HEADS
    nc = t // CHUNK
    hw = 2 * dk + dv
    cps = CHUNKS_PER_STEP if nc % CHUNKS_PER_STEP == 0 else 1
    rows_per_step = cps * CHUNK

    def body(p_ref, colf_ref, rowf_ref, w_ref, hh_ref, stat_ref, cs_ref, ns_ref, dhm_ref,
             dp_ref, gs_ref, dw_ref, *state):
        dc_refs, dn_refs, dwacc_refs = state[:nh], state[nh:2 * nh], state[2 * nh:]
        c = pl.program_id(0)

        @pl.when(c == 0)
        def _():
            for ref in state:
                ref[...] = jnp.zeros_like(ref)

        for cc in reversed(range(cps)):
            rows = pl.ds(cc * CHUNK, CHUNK)
            for h in range(nh):
                cols = pl.ds(h * dv, dv)
                oz = pl.ds(nh * hw + h * 2 * dv, 2 * dv)
                head(p_ref.at[rows, pl.ds(h * hw, hw)], p_ref.at[rows, oz], colf_ref.at[h, rows], rowf_ref.at[h, cc],
                     w_ref.at[:, cols], hh_ref.at[rows, cols], stat_ref.at[h, rows], cs_ref.at[h, cc],
                     ns_ref.at[h, cc], dhm_ref.at[rows, cols], dp_ref.at[rows, pl.ds(h * hw, hw)],
                     dp_ref.at[rows, oz], gs_ref.at[h, rows], dwacc_refs[h], dc_refs[h], dn_refs[h])

        @pl.when(c == nc // cps - 1)
        def _():
            for h in range(nh):
                dw_ref[h] = dwacc_refs[h][...]

    def head(p_ref, oz_ref, colf_ref, rowf_ref, w_ref, hh_ref, stat_ref, cs_ref, ns_ref, dhm_ref,
             dp_ref, doz_ref, gs_ref, dw_ref, dc_ref, dn_ref):
        q = p_ref[:, 0:dk]
        k = p_ref[:, dk:2 * dk]
        v = p_ref[:, 2 * dk:2 * dk + dv]
        o = oz_ref[:, 0:dv].astype(F32)
        z = oz_ref[:, dv:2 * dv].astype(F32)
        qf = q.astype(F32)
        kf = k.astype(F32)
        n_prev = ns_ref[0:1, :]
        m_prev = ns_ref[1:2, 0:1]
        c_prev = cs_ref[...]
        m_row, dmat, inter, m_new, decay, w = _chunk_gates(colf_ref, rowf_ref, m_prev)

        hh = hh_ref[...]
        dhm_v = dhm_ref[...].astype(F32)
        so = _sigmoid(o)
        sg = _sigmoid(z)
        sz = z * sg
        r = lax.rsqrt(jnp.mean(hh * hh, axis=-1, keepdims=True) + RMS_EPS)
        hn = (hh * r) * w_ref[...]
        dhn = dhm_v * so * sz
        doz_ref[:, 0:dv] = (dhm_v * hn * sz * (so * (1.0 - so))).astype(BF16)
        doz_ref[:, dv:2 * dv] = (dhm_v * hn * so * (sg * (1.0 + z * (1.0 - sg)))).astype(BF16)
        dw_ref[...] += jnp.broadcast_to(jnp.sum(dhn * (hh * r), axis=0, keepdims=True), dw_ref.shape)
        gwn = dhn * w_ref[...]
        dhh = r * gwn - hh * ((r * r * r) * jnp.mean(gwn * hh, axis=-1, keepdims=True))

        den = stat_ref[:, 0:1]
        floor = jnp.exp(-m_row)
        denom = jnp.maximum(jnp.abs(den), floor)
        inv = 1.0 / denom
        dnum = dhh * inv
        hdot = jnp.sum(dhh * hh, axis=-1, keepdims=True)
        dden = jnp.where(jnp.abs(den) > floor, -(hdot * inv) * jnp.sign(den), 0.0)
        s = _dot_nt(q, k) * dmat
        dqk = (_dot_nt(dnum, v) + dden) * dmat
        dc_new = dc_ref[...]
        dn_new = dn_ref[...]
        idd = inter * dden
        dq = _dot(dqk, k) + inter * _dot_nt(dnum, c_prev) + idd * n_prev
        dkv = _dot_tn(dqk, q) + w * (_dot_nt(v, dc_new) + dn_new)
        dvv = _dot_tn(s, dnum) + w * _dot(k, dc_new)
        dp_ref[:, 0:dk] = dq.astype(BF16)
        dp_ref[:, dk:2 * dk] = dkv.astype(BF16)
        dp_ref[:, 2 * dk:2 * dk + dv] = dvv.astype(BF16)
        qdq = jnp.sum(qf * dq, axis=-1, keepdims=True)
        kdk = jnp.sum(kf * dkv, axis=-1, keepdims=True)
        dc_prev = decay * dc_new + _dot_tn(inter * qf, dnum)
        dn_prev = decay * dn_new + jnp.sum(idd * qf, axis=0, keepdims=True)
        dc_ref[...] = dc_prev
        dn_ref[...] = dn_prev
        cross = (jnp.sum(jnp.sum(c_prev.astype(F32) * dc_prev, axis=-1, keepdims=True), axis=0, keepdims=True)
                 + jnp.sum(n_prev * dn_prev, axis=-1, keepdims=True))
        lane = lax.broadcasted_iota(jnp.int32, (CHUNK, LANES), 1)
        gs_ref[...] = jnp.where(lane == 0, qdq - kdk, jnp.where(lane == 1, kdk, jnp.where(lane == 2, cross, 0.0)))

    ns = nc // cps
    rc = lambda c: (ns - 1 - c, 0)
    rc3 = lambda c: (0, ns - 1 - c, 0)
    rc4 = lambda c: (0, ns - 1 - c, 0, 0)
    return _call(
        body, name=name, grid=(ns,),
        in_specs=[pl.BlockSpec((rows_per_step, nh * (hw + 2 * dv)), rc),
                  pl.BlockSpec((nh, rows_per_step, LANES), rc3),
                  pl.BlockSpec((nh, cps, 8, LANES), rc4),
                  pl.BlockSpec((1, nh * dv), lambda c: (0, 0)),
                  pl.BlockSpec((rows_per_step, nh * dv), rc),
                  pl.BlockSpec((nh, rows_per_step, LANES), rc3),
                  pl.BlockSpec((nh, cps, dk, dv), rc4),
                  pl.BlockSpec((nh, cps, 8, dk), rc4),
                  pl.BlockSpec((rows_per_step, nh * dv), rc)],
        out_specs=[pl.BlockSpec((rows_per_step, nh * (hw + 2 * dv)), rc),
                   pl.BlockSpec((nh, rows_per_step, LANES), rc3),
                   pl.BlockSpec((nh, 8, dv), lambda c: (0, 0, 0))],
        out_shape=[jax.ShapeDtypeStruct((t, n_cols), BF16), jax.ShapeDtypeStruct((nh, t, LANES), F32),
                   jax.ShapeDtypeStruct((nh, 8, dv), F32)],
        scratch_shapes=([pltpu.VMEM((dk, dv), F32)] * nh + [pltpu.VMEM((1, dk), F32)] * nh
                        + [pltpu.VMEM((8, dv), F32)] * nh),
        args=(p1, colf, rowf, head_w, hh, stat, csave, nsave, dhm), exchange=exchange)


def _position():
    return lax.axis_index("x"), lax.axis_index("y"), lax.axis_index("c")


class _Exchange:
    def __init__(self, ins, out_shapes, start, mid, finish):
        n = len(ins)
        self.ins, self.out_shapes = list(ins), list(out_shapes)
        self.start, self.mid, self.finish = start, mid, finish
        self.scratch = [pltpu.SemaphoreType.DMA((n, 7)), pltpu.SemaphoreType.DMA((n, 7)),
                        pltpu.SemaphoreType.DMA((n,))]


def _gather_exchange(shards):
    n = len(shards)

    def plan(ins, outs, sems):
        send_sems, recv_sems, local_sems = sems
        x, y, c = _position()
        me, sibling = (x, y, c), (x, y, 1 - c)
        chips = [(1 - x, y), (x, 1 - y), (1 - x, 1 - y)]

        def copy(a, k, block, to, src=None):
            px, py, pc = block
            dst = outs[a].at[4 * px + 2 * py + pc]
            return pltpu.make_async_remote_copy(
                src_ref=dst if src is None else src, dst_ref=dst,
                send_sem=send_sems.at[a, k], recv_sem=recv_sems.at[a, k],
                device_id=to, device_id_type=MESH)

        def mine():
            return [pltpu.make_async_copy(ins[a], outs[a].at[4 * x + 2 * y + c], local_sems.at[a])
                    for a in range(n)]

        def first():
            out = []
            for a in range(n):
                out.append(copy(a, 0, me, sibling, src=ins[a]))
                out += [copy(a, 1 + j, me, (*chip, c), src=ins[a]) for j, chip in enumerate(chips)]
            return out

        def ici_in():
            return [copy(a, 1 + j, (*chip, c), me) for j, chip in enumerate(chips) for a in range(n)]

        def passed():
            return [copy(a, 4 + j, (*chip, c), sibling) for j, chip in enumerate(chips) for a in range(n)]

        def d2d_in():
            return ([copy(a, 0, sibling, me) for a in range(n)]
                    + [copy(a, 4 + j, (*chip, 1 - c), me) for j, chip in enumerate(chips) for a in range(n)])

        return mine, first, ici_in, passed, d2d_in

    def start(ins, outs, sems):
        mine, first, _, _, _ = plan(ins, outs, sems)
        for cp in mine() + first():
            cp.start()

    def mid(ins, outs, sems):
        _, _, ici_in, passed, _ = plan(ins, outs, sems)
        for arrived, onward in zip(ici_in(), passed()):
            arrived.wait_recv()
            onward.start()

    def finish(ins, outs, sems):
        mine, first, _, passed, d2d_in = plan(ins, outs, sems)
        for cp in d2d_in():
            cp.wait_recv()
        for cp in first() + passed():
            cp.wait_send()
        for cp in mine():
            cp.wait()

    shapes = [jax.ShapeDtypeStruct((N_DEV,) + s.shape, s.dtype) for s in shards]
    return _Exchange(shards, shapes, start, mid, finish)


def _scatter_exchange(fulls):
    n = len(fulls)

    def plan(ins, outs, sems):
        send_sems, recv_sems, local_sems = sems
        x, y, c = _position()
        my_slot = 4 * x + 2 * y + c

        def mine():
            return [pltpu.make_async_copy(ins[a].at[my_slot], outs[a].at[my_slot], local_sems.at[a])
                    for a in range(n)]

        def remote(arriving):
            out = []
            for kk in (1, 2, 4, 6, 3, 5, 7):
                kx, ky, kc = (kk >> 2) & 1, (kk >> 1) & 1, kk & 1
                px = 1 - x if kx else x
                py = 1 - y if ky else y
                pc = 1 - c if kc else c
                peer_slot = 4 * px + 2 * py + pc
                for a in range(n):
                    out.append(pltpu.make_async_remote_copy(
                        src_ref=ins[a].at[peer_slot], dst_ref=outs[a].at[peer_slot if arriving else my_slot],
                        send_sem=send_sems.at[a, kk - 1], recv_sem=recv_sems.at[a, kk - 1],
                        device_id=(px, py, pc), device_id_type=MESH))
            return out

        return mine, remote

    def start(ins, outs, sems):
        mine, remote = plan(ins, outs, sems)
        for cp in mine() + remote(False):
            cp.start()

    def finish(ins, outs, sems):
        mine, remote = plan(ins, outs, sems)
        for cp in remote(True):
            cp.wait_recv()
        for cp in remote(False):
            cp.wait_send()
        for cp in mine():
            cp.wait()

    shapes = [jax.ShapeDtypeStruct(f.shape, f.dtype) for f in fulls]
    return _Exchange(fulls, shapes, start, None, finish)


def _run_exchange(ex, name):
    n_in, n_out = len(ex.ins), len(ex.out_shapes)

    def body(*refs):
        ins, outs, sems = refs[:n_in], refs[n_in:n_in + n_out], refs[n_in + n_out:]
        ex.start(ins, outs, sems)
        if ex.mid is not None:
            ex.mid(ins, outs, sems)
        ex.finish(ins, outs, sems)

    any_spec = pl.BlockSpec(memory_space=pl.ANY)
    return pl.pallas_call(
        body, name=name,
        in_specs=[any_spec] * n_in, out_specs=[any_spec] * n_out,
        out_shape=ex.out_shapes, scratch_shapes=ex.scratch,
    )(*ex.ins)


def _natural_to_head_major(n0, dk, dv):
    nh = N_HEADS
    qk, e, hw = nh * dk, nh * dv, 2 * dk + dv
    if n0 < qk:
        h, off = divmod(n0, dk)
        return h * hw + off, True
    n1 = n0 - qk
    if n1 < qk:
        h, off = divmod(n1, dk)
        return h * hw + dk + off, False
    part, n3 = divmod(n1 - qk, e)
    h, off = divmod(n3, dv)
    if part == 0:
        return h * hw + 2 * dk + off, False
    return nh * hw + h * 2 * dv + (part - 1) * dv + off, False


def _mlstm_weight_layout(w_blocks, dk, dv, name):
    _, d, blk = w_blocks.shape
    nh = N_HEADS
    hw = 2 * dk + 3 * dv
    n_tiles = nh * hw // LANES
    tiles_per_block = d // LANES
    tr = _pick(d, 128, 16)
    scale = dk ** -0.5

    def body(w_ref, p_ref, g_ref):
        lane = lax.broadcasted_iota(jnp.int32, (tr, LANES), 1)

        def tile(s, r):
            return w_ref[s, :, r * LANES:(r + 1) * LANES].astype(F32)

        def last_col(s):
            return w_ref[s, :, d:d + 1].astype(F32)

        for tn in range(n_tiles):
            s, r = divmod(tn, tiles_per_block)
            if s == 0:
                val = tile(0, r)
            elif r == 0:
                from_prev = pltpu.roll(tile(s - 1, tiles_per_block - 1), s - 1, 1)
                from_here = pltpu.roll(tile(s, 0), s, 1)
                val = jnp.where(lane < s - 1, from_prev, jnp.where(lane == s - 1, last_col(s - 1), from_here))
            else:
                slab = jnp.concatenate([tile(s, r - 1), tile(s, r)], axis=1)
                val = pltpu.roll(slab, s, 1)[:, LANES:]
            at, is_q = _natural_to_head_major(tn * LANES, dk, dv)
            if is_q:
                val = val * scale
            p_ref[:, at:at + LANES] = val.astype(BF16)
        n_gate = 2 * nh
        s = N_DEV - 1
        gates = pltpu.roll(tile(s, tiles_per_block - 1), n_gate - 1, 1)
        gates = jnp.where(lane < n_gate - 1, gates, jnp.where(lane == n_gate - 1, last_col(s), 0.0)).astype(BF16)
        g_ref[...] = gates
        p_ref[:, nh * hw:nh * hw + LANES] = gates

    return pl.pallas_call(
        body, name=name, grid=(d // tr,),
        in_specs=[pl.BlockSpec((N_DEV, tr, blk), lambda i: (0, i, 0))],
        out_specs=[pl.BlockSpec((tr, nh * hw + LANES), lambda i: (i, 0)), pl.BlockSpec((tr, LANES), lambda i: (i, 0))],
        out_shape=[jax.ShapeDtypeStruct((d, nh * hw + LANES), BF16), jax.ShapeDtypeStruct((d, LANES), BF16)],
        compiler_params=_params(),
    )(w_blocks)


def _mlstm_grad_layout(g_p1, row0, n_rows, dk, dv, name):
    d = g_p1.shape[0]
    nh = N_HEADS
    hw = 2 * dk + 3 * dv
    n_tiles = nh * hw // LANES
    tiles_per_block = d // LANES
    tr = _pick(d, 128, 16)
    scale = dk ** -0.5

    def body(p_ref, o_ref):
        def natural(tn):
            if tn == n_tiles:
                return p_ref[:, nh * hw:nh * hw + LANES].astype(F32)
            at, is_q = _natural_to_head_major(tn * LANES, dk, dv)
            val = p_ref[:, at:at + LANES].astype(F32)
            return val * scale if is_q else val

        for s in range(N_DEV):
            for r in range(tiles_per_block):
                tn = s * tiles_per_block + r
                if s == 0:
                    val = natural(tn)
                else:
                    slab = jnp.concatenate([natural(tn), natural(tn + 1)], axis=1)
                    val = pltpu.roll(slab, 2 * LANES - s, 1)[:, :LANES]
                o_ref[s, :, r * LANES:(r + 1) * LANES] = val.astype(BF16)
            o_ref[s, :, d:d + 1] = natural((s + 1) * tiles_per_block)[:, s:s + 1].astype(BF16)

    first = row0 // tr
    return pl.pallas_call(
        body, name=name, grid=(n_rows // tr,),
        in_specs=[pl.BlockSpec((tr, nh * hw + LANES), lambda i: (i + first, 0))],
        out_specs=pl.BlockSpec((N_DEV, tr, d + 1), lambda i: (0, i, 0)),
        out_shape=jax.ShapeDtypeStruct((N_DEV, n_rows, d + 1), BF16),
        compiler_params=_params(),
    )(g_p1)


def _adamw_math(w, g, m, v):
    m = ADAM_B1 * m + (1.0 - ADAM_B1) * g
    v = ADAM_B2 * v + (1.0 - ADAM_B2) * (g * g)
    m_hat = m / (1.0 - ADAM_B1 ** ADAM_STEP)
    v_hat = v / (1.0 - ADAM_B2 ** ADAM_STEP)
    delta = -ADAM_LR * (m_hat / (jnp.sqrt(v_hat) + ADAM_EPS) + ADAM_WD * w)
    return delta, m, v


def _adamw_sharded(parts, w, m, v, name):
    n_parts = len(parts)
    _, rk, c = parts[0].shape
    r = n_parts * rk
    tr = _pick(rk, 128, 8)
    per = rk // tr

    def body(*refs):
        p_refs = refs[:n_parts]
        w_ref, m_ref, v_ref, g_ref, d_ref, nm_ref, nv_ref, gsum_ref = refs[n_parts:]
        i = pl.program_id(0)
        for k, p_ref in enumerate(p_refs):
            @pl.when(i // per == k)
            def _(p_ref=p_ref):
                g = p_ref[0].astype(F32)
                for s in range(1, N_DEV):
                    g = g + p_ref[s].astype(F32)
                gsum_ref[...] = g
        g = gsum_ref[...]
        delta, m_new, v_new = _adamw_math(w_ref[...], g, m_ref[...], v_ref[...])
        g_ref[...] = g
        d_ref[...] = delta
        nm_ref[...] = m_new
        nv_ref[...] = v_new

    def part_spec(k):
        return pl.BlockSpec((N_DEV, tr, c), lambda i: (0, jnp.clip(i - k * per, 0, per - 1), 0))

    blk = pl.BlockSpec((tr, c), lambda i: (i, 0))
    return pl.pallas_call(
        body, name=name, grid=(r // tr,),
        in_specs=[part_spec(k) for k in range(n_parts)] + [blk, blk, blk],
        out_specs=[blk] * 4,
        out_shape=[jax.ShapeDtypeStruct((r, c), F32)] * 4,
        scratch_shapes=[pltpu.VMEM((tr, c), F32)],
        compiler_params=_params(),
    )(*parts, w, m, v)


def _sum_devices(parts, name):
    _, r, c = parts.shape

    def body(p_ref, o_ref):
        g = p_ref[0]
        for s in range(1, N_DEV):
            g = g + p_ref[s]
        o_ref[...] = g

    return pl.pallas_call(
        body, name=name, out_shape=jax.ShapeDtypeStruct((r, c), F32), compiler_params=_params(),
    )(parts)


def _adamw_small(gs, ws, ms, vs, name):
    n = len(gs)

    def body(*refs):
        g_refs, w_refs, m_refs, v_refs = refs[:n], refs[n:2 * n], refs[2 * n:3 * n], refs[3 * n:4 * n]
        d_refs, nm_refs, nv_refs = refs[4 * n:5 * n], refs[5 * n:6 * n], refs[6 * n:7 * n]
        for a in range(n):
            delta, m_new, v_new = _adamw_math(w_refs[a][...], g_refs[a][...], m_refs[a][...], v_refs[a][...])
            d_refs[a][...] = delta
            nm_refs[a][...] = m_new
            nv_refs[a][...] = v_new

    shapes = [jax.ShapeDtypeStruct(w.shape, F32) for w in ws]
    outs = pl.pallas_call(
        body, name=name, out_shape=shapes * 3, compiler_params=_params(),
    )(*gs, *ws, *ms, *vs)
    return outs[:n], outs[n:2 * n], outs[2 * n:]


def _pad_rows(a, rows):
    return jnp.pad(a, ((0, rows - a.shape[0]), (0, 0)))


def kernel(x, meta_tokens, norm_w, conv_in_w, conv_w, conv_out_w, mlstm_in_w, mlstm_gate_b, mlstm_head_norm_w, mlstm_out_w, final_norm_w, loss_target, m_meta_tokens, m_norm_w, m_conv_in_w, m_conv_w, m_conv_out_w, m_mlstm_in_w, m_mlstm_gate_b, m_mlstm_head_norm_w, m_mlstm_out_w, m_final_norm_w, v_meta_tokens, v_norm_w, v_conv_in_w, v_conv_w, v_conv_out_w, v_mlstm_in_w, v_mlstm_gate_b, v_mlstm_head_norm_w, v_mlstm_out_w, v_final_norm_w):
    seq, d = x.shape[1], x.shape[2]
    t = seq + LEAD
    e = 2 * d
    ec = e // 2
    nh = N_HEADS
    dv = e // nh
    dk = dv // 2
    qk = nh * dk
    hw = 2 * dk + 3 * dv
    n_in = 2 * qk + 3 * e + 2 * nh
    n_in_s = n_in // N_DEV
    me = 4 * lax.axis_index("x") + 2 * lax.axis_index("y") + lax.axis_index("c")
    tm = _pick(t, 832, 16)
    tm_in = _pick(t, 1664, 16)
    tkw = _pick(t, 2080, 16)

    small = jnp.concatenate([
        meta_tokens,
        _pad_rows(conv_w[0].reshape(3 * (e // N_DEV) // LANES, LANES), 8),
        _pad_rows(mlstm_head_norm_w[0].reshape((e // N_DEV) // LANES, LANES), 8),
    ], axis=0) if d // N_DEV == LANES else None
    assert small is not None, "the packed small-weight block assumes d_model / 8 == 128"
    (small_g,) = _run_exchange(_gather_exchange([small]), "gather_small_weights")
    meta_full = jnp.transpose(small_g[:, 0:N_META, :], (1, 0, 2)).reshape(N_META, d)
    cw_rows = 3 * (e // N_DEV) // LANES
    conv_w_full = jnp.transpose(
        small_g[:, N_META:N_META + cw_rows, :].reshape(N_DEV, 3, e // N_DEV), (1, 0, 2)).reshape(3, e)
    hn_rows = (e // N_DEV) // LANES
    head_w_full = small_g[:, N_META + 8:N_META + 8 + hn_rows, :].reshape(1, e)

    tgt = loss_target[0]

    ci_map = lambda blk: 2 * (blk % 4) + blk // 4
    (h0, u0), (w_ci,) = _rms_first(x[0], meta_full, norm_w[0:1], "rms0",
                                   exchange=_gather_exchange([conv_in_w[0].astype(BF16)]))
    p0, (w_co, w_mi) = _matmul(
        u0, w_ci, form="nn", m=t, n=8 * ec, kdim=d, tm=tm_in, tn=ec, tk=d, out_dtype=BF16, name="conv_in",
        b_spec=pl.BlockSpec((None, d, ec), lambda i, j, k: (ci_map(j), 0, 0)),
        exchange=_gather_exchange([conv_out_w[0].astype(BF16), mlstm_in_w[0].astype(BF16)]))
    w_co = w_co.reshape(e, d)
    (g0, y0), (w_mo,) = _conv_gate_fwd(p0, conv_w_full, "conv_gate",
                                       exchange=_gather_exchange([mlstm_out_w[0].astype(BF16)]))
    w_mo = w_mo.reshape(e, d)
    h1 = _matmul(g0, w_co, form="nn", m=t, n=d, kdim=e, tm=tm, tn=d, tk=e, out_dtype=F32, name="conv_out",
                 residual=h0)

    assert n_in_s == d + 1 and w_mi.shape == (N_DEV, d, d + 1)
    w_p1, w_gate = _mlstm_weight_layout(w_mi, dk, dv, "mlstm_w_layout")
    w_gate_t = jnp.transpose(w_gate[:, 0:16])
    bias_row = jnp.pad(mlstm_gate_b, ((0, 0), (0, LANES - 2 * nh)))
    bias_col = jnp.pad(mlstm_gate_b.T, ((0, 16 - 2 * nh), (0, 0)))

    u1 = _rms_fwd(h1, norm_w[1:2], "rms1")
    p1 = _matmul(u1, w_p1, form="nn", m=t, n=nh * hw, kdim=d, tm=tm_in, tn=_pick(nh * hw, 1024, LANES), tk=d,
                 out_dtype=BF16, name="mlstm_in")
    graw = _matmul(u1, w_gate, form="nn", m=t, n=LANES, kdim=d, tm=tm, tn=LANES, tk=d, out_dtype=F32,
                   name="gates_col")
    graw_t = _matmul(w_gate_t, u1, form="nt", m=16, n=t, kdim=d, tm=16, tn=_pick(t, 1664, LANES), tk=d,
                     out_dtype=F32, name="gates_row")
    colf, rowf = _gates_fwd(graw, graw_t, bias_row, bias_col, "gates_fwd")
    hh, stat, csave, nsave = _mlstm_fwd(p1, colf, rowf, dk, dv, "mlstm_fwd")
    hm = _head_gate_fwd(hh, p1, head_w_full, dv, "head_gate")
    h2 =_matmul(hm, w_mo, form="nn", m=t, n=d, kdim=e, tm=tm, tn=d, tk=e, out_dtype=F32, name="mlstm_out",
                 residual=h1)

    dh2, dwf, loss_part = _final_loss(h2, final_norm_w.reshape(1, d), tgt, "final_loss")

    dhm = _matmul(dh2, w_mo, form="nt", m=t, n=e, kdim=d, tm=tm, tn=_pick(e, 2048, LANES), tk=d, out_dtype=BF16,
                  name="mlstm_out_dx")
    g_mo = _matmul(hm, dh2, form="tn", m=e, n=d, kdim=t, tm=_pick(e, 1024, LANES), tn=d, tk=tkw, out_dtype=BF16,
                   name="mlstm_out_dw")
    n_p1 = nh * hw + LANES
    (dp1, gstat, dhead), (r_mo,) = _mlstm_bwd(
        p1, colf, rowf, head_w_full, hh, stat, csave, nsave, dhm, n_p1, dk, dv, "mlstm_bwd",
        exchange=_scatter_exchange([g_mo.reshape(N_DEV, e // N_DEV, d)]))
    dp1, dbias = _gates_bwd(gstat, colf, dp1, "gates_bwd")
    du1 = _matmul(dp1, w_p1, form="nt", m=t, n=d, kdim=n_p1, tm=tm_in, tn=d, tk=_pick(n_p1, 2048, LANES),
                  out_dtype=F32, name="mlstm_in_dx")
    g_p1 = _matmul(u1, dp1, form="tn", m=d, n=n_p1, kdim=t, tm=d, tn=_pick(n_p1, 2048, LANES), tk=tkw,
                   out_dtype=BF16, name="mlstm_in_dw")
    dh1, dnw1 = _rms_bwd(h1, norm_w[1:2], du1, dh2, "rms1_bwd")

    g_mi_a = _mlstm_grad_layout(g_p1, 0, d // 2, dk, dv, "mlstm_g_layout_a")
    g_mi_b = _mlstm_grad_layout(g_p1, d // 2, d // 2, dk, dv, "mlstm_g_layout_b")

    dg0 =_matmul(dh1, w_co, form="nt", m=t, n=e, kdim=d, tm=tm, tn=_pick(e, 2048, LANES), tk=d, out_dtype=BF16,
                  name="conv_out_dx")
    g_co = _matmul(g0, dh1, form="tn", m=e, n=d, kdim=t, tm=_pick(e, 1024, LANES), tn=d, tk=tkw, out_dtype=BF16,
                   name="conv_out_dw")
    (dp0, dconv), (r_co, r_mi_a) = _conv_gate_bwd(
        p0, y0, dg0, conv_w_full, "conv_gate_bwd",
        exchange=_scatter_exchange([g_co.reshape(N_DEV, e // N_DEV, d), g_mi_a]))
    g_ci, (r_mi_b,) = _matmul(
        u0, dp0, form="tn", m=d, n=8 * ec, kdim=t, tm=d, tn=ec, tk=tkw, out_dtype=BF16, name="conv_in_dw",
        out_shape=(N_DEV, d, ec), out_spec=pl.BlockSpec((None, d, ec), lambda i, j, k: (ci_map(j), 0, 0)),
        exchange=_scatter_exchange([g_mi_b]))
    du0, (r_ci,) = _matmul(
        dp0, w_ci, form="nt", m=t, n=d, kdim=8 * ec, tm=tm_in, tn=d, tk=ec, out_dtype=F32, name="conv_in_dx",
        b_spec=pl.BlockSpec((None, d, ec), lambda i, j, k: (ci_map(k), 0, 0)),
        exchange=_scatter_exchange([g_ci]))
    grad_x, dmeta, dnw0 = _rms_bwd_first(h0, norm_w[0:1], du0, dh1, "rms0_bwd")
    grad_x = grad_x[None]

    row8 = lax.broadcasted_iota(jnp.int32, (8, 1), 0)
    loss_wide = jnp.pad(loss_part, ((0, 0), (0, d - LANES)))
    payload = jnp.concatenate([
        jnp.where(row8 == 0, dnw0, jnp.where(row8 == 1, dnw1, 0.0)),
        jnp.where(row8 == 0, dwf, jnp.where(row8 == 1, loss_wide, 0.0)),
        jnp.where(row8 == 0, jnp.pad(dbias, ((0, 0), (0, d - LANES))), 0.0),
        dmeta,
        _pad_rows(dconv[0:3].reshape(3 * e // d, d), 8),
        _pad_rows(dhead[:, 0, :].reshape(e // d, d), 8),
    ], axis=0)
    (payload_g,) = _run_exchange(_gather_exchange([payload]), "gather_small_grads")
    tot = _sum_devices(payload_g, "sum_small_grads")

    loss = tot[9, 0]
    g_norm = tot[0:2]
    g_final = tot[8]
    g_gate_b = tot[16:17, 0:2 * nh]
    g_meta = lax.dynamic_slice(tot[24:24 + N_META], (0, me * (d // N_DEV)), (N_META, d // N_DEV))
    g_conv_w = lax.dynamic_slice(tot[40:40 + 3 * e // d].reshape(3, e), (0, me * (e // N_DEV)), (3, e // N_DEV))
    g_head = lax.dynamic_slice(tot[48:48 + e // d].reshape(1, e), (0, me * (e // N_DEV)), (1, e // N_DEV))

    g1, d1, nm1, nv1 = _adamw_sharded([r_ci], conv_in_w[0], m_conv_in_w[0], v_conv_in_w[0], "adamw_conv_in")
    g2, d2, nm2, nv2 = _adamw_sharded([r_co], conv_out_w[0], m_conv_out_w[0], v_conv_out_w[0], "adamw_conv_out")
    g3, d3, nm3, nv3 = _adamw_sharded([r_mi_a, r_mi_b], mlstm_in_w[0], m_mlstm_in_w[0], v_mlstm_in_w[0],
                                      "adamw_mlstm_in")
    g4, d4, nm4, nv4 = _adamw_sharded([r_mo], mlstm_out_w[0], m_mlstm_out_w[0], v_mlstm_out_w[0],
                                      "adamw_mlstm_out")

    small_g = [g_meta, g_norm, g_conv_w, g_gate_b, g_head, g_final.reshape(1, d)]
    small_w = [meta_tokens, norm_w, conv_w[0], mlstm_gate_b, mlstm_head_norm_w, final_norm_w.reshape(1, d)]
    small_m = [m_meta_tokens, m_norm_w, m_conv_w[0], m_mlstm_gate_b, m_mlstm_head_norm_w, m_final_norm_w.reshape(1, d)]
    small_v = [v_meta_tokens, v_norm_w, v_conv_w[0], v_mlstm_gate_b, v_mlstm_head_norm_w, v_final_norm_w.reshape(1, d)]
    sd, snm, snv = _adamw_small(small_g, small_w, small_m, small_v, "adamw_small")

    def order(meta, norm, cin, cw, cout, min_, gb, hn, mout, fin):
        return (meta, norm, cin[None], cw[None], cout[None], min_[None], gb, hn, mout[None], fin.reshape(d))

    grads = order(g_meta, g_norm, g1, g_conv_w, g2, g3, g_gate_b, g_head, g4, g_final)
    deltas = order(sd[0], sd[1], d1, sd[2], d2, d3, sd[3], sd[4], d4, sd[5])
    new_m = order(snm[0], snm[1], nm1, snm[2], nm2, nm3, snm[3], snm[4], nm4, snm[5])
    new_v = order(snv[0], snv[1], nv1, snv[2], nv2, nv3, snv[3], snv[4], nv4, snv[5])
    return (loss, grad_x, *grads, *deltas, *new_m, *new_v)
```

```python
import functools

import jax
import jax.numpy as jnp
from jax import lax
from jax.experimental import pallas as pl
from jax.experimental.pallas import tpu as pltpu

F32 = jnp.float32
BF16 = jnp.bfloat16
MESH = pl.DeviceIdType.MESH

N_DEV = 8
N_META = 16
N_HEADS = 4
CHUNK = 64
CHUNKS_PER_STEP = 2
LEAD = 128
PAD_ROWS = LEAD - N_META
RMS_EPS = 1e-6
NEG = -1e30
LANES = 128
VMEM_LIMIT = 48 * 1024 * 1024

ADAM_LR = 0.001
ADAM_B1 = 0.9
ADAM_B2 = 0.999
ADAM_EPS = 1e-08
ADAM_WD = 0.01
ADAM_STEP = 10

HIGHEST = lax.Precision.HIGHEST


def _pick(n, target, mult):
    best = None
    for d in range(mult, min(n, target) + 1, mult):
        if n % d == 0:
            best = d
    return n if best is None else best


def _params(**kw):
    return pltpu.CompilerParams(vmem_limit_bytes=VMEM_LIMIT, **kw)


def _sigmoid(x):
    return 0.5 * jnp.tanh(0.5 * x) + 0.5


STRIP_ROWS = 16
STRIP_COLS = 256


def _strips(n_rows, n_cols):
    for c0 in range(0, n_cols, STRIP_COLS):
        for r0 in range(0, n_rows, STRIP_ROWS):
            yield slice(r0, r0 + STRIP_ROWS), slice(c0, min(c0 + STRIP_COLS, n_cols))


def _shift(sl, by):
    return slice(sl.start + by, sl.stop + by)


def _call(body, *, name, grid, in_specs, out_specs, out_shape, args, scratch_shapes=(), aliases=None,
          exchange=None):
    aliases = {} if aliases is None else aliases
    if exchange is None:
        return pl.pallas_call(
            body, name=name, grid=grid, in_specs=list(in_specs), out_specs=out_specs, out_shape=out_shape,
            scratch_shapes=list(scratch_shapes), input_output_aliases=aliases,
            compiler_params=_params())(*args)
    ex = exchange
    single = not isinstance(out_shape, (list, tuple))
    shapes = [out_shape] if single else list(out_shape)
    specs = [out_specs] if single else list(out_specs)
    n_in, n_out, n_scr = len(in_specs), len(shapes), len(scratch_shapes)
    n_ex_in, n_ex_out = len(ex.ins), len(ex.out_shapes)
    steps = 1
    for size in grid:
        steps *= size

    def wrapped(*refs):
        own_in, ex_in = refs[:n_in], refs[n_in:n_in + n_ex_in]
        at = n_in + n_ex_in
        own_out, ex_out = refs[at:at + n_out], refs[at + n_out:at + n_out + n_ex_out]
        at += n_out + n_ex_out
        own_scr, ex_scr = refs[at:at + n_scr], refs[at + n_scr:]
        step = 0
        for axis, size in enumerate(grid):
            step = step * size + pl.program_id(axis)

        @pl.when(step == 0)
        def _():
            ex.start(ex_in, ex_out, ex_scr)

        if ex.mid is not None:
            @pl.when(step == (3 * steps) // 4)
            def _():
                ex.mid(ex_in, ex_out, ex_scr)

        body(*own_in, *own_out, *own_scr)

        @pl.when(step == steps - 1)
        def _():
            ex.finish(ex_in, ex_out, ex_scr)

    any_spec = pl.BlockSpec(memory_space=pl.ANY)
    res = pl.pallas_call(
        wrapped, name=name, grid=grid,
        in_specs=list(in_specs) + [any_spec] * n_ex_in, out_specs=specs + [any_spec] * n_ex_out,
        out_shape=shapes + ex.out_shapes, scratch_shapes=list(scratch_shapes) + ex.scratch,
        input_output_aliases=aliases, compiler_params=_params())(*args, *ex.ins)
    return (res[0] if single else res[:n_out]), res[n_out:]


def _matmul(a, b, *, form, m, n, kdim, tm, tn, tk, out_dtype, name,
            a_spec=None, b_spec=None, out_spec=None, out_shape=None, residual=None, exchange=None):
    ni, nj, nk = m // tm, n // tn, kdim // tk
    assert ni * tm == m and nj * tn == n and nk * tk == kdim, (name, m, n, kdim, tm, tn, tk)
    if form == "nn":
        dn = (((1,), (0,)), ((), ()))
        a_def = pl.BlockSpec((tm, tk), lambda i, j, k: (i, k))
        b_def = pl.BlockSpec((tk, tn), lambda i, j, k: (k, j))
    elif form == "nt":
        dn = (((1,), (1,)), ((), ()))
        a_def = pl.BlockSpec((tm, tk), lambda i, j, k: (i, k))
        b_def = pl.BlockSpec((tn, tk), lambda i, j, k: (j, k))
    else:
        dn = (((0,), (0,)), ((), ()))
        a_def = pl.BlockSpec((tk, tm), lambda i, j, k: (k, i))
        b_def = pl.BlockSpec((tk, tn), lambda i, j, k: (k, j))
    a_spec = a_def if a_spec is None else a_spec
    b_spec = b_def if b_spec is None else b_spec
    o_spec = pl.BlockSpec((tm, tn), lambda i, j, k: (i, j)) if out_spec is None else out_spec
    has_res = residual is not None

    def body(*refs):
        a_ref, b_ref = refs[:2]
        r_ref = refs[2] if has_res else None
        o_ref = refs[2 + has_res]

        def product():
            return lax.dot_general(a_ref[...].astype(BF16), b_ref[...].astype(BF16), dn,
                                   preferred_element_type=F32)

        def finish(acc):
            if has_res:
                acc = acc + r_ref[...].astype(F32)
            o_ref[...] = acc.astype(o_ref.dtype)

        if nk == 1:
            finish(product())
        else:
            acc_ref = refs[3 + has_res]
            k = pl.program_id(2)

            @pl.when(k == 0)
            def _():
                acc_ref[...] = jnp.zeros_like(acc_ref)

            acc_ref[...] += product()

            @pl.when(k == nk - 1)
            def _():
                finish(acc_ref[...])

    in_specs = [a_spec, b_spec]
    args = [a, b]
    if has_res:
        in_specs.append(pl.BlockSpec((tm, tn), lambda i, j, k: (i, j)))
        args.append(residual)
    return _call(
        body, name=name, grid=(ni, nj, nk), in_specs=in_specs, out_specs=o_spec,
        out_shape=jax.ShapeDtypeStruct((m, n) if out_shape is None else out_shape, out_dtype),
        scratch_shapes=[] if nk == 1 else [pltpu.VMEM((tm, tn), F32)], args=args, exchange=exchange)


def _rms_fwd(h, w, name, exchange=None):
    t, d = h.shape
    tm = _pick(t, 832, 16)

    def body(h_ref, w_ref, u_ref):
        x = h_ref[...]
        r = lax.rsqrt(jnp.mean(x * x, axis=-1, keepdims=True) + RMS_EPS)
        u_ref[...] = ((x * r) * w_ref[...]).astype(BF16)

    return _call(
        body, name=name, grid=(t // tm,),
        in_specs=[pl.BlockSpec((tm, d), lambda i: (i, 0)), pl.BlockSpec((1, d), lambda i: (0, 0))],
        out_specs=pl.BlockSpec((tm, d), lambda i: (i, 0)),
        out_shape=jax.ShapeDtypeStruct((t, d), BF16), args=(h, w), exchange=exchange)


def _rms_first(x, meta, w, name, exchange=None):
    seq, d = x.shape
    t = seq + LEAD
    tm = LEAD

    def body(x_ref, meta_ref, w_ref, h_ref, u_ref):
        i = pl.program_id(0)

        @pl.when(i == 0)
        def _():
            h_ref[0:PAD_ROWS, :] = jnp.zeros((PAD_ROWS, d), F32)
            h_ref[PAD_ROWS:LEAD, :] = meta_ref[...]

        @pl.when(i > 0)
        def _():
            h_ref[...] = x_ref[...]

        hv = h_ref[...]
        r = lax.rsqrt(jnp.mean(hv * hv, axis=-1, keepdims=True) + RMS_EPS)
        u_ref[...] = ((hv * r) * w_ref[...]).astype(BF16)

    row = pl.BlockSpec((tm, d), lambda i: (i, 0))
    return _call(
        body, name=name, grid=(t // tm,),
        in_specs=[pl.BlockSpec((tm, d), lambda i: (jnp.maximum(i - 1, 0), 0)),
                  pl.BlockSpec((N_META, d), lambda i: (0, 0)), pl.BlockSpec((1, d), lambda i: (0, 0))],
        out_specs=[row, row],
        out_shape=[jax.ShapeDtypeStruct((t, d), F32), jax.ShapeDtypeStruct((t, d), BF16)],
        args=(x, meta, w), exchange=exchange)


def _rms_bwd(h, w, du, dres, name):
    t, d = h.shape
    tm = _pick(t, 832, 16)

    def body(h_ref, w_ref, du_ref, dres_ref, dh_ref, dw_ref):
        i = pl.program_id(0)
        x = h_ref[...]
        g = du_ref[...].astype(F32)
        r = lax.rsqrt(jnp.mean(x * x, axis=-1, keepdims=True) + RMS_EPS)
        gw = g * w_ref[...]
        dot = jnp.mean(gw * x, axis=-1, keepdims=True)
        dh_ref[...] = dres_ref[...] + (r * gw - x * ((r * r * r) * dot))
        part = jnp.sum(g * (x * r), axis=0, keepdims=True)

        @pl.when(i == 0)
        def _():
            dw_ref[...] = jnp.zeros_like(dw_ref)

        dw_ref[...] += jnp.broadcast_to(part, dw_ref.shape)

    row = pl.BlockSpec((tm, d), lambda i: (i, 0))
    return pl.pallas_call(
        body, name=name, grid=(t // tm,),
        in_specs=[row, pl.BlockSpec((1, d), lambda i: (0, 0)), row, row],
        out_specs=[row, pl.BlockSpec((8, d), lambda i: (0, 0))],
        out_shape=[jax.ShapeDtypeStruct((t, d), F32), jax.ShapeDtypeStruct((8, d), F32)],
        compiler_params=_params(),
    )(h, w, du, dres)


def _rms_bwd_first(h, w, du, dres, name):
    t, d = h.shape
    sub = LEAD
    per = _pick((t - LEAD) // sub, 8, 1)
    tmo = per * sub
    n_out = (t - LEAD) // tmo
    n_blocks = t // sub

    def body(*refs):
        w_ref = refs[0]
        h_refs, du_refs, dres_refs = refs[1:1 + per], refs[1 + per:1 + 2 * per], refs[1 + 2 * per:1 + 3 * per]
        gx_ref, dmeta_ref, dw_ref = refs[1 + 3 * per:]
        i = pl.program_id(0)

        def piece(q):
            x = h_refs[q][...]
            g = du_refs[q][...].astype(F32)
            r = lax.rsqrt(jnp.mean(x * x, axis=-1, keepdims=True) + RMS_EPS)
            gw = g * w_ref[...]
            dot = jnp.mean(gw * x, axis=-1, keepdims=True)
            dh = dres_refs[q][...] + (r * gw - x * ((r * r * r) * dot))
            return dh, jnp.sum(g * (x * r), axis=0, keepdims=True)

        @pl.when(i == 0)
        def _():
            dh, part = piece(0)
            dmeta_ref[...] = dh[PAD_ROWS:LEAD, :]
            dw_ref[...] = jnp.broadcast_to(part, dw_ref.shape)

        @pl.when(i > 0)
        def _():
            for q in range(per):
                dh, part = piece(q)
                gx_ref[q * sub:(q + 1) * sub, :] = dh
                dw_ref[...] += jnp.broadcast_to(part, dw_ref.shape)

    def piece_spec(q):
        return pl.BlockSpec((sub, d), lambda i: (jnp.clip(1 + per * (i - 1) + q, 0, n_blocks - 1), 0))

    pieces = [piece_spec(q) for q in range(per)]
    return pl.pallas_call(
        body, name=name, grid=(n_out + 1,),
        in_specs=[pl.BlockSpec((1, d), lambda i: (0, 0))] + pieces * 3,
        out_specs=[pl.BlockSpec((tmo, d), lambda i: (jnp.maximum(i - 1, 0), 0)),
                   pl.BlockSpec((N_META, d), lambda i: (0, 0)), pl.BlockSpec((8, d), lambda i: (0, 0))],
        out_shape=[jax.ShapeDtypeStruct((t - LEAD, d), F32), jax.ShapeDtypeStruct((N_META, d), F32),
                   jax.ShapeDtypeStruct((8, d), F32)],
        compiler_params=_params(),
    )(w, *([h] * per), *([du] * per), *([dres] * per))


def _final_loss(h, w, tgt, name):
    t, d = h.shape
    sub = LEAD
    per = _pick(t // sub, 5, 1)
    tm = per * sub

    def body(h_ref, w_ref, *rest):
        t_refs, (dh_ref, dw_ref, loss_ref) = rest[:per], rest[per:]
        i = pl.program_id(0)

        @pl.when(i == 0)
        def _():
            dw_ref[...] = jnp.zeros_like(dw_ref)
            loss_ref[...] = jnp.zeros_like(loss_ref)

        for q in range(per):
            sl = slice(q * sub, (q + 1) * sub)
            x = h_ref[sl, :]
            r = lax.rsqrt(jnp.mean(x * x, axis=-1, keepdims=True) + RMS_EPS)
            yn = x * r
            y = yn * w_ref[...]
            rows = i * tm + q * sub + lax.broadcasted_iota(jnp.int32, (sub, 1), 0)
            diff = jnp.where(rows >= LEAD, y - t_refs[q][...], 0.0)
            tile_loss = 0.5 * jnp.sum(jnp.mean(diff * diff, axis=-1, keepdims=True), axis=0, keepdims=True)
            dy = diff / d
            gw = dy * w_ref[...]
            dot = jnp.mean(gw * x, axis=-1, keepdims=True)
            dh_ref[sl, :] = r * gw - x * ((r * r * r) * dot)
            dw_ref[...] += jnp.broadcast_to(jnp.sum(dy * yn, axis=0, keepdims=True), dw_ref.shape)
            loss_ref[...] += jnp.broadcast_to(tile_loss, loss_ref.shape)

    row = pl.BlockSpec((tm, d), lambda i: (i, 0))
    piece = [pl.BlockSpec((sub, d), functools.partial(lambda i, q: (jnp.maximum(i * per + q - 1, 0), 0), q=q))
             for q in range(per)]
    return pl.pallas_call(
        body, name=name, grid=(t // tm,),
        in_specs=[row, pl.BlockSpec((1, d), lambda i: (0, 0))] + piece,
        out_specs=[row, pl.BlockSpec((8, d), lambda i: (0, 0)), pl.BlockSpec((8, LANES), lambda i: (0, 0))],
        out_shape=[jax.ShapeDtypeStruct((t, d), F32), jax.ShapeDtypeStruct((8, d), F32),
                   jax.ShapeDtypeStruct((8, LANES), F32)],
        compiler_params=_params(),
    )(h, w, *([tgt] * per))


def _conv_gate_fwd(p0, conv_w, name, exchange=None):
    t = p0.shape[0]
    ec = p0.shape[1] // 8
    tm = _pick(t, 832, 16)
    nt = t // tm

    def body(p_ref, w_ref, g_ref, y_ref, ext_ref):
        i = pl.program_id(1)

        @pl.when(i == 0)
        def _():
            ext_ref[0:8, :] = jnp.zeros((8, ec), F32)

        for rows, cols in _strips(tm, ec):
            cg = p_ref[rows, _shift(cols, ec)].astype(F32)
            xin = p_ref[rows, _shift(cols, 2 * ec)].astype(F32)
            ext_ref[_shift(rows, 8), cols] = cg * xin
        for rows, cols in _strips(tm, ec):
            y = (w_ref[0:1, cols] * ext_ref[_shift(rows, 6), cols] + w_ref[1:2, cols] * ext_ref[_shift(rows, 7), cols]
                 + w_ref[2:3, cols] * ext_ref[_shift(rows, 8), cols])
            bg = p_ref[rows, cols].astype(F32)
            z = p_ref[rows, _shift(cols, 3 * ec)].astype(F32)
            y_ref[rows, cols] = y.astype(BF16)
            g_ref[rows, cols] = ((z * _sigmoid(z)) * bg * y).astype(BF16)
        ext_ref[0:8, :] = ext_ref[tm:tm + 8, :]

    out = pl.BlockSpec((tm, ec), lambda j, i: (i, j))
    return _call(
        body, name=name, grid=(2, nt),
        in_specs=[pl.BlockSpec((tm, 4 * ec), lambda j, i: (i, j)), pl.BlockSpec((3, ec), lambda j, i: (0, j))],
        out_specs=[out, out],
        out_shape=[jax.ShapeDtypeStruct((t, 2 * ec), BF16)] * 2,
        scratch_shapes=[pltpu.VMEM((tm + 8, ec), F32)], args=(p0, conv_w), exchange=exchange)


def _conv_gate_bwd(p0, y, dg, conv_w, name, exchange=None):
    t = p0.shape[0]
    ec = p0.shape[1] // 8
    tm = _pick(t, 832, 16)
    nt = t // tm

    def body(p_ref, y_ref, dg_ref, w_ref, dp_ref, dw_ref, ext_ref):
        i = pl.program_id(1)

        @pl.when(i == 0)
        def _():
            ext_ref[tm:tm + 8, :] = jnp.zeros((8, ec), F32)
            dw_ref[...] = jnp.zeros_like(dw_ref)

        for rows, cols in _strips(tm, ec):
            bg = p_ref[rows, cols].astype(F32)
            z = p_ref[rows, _shift(cols, 3 * ec)].astype(F32)
            yv = y_ref[rows, cols].astype(F32)
            dgv = dg_ref[rows, cols].astype(F32)
            sig = _sigmoid(z)
            sz = z * sig
            dp_ref[rows, _shift(cols, 3 * ec)] = (dgv * bg * yv * (sig * (1.0 + z * (1.0 - sig)))).astype(BF16)
            dp_ref[rows, cols] = (dgv * sz * yv).astype(BF16)
            ext_ref[rows, cols] = dgv * sz * bg
        acc = None
        for rows, cols in _strips(tm, ec):
            if rows.start == 0:
                acc = [jnp.zeros((STRIP_ROWS, cols.stop - cols.start), F32) for _ in range(3)]
            dy = ext_ref[rows, cols]
            dy1 = ext_ref[_shift(rows, 1), cols]
            dy2 = ext_ref[_shift(rows, 2), cols]
            cg = p_ref[rows, _shift(cols, ec)].astype(F32)
            xin = p_ref[rows, _shift(cols, 2 * ec)].astype(F32)
            da = w_ref[0:1, cols] * dy2 + w_ref[1:2, cols] * dy1 + w_ref[2:3, cols] * dy
            dp_ref[rows, _shift(cols, ec)] = (da * xin).astype(BF16)
            dp_ref[rows, _shift(cols, 2 * ec)] = (da * cg).astype(BF16)
            a = cg * xin
            acc = [acc[0] + a * dy2, acc[1] + a * dy1, acc[2] + a * dy]
            if rows.stop == tm:
                for tap in range(3):
                    dw_ref[tap:tap + 1, cols] += jnp.sum(acc[tap], axis=0, keepdims=True)
        ext_ref[tm:tm + 8, :] = ext_ref[0:8, :]

    rev = lambda j, i: (nt - 1 - i, j)
    return _call(
        body, name=name, grid=(2, nt),
        in_specs=[pl.BlockSpec((tm, 4 * ec), rev), pl.BlockSpec((tm, ec), rev), pl.BlockSpec((tm, ec), rev),
                  pl.BlockSpec((3, ec), lambda j, i: (0, j))],
        out_specs=[pl.BlockSpec((tm, 4 * ec), rev), pl.BlockSpec((8, ec), lambda j, i: (0, j))],
        out_shape=[jax.ShapeDtypeStruct((t, 8 * ec), BF16), jax.ShapeDtypeStruct((8, 2 * ec), F32)],
        scratch_shapes=[pltpu.VMEM((tm + 8, ec), F32)], args=(p0, y, dg, conv_w), exchange=exchange)


def _log_sigmoid(x):
    return jnp.minimum(x, 0.0) - jnp.log(1.0 + jnp.exp(-jnp.abs(x)))


def _gates_fwd(graw, graw_t, bias_row, bias_col, name):
    t = graw.shape[0]
    tm = _pick(t, 640, 128)
    cpt = tm // CHUNK
    nh = N_HEADS

    def body(g_ref, gt_ref, br_ref, bc_ref, colf_ref, rowf_ref):
        i = pl.program_id(0)
        gc = g_ref[...] + br_ref[...]
        rows = i * tm + lax.broadcasted_iota(jnp.int32, (tm, 1), 0)
        live = rows >= PAD_ROWS
        lf = jnp.where(live, _log_sigmoid(gc), 0.0)
        li = jnp.where(live, gc, NEG)
        gt = gt_ref[...] + bc_ref[...]
        cols = i * tm + lax.broadcasted_iota(jnp.int32, (1, tm), 1)
        live_t = cols >= PAD_ROWS
        lf_t = jnp.where(live_t, _log_sigmoid(gt), 0.0)
        li_t = jnp.where(live_t, gt, NEG)
        ri = lax.broadcasted_iota(jnp.int32, (CHUNK, CHUNK), 0)
        ci = lax.broadcasted_iota(jnp.int32, (CHUNK, CHUNK), 1)
        lower = (ri >= ci).astype(F32)
        upper = (ri <= ci).astype(F32)
        lane = lax.broadcasted_iota(jnp.int32, (CHUNK, LANES), 1)
        sub = lax.broadcasted_iota(jnp.int32, (8, CHUNK), 0)
        for c in range(cpt):
            sl = slice(c * CHUNK, (c + 1) * CHUNK)
            b_all = jnp.dot(lower, lf[sl, :], precision=HIGHEST, preferred_element_type=F32)
            bt_all = jnp.dot(lf_t[:, sl], upper, precision=HIGHEST, preferred_element_type=F32)
            for h in range(nh):
                b = b_all[:, nh + h:nh + h + 1]
                r = li[sl, h:h + 1] - b
                pre = gc[sl, nh + h:nh + h + 1]
                colf_ref[h, sl, :] = jnp.where(lane == 0, b, jnp.where(lane == 1, r, jnp.where(lane == 2, pre, 0.0)))
                b_row = bt_all[nh + h:nh + h + 1, :]
                r_row = li_t[h:h + 1, sl] - b_row
                rowf_ref[h, c, :, 0:CHUNK] = jnp.where(sub == 0, r_row, jnp.where(sub == 1, b_row, 0.0))
                rowf_ref[h, c, :, CHUNK:LANES] = jnp.zeros((8, LANES - CHUNK), F32)

    return pl.pallas_call(
        body, name=name, grid=(t // tm,),
        in_specs=[pl.BlockSpec((tm, LANES), lambda i: (i, 0)), pl.BlockSpec((16, tm), lambda i: (0, i)),
                  pl.BlockSpec((1, LANES), lambda i: (0, 0)), pl.BlockSpec((16, 1), lambda i: (0, 0))],
        out_specs=[pl.BlockSpec((nh, tm, LANES), lambda i: (0, i, 0)),
                   pl.BlockSpec((nh, cpt, 8, LANES), lambda i: (0, i, 0, 0))],
        out_shape=[jax.ShapeDtypeStruct((nh, t, LANES), F32),
                   jax.ShapeDtypeStruct((nh, t // CHUNK, 8, LANES), F32)],
        compiler_params=_params(),
    )(graw, graw_t, bias_row, bias_col)


def _gates_bwd(gstat, colf, dp1, name):
    nh, t, _ = gstat.shape
    tm = _pick(t, 640, 128)
    cpt = tm // CHUNK
    nt = t // tm
    nc = t // CHUNK
    gate_block = dp1.shape[1] // LANES - 1

    def body(gs_ref, nx_ref, colf_ref, dp_any, dg_ref, db_ref):
        i = pl.program_id(0)
        ri = lax.broadcasted_iota(jnp.int32, (CHUNK, CHUNK), 0)
        ci = lax.broadcasted_iota(jnp.int32, (CHUNK, CHUNK), 1)
        upper = (ri <= ci).astype(F32)
        lane = lax.broadcasted_iota(jnp.int32, (CHUNK, LANES), 1)
        total = jnp.zeros((1, LANES), F32)
        for c in range(cpt):
            sl = slice(c * CHUNK, (c + 1) * CHUNK)
            rows = i * tm + c * CHUNK + lax.broadcasted_iota(jnp.int32, (CHUNK, 1), 0)
            live = rows >= PAD_ROWS
            acc = jnp.zeros((CHUNK, LANES), F32)
            for h in range(nh):
                blk = gs_ref[h, sl, :]
                rev = jnp.dot(upper, blk, precision=HIGHEST, preferred_element_type=F32)
                if c + 1 < cpt:
                    carry = gs_ref[h, (c + 1) * CHUNK:(c + 1) * CHUNK + 1, 2:3]
                else:
                    carry = jnp.where(i == nt - 1, 0.0, nx_ref[h, 0:1, 2:3])
                dlogf = rev[:, 0:1] + carry
                pre = colf_ref[h, sl, 2:3]
                dgf = jnp.where(live, dlogf * (1.0 - _sigmoid(pre)), 0.0)
                dgi = jnp.where(live, blk[:, 1:2], 0.0)
                acc = acc + jnp.where(lane == h, dgi, 0.0) + jnp.where(lane == nh + h, dgf, 0.0)
            dg_ref[sl, :] = acc.astype(BF16)
            total = total + jnp.sum(acc, axis=0, keepdims=True)

        @pl.when(i == 0)
        def _():
            db_ref[...] = jnp.zeros_like(db_ref)

        db_ref[...] += jnp.broadcast_to(total, db_ref.shape)

    return pl.pallas_call(
        body, name=name, grid=(nt,),
        in_specs=[pl.BlockSpec((nh, tm, LANES), lambda i: (0, i, 0)),
                  pl.BlockSpec((nh, CHUNK, LANES), lambda i: (0, jnp.minimum((i + 1) * cpt, nc - 1), 0)),
                  pl.BlockSpec((nh, tm, LANES), lambda i: (0, i, 0)),
                  pl.BlockSpec(memory_space=pl.ANY)],
        out_specs=[pl.BlockSpec((tm, LANES), lambda i: (i, gate_block)), pl.BlockSpec((8, LANES), lambda i: (0, 0))],
        out_shape=[jax.ShapeDtypeStruct(dp1.shape, dp1.dtype), jax.ShapeDtypeStruct((8, LANES), F32)],
        input_output_aliases={3: 0},
        compiler_params=_params(),
    )(gstat, gstat, colf, dp1)


NT_DIMS = (((1,), (1,)), ((), ()))
TN_DIMS = (((0,), (0,)), ((), ()))


def _dot(a, b):
    return jnp.dot(a.astype(BF16), b.astype(BF16), preferred_element_type=F32)


def _dot_nt(a, b):
    return lax.dot_general(a.astype(BF16), b.astype(BF16), NT_DIMS, preferred_element_type=F32)


def _dot_tn(a, b):
    return lax.dot_general(a.astype(BF16), b.astype(BF16), TN_DIMS, preferred_element_type=F32)


def _chunk_gates(colf_ref, rowf_ref, m_prev, width=CHUNK):
    b = colf_ref[:, 0:1]
    rcol = colf_ref[:, 1:2]
    rrow = rowf_ref[0:1, 0:width]
    ri = lax.broadcasted_iota(jnp.int32, (CHUNK, width), 0)
    ci = lax.broadcasted_iota(jnp.int32, (CHUNK, width), 1)
    log_d = jnp.where(ri >= ci, b + rrow, NEG)
    m_row = jnp.maximum(b + m_prev, jnp.max(log_d, axis=-1, keepdims=True))
    dmat = jnp.exp(log_d - m_row)
    inter = jnp.exp(b + m_prev - m_row)
    b_last = b[CHUNK - 1:CHUNK, :]
    log_w = rcol + b_last
    m_new = jnp.maximum(b_last + m_prev, jnp.max(log_w, axis=0, keepdims=True))
    decay = jnp.exp(b_last + m_prev - m_new)
    w = jnp.exp(log_w - m_new)
    return m_row, dmat, inter, m_new, decay, w


def _mlstm_fwd(p1, colf, rowf, dk, dv, name):
    t = p1.shape[0]
    nh = N_HEADS
    nc = t // CHUNK
    hw = 2 * dk + dv
    cps = CHUNKS_PER_STEP if nc % CHUNKS_PER_STEP == 0 else 1
    rows_per_step = cps * CHUNK

    def body(p_ref, colf_ref, rowf_ref, hh_ref, stat_ref, cs_ref, ns_ref, *state):
        c_refs, n_refs, m_refs = state[:nh], state[nh:2 * nh], state[2 * nh:]
        c = pl.program_id(0)

        @pl.when(c == 0)
        def _():
            for ref in state:
                ref[...] = jnp.zeros_like(ref)

        for cc in range(cps):
            rows = pl.ds(cc * CHUNK, CHUNK)
            for h in range(nh):
                head(p_ref.at[rows, pl.ds(h * hw, hw)], colf_ref.at[h, rows], rowf_ref.at[h, cc],
                     hh_ref.at[rows, pl.ds(h * dv, dv)], stat_ref.at[h, rows], cs_ref.at[h, cc],
                     ns_ref.at[h, cc], c_refs[h], n_refs[h], m_refs[h])

    def head(p_ref, colf_ref, rowf_ref, hh_ref, stat_ref, cs_ref, ns_ref, c_ref, n_ref, m_ref):
        m_prev = m_ref[...]
        n_prev = n_ref[...]
        c_prev = c_ref[...]
        cs_ref[...] = c_prev.astype(BF16)
        sub = lax.broadcasted_iota(jnp.int32, (8, dk), 0)
        ns_ref[...] = jnp.where(sub == 0, n_prev, jnp.where(sub == 1, m_prev, 0.0))

        q = p_ref[:, 0:dk]
        k = p_ref[:, dk:2 * dk]
        v = p_ref[:, 2 * dk:2 * dk + dv]
        m_row, dmat, inter, m_new, decay, w = _chunk_gates(colf_ref, rowf_ref, m_prev)
        s = _dot_nt(q, k) * dmat
        num = _dot(s, v) + inter * _dot(q, c_prev)
        den = jnp.sum(s, axis=-1, keepdims=True) + inter * jnp.sum(q.astype(F32) * n_prev, axis=-1, keepdims=True)
        denom = jnp.maximum(jnp.abs(den), jnp.exp(-m_row))
        hh_ref[...] = num * (1.0 / denom)
        lane = lax.broadcasted_iota(jnp.int32, (CHUNK, LANES), 1)
        stat_ref[...] = jnp.where(lane == 0, den, 0.0)

        wk = w * k.astype(F32)
        c_ref[...] = decay * c_prev + _dot_tn(wk, v)
        n_ref[...] = decay * n_prev + jnp.sum(wk, axis=0, keepdims=True)
        m_ref[...] = m_new

    return pl.pallas_call(
        body, name=name, grid=(nc // cps,),
        in_specs=[pl.BlockSpec((rows_per_step, nh * hw), lambda c: (c, 0)),
                  pl.BlockSpec((nh, rows_per_step, LANES), lambda c: (0, c, 0)),
                  pl.BlockSpec((nh, cps, 8, LANES), lambda c: (0, c, 0, 0))],
        out_specs=[pl.BlockSpec((rows_per_step, nh * dv), lambda c: (c, 0)),
                   pl.BlockSpec((nh, rows_per_step, LANES), lambda c: (0, c, 0)),
                   pl.BlockSpec((nh, cps, dk, dv), lambda c: (0, c, 0, 0)),
                   pl.BlockSpec((nh, cps, 8, dk), lambda c: (0, c, 0, 0))],
        out_shape=[jax.ShapeDtypeStruct((t, nh * dv), F32), jax.ShapeDtypeStruct((nh, t, LANES), F32),
                   jax.ShapeDtypeStruct((nh, nc, dk, dv), BF16), jax.ShapeDtypeStruct((nh, nc, 8, dk), F32)],
        scratch_shapes=([pltpu.VMEM((dk, dv), F32)] * nh + [pltpu.VMEM((1, dk), F32)] * nh
                        + [pltpu.VMEM((1, 1), F32)] * nh),
        compiler_params=_params(),
    )(p1, colf, rowf)


def _head_gate_fwd(hh, p1, head_w, dv, name):
    t = hh.shape[0]
    nh = N_HEADS
    e = nh * dv
    tm = _pick(t, 416, 16)

    def body(hh_ref, oz_ref, w_ref, hm_ref):
        for h in range(nh):
            cols = slice(h * dv, (h + 1) * dv)
            x = hh_ref[:, cols]
            o = oz_ref[:, 2 * h * dv:(2 * h + 1) * dv].astype(F32)
            z = oz_ref[:, (2 * h + 1) * dv:(2 * h + 2) * dv].astype(F32)
            r = lax.rsqrt(jnp.mean(x * x, axis=-1, keepdims=True) + RMS_EPS)
            hn = (x * r) * w_ref[:, cols]
            hm_ref[:, cols] = (hn * _sigmoid(o) * (z * _sigmoid(z))).astype(BF16)

    return pl.pallas_call(
        body, name=name, grid=(t // tm,),
        in_specs=[pl.BlockSpec((tm, e), lambda i: (i, 0)), pl.BlockSpec((tm, 2 * e), lambda i: (i, 1)),
                  pl.BlockSpec((1, e), lambda i: (0, 0))],
        out_specs=pl.BlockSpec((tm, e), lambda i: (i, 0)),
        out_shape=jax.ShapeDtypeStruct((t, e), BF16),
        compiler_params=_params(),
    )(hh, p1, head_w)


def _mlstm_bwd(p1, colf, rowf, head_w, hh, stat, csave, nsave, dhm, n_cols, dk, dv, name, exchange=None):
    t = p1.shape[0]
    nh = N_HEADS
    nc = t // CHUNK
    hw = 2 * dk + dv
    cps = CHUNKS_PER_STEP if nc % CHUNKS_PER_STEP == 0 else 1
    rows_per_step = cps * CHUNK

    def body(p_ref, colf_ref, rowf_ref, w_ref, hh_ref, stat_ref, cs_ref, ns_ref, dhm_ref,
             dp_ref, gs_ref, dw_ref, *state):
        dc_refs, dn_refs, dwacc_refs = state[:nh], state[nh:2 * nh], state[2 * nh:]
        c = pl.program_id(0)

        @pl.when(c == 0)
        def _():
            for ref in state:
                ref[...] = jnp.zeros_like(ref)

        for cc in reversed(range(cps)):
            rows = pl.ds(cc * CHUNK, CHUNK)
            for h in range(nh):
                cols = pl.ds(h * dv, dv)
                oz = pl.ds(nh * hw + h * 2 * dv, 2 * dv)
                head(p_ref.at[rows, pl.ds(h * hw, hw)], p_ref.at[rows, oz], colf_ref.at[h, rows], rowf_ref.at[h, cc],
                     w_ref.at[:, cols], hh_ref.at[rows, cols], stat_ref.at[h, rows], cs_ref.at[h, cc],
                     ns_ref.at[h, cc], dhm_ref.at[rows, cols], dp_ref.at[rows, pl.ds(h * hw, hw)],
                     dp_ref.at[rows, oz], gs_ref.at[h, rows], dwacc_refs[h], dc_refs[h], dn_refs[h])

        @pl.when(c == nc // cps - 1)
        def _():
            for h in range(nh):
                dw_ref[h] = dwacc_refs[h][...]

    def head(p_ref, oz_ref, colf_ref, rowf_ref, w_ref, hh_ref, stat_ref, cs_ref, ns_ref, dhm_ref,
             dp_ref, doz_ref, gs_ref, dw_ref, dc_ref, dn_ref):
        q = p_ref[:, 0:dk]
        k = p_ref[:, dk:2 * dk]
        v = p_ref[:, 2 * dk:2 * dk + dv]
        o = oz_ref[:, 0:dv].astype(F32)
        z = oz_ref[:, dv:2 * dv].astype(F32)
        qf = q.astype(F32)
        kf = k.astype(F32)
        n_prev = ns_ref[0:1, :]
        m_prev = ns_ref[1:2, 0:1]
        c_prev = cs_ref[...]
        m_row, dmat, inter, m_new, decay, w = _chunk_gates(colf_ref, rowf_ref, m_prev)

        hh = hh_ref[...]
        dhm_v = dhm_ref[...].astype(F32)
        so = _sigmoid(o)
        sg = _sigmoid(z)
        sz = z * sg
        r = lax.rsqrt(jnp.mean(hh * hh, axis=-1, keepdims=True) + RMS_EPS)
        hn = (hh * r) * w_ref[...]
        dhn = dhm_v * so * sz
        doz_ref[:, 0:dv] = (dhm_v * hn * sz * (so * (1.0 - so))).astype(BF16)
        doz_ref[:, dv:2 * dv] = (dhm_v * hn * so * (sg * (1.0 + z * (1.0 - sg)))).astype(BF16)
        dw_ref[...] += jnp.broadcast_to(jnp.sum(dhn * (hh * r), axis=0, keepdims=True), dw_ref.shape)
        gwn = dhn * w_ref[...]
        dhh = r * gwn - hh * ((r * r * r) * jnp.mean(gwn * hh, axis=-1, keepdims=True))

        den = stat_ref[:, 0:1]
        floor = jnp.exp(-m_row)
        denom = jnp.maximum(jnp.abs(den), floor)
        inv = 1.0 / denom
        dnum = dhh * inv
        hdot = jnp.sum(dhh * hh, axis=-1, keepdims=True)
        dden = jnp.where(jnp.abs(den) > floor, -(hdot * inv) * jnp.sign(den), 0.0)
        s = _dot_nt(q, k) * dmat
        dqk = (_dot_nt(dnum, v) + dden) * dmat
        dc_new = dc_ref[...]
        dn_new = dn_ref[...]
        idd = inter * dden
        dq = _dot(dqk, k) + inter * _dot_nt(dnum, c_prev) + idd * n_prev
        dkv = _dot_tn(dqk, q) + w * (_dot_nt(v, dc_new) + dn_new)
        dvv = _dot_tn(s, dnum) + w * _dot(k, dc_new)
        dp_ref[:, 0:dk] = dq.astype(BF16)
        dp_ref[:, dk:2 * dk] = dkv.astype(BF16)
        dp_ref[:, 2 * dk:2 * dk + dv] = dvv.astype(BF16)
        qdq = jnp.sum(qf * dq, axis=-1, keepdims=True)
        kdk = jnp.sum(kf * dkv, axis=-1, keepdims=True)
        dc_prev = decay * dc_new + _dot_tn(inter * qf, dnum)
        dn_prev = decay * dn_new + jnp.sum(idd * qf, axis=0, keepdims=True)
        dc_ref[...] = dc_prev
        dn_ref[...] = dn_prev
        cross = (jnp.sum(jnp.sum(c_prev.astype(F32) * dc_prev, axis=-1, keepdims=True), axis=0, keepdims=True)
                 + jnp.sum(n_prev * dn_prev, axis=-1, keepdims=True))
        lane = lax.broadcasted_iota(jnp.int32, (CHUNK, LANES), 1)
        gs_ref[...] = jnp.where(lane == 0, qdq - kdk, jnp.where(lane == 1, kdk, jnp.where(lane == 2, cross, 0.0)))

    ns = nc // cps
    rc = lambda c: (ns - 1 - c, 0)
    rc3 = lambda c: (0, ns - 1 - c, 0)
    rc4 = lambda c: (0, ns - 1 - c, 0, 0)
    return _call(
        body, name=name, grid=(ns,),
        in_specs=[pl.BlockSpec((rows_per_step, nh * (hw + 2 * dv)), rc),
                  pl.BlockSpec((nh, rows_per_step, LANES), rc3),
                  pl.BlockSpec((nh, cps, 8, LANES), rc4),
                  pl.BlockSpec((1, nh * dv), lambda c: (0, 0)),
                  pl.BlockSpec((rows_per_step, nh * dv), rc),
                  pl.BlockSpec((nh, rows_per_step, LANES), rc3),
                  pl.BlockSpec((nh, cps, dk, dv), rc4),
                  pl.BlockSpec((nh, cps, 8, dk), rc4),
                  pl.BlockSpec((rows_per_step, nh * dv), rc)],
        out_specs=[pl.BlockSpec((rows_per_step, nh * (hw + 2 * dv)), rc),
                   pl.BlockSpec((nh, rows_per_step, LANES), rc3),
                   pl.BlockSpec((nh, 8, dv), lambda c: (0, 0, 0))],
        out_shape=[jax.ShapeDtypeStruct((t, n_cols), BF16), jax.ShapeDtypeStruct((nh, t, LANES), F32),
                   jax.ShapeDtypeStruct((nh, 8, dv), F32)],
        scratch_shapes=([pltpu.VMEM((dk, dv), F32)] * nh + [pltpu.VMEM((1, dk), F32)] * nh
                        + [pltpu.VMEM((8, dv), F32)] * nh),
        args=(p1, colf, rowf, head_w, hh, stat, csave, nsave, dhm), exchange=exchange)


def _position():
    return lax.axis_index("x"), lax.axis_index("y"), lax.axis_index("c")


class _Exchange:
    def __init__(self, ins, out_shapes, start, mid, finish):
        n = len(ins)
        self.ins, self.out_shapes = list(ins), list(out_shapes)
        self.start, self.mid, self.finish = start, mid, finish
        self.scratch = [pltpu.SemaphoreType.DMA((n, 7)), pltpu.SemaphoreType.DMA((n, 7)),
                        pltpu.SemaphoreType.DMA((n,))]


def _gather_exchange(shards):
    n = len(shards)

    def plan(ins, outs, sems):
        send_sems, recv_sems, local_sems = sems
        x, y, c = _position()
        me, sibling = (x, y, c), (x, y, 1 - c)
        chips = [(1 - x, y), (x, 1 - y), (1 - x, 1 - y)]

        def copy(a, k, block, to, src=None):
            px, py, pc = block
            dst = outs[a].at[4 * px + 2 * py + pc]
            return pltpu.make_async_remote_copy(
                src_ref=dst if src is None else src, dst_ref=dst,
                send_sem=send_sems.at[a, k], recv_sem=recv_sems.at[a, k],
                device_id=to, device_id_type=MESH)

        def mine():
            return [pltpu.make_async_copy(ins[a], outs[a].at[4 * x + 2 * y + c], local_sems.at[a])
                    for a in range(n)]

        def first():
            out = []
            for a in range(n):
                out.append(copy(a, 0, me, sibling, src=ins[a]))
                out += [copy(a, 1 + j, me, (*chip, c), src=ins[a]) for j, chip in enumerate(chips)]
            return out

        def ici_in():
            return [copy(a, 1 + j, (*chip, c), me) for j, chip in enumerate(chips) for a in range(n)]

        def passed():
            return [copy(a, 4 + j, (*chip, c), sibling) for j, chip in enumerate(chips) for a in range(n)]

        def d2d_in():
            return ([copy(a, 0, sibling, me) for a in range(n)]
                    + [copy(a, 4 + j, (*chip, 1 - c), me) for j, chip in enumerate(chips) for a in range(n)])

        return mine, first, ici_in, passed, d2d_in

    def start(ins, outs, sems):
        mine, first, _, _, _ = plan(ins, outs, sems)
        for cp in mine() + first():
            cp.start()

    def mid(ins, outs, sems):
        _, _, ici_in, passed, _ = plan(ins, outs, sems)
        for arrived, onward in zip(ici_in(), passed()):
            arrived.wait_recv()
            onward.start()

    def finish(ins, outs, sems):
        mine, first, _, passed, d2d_in = plan(ins, outs, sems)
        for cp in d2d_in():
            cp.wait_recv()
        for cp in first() + passed():
            cp.wait_send()
        for cp in mine():
            cp.wait()

    shapes = [jax.ShapeDtypeStruct((N_DEV,) + s.shape, s.dtype) for s in shards]
    return _Exchange(shards, shapes, start, mid, finish)


def _scatter_exchange(fulls):
    n = len(fulls)

    def plan(ins, outs, sems):
        send_sems, recv_sems, local_sems = sems
        x, y, c = _position()
        my_slot = 4 * x + 2 * y + c

        def mine():
            return [pltpu.make_async_copy(ins[a].at[my_slot], outs[a].at[my_slot], local_sems.at[a])
                    for a in range(n)]

        def remote(arriving):
            out = []
            for kk in (1, 2, 4, 6, 3, 5, 7):
                kx, ky, kc = (kk >> 2) & 1, (kk >> 1) & 1, kk & 1
                px = 1 - x if kx else x
                py = 1 - y if ky else y
                pc = 1 - c if kc else c
                peer_slot = 4 * px + 2 * py + pc
                for a in range(n):
                    out.append(pltpu.make_async_remote_copy(
                        src_ref=ins[a].at[peer_slot], dst_ref=outs[a].at[peer_slot if arriving else my_slot],
                        send_sem=send_sems.at[a, kk - 1], recv_sem=recv_sems.at[a, kk - 1],
                        device_id=(px, py, pc), device_id_type=MESH))
            return out

        return mine, remote

    def start(ins, outs, sems):
        mine, remote = plan(ins, outs, sems)
        for cp in mine() + remote(False):
            cp.start()

    def finish(ins, outs, sems):
        mine, remote = plan(ins, outs, sems)
        for cp in remote(True):
            cp.wait_recv()
        for cp in remote(False):
            cp.wait_send()
        for cp in mine():
            cp.wait()

    shapes = [jax.ShapeDtypeStruct(f.shape, f.dtype) for f in fulls]
    return _Exchange(fulls, shapes, start, None, finish)


def _run_exchange(ex, name):
    n_in, n_out = len(ex.ins), len(ex.out_shapes)

    def body(*refs):
        ins, outs, sems = refs[:n_in], refs[n_in:n_in + n_out], refs[n_in + n_out:]
        ex.start(ins, outs, sems)
        if ex.mid is not None:
            ex.mid(ins, outs, sems)
        ex.finish(ins, outs, sems)

    any_spec = pl.BlockSpec(memory_space=pl.ANY)
    return pl.pallas_call(
        body, name=name,
        in_specs=[any_spec] * n_in, out_specs=[any_spec] * n_out,
        out_shape=ex.out_shapes, scratch_shapes=ex.scratch,
    )(*ex.ins)


def _natural_to_head_major(n0, dk, dv):
    nh = N_HEADS
    qk, e, hw = nh * dk, nh * dv, 2 * dk + dv
    if n0 < qk:
        h, off = divmod(n0, dk)
        return h * hw + off, True
    n1 = n0 - qk
    if n1 < qk:
        h, off = divmod(n1, dk)
        return h * hw + dk + off, False
    part, n3 = divmod(n1 - qk, e)
    h, off = divmod(n3, dv)
    if part == 0:
        return h * hw + 2 * dk + off, False
    return nh * hw + h * 2 * dv + (part - 1) * dv + off, False


def _mlstm_weight_layout(w_blocks, dk, dv, name):
    _, d, blk = w_blocks.shape
    nh = N_HEADS
    hw = 2 * dk + 3 * dv
    n_tiles = nh * hw // LANES
    tiles_per_block = d // LANES
    tr = _pick(d, 128, 16)
    scale = dk ** -0.5

    def body(w_ref, p_ref, g_ref):
        lane = lax.broadcasted_iota(jnp.int32, (tr, LANES), 1)

        def tile(s, r):
            return w_ref[s, :, r * LANES:(r + 1) * LANES].astype(F32)

        def last_col(s):
            return w_ref[s, :, d:d + 1].astype(F32)

        for tn in range(n_tiles):
            s, r = divmod(tn, tiles_per_block)
            if s == 0:
                val = tile(0, r)
            elif r == 0:
                from_prev = pltpu.roll(tile(s - 1, tiles_per_block - 1), s - 1, 1)
                from_here = pltpu.roll(tile(s, 0), s, 1)
                val = jnp.where(lane < s - 1, from_prev, jnp.where(lane == s - 1, last_col(s - 1), from_here))
            else:
                slab = jnp.concatenate([tile(s, r - 1), tile(s, r)], axis=1)
                val = pltpu.roll(slab, s, 1)[:, LANES:]
            at, is_q = _natural_to_head_major(tn * LANES, dk, dv)
            if is_q:
                val = val * scale
            p_ref[:, at:at + LANES] = val.astype(BF16)
        n_gate = 2 * nh
        s = N_DEV - 1
        gates = pltpu.roll(tile(s, tiles_per_block - 1), n_gate - 1, 1)
        gates = jnp.where(lane < n_gate - 1, gates, jnp.where(lane == n_gate - 1, last_col(s), 0.0)).astype(BF16)
        g_ref[...] = gates
        p_ref[:, nh * hw:nh * hw + LANES] = gates

    return pl.pallas_call(
        body, name=name, grid=(d // tr,),
        in_specs=[pl.BlockSpec((N_DEV, tr, blk), lambda i: (0, i, 0))],
        out_specs=[pl.BlockSpec((tr, nh * hw + LANES), lambda i: (i, 0)), pl.BlockSpec((tr, LANES), lambda i: (i, 0))],
        out_shape=[jax.ShapeDtypeStruct((d, nh * hw + LANES), BF16), jax.ShapeDtypeStruct((d, LANES), BF16)],
        compiler_params=_params(),
    )(w_blocks)


def _mlstm_grad_layout(g_p1, row0, n_rows, dk, dv, name):
    d = g_p1.shape[0]
    nh = N_HEADS
    hw = 2 * dk + 3 * dv
    n_tiles = nh * hw // LANES
    tiles_per_block = d // LANES
    tr = _pick(d, 128, 16)
    scale = dk ** -0.5

    def body(p_ref, o_ref):
        def natural(tn):
            if tn == n_tiles:
                return p_ref[:, nh * hw:nh * hw + LANES].astype(F32)
            at, is_q = _natural_to_head_major(tn * LANES, dk, dv)
            val = p_ref[:, at:at + LANES].astype(F32)
            return val * scale if is_q else val

        for s in range(N_DEV):
            for r in range(tiles_per_block):
                tn = s * tiles_per_block + r
                if s == 0:
                    val = natural(tn)
                else:
                    slab = jnp.concatenate([natural(tn), natural(tn + 1)], axis=1)
                    val = pltpu.roll(slab, 2 * LANES - s, 1)[:, :LANES]
                o_ref[s, :, r * LANES:(r + 1) * LANES] = val.astype(BF16)
            o_ref[s, :, d:d + 1] = natural((s + 1) * tiles_per_block)[:, s:s + 1].astype(BF16)

    first = row0 // tr
    return pl.pallas_call(
        body, name=name, grid=(n_rows // tr,),
        in_specs=[pl.BlockSpec((tr, nh * hw + LANES), lambda i: (i + first, 0))],
        out_specs=pl.BlockSpec((N_DEV, tr, d + 1), lambda i: (0, i, 0)),
        out_shape=jax.ShapeDtypeStruct((N_DEV, n_rows, d + 1), BF16),
        compiler_params=_params(),
    )(g_p1)


def _adamw_math(w, g, m, v):
    m = ADAM_B1 * m + (1.0 - ADAM_B1) * g
    v = ADAM_B2 * v + (1.0 - ADAM_B2) * (g * g)
    m_hat = m / (1.0 - ADAM_B1 ** ADAM_STEP)
    v_hat = v / (1.0 - ADAM_B2 ** ADAM_STEP)
    delta = -ADAM_LR * (m_hat / (jnp.sqrt(v_hat) + ADAM_EPS) + ADAM_WD * w)
    return delta, m, v


def _adamw_sharded(parts, w, m, v, name):
    n_parts = len(parts)
    _, rk, c = parts[0].shape
    r = n_parts * rk
    tr = _pick(rk, 128, 8)
    per = rk // tr

    def body(*refs):
        p_refs = refs[:n_parts]
        w_ref, m_ref, v_ref, g_ref, d_ref, nm_ref, nv_ref, gsum_ref = refs[n_parts:]
        i = pl.program_id(0)
        for k, p_ref in enumerate(p_refs):
            @pl.when(i // per == k)
            def _(p_ref=p_ref):
                g = p_ref[0].astype(F32)
                for s in range(1, N_DEV):
                    g = g + p_ref[s].astype(F32)
                gsum_ref[...] = g
        g = gsum_ref[...]
        delta, m_new, v_new = _adamw_math(w_ref[...], g, m_ref[...], v_ref[...])
        g_ref[...] = g
        d_ref[...] = delta
        nm_ref[...] = m_new
        nv_ref[...] = v_new

    def part_spec(k):
        return pl.BlockSpec((N_DEV, tr, c), lambda i: (0, jnp.clip(i - k * per, 0, per - 1), 0))

    blk = pl.BlockSpec((tr, c), lambda i: (i, 0))
    return pl.pallas_call(
        body, name=name, grid=(r // tr,),
        in_specs=[part_spec(k) for k in range(n_parts)] + [blk, blk, blk],
        out_specs=[blk] * 4,
        out_shape=[jax.ShapeDtypeStruct((r, c), F32)] * 4,
        scratch_shapes=[pltpu.VMEM((tr, c), F32)],
        compiler_params=_params(),
    )(*parts, w, m, v)


def _sum_devices(parts, name):
    _, r, c = parts.shape

    def body(p_ref, o_ref):
        g = p_ref[0]
        for s in range(1, N_DEV):
            g = g + p_ref[s]
        o_ref[...] = g

    return pl.pallas_call(
        body, name=name, out_shape=jax.ShapeDtypeStruct((r, c), F32), compiler_params=_params(),
    )(parts)


def _adamw_small(gs, ws, ms, vs, name):
    n = len(gs)

    def body(*refs):
        g_refs, w_refs, m_refs, v_refs = refs[:n], refs[n:2 * n], refs[2 * n:3 * n], refs[3 * n:4 * n]
        d_refs, nm_refs, nv_refs = refs[4 * n:5 * n], refs[5 * n:6 * n], refs[6 * n:7 * n]
        for a in range(n):
            delta, m_new, v_new = _adamw_math(w_refs[a][...], g_refs[a][...], m_refs[a][...], v_refs[a][...])
            d_refs[a][...] = delta
            nm_refs[a][...] = m_new
            nv_refs[a][...] = v_new

    shapes = [jax.ShapeDtypeStruct(w.shape, F32) for w in ws]
    outs = pl.pallas_call(
        body, name=name, out_shape=shapes * 3, compiler_params=_params(),
    )(*gs, *ws, *ms, *vs)
    return outs[:n], outs[n:2 * n], outs[2 * n:]


def _pad_rows(a, rows):
    return jnp.pad(a, ((0, rows - a.shape[0]), (0, 0)))


def kernel(x, meta_tokens, norm_w, conv_in_w, conv_w, conv_out_w, mlstm_in_w, mlstm_gate_b, mlstm_head_norm_w, mlstm_out_w, final_norm_w, loss_target, m_meta_tokens, m_norm_w, m_conv_in_w, m_conv_w, m_conv_out_w, m_mlstm_in_w, m_mlstm_gate_b, m_mlstm_head_norm_w, m_mlstm_out_w, m_final_norm_w, v_meta_tokens, v_norm_w, v_conv_in_w, v_conv_w, v_conv_out_w, v_mlstm_in_w, v_mlstm_gate_b, v_mlstm_head_norm_w, v_mlstm_out_w, v_final_norm_w):
    seq, d = x.shape[1], x.shape[2]
    t = seq + LEAD
    e = 2 * d
    ec = e // 2
    nh = N_HEADS
    dv = e // nh
    dk = dv // 2
    qk = nh * dk
    hw = 2 * dk + 3 * dv
    n_in = 2 * qk + 3 * e + 2 * nh
    n_in_s = n_in // N_DEV
    me = 4 * lax.axis_index("x") + 2 * lax.axis_index("y") + lax.axis_index("c")
    tm = _pick(t, 832, 16)
    tm_in = _pick(t, 1664, 16)
    tkw = _pick(t, 2080, 16)

    small = jnp.concatenate([
        meta_tokens,
        _pad_rows(conv_w[0].reshape(3 * (e // N_DEV) // LANES, LANES), 8),
        _pad_rows(mlstm_head_norm_w[0].reshape((e // N_DEV) // LANES, LANES), 8),
    ], axis=0) if d // N_DEV == LANES else None
    assert small is not None, "the packed small-weight block assumes d_model / 8 == 128"
    (small_g,) = _run_exchange(_gather_exchange([small]), "gather_small_weights")
    meta_full = jnp.transpose(small_g[:, 0:N_META, :], (1, 0, 2)).reshape(N_META, d)
    cw_rows = 3 * (e // N_DEV) // LANES
    conv_w_full = jnp.transpose(
        small_g[:, N_META:N_META + cw_rows, :].reshape(N_DEV, 3, e // N_DEV), (1, 0, 2)).reshape(3, e)
    hn_rows = (e // N_DEV) // LANES
    head_w_full = small_g[:, N_META + 8:N_META + 8 + hn_rows, :].reshape(1, e)

    tgt = loss_target[0]

    ci_map = lambda blk: 2 * (blk % 4) + blk // 4
    (h0, u0), (w_ci,) = _rms_first(x[0], meta_full, norm_w[0:1], "rms0",
                                   exchange=_gather_exchange([conv_in_w[0].astype(BF16)]))
    p0, (w_co, w_mi) = _matmul(
        u0, w_ci, form="nn", m=t, n=8 * ec, kdim=d, tm=tm_in, tn=ec, tk=d, out_dtype=BF16, name="conv_in",
        b_spec=pl.BlockSpec((None, d, ec), lambda i, j, k: (ci_map(j), 0, 0)),
        exchange=_gather_exchange([conv_out_w[0].astype(BF16), mlstm_in_w[0].astype(BF16)]))
    w_co = w_co.reshape(e, d)
    (g0, y0), (w_mo,) = _conv_gate_fwd(p0, conv_w_full, "conv_gate",
                                       exchange=_gather_exchange([mlstm_out_w[0].astype(BF16)]))
    w_mo = w_mo.reshape(e, d)
    h1 = _matmul(g0, w_co, form="nn", m=t, n=d, kdim=e, tm=tm, tn=d, tk=e, out_dtype=F32, name="conv_out",
                 residual=h0)

    assert n_in_s == d + 1 and w_mi.shape == (N_DEV, d, d + 1)
    w_p1, w_gate = _mlstm_weight_layout(w_mi, dk, dv, "mlstm_w_layout")
    w_gate_t = jnp.transpose(w_gate[:, 0:16])
    bias_row = jnp.pad(mlstm_gate_b, ((0, 0), (0, LANES - 2 * nh)))
    bias_col = jnp.pad(mlstm_gate_b.T, ((0, 16 - 2 * nh), (0, 0)))

    u1 = _rms_fwd(h1, norm_w[1:2], "rms1")
    p1 = _matmul(u1, w_p1, form="nn", m=t, n=nh * hw, kdim=d, tm=tm_in, tn=_pick(nh * hw, 1024, LANES), tk=d,
                 out_dtype=BF16, name="mlstm_in")
    graw = _matmul(u1, w_gate, form="nn", m=t, n=LANES, kdim=d, tm=tm, tn=LANES, tk=d, out_dtype=F32,
                   name="gates_col")
    graw_t = _matmul(w_gate_t, u1, form="nt", m=16, n=t, kdim=d, tm=16, tn=_pick(t, 1664, LANES), tk=d,
                     out_dtype=F32, name="gates_row")
    colf, rowf = _gates_fwd(graw, graw_t, bias_row, bias_col, "gates_fwd")
    hh, stat, csave, nsave = _mlstm_fwd(p1, colf, rowf, dk, dv, "mlstm_fwd")
    hm = _head_gate_fwd(hh, p1, head_w_full, dv, "head_gate")
    h2 =_matmul(hm, w_mo, form="nn", m=t, n=d, kdim=e, tm=tm, tn=d, tk=e, out_dtype=F32, name="mlstm_out",
                 residual=h1)

    dh2, dwf, loss_part = _final_loss(h2, final_norm_w.reshape(1, d), tgt, "final_loss")

    dhm = _matmul(dh2, w_mo, form="nt", m=t, n=e, kdim=d, tm=tm, tn=_pick(e, 2048, LANES), tk=d, out_dtype=BF16,
                  name="mlstm_out_dx")
    g_mo = _matmul(hm, dh2, form="tn", m=e, n=d, kdim=t, tm=_pick(e, 1024, LANES), tn=d, tk=tkw, out_dtype=BF16,
                   name="mlstm_out_dw")
    n_p1 = nh * hw + LANES
    (dp1, gstat, dhead), (r_mo,) = _mlstm_bwd(
        p1, colf, rowf, head_w_full, hh, stat, csave, nsave, dhm, n_p1, dk, dv, "mlstm_bwd",
        exchange=_scatter_exchange([g_mo.reshape(N_DEV, e // N_DEV, d)]))
    dp1, dbias = _gates_bwd(gstat, colf, dp1, "gates_bwd")
    du1 = _matmul(dp1, w_p1, form="nt", m=t, n=d, kdim=n_p1, tm=tm_in, tn=d, tk=_pick(n_p1, 2048, LANES),
                  out_dtype=F32, name="mlstm_in_dx")
    g_p1 = _matmul(u1, dp1, form="tn", m=d, n=n_p1, kdim=t, tm=d, tn=_pick(n_p1, 2048, LANES), tk=tkw,
                   out_dtype=BF16, name="mlstm_in_dw")
    dh1, dnw1 = _rms_bwd(h1, norm_w[1:2], du1, dh2, "rms1_bwd")

    g_mi_a = _mlstm_grad_layout(g_p1, 0, d // 2, dk, dv, "mlstm_g_layout_a")
    g_mi_b = _mlstm_grad_layout(g_p1, d // 2, d // 2, dk, dv, "mlstm_g_layout_b")

    dg0 =_matmul(dh1, w_co, form="nt", m=t, n=e, kdim=d, tm=tm, tn=_pick(e, 2048, LANES), tk=d, out_dtype=BF16,
                  name="conv_out_dx")
    g_co = _matmul(g0, dh1, form="tn", m=e, n=d, kdim=t, tm=_pick(e, 1024, LANES), tn=d, tk=tkw, out_dtype=BF16,
                   name="conv_out_dw")
    (dp0, dconv), (r_co, r_mi_a) = _conv_gate_bwd(
        p0, y0, dg0, conv_w_full, "conv_gate_bwd",
        exchange=_scatter_exchange([g_co.reshape(N_DEV, e // N_DEV, d), g_mi_a]))
    g_ci, (r_mi_b,) = _matmul(
        u0, dp0, form="tn", m=d, n=8 * ec, kdim=t, tm=d, tn=ec, tk=tkw, out_dtype=BF16, name="conv_in_dw",
        out_shape=(N_DEV, d, ec), out_spec=pl.BlockSpec((None, d, ec), lambda i, j, k: (ci_map(j), 0, 0)),
        exchange=_scatter_exchange([g_mi_b]))
    du0, (r_ci,) = _matmul(
        dp0, w_ci, form="nt", m=t, n=d, kdim=8 * ec, tm=tm_in, tn=d, tk=ec, out_dtype=F32, name="conv_in_dx",
        b_spec=pl.BlockSpec((None, d, ec), lambda i, j, k: (ci_map(k), 0, 0)),
        exchange=_scatter_exchange([g_ci]))
    grad_x, dmeta, dnw0 = _rms_bwd_first(h0, norm_w[0:1], du0, dh1, "rms0_bwd")
    grad_x = grad_x[None]

    row8 = lax.broadcasted_iota(jnp.int32, (8, 1), 0)
    loss_wide = jnp.pad(loss_part, ((0, 0), (0, d - LANES)))
    payload = jnp.concatenate([
        jnp.where(row8 == 0, dnw0, jnp.where(row8 == 1, dnw1, 0.0)),
        jnp.where(row8 == 0, dwf, jnp.where(row8 == 1, loss_wide, 0.0)),
        jnp.where(row8 == 0, jnp.pad(dbias, ((0, 0), (0, d - LANES))), 0.0),
        dmeta,
        _pad_rows(dconv[0:3].reshape(3 * e // d, d), 8),
        _pad_rows(dhead[:, 0, :].reshape(e // d, d), 8),
    ], axis=0)
    (payload_g,) = _run_exchange(_gather_exchange([payload]), "gather_small_grads")
    tot = _sum_devices(payload_g, "sum_small_grads")

    loss = tot[9, 0]
    g_norm = tot[0:2]
    g_final = tot[8]
    g_gate_b = tot[16:17, 0:2 * nh]
    g_meta = lax.dynamic_slice(tot[24:24 + N_META], (0, me * (d // N_DEV)), (N_META, d // N_DEV))
    g_conv_w = lax.dynamic_slice(tot[40:40 + 3 * e // d].reshape(3, e), (0, me * (e // N_DEV)), (3, e // N_DEV))
    g_head = lax.dynamic_slice(tot[48:48 + e // d].reshape(1, e), (0, me * (e // N_DEV)), (1, e // N_DEV))

    g1, d1, nm1, nv1 = _adamw_sharded([r_ci], conv_in_w[0], m_conv_in_w[0], v_conv_in_w[0], "adamw_conv_in")
    g2, d2, nm2, nv2 = _adamw_sharded([r_co], conv_out_w[0], m_conv_out_w[0], v_conv_out_w[0], "adamw_conv_out")
    g3, d3, nm3, nv3 = _adamw_sharded([r_mi_a, r_mi_b], mlstm_in_w[0], m_mlstm_in_w[0], v_mlstm_in_w[0],
                                      "adamw_mlstm_in")
    g4, d4, nm4, nv4 = _adamw_sharded([r_mo], mlstm_out_w[0], m_mlstm_out_w[0], v_mlstm_out_w[0],
                                      "adamw_mlstm_out")

    small_g = [g_meta, g_norm, g_conv_w, g_gate_b, g_head, g_final.reshape(1, d)]
    small_w = [meta_tokens, norm_w, conv_w[0], mlstm_gate_b, mlstm_head_norm_w, final_norm_w.reshape(1, d)]
    small_m = [m_meta_tokens, m_norm_w, m_conv_w[0], m_mlstm_gate_b, m_mlstm_head_norm_w, m_final_norm_w.reshape(1, d)]
    small_v = [v_meta_tokens, v_norm_w, v_conv_w[0], v_mlstm_gate_b, v_mlstm_head_norm_w, v_final_norm_w.reshape(1, d)]
    sd, snm, snv = _adamw_small(small_g, small_w, small_m, small_v, "adamw_small")

    def order(meta, norm, cin, cw, cout, min_, gb, hn, mout, fin):
        return (meta, norm, cin[None], cw[None], cout[None], min_[None], gb, hn, mout[None], fin.reshape(d))

    grads = order(g_meta, g_norm, g1, g_conv_w, g2, g3, g_gate_b, g_head, g4, g_final)
    deltas = order(sd[0], sd[1], d1, sd[2], d2, d3, sd[3], sd[4], d4, sd[5])
    new_m = order(snm[0], snm[1], nm1, snm[2], nm2, nm3, snm[3], snm[4], nm4, snm[5])
    new_v = order(snv[0], snv[1], nv1, snv[2], nv2, nv3, snv[3], snv[4], nv4, snv[5])
    return (loss, grad_x, *grads, *deltas, *new_m, *new_v)
```

```python
import functools

import jax
import jax.numpy as jnp
from jax import lax
from jax.experimental import pallas as pl
from jax.experimental.pallas import tpu as pltpu

F32 = jnp.float32
BF16 = jnp.bfloat16
MESH = pl.DeviceIdType.MESH

N_DEV = 8
N_META = 16
N_HEADS = 4
CHUNK = 64
CHUNKS_PER_STEP = 2
LEAD = 128
PAD_ROWS = LEAD - N_META
RMS_EPS = 1e-6
NEG = -1e30
LANES = 128
VMEM_LIMIT = 48 * 1024 * 1024

ADAM_LR = 0.001
ADAM_B1 = 0.9
ADAM_B2 = 0.999
ADAM_EPS = 1e-08
ADAM_WD = 0.01
ADAM_STEP = 10

HIGHEST = lax.Precision.HIGHEST


def _pick(n, target, mult):
    best = None
    for d in range(mult, min(n, target) + 1, mult):
        if n % d == 0:
            best = d
    return n if best is None else best


def _params(**kw):
    return pltpu.CompilerParams(vmem_limit_bytes=VMEM_LIMIT, **kw)


def _sigmoid(x):
    return 0.5 * jnp.tanh(0.5 * x) + 0.5


STRIP_ROWS = 16
STRIP_COLS = 256


def _strips(n_rows, n_cols):
    for c0 in range(0, n_cols, STRIP_COLS):
        for r0 in range(0, n_rows, STRIP_ROWS):
            yield slice(r0, r0 + STRIP_ROWS), slice(c0, min(c0 + STRIP_COLS, n_cols))


def _shift(sl, by):
    return slice(sl.start + by, sl.stop + by)


def _call(body, *, name, grid, in_specs, out_specs, out_shape, args, scratch_shapes=(), aliases=None,
          exchange=None):
    aliases = {} if aliases is None else aliases
    if exchange is None:
        return pl.pallas_call(
            body, name=name, grid=grid, in_specs=list(in_specs), out_specs=out_specs, out_shape=out_shape,
            scratch_shapes=list(scratch_shapes), input_output_aliases=aliases,
            compiler_params=_params())(*args)
    ex = exchange
    single = not isinstance(out_shape, (list, tuple))
    shapes = [out_shape] if single else list(out_shape)
    specs = [out_specs] if single else list(out_specs)
    n_in, n_out, n_scr = len(in_specs), len(shapes), len(scratch_shapes)
    n_ex_in, n_ex_out = len(ex.ins), len(ex.out_shapes)
    steps = 1
    for size in grid:
        steps *= size

    def wrapped(*refs):
        own_in, ex_in = refs[:n_in], refs[n_in:n_in + n_ex_in]
        at = n_in + n_ex_in
        own_out, ex_out = refs[at:at + n_out], refs[at + n_out:at + n_out + n_ex_out]
        at += n_out + n_ex_out
        own_scr, ex_scr = refs[at:at + n_scr], refs[at + n_scr:]
        step = 0
        for axis, size in enumerate(grid):
            step = step * size + pl.program_id(axis)

        @pl.when(step == 0)
        def _():
            ex.start(ex_in, ex_out, ex_scr)

        if ex.mid is not None:
            @pl.when(step == (3 * steps) // 4)
            def _():
                ex.mid(ex_in, ex_out, ex_scr)

        body(*own_in, *own_out, *own_scr)

        @pl.when(step == steps - 1)
        def _():
            ex.finish(ex_in, ex_out, ex_scr)

    any_spec = pl.BlockSpec(memory_space=pl.ANY)
    res = pl.pallas_call(
        wrapped, name=name, grid=grid,
        in_specs=list(in_specs) + [any_spec] * n_ex_in, out_specs=specs + [any_spec] * n_ex_out,
        out_shape=shapes + ex.out_shapes, scratch_shapes=list(scratch_shapes) + ex.scratch,
        input_output_aliases=aliases, compiler_params=_params())(*args, *ex.ins)
    return (res[0] if single else res[:n_out]), res[n_out:]


def _matmul(a, b, *, form, m, n, kdim, tm, tn, tk, out_dtype, name,
            a_spec=None, b_spec=None, out_spec=None, out_shape=None, residual=None, exchange=None):
    ni, nj, nk = m // tm, n // tn, kdim // tk
    assert ni * tm == m and nj * tn == n and nk * tk == kdim, (name, m, n, kdim, tm, tn, tk)
    if form == "nn":
        dn = (((1,), (0,)), ((), ()))
        a_def = pl.BlockSpec((tm, tk), lambda i, j, k: (i, k))
        b_def = pl.BlockSpec((tk, tn), lambda i, j, k: (k, j))
    elif form == "nt":
        dn = (((1,), (1,)), ((), ()))
        a_def = pl.BlockSpec((tm, tk), lambda i, j, k: (i, k))
        b_def = pl.BlockSpec((tn, tk), lambda i, j, k: (j, k))
    else:
        dn = (((0,), (0,)), ((), ()))
        a_def = pl.BlockSpec((tk, tm), lambda i, j, k: (k, i))
        b_def = pl.BlockSpec((tk, tn), lambda i, j, k: (k, j))
    a_spec = a_def if a_spec is None else a_spec
    b_spec = b_def if b_spec is None else b_spec
    o_spec = pl.BlockSpec((tm, tn), lambda i, j, k: (i, j)) if out_spec is None else out_spec
    has_res = residual is not None

    def body(*refs):
        a_ref, b_ref = refs[:2]
        r_ref = refs[2] if has_res else None
        o_ref = refs[2 + has_res]

        def product():
            return lax.dot_general(a_ref[...].astype(BF16), b_ref[...].astype(BF16), dn,
                                   preferred_element_type=F32)

        def finish(acc):
            if has_res:
                acc = acc + r_ref[...].astype(F32)
            o_ref[...] = acc.astype(o_ref.dtype)

        if nk == 1:
            finish(product())
        else:
            acc_ref = refs[3 + has_res]
            k = pl.program_id(2)

            @pl.when(k == 0)
            def _():
                acc_ref[...] = jnp.zeros_like(acc_ref)

            acc_ref[...] += product()

            @pl.when(k == nk - 1)
            def _():
                finish(acc_ref[...])

    in_specs = [a_spec, b_spec]
    args = [a, b]
    if has_res:
        in_specs.append(pl.BlockSpec((tm, tn), lambda i, j, k: (i, j)))
        args.append(residual)
    return _call(
        body, name=name, grid=(ni, nj, nk), in_specs=in_specs, out_specs=o_spec,
        out_shape=jax.ShapeDtypeStruct((m, n) if out_shape is None else out_shape, out_dtype),
        scratch_shapes=[] if nk == 1 else [pltpu.VMEM((tm, tn), F32)], args=args, exchange=exchange)


def _rms_fwd(h, w, name, exchange=None):
    t, d = h.shape
    tm = _pick(t, 832, 16)

    def body(h_ref, w_ref, u_ref):
        x = h_ref[...]
        r = lax.rsqrt(jnp.mean(x * x, axis=-1, keepdims=True) + RMS_EPS)
        u_ref[...] = ((x * r) * w_ref[...]).astype(BF16)

    return _call(
        body, name=name, grid=(t // tm,),
        in_specs=[pl.BlockSpec((tm, d), lambda i: (i, 0)), pl.BlockSpec((1, d), lambda i: (0, 0))],
        out_specs=pl.BlockSpec((tm, d), lambda i: (i, 0)),
        out_shape=jax.ShapeDtypeStruct((t, d), BF16), args=(h, w), exchange=exchange)


def _rms_first(x, meta, w, name, exchange=None):
    seq, d = x.shape
    t = seq + LEAD
    tm = LEAD

    def body(x_ref, meta_ref, w_ref, h_ref, u_ref):
        i = pl.program_id(0)

        @pl.when(i == 0)
        def _():
            h_ref[0:PAD_ROWS, :] = jnp.zeros((PAD_ROWS, d), F32)
            h_ref[PAD_ROWS:LEAD, :] = meta_ref[...]

        @pl.when(i > 0)
        def _():
            h_ref[...] = x_ref[...]

        hv = h_ref[...]
        r = lax.rsqrt(jnp.mean(hv * hv, axis=-1, keepdims=True) + RMS_EPS)
        u_ref[...] = ((hv * r) * w_ref[...]).astype(BF16)

    row = pl.BlockSpec((tm, d), lambda i: (i, 0))
    return _call(
        body, name=name, grid=(t // tm,),
        in_specs=[pl.BlockSpec((tm, d), lambda i: (jnp.maximum(i - 1, 0), 0)),
                  pl.BlockSpec((N_META, d), lambda i: (0, 0)), pl.BlockSpec((1, d), lambda i: (0, 0))],
        out_specs=[row, row],
        out_shape=[jax.ShapeDtypeStruct((t, d), F32), jax.ShapeDtypeStruct((t, d), BF16)],
        args=(x, meta, w), exchange=exchange)


def _rms_bwd(h, w, du, dres, name):
    t, d = h.shape
    tm = _pick(t, 832, 16)

    def body(h_ref, w_ref, du_ref, dres_ref, dh_ref, dw_ref):
        i = pl.program_id(0)
        x = h_ref[...]
        g = du_ref[...].astype(F32)
        r = lax.rsqrt(jnp.mean(x * x, axis=-1, keepdims=True) + RMS_EPS)
        gw = g * w_ref[...]
        dot = jnp.mean(gw * x, axis=-1, keepdims=True)
        dh_ref[...] = dres_ref[...] + (r * gw - x * ((r * r * r) * dot))
        part = jnp.sum(g * (x * r), axis=0, keepdims=True)

        @pl.when(i == 0)
        def _():
            dw_ref[...] = jnp.zeros_like(dw_ref)

        dw_ref[...] += jnp.broadcast_to(part, dw_ref.shape)

    row = pl.BlockSpec((tm, d), lambda i: (i, 0))
    return pl.pallas_call(
        body, name=name, grid=(t // tm,),
        in_specs=[row, pl.BlockSpec((1, d), lambda i: (0, 0)), row, row],
        out_specs=[row, pl.BlockSpec((8, d), lambda i: (0, 0))],
        out_shape=[jax.ShapeDtypeStruct((t, d), F32), jax.ShapeDtypeStruct((8, d), F32)],
        compiler_params=_params(),
    )(h, w, du, dres)


def _rms_bwd_first(h, w, du, dres, name):
    t, d = h.shape
    sub = LEAD
    per = _pick((t - LEAD) // sub, 8, 1)
    tmo = per * sub
    n_out = (t - LEAD) // tmo
    n_blocks = t // sub

    def body(*refs):
        w_ref = refs[0]
        h_refs, du_refs, dres_refs = refs[1:1 + per], refs[1 + per:1 + 2 * per], refs[1 + 2 * per:1 + 3 * per]
        gx_ref, dmeta_ref, dw_ref = refs[1 + 3 * per:]
        i = pl.program_id(0)

        def piece(q):
            x = h_refs[q][...]
            g = du_refs[q][...].astype(F32)
            r = lax.rsqrt(jnp.mean(x * x, axis=-1, keepdims=True) + RMS_EPS)
            gw = g * w_ref[...]
            dot = jnp.mean(gw * x, axis=-1, keepdims=True)
            dh = dres_refs[q][...] + (r * gw - x * ((r * r * r) * dot))
            return dh, jnp.sum(g * (x * r), axis=0, keepdims=True)

        @pl.when(i == 0)
        def _():
            dh, part = piece(0)
            dmeta_ref[...] = dh[PAD_ROWS:LEAD, :]
            dw_ref[...] = jnp.broadcast_to(part, dw_ref.shape)

        @pl.when(i > 0)
        def _():
            for q in range(per):
                dh, part = piece(q)
                gx_ref[q * sub:(q + 1) * sub, :] = dh
                dw_ref[...] += jnp.broadcast_to(part, dw_ref.shape)

    def piece_spec(q):
        return pl.BlockSpec((sub, d), lambda i: (jnp.clip(1 + per * (i - 1) + q, 0, n_blocks - 1), 0))

    pieces = [piece_spec(q) for q in range(per)]
    return pl.pallas_call(
        body, name=name, grid=(n_out + 1,),
        in_specs=[pl.BlockSpec((1, d), lambda i: (0, 0))] + pieces * 3,
        out_specs=[pl.BlockSpec((tmo, d), lambda i: (jnp.maximum(i - 1, 0), 0)),
                   pl.BlockSpec((N_META, d), lambda i: (0, 0)), pl.BlockSpec((8, d), lambda i: (0, 0))],
        out_shape=[jax.ShapeDtypeStruct((t - LEAD, d), F32), jax.ShapeDtypeStruct((N_META, d), F32),
                   jax.ShapeDtypeStruct((8, d), F32)],
        compiler_params=_params(),
    )(w, *([h] * per), *([du] * per), *([dres] * per))


def _final_loss(h, w, tgt, name):
    t, d = h.shape
    sub = LEAD
    per = _pick(t // sub, 5, 1)
    tm = per * sub

    def body(h_ref, w_ref, *rest):
        t_refs, (dh_ref, dw_ref, loss_ref) = rest[:per], rest[per:]
        i = pl.program_id(0)

        @pl.when(i == 0)
        def _():
            dw_ref[...] = jnp.zeros_like(dw_ref)
            loss_ref[...] = jnp.zeros_like(loss_ref)

        for q in range(per):
            sl = slice(q * sub, (q + 1) * sub)
            x = h_ref[sl, :]
            r = lax.rsqrt(jnp.mean(x * x, axis=-1, keepdims=True) + RMS_EPS)
            yn = x * r
            y = yn * w_ref[...]
            rows = i * tm + q * sub + lax.broadcasted_iota(jnp.int32, (sub, 1), 0)
            diff = jnp.where(rows >= LEAD, y - t_refs[q][...], 0.0)
            tile_loss = 0.5 * jnp.sum(jnp.mean(diff * diff, axis=-1, keepdims=True), axis=0, keepdims=True)
            dy = diff / d
            gw = dy * w_ref[...]
            dot = jnp.mean(gw * x, axis=-1, keepdims=True)
            dh_ref[sl, :] = r * gw - x * ((r * r * r) * dot)
            dw_ref[...] += jnp.broadcast_to(jnp.sum(dy * yn, axis=0, keepdims=True), dw_ref.shape)
            loss_ref[...] += jnp.broadcast_to(tile_loss, loss_ref.shape)

    row = pl.BlockSpec((tm, d), lambda i: (i, 0))
    piece = [pl.BlockSpec((sub, d), functools.partial(lambda i, q: (jnp.maximum(i * per + q - 1, 0), 0), q=q))
             for q in range(per)]
    return pl.pallas_call(
        body, name=name, grid=(t // tm,),
        in_specs=[row, pl.BlockSpec((1, d), lambda i: (0, 0))] + piece,
        out_specs=[row, pl.BlockSpec((8, d), lambda i: (0, 0)), pl.BlockSpec((8, LANES), lambda i: (0, 0))],
        out_shape=[jax.ShapeDtypeStruct((t, d), F32), jax.ShapeDtypeStruct((8, d), F32),
                   jax.ShapeDtypeStruct((8, LANES), F32)],
        compiler_params=_params(),
    )(h, w, *([tgt] * per))


def _conv_gate_fwd(p0, conv_w, name, exchange=None):
    t = p0.shape[0]
    ec = p0.shape[1] // 8
    tm = _pick(t, 832, 16)
    nt = t // tm

    def body(p_ref, w_ref, g_ref, y_ref, ext_ref):
        i = pl.program_id(1)

        @pl.when(i == 0)
        def _():
            ext_ref[0:8, :] = jnp.zeros((8, ec), F32)

        for rows, cols in _strips(tm, ec):
            cg = p_ref[rows, _shift(cols, ec)].astype(F32)
            xin = p_ref[rows, _shift(cols, 2 * ec)].astype(F32)
            ext_ref[_shift(rows, 8), cols] = cg * xin
        for rows, cols in _strips(tm, ec):
            y = (w_ref[0:1, cols] * ext_ref[_shift(rows, 6), cols] + w_ref[1:2, cols] * ext_ref[_shift(rows, 7), cols]
                 + w_ref[2:3, cols] * ext_ref[_shift(rows, 8), cols])
            bg = p_ref[rows, cols].astype(F32)
            z = p_ref[rows, _shift(cols, 3 * ec)].astype(F32)
            y_ref[rows, cols] = y.astype(BF16)
            g_ref[rows, cols] = ((z * _sigmoid(z)) * bg * y).astype(BF16)
        ext_ref[0:8, :] = ext_ref[tm:tm + 8, :]

    out = pl.BlockSpec((tm, ec), lambda j, i: (i, j))
    return _call(
        body, name=name, grid=(2, nt),
        in_specs=[pl.BlockSpec((tm, 4 * ec), lambda j, i: (i, j)), pl.BlockSpec((3, ec), lambda j, i: (0, j))],
        out_specs=[out, out],
        out_shape=[jax.ShapeDtypeStruct((t, 2 * ec), BF16)] * 2,
        scratch_shapes=[pltpu.VMEM((tm + 8, ec), F32)], args=(p0, conv_w), exchange=exchange)


def _conv_gate_bwd(p0, y, dg, conv_w, name, exchange=None):
    t = p0.shape[0]
    ec = p0.shape[1] // 8
    tm = _pick(t, 416, 16)
    nt = t // tm

    def body(p_ref, y_ref, dg_ref, w_ref, dp_ref, dw_ref, ext_ref):
        i = pl.program_id(1)

        @pl.when(i == 0)
        def _():
            ext_ref[tm:tm + 8, :] = jnp.zeros((8, ec), F32)
            dw_ref[...] = jnp.zeros_like(dw_ref)

        for rows, cols in _strips(tm, ec):
            bg = p_ref[rows, cols].astype(F32)
            z = p_ref[rows, _shift(cols, 3 * ec)].astype(F32)
            yv = y_ref[rows, cols].astype(F32)
            dgv = dg_ref[rows, cols].astype(F32)
            sig = _sigmoid(z)
            sz = z * sig
            dp_ref[rows, _shift(cols, 3 * ec)] = (dgv * bg * yv * (sig * (1.0 + z * (1.0 - sig)))).astype(BF16)
            dp_ref[rows, cols] = (dgv * sz * yv).astype(BF16)
            ext_ref[rows, cols] = dgv * sz * bg
        acc = None
        for rows, cols in _strips(tm, ec):
            if rows.start == 0:
                acc = [jnp.zeros((STRIP_ROWS, cols.stop - cols.start), F32) for _ in range(3)]
            dy = ext_ref[rows, cols]
            dy1 = ext_ref[_shift(rows, 1), cols]
            dy2 = ext_ref[_shift(rows, 2), cols]
            cg = p_ref[rows, _shift(cols, ec)].astype(F32)
            xin = p_ref[rows, _shift(cols, 2 * ec)].astype(F32)
            da = w_ref[0:1, cols] * dy2 + w_ref[1:2, cols] * dy1 + w_ref[2:3, cols] * dy
            dp_ref[rows, _shift(cols, ec)] = (da * xin).astype(BF16)
            dp_ref[rows, _shift(cols, 2 * ec)] = (da * cg).astype(BF16)
            a = cg * xin
            acc = [acc[0] + a * dy2, acc[1] + a * dy1, acc[2] + a * dy]
            if rows.stop == tm:
                for tap in range(3):
                    dw_ref[tap:tap + 1, cols] += jnp.sum(acc[tap], axis=0, keepdims=True)
        ext_ref[tm:tm + 8, :] = ext_ref[0:8, :]

    rev = lambda j, i: (nt - 1 - i, j)
    return _call(
        body, name=name, grid=(2, nt),
        in_specs=[pl.BlockSpec((tm, 4 * ec), rev), pl.BlockSpec((tm, ec), rev), pl.BlockSpec((tm, ec), rev),
                  pl.BlockSpec((3, ec), lambda j, i: (0, j))],
        out_specs=[pl.BlockSpec((tm, 4 * ec), rev), pl.BlockSpec((8, ec), lambda j, i: (0, j))],
        out_shape=[jax.ShapeDtypeStruct((t, 8 * ec), BF16), jax.ShapeDtypeStruct((8, 2 * ec), F32)],
        scratch_shapes=[pltpu.VMEM((tm + 8, ec), F32)], args=(p0, y, dg, conv_w), exchange=exchange)


def _log_sigmoid(x):
    return jnp.minimum(x, 0.0) - jnp.log(1.0 + jnp.exp(-jnp.abs(x)))


def _gates_fwd(graw, graw_t, bias_row, bias_col, name):
    t = graw.shape[0]
    tm = _pick(t, 640, 128)
    cpt = tm // CHUNK
    nh = N_HEADS

    def body(g_ref, gt_ref, br_ref, bc_ref, colf_ref, rowf_ref):
        i = pl.program_id(0)
        gc = g_ref[...] + br_ref[...]
        rows = i * tm + lax.broadcasted_iota(jnp.int32, (tm, 1), 0)
        live = rows >= PAD_ROWS
        lf = jnp.where(live, _log_sigmoid(gc), 0.0)
        li = jnp.where(live, gc, NEG)
        gt = gt_ref[...] + bc_ref[...]
        cols = i * tm + lax.broadcasted_iota(jnp.int32, (1, tm), 1)
        live_t = cols >= PAD_ROWS
        lf_t = jnp.where(live_t, _log_sigmoid(gt), 0.0)
        li_t = jnp.where(live_t, gt, NEG)
        ri = lax.broadcasted_iota(jnp.int32, (CHUNK, CHUNK), 0)
        ci = lax.broadcasted_iota(jnp.int32, (CHUNK, CHUNK), 1)
        lower = (ri >= ci).astype(F32)
        upper = (ri <= ci).astype(F32)
        lane = lax.broadcasted_iota(jnp.int32, (CHUNK, LANES), 1)
        sub = lax.broadcasted_iota(jnp.int32, (8, CHUNK), 0)
        for c in range(cpt):
            sl = slice(c * CHUNK, (c + 1) * CHUNK)
            b_all = jnp.dot(lower, lf[sl, :], precision=HIGHEST, preferred_element_type=F32)
            bt_all = jnp.dot(lf_t[:, sl], upper, precision=HIGHEST, preferred_element_type=F32)
            for h in range(nh):
                b = b_all[:, nh + h:nh + h + 1]
                r = li[sl, h:h + 1] - b
                pre = gc[sl, nh + h:nh + h + 1]
                colf_ref[h, sl, :] = jnp.where(lane == 0, b, jnp.where(lane == 1, r, jnp.where(lane == 2, pre, 0.0)))
                b_row = bt_all[nh + h:nh + h + 1, :]
                r_row = li_t[h:h + 1, sl] - b_row
                rowf_ref[h, c, :, 0:CHUNK] = jnp.where(sub == 0, r_row, jnp.where(sub == 1, b_row, 0.0))
                rowf_ref[h, c, :, CHUNK:LANES] = jnp.zeros((8, LANES - CHUNK), F32)

    return pl.pallas_call(
        body, name=name, grid=(t // tm,),
        in_specs=[pl.BlockSpec((tm, LANES), lambda i: (i, 0)), pl.BlockSpec((16, tm), lambda i: (0, i)),
                  pl.BlockSpec((1, LANES), lambda i: (0, 0)), pl.BlockSpec((16, 1), lambda i: (0, 0))],
        out_specs=[pl.BlockSpec((nh, tm, LANES), lambda i: (0, i, 0)),
                   pl.BlockSpec((nh, cpt, 8, LANES), lambda i: (0, i, 0, 0))],
        out_shape=[jax.ShapeDtypeStruct((nh, t, LANES), F32),
                   jax.ShapeDtypeStruct((nh, t // CHUNK, 8, LANES), F32)],
        compiler_params=_params(),
    )(graw, graw_t, bias_row, bias_col)


def _gates_bwd(gstat, colf, dp1, name):
    nh, t, _ = gstat.shape
    tm = _pick(t, 640, 128)
    cpt = tm // CHUNK
    nt = t // tm
    nc = t // CHUNK
    gate_block = dp1.shape[1] // LANES - 1

    def body(gs_ref, nx_ref, colf_ref, dp_any, dg_ref, db_ref):
        i = pl.program_id(0)
        ri = lax.broadcasted_iota(jnp.int32, (CHUNK, CHUNK), 0)
        ci = lax.broadcasted_iota(jnp.int32, (CHUNK, CHUNK), 1)
        upper = (ri <= ci).astype(F32)
        lane = lax.broadcasted_iota(jnp.int32, (CHUNK, LANES), 1)
        total = jnp.zeros((1, LANES), F32)
        for c in range(cpt):
            sl = slice(c * CHUNK, (c + 1) * CHUNK)
            rows = i * tm + c * CHUNK + lax.broadcasted_iota(jnp.int32, (CHUNK, 1), 0)
            live = rows >= PAD_ROWS
            acc = jnp.zeros((CHUNK, LANES), F32)
            for h in range(nh):
                blk = gs_ref[h, sl, :]
                rev = jnp.dot(upper, blk, precision=HIGHEST, preferred_element_type=F32)
                if c + 1 < cpt:
                    carry = gs_ref[h, (c + 1) * CHUNK:(c + 1) * CHUNK + 1, 2:3]
                else:
                    carry = jnp.where(i == nt - 1, 0.0, nx_ref[h, 0:1, 2:3])
                dlogf = rev[:, 0:1] + carry
                pre = colf_ref[h, sl, 2:3]
                dgf = jnp.where(live, dlogf * (1.0 - _sigmoid(pre)), 0.0)
                dgi = jnp.where(live, blk[:, 1:2], 0.0)
                acc = acc + jnp.where(lane == h, dgi, 0.0) + jnp.where(lane == nh + h, dgf, 0.0)
            dg_ref[sl, :] = acc.astype(BF16)
            total = total + jnp.sum(acc, axis=0, keepdims=True)

        @pl.when(i == 0)
        def _():
            db_ref[...] = jnp.zeros_like(db_ref)

        db_ref[...] += jnp.broadcast_to(total, db_ref.shape)

    return pl.pallas_call(
        body, name=name, grid=(nt,),
        in_specs=[pl.BlockSpec((nh, tm, LANES), lambda i: (0, i, 0)),
                  pl.BlockSpec((nh, CHUNK, LANES), lambda i: (0, jnp.minimum((i + 1) * cpt, nc - 1), 0)),
                  pl.BlockSpec((nh, tm, LANES), lambda i: (0, i, 0)),
                  pl.BlockSpec(memory_space=pl.ANY)],
        out_specs=[pl.BlockSpec((tm, LANES), lambda i: (i, gate_block)), pl.BlockSpec((8, LANES), lambda i: (0, 0))],
        out_shape=[jax.ShapeDtypeStruct(dp1.shape, dp1.dtype), jax.ShapeDtypeStruct((8, LANES), F32)],
        input_output_aliases={3: 0},
        compiler_params=_params(),
    )(gstat, gstat, colf, dp1)


NT_DIMS = (((1,), (1,)), ((), ()))
TN_DIMS = (((0,), (0,)), ((), ()))


def _dot(a, b):
    return jnp.dot(a.astype(BF16), b.astype(BF16), preferred_element_type=F32)


def _dot_nt(a, b):
    return lax.dot_general(a.astype(BF16), b.astype(BF16), NT_DIMS, preferred_element_type=F32)


def _dot_tn(a, b):
    return lax.dot_general(a.astype(BF16), b.astype(BF16), TN_DIMS, preferred_element_type=F32)


def _chunk_gates(colf_ref, rowf_ref, m_prev, width=CHUNK):
    b = colf_ref[:, 0:1]
    rcol = colf_ref[:, 1:2]
    rrow = rowf_ref[0:1, 0:width]
    ri = lax.broadcasted_iota(jnp.int32, (CHUNK, width), 0)
    ci = lax.broadcasted_iota(jnp.int32, (CHUNK, width), 1)
    log_d = jnp.where(ri >= ci, b + rrow, NEG)
    m_row = jnp.maximum(b + m_prev, jnp.max(log_d, axis=-1, keepdims=True))
    dmat = jnp.exp(log_d - m_row)
    inter = jnp.exp(b + m_prev - m_row)
    b_last = b[CHUNK - 1:CHUNK, :]
    log_w = rcol + b_last
    m_new = jnp.maximum(b_last + m_prev, jnp.max(log_w, axis=0, keepdims=True))
    decay = jnp.exp(b_last + m_prev - m_new)
    w = jnp.exp(log_w - m_new)
    return m_row, dmat, inter, m_new, decay, w


def _mlstm_fwd(p1, colf, rowf, dk, dv, name):
    t = p1.shape[0]
    nh = N_HEADS
    nc = t // CHUNK
    hw = 2 * dk + dv
    cps = CHUNKS_PER_STEP if nc % CHUNKS_PER_STEP == 0 else 1
    rows_per_step = cps * CHUNK

    def body(p_ref, colf_ref, rowf_ref, hh_ref, stat_ref, cs_ref, ns_ref, *state):
        c_refs, n_refs, m_refs = state[:nh], state[nh:2 * nh], state[2 * nh:]
        c = pl.program_id(0)

        @pl.when(c == 0)
        def _():
            for ref in state:
                ref[...] = jnp.zeros_like(ref)

        for cc in range(cps):
            rows = pl.ds(cc * CHUNK, CHUNK)
            for h in range(nh):
                head(p_ref.at[rows, pl.ds(h * hw, hw)], colf_ref.at[h, rows], rowf_ref.at[h, cc],
                     hh_ref.at[rows, pl.ds(h * dv, dv)], stat_ref.at[h, rows], cs_ref.at[h, cc],
                     ns_ref.at[h, cc], c_refs[h], n_refs[h], m_refs[h])

    def head(p_ref, colf_ref, rowf_ref, hh_ref, stat_ref, cs_ref, ns_ref, c_ref, n_ref, m_ref):
        m_prev = m_ref[...]
        n_prev = n_ref[...]
        c_prev = c_ref[...]
        cs_ref[...] = c_prev.astype(BF16)
        sub = lax.broadcasted_iota(jnp.int32, (8, dk), 0)
        ns_ref[...] = jnp.where(sub == 0, n_prev, jnp.where(sub == 1, m_prev, 0.0))

        q = p_ref[:, 0:dk]
        k = p_ref[:, dk:2 * dk]
        v = p_ref[:, 2 * dk:2 * dk + dv]
        m_row, dmat, inter, m_new, decay, w = _chunk_gates(colf_ref, rowf_ref, m_prev)
        s = _dot_nt(q, k) * dmat
        num = _dot(s, v) + inter * _dot(q, c_prev)
        den = jnp.sum(s, axis=-1, keepdims=True) + inter * jnp.sum(q.astype(F32) * n_prev, axis=-1, keepdims=True)
        denom = jnp.maximum(jnp.abs(den), jnp.exp(-m_row))
        hh_ref[...] = num * (1.0 / denom)
        lane = lax.broadcasted_iota(jnp.int32, (CHUNK, LANES), 1)
        stat_ref[...] = jnp.where(lane == 0, den, 0.0)

        wk = w * k.astype(F32)
        c_ref[...] = decay * c_prev + _dot_tn(wk, v)
        n_ref[...] = decay * n_prev + jnp.sum(wk, axis=0, keepdims=True)
        m_ref[...] = m_new

    return pl.pallas_call(
        body, name=name, grid=(nc // cps,),
        in_specs=[pl.BlockSpec((rows_per_step, nh * hw), lambda c: (c, 0)),
                  pl.BlockSpec((nh, rows_per_step, LANES), lambda c: (0, c, 0)),
                  pl.BlockSpec((nh, cps, 8, LANES), lambda c: (0, c, 0, 0))],
        out_specs=[pl.BlockSpec((rows_per_step, nh * dv), lambda c: (c, 0)),
                   pl.BlockSpec((nh, rows_per_step, LANES), lambda c: (0, c, 0)),
                   pl.BlockSpec((nh, cps, dk, dv), lambda c: (0, c, 0, 0)),
                   pl.BlockSpec((nh, cps, 8, dk), lambda c: (0, c, 0, 0))],
        out_shape=[jax.ShapeDtypeStruct((t, nh * dv), F32), jax.ShapeDtypeStruct((nh, t, LANES), F32),
                   jax.ShapeDtypeStruct((nh, nc, dk, dv), BF16), jax.ShapeDtypeStruct((nh, nc, 8, dk), F32)],
        scratch_shapes=([pltpu.VMEM((dk, dv), F32)] * nh + [pltpu.VMEM((1, dk), F32)] * nh
                        + [pltpu.VMEM((1, 1), F32)] * nh),
        compiler_params=_params(),
    )(p1, colf, rowf)


def _head_gate_fwd(hh, p1, head_w, dv, name):
    t = hh.shape[0]
    nh = N_HEADS
    e = nh * dv
    tm = _pick(t, 416, 16)

    def body(hh_ref, oz_ref, w_ref, hm_ref):
        for h in range(nh):
            cols = slice(h * dv, (h + 1) * dv)
            x = hh_ref[:, cols]
            o = oz_ref[:, 2 * h * dv:(2 * h + 1) * dv].astype(F32)
            z = oz_ref[:, (2 * h + 1) * dv:(2 * h + 2) * dv].astype(F32)
            r = lax.rsqrt(jnp.mean(x * x, axis=-1, keepdims=True) + RMS_EPS)
            hn = (x * r) * w_ref[:, cols]
            hm_ref[:, cols] = (hn * _sigmoid(o) * (z * _sigmoid(z))).astype(BF16)

    return pl.pallas_call(
        body, name=name, grid=(t // tm,),
        in_specs=[pl.BlockSpec((tm, e), lambda i: (i, 0)), pl.BlockSpec((tm, 2 * e), lambda i: (i, 1)),
                  pl.BlockSpec((1, e), lambda i: (0, 0))],
        out_specs=pl.BlockSpec((tm, e), lambda i: (i, 0)),
        out_shape=jax.ShapeDtypeStruct((t, e), BF16),
        compiler_params=_params(),
    )(hh, p1, head_w)


def _mlstm_bwd(p1, colf, rowf, head_w, hh, stat, csave, nsave, dhm, n_cols, dk, dv, name, exchange=None):
    t = p1.shape[0]
    nh = N_HEADS
    nc = t // CHUNK
    hw = 2 * dk + dv
    cps = CHUNKS_PER_STEP if nc % CHUNKS_PER_STEP == 0 else 1
    rows_per_step = cps * CHUNK

    def body(p_ref, colf_ref, rowf_ref, w_ref, hh_ref, stat_ref, cs_ref, ns_ref, dhm_ref,
             dp_ref, gs_ref, dw_ref, *state):
        dc_refs, dn_refs, dwacc_refs = state[:nh], state[nh:2 * nh], state[2 * nh:]
        c = pl.program_id(0)

        @pl.when(c == 0)
        def _():
            for ref in state:
                ref[...] = jnp.zeros_like(ref)

        for cc in reversed(range(cps)):
            rows = pl.ds(cc * CHUNK, CHUNK)
            for h in range(nh):
                cols = pl.ds(h * dv, dv)
                oz = pl.ds(nh * hw + h * 2 * dv, 2 * dv)
                head(p_ref.at[rows, pl.ds(h * hw, hw)], p_ref.at[rows, oz], colf_ref.at[h, rows], rowf_ref.at[h, cc],
                     w_ref.at[:, cols], hh_ref.at[rows, cols], stat_ref.at[h, rows], cs_ref.at[h, cc],
                     ns_ref.at[h, cc], dhm_ref.at[rows, cols], dp_ref.at[rows, pl.ds(h * hw, hw)],
                     dp_ref.at[rows, oz], gs_ref.at[h, rows], dwacc_refs[h], dc_refs[h], dn_refs[h])

        @pl.when(c == nc // cps - 1)
        def _():
            for h in range(nh):
                dw_ref[h] = dwacc_refs[h][...]

    def head(p_ref, oz_ref, colf_ref, rowf_ref, w_ref, hh_ref, stat_ref, cs_ref, ns_ref, dhm_ref,
             dp_ref, doz_ref, gs_ref, dw_ref, dc_ref, dn_ref):
        q = p_ref[:, 0:dk]
        k = p_ref[:, dk:2 * dk]
        v = p_ref[:, 2 * dk:2 * dk + dv]
        o = oz_ref[:, 0:dv].astype(F32)
        z = oz_ref[:, dv:2 * dv].astype(F32)
        qf = q.astype(F32)
        kf = k.astype(F32)
        n_prev = ns_ref[0:1, :]
        m_prev = ns_ref[1:2, 0:1]
        c_prev = cs_ref[...]
        m_row, dmat, inter, m_new, decay, w = _chunk_gates(colf_ref, rowf_ref, m_prev)

        hh = hh_ref[...]
        dhm_v = dhm_ref[...].astype(F32)
        so = _sigmoid(o)
        sg = _sigmoid(z)
        sz = z * sg
        r = lax.rsqrt(jnp.mean(hh * hh, axis=-1, keepdims=True) + RMS_EPS)
        hn = (hh * r) * w_ref[...]
        dhn = dhm_v * so * sz
        doz_ref[:, 0:dv] = (dhm_v * hn * sz * (so * (1.0 - so))).astype(BF16)
        doz_ref[:, dv:2 * dv] = (dhm_v * hn * so * (sg * (1.0 + z * (1.0 - sg)))).astype(BF16)
        dw_ref[...] += jnp.broadcast_to(jnp.sum(dhn * (hh * r), axis=0, keepdims=True), dw_ref.shape)
        gwn = dhn * w_ref[...]
        dhh = r * gwn - hh * ((r * r * r) * jnp.mean(gwn * hh, axis=-1, keepdims=True))

        den = stat_ref[:, 0:1]
        floor = jnp.exp(-m_row)
        denom = jnp.maximum(jnp.abs(den), floor)
        inv = 1.0 / denom
        dnum = dhh * inv
        hdot = jnp.sum(dhh * hh, axis=-1, keepdims=True)
        dden = jnp.where(jnp.abs(den) > floor, -(hdot * inv) * jnp.sign(den), 0.0)
        s = _dot_nt(q, k) * dmat
        dqk = (_dot_nt(dnum, v) + dden) * dmat
        dc_new = dc_ref[...]
        dn_new = dn_ref[...]
        idd = inter * dden
        dq = _dot(dqk, k) + inter * _dot_nt(dnum, c_prev) + idd * n_prev
        dkv = _dot_tn(dqk, q) + w * (_dot_nt(v, dc_new) + dn_new)
        dvv = _dot_tn(s, dnum) + w * _dot(k, dc_new)
        dp_ref[:, 0:dk] = dq.astype(BF16)
        dp_ref[:, dk:2 * dk] = dkv.astype(BF16)
        dp_ref[:, 2 * dk:2 * dk + dv] = dvv.astype(BF16)
        qdq = jnp.sum(qf * dq, axis=-1, keepdims=True)
        kdk = jnp.sum(kf * dkv, axis=-1, keepdims=True)
        dc_prev = decay * dc_new + _dot_tn(inter * qf, dnum)
        dn_prev = decay * dn_new + jnp.sum(idd * qf, axis=0, keepdims=True)
        dc_ref[...] = dc_prev
        dn_ref[...] = dn_prev
        cross = (jnp.sum(jnp.sum(c_prev.astype(F32) * dc_prev, axis=-1, keepdims=True), axis=0, keepdims=True)
                 + jnp.sum(n_prev * dn_prev, axis=-1, keepdims=True))
        lane = lax.broadcasted_iota(jnp.int32, (CHUNK, LANES), 1)
        gs_ref[...] = jnp.where(lane == 0, qdq - kdk, jnp.where(lane == 1, kdk, jnp.where(lane == 2, cross, 0.0)))

    ns = nc // cps
    rc = lambda c: (ns - 1 - c, 0)
    rc3 = lambda c: (0, ns - 1 - c, 0)
    rc4 = lambda c: (0, ns - 1 - c, 0, 0)
    return _call(
        body, name=name, grid=(ns,),
        in_specs=[pl.BlockSpec((rows_per_step, nh * (hw + 2 * dv)), rc),
                  pl.BlockSpec((nh, rows_per_step, LANES), rc3),
                  pl.BlockSpec((nh, cps, 8, LANES), rc4),
                  pl.BlockSpec((1, nh * dv), lambda c: (0, 0)),
                  pl.BlockSpec((rows_per_step, nh * dv), rc),
                  pl.BlockSpec((nh, rows_per_step, LANES), rc3),
                  pl.BlockSpec((nh, cps, dk, dv), rc4),
                  pl.BlockSpec((nh, cps, 8, dk), rc4),
                  pl.BlockSpec((rows_per_step, nh * dv), rc)],
        out_specs=[pl.BlockSpec((rows_per_step, nh * (hw + 2 * dv)), rc),
                   pl.BlockSpec((nh, rows_per_step, LANES), rc3),
                   pl.BlockSpec((nh, 8, dv), lambda c: (0, 0, 0))],
        out_shape=[jax.ShapeDtypeStruct((t, n_cols), BF16), jax.ShapeDtypeStruct((nh, t, LANES), F32),
                   jax.ShapeDtypeStruct((nh, 8, dv), F32)],
        scratch_shapes=([pltpu.VMEM((dk, dv), F32)] * nh + [pltpu.VMEM((1, dk), F32)] * nh
                        + [pltpu.VMEM((8, dv), F32)] * nh),
        args=(p1, colf, rowf, head_w, hh, stat, csave, nsave, dhm), exchange=exchange)


def _position():
    return lax.axis_index("x"), lax.axis_index("y"), lax.axis_index("c")


class _Exchange:
    def __init__(self, ins, out_shapes, start, mid, finish):
        n = len(ins)
        self.ins, self.out_shapes = list(ins), list(out_shapes)
        self.start, self.mid, self.finish = start, mid, finish
        self.scratch = [pltpu.SemaphoreType.DMA((n, 7)), pltpu.SemaphoreType.DMA((n, 7)),
                        pltpu.SemaphoreType.DMA((n,))]


def _gather_exchange(shards):
    n = len(shards)

    def plan(ins, outs, sems):
        send_sems, recv_sems, local_sems = sems
        x, y, c = _position()
        me, sibling = (x, y, c), (x, y, 1 - c)
        chips = [(1 - x, y), (x, 1 - y), (1 - x, 1 - y)]

        def copy(a, k, block, to, src=None):
            px, py, pc = block
            dst = outs[a].at[4 * px + 2 * py + pc]
            return pltpu.make_async_remote_copy(
                src_ref=dst if src is None else src, dst_ref=dst,
                send_sem=send_sems.at[a, k], recv_sem=recv_sems.at[a, k],
                device_id=to, device_id_type=MESH)

        def mine():
            return [pltpu.make_async_copy(ins[a], outs[a].at[4 * x + 2 * y + c], local_sems.at[a])
                    for a in range(n)]

        def first():
            out = []
            for a in range(n):
                out.append(copy(a, 0, me, sibling, src=ins[a]))
                out += [copy(a, 1 + j, me, (*chip, c), src=ins[a]) for j, chip in enumerate(chips)]
            return out

        def ici_in():
            return [copy(a, 1 + j, (*chip, c), me) for j, chip in enumerate(chips) for a in range(n)]

        def passed():
            return [copy(a, 4 + j, (*chip, c), sibling) for j, chip in enumerate(chips) for a in range(n)]

        def d2d_in():
            return ([copy(a, 0, sibling, me) for a in range(n)]
                    + [copy(a, 4 + j, (*chip, 1 - c), me) for j, chip in enumerate(chips) for a in range(n)])

        return mine, first, ici_in, passed, d2d_in

    def start(ins, outs, sems):
        mine, first, _, _, _ = plan(ins, outs, sems)
        for cp in mine() + first():
            cp.start()

    def mid(ins, outs, sems):
        _, _, ici_in, passed, _ = plan(ins, outs, sems)
        for arrived, onward in zip(ici_in(), passed()):
            arrived.wait_recv()
            onward.start()

    def finish(ins, outs, sems):
        mine, first, _, passed, d2d_in = plan(ins, outs, sems)
        for cp in d2d_in():
            cp.wait_recv()
        for cp in first() + passed():
            cp.wait_send()
        for cp in mine():
            cp.wait()

    shapes = [jax.ShapeDtypeStruct((N_DEV,) + s.shape, s.dtype) for s in shards]
    return _Exchange(shards, shapes, start, mid, finish)


def _scatter_exchange(fulls):
    n = len(fulls)

    def plan(ins, outs, sems):
        send_sems, recv_sems, local_sems = sems
        x, y, c = _position()
        my_slot = 4 * x + 2 * y + c

        def mine():
            return [pltpu.make_async_copy(ins[a].at[my_slot], outs[a].at[my_slot], local_sems.at[a])
                    for a in range(n)]

        def remote(arriving):
            out = []
            for kk in (1, 2, 4, 6, 3, 5, 7):
                kx, ky, kc = (kk >> 2) & 1, (kk >> 1) & 1, kk & 1
                px = 1 - x if kx else x
                py = 1 - y if ky else y
                pc = 1 - c if kc else c
                peer_slot = 4 * px + 2 * py + pc
                for a in range(n):
                    out.append(pltpu.make_async_remote_copy(
                        src_ref=ins[a].at[peer_slot], dst_ref=outs[a].at[peer_slot if arriving else my_slot],
                        send_sem=send_sems.at[a, kk - 1], recv_sem=recv_sems.at[a, kk - 1],
                        device_id=(px, py, pc), device_id_type=MESH))
            return out

        return mine, remote

    def start(ins, outs, sems):
        mine, remote = plan(ins, outs, sems)
        for cp in mine() + remote(False):
            cp.start()

    def finish(ins, outs, sems):
        mine, remote = plan(ins, outs, sems)
        for cp in remote(True):
            cp.wait_recv()
        for cp in remote(False):
            cp.wait_send()
        for cp in mine():
            cp.wait()

    shapes = [jax.ShapeDtypeStruct(f.shape, f.dtype) for f in fulls]
    return _Exchange(fulls, shapes, start, None, finish)


def _run_exchange(ex, name):
    n_in, n_out = len(ex.ins), len(ex.out_shapes)

    def body(*refs):
        ins, outs, sems = refs[:n_in], refs[n_in:n_in + n_out], refs[n_in + n_out:]
        ex.start(ins, outs, sems)
        if ex.mid is not None:
            ex.mid(ins, outs, sems)
        ex.finish(ins, outs, sems)

    any_spec = pl.BlockSpec(memory_space=pl.ANY)
    return pl.pallas_call(
        body, name=name,
        in_specs=[any_spec] * n_in, out_specs=[any_spec] * n_out,
        out_shape=ex.out_shapes, scratch_shapes=ex.scratch,
    )(*ex.ins)


def _natural_to_head_major(n0, dk, dv):
    nh = N_HEADS
    qk, e, hw = nh * dk, nh * dv, 2 * dk + dv
    if n0 < qk:
        h, off = divmod(n0, dk)
        return h * hw + off, True
    n1 = n0 - qk
    if n1 < qk:
        h, off = divmod(n1, dk)
        return h * hw + dk + off, False
    part, n3 = divmod(n1 - qk, e)
    h, off = divmod(n3, dv)
    if part == 0:
        return h * hw + 2 * dk + off, False
    return nh * hw + h * 2 * dv + (part - 1) * dv + off, False


def _mlstm_weight_layout(w_blocks, dk, dv, name):
    _, d, blk = w_blocks.shape
    nh = N_HEADS
    hw = 2 * dk + 3 * dv
    n_tiles = nh * hw // LANES
    tiles_per_block = d // LANES
    tr = _pick(d, 128, 16)
    scale = dk ** -0.5

    def body(w_ref, p_ref, g_ref):
        lane = lax.broadcasted_iota(jnp.int32, (tr, LANES), 1)

        def tile(s, r):
            return w_ref[s, :, r * LANES:(r + 1) * LANES].astype(F32)

        def last_col(s):
            return w_ref[s, :, d:d + 1].astype(F32)

        for tn in range(n_tiles):
            s, r = divmod(tn, tiles_per_block)
            if s == 0:
                val = tile(0, r)
            elif r == 0:
                from_prev = pltpu.roll(tile(s - 1, tiles_per_block - 1), s - 1, 1)
                from_here = pltpu.roll(tile(s, 0), s, 1)
                val = jnp.where(lane < s - 1, from_prev, jnp.where(lane == s - 1, last_col(s - 1), from_here))
            else:
                slab = jnp.concatenate([tile(s, r - 1), tile(s, r)], axis=1)
                val = pltpu.roll(slab, s, 1)[:, LANES:]
            at, is_q = _natural_to_head_major(tn * LANES, dk, dv)
            if is_q:
                val = val * scale
            p_ref[:, at:at + LANES] = val.astype(BF16)
        n_gate = 2 * nh
        s = N_DEV - 1
        gates = pltpu.roll(tile(s, tiles_per_block - 1), n_gate - 1, 1)
        gates = jnp.where(lane < n_gate - 1, gates, jnp.where(lane == n_gate - 1, last_col(s), 0.0)).astype(BF16)
        g_ref[...] = gates
        p_ref[:, nh * hw:nh * hw + LANES] = gates

    return pl.pallas_call(
        body, name=name, grid=(d // tr,),
        in_specs=[pl.BlockSpec((N_DEV, tr, blk), lambda i: (0, i, 0))],
        out_specs=[pl.BlockSpec((tr, nh * hw + LANES), lambda i: (i, 0)), pl.BlockSpec((tr, LANES), lambda i: (i, 0))],
        out_shape=[jax.ShapeDtypeStruct((d, nh * hw + LANES), BF16), jax.ShapeDtypeStruct((d, LANES), BF16)],
        compiler_params=_params(),
    )(w_blocks)


def _mlstm_grad_layout(g_p1, row0, n_rows, dk, dv, name):
    d = g_p1.shape[0]
    nh = N_HEADS
    hw = 2 * dk + 3 * dv
    n_tiles = nh * hw // LANES
    tiles_per_block = d // LANES
    tr = _pick(d, 128, 16)
    scale = dk ** -0.5

    def body(p_ref, o_ref):
        def natural(tn):
            if tn == n_tiles:
                return p_ref[:, nh * hw:nh * hw + LANES].astype(F32)
            at, is_q = _natural_to_head_major(tn * LANES, dk, dv)
            val = p_ref[:, at:at + LANES].astype(F32)
            return val * scale if is_q else val

        for s in range(N_DEV):
            for r in range(tiles_per_block):
                tn = s * tiles_per_block + r
                if s == 0:
                    val = natural(tn)
                else:
                    slab = jnp.concatenate([natural(tn), natural(tn + 1)], axis=1)
                    val = pltpu.roll(slab, 2 * LANES - s, 1)[:, :LANES]
                o_ref[s, :, r * LANES:(r + 1) * LANES] = val.astype(BF16)
            o_ref[s, :, d:d + 1] = natural((s + 1) * tiles_per_block)[:, s:s + 1].astype(BF16)

    first = row0 // tr
    return pl.pallas_call(
        body, name=name, grid=(n_rows // tr,),
        in_specs=[pl.BlockSpec((tr, nh * hw + LANES), lambda i: (i + first, 0))],
        out_specs=pl.BlockSpec((N_DEV, tr, d + 1), lambda i: (0, i, 0)),
        out_shape=jax.ShapeDtypeStruct((N_DEV, n_rows, d + 1), BF16),
        compiler_params=_params(),
    )(g_p1)


def _adamw_math(w, g, m, v):
    m = ADAM_B1 * m + (1.0 - ADAM_B1) * g
    v = ADAM_B2 * v + (1.0 - ADAM_B2) * (g * g)
    m_hat = m / (1.0 - ADAM_B1 ** ADAM_STEP)
    v_hat = v / (1.0 - ADAM_B2 ** ADAM_STEP)
    delta = -ADAM_LR * (m_hat / (jnp.sqrt(v_hat) + ADAM_EPS) + ADAM_WD * w)
    return delta, m, v


def _adamw_sharded(parts, w, m, v, name):
    n_parts = len(parts)
    _, rk, c = parts[0].shape
    r = n_parts * rk
    tr = _pick(rk, 128, 8)
    per = rk // tr

    def body(*refs):
        p_refs = refs[:n_parts]
        w_ref, m_ref, v_ref, g_ref, d_ref, nm_ref, nv_ref, gsum_ref = refs[n_parts:]
        i = pl.program_id(0)
        for k, p_ref in enumerate(p_refs):
            @pl.when(i // per == k)
            def _(p_ref=p_ref):
                g = p_ref[0].astype(F32)
                for s in range(1, N_DEV):
                    g = g + p_ref[s].astype(F32)
                gsum_ref[...] = g
        g = gsum_ref[...]
        delta, m_new, v_new = _adamw_math(w_ref[...], g, m_ref[...], v_ref[...])
        g_ref[...] = g
        d_ref[...] = delta
        nm_ref[...] = m_new
        nv_ref[...] = v_new

    def part_spec(k):
        return pl.BlockSpec((N_DEV, tr, c), lambda i: (0, jnp.clip(i - k * per, 0, per - 1), 0))

    blk = pl.BlockSpec((tr, c), lambda i: (i, 0))
    return pl.pallas_call(
        body, name=name, grid=(r // tr,),
        in_specs=[part_spec(k) for k in range(n_parts)] + [blk, blk, blk],
        out_specs=[blk] * 4,
        out_shape=[jax.ShapeDtypeStruct((r, c), F32)] * 4,
        scratch_shapes=[pltpu.VMEM((tr, c), F32)],
        compiler_params=_params(),
    )(*parts, w, m, v)


def _sum_devices(parts, name):
    _, r, c = parts.shape

    def body(p_ref, o_ref):
        g = p_ref[0]
        for s in range(1, N_DEV):
            g = g + p_ref[s]
        o_ref[...] = g

    return pl.pallas_call(
        body, name=name, out_shape=jax.ShapeDtypeStruct((r, c), F32), compiler_params=_params(),
    )(parts)


def _adamw_small(gs, ws, ms, vs, name):
    n = len(gs)

    def body(*refs):
        g_refs, w_refs, m_refs, v_refs = refs[:n], refs[n:2 * n], refs[2 * n:3 * n], refs[3 * n:4 * n]
        d_refs, nm_refs, nv_refs = refs[4 * n:5 * n], refs[5 * n:6 * n], refs[6 * n:7 * n]
        for a in range(n):
            delta, m_new, v_new = _adamw_math(w_refs[a][...], g_refs[a][...], m_refs[a][...], v_refs[a][...])
            d_refs[a][...] = delta
            nm_refs[a][...] = m_new
            nv_refs[a][...] = v_new

    shapes = [jax.ShapeDtypeStruct(w.shape, F32) for w in ws]
    outs = pl.pallas_call(
        body, name=name, out_shape=shapes * 3, compiler_params=_params(),
    )(*gs, *ws, *ms, *vs)
    return outs[:n], outs[n:2 * n], outs[2 * n:]


def _pad_rows(a, rows):
    return jnp.pad(a, ((0, rows - a.shape[0]), (0, 0)))


def kernel(x, meta_tokens, norm_w, conv_in_w, conv_w, conv_out_w, mlstm_in_w, mlstm_gate_b, mlstm_head_norm_w, mlstm_out_w, final_norm_w, loss_target, m_meta_tokens, m_norm_w, m_conv_in_w, m_conv_w, m_conv_out_w, m_mlstm_in_w, m_mlstm_gate_b, m_mlstm_head_norm_w, m_mlstm_out_w, m_final_norm_w, v_meta_tokens, v_norm_w, v_conv_in_w, v_conv_w, v_conv_out_w, v_mlstm_in_w, v_mlstm_gate_b, v_mlstm_head_norm_w, v_mlstm_out_w, v_final_norm_w):
    seq, d = x.shape[1], x.shape[2]
    t = seq + LEAD
    e = 2 * d
    ec = e // 2
    nh = N_HEADS
    dv = e // nh
    dk = dv // 2
    qk = nh * dk
    hw = 2 * dk + 3 * dv
    n_in = 2 * qk + 3 * e + 2 * nh
    n_in_s = n_in // N_DEV
    me = 4 * lax.axis_index("x") + 2 * lax.axis_index("y") + lax.axis_index("c")
    tm = _pick(t, 832, 16)
    tm_in = _pick(t, 1664, 16)
    tkw = _pick(t, 2080, 16)

    small = jnp.concatenate([
        meta_tokens,
        _pad_rows(conv_w[0].reshape(3 * (e // N_DEV) // LANES, LANES), 8),
        _pad_rows(mlstm_head_norm_w[0].reshape((e // N_DEV) // LANES, LANES), 8),
    ], axis=0) if d // N_DEV == LANES else None
    assert small is not None, "the packed small-weight block assumes d_model / 8 == 128"
    (small_g,) = _run_exchange(_gather_exchange([small]), "gather_small_weights")
    meta_full = jnp.transpose(small_g[:, 0:N_META, :], (1, 0, 2)).reshape(N_META, d)
    cw_rows = 3 * (e // N_DEV) // LANES
    conv_w_full = jnp.transpose(
        small_g[:, N_META:N_META + cw_rows, :].reshape(N_DEV, 3, e // N_DEV), (1, 0, 2)).reshape(3, e)
    hn_rows = (e // N_DEV) // LANES
    head_w_full = small_g[:, N_META + 8:N_META + 8 + hn_rows, :].reshape(1, e)

    tgt = loss_target[0]

    ci_map = lambda blk: 2 * (blk % 4) + blk // 4
    (h0, u0), (w_ci,) = _rms_first(x[0], meta_full, norm_w[0:1], "rms0",
                                   exchange=_gather_exchange([conv_in_w[0].astype(BF16)]))
    p0, (w_co, w_mi) = _matmul(
        u0, w_ci, form="nn", m=t, n=8 * ec, kdim=d, tm=tm_in, tn=ec, tk=d, out_dtype=BF16, name="conv_in",
        b_spec=pl.BlockSpec((None, d, ec), lambda i, j, k: (ci_map(j), 0, 0)),
        exchange=_gather_exchange([conv_out_w[0].astype(BF16), mlstm_in_w[0].astype(BF16)]))
    w_co = w_co.reshape(e, d)
    (g0, y0), (w_mo,) = _conv_gate_fwd(p0, conv_w_full, "conv_gate",
                                       exchange=_gather_exchange([mlstm_out_w[0].astype(BF16)]))
    w_mo = w_mo.reshape(e, d)
    h1 = _matmul(g0, w_co, form="nn", m=t, n=d, kdim=e, tm=tm, tn=d, tk=e, out_dtype=F32, name="conv_out",
                 residual=h0)

    assert n_in_s == d + 1 and w_mi.shape == (N_DEV, d, d + 1)
    w_p1, w_gate = _mlstm_weight_layout(w_mi, dk, dv, "mlstm_w_layout")
    w_gate_t = jnp.transpose(w_gate[:, 0:16])
    bias_row = jnp.pad(mlstm_gate_b, ((0, 0), (0, LANES - 2 * nh)))
    bias_col = jnp.pad(mlstm_gate_b.T, ((0, 16 - 2 * nh), (0, 0)))

    u1 = _rms_fwd(h1, norm_w[1:2], "rms1")
    p1 = _matmul(u1, w_p1, form="nn", m=t, n=nh * hw, kdim=d, tm=tm_in, tn=_pick(nh * hw, 2048, LANES), tk=d,
                 out_dtype=BF16, name="mlstm_in")
    graw = _matmul(u1, w_gate, form="nn", m=t, n=LANES, kdim=d, tm=tm, tn=LANES, tk=d, out_dtype=F32,
                   name="gates_col")
    graw_t = _matmul(w_gate_t, u1, form="nt", m=16, n=t, kdim=d, tm=16, tn=_pick(t, 1664, LANES), tk=d,
                     out_dtype=F32, name="gates_row")
    colf, rowf = _gates_fwd(graw, graw_t, bias_row, bias_col, "gates_fwd")
    hh, stat, csave, nsave = _mlstm_fwd(p1, colf, rowf, dk, dv, "mlstm_fwd")
    hm = _head_gate_fwd(hh, p1, head_w_full, dv, "head_gate")
    h2 =_matmul(hm, w_mo, form="nn", m=t, n=d, kdim=e, tm=tm, tn=d, tk=e, out_dtype=F32, name="mlstm_out",
                 residual=h1)

    dh2, dwf, loss_part = _final_loss(h2, final_norm_w.reshape(1, d), tgt, "final_loss")

    dhm = _matmul(dh2, w_mo, form="nt", m=t, n=e, kdim=d, tm=tm, tn=_pick(e, 2048, LANES), tk=d, out_dtype=BF16,
                  name="mlstm_out_dx")
    g_mo = _matmul(hm, dh2, form="tn", m=e, n=d, kdim=t, tm=_pick(e, 1024, LANES), tn=d, tk=tkw, out_dtype=BF16,
                   name="mlstm_out_dw")
    n_p1 = nh * hw + LANES
    (dp1, gstat, dhead), (r_mo,) = _mlstm_bwd(
        p1, colf, rowf, head_w_full, hh, stat, csave, nsave, dhm, n_p1, dk, dv, "mlstm_bwd",
        exchange=_scatter_exchange([g_mo.reshape(N_DEV, e // N_DEV, d)]))
    dp1, dbias = _gates_bwd(gstat, colf, dp1, "gates_bwd")
    du1 = _matmul(dp1, w_p1, form="nt", m=t, n=d, kdim=n_p1, tm=tm_in, tn=d, tk=_pick(n_p1, 2048, LANES),
                  out_dtype=F32, name="mlstm_in_dx")
    g_p1 = _matmul(u1, dp1, form="tn", m=d, n=n_p1, kdim=t, tm=d, tn=_pick(n_p1, 2048, LANES), tk=tkw,
                   out_dtype=BF16, name="mlstm_in_dw")
    dh1, dnw1 = _rms_bwd(h1, norm_w[1:2], du1, dh2, "rms1_bwd")

    g_mi_a = _mlstm_grad_layout(g_p1, 0, d // 2, dk, dv, "mlstm_g_layout_a")
    g_mi_b = _mlstm_grad_layout(g_p1, d // 2, d // 2, dk, dv, "mlstm_g_layout_b")

    dg0 =_matmul(dh1, w_co, form="nt", m=t, n=e, kdim=d, tm=tm, tn=_pick(e, 2048, LANES), tk=d, out_dtype=BF16,
                  name="conv_out_dx")
    g_co = _matmul(g0, dh1, form="tn", m=e, n=d, kdim=t, tm=_pick(e, 1024, LANES), tn=d, tk=tkw, out_dtype=BF16,
                   name="conv_out_dw")
    (dp0, dconv), (r_co, r_mi_a) = _conv_gate_bwd(
        p0, y0, dg0, conv_w_full, "conv_gate_bwd",
        exchange=_scatter_exchange([g_co.reshape(N_DEV, e // N_DEV, d), g_mi_a]))
    g_ci, (r_mi_b,) = _matmul(
        u0, dp0, form="tn", m=d, n=8 * ec, kdim=t, tm=d, tn=ec, tk=tkw, out_dtype=BF16, name="conv_in_dw",
        out_shape=(N_DEV, d, ec), out_spec=pl.BlockSpec((None, d, ec), lambda i, j, k: (ci_map(j), 0, 0)),
        exchange=_scatter_exchange([g_mi_b]))
    du0, (r_ci,) = _matmul(
        dp0, w_ci, form="nt", m=t, n=d, kdim=8 * ec, tm=tm_in, tn=d, tk=ec, out_dtype=F32, name="conv_in_dx",
        b_spec=pl.BlockSpec((None, d, ec), lambda i, j, k: (ci_map(k), 0, 0)),
        exchange=_scatter_exchange([g_ci]))
    grad_x, dmeta, dnw0 = _rms_bwd_first(h0, norm_w[0:1], du0, dh1, "rms0_bwd")
    grad_x = grad_x[None]

    row8 = lax.broadcasted_iota(jnp.int32, (8, 1), 0)
    loss_wide = jnp.pad(loss_part, ((0, 0), (0, d - LANES)))
    payload = jnp.concatenate([
        jnp.where(row8 == 0, dnw0, jnp.where(row8 == 1, dnw1, 0.0)),
        jnp.where(row8 == 0, dwf, jnp.where(row8 == 1, loss_wide, 0.0)),
        jnp.where(row8 == 0, jnp.pad(dbias, ((0, 0), (0, d - LANES))), 0.0),
        dmeta,
        _pad_rows(dconv[0:3].reshape(3 * e // d, d), 8),
        _pad_rows(dhead[:, 0, :].reshape(e // d, d), 8),
    ], axis=0)
    (payload_g,) = _run_exchange(_gather_exchange([payload]), "gather_small_grads")
    tot = _sum_devices(payload_g, "sum_small_grads")

    loss = tot[9, 0]
    g_norm = tot[0:2]
    g_final = tot[8]
    g_gate_b = tot[16:17, 0:2 * nh]
    g_meta = lax.dynamic_slice(tot[24:24 + N_META], (0, me * (d // N_DEV)), (N_META, d // N_DEV))
    g_conv_w = lax.dynamic_slice(tot[40:40 + 3 * e // d].reshape(3, e), (0, me * (e // N_DEV)), (3, e // N_DEV))
    g_head = lax.dynamic_slice(tot[48:48 + e // d].reshape(1, e), (0, me * (e // N_DEV)), (1, e // N_DEV))

    g1, d1, nm1, nv1 = _adamw_sharded([r_ci], conv_in_w[0], m_conv_in_w[0], v_conv_in_w[0], "adamw_conv_in")
    g2, d2, nm2, nv2 = _adamw_sharded([r_co], conv_out_w[0], m_conv_out_w[0], v_conv_out_w[0], "adamw_conv_out")
    g3, d3, nm3, nv3 = _adamw_sharded([r_mi_a, r_mi_b], mlstm_in_w[0], m_mlstm_in_w[0], v_mlstm_in_w[0],
                                      "adamw_mlstm_in")
    g4, d4, nm4, nv4 = _adamw_sharded([r_mo], mlstm_out_w[0], m_mlstm_out_w[0], v_mlstm_out_w[0],
                                      "adamw_mlstm_out")

    small_g = [g_meta, g_norm, g_conv_w, g_gate_b, g_head, g_final.reshape(1, d)]
    small_w = [meta_tokens, norm_w, conv_w[0], mlstm_gate_b, mlstm_head_norm_w, final_norm_w.reshape(1, d)]
    small_m = [m_meta_tokens, m_norm_w, m_conv_w[0], m_mlstm_gate_b, m_mlstm_head_norm_w, m_final_norm_w.reshape(1, d)]
    small_v = [v_meta_tokens, v_norm_w, v_conv_w[0], v_mlstm_gate_b, v_mlstm_head_norm_w, v_final_norm_w.reshape(1, d)]
    sd, snm, snv = _adamw_small(small_g, small_w, small_m, small_v, "adamw_small")

    def order(meta, norm, cin, cw, cout, min_, gb, hn, mout, fin):
        return (meta, norm, cin[None], cw[None], cout[None], min_[None], gb, hn, mout[None], fin.reshape(d))

    grads = order(g_meta, g_norm, g1, g_conv_w, g2, g3, g_gate_b, g_head, g4, g_final)
    deltas = order(sd[0], sd[1], d1, sd[2], d2, d3, sd[3], sd[4], d4, sd[5])
    new_m = order(snm[0], snm[1], nm1, snm[2], nm2, nm3, snm[3], snm[4], nm4, snm[5])
    new_v = order(snv[0], snv[1], nv1, snv[2], nv2, nv3, snv[3], snv[4], nv4, snv[5])
    return (loss, grad_x, *grads, *deltas, *new_m, *new_v)
```

```python
import functools

import jax
import jax.numpy as jnp
from jax import lax
from jax.experimental import pallas as pl
from jax.experimental.pallas import tpu as pltpu

F32 = jnp.float32
BF16 = jnp.bfloat16
MESH = pl.DeviceIdType.MESH

N_DEV = 8
N_META = 16
N_HEADS = 4
CHUNK = 64
CHUNKS_PER_STEP = 2
LEAD = 128
PAD_ROWS = LEAD - N_META
RMS_EPS = 1e-6
NEG = -1e30
LANES = 128
VMEM_LIMIT = 48 * 1024 * 1024

ADAM_LR = 0.001
ADAM_B1 = 0.9
ADAM_B2 = 0.999
ADAM_EPS = 1e-08
ADAM_WD = 0.01
ADAM_STEP = 10

HIGHEST = lax.Precision.HIGHEST


def _pick(n, target, mult):
    best = None
    for d in range(mult, min(n, target) + 1, mult):
        if n % d == 0:
            best = d
    return n if best is None else best


def _params(**kw):
    return pltpu.CompilerParams(vmem_limit_bytes=VMEM_LIMIT, **kw)


def _sigmoid(x):
    return 0.5 * jnp.tanh(0.5 * x) + 0.5


STRIP_ROWS = 16
STRIP_COLS = 256


def _strips(n_rows, n_cols):
    for c0 in range(0, n_cols, STRIP_COLS):
        for r0 in range(0, n_rows, STRIP_ROWS):
            yield slice(r0, r0 + STRIP_ROWS), slice(c0, min(c0 + STRIP_COLS, n_cols))


def _shift(sl, by):
    return slice(sl.start + by, sl.stop + by)


def _call(body, *, name, grid, in_specs, out_specs, out_shape, args, scratch_shapes=(), aliases=None,
          exchange=None):
    aliases = {} if aliases is None else aliases
    if exchange is None:
        return pl.pallas_call(
            body, name=name, grid=grid, in_specs=list(in_specs), out_specs=out_specs, out_shape=out_shape,
            scratch_shapes=list(scratch_shapes), input_output_aliases=aliases,
            compiler_params=_params())(*args)
    ex = exchange
    single = not isinstance(out_shape, (list, tuple))
    shapes = [out_shape] if single else list(out_shape)
    specs = [out_specs] if single else list(out_specs)
    n_in, n_out, n_scr = len(in_specs), len(shapes), len(scratch_shapes)
    n_ex_in, n_ex_out = len(ex.ins), len(ex.out_shapes)
    steps = 1
    for size in grid:
        steps *= size

    def wrapped(*refs):
        own_in, ex_in = refs[:n_in], refs[n_in:n_in + n_ex_in]
        at = n_in + n_ex_in
        own_out, ex_out = refs[at:at + n_out], refs[at + n_out:at + n_out + n_ex_out]
        at += n_out + n_ex_out
        own_scr, ex_scr = refs[at:at + n_scr], refs[at + n_scr:]
        step = 0
        for axis, size in enumerate(grid):
            step = step * size + pl.program_id(axis)

        @pl.when(step == 0)
        def _():
            ex.start(ex_in, ex_out, ex_scr)

        if ex.mid is not None:
            @pl.when(step == (3 * steps) // 4)
            def _():
                ex.mid(ex_in, ex_out, ex_scr)

        body(*own_in, *own_out, *own_scr)

        @pl.when(step == steps - 1)
        def _():
            ex.finish(ex_in, ex_out, ex_scr)

    any_spec = pl.BlockSpec(memory_space=pl.ANY)
    res = pl.pallas_call(
        wrapped, name=name, grid=grid,
        in_specs=list(in_specs) + [any_spec] * n_ex_in, out_specs=specs + [any_spec] * n_ex_out,
        out_shape=shapes + ex.out_shapes, scratch_shapes=list(scratch_shapes) + ex.scratch,
        input_output_aliases=aliases, compiler_params=_params())(*args, *ex.ins)
    return (res[0] if single else res[:n_out]), res[n_out:]


def _matmul(a, b, *, form, m, n, kdim, tm, tn, tk, out_dtype, name,
            a_spec=None, b_spec=None, out_spec=None, out_shape=None, residual=None, fill=None, exchange=None):
    ni, nj, nk = m // tm, n // tn, kdim // tk
    assert ni * tm == m and nj * tn == n and nk * tk == kdim, (name, m, n, kdim, tm, tn, tk)
    if form == "nn":
        dn = (((1,), (0,)), ((), ()))
        a_def = pl.BlockSpec((tm, tk), lambda i, j, k: (i, k))
        b_def = pl.BlockSpec((tk, tn), lambda i, j, k: (k, j))
    elif form == "nt":
        dn = (((1,), (1,)), ((), ()))
        a_def = pl.BlockSpec((tm, tk), lambda i, j, k: (i, k))
        b_def = pl.BlockSpec((tn, tk), lambda i, j, k: (j, k))
    else:
        dn = (((0,), (0,)), ((), ()))
        a_def = pl.BlockSpec((tk, tm), lambda i, j, k: (k, i))
        b_def = pl.BlockSpec((tk, tn), lambda i, j, k: (k, j))
    a_spec = a_def if a_spec is None else a_spec
    b_spec = b_def if b_spec is None else b_spec
    o_spec = pl.BlockSpec((tm, tn), lambda i, j, k: (i, j)) if out_spec is None else out_spec
    has_res = residual is not None
    has_fill = fill is not None

    def body(*refs):
        a_ref, b_ref = refs[:2]
        r_ref = refs[2] if has_res else None
        o_ref = refs[2 + has_res + has_fill]

        def product():
            return lax.dot_general(a_ref[...].astype(BF16), b_ref[...].astype(BF16), dn,
                                   preferred_element_type=F32)

        def finish(acc):
            if has_res:
                acc = acc + r_ref[...].astype(F32)
            o_ref[...] = acc.astype(o_ref.dtype)

        if nk == 1:
            finish(product())
        else:
            acc_ref = refs[3 + has_res + has_fill]
            k = pl.program_id(2)

            @pl.when(k == 0)
            def _():
                acc_ref[...] = jnp.zeros_like(acc_ref)

            acc_ref[...] += product()

            @pl.when(k == nk - 1)
            def _():
                finish(acc_ref[...])

    in_specs = [a_spec, b_spec]
    args = [a, b]
    if has_res:
        in_specs.append(pl.BlockSpec((tm, tn), lambda i, j, k: (i, j)))
        args.append(residual)
    aliases = None
    if has_fill:
        in_specs.append(pl.BlockSpec(memory_space=pl.ANY))
        args.append(fill)
        aliases = {len(args) - 1: 0}
    return _call(
        body, name=name, grid=(ni, nj, nk), in_specs=in_specs, out_specs=o_spec,
        out_shape=jax.ShapeDtypeStruct((m, n) if out_shape is None else out_shape, out_dtype),
        scratch_shapes=[] if nk == 1 else [pltpu.VMEM((tm, tn), F32)], args=args, aliases=aliases,
        exchange=exchange)


def _rms_fwd(h, w, name, exchange=None):
    t, d = h.shape
    tm = _pick(t, 832, 16)

    def body(h_ref, w_ref, u_ref):
        x = h_ref[...]
        r = lax.rsqrt(jnp.mean(x * x, axis=-1, keepdims=True) + RMS_EPS)
        u_ref[...] = ((x * r) * w_ref[...]).astype(BF16)

    return _call(
        body, name=name, grid=(t // tm,),
        in_specs=[pl.BlockSpec((tm, d), lambda i: (i, 0)), pl.BlockSpec((1, d), lambda i: (0, 0))],
        out_specs=pl.BlockSpec((tm, d), lambda i: (i, 0)),
        out_shape=jax.ShapeDtypeStruct((t, d), BF16), args=(h, w), exchange=exchange)


def _rms_first(x, meta, w, name, exchange=None):
    seq, d = x.shape
    t = seq + LEAD
    tm = LEAD

    def body(x_ref, meta_ref, w_ref, h_ref, u_ref):
        i = pl.program_id(0)

        @pl.when(i == 0)
        def _():
            h_ref[0:PAD_ROWS, :] = jnp.zeros((PAD_ROWS, d), F32)
            h_ref[PAD_ROWS:LEAD, :] = meta_ref[...]

        @pl.when(i > 0)
        def _():
            h_ref[...] = x_ref[...]

        hv = h_ref[...]
        r = lax.rsqrt(jnp.mean(hv * hv, axis=-1, keepdims=True) + RMS_EPS)
        u_ref[...] = ((hv * r) * w_ref[...]).astype(BF16)

    row = pl.BlockSpec((tm, d), lambda i: (i, 0))
    return _call(
        body, name=name, grid=(t // tm,),
        in_specs=[pl.BlockSpec((tm, d), lambda i: (jnp.maximum(i - 1, 0), 0)),
                  pl.BlockSpec((N_META, d), lambda i: (0, 0)), pl.BlockSpec((1, d), lambda i: (0, 0))],
        out_specs=[row, row],
        out_shape=[jax.ShapeDtypeStruct((t, d), F32), jax.ShapeDtypeStruct((t, d), BF16)],
        args=(x, meta, w), exchange=exchange)


def _rms_bwd(h, w, du, dres, name):
    t, d = h.shape
    tm = _pick(t, 832, 16)

    def body(h_ref, w_ref, du_ref, dres_ref, dh_ref, dw_ref):
        i = pl.program_id(0)
        x = h_ref[...]
        g = du_ref[...].astype(F32)
        r = lax.rsqrt(jnp.mean(x * x, axis=-1, keepdims=True) + RMS_EPS)
        gw = g * w_ref[...]
        dot = jnp.mean(gw * x, axis=-1, keepdims=True)
        dh_ref[...] = dres_ref[...] + (r * gw - x * ((r * r * r) * dot))
        part = jnp.sum(g * (x * r), axis=0, keepdims=True)

        @pl.when(i == 0)
        def _():
            dw_ref[...] = jnp.zeros_like(dw_ref)

        dw_ref[...] += jnp.broadcast_to(part, dw_ref.shape)

    row = pl.BlockSpec((tm, d), lambda i: (i, 0))
    return pl.pallas_call(
        body, name=name, grid=(t // tm,),
        in_specs=[row, pl.BlockSpec((1, d), lambda i: (0, 0)), row, row],
        out_specs=[row, pl.BlockSpec((8, d), lambda i: (0, 0))],
        out_shape=[jax.ShapeDtypeStruct((t, d), F32), jax.ShapeDtypeStruct((8, d), F32)],
        compiler_params=_params(),
    )(h, w, du, dres)


def _rms_bwd_first(h, w, du, dres, name):
    t, d = h.shape
    sub = LEAD
    per = _pick((t - LEAD) // sub, 8, 1)
    tmo = per * sub
    n_out = (t - LEAD) // tmo
    n_blocks = t // sub

    def body(*refs):
        w_ref = refs[0]
        h_refs, du_refs, dres_refs = refs[1:1 + per], refs[1 + per:1 + 2 * per], refs[1 + 2 * per:1 + 3 * per]
        gx_ref, dmeta_ref, dw_ref = refs[1 + 3 * per:]
        i = pl.program_id(0)

        def piece(q):
            x = h_refs[q][...]
            g = du_refs[q][...].astype(F32)
            r = lax.rsqrt(jnp.mean(x * x, axis=-1, keepdims=True) + RMS_EPS)
            gw = g * w_ref[...]
            dot = jnp.mean(gw * x, axis=-1, keepdims=True)
            dh = dres_refs[q][...] + (r * gw - x * ((r * r * r) * dot))
            return dh, jnp.sum(g * (x * r), axis=0, keepdims=True)

        @pl.when(i == 0)
        def _():
            dh, part = piece(0)
            dmeta_ref[...] = dh[PAD_ROWS:LEAD, :]
            dw_ref[...] = jnp.broadcast_to(part, dw_ref.shape)

        @pl.when(i > 0)
        def _():
            for q in range(per):
                dh, part = piece(q)
                gx_ref[q * sub:(q + 1) * sub, :] = dh
                dw_ref[...] += jnp.broadcast_to(part, dw_ref.shape)

    def piece_spec(q):
        return pl.BlockSpec((sub, d), lambda i: (jnp.clip(1 + per * (i - 1) + q, 0, n_blocks - 1), 0))

    pieces = [piece_spec(q) for q in range(per)]
    return pl.pallas_call(
        body, name=name, grid=(n_out + 1,),
        in_specs=[pl.BlockSpec((1, d), lambda i: (0, 0))] + pieces * 3,
        out_specs=[pl.BlockSpec((tmo, d), lambda i: (jnp.maximum(i - 1, 0), 0)),
                   pl.BlockSpec((N_META, d), lambda i: (0, 0)), pl.BlockSpec((8, d), lambda i: (0, 0))],
        out_shape=[jax.ShapeDtypeStruct((t - LEAD, d), F32), jax.ShapeDtypeStruct((N_META, d), F32),
                   jax.ShapeDtypeStruct((8, d), F32)],
        compiler_params=_params(),
    )(w, *([h] * per), *([du] * per), *([dres] * per))


def _final_loss(h, w, tgt, name):
    t, d = h.shape
    sub = LEAD
    per = _pick(t // sub, 5, 1)
    tm = per * sub

    def body(h_ref, w_ref, *rest):
        t_refs, (dh_ref, dw_ref, loss_ref) = rest[:per], rest[per:]
        i = pl.program_id(0)

        @pl.when(i == 0)
        def _():
            dw_ref[...] = jnp.zeros_like(dw_ref)
            loss_ref[...] = jnp.zeros_like(loss_ref)

        for q in range(per):
            sl = slice(q * sub, (q + 1) * sub)
            x = h_ref[sl, :]
            r = lax.rsqrt(jnp.mean(x * x, axis=-1, keepdims=True) + RMS_EPS)
            yn = x * r
            y = yn * w_ref[...]
            rows = i * tm + q * sub + lax.broadcasted_iota(jnp.int32, (sub, 1), 0)
            diff = jnp.where(rows >= LEAD, y - t_refs[q][...], 0.0)
            tile_loss = 0.5 * jnp.sum(jnp.mean(diff * diff, axis=-1, keepdims=True), axis=0, keepdims=True)
            dy = diff / d
            gw = dy * w_ref[...]
            dot = jnp.mean(gw * x, axis=-1, keepdims=True)
            dh_ref[sl, :] = r * gw - x * ((r * r * r) * dot)
            dw_ref[...] += jnp.broadcast_to(jnp.sum(dy * yn, axis=0, keepdims=True), dw_ref.shape)
            loss_ref[...] += jnp.broadcast_to(tile_loss, loss_ref.shape)

    row = pl.BlockSpec((tm, d), lambda i: (i, 0))
    piece = [pl.BlockSpec((sub, d), functools.partial(lambda i, q: (jnp.maximum(i * per + q - 1, 0), 0), q=q))
             for q in range(per)]
    return pl.pallas_call(
        body, name=name, grid=(t // tm,),
        in_specs=[row, pl.BlockSpec((1, d), lambda i: (0, 0))] + piece,
        out_specs=[row, pl.BlockSpec((8, d), lambda i: (0, 0)), pl.BlockSpec((8, LANES), lambda i: (0, 0))],
        out_shape=[jax.ShapeDtypeStruct((t, d), F32), jax.ShapeDtypeStruct((8, d), F32),
                   jax.ShapeDtypeStruct((8, LANES), F32)],
        compiler_params=_params(),
    )(h, w, *([tgt] * per))


def _conv_gate_fwd(p0, conv_w, name, exchange=None):
    t = p0.shape[0]
    ec = p0.shape[1] // 8
    tm = _pick(t, 832, 16)
    nt = t // tm

    def body(p_ref, w_ref, g_ref, y_ref, ext_ref):
        i = pl.program_id(1)

        @pl.when(i == 0)
        def _():
            ext_ref[0:8, :] = jnp.zeros((8, ec), F32)

        for rows, cols in _strips(tm, ec):
            cg = p_ref[rows, _shift(cols, ec)].astype(F32)
            xin = p_ref[rows, _shift(cols, 2 * ec)].astype(F32)
            ext_ref[_shift(rows, 8), cols] = cg * xin
        for rows, cols in _strips(tm, ec):
            y = (w_ref[0:1, cols] * ext_ref[_shift(rows, 6), cols] + w_ref[1:2, cols] * ext_ref[_shift(rows, 7), cols]
                 + w_ref[2:3, cols] * ext_ref[_shift(rows, 8), cols])
            bg = p_ref[rows, cols].astype(F32)
            z = p_ref[rows, _shift(cols, 3 * ec)].astype(F32)
            y_ref[rows, cols] = y.astype(BF16)
            g_ref[rows, cols] = ((z * _sigmoid(z)) * bg * y).astype(BF16)
        ext_ref[0:8, :] = ext_ref[tm:tm + 8, :]

    out = pl.BlockSpec((tm, ec), lambda j, i: (i, j))
    return _call(
        body, name=name, grid=(2, nt),
        in_specs=[pl.BlockSpec((tm, 4 * ec), lambda j, i: (i, j)), pl.BlockSpec((3, ec), lambda j, i: (0, j))],
        out_specs=[out, out],
        out_shape=[jax.ShapeDtypeStruct((t, 2 * ec), BF16)] * 2,
        scratch_shapes=[pltpu.VMEM((tm + 8, ec), F32)], args=(p0, conv_w), exchange=exchange)


def _conv_gate_bwd(p0, y, dg, conv_w, name, exchange=None):
    t = p0.shape[0]
    ec = p0.shape[1] // 8
    tm = _pick(t, 416, 16)
    nt = t // tm

    def body(p_ref, y_ref, dg_ref, w_ref, dp_ref, dw_ref, ext_ref):
        i = pl.program_id(1)

        @pl.when(i == 0)
        def _():
            ext_ref[tm:tm + 8, :] = jnp.zeros((8, ec), F32)
            dw_ref[...] = jnp.zeros_like(dw_ref)

        for rows, cols in _strips(tm, ec):
            bg = p_ref[rows, cols].astype(F32)
            z = p_ref[rows, _shift(cols, 3 * ec)].astype(F32)
            yv = y_ref[rows, cols].astype(F32)
            dgv = dg_ref[rows, cols].astype(F32)
            sig = _sigmoid(z)
            sz = z * sig
            dp_ref[rows, _shift(cols, 3 * ec)] = (dgv * bg * yv * (sig * (1.0 + z * (1.0 - sig)))).astype(BF16)
            dp_ref[rows, cols] = (dgv * sz * yv).astype(BF16)
            ext_ref[rows, cols] = dgv * sz * bg
        acc = None
        for rows, cols in _strips(tm, ec):
            if rows.start == 0:
                acc = [jnp.zeros((STRIP_ROWS, cols.stop - cols.start), F32) for _ in range(3)]
            dy = ext_ref[rows, cols]
            dy1 = ext_ref[_shift(rows, 1), cols]
            dy2 = ext_ref[_shift(rows, 2), cols]
            cg = p_ref[rows, _shift(cols, ec)].astype(F32)
            xin = p_ref[rows, _shift(cols, 2 * ec)].astype(F32)
            da = w_ref[0:1, cols] * dy2 + w_ref[1:2, cols] * dy1 + w_ref[2:3, cols] * dy
            dp_ref[rows, _shift(cols, ec)] = (da * xin).astype(BF16)
            dp_ref[rows, _shift(cols, 2 * ec)] = (da * cg).astype(BF16)
            a = cg * xin
            acc = [acc[0] + a * dy2, acc[1] + a * dy1, acc[2] + a * dy]
            if rows.stop == tm:
                for tap in range(3):
                    dw_ref[tap:tap + 1, cols] += jnp.sum(acc[tap], axis=0, keepdims=True)
        ext_ref[tm:tm + 8, :] = ext_ref[0:8, :]

    rev = lambda j, i: (nt - 1 - i, j)
    return _call(
        body, name=name, grid=(2, nt),
        in_specs=[pl.BlockSpec((tm, 4 * ec), rev), pl.BlockSpec((tm, ec), rev), pl.BlockSpec((tm, ec), rev),
                  pl.BlockSpec((3, ec), lambda j, i: (0, j))],
        out_specs=[pl.BlockSpec((tm, 4 * ec), rev), pl.BlockSpec((8, ec), lambda j, i: (0, j))],
        out_shape=[jax.ShapeDtypeStruct((t, 8 * ec), BF16), jax.ShapeDtypeStruct((8, 2 * ec), F32)],
        scratch_shapes=[pltpu.VMEM((tm + 8, ec), F32)], args=(p0, y, dg, conv_w), exchange=exchange)


def _log_sigmoid(x):
    return jnp.minimum(x, 0.0) - jnp.log(1.0 + jnp.exp(-jnp.abs(x)))


def _gates_fwd(graw, graw_t, bias_row, bias_col, name):
    t = graw.shape[0]
    tm = _pick(t, 640, 128)
    cpt = tm // CHUNK
    nh = N_HEADS

    def body(g_ref, gt_ref, br_ref, bc_ref, colf_ref, rowf_ref):
        i = pl.program_id(0)
        gc = g_ref[...] + br_ref[...]
        rows = i * tm + lax.broadcasted_iota(jnp.int32, (tm, 1), 0)
        live = rows >= PAD_ROWS
        lf = jnp.where(live, _log_sigmoid(gc), 0.0)
        li = jnp.where(live, gc, NEG)
        gt = gt_ref[...] + bc_ref[...]
        cols = i * tm + lax.broadcasted_iota(jnp.int32, (1, tm), 1)
        live_t = cols >= PAD_ROWS
        lf_t = jnp.where(live_t, _log_sigmoid(gt), 0.0)
        li_t = jnp.where(live_t, gt, NEG)
        ri = lax.broadcasted_iota(jnp.int32, (CHUNK, CHUNK), 0)
        ci = lax.broadcasted_iota(jnp.int32, (CHUNK, CHUNK), 1)
        lower = (ri >= ci).astype(F32)
        upper = (ri <= ci).astype(F32)
        lane = lax.broadcasted_iota(jnp.int32, (CHUNK, LANES), 1)
        sub = lax.broadcasted_iota(jnp.int32, (8, CHUNK), 0)
        for c in range(cpt):
            sl = slice(c * CHUNK, (c + 1) * CHUNK)
            b_all = jnp.dot(lower, lf[sl, :], precision=HIGHEST, preferred_element_type=F32)
            bt_all = jnp.dot(lf_t[:, sl], upper, precision=HIGHEST, preferred_element_type=F32)
            for h in range(nh):
                b = b_all[:, nh + h:nh + h + 1]
                r = li[sl, h:h + 1] - b
                pre = gc[sl, nh + h:nh + h + 1]
                colf_ref[h, sl, :] = jnp.where(lane == 0, b, jnp.where(lane == 1, r, jnp.where(lane == 2, pre, 0.0)))
                b_row = bt_all[nh + h:nh + h + 1, :]
                r_row = li_t[h:h + 1, sl] - b_row
                rowf_ref[h, c, :, 0:CHUNK] = jnp.where(sub == 0, r_row, jnp.where(sub == 1, b_row, 0.0))
                rowf_ref[h, c, :, CHUNK:LANES] = jnp.zeros((8, LANES - CHUNK), F32)

    return pl.pallas_call(
        body, name=name, grid=(t // tm,),
        in_specs=[pl.BlockSpec((tm, LANES), lambda i: (i, 0)), pl.BlockSpec((16, tm), lambda i: (0, i)),
                  pl.BlockSpec((1, LANES), lambda i: (0, 0)), pl.BlockSpec((16, 1), lambda i: (0, 0))],
        out_specs=[pl.BlockSpec((nh, tm, LANES), lambda i: (0, i, 0)),
                   pl.BlockSpec((nh, cpt, 8, LANES), lambda i: (0, i, 0, 0))],
        out_shape=[jax.ShapeDtypeStruct((nh, t, LANES), F32),
                   jax.ShapeDtypeStruct((nh, t // CHUNK, 8, LANES), F32)],
        compiler_params=_params(),
    )(graw, graw_t, bias_row, bias_col)


def _gates_bwd(gstat, colf, dp1, name):
    nh, t, _ = gstat.shape
    tm = _pick(t, 640, 128)
    cpt = tm // CHUNK
    nt = t // tm
    nc = t // CHUNK
    gate_block = dp1.shape[1] // LANES - 1

    def body(gs_ref, nx_ref, colf_ref, dp_any, dg_ref, db_ref):
        i = pl.program_id(0)
        ri = lax.broadcasted_iota(jnp.int32, (CHUNK, CHUNK), 0)
        ci = lax.broadcasted_iota(jnp.int32, (CHUNK, CHUNK), 1)
        upper = (ri <= ci).astype(F32)
        lane = lax.broadcasted_iota(jnp.int32, (CHUNK, LANES), 1)
        total = jnp.zeros((1, LANES), F32)
        for c in range(cpt):
            sl = slice(c * CHUNK, (c + 1) * CHUNK)
            rows = i * tm + c * CHUNK + lax.broadcasted_iota(jnp.int32, (CHUNK, 1), 0)
            live = rows >= PAD_ROWS
            acc = jnp.zeros((CHUNK, LANES), F32)
            for h in range(nh):
                blk = gs_ref[h, sl, :]
                rev = jnp.dot(upper, blk, precision=HIGHEST, preferred_element_type=F32)
                if c + 1 < cpt:
                    carry = gs_ref[h, (c + 1) * CHUNK:(c + 1) * CHUNK + 1, 2:3]
                else:
                    carry = jnp.where(i == nt - 1, 0.0, nx_ref[h, 0:1, 2:3])
                dlogf = rev[:, 0:1] + carry
                pre = colf_ref[h, sl, 2:3]
                dgf = jnp.where(live, dlogf * (1.0 - _sigmoid(pre)), 0.0)
                dgi = jnp.where(live, blk[:, 1:2], 0.0)
                acc = acc + jnp.where(lane == h, dgi, 0.0) + jnp.where(lane == nh + h, dgf, 0.0)
            dg_ref[sl, :] = acc.astype(BF16)
            total = total + jnp.sum(acc, axis=0, keepdims=True)

        @pl.when(i == 0)
        def _():
            db_ref[...] = jnp.zeros_like(db_ref)

        db_ref[...] += jnp.broadcast_to(total, db_ref.shape)

    return pl.pallas_call(
        body, name=name, grid=(nt,),
        in_specs=[pl.BlockSpec((nh, tm, LANES), lambda i: (0, i, 0)),
                  pl.BlockSpec((nh, CHUNK, LANES), lambda i: (0, jnp.minimum((i + 1) * cpt, nc - 1), 0)),
                  pl.BlockSpec((nh, tm, LANES), lambda i: (0, i, 0)),
                  pl.BlockSpec(memory_space=pl.ANY)],
        out_specs=[pl.BlockSpec((tm, LANES), lambda i: (i, gate_block)), pl.BlockSpec((8, LANES), lambda i: (0, 0))],
        out_shape=[jax.ShapeDtypeStruct(dp1.shape, dp1.dtype), jax.ShapeDtypeStruct((8, LANES), F32)],
        input_output_aliases={3: 0},
        compiler_params=_params(),
    )(gstat, gstat, colf, dp1)


NT_DIMS = (((1,), (1,)), ((), ()))
TN_DIMS = (((0,), (0,)), ((), ()))


def _dot(a, b):
    return jnp.dot(a.astype(BF16), b.astype(BF16), preferred_element_type=F32)


def _dot_nt(a, b):
    return lax.dot_general(a.astype(BF16), b.astype(BF16), NT_DIMS, preferred_element_type=F32)


def _dot_tn(a, b):
    return lax.dot_general(a.astype(BF16), b.astype(BF16), TN_DIMS, preferred_element_type=F32)


def _chunk_gates(colf_ref, rowf_ref, m_prev, width=CHUNK):
    b = colf_ref[:, 0:1]
    rcol = colf_ref[:, 1:2]
    rrow = rowf_ref[0:1, 0:width]
    ri = lax.broadcasted_iota(jnp.int32, (CHUNK, width), 0)
    ci = lax.broadcasted_iota(jnp.int32, (CHUNK, width), 1)
    log_d = jnp.where(ri >= ci, b + rrow, NEG)
    m_row = jnp.maximum(b + m_prev, jnp.max(log_d, axis=-1, keepdims=True))
    dmat = jnp.exp(log_d - m_row)
    inter = jnp.exp(b + m_prev - m_row)
    b_last = b[CHUNK - 1:CHUNK, :]
    log_w = rcol + b_last
    m_new = jnp.maximum(b_last + m_prev, jnp.max(log_w, axis=0, keepdims=True))
    decay = jnp.exp(b_last + m_prev - m_new)
    w = jnp.exp(log_w - m_new)
    return m_row, dmat, inter, m_new, decay, w


def _mlstm_fwd(p1, colf, rowf, dk, dv, name):
    t = p1.shape[0]
    nh = N_HEADS
    nc = t // CHUNK
    hw = 2 * dk + dv
    cps = CHUNKS_PER_STEP if nc % CHUNKS_PER_STEP == 0 else 1
    rows_per_step = cps * CHUNK

    def body(p_ref, colf_ref, rowf_ref, hh_ref, stat_ref, cs_ref, ns_ref, *state):
        c_refs, n_refs, m_refs = state[:nh], state[nh:2 * nh], state[2 * nh:]
        c = pl.program_id(0)

        @pl.when(c == 0)
        def _():
            for ref in state:
                ref[...] = jnp.zeros_like(ref)

        for cc in range(cps):
            rows = pl.ds(cc * CHUNK, CHUNK)
            for h in range(nh):
                head(p_ref.at[rows, pl.ds(h * hw, hw)], colf_ref.at[h, rows], rowf_ref.at[h, cc],
                     hh_ref.at[rows, pl.ds(h * dv, dv)], stat_ref.at[h, rows], cs_ref.at[h, cc],
                     ns_ref.at[h, cc], c_refs[h], n_refs[h], m_refs[h])

    def head(p_ref, colf_ref, rowf_ref, hh_ref, stat_ref, cs_ref, ns_ref, c_ref, n_ref, m_ref):
        m_prev = m_ref[...]
        n_prev = n_ref[...]
        c_prev = c_ref[...]
        cs_ref[...] = c_prev.astype(BF16)
        sub = lax.broadcasted_iota(jnp.int32, (8, dk), 0)
        ns_ref[...] = jnp.where(sub == 0, n_prev, jnp.where(sub == 1, m_prev, 0.0))

        q = p_ref[:, 0:dk]
        k = p_ref[:, dk:2 * dk]
        v = p_ref[:, 2 * dk:2 * dk + dv]
        m_row, dmat, inter, m_new, decay, w = _chunk_gates(colf_ref, rowf_ref, m_prev)
        s = _dot_nt(q, k) * dmat
        num = _dot(s, v) + inter * _dot(q, c_prev)
        den = jnp.sum(s, axis=-1, keepdims=True) + inter * jnp.sum(q.astype(F32) * n_prev, axis=-1, keepdims=True)
        denom = jnp.maximum(jnp.abs(den), jnp.exp(-m_row))
        hh_ref[...] = num * (1.0 / denom)
        lane = lax.broadcasted_iota(jnp.int32, (CHUNK, LANES), 1)
        stat_ref[...] = jnp.where(lane == 0, den, 0.0)

        wk = w * k.astype(F32)
        c_ref[...] = decay * c_prev + _dot_tn(wk, v)
        n_ref[...] = decay * n_prev + jnp.sum(wk, axis=0, keepdims=True)
        m_ref[...] = m_new

    return pl.pallas_call(
        body, name=name, grid=(nc // cps,),
        in_specs=[pl.BlockSpec((rows_per_step, nh * hw), lambda c: (c, 0)),
                  pl.BlockSpec((nh, rows_per_step, LANES), lambda c: (0, c, 0)),
                  pl.BlockSpec((nh, cps, 8, LANES), lambda c: (0, c, 0, 0))],
        out_specs=[pl.BlockSpec((rows_per_step, nh * dv), lambda c: (c, 0)),
                   pl.BlockSpec((nh, rows_per_step, LANES), lambda c: (0, c, 0)),
                   pl.BlockSpec((nh, cps, dk, dv), lambda c: (0, c, 0, 0)),
                   pl.BlockSpec((nh, cps, 8, dk), lambda c: (0, c, 0, 0))],
        out_shape=[jax.ShapeDtypeStruct((t, nh * dv), F32), jax.ShapeDtypeStruct((nh, t, LANES), F32),
                   jax.ShapeDtypeStruct((nh, nc, dk, dv), BF16), jax.ShapeDtypeStruct((nh, nc, 8, dk), F32)],
        scratch_shapes=([pltpu.VMEM((dk, dv), F32)] * nh + [pltpu.VMEM((1, dk), F32)] * nh
                        + [pltpu.VMEM((1, 1), F32)] * nh),
        compiler_params=_params(),
    )(p1, colf, rowf)


def _head_gate_fwd(hh, p1, head_w, dv, name):
    t = hh.shape[0]
    nh = N_HEADS
    e = nh * dv
    tm = _pick(t, 416, 16)

    def body(hh_ref, oz_ref, w_ref, hm_ref):
        for h in range(nh):
            cols = slice(h * dv, (h + 1) * dv)
            x = hh_ref[:, cols]
            o = oz_ref[:, 2 * h * dv:(2 * h + 1) * dv].astype(F32)
            z = oz_ref[:, (2 * h + 1) * dv:(2 * h + 2) * dv].astype(F32)
            r = lax.rsqrt(jnp.mean(x * x, axis=-1, keepdims=True) + RMS_EPS)
            hn = (x * r) * w_ref[:, cols]
            hm_ref[:, cols] = (hn * _sigmoid(o) * (z * _sigmoid(z))).astype(BF16)

    return pl.pallas_call(
        body, name=name, grid=(t // tm,),
        in_specs=[pl.BlockSpec((tm, e), lambda i: (i, 0)), pl.BlockSpec((tm, 2 * e), lambda i: (i, 1)),
                  pl.BlockSpec((1, e), lambda i: (0, 0))],
        out_specs=pl.BlockSpec((tm, e), lambda i: (i, 0)),
        out_shape=jax.ShapeDtypeStruct((t, e), BF16),
        compiler_params=_params(),
    )(hh, p1, head_w)


def _mlstm_bwd(p1, colf, rowf, head_w, hh, stat, csave, nsave, dhm, n_cols, dk, dv, name, exchange=None):
    t = p1.shape[0]
    nh = N_HEADS
    nc = t // CHUNK
    hw = 2 * dk + dv
    cps = CHUNKS_PER_STEP if nc % CHUNKS_PER_STEP == 0 else 1
    rows_per_step = cps * CHUNK

    def body(p_ref, colf_ref, rowf_ref, w_ref, hh_ref, stat_ref, cs_ref, ns_ref, dhm_ref,
             dp_ref, gs_ref, dw_ref, *state):
        dc_refs, dn_refs, dwacc_refs = state[:nh], state[nh:2 * nh], state[2 * nh:]
        c = pl.program_id(0)

        @pl.when(c == 0)
        def _():
            for ref in state:
                ref[...] = jnp.zeros_like(ref)

        for cc in reversed(range(cps)):
            rows = pl.ds(cc * CHUNK, CHUNK)
            for h in range(nh):
                cols = pl.ds(h * dv, dv)
                oz = pl.ds(nh * hw + h * 2 * dv, 2 * dv)
                head(p_ref.at[rows, pl.ds(h * hw, hw)], p_ref.at[rows, oz], colf_ref.at[h, rows], rowf_ref.at[h, cc],
                     w_ref.at[:, cols], hh_ref.at[rows, cols], stat_ref.at[h, rows], cs_ref.at[h, cc],
                     ns_ref.at[h, cc], dhm_ref.at[rows, cols], dp_ref.at[rows, pl.ds(h * hw, hw)],
                     dp_ref.at[rows, oz], gs_ref.at[h, rows], dwacc_refs[h], dc_refs[h], dn_refs[h])

        @pl.when(c == nc // cps - 1)
        def _():
            for h in range(nh):
                dw_ref[h] = dwacc_refs[h][...]

    def head(p_ref, oz_ref, colf_ref, rowf_ref, w_ref, hh_ref, stat_ref, cs_ref, ns_ref, dhm_ref,
             dp_ref, doz_ref, gs_ref, dw_ref, dc_ref, dn_ref):
        q = p_ref[:, 0:dk]
        k = p_ref[:, dk:2 * dk]
        v = p_ref[:, 2 * dk:2 * dk + dv]
        o = oz_ref[:, 0:dv].astype(F32)
        z = oz_ref[:, dv:2 * dv].astype(F32)
        qf = q.astype(F32)
        kf = k.astype(F32)
        n_prev = ns_ref[0:1, :]
        m_prev = ns_ref[1:2, 0:1]
        c_prev = cs_ref[...]
        m_row, dmat, inter, m_new, decay, w = _chunk_gates(colf_ref, rowf_ref, m_prev)

        hh = hh_ref[...]
        dhm_v = dhm_ref[...].astype(F32)
        so = _sigmoid(o)
        sg = _sigmoid(z)
        sz = z * sg
        r = lax.rsqrt(jnp.mean(hh * hh, axis=-1, keepdims=True) + RMS_EPS)
        hn = (hh * r) * w_ref[...]
        dhn = dhm_v * so * sz
        doz_ref[:, 0:dv] = (dhm_v * hn * sz * (so * (1.0 - so))).astype(BF16)
        doz_ref[:, dv:2 * dv] = (dhm_v * hn * so * (sg * (1.0 + z * (1.0 - sg)))).astype(BF16)
        dw_ref[...] += jnp.broadcast_to(jnp.sum(dhn * (hh * r), axis=0, keepdims=True), dw_ref.shape)
        gwn = dhn * w_ref[...]
        dhh = r * gwn - hh * ((r * r * r) * jnp.mean(gwn * hh, axis=-1, keepdims=True))

        den = stat_ref[:, 0:1]
        floor = jnp.exp(-m_row)
        denom = jnp.maximum(jnp.abs(den), floor)
        inv = 1.0 / denom
        dnum = dhh * inv
        hdot = jnp.sum(dhh * hh, axis=-1, keepdims=True)
        dden = jnp.where(jnp.abs(den) > floor, -(hdot * inv) * jnp.sign(den), 0.0)
        s = _dot_nt(q, k) * dmat
        dqk = (_dot_nt(dnum, v) + dden) * dmat
        dc_new = dc_ref[...]
        dn_new = dn_ref[...]
        idd = inter * dden
        dq = _dot(dqk, k) + inter * _dot_nt(dnum, c_prev) + idd * n_prev
        dkv = _dot_tn(dqk, q) + w * (_dot_nt(v, dc_new) + dn_new)
        dvv = _dot_tn(s, dnum) + w * _dot(k, dc_new)
        dp_ref[:, 0:dk] = dq.astype(BF16)
        dp_ref[:, dk:2 * dk] = dkv.astype(BF16)
        dp_ref[:, 2 * dk:2 * dk + dv] = dvv.astype(BF16)
        qdq = jnp.sum(qf * dq, axis=-1, keepdims=True)
        kdk = jnp.sum(kf * dkv, axis=-1, keepdims=True)
        dc_prev = decay * dc_new + _dot_tn(inter * qf, dnum)
        dn_prev = decay * dn_new + jnp.sum(idd * qf, axis=0, keepdims=True)
        dc_ref[...] = dc_prev
        dn_ref[...] = dn_prev
        cross = (jnp.sum(jnp.sum(c_prev.astype(F32) * dc_prev, axis=-1, keepdims=True), axis=0, keepdims=True)
                 + jnp.sum(n_prev * dn_prev, axis=-1, keepdims=True))
        lane = lax.broadcasted_iota(jnp.int32, (CHUNK, LANES), 1)
        gs_ref[...] = jnp.where(lane == 0, qdq - kdk, jnp.where(lane == 1, kdk, jnp.where(lane == 2, cross, 0.0)))

    ns = nc // cps
    rc = lambda c: (ns - 1 - c, 0)
    rc3 = lambda c: (0, ns - 1 - c, 0)
    rc4 = lambda c: (0, ns - 1 - c, 0, 0)
    return _call(
        body, name=name, grid=(ns,),
        in_specs=[pl.BlockSpec((rows_per_step, nh * (hw + 2 * dv)), rc),
                  pl.BlockSpec((nh, rows_per_step, LANES), rc3),
                  pl.BlockSpec((nh, cps, 8, LANES), rc4),
                  pl.BlockSpec((1, nh * dv), lambda c: (0, 0)),
                  pl.BlockSpec((rows_per_step, nh * dv), rc),
                  pl.BlockSpec((nh, rows_per_step, LANES), rc3),
                  pl.BlockSpec((nh, cps, dk, dv), rc4),
                  pl.BlockSpec((nh, cps, 8, dk), rc4),
                  pl.BlockSpec((rows_per_step, nh * dv), rc)],
        out_specs=[pl.BlockSpec((rows_per_step, nh * (hw + 2 * dv)), rc),
                   pl.BlockSpec((nh, rows_per_step, LANES), rc3),
                   pl.BlockSpec((nh, 8, dv), lambda c: (0, 0, 0))],
        out_shape=[jax.ShapeDtypeStruct((t, n_cols), BF16), jax.ShapeDtypeStruct((nh, t, LANES), F32),
                   jax.ShapeDtypeStruct((nh, 8, dv), F32)],
        scratch_shapes=([pltpu.VMEM((dk, dv), F32)] * nh + [pltpu.VMEM((1, dk), F32)] * nh
                        + [pltpu.VMEM((8, dv), F32)] * nh),
        args=(p1, colf, rowf, head_w, hh, stat, csave, nsave, dhm), exchange=exchange)


def _position():
    return lax.axis_index("x"), lax.axis_index("y"), lax.axis_index("c")


class _Exchange:
    def __init__(self, ins, out_shapes, start, mid, finish):
        n = len(ins)
        self.ins, self.out_shapes = list(ins), list(out_shapes)
        self.start, self.mid, self.finish = start, mid, finish
        self.scratch = [pltpu.SemaphoreType.DMA((n, 7)), pltpu.SemaphoreType.DMA((n, 7)),
                        pltpu.SemaphoreType.DMA((n,))]


def _gather_exchange(shards):
    n = len(shards)

    def plan(ins, outs, sems):
        send_sems, recv_sems, local_sems = sems
        x, y, c = _position()
        me, sibling = (x, y, c), (x, y, 1 - c)
        chips = [(1 - x, y), (x, 1 - y), (1 - x, 1 - y)]

        def copy(a, k, block, to, src=None):
            px, py, pc = block
            dst = outs[a].at[4 * px + 2 * py + pc]
            return pltpu.make_async_remote_copy(
                src_ref=dst if src is None else src, dst_ref=dst,
                send_sem=send_sems.at[a, k], recv_sem=recv_sems.at[a, k],
                device_id=to, device_id_type=MESH)

        def mine():
            return [pltpu.make_async_copy(ins[a], outs[a].at[4 * x + 2 * y + c], local_sems.at[a])
                    for a in range(n)]

        def first():
            out = []
            for a in range(n):
                out.append(copy(a, 0, me, sibling, src=ins[a]))
                out += [copy(a, 1 + j, me, (*chip, c), src=ins[a]) for j, chip in enumerate(chips)]
            return out

        def ici_in():
            return [copy(a, 1 + j, (*chip, c), me) for j, chip in enumerate(chips) for a in range(n)]

        def passed():
            return [copy(a, 4 + j, (*chip, c), sibling) for j, chip in enumerate(chips) for a in range(n)]

        def d2d_in():
            return ([copy(a, 0, sibling, me) for a in range(n)]
                    + [copy(a, 4 + j, (*chip, 1 - c), me) for j, chip in enumerate(chips) for a in range(n)])

        return mine, first, ici_in, passed, d2d_in

    def start(ins, outs, sems):
        mine, first, _, _, _ = plan(ins, outs, sems)
        for cp in mine() + first():
            cp.start()

    def mid(ins, outs, sems):
        _, _, ici_in, passed, _ = plan(ins, outs, sems)
        for arrived, onward in zip(ici_in(), passed()):
            arrived.wait_recv()
            onward.start()

    def finish(ins, outs, sems):
        mine, first, _, passed, d2d_in = plan(ins, outs, sems)
        for cp in d2d_in():
            cp.wait_recv()
        for cp in first() + passed():
            cp.wait_send()
        for cp in mine():
            cp.wait()

    shapes = [jax.ShapeDtypeStruct((N_DEV,) + s.shape, s.dtype) for s in shards]
    return _Exchange(shards, shapes, start, mid, finish)


def _scatter_exchange(fulls):
    n = len(fulls)

    def plan(ins, outs, sems):
        send_sems, recv_sems, local_sems = sems
        x, y, c = _position()
        my_slot = 4 * x + 2 * y + c

        def mine():
            return [pltpu.make_async_copy(ins[a].at[my_slot], outs[a].at[my_slot], local_sems.at[a])
                    for a in range(n)]

        def remote(arriving):
            out = []
            for kk in (1, 2, 4, 6, 3, 5, 7):
                kx, ky, kc = (kk >> 2) & 1, (kk >> 1) & 1, kk & 1
                px = 1 - x if kx else x
                py = 1 - y if ky else y
                pc = 1 - c if kc else c
                peer_slot = 4 * px + 2 * py + pc
                for a in range(n):
                    out.append(pltpu.make_async_remote_copy(
                        src_ref=ins[a].at[peer_slot], dst_ref=outs[a].at[peer_slot if arriving else my_slot],
                        send_sem=send_sems.at[a, kk - 1], recv_sem=recv_sems.at[a, kk - 1],
                        device_id=(px, py, pc), device_id_type=MESH))
            return out

        return mine, remote

    def start(ins, outs, sems):
        mine, remote = plan(ins, outs, sems)
        for cp in mine() + remote(False):
            cp.start()

    def finish(ins, outs, sems):
        mine, remote = plan(ins, outs, sems)
        for cp in remote(True):
            cp.wait_recv()
        for cp in remote(False):
            cp.wait_send()
        for cp in mine():
            cp.wait()

    shapes = [jax.ShapeDtypeStruct(f.shape, f.dtype) for f in fulls]
    return _Exchange(fulls, shapes, start, None, finish)


def _run_exchange(ex, name):
    n_in, n_out = len(ex.ins), len(ex.out_shapes)

    def body(*refs):
        ins, outs, sems = refs[:n_in], refs[n_in:n_in + n_out], refs[n_in + n_out:]
        ex.start(ins, outs, sems)
        if ex.mid is not None:
            ex.mid(ins, outs, sems)
        ex.finish(ins, outs, sems)

    any_spec = pl.BlockSpec(memory_space=pl.ANY)
    return pl.pallas_call(
        body, name=name,
        in_specs=[any_spec] * n_in, out_specs=[any_spec] * n_out,
        out_shape=ex.out_shapes, scratch_shapes=ex.scratch,
    )(*ex.ins)


def _natural_to_head_major(n0, dk, dv):
    nh = N_HEADS
    qk, e, hw = nh * dk, nh * dv, 2 * dk + dv
    if n0 < qk:
        h, off = divmod(n0, dk)
        return h * hw + off, True
    n1 = n0 - qk
    if n1 < qk:
        h, off = divmod(n1, dk)
        return h * hw + dk + off, False
    part, n3 = divmod(n1 - qk, e)
    h, off = divmod(n3, dv)
    if part == 0:
        return h * hw + 2 * dk + off, False
    return nh * hw + h * 2 * dv + (part - 1) * dv + off, False


def _mlstm_weight_layout(w_blocks, dk, dv, name):
    _, d, blk = w_blocks.shape
    nh = N_HEADS
    hw = 2 * dk + 3 * dv
    n_tiles = nh * hw // LANES
    tiles_per_block = d // LANES
    tr = _pick(d, 128, 16)
    scale = dk ** -0.5

    def body(w_ref, p_ref, g_ref):
        lane = lax.broadcasted_iota(jnp.int32, (tr, LANES), 1)

        def tile(s, r):
            return w_ref[s, :, r * LANES:(r + 1) * LANES].astype(F32)

        def last_col(s):
            return w_ref[s, :, d:d + 1].astype(F32)

        for tn in range(n_tiles):
            s, r = divmod(tn, tiles_per_block)
            if s == 0:
                val = tile(0, r)
            elif r == 0:
                from_prev = pltpu.roll(tile(s - 1, tiles_per_block - 1), s - 1, 1)
                from_here = pltpu.roll(tile(s, 0), s, 1)
                val = jnp.where(lane < s - 1, from_prev, jnp.where(lane == s - 1, last_col(s - 1), from_here))
            else:
                slab = jnp.concatenate([tile(s, r - 1), tile(s, r)], axis=1)
                val = pltpu.roll(slab, s, 1)[:, LANES:]
            at, is_q = _natural_to_head_major(tn * LANES, dk, dv)
            if is_q:
                val = val * scale
            p_ref[:, at:at + LANES] = val.astype(BF16)
        n_gate = 2 * nh
        s = N_DEV - 1
        gates = pltpu.roll(tile(s, tiles_per_block - 1), n_gate - 1, 1)
        gates = jnp.where(lane < n_gate - 1, gates, jnp.where(lane == n_gate - 1, last_col(s), 0.0)).astype(BF16)
        g_ref[...] = gates
        p_ref[:, nh * hw:nh * hw + LANES] = gates

    return pl.pallas_call(
        body, name=name, grid=(d // tr,),
        in_specs=[pl.BlockSpec((N_DEV, tr, blk), lambda i: (0, i, 0))],
        out_specs=[pl.BlockSpec((tr, nh * hw + LANES), lambda i: (i, 0)), pl.BlockSpec((tr, LANES), lambda i: (i, 0))],
        out_shape=[jax.ShapeDtypeStruct((d, nh * hw + LANES), BF16), jax.ShapeDtypeStruct((d, LANES), BF16)],
        compiler_params=_params(),
    )(w_blocks)


def _mlstm_grad_layout(g_p1, row0, n_rows, dk, dv, name):
    d = g_p1.shape[0]
    nh = N_HEADS
    hw = 2 * dk + 3 * dv
    n_tiles = nh * hw // LANES
    tiles_per_block = d // LANES
    tr = _pick(d, 128, 16)
    scale = dk ** -0.5

    def body(p_ref, o_ref):
        def natural(tn):
            if tn == n_tiles:
                return p_ref[:, nh * hw:nh * hw + LANES].astype(F32)
            at, is_q = _natural_to_head_major(tn * LANES, dk, dv)
            val = p_ref[:, at:at + LANES].astype(F32)
            return val * scale if is_q else val

        for s in range(N_DEV):
            for r in range(tiles_per_block):
                tn = s * tiles_per_block + r
                if s == 0:
                    val = natural(tn)
                else:
                    slab = jnp.concatenate([natural(tn), natural(tn + 1)], axis=1)
                    val = pltpu.roll(slab, 2 * LANES - s, 1)[:, :LANES]
                o_ref[s, :, r * LANES:(r + 1) * LANES] = val.astype(BF16)
            o_ref[s, :, d:d + 1] = natural((s + 1) * tiles_per_block)[:, s:s + 1].astype(BF16)

    first = row0 // tr
    return pl.pallas_call(
        body, name=name, grid=(n_rows // tr,),
        in_specs=[pl.BlockSpec((tr, nh * hw + LANES), lambda i: (i + first, 0))],
        out_specs=pl.BlockSpec((N_DEV, tr, d + 1), lambda i: (0, i, 0)),
        out_shape=jax.ShapeDtypeStruct((N_DEV, n_rows, d + 1), BF16),
        compiler_params=_params(),
    )(g_p1)


def _adamw_math(w, g, m, v):
    m = ADAM_B1 * m + (1.0 - ADAM_B1) * g
    v = ADAM_B2 * v + (1.0 - ADAM_B2) * (g * g)
    m_hat = m / (1.0 - ADAM_B1 ** ADAM_STEP)
    v_hat = v / (1.0 - ADAM_B2 ** ADAM_STEP)
    delta = -ADAM_LR * (m_hat / (jnp.sqrt(v_hat) + ADAM_EPS) + ADAM_WD * w)
    return delta, m, v


def _adamw_sharded(parts, w, m, v, name):
    n_parts = len(parts)
    _, rk, c = parts[0].shape
    r = n_parts * rk
    tr = _pick(rk, 128, 8)
    per = rk // tr

    def body(*refs):
        p_refs = refs[:n_parts]
        w_ref, m_ref, v_ref, g_ref, d_ref, nm_ref, nv_ref, gsum_ref = refs[n_parts:]
        i = pl.program_id(0)
        for k, p_ref in enumerate(p_refs):
            @pl.when(i // per == k)
            def _(p_ref=p_ref):
                g = p_ref[0].astype(F32)
                for s in range(1, N_DEV):
                    g = g + p_ref[s].astype(F32)
                gsum_ref[...] = g
        g = gsum_ref[...]
        delta, m_new, v_new = _adamw_math(w_ref[...], g, m_ref[...], v_ref[...])
        g_ref[...] = g
        d_ref[...] = delta
        nm_ref[...] = m_new
        nv_ref[...] = v_new

    def part_spec(k):
        return pl.BlockSpec((N_DEV, tr, c), lambda i: (0, jnp.clip(i - k * per, 0, per - 1), 0))

    blk = pl.BlockSpec((tr, c), lambda i: (i, 0))
    return pl.pallas_call(
        body, name=name, grid=(r // tr,),
        in_specs=[part_spec(k) for k in range(n_parts)] + [blk, blk, blk],
        out_specs=[blk] * 4,
        out_shape=[jax.ShapeDtypeStruct((r, c), F32)] * 4,
        scratch_shapes=[pltpu.VMEM((tr, c), F32)],
        compiler_params=_params(),
    )(*parts, w, m, v)


def _sum_devices(parts, name):
    _, r, c = parts.shape

    def body(p_ref, o_ref):
        g = p_ref[0]
        for s in range(1, N_DEV):
            g = g + p_ref[s]
        o_ref[...] = g

    return pl.pallas_call(
        body, name=name, out_shape=jax.ShapeDtypeStruct((r, c), F32), compiler_params=_params(),
    )(parts)


def _adamw_small(gs, ws, ms, vs, name):
    n = len(gs)

    def body(*refs):
        g_refs, w_refs, m_refs, v_refs = refs[:n], refs[n:2 * n], refs[2 * n:3 * n], refs[3 * n:4 * n]
        d_refs, nm_refs, nv_refs = refs[4 * n:5 * n], refs[5 * n:6 * n], refs[6 * n:7 * n]
        for a in range(n):
            delta, m_new, v_new = _adamw_math(w_refs[a][...], g_refs[a][...], m_refs[a][...], v_refs[a][...])
            d_refs[a][...] = delta
            nm_refs[a][...] = m_new
            nv_refs[a][...] = v_new

    shapes = [jax.ShapeDtypeStruct(w.shape, F32) for w in ws]
    outs = pl.pallas_call(
        body, name=name, out_shape=shapes * 3, compiler_params=_params(),
    )(*gs, *ws, *ms, *vs)
    return outs[:n], outs[n:2 * n], outs[2 * n:]


def _pad_rows(a, rows):
    return jnp.pad(a, ((0, rows - a.shape[0]), (0, 0)))


def kernel(x, meta_tokens, norm_w, conv_in_w, conv_w, conv_out_w, mlstm_in_w, mlstm_gate_b, mlstm_head_norm_w, mlstm_out_w, final_norm_w, loss_target, m_meta_tokens, m_norm_w, m_conv_in_w, m_conv_w, m_conv_out_w, m_mlstm_in_w, m_mlstm_gate_b, m_mlstm_head_norm_w, m_mlstm_out_w, m_final_norm_w, v_meta_tokens, v_norm_w, v_conv_in_w, v_conv_w, v_conv_out_w, v_mlstm_in_w, v_mlstm_gate_b, v_mlstm_head_norm_w, v_mlstm_out_w, v_final_norm_w):
    seq, d = x.shape[1], x.shape[2]
    t = seq + LEAD
    e = 2 * d
    ec = e // 2
    nh = N_HEADS
    dv = e // nh
    dk = dv // 2
    qk = nh * dk
    hw = 2 * dk + 3 * dv
    n_in = 2 * qk + 3 * e + 2 * nh
    n_in_s = n_in // N_DEV
    me = 4 * lax.axis_index("x") + 2 * lax.axis_index("y") + lax.axis_index("c")
    tm = _pick(t, 832, 16)
    tm_in = _pick(t, 1664, 16)
    tkw = _pick(t, 2080, 16)

    small = jnp.concatenate([
        meta_tokens,
        _pad_rows(conv_w[0].reshape(3 * (e // N_DEV) // LANES, LANES), 8),
        _pad_rows(mlstm_head_norm_w[0].reshape((e // N_DEV) // LANES, LANES), 8),
    ], axis=0) if d // N_DEV == LANES else None
    assert small is not None, "the packed small-weight block assumes d_model / 8 == 128"
    (small_g,) = _run_exchange(_gather_exchange([small]), "gather_small_weights")
    meta_full = jnp.transpose(small_g[:, 0:N_META, :], (1, 0, 2)).reshape(N_META, d)
    cw_rows = 3 * (e // N_DEV) // LANES
    conv_w_full = jnp.transpose(
        small_g[:, N_META:N_META + cw_rows, :].reshape(N_DEV, 3, e // N_DEV), (1, 0, 2)).reshape(3, e)
    hn_rows = (e // N_DEV) // LANES
    head_w_full = small_g[:, N_META + 8:N_META + 8 + hn_rows, :].reshape(1, e)

    tgt = loss_target[0]

    ci_map = lambda blk: 2 * (blk % 4) + blk // 4
    half = ec // 2
    w_ci_own = conv_in_w[0].astype(BF16)
    (h0, u0), (w_ci_a,) = _rms_first(x[0], meta_full, norm_w[0:1], "rms0",
                                     exchange=_gather_exchange([w_ci_own[:, :half]]))
    half_spec = pl.BlockSpec((None, d, half), lambda i, j, k: (ci_map(j), 0, 0))
    p0, (w_ci_b, w_co) = _matmul(
        u0, w_ci_a, form="nn", m=t, n=8 * half, kdim=d, tm=tm_in, tn=half, tk=d, out_dtype=BF16, name="conv_in_a",
        b_spec=half_spec, out_shape=(t, 8 * ec), out_spec=pl.BlockSpec((tm_in, half), lambda i, j, k: (i, 2 * j)),
        exchange=_gather_exchange([w_ci_own[:, half:], conv_out_w[0].astype(BF16)]))
    p0, (w_mi,) = _matmul(
        u0, w_ci_b, form="nn", m=t, n=8 * half, kdim=d, tm=tm_in, tn=half, tk=d, out_dtype=BF16, name="conv_in_b",
        b_spec=half_spec, out_shape=(t, 8 * ec), out_spec=pl.BlockSpec((tm_in, half), lambda i, j, k: (i, 2 * j + 1)),
        fill=p0, exchange=_gather_exchange([mlstm_in_w[0].astype(BF16)]))
    w_ci = jnp.concatenate([w_ci_a, w_ci_b], axis=2)
    w_co = w_co.reshape(e, d)
    (g0, y0), (w_mo,) = _conv_gate_fwd(p0, conv_w_full, "conv_gate",
                                       exchange=_gather_exchange([mlstm_out_w[0].astype(BF16)]))
    w_mo = w_mo.reshape(e, d)
    h1 = _matmul(g0, w_co, form="nn", m=t, n=d, kdim=e, tm=tm, tn=d, tk=e, out_dtype=F32, name="conv_out",
                 residual=h0)

    assert n_in_s == d + 1 and w_mi.shape == (N_DEV, d, d + 1)
    w_p1, w_gate = _mlstm_weight_layout(w_mi, dk, dv, "mlstm_w_layout")
    w_gate_t = jnp.transpose(w_gate[:, 0:16])
    bias_row = jnp.pad(mlstm_gate_b, ((0, 0), (0, LANES - 2 * nh)))
    bias_col = jnp.pad(mlstm_gate_b.T, ((0, 16 - 2 * nh), (0, 0)))

    u1 = _rms_fwd(h1, norm_w[1:2], "rms1")
    p1 = _matmul(u1, w_p1, form="nn", m=t, n=nh * hw, kdim=d, tm=tm_in, tn=_pick(nh * hw, 2048, LANES), tk=d,
                 out_dtype=BF16, name="mlstm_in")
    graw = _matmul(u1, w_gate, form="nn", m=t, n=LANES, kdim=d, tm=tm, tn=LANES, tk=d, out_dtype=F32,
                   name="gates_col")
    graw_t = _matmul(w_gate_t, u1, form="nt", m=16, n=t, kdim=d, tm=16, tn=_pick(t, 1664, LANES), tk=d,
                     out_dtype=F32, name="gates_row")
    colf, rowf = _gates_fwd(graw, graw_t, bias_row, bias_col, "gates_fwd")
    hh, stat, csave, nsave = _mlstm_fwd(p1, colf, rowf, dk, dv, "mlstm_fwd")
    hm = _head_gate_fwd(hh, p1, head_w_full, dv, "head_gate")
    h2 =_matmul(hm, w_mo, form="nn", m=t, n=d, kdim=e, tm=tm, tn=d, tk=e, out_dtype=F32, name="mlstm_out",
                 residual=h1)

    dh2, dwf, loss_part = _final_loss(h2, final_norm_w.reshape(1, d), tgt, "final_loss")

    dhm = _matmul(dh2, w_mo, form="nt", m=t, n=e, kdim=d, tm=tm, tn=_pick(e, 2048, LANES), tk=d, out_dtype=BF16,
                  name="mlstm_out_dx")
    g_mo = _matmul(hm, dh2, form="tn", m=e, n=d, kdim=t, tm=_pick(e, 1024, LANES), tn=d, tk=tkw, out_dtype=BF16,
                   name="mlstm_out_dw")
    n_p1 = nh * hw + LANES
    (dp1, gstat, dhead), (r_mo,) = _mlstm_bwd(
        p1, colf, rowf, head_w_full, hh, stat, csave, nsave, dhm, n_p1, dk, dv, "mlstm_bwd",
        exchange=_scatter_exchange([g_mo.reshape(N_DEV, e // N_DEV, d)]))
    dp1, dbias = _gates_bwd(gstat, colf, dp1, "gates_bwd")
    du1 = _matmul(dp1, w_p1, form="nt", m=t, n=d, kdim=n_p1, tm=tm_in, tn=d, tk=_pick(n_p1, 2048, LANES),
                  out_dtype=F32, name="mlstm_in_dx")
    g_p1 = _matmul(u1, dp1, form="tn", m=d, n=n_p1, kdim=t, tm=d, tn=_pick(n_p1, 2048, LANES), tk=tkw,
                   out_dtype=BF16, name="mlstm_in_dw")
    dh1, dnw1 = _rms_bwd(h1, norm_w[1:2], du1, dh2, "rms1_bwd")

    g_mi_a = _mlstm_grad_layout(g_p1, 0, d // 2, dk, dv, "mlstm_g_layout_a")
    g_mi_b = _mlstm_grad_layout(g_p1, d // 2, d // 2, dk, dv, "mlstm_g_layout_b")

    dg0 =_matmul(dh1, w_co, form="nt", m=t, n=e, kdim=d, tm=tm, tn=_pick(e, 2048, LANES), tk=d, out_dtype=BF16,
                  name="conv_out_dx")
    g_co = _matmul(g0, dh1, form="tn", m=e, n=d, kdim=t, tm=_pick(e, 1024, LANES), tn=d, tk=tkw, out_dtype=BF16,
                   name="conv_out_dw")
    (dp0, dconv), (r_mi_a,) = _conv_gate_bwd(
        p0, y0, dg0, conv_w_full, "conv_gate_bwd", exchange=_scatter_exchange([g_mi_a]))
    g_ci, (r_mi_b, r_co) = _matmul(
        u0, dp0, form="tn", m=d, n=8 * ec, kdim=t, tm=d, tn=ec, tk=tkw, out_dtype=BF16, name="conv_in_dw",
        out_shape=(N_DEV, d, ec), out_spec=pl.BlockSpec((None, d, ec), lambda i, j, k: (ci_map(j), 0, 0)),
        exchange=_scatter_exchange([g_mi_b, g_co.reshape(N_DEV, e // N_DEV, d)]))
    du0, (r_ci,) = _matmul(
        dp0, w_ci, form="nt", m=t, n=d, kdim=8 * ec, tm=tm_in, tn=d, tk=ec, out_dtype=F32, name="conv_in_dx",
        b_spec=pl.BlockSpec((None, d, ec), lambda i, j, k: (ci_map(k), 0, 0)),
        exchange=_scatter_exchange([g_ci]))
    grad_x, dmeta, dnw0 = _rms_bwd_first(h0, norm_w[0:1], du0, dh1, "rms0_bwd")
    grad_x = grad_x[None]

    row8 = lax.broadcasted_iota(jnp.int32, (8, 1), 0)
    loss_wide = jnp.pad(loss_part, ((0, 0), (0, d - LANES)))
    payload = jnp.concatenate([
        jnp.where(row8 == 0, dnw0, jnp.where(row8 == 1, dnw1, 0.0)),
        jnp.where(row8 == 0, dwf, jnp.where(row8 == 1, loss_wide, 0.0)),
        jnp.where(row8 == 0, jnp.pad(dbias, ((0, 0), (0, d - LANES))), 0.0),
        dmeta,
        _pad_rows(dconv[0:3].reshape(3 * e // d, d), 8),
        _pad_rows(dhead[:, 0, :].reshape(e // d, d), 8),
    ], axis=0)
    (payload_g,) = _run_exchange(_gather_exchange([payload]), "gather_small_grads")
    tot = _sum_devices(payload_g, "sum_small_grads")

    loss = tot[9, 0]
    g_norm = tot[0:2]
    g_final = tot[8]
    g_gate_b = tot[16:17, 0:2 * nh]
    g_meta = lax.dynamic_slice(tot[24:24 + N_META], (0, me * (d // N_DEV)), (N_META, d // N_DEV))
    g_conv_w = lax.dynamic_slice(tot[40:40 + 3 * e // d].reshape(3, e), (0, me * (e // N_DEV)), (3, e // N_DEV))
    g_head = lax.dynamic_slice(tot[48:48 + e // d].reshape(1, e), (0, me * (e // N_DEV)), (1, e // N_DEV))

    g1, d1, nm1, nv1 = _adamw_sharded([r_ci], conv_in_w[0], m_conv_in_w[0], v_conv_in_w[0], "adamw_conv_in")
    g2, d2, nm2, nv2 = _adamw_sharded([r_co], conv_out_w[0], m_conv_out_w[0], v_conv_out_w[0], "adamw_conv_out")
    g3, d3, nm3, nv3 = _adamw_sharded([r_mi_a, r_mi_b], mlstm_in_w[0], m_mlstm_in_w[0], v_mlstm_in_w[0],
                                      "adamw_mlstm_in")
    g4, d4, nm4, nv4 = _adamw_sharded([r_mo], mlstm_out_w[0], m_mlstm_out_w[0], v_mlstm_out_w[0],
                                      "adamw_mlstm_out")

    small_g = [g_meta, g_norm, g_conv_w, g_gate_b, g_head, g_final.reshape(1, d)]
    small_w = [meta_tokens, norm_w, conv_w[0], mlstm_gate_b, mlstm_head_norm_w, final_norm_w.reshape(1, d)]
    small_m = [m_meta_tokens, m_norm_w, m_conv_w[0], m_mlstm_gate_b, m_mlstm_head_norm_w, m_final_norm_w.reshape(1, d)]
    small_v = [v_meta_tokens, v_norm_w, v_conv_w[0], v_mlstm_gate_b, v_mlstm_head_norm_w, v_final_norm_w.reshape(1, d)]
    sd, snm, snv = _adamw_small(small_g, small_w, small_m, small_v, "adamw_small")

    def order(meta, norm, cin, cw, cout, min_, gb, hn, mout, fin):
        return (meta, norm, cin[None], cw[None], cout[None], min_[None], gb, hn, mout[None], fin.reshape(d))

    grads = order(g_meta, g_norm, g1, g_conv_w, g2, g3, g_gate_b, g_head, g4, g_final)
    deltas = order(sd[0], sd[1], d1, sd[2], d2, d3, sd[3], sd[4], d4, sd[5])
    new_m = order(snm[0], snm[1], nm1, snm[2], nm2, nm3, snm[3], snm[4], nm4, snm[5])
    new_v = order(snv[0], snv[1], nv1, snv[2], nv2, nv3, snv[3], snv[4], nv4, snv[5])
    return (loss, grad_x, *grads, *deltas, *new_m, *new_v)
```

```python
import functools

import jax
import jax.numpy as jnp
from jax import lax
from jax.experimental import pallas as pl
from jax.experimental.pallas import tpu as pltpu

F32 = jnp.float32
BF16 = jnp.bfloat16
MESH = pl.DeviceIdType.MESH

N_DEV = 8
N_META = 16
N_HEADS = 4
CHUNK = 64
CHUNKS_PER_STEP = 2
LEAD = 128
PAD_ROWS = LEAD - N_META
RMS_EPS = 1e-6
NEG = -1e30
LANES = 128
VMEM_LIMIT = 48 * 1024 * 1024

ADAM_LR = 0.001
ADAM_B1 = 0.9
ADAM_B2 = 0.999
ADAM_EPS = 1e-08
ADAM_WD = 0.01
ADAM_STEP = 10

HIGHEST = lax.Precision.HIGHEST


def _pick(n, target, mult):
    best = None
    for d in range(mult, min(n, target) + 1, mult):
        if n % d == 0:
            best = d
    return n if best is None else best


def _params(**kw):
    return pltpu.CompilerParams(vmem_limit_bytes=VMEM_LIMIT, **kw)


def _sigmoid(x):
    return 0.5 * jnp.tanh(0.5 * x) + 0.5


STRIP_ROWS = 16
STRIP_COLS = 256


def _strips(n_rows, n_cols):
    for c0 in range(0, n_cols, STRIP_COLS):
        for r0 in range(0, n_rows, STRIP_ROWS):
            yield slice(r0, r0 + STRIP_ROWS), slice(c0, min(c0 + STRIP_COLS, n_cols))


def _shift(sl, by):
    return slice(sl.start + by, sl.stop + by)


def _call(body, *, name, grid, in_specs, out_specs, out_shape, args, scratch_shapes=(), aliases=None,
          exchange=None):
    aliases = {} if aliases is None else aliases
    if exchange is None:
        return pl.pallas_call(
            body, name=name, grid=grid, in_specs=list(in_specs), out_specs=out_specs, out_shape=out_shape,
            scratch_shapes=list(scratch_shapes), input_output_aliases=aliases,
            compiler_params=_params())(*args)
    ex = exchange
    single = not isinstance(out_shape, (list, tuple))
    shapes = [out_shape] if single else list(out_shape)
    specs = [out_specs] if single else list(out_specs)
    n_in, n_out, n_scr = len(in_specs), len(shapes), len(scratch_shapes)
    n_ex_in, n_ex_out = len(ex.ins), len(ex.out_shapes)
    steps = 1
    for size in grid:
        steps *= size

    def wrapped(*refs):
        own_in, ex_in = refs[:n_in], refs[n_in:n_in + n_ex_in]
        at = n_in + n_ex_in
        own_out, ex_out = refs[at:at + n_out], refs[at + n_out:at + n_out + n_ex_out]
        at += n_out + n_ex_out
        own_scr, ex_scr = refs[at:at + n_scr], refs[at + n_scr:]
        step = 0
        for axis, size in enumerate(grid):
            step = step * size + pl.program_id(axis)

        @pl.when(step == 0)
        def _():
            ex.start(ex_in, ex_out, ex_scr)

        if ex.mid is not None:
            @pl.when(step == (3 * steps) // 4)
            def _():
                ex.mid(ex_in, ex_out, ex_scr)

        body(*own_in, *own_out, *own_scr)

        @pl.when(step == steps - 1)
        def _():
            ex.finish(ex_in, ex_out, ex_scr)

    any_spec = pl.BlockSpec(memory_space=pl.ANY)
    res = pl.pallas_call(
        wrapped, name=name, grid=grid,
        in_specs=list(in_specs) + [any_spec] * n_ex_in, out_specs=specs + [any_spec] * n_ex_out,
        out_shape=shapes + ex.out_shapes, scratch_shapes=list(scratch_shapes) + ex.scratch,
        input_output_aliases=aliases, compiler_params=_params())(*args, *ex.ins)
    return (res[0] if single else res[:n_out]), res[n_out:]


def _matmul(a, b, *, form, m, n, kdim, tm, tn, tk, out_dtype, name,
            a_spec=None, b_spec=None, out_spec=None, out_shape=None, residual=None, fill=None, exchange=None):
    ni, nj, nk = m // tm, n // tn, kdim // tk
    assert ni * tm == m and nj * tn == n and nk * tk == kdim, (name, m, n, kdim, tm, tn, tk)
    if form == "nn":
        dn = (((1,), (0,)), ((), ()))
        a_def = pl.BlockSpec((tm, tk), lambda i, j, k: (i, k))
        b_def = pl.BlockSpec((tk, tn), lambda i, j, k: (k, j))
    elif form == "nt":
        dn = (((1,), (1,)), ((), ()))
        a_def = pl.BlockSpec((tm, tk), lambda i, j, k: (i, k))
        b_def = pl.BlockSpec((tn, tk), lambda i, j, k: (j, k))
    else:
        dn = (((0,), (0,)), ((), ()))
        a_def = pl.BlockSpec((tk, tm), lambda i, j, k: (k, i))
        b_def = pl.BlockSpec((tk, tn), lambda i, j, k: (k, j))
    a_spec = a_def if a_spec is None else a_spec
    b_spec = b_def if b_spec is None else b_spec
    o_spec = pl.BlockSpec((tm, tn), lambda i, j, k: (i, j)) if out_spec is None else out_spec
    has_res = residual is not None
    has_fill = fill is not None

    def body(*refs):
        a_ref, b_ref = refs[:2]
        r_ref = refs[2] if has_res else None
        o_ref = refs[2 + has_res + has_fill]

        def product():
            return lax.dot_general(a_ref[...].astype(BF16), b_ref[...].astype(BF16), dn,
                                   preferred_element_type=F32)

        def finish(acc):
            if has_res:
                acc = acc + r_ref[...].astype(F32)
            o_ref[...] = acc.astype(o_ref.dtype)

        if nk == 1:
            finish(product())
        else:
            acc_ref = refs[3 + has_res + has_fill]
            k = pl.program_id(2)

            @pl.when(k == 0)
            def _():
                acc_ref[...] = jnp.zeros_like(acc_ref)

            acc_ref[...] += product()

            @pl.when(k == nk - 1)
            def _():
                finish(acc_ref[...])

    in_specs = [a_spec, b_spec]
    args = [a, b]
    if has_res:
        in_specs.append(pl.BlockSpec((tm, tn), lambda i, j, k: (i, j)))
        args.append(residual)
    aliases = None
    if has_fill:
        in_specs.append(pl.BlockSpec(memory_space=pl.ANY))
        args.append(fill)
        aliases = {len(args) - 1: 0}
    return _call(
        body, name=name, grid=(ni, nj, nk), in_specs=in_specs, out_specs=o_spec,
        out_shape=jax.ShapeDtypeStruct((m, n) if out_shape is None else out_shape, out_dtype),
        scratch_shapes=[] if nk == 1 else [pltpu.VMEM((tm, tn), F32)], args=args, aliases=aliases,
        exchange=exchange)


def _rms_fwd(h, w, name, exchange=None):
    t, d = h.shape
    tm = _pick(t, 832, 16)

    def body(h_ref, w_ref, u_ref):
        x = h_ref[...]
        r = lax.rsqrt(jnp.mean(x * x, axis=-1, keepdims=True) + RMS_EPS)
        u_ref[...] = ((x * r) * w_ref[...]).astype(BF16)

    return _call(
        body, name=name, grid=(t // tm,),
        in_specs=[pl.BlockSpec((tm, d), lambda i: (i, 0)), pl.BlockSpec((1, d), lambda i: (0, 0))],
        out_specs=pl.BlockSpec((tm, d), lambda i: (i, 0)),
        out_shape=jax.ShapeDtypeStruct((t, d), BF16), args=(h, w), exchange=exchange)


def _rms_first(x, meta, w, name, exchange=None):
    seq, d = x.shape
    t = seq + LEAD
    tm = LEAD

    def body(x_ref, meta_ref, w_ref, h_ref, u_ref):
        i = pl.program_id(0)

        @pl.when(i == 0)
        def _():
            h_ref[0:PAD_ROWS, :] = jnp.zeros((PAD_ROWS, d), F32)
            h_ref[PAD_ROWS:LEAD, :] = meta_ref[...]

        @pl.when(i > 0)
        def _():
            h_ref[...] = x_ref[...]

        hv = h_ref[...]
        r = lax.rsqrt(jnp.mean(hv * hv, axis=-1, keepdims=True) + RMS_EPS)
        u_ref[...] = ((hv * r) * w_ref[...]).astype(BF16)

    row = pl.BlockSpec((tm, d), lambda i: (i, 0))
    return _call(
        body, name=name, grid=(t // tm,),
        in_specs=[pl.BlockSpec((tm, d), lambda i: (jnp.maximum(i - 1, 0), 0)),
                  pl.BlockSpec((N_META, d), lambda i: (0, 0)), pl.BlockSpec((1, d), lambda i: (0, 0))],
        out_specs=[row, row],
        out_shape=[jax.ShapeDtypeStruct((t, d), F32), jax.ShapeDtypeStruct((t, d), BF16)],
        args=(x, meta, w), exchange=exchange)


def _rms_bwd(h, w, du, dres, name):
    t, d = h.shape
    tm = _pick(t, 832, 16)

    def body(h_ref, w_ref, du_ref, dres_ref, dh_ref, dw_ref):
        i = pl.program_id(0)
        x = h_ref[...]
        g = du_ref[...].astype(F32)
        r = lax.rsqrt(jnp.mean(x * x, axis=-1, keepdims=True) + RMS_EPS)
        gw = g * w_ref[...]
        dot = jnp.mean(gw * x, axis=-1, keepdims=True)
        dh_ref[...] = dres_ref[...] + (r * gw - x * ((r * r * r) * dot))
        part = jnp.sum(g * (x * r), axis=0, keepdims=True)

        @pl.when(i == 0)
        def _():
            dw_ref[...] = jnp.zeros_like(dw_ref)

        dw_ref[...] += jnp.broadcast_to(part, dw_ref.shape)

    row = pl.BlockSpec((tm, d), lambda i: (i, 0))
    return pl.pallas_call(
        body, name=name, grid=(t // tm,),
        in_specs=[row, pl.BlockSpec((1, d), lambda i: (0, 0)), row, row],
        out_specs=[row, pl.BlockSpec((8, d), lambda i: (0, 0))],
        out_shape=[jax.ShapeDtypeStruct((t, d), F32), jax.ShapeDtypeStruct((8, d), F32)],
        compiler_params=_params(),
    )(h, w, du, dres)


def _rms_bwd_first(h, w, du, dres, name):
    t, d = h.shape
    sub = LEAD
    per = _pick((t - LEAD) // sub, 8, 1)
    tmo = per * sub
    n_out = (t - LEAD) // tmo
    n_blocks = t // sub

    def body(*refs):
        w_ref = refs[0]
        h_refs, du_refs, dres_refs = refs[1:1 + per], refs[1 + per:1 + 2 * per], refs[1 + 2 * per:1 + 3 * per]
        gx_ref, dmeta_ref, dw_ref = refs[1 + 3 * per:]
        i = pl.program_id(0)

        def piece(q):
            x = h_refs[q][...]
            g = du_refs[q][...].astype(F32)
            r = lax.rsqrt(jnp.mean(x * x, axis=-1, keepdims=True) + RMS_EPS)
            gw = g * w_ref[...]
            dot = jnp.mean(gw * x, axis=-1, keepdims=True)
            dh = dres_refs[q][...] + (r * gw - x * ((r * r * r) * dot))
            return dh, jnp.sum(g * (x * r), axis=0, keepdims=True)

        @pl.when(i == 0)
        def _():
            dh, part = piece(0)
            dmeta_ref[...] = dh[PAD_ROWS:LEAD, :]
            dw_ref[...] = jnp.broadcast_to(part, dw_ref.shape)

        @pl.when(i > 0)
        def _():
            for q in range(per):
                dh, part = piece(q)
                gx_ref[q * sub:(q + 1) * sub, :] = dh
                dw_ref[...] += jnp.broadcast_to(part, dw_ref.shape)

    def piece_spec(q):
        return pl.BlockSpec((sub, d), lambda i: (jnp.clip(1 + per * (i - 1) + q, 0, n_blocks - 1), 0))

    pieces = [piece_spec(q) for q in range(per)]
    return pl.pallas_call(
        body, name=name, grid=(n_out + 1,),
        in_specs=[pl.BlockSpec((1, d), lambda i: (0, 0))] + pieces * 3,
        out_specs=[pl.BlockSpec((tmo, d), lambda i: (jnp.maximum(i - 1, 0), 0)),
                   pl.BlockSpec((N_META, d), lambda i: (0, 0)), pl.BlockSpec((8, d), lambda i: (0, 0))],
        out_shape=[jax.ShapeDtypeStruct((t - LEAD, d), F32), jax.ShapeDtypeStruct((N_META, d), F32),
                   jax.ShapeDtypeStruct((8, d), F32)],
        compiler_params=_params(),
    )(w, *([h] * per), *([du] * per), *([dres] * per))


def _final_loss(h, w, tgt, name):
    t, d = h.shape
    sub = LEAD
    per = _pick(t // sub, 5, 1)
    tm = per * sub

    def body(h_ref, w_ref, *rest):
        t_refs, (dh_ref, dw_ref, loss_ref) = rest[:per], rest[per:]
        i = pl.program_id(0)

        @pl.when(i == 0)
        def _():
            dw_ref[...] = jnp.zeros_like(dw_ref)
            loss_ref[...] = jnp.zeros_like(loss_ref)

        for q in range(per):
            sl = slice(q * sub, (q + 1) * sub)
            x = h_ref[sl, :]
            r = lax.rsqrt(jnp.mean(x * x, axis=-1, keepdims=True) + RMS_EPS)
            yn = x * r
            y = yn * w_ref[...]
            rows = i * tm + q * sub + lax.broadcasted_iota(jnp.int32, (sub, 1), 0)
            diff = jnp.where(rows >= LEAD, y - t_refs[q][...], 0.0)
            tile_loss = 0.5 * jnp.sum(jnp.mean(diff * diff, axis=-1, keepdims=True), axis=0, keepdims=True)
            dy = diff / d
            gw = dy * w_ref[...]
            dot = jnp.mean(gw * x, axis=-1, keepdims=True)
            dh_ref[sl, :] = r * gw - x * ((r * r * r) * dot)
            dw_ref[...] += jnp.broadcast_to(jnp.sum(dy * yn, axis=0, keepdims=True), dw_ref.shape)
            loss_ref[...] += jnp.broadcast_to(tile_loss, loss_ref.shape)

    row = pl.BlockSpec((tm, d), lambda i: (i, 0))
    piece = [pl.BlockSpec((sub, d), functools.partial(lambda i, q: (jnp.maximum(i * per + q - 1, 0), 0), q=q))
             for q in range(per)]
    return pl.pallas_call(
        body, name=name, grid=(t // tm,),
        in_specs=[row, pl.BlockSpec((1, d), lambda i: (0, 0))] + piece,
        out_specs=[row, pl.BlockSpec((8, d), lambda i: (0, 0)), pl.BlockSpec((8, LANES), lambda i: (0, 0))],
        out_shape=[jax.ShapeDtypeStruct((t, d), F32), jax.ShapeDtypeStruct((8, d), F32),
                   jax.ShapeDtypeStruct((8, LANES), F32)],
        compiler_params=_params(),
    )(h, w, *([tgt] * per))


def _conv_gate_fwd(p0, conv_w, name, exchange=None):
    t = p0.shape[0]
    ec = p0.shape[1] // 8
    tm = _pick(t, 832, 16)
    nt = t // tm

    def body(p_ref, w_ref, g_ref, y_ref, ext_ref):
        i = pl.program_id(1)

        @pl.when(i == 0)
        def _():
            ext_ref[0:8, :] = jnp.zeros((8, ec), F32)

        for rows, cols in _strips(tm, ec):
            cg = p_ref[rows, _shift(cols, ec)].astype(F32)
            xin = p_ref[rows, _shift(cols, 2 * ec)].astype(F32)
            ext_ref[_shift(rows, 8), cols] = cg * xin
        for rows, cols in _strips(tm, ec):
            y = (w_ref[0:1, cols] * ext_ref[_shift(rows, 6), cols] + w_ref[1:2, cols] * ext_ref[_shift(rows, 7), cols]
                 + w_ref[2:3, cols] * ext_ref[_shift(rows, 8), cols])
            bg = p_ref[rows, cols].astype(F32)
            z = p_ref[rows, _shift(cols, 3 * ec)].astype(F32)
            y_ref[rows, cols] = y.astype(BF16)
            g_ref[rows, cols] = ((z * _sigmoid(z)) * bg * y).astype(BF16)
        ext_ref[0:8, :] = ext_ref[tm:tm + 8, :]

    out = pl.BlockSpec((tm, ec), lambda j, i: (i, j))
    return _call(
        body, name=name, grid=(2, nt),
        in_specs=[pl.BlockSpec((tm, 4 * ec), lambda j, i: (i, j)), pl.BlockSpec((3, ec), lambda j, i: (0, j))],
        out_specs=[out, out],
        out_shape=[jax.ShapeDtypeStruct((t, 2 * ec), BF16)] * 2,
        scratch_shapes=[pltpu.VMEM((tm + 8, ec), F32)], args=(p0, conv_w), exchange=exchange)


def _conv_gate_bwd(p0, y, dg, conv_w, name, exchange=None):
    t = p0.shape[0]
    ec = p0.shape[1] // 8
    tm = _pick(t, 416, 16)
    nt = t // tm

    def body(p_ref, y_ref, dg_ref, w_ref, dp_ref, dw_ref, ext_ref):
        i = pl.program_id(1)

        @pl.when(i == 0)
        def _():
            ext_ref[tm:tm + 8, :] = jnp.zeros((8, ec), F32)
            dw_ref[...] = jnp.zeros_like(dw_ref)

        for rows, cols in _strips(tm, ec):
            bg = p_ref[rows, cols].astype(F32)
            z = p_ref[rows, _shift(cols, 3 * ec)].astype(F32)
            yv = y_ref[rows, cols].astype(F32)
            dgv = dg_ref[rows, cols].astype(F32)
            sig = _sigmoid(z)
            sz = z * sig
            dp_ref[rows, _shift(cols, 3 * ec)] = (dgv * bg * yv * (sig * (1.0 + z * (1.0 - sig)))).astype(BF16)
            dp_ref[rows, cols] = (dgv * sz * yv).astype(BF16)
            ext_ref[rows, cols] = dgv * sz * bg
        acc = None
        for rows, cols in _strips(tm, ec):
            if rows.start == 0:
                acc = [jnp.zeros((STRIP_ROWS, cols.stop - cols.start), F32) for _ in range(3)]
            dy = ext_ref[rows, cols]
            dy1 = ext_ref[_shift(rows, 1), cols]
            dy2 = ext_ref[_shift(rows, 2), cols]
            cg = p_ref[rows, _shift(cols, ec)].astype(F32)
            xin = p_ref[rows, _shift(cols, 2 * ec)].astype(F32)
            da = w_ref[0:1, cols] * dy2 + w_ref[1:2, cols] * dy1 + w_ref[2:3, cols] * dy
            dp_ref[rows, _shift(cols, ec)] = (da * xin).astype(BF16)
            dp_ref[rows, _shift(cols, 2 * ec)] = (da * cg).astype(BF16)
            a = cg * xin
            acc = [acc[0] + a * dy2, acc[1] + a * dy1, acc[2] + a * dy]
            if rows.stop == tm:
                for tap in range(3):
                    dw_ref[tap:tap + 1, cols] += jnp.sum(acc[tap], axis=0, keepdims=True)
        ext_ref[tm:tm + 8, :] = ext_ref[0:8, :]

    rev = lambda j, i: (nt - 1 - i, j)
    return _call(
        body, name=name, grid=(2, nt),
        in_specs=[pl.BlockSpec((tm, 4 * ec), rev), pl.BlockSpec((tm, ec), rev), pl.BlockSpec((tm, ec), rev),
                  pl.BlockSpec((3, ec), lambda j, i: (0, j))],
        out_specs=[pl.BlockSpec((tm, 4 * ec), rev), pl.BlockSpec((8, ec), lambda j, i: (0, j))],
        out_shape=[jax.ShapeDtypeStruct((t, 8 * ec), BF16), jax.ShapeDtypeStruct((8, 2 * ec), F32)],
        scratch_shapes=[pltpu.VMEM((tm + 8, ec), F32)], args=(p0, y, dg, conv_w), exchange=exchange)


def _log_sigmoid(x):
    return jnp.minimum(x, 0.0) - jnp.log(1.0 + jnp.exp(-jnp.abs(x)))


def _gates_fwd(graw, graw_t, bias_row, bias_col, name):
    t = graw.shape[0]
    tm = _pick(t, 640, 128)
    cpt = tm // CHUNK
    nh = N_HEADS

    def body(g_ref, gt_ref, br_ref, bc_ref, colf_ref, rowf_ref):
        i = pl.program_id(0)
        gc = g_ref[...] + br_ref[...]
        rows = i * tm + lax.broadcasted_iota(jnp.int32, (tm, 1), 0)
        live = rows >= PAD_ROWS
        lf = jnp.where(live, _log_sigmoid(gc), 0.0)
        li = jnp.where(live, gc, NEG)
        gt = gt_ref[...] + bc_ref[...]
        cols = i * tm + lax.broadcasted_iota(jnp.int32, (1, tm), 1)
        live_t = cols >= PAD_ROWS
        lf_t = jnp.where(live_t, _log_sigmoid(gt), 0.0)
        li_t = jnp.where(live_t, gt, NEG)
        ri = lax.broadcasted_iota(jnp.int32, (CHUNK, CHUNK), 0)
        ci = lax.broadcasted_iota(jnp.int32, (CHUNK, CHUNK), 1)
        lower = (ri >= ci).astype(F32)
        upper = (ri <= ci).astype(F32)
        lane = lax.broadcasted_iota(jnp.int32, (CHUNK, LANES), 1)
        sub = lax.broadcasted_iota(jnp.int32, (8, CHUNK), 0)
        for c in range(cpt):
            sl = slice(c * CHUNK, (c + 1) * CHUNK)
            b_all = jnp.dot(lower, lf[sl, :], precision=HIGHEST, preferred_element_type=F32)
            bt_all = jnp.dot(lf_t[:, sl], upper, precision=HIGHEST, preferred_element_type=F32)
            for h in range(nh):
                b = b_all[:, nh + h:nh + h + 1]
                r = li[sl, h:h + 1] - b
                pre = gc[sl, nh + h:nh + h + 1]
                colf_ref[h, sl, :] = jnp.where(lane == 0, b, jnp.where(lane == 1, r, jnp.where(lane == 2, pre, 0.0)))
                b_row = bt_all[nh + h:nh + h + 1, :]
                r_row = li_t[h:h + 1, sl] - b_row
                rowf_ref[h, c, :, 0:CHUNK] = jnp.where(sub == 0, r_row, jnp.where(sub == 1, b_row, 0.0))
                rowf_ref[h, c, :, CHUNK:LANES] = jnp.zeros((8, LANES - CHUNK), F32)

    return pl.pallas_call(
        body, name=name, grid=(t // tm,),
        in_specs=[pl.BlockSpec((tm, LANES), lambda i: (i, 0)), pl.BlockSpec((16, tm), lambda i: (0, i)),
                  pl.BlockSpec((1, LANES), lambda i: (0, 0)), pl.BlockSpec((16, 1), lambda i: (0, 0))],
        out_specs=[pl.BlockSpec((nh, tm, LANES), lambda i: (0, i, 0)),
                   pl.BlockSpec((nh, cpt, 8, LANES), lambda i: (0, i, 0, 0))],
        out_shape=[jax.ShapeDtypeStruct((nh, t, LANES), F32),
                   jax.ShapeDtypeStruct((nh, t // CHUNK, 8, LANES), F32)],
        compiler_params=_params(),
    )(graw, graw_t, bias_row, bias_col)


def _gates_bwd(gstat, colf, dp1, name):
    nh, t, _ = gstat.shape
    tm = _pick(t, 640, 128)
    cpt = tm // CHUNK
    nt = t // tm
    nc = t // CHUNK
    gate_block = dp1.shape[1] // LANES - 1

    def body(gs_ref, nx_ref, colf_ref, dp_any, dg_ref, db_ref):
        i = pl.program_id(0)
        ri = lax.broadcasted_iota(jnp.int32, (CHUNK, CHUNK), 0)
        ci = lax.broadcasted_iota(jnp.int32, (CHUNK, CHUNK), 1)
        upper = (ri <= ci).astype(F32)
        lane = lax.broadcasted_iota(jnp.int32, (CHUNK, LANES), 1)
        total = jnp.zeros((1, LANES), F32)
        for c in range(cpt):
            sl = slice(c * CHUNK, (c + 1) * CHUNK)
            rows = i * tm + c * CHUNK + lax.broadcasted_iota(jnp.int32, (CHUNK, 1), 0)
            live = rows >= PAD_ROWS
            acc = jnp.zeros((CHUNK, LANES), F32)
            for h in range(nh):
                blk = gs_ref[h, sl, :]
                rev = jnp.dot(upper, blk, precision=HIGHEST, preferred_element_type=F32)
                if c + 1 < cpt:
                    carry = gs_ref[h, (c + 1) * CHUNK:(c + 1) * CHUNK + 1, 2:3]
                else:
                    carry = jnp.where(i == nt - 1, 0.0, nx_ref[h, 0:1, 2:3])
                dlogf = rev[:, 0:1] + carry
                pre = colf_ref[h, sl, 2:3]
                dgf = jnp.where(live, dlogf * (1.0 - _sigmoid(pre)), 0.0)
                dgi = jnp.where(live, blk[:, 1:2], 0.0)
                acc = acc + jnp.where(lane == h, dgi, 0.0) + jnp.where(lane == nh + h, dgf, 0.0)
            dg_ref[sl, :] = acc.astype(BF16)
            total = total + jnp.sum(acc, axis=0, keepdims=True)

        @pl.when(i == 0)
        def _():
            db_ref[...] = jnp.zeros_like(db_ref)

        db_ref[...] += jnp.broadcast_to(total, db_ref.shape)

    return pl.pallas_call(
        body, name=name, grid=(nt,),
        in_specs=[pl.BlockSpec((nh, tm, LANES), lambda i: (0, i, 0)),
                  pl.BlockSpec((nh, CHUNK, LANES), lambda i: (0, jnp.minimum((i + 1) * cpt, nc - 1), 0)),
                  pl.BlockSpec((nh, tm, LANES), lambda i: (0, i, 0)),
                  pl.BlockSpec(memory_space=pl.ANY)],
        out_specs=[pl.BlockSpec((tm, LANES), lambda i: (i, gate_block)), pl.BlockSpec((8, LANES), lambda i: (0, 0))],
        out_shape=[jax.ShapeDtypeStruct(dp1.shape, dp1.dtype), jax.ShapeDtypeStruct((8, LANES), F32)],
        input_output_aliases={3: 0},
        compiler_params=_params(),
    )(gstat, gstat, colf, dp1)


NT_DIMS = (((1,), (1,)), ((), ()))
TN_DIMS = (((0,), (0,)), ((), ()))


def _dot(a, b):
    return jnp.dot(a.astype(BF16), b.astype(BF16), preferred_element_type=F32)


def _dot_nt(a, b):
    return lax.dot_general(a.astype(BF16), b.astype(BF16), NT_DIMS, preferred_element_type=F32)


def _dot_tn(a, b):
    return lax.dot_general(a.astype(BF16), b.astype(BF16), TN_DIMS, preferred_element_type=F32)


def _chunk_gates(colf_ref, rowf_ref, m_prev, width=CHUNK):
    b = colf_ref[:, 0:1]
    rcol = colf_ref[:, 1:2]
    rrow = rowf_ref[0:1, 0:width]
    ri = lax.broadcasted_iota(jnp.int32, (CHUNK, width), 0)
    ci = lax.broadcasted_iota(jnp.int32, (CHUNK, width), 1)
    log_d = jnp.where(ri >= ci, b + rrow, NEG)
    m_row = jnp.maximum(b + m_prev, jnp.max(log_d, axis=-1, keepdims=True))
    dmat = jnp.exp(log_d - m_row)
    inter = jnp.exp(b + m_prev - m_row)
    b_last = b[CHUNK - 1:CHUNK, :]
    log_w = rcol + b_last
    m_new = jnp.maximum(b_last + m_prev, jnp.max(log_w, axis=0, keepdims=True))
    decay = jnp.exp(b_last + m_prev - m_new)
    w = jnp.exp(log_w - m_new)
    return m_row, dmat, inter, m_new, decay, w


def _mlstm_fwd(p1, colf, rowf, dk, dv, name):
    t = p1.shape[0]
    nh = N_HEADS
    nc = t // CHUNK
    hw = 2 * dk + dv
    cps = CHUNKS_PER_STEP if nc % CHUNKS_PER_STEP == 0 else 1
    rows_per_step = cps * CHUNK

    def body(p_ref, colf_ref, rowf_ref, hh_ref, stat_ref, cs_ref, ns_ref, *state):
        c_refs, n_refs, m_refs = state[:nh], state[nh:2 * nh], state[2 * nh:]
        c = pl.program_id(0)

        @pl.when(c == 0)
        def _():
            for ref in state:
                ref[...] = jnp.zeros_like(ref)

        for cc in range(cps):
            rows = pl.ds(cc * CHUNK, CHUNK)
            for h in range(nh):
                head(p_ref.at[rows, pl.ds(h * hw, hw)], colf_ref.at[h, rows], rowf_ref.at[h, cc],
                     hh_ref.at[rows, pl.ds(h * dv, dv)], stat_ref.at[h, rows], cs_ref.at[h, cc],
                     ns_ref.at[h, cc], c_refs[h], n_refs[h], m_refs[h])

    def head(p_ref, colf_ref, rowf_ref, hh_ref, stat_ref, cs_ref, ns_ref, c_ref, n_ref, m_ref):
        m_prev = m_ref[...]
        n_prev = n_ref[...]
        c_prev = c_ref[...]
        cs_ref[...] = c_prev.astype(BF16)
        sub = lax.broadcasted_iota(jnp.int32, (8, dk), 0)
        ns_ref[...] = jnp.where(sub == 0, n_prev, jnp.where(sub == 1, m_prev, 0.0))

        q = p_ref[:, 0:dk]
        k = p_ref[:, dk:2 * dk]
        v = p_ref[:, 2 * dk:2 * dk + dv]
        m_row, dmat, inter, m_new, decay, w = _chunk_gates(colf_ref, rowf_ref, m_prev)
        s = _dot_nt(q, k) * dmat
        num = _dot(s, v) + inter * _dot(q, c_prev)
        den = jnp.sum(s, axis=-1, keepdims=True) + inter * jnp.sum(q.astype(F32) * n_prev, axis=-1, keepdims=True)
        denom = jnp.maximum(jnp.abs(den), jnp.exp(-m_row))
        hh_ref[...] = num * (1.0 / denom)
        lane = lax.broadcasted_iota(jnp.int32, (CHUNK, LANES), 1)
        stat_ref[...] = jnp.where(lane == 0, den, 0.0)

        wk = w * k.astype(F32)
        c_ref[...] = decay * c_prev + _dot_tn(wk, v)
        n_ref[...] = decay * n_prev + jnp.sum(wk, axis=0, keepdims=True)
        m_ref[...] = m_new

    return pl.pallas_call(
        body, name=name, grid=(nc // cps,),
        in_specs=[pl.BlockSpec((rows_per_step, nh * hw), lambda c: (c, 0)),
                  pl.BlockSpec((nh, rows_per_step, LANES), lambda c: (0, c, 0)),
                  pl.BlockSpec((nh, cps, 8, LANES), lambda c: (0, c, 0, 0))],
        out_specs=[pl.BlockSpec((rows_per_step, nh * dv), lambda c: (c, 0)),
                   pl.BlockSpec((nh, rows_per_step, LANES), lambda c: (0, c, 0)),
                   pl.BlockSpec((nh, cps, dk, dv), lambda c: (0, c, 0, 0)),
                   pl.BlockSpec((nh, cps, 8, dk), lambda c: (0, c, 0, 0))],
        out_shape=[jax.ShapeDtypeStruct((t, nh * dv), F32), jax.ShapeDtypeStruct((nh, t, LANES), F32),
                   jax.ShapeDtypeStruct((nh, nc, dk, dv), BF16), jax.ShapeDtypeStruct((nh, nc, 8, dk), F32)],
        scratch_shapes=([pltpu.VMEM((dk, dv), F32)] * nh + [pltpu.VMEM((1, dk), F32)] * nh
                        + [pltpu.VMEM((1, 1), F32)] * nh),
        compiler_params=_params(),
    )(p1, colf, rowf)


def _head_gate_fwd(hh, p1, head_w, dv, name):
    t = hh.shape[0]
    nh = N_HEADS
    e = nh * dv
    tm = _pick(t, 416, 16)

    def body(hh_ref, oz_ref, w_ref, hm_ref):
        for h in range(nh):
            cols = slice(h * dv, (h + 1) * dv)
            x = hh_ref[:, cols]
            o = oz_ref[:, 2 * h * dv:(2 * h + 1) * dv].astype(F32)
            z = oz_ref[:, (2 * h + 1) * dv:(2 * h + 2) * dv].astype(F32)
            r = lax.rsqrt(jnp.mean(x * x, axis=-1, keepdims=True) + RMS_EPS)
            hn = (x * r) * w_ref[:, cols]
            hm_ref[:, cols] = (hn * _sigmoid(o) * (z * _sigmoid(z))).astype(BF16)

    return pl.pallas_call(
        body, name=name, grid=(t // tm,),
        in_specs=[pl.BlockSpec((tm, e), lambda i: (i, 0)), pl.BlockSpec((tm, 2 * e), lambda i: (i, 1)),
                  pl.BlockSpec((1, e), lambda i: (0, 0))],
        out_specs=pl.BlockSpec((tm, e), lambda i: (i, 0)),
        out_shape=jax.ShapeDtypeStruct((t, e), BF16),
        compiler_params=_params(),
    )(hh, p1, head_w)


def _mlstm_bwd(p1, colf, rowf, head_w, hh, stat, csave, nsave, dhm, n_cols, dk, dv, name, exchange=None):
    t = p1.shape[0]
    nh = N_HEADS
    nc = t // CHUNK
    hw = 2 * dk + dv
    cps = CHUNKS_PER_STEP if nc % CHUNKS_PER_STEP == 0 else 1
    rows_per_step = cps * CHUNK

    def body(p_ref, colf_ref, rowf_ref, w_ref, hh_ref, stat_ref, cs_ref, ns_ref, dhm_ref,
             dp_ref, gs_ref, dw_ref, *state):
        dc_refs, dn_refs, dwacc_refs = state[:nh], state[nh:2 * nh], state[2 * nh:]
        c = pl.program_id(0)

        @pl.when(c == 0)
        def _():
            for ref in state:
                ref[...] = jnp.zeros_like(ref)

        for cc in reversed(range(cps)):
            rows = pl.ds(cc * CHUNK, CHUNK)
            for h in range(nh):
                cols = pl.ds(h * dv, dv)
                oz = pl.ds(nh * hw + h * 2 * dv, 2 * dv)
                head(p_ref.at[rows, pl.ds(h * hw, hw)], p_ref.at[rows, oz], colf_ref.at[h, rows], rowf_ref.at[h, cc],
                     w_ref.at[:, cols], hh_ref.at[rows, cols], stat_ref.at[h, rows], cs_ref.at[h, cc],
                     ns_ref.at[h, cc], dhm_ref.at[rows, cols], dp_ref.at[rows, pl.ds(h * hw, hw)],
                     dp_ref.at[rows, oz], gs_ref.at[h, rows], dwacc_refs[h], dc_refs[h], dn_refs[h])

        @pl.when(c == nc // cps - 1)
        def _():
            for h in range(nh):
                dw_ref[h] = dwacc_refs[h][...]

    def head(p_ref, oz_ref, colf_ref, rowf_ref, w_ref, hh_ref, stat_ref, cs_ref, ns_ref, dhm_ref,
             dp_ref, doz_ref, gs_ref, dw_ref, dc_ref, dn_ref):
        q = p_ref[:, 0:dk]
        k = p_ref[:, dk:2 * dk]
        v = p_ref[:, 2 * dk:2 * dk + dv]
        o = oz_ref[:, 0:dv].astype(F32)
        z = oz_ref[:, dv:2 * dv].astype(F32)
        qf = q.astype(F32)
        kf = k.astype(F32)
        n_prev = ns_ref[0:1, :]
        m_prev = ns_ref[1:2, 0:1]
        c_prev = cs_ref[...]
        m_row, dmat, inter, m_new, decay, w = _chunk_gates(colf_ref, rowf_ref, m_prev)

        hh = hh_ref[...]
        dhm_v = dhm_ref[...].astype(F32)
        so = _sigmoid(o)
        sg = _sigmoid(z)
        sz = z * sg
        r = lax.rsqrt(jnp.mean(hh * hh, axis=-1, keepdims=True) + RMS_EPS)
        hn = (hh * r) * w_ref[...]
        dhn = dhm_v * so * sz
        doz_ref[:, 0:dv] = (dhm_v * hn * sz * (so * (1.0 - so))).astype(BF16)
        doz_ref[:, dv:2 * dv] = (dhm_v * hn * so * (sg * (1.0 + z * (1.0 - sg)))).astype(BF16)
        dw_ref[...] += jnp.broadcast_to(jnp.sum(dhn * (hh * r), axis=0, keepdims=True), dw_ref.shape)
        gwn = dhn * w_ref[...]
        dhh = r * gwn - hh * ((r * r * r) * jnp.mean(gwn * hh, axis=-1, keepdims=True))

        den = stat_ref[:, 0:1]
        floor = jnp.exp(-m_row)
        denom = jnp.maximum(jnp.abs(den), floor)
        inv = 1.0 / denom
        dnum = dhh * inv
        hdot = jnp.sum(dhh * hh, axis=-1, keepdims=True)
        dden = jnp.where(jnp.abs(den) > floor, -(hdot * inv) * jnp.sign(den), 0.0)
        s = _dot_nt(q, k) * dmat
        dqk = (_dot_nt(dnum, v) + dden) * dmat
        dc_new = dc_ref[...]
        dn_new = dn_ref[...]
        idd = inter * dden
        dq = _dot(dqk, k) + inter * _dot_nt(dnum, c_prev) + idd * n_prev
        dkv = _dot_tn(dqk, q) + w * (_dot_nt(v, dc_new) + dn_new)
        dvv = _dot_tn(s, dnum) + w * _dot(k, dc_new)
        dp_ref[:, 0:dk] = dq.astype(BF16)
        dp_ref[:, dk:2 * dk] = dkv.astype(BF16)
        dp_ref[:, 2 * dk:2 * dk + dv] = dvv.astype(BF16)
        qdq = jnp.sum(qf * dq, axis=-1, keepdims=True)
        kdk = jnp.sum(kf * dkv, axis=-1, keepdims=True)
        dc_prev = decay * dc_new + _dot_tn(inter * qf, dnum)
        dn_prev = decay * dn_new + jnp.sum(idd * qf, axis=0, keepdims=True)
        dc_ref[...] = dc_prev
        dn_ref[...] = dn_prev
        cross = (jnp.sum(jnp.sum(c_prev.astype(F32) * dc_prev, axis=-1, keepdims=True), axis=0, keepdims=True)
                 + jnp.sum(n_prev * dn_prev, axis=-1, keepdims=True))
        lane = lax.broadcasted_iota(jnp.int32, (CHUNK, LANES), 1)
        gs_ref[...] = jnp.where(lane == 0, qdq - kdk, jnp.where(lane == 1, kdk, jnp.where(lane == 2, cross, 0.0)))

    ns = nc // cps
    rc = lambda c: (ns - 1 - c, 0)
    rc3 = lambda c: (0, ns - 1 - c, 0)
    rc4 = lambda c: (0, ns - 1 - c, 0, 0)
    return _call(
        body, name=name, grid=(ns,),
        in_specs=[pl.BlockSpec((rows_per_step, nh * (hw + 2 * dv)), rc),
                  pl.BlockSpec((nh, rows_per_step, LANES), rc3),
                  pl.BlockSpec((nh, cps, 8, LANES), rc4),
                  pl.BlockSpec((1, nh * dv), lambda c: (0, 0)),
                  pl.BlockSpec((rows_per_step, nh * dv), rc),
                  pl.BlockSpec((nh, rows_per_step, LANES), rc3),
                  pl.BlockSpec((nh, cps, dk, dv), rc4),
                  pl.BlockSpec((nh, cps, 8, dk), rc4),
                  pl.BlockSpec((rows_per_step, nh * dv), rc)],
        out_specs=[pl.BlockSpec((rows_per_step, nh * (hw + 2 * dv)), rc),
                   pl.BlockSpec((nh, rows_per_step, LANES), rc3),
                   pl.BlockSpec((nh, 8, dv), lambda c: (0, 0, 0))],
        out_shape=[jax.ShapeDtypeStruct((t, n_cols), BF16), jax.ShapeDtypeStruct((nh, t, LANES), F32),
                   jax.ShapeDtypeStruct((nh, 8, dv), F32)],
        scratch_shapes=([pltpu.VMEM((dk, dv), F32)] * nh + [pltpu.VMEM((1, dk), F32)] * nh
                        + [pltpu.VMEM((8, dv), F32)] * nh),
        args=(p1, colf, rowf, head_w, hh, stat, csave, nsave, dhm), exchange=exchange)


def _position():
    return lax.axis_index("x"), lax.axis_index("y"), lax.axis_index("c")


class _Exchange:
    def __init__(self, ins, out_shapes, start, mid, finish):
        n = len(ins)
        self.ins, self.out_shapes = list(ins), list(out_shapes)
        self.start, self.mid, self.finish = start, mid, finish
        self.scratch = [pltpu.SemaphoreType.DMA((n, 7)), pltpu.SemaphoreType.DMA((n, 7)),
                        pltpu.SemaphoreType.DMA((n,))]


def _gather_exchange(shards):
    n = len(shards)

    def plan(ins, outs, sems):
        send_sems, recv_sems, local_sems = sems
        x, y, c = _position()
        me, sibling = (x, y, c), (x, y, 1 - c)
        chips = [(1 - x, y), (x, 1 - y), (1 - x, 1 - y)]

        def copy(a, k, block, to, src=None):
            px, py, pc = block
            dst = outs[a].at[4 * px + 2 * py + pc]
            return pltpu.make_async_remote_copy(
                src_ref=dst if src is None else src, dst_ref=dst,
                send_sem=send_sems.at[a, k], recv_sem=recv_sems.at[a, k],
                device_id=to, device_id_type=MESH)

        def mine():
            return [pltpu.make_async_copy(ins[a], outs[a].at[4 * x + 2 * y + c], local_sems.at[a])
                    for a in range(n)]

        def first():
            out = []
            for a in range(n):
                out.append(copy(a, 0, me, sibling, src=ins[a]))
                out += [copy(a, 1 + j, me, (*chip, c), src=ins[a]) for j, chip in enumerate(chips)]
            return out

        def ici_in():
            return [copy(a, 1 + j, (*chip, c), me) for j, chip in enumerate(chips) for a in range(n)]

        def passed():
            return [copy(a, 4 + j, (*chip, c), sibling) for j, chip in enumerate(chips) for a in range(n)]

        def d2d_in():
            return ([copy(a, 0, sibling, me) for a in range(n)]
                    + [copy(a, 4 + j, (*chip, 1 - c), me) for j, chip in enumerate(chips) for a in range(n)])

        return mine, first, ici_in, passed, d2d_in

    def start(ins, outs, sems):
        mine, first, _, _, _ = plan(ins, outs, sems)
        for cp in mine() + first():
            cp.start()

    def mid(ins, outs, sems):
        _, _, ici_in, passed, _ = plan(ins, outs, sems)
        for arrived, onward in zip(ici_in(), passed()):
            arrived.wait_recv()
            onward.start()

    def finish(ins, outs, sems):
        mine, first, _, passed, d2d_in = plan(ins, outs, sems)
        for cp in d2d_in():
            cp.wait_recv()
        for cp in first() + passed():
            cp.wait_send()
        for cp in mine():
            cp.wait()

    shapes = [jax.ShapeDtypeStruct((N_DEV,) + s.shape, s.dtype) for s in shards]
    return _Exchange(shards, shapes, start, mid, finish)


def _scatter_exchange(fulls):
    n = len(fulls)

    def plan(ins, outs, sems):
        send_sems, recv_sems, local_sems = sems
        x, y, c = _position()
        my_slot = 4 * x + 2 * y + c

        def mine():
            return [pltpu.make_async_copy(ins[a].at[my_slot], outs[a].at[my_slot], local_sems.at[a])
                    for a in range(n)]

        def remote(arriving):
            out = []
            for kk in (1, 2, 4, 6, 3, 5, 7):
                kx, ky, kc = (kk >> 2) & 1, (kk >> 1) & 1, kk & 1
                px = 1 - x if kx else x
                py = 1 - y if ky else y
                pc = 1 - c if kc else c
                peer_slot = 4 * px + 2 * py + pc
                for a in range(n):
                    out.append(pltpu.make_async_remote_copy(
                        src_ref=ins[a].at[peer_slot], dst_ref=outs[a].at[peer_slot if arriving else my_slot],
                        send_sem=send_sems.at[a, kk - 1], recv_sem=recv_sems.at[a, kk - 1],
                        device_id=(px, py, pc), device_id_type=MESH))
            return out

        return mine, remote

    def start(ins, outs, sems):
        mine, remote = plan(ins, outs, sems)
        for cp in mine() + remote(False):
            cp.start()

    def finish(ins, outs, sems):
        mine, remote = plan(ins, outs, sems)
        for cp in remote(True):
            cp.wait_recv()
        for cp in remote(False):
            cp.wait_send()
        for cp in mine():
            cp.wait()

    shapes = [jax.ShapeDtypeStruct(f.shape, f.dtype) for f in fulls]
    return _Exchange(fulls, shapes, start, None, finish)


def _run_exchange(ex, name):
    n_in, n_out = len(ex.ins), len(ex.out_shapes)

    def body(*refs):
        ins, outs, sems = refs[:n_in], refs[n_in:n_in + n_out], refs[n_in + n_out:]
        ex.start(ins, outs, sems)
        if ex.mid is not None:
            ex.mid(ins, outs, sems)
        ex.finish(ins, outs, sems)

    any_spec = pl.BlockSpec(memory_space=pl.ANY)
    return pl.pallas_call(
        body, name=name,
        in_specs=[any_spec] * n_in, out_specs=[any_spec] * n_out,
        out_shape=ex.out_shapes, scratch_shapes=ex.scratch,
    )(*ex.ins)


def _natural_to_head_major(n0, dk, dv):
    nh = N_HEADS
    qk, e, hw = nh * dk, nh * dv, 2 * dk + dv
    if n0 < qk:
        h, off = divmod(n0, dk)
        return h * hw + off, True
    n1 = n0 - qk
    if n1 < qk:
        h, off = divmod(n1, dk)
        return h * hw + dk + off, False
    part, n3 = divmod(n1 - qk, e)
    h, off = divmod(n3, dv)
    if part == 0:
        return h * hw + 2 * dk + off, False
    return nh * hw + h * 2 * dv + (part - 1) * dv + off, False


def _mlstm_weight_layout(w_blocks, row0, dk, dv, name, fill=None):
    _, n_rows, blk = w_blocks.shape
    d = blk - 1
    nh = N_HEADS
    hw = 2 * dk + 3 * dv
    n_tiles = nh * hw // LANES
    tiles_per_block = d // LANES
    tr = _pick(d, 128, 16)
    scale = dk ** -0.5

    def body(w_ref, *rest):
        p_ref, g_ref = rest[-2:]
        lane = lax.broadcasted_iota(jnp.int32, (tr, LANES), 1)

        def tile(s, r):
            return w_ref[s, :, r * LANES:(r + 1) * LANES].astype(F32)

        def last_col(s):
            return w_ref[s, :, d:d + 1].astype(F32)

        for tn in range(n_tiles):
            s, r = divmod(tn, tiles_per_block)
            if s == 0:
                val = tile(0, r)
            elif r == 0:
                from_prev = pltpu.roll(tile(s - 1, tiles_per_block - 1), s - 1, 1)
                from_here = pltpu.roll(tile(s, 0), s, 1)
                val = jnp.where(lane < s - 1, from_prev, jnp.where(lane == s - 1, last_col(s - 1), from_here))
            else:
                slab = jnp.concatenate([tile(s, r - 1), tile(s, r)], axis=1)
                val = pltpu.roll(slab, s, 1)[:, LANES:]
            at, is_q = _natural_to_head_major(tn * LANES, dk, dv)
            if is_q:
                val = val * scale
            p_ref[:, at:at + LANES] = val.astype(BF16)
        n_gate = 2 * nh
        s = N_DEV - 1
        gates = pltpu.roll(tile(s, tiles_per_block - 1), n_gate - 1, 1)
        gates = jnp.where(lane < n_gate - 1, gates, jnp.where(lane == n_gate - 1, last_col(s), 0.0)).astype(BF16)
        g_ref[...] = gates
        p_ref[:, nh * hw:nh * hw + LANES] = gates

    first = row0 // tr
    shapes = [jax.ShapeDtypeStruct((d, nh * hw + LANES), BF16), jax.ShapeDtypeStruct((d, LANES), BF16)]
    fills = [] if fill is None else list(fill)
    return pl.pallas_call(
        body, name=name, grid=(n_rows // tr,),
        in_specs=[pl.BlockSpec((N_DEV, tr, blk), lambda i: (0, i, 0))] + [pl.BlockSpec(memory_space=pl.ANY)] * len(fills),
        out_specs=[pl.BlockSpec((tr, nh * hw + LANES), lambda i: (i + first, 0)),
                   pl.BlockSpec((tr, LANES), lambda i: (i + first, 0))],
        out_shape=shapes, input_output_aliases={1 + k: k for k in range(len(fills))},
        compiler_params=_params(),
    )(w_blocks, *fills)


def _mlstm_grad_layout(g_p1, row0, n_rows, dk, dv, name):
    d = g_p1.shape[0]
    nh = N_HEADS
    hw = 2 * dk + 3 * dv
    n_tiles = nh * hw // LANES
    tiles_per_block = d // LANES
    tr = _pick(d, 128, 16)
    scale = dk ** -0.5

    def body(p_ref, o_ref):
        def natural(tn):
            if tn == n_tiles:
                return p_ref[:, nh * hw:nh * hw + LANES].astype(F32)
            at, is_q = _natural_to_head_major(tn * LANES, dk, dv)
            val = p_ref[:, at:at + LANES].astype(F32)
            return val * scale if is_q else val

        for s in range(N_DEV):
            for r in range(tiles_per_block):
                tn = s * tiles_per_block + r
                if s == 0:
                    val = natural(tn)
                else:
                    slab = jnp.concatenate([natural(tn), natural(tn + 1)], axis=1)
                    val = pltpu.roll(slab, 2 * LANES - s, 1)[:, :LANES]
                o_ref[s, :, r * LANES:(r + 1) * LANES] = val.astype(BF16)
            o_ref[s, :, d:d + 1] = natural((s + 1) * tiles_per_block)[:, s:s + 1].astype(BF16)

    first = row0 // tr
    return pl.pallas_call(
        body, name=name, grid=(n_rows // tr,),
        in_specs=[pl.BlockSpec((tr, nh * hw + LANES), lambda i: (i + first, 0))],
        out_specs=pl.BlockSpec((N_DEV, tr, d + 1), lambda i: (0, i, 0)),
        out_shape=jax.ShapeDtypeStruct((N_DEV, n_rows, d + 1), BF16),
        compiler_params=_params(),
    )(g_p1)


def _adamw_math(w, g, m, v):
    m = ADAM_B1 * m + (1.0 - ADAM_B1) * g
    v = ADAM_B2 * v + (1.0 - ADAM_B2) * (g * g)
    m_hat = m / (1.0 - ADAM_B1 ** ADAM_STEP)
    v_hat = v / (1.0 - ADAM_B2 ** ADAM_STEP)
    delta = -ADAM_LR * (m_hat / (jnp.sqrt(v_hat) + ADAM_EPS) + ADAM_WD * w)
    return delta, m, v


def _adamw_sharded(parts, w, m, v, name):
    n_parts = len(parts)
    _, rk, c = parts[0].shape
    r = n_parts * rk
    tr = _pick(rk, 128, 8)
    per = rk // tr

    def body(*refs):
        p_refs = refs[:n_parts]
        w_ref, m_ref, v_ref, g_ref, d_ref, nm_ref, nv_ref, gsum_ref = refs[n_parts:]
        i = pl.program_id(0)
        for k, p_ref in enumerate(p_refs):
            @pl.when(i // per == k)
            def _(p_ref=p_ref):
                g = p_ref[0].astype(F32)
                for s in range(1, N_DEV):
                    g = g + p_ref[s].astype(F32)
                gsum_ref[...] = g
        g = gsum_ref[...]
        delta, m_new, v_new = _adamw_math(w_ref[...], g, m_ref[...], v_ref[...])
        g_ref[...] = g
        d_ref[...] = delta
        nm_ref[...] = m_new
        nv_ref[...] = v_new

    def part_spec(k):
        return pl.BlockSpec((N_DEV, tr, c), lambda i: (0, jnp.clip(i - k * per, 0, per - 1), 0))

    blk = pl.BlockSpec((tr, c), lambda i: (i, 0))
    return pl.pallas_call(
        body, name=name, grid=(r // tr,),
        in_specs=[part_spec(k) for k in range(n_parts)] + [blk, blk, blk],
        out_specs=[blk] * 4,
        out_shape=[jax.ShapeDtypeStruct((r, c), F32)] * 4,
        scratch_shapes=[pltpu.VMEM((tr, c), F32)],
        compiler_params=_params(),
    )(*parts, w, m, v)


def _sum_devices(parts, name):
    _, r, c = parts.shape

    def body(p_ref, o_ref):
        g = p_ref[0]
        for s in range(1, N_DEV):
            g = g + p_ref[s]
        o_ref[...] = g

    return pl.pallas_call(
        body, name=name, out_shape=jax.ShapeDtypeStruct((r, c), F32), compiler_params=_params(),
    )(parts)


def _adamw_small(gs, ws, ms, vs, name):
    n = len(gs)

    def body(*refs):
        g_refs, w_refs, m_refs, v_refs = refs[:n], refs[n:2 * n], refs[2 * n:3 * n], refs[3 * n:4 * n]
        d_refs, nm_refs, nv_refs = refs[4 * n:5 * n], refs[5 * n:6 * n], refs[6 * n:7 * n]
        for a in range(n):
            delta, m_new, v_new = _adamw_math(w_refs[a][...], g_refs[a][...], m_refs[a][...], v_refs[a][...])
            d_refs[a][...] = delta
            nm_refs[a][...] = m_new
            nv_refs[a][...] = v_new

    shapes = [jax.ShapeDtypeStruct(w.shape, F32) for w in ws]
    outs = pl.pallas_call(
        body, name=name, out_shape=shapes * 3, compiler_params=_params(),
    )(*gs, *ws, *ms, *vs)
    return outs[:n], outs[n:2 * n], outs[2 * n:]


def _pad_rows(a, rows):
    return jnp.pad(a, ((0, rows - a.shape[0]), (0, 0)))


def kernel(x, meta_tokens, norm_w, conv_in_w, conv_w, conv_out_w, mlstm_in_w, mlstm_gate_b, mlstm_head_norm_w, mlstm_out_w, final_norm_w, loss_target, m_meta_tokens, m_norm_w, m_conv_in_w, m_conv_w, m_conv_out_w, m_mlstm_in_w, m_mlstm_gate_b, m_mlstm_head_norm_w, m_mlstm_out_w, m_final_norm_w, v_meta_tokens, v_norm_w, v_conv_in_w, v_conv_w, v_conv_out_w, v_mlstm_in_w, v_mlstm_gate_b, v_mlstm_head_norm_w, v_mlstm_out_w, v_final_norm_w):
    seq, d = x.shape[1], x.shape[2]
    t = seq + LEAD
    e = 2 * d
    ec = e // 2
    nh = N_HEADS
    dv = e // nh
    dk = dv // 2
    qk = nh * dk
    hw = 2 * dk + 3 * dv
    n_in = 2 * qk + 3 * e + 2 * nh
    n_in_s = n_in // N_DEV
    me = 4 * lax.axis_index("x") + 2 * lax.axis_index("y") + lax.axis_index("c")
    tm = _pick(t, 832, 16)
    tm_in = _pick(t, 1664, 16)
    tkw = _pick(t, 2080, 16)

    small = jnp.concatenate([
        meta_tokens,
        _pad_rows(conv_w[0].reshape(3 * (e // N_DEV) // LANES, LANES), 8),
        _pad_rows(mlstm_head_norm_w[0].reshape((e // N_DEV) // LANES, LANES), 8),
    ], axis=0) if d // N_DEV == LANES else None
    assert small is not None, "the packed small-weight block assumes d_model / 8 == 128"
    (small_g,) = _run_exchange(_gather_exchange([small]), "gather_small_weights")
    meta_full = jnp.transpose(small_g[:, 0:N_META, :], (1, 0, 2)).reshape(N_META, d)
    cw_rows = 3 * (e // N_DEV) // LANES
    conv_w_full = jnp.transpose(
        small_g[:, N_META:N_META + cw_rows, :].reshape(N_DEV, 3, e // N_DEV), (1, 0, 2)).reshape(3, e)
    hn_rows = (e // N_DEV) // LANES
    head_w_full = small_g[:, N_META + 8:N_META + 8 + hn_rows, :].reshape(1, e)

    tgt = loss_target[0]

    ci_map = lambda blk: 2 * (blk % 4) + blk // 4
    half = ec // 2
    w_ci_own = conv_in_w[0].astype(BF16)
    (h0, u0), (w_ci_a,) = _rms_first(x[0], meta_full, norm_w[0:1], "rms0",
                                     exchange=_gather_exchange([w_ci_own[:, :half]]))
    half_spec = pl.BlockSpec((None, d, half), lambda i, j, k: (ci_map(j), 0, 0))
    p0, (w_ci_b, w_co) = _matmul(
        u0, w_ci_a, form="nn", m=t, n=8 * half, kdim=d, tm=tm_in, tn=half, tk=d, out_dtype=BF16, name="conv_in_a",
        b_spec=half_spec, out_shape=(t, 8 * ec), out_spec=pl.BlockSpec((tm_in, half), lambda i, j, k: (i, 2 * j)),
        exchange=_gather_exchange([w_ci_own[:, half:], conv_out_w[0].astype(BF16)]))
    w_mi_own = mlstm_in_w[0].astype(BF16)
    p0, (w_mi_a,) = _matmul(
        u0, w_ci_b, form="nn", m=t, n=8 * half, kdim=d, tm=tm_in, tn=half, tk=d, out_dtype=BF16, name="conv_in_b",
        b_spec=half_spec, out_shape=(t, 8 * ec), out_spec=pl.BlockSpec((tm_in, half), lambda i, j, k: (i, 2 * j + 1)),
        fill=p0, exchange=_gather_exchange([w_mi_own[:d // 2]]))
    w_ci = jnp.concatenate([w_ci_a, w_ci_b], axis=2)
    w_co = w_co.reshape(e, d)
    (g0, y0), (w_mi_b, w_mo) = _conv_gate_fwd(
        p0, conv_w_full, "conv_gate", exchange=_gather_exchange([w_mi_own[d // 2:], mlstm_out_w[0].astype(BF16)]))
    w_mo = w_mo.reshape(e, d)
    h1 = _matmul(g0, w_co, form="nn", m=t, n=d, kdim=e, tm=tm, tn=d, tk=e, out_dtype=F32, name="conv_out",
                 residual=h0)

    assert n_in_s == d + 1 and w_mi_a.shape == (N_DEV, d // 2, d + 1)
    w_p1, w_gate = _mlstm_weight_layout(w_mi_a, 0, dk, dv, "mlstm_w_layout_a")
    w_p1, w_gate = _mlstm_weight_layout(w_mi_b, d // 2, dk, dv, "mlstm_w_layout_b", fill=(w_p1, w_gate))
    w_gate_t = jnp.transpose(w_gate[:, 0:16])
    bias_row = jnp.pad(mlstm_gate_b, ((0, 0), (0, LANES - 2 * nh)))
    bias_col = jnp.pad(mlstm_gate_b.T, ((0, 16 - 2 * nh), (0, 0)))

    u1 = _rms_fwd(h1, norm_w[1:2], "rms1")
    p1 = _matmul(u1, w_p1, form="nn", m=t, n=nh * hw, kdim=d, tm=tm_in, tn=_pick(nh * hw, 2048, LANES), tk=d,
                 out_dtype=BF16, name="mlstm_in")
    graw = _matmul(u1, w_gate, form="nn", m=t, n=LANES, kdim=d, tm=tm, tn=LANES, tk=d, out_dtype=F32,
                   name="gates_col")
    graw_t = _matmul(w_gate_t, u1, form="nt", m=16, n=t, kdim=d, tm=16, tn=_pick(t, 1664, LANES), tk=d,
                     out_dtype=F32, name="gates_row")
    colf, rowf = _gates_fwd(graw, graw_t, bias_row, bias_col, "gates_fwd")
    hh, stat, csave, nsave = _mlstm_fwd(p1, colf, rowf, dk, dv, "mlstm_fwd")
    hm = _head_gate_fwd(hh, p1, head_w_full, dv, "head_gate")
    h2 =_matmul(hm, w_mo, form="nn", m=t, n=d, kdim=e, tm=tm, tn=d, tk=e, out_dtype=F32, name="mlstm_out",
                 residual=h1)

    dh2, dwf, loss_part = _final_loss(h2, final_norm_w.reshape(1, d), tgt, "final_loss")

    dhm = _matmul(dh2, w_mo, form="nt", m=t, n=e, kdim=d, tm=tm, tn=_pick(e, 2048, LANES), tk=d, out_dtype=BF16,
                  name="mlstm_out_dx")
    g_mo = _matmul(hm, dh2, form="tn", m=e, n=d, kdim=t, tm=_pick(e, 1024, LANES), tn=d, tk=tkw, out_dtype=BF16,
                   name="mlstm_out_dw")
    n_p1 = nh * hw + LANES
    (dp1, gstat, dhead), (r_mo,) = _mlstm_bwd(
        p1, colf, rowf, head_w_full, hh, stat, csave, nsave, dhm, n_p1, dk, dv, "mlstm_bwd",
        exchange=_scatter_exchange([g_mo.reshape(N_DEV, e // N_DEV, d)]))
    dp1, dbias = _gates_bwd(gstat, colf, dp1, "gates_bwd")
    du1 = _matmul(dp1, w_p1, form="nt", m=t, n=d, kdim=n_p1, tm=tm_in, tn=d, tk=_pick(n_p1, 2048, LANES),
                  out_dtype=F32, name="mlstm_in_dx")
    g_p1 = _matmul(u1, dp1, form="tn", m=d, n=n_p1, kdim=t, tm=d, tn=_pick(n_p1, 2048, LANES), tk=tkw,
                   out_dtype=BF16, name="mlstm_in_dw")
    dh1, dnw1 = _rms_bwd(h1, norm_w[1:2], du1, dh2, "rms1_bwd")

    g_mi_a = _mlstm_grad_layout(g_p1, 0, d // 2, dk, dv, "mlstm_g_layout_a")
    g_mi_b = _mlstm_grad_layout(g_p1, d // 2, d // 2, dk, dv, "mlstm_g_layout_b")

    dg0 =_matmul(dh1, w_co, form="nt", m=t, n=e, kdim=d, tm=tm, tn=_pick(e, 2048, LANES), tk=d, out_dtype=BF16,
                  name="conv_out_dx")
    g_co = _matmul(g0, dh1, form="tn", m=e, n=d, kdim=t, tm=_pick(e, 1024, LANES), tn=d, tk=tkw, out_dtype=BF16,
                   name="conv_out_dw")
    (dp0, dconv), (r_mi_a,) = _conv_gate_bwd(
        p0, y0, dg0, conv_w_full, "conv_gate_bwd", exchange=_scatter_exchange([g_mi_a]))
    g_ci, (r_mi_b, r_co) = _matmul(
        u0, dp0, form="tn", m=d, n=8 * ec, kdim=t, tm=d, tn=ec, tk=tkw, out_dtype=BF16, name="conv_in_dw",
        out_shape=(N_DEV, d, ec), out_spec=pl.BlockSpec((None, d, ec), lambda i, j, k: (ci_map(j), 0, 0)),
        exchange=_scatter_exchange([g_mi_b, g_co.reshape(N_DEV, e // N_DEV, d)]))
    du0, (r_ci,) = _matmul(
        dp0, w_ci, form="nt", m=t, n=d, kdim=8 * ec, tm=tm_in, tn=d, tk=ec, out_dtype=F32, name="conv_in_dx",
        b_spec=pl.BlockSpec((None, d, ec), lambda i, j, k: (ci_map(k), 0, 0)),
        exchange=_scatter_exchange([g_ci]))
    grad_x, dmeta, dnw0 = _rms_bwd_first(h0, norm_w[0:1], du0, dh1, "rms0_bwd")
    grad_x = grad_x[None]

    row8 = lax.broadcasted_iota(jnp.int32, (8, 1), 0)
    loss_wide = jnp.pad(loss_part, ((0, 0), (0, d - LANES)))
    payload = jnp.concatenate([
        jnp.where(row8 == 0, dnw0, jnp.where(row8 == 1, dnw1, 0.0)),
        jnp.where(row8 == 0, dwf, jnp.where(row8 == 1, loss_wide, 0.0)),
        jnp.where(row8 == 0, jnp.pad(dbias, ((0, 0), (0, d - LANES))), 0.0),
        dmeta,
        _pad_rows(dconv[0:3].reshape(3 * e // d, d), 8),
        _pad_rows(dhead[:, 0, :].reshape(e // d, d), 8),
    ], axis=0)
    (payload_g,) = _run_exchange(_gather_exchange([payload]), "gather_small_grads")
    tot = _sum_devices(payload_g, "sum_small_grads")

    loss = tot[9, 0]
    g_norm = tot[0:2]
    g_final = tot[8]
    g_gate_b = tot[16:17, 0:2 * nh]
    g_meta = lax.dynamic_slice(tot[24:24 + N_META], (0, me * (d // N_DEV)), (N_META, d // N_DEV))
    g_conv_w = lax.dynamic_slice(tot[40:40 + 3 * e // d].reshape(3, e), (0, me * (e // N_DEV)), (3, e // N_DEV))
    g_head = lax.dynamic_slice(tot[48:48 + e // d].reshape(1, e), (0, me * (e // N_DEV)), (1, e // N_DEV))

    g1, d1, nm1, nv1 = _adamw_sharded([r_ci], conv_in_w[0], m_conv_in_w[0], v_conv_in_w[0], "adamw_conv_in")
    g2, d2, nm2, nv2 = _adamw_sharded([r_co], conv_out_w[0], m_conv_out_w[0], v_conv_out_w[0], "adamw_conv_out")
    g3, d3, nm3, nv3 = _adamw_sharded([r_mi_a, r_mi_b], mlstm_in_w[0], m_mlstm_in_w[0], v_mlstm_in_w[0],
                                      "adamw_mlstm_in")
    g4, d4, nm4, nv4 = _adamw_sharded([r_mo], mlstm_out_w[0], m_mlstm_out_w[0], v_mlstm_out_w[0],
                                      "adamw_mlstm_out")

    small_g = [g_meta, g_norm, g_conv_w, g_gate_b, g_head, g_final.reshape(1, d)]
    small_w = [meta_tokens, norm_w, conv_w[0], mlstm_gate_b, mlstm_head_norm_w, final_norm_w.reshape(1, d)]
    small_m = [m_meta_tokens, m_norm_w, m_conv_w[0], m_mlstm_gate_b, m_mlstm_head_norm_w, m_final_norm_w.reshape(1, d)]
    small_v = [v_meta_tokens, v_norm_w, v_conv_w[0], v_mlstm_gate_b, v_mlstm_head_norm_w, v_final_norm_w.reshape(1, d)]
    sd, snm, snv = _adamw_small(small_g, small_w, small_m, small_v, "adamw_small")

    def order(meta, norm, cin, cw, cout, min_, gb, hn, mout, fin):
        return (meta, norm, cin[None], cw[None], cout[None], min_[None], gb, hn, mout[None], fin.reshape(d))

    grads = order(g_meta, g_norm, g1, g_conv_w, g2, g3, g_gate_b, g_head, g4, g_final)
    deltas = order(sd[0], sd[1], d1, sd[2], d2, d3, sd[3], sd[4], d4, sd[5])
    new_m = order(snm[0], snm[1], nm1, snm[2], nm2, nm3, snm[3], snm[4], nm4, snm[5])
    new_v = order(snv[0], snv[1], nv1, snv[2], nv2, nv3, snv[3], snv[4], nv4, snv[5])
    return (loss, grad_x, *grads, *deltas, *new_m, *new_v)
```

```python
import functools

import jax
import jax.numpy as jnp
from jax import lax
from jax.experimental import pallas as pl
from jax.experimental.pallas import tpu as pltpu

F32 = jnp.float32
BF16 = jnp.bfloat16
MESH = pl.DeviceIdType.MESH

N_DEV = 8
N_META = 16
N_HEADS = 4
CHUNK = 64
CHUNKS_PER_STEP = 2
CHUNKS_PER_STEP_BWD = 5
LEAD = 128
PAD_ROWS = LEAD - N_META
RMS_EPS = 1e-6
NEG = -1e30
LANES = 128
VMEM_LIMIT = 56 * 1024 * 1024

ADAM_LR = 0.001
ADAM_B1 = 0.9
ADAM_B2 = 0.999
ADAM_EPS = 1e-08
ADAM_WD = 0.01
ADAM_STEP = 10

HIGHEST = lax.Precision.HIGHEST


def _pick(n, target, mult):
    best = None
    for d in range(mult, min(n, target) + 1, mult):
        if n % d == 0:
            best = d
    return n if best is None else best


def _params(**kw):
    return pltpu.CompilerParams(vmem_limit_bytes=VMEM_LIMIT, **kw)


def _sigmoid(x):
    return 0.5 * jnp.tanh(0.5 * x) + 0.5


STRIP_ROWS = 16
STRIP_COLS = 256


def _strips(n_rows, n_cols):
    for c0 in range(0, n_cols, STRIP_COLS):
        for r0 in range(0, n_rows, STRIP_ROWS):
            yield slice(r0, r0 + STRIP_ROWS), slice(c0, min(c0 + STRIP_COLS, n_cols))


def _shift(sl, by):
    return slice(sl.start + by, sl.stop + by)


def _call(body, *, name, grid, in_specs, out_specs, out_shape, args, scratch_shapes=(), aliases=None,
          exchange=None):
    aliases = {} if aliases is None else aliases
    if exchange is None:
        return pl.pallas_call(
            body, name=name, grid=grid, in_specs=list(in_specs), out_specs=out_specs, out_shape=out_shape,
            scratch_shapes=list(scratch_shapes), input_output_aliases=aliases,
            compiler_params=_params())(*args)
    ex = exchange
    single = not isinstance(out_shape, (list, tuple))
    shapes = [out_shape] if single else list(out_shape)
    specs = [out_specs] if single else list(out_specs)
    n_in, n_out, n_scr = len(in_specs), len(shapes), len(scratch_shapes)
    n_ex_in, n_ex_out = len(ex.ins), len(ex.out_shapes)
    steps = 1
    for size in grid:
        steps *= size

    def wrapped(*refs):
        own_in, ex_in = refs[:n_in], refs[n_in:n_in + n_ex_in]
        at = n_in + n_ex_in
        own_out, ex_out = refs[at:at + n_out], refs[at + n_out:at + n_out + n_ex_out]
        at += n_out + n_ex_out
        own_scr, ex_scr = refs[at:at + n_scr], refs[at + n_scr:]
        step = 0
        for axis, size in enumerate(grid):
            step = step * size + pl.program_id(axis)

        @pl.when(step == 0)
        def _():
            ex.start(ex_in, ex_out, ex_scr)

        if ex.mid is not None:
            @pl.when(step == (3 * steps) // 4)
            def _():
                ex.mid(ex_in, ex_out, ex_scr)

        body(*own_in, *own_out, *own_scr)

        @pl.when(step == steps - 1)
        def _():
            ex.finish(ex_in, ex_out, ex_scr)

    any_spec = pl.BlockSpec(memory_space=pl.ANY)
    res = pl.pallas_call(
        wrapped, name=name, grid=grid,
        in_specs=list(in_specs) + [any_spec] * n_ex_in, out_specs=specs + [any_spec] * n_ex_out,
        out_shape=shapes + ex.out_shapes, scratch_shapes=list(scratch_shapes) + ex.scratch,
        input_output_aliases=aliases, compiler_params=_params())(*args, *ex.ins)
    return (res[0] if single else res[:n_out]), res[n_out:]


def _matmul(a, b, *, form, m, n, kdim, tm, tn, tk, out_dtype, name,
            a_spec=None, b_spec=None, out_spec=None, out_shape=None, residual=None, exchange=None):
    ni, nj, nk = m // tm, n // tn, kdim // tk
    assert ni * tm == m and nj * tn == n and nk * tk == kdim, (name, m, n, kdim, tm, tn, tk)
    if form == "nn":
        dn = (((1,), (0,)), ((), ()))
        a_def = pl.BlockSpec((tm, tk), lambda i, j, k: (i, k))
        b_def = pl.BlockSpec((tk, tn), lambda i, j, k: (k, j))
    elif form == "nt":
        dn = (((1,), (1,)), ((), ()))
        a_def = pl.BlockSpec((tm, tk), lambda i, j, k: (i, k))
        b_def = pl.BlockSpec((tn, tk), lambda i, j, k: (j, k))
    else:
        dn = (((0,), (0,)), ((), ()))
        a_def = pl.BlockSpec((tk, tm), lambda i, j, k: (k, i))
        b_def = pl.BlockSpec((tk, tn), lambda i, j, k: (k, j))
    a_spec = a_def if a_spec is None else a_spec
    b_spec = b_def if b_spec is None else b_spec
    o_spec = pl.BlockSpec((tm, tn), lambda i, j, k: (i, j)) if out_spec is None else out_spec
    has_res = residual is not None

    def body(*refs):
        a_ref, b_ref = refs[:2]
        r_ref = refs[2] if has_res else None
        o_ref = refs[2 + has_res]

        def product():
            return lax.dot_general(a_ref[...].astype(BF16), b_ref[...].astype(BF16), dn,
                                   preferred_element_type=F32)

        def finish(acc):
            if has_res:
                acc = acc + r_ref[...].astype(F32)
            o_ref[...] = acc.astype(o_ref.dtype)

        if nk == 1:
            finish(product())
        else:
            acc_ref = refs[3 + has_res]
            k = pl.program_id(2)

            @pl.when(k == 0)
            def _():
                acc_ref[...] = jnp.zeros_like(acc_ref)

            acc_ref[...] += product()

            @pl.when(k == nk - 1)
            def _():
                finish(acc_ref[...])

    in_specs = [a_spec, b_spec]
    args = [a, b]
    if has_res:
        in_specs.append(pl.BlockSpec((tm, tn), lambda i, j, k: (i, j)))
        args.append(residual)
    return _call(
        body, name=name, grid=(ni, nj, nk), in_specs=in_specs, out_specs=o_spec,
        out_shape=jax.ShapeDtypeStruct((m, n) if out_shape is None else out_shape, out_dtype),
        scratch_shapes=[] if nk == 1 else [pltpu.VMEM((tm, tn), F32)], args=args, exchange=exchange)


def _rms_fwd(h, w, name, exchange=None):
    t, d = h.shape
    tm = _pick(t, 832, 16)

    def body(h_ref, w_ref, u_ref):
        x = h_ref[...]
        r = lax.rsqrt(jnp.mean(x * x, axis=-1, keepdims=True) + RMS_EPS)
        u_ref[...] = ((x * r) * w_ref[...]).astype(BF16)

    return _call(
        body, name=name, grid=(t // tm,),
        in_specs=[pl.BlockSpec((tm, d), lambda i: (i, 0)), pl.BlockSpec((1, d), lambda i: (0, 0))],
        out_specs=pl.BlockSpec((tm, d), lambda i: (i, 0)),
        out_shape=jax.ShapeDtypeStruct((t, d), BF16), args=(h, w), exchange=exchange)


def _rms_first(x, meta, w, name, exchange=None):
    seq, d = x.shape
    t = seq + LEAD
    tm = LEAD

    def body(x_ref, meta_ref, w_ref, h_ref, u_ref):
        i = pl.program_id(0)

        @pl.when(i == 0)
        def _():
            h_ref[0:PAD_ROWS, :] = jnp.zeros((PAD_ROWS, d), F32)
            h_ref[PAD_ROWS:LEAD, :] = meta_ref[...]

        @pl.when(i > 0)
        def _():
            h_ref[...] = x_ref[...]

        hv = h_ref[...]
        r = lax.rsqrt(jnp.mean(hv * hv, axis=-1, keepdims=True) + RMS_EPS)
        u_ref[...] = ((hv * r) * w_ref[...]).astype(BF16)

    row = pl.BlockSpec((tm, d), lambda i: (i, 0))
    return _call(
        body, name=name, grid=(t // tm,),
        in_specs=[pl.BlockSpec((tm, d), lambda i: (jnp.maximum(i - 1, 0), 0)),
                  pl.BlockSpec((N_META, d), lambda i: (0, 0)), pl.BlockSpec((1, d), lambda i: (0, 0))],
        out_specs=[row, row],
        out_shape=[jax.ShapeDtypeStruct((t, d), F32), jax.ShapeDtypeStruct((t, d), BF16)],
        args=(x, meta, w), exchange=exchange)


def _rms_bwd(h, w, du, dres, name):
    t, d = h.shape
    tm = _pick(t, 832, 16)

    def body(h_ref, w_ref, du_ref, dres_ref, dh_ref, dw_ref):
        i = pl.program_id(0)
        x = h_ref[...]
        g = du_ref[...].astype(F32)
        r = lax.rsqrt(jnp.mean(x * x, axis=-1, keepdims=True) + RMS_EPS)
        gw = g * w_ref[...]
        dot = jnp.mean(gw * x, axis=-1, keepdims=True)
        dh_ref[...] = dres_ref[...] + (r * gw - x * ((r * r * r) * dot))
        part = jnp.sum(g * (x * r), axis=0, keepdims=True)

        @pl.when(i == 0)
        def _():
            dw_ref[...] = jnp.zeros_like(dw_ref)

        dw_ref[...] += jnp.broadcast_to(part, dw_ref.shape)

    row = pl.BlockSpec((tm, d), lambda i: (i, 0))
    return pl.pallas_call(
        body, name=name, grid=(t // tm,),
        in_specs=[row, pl.BlockSpec((1, d), lambda i: (0, 0)), row, row],
        out_specs=[row, pl.BlockSpec((8, d), lambda i: (0, 0))],
        out_shape=[jax.ShapeDtypeStruct((t, d), F32), jax.ShapeDtypeStruct((8, d), F32)],
        compiler_params=_params(),
    )(h, w, du, dres)


def _rms_bwd_first(h, w, du, dres, name):
    t, d = h.shape
    sub = LEAD
    per = _pick((t - LEAD) // sub, 8, 1)
    tmo = per * sub
    n_out = (t - LEAD) // tmo
    n_blocks = t // sub

    def body(*refs):
        w_ref = refs[0]
        h_refs, du_refs, dres_refs = refs[1:1 + per], refs[1 + per:1 + 2 * per], refs[1 + 2 * per:1 + 3 * per]
        gx_ref, dmeta_ref, dw_ref = refs[1 + 3 * per:]
        i = pl.program_id(0)

        def piece(q):
            x = h_refs[q][...]
            g = du_refs[q][...].astype(F32)
            r = lax.rsqrt(jnp.mean(x * x, axis=-1, keepdims=True) + RMS_EPS)
            gw = g * w_ref[...]
            dot = jnp.mean(gw * x, axis=-1, keepdims=True)
            dh = dres_refs[q][...] + (r * gw - x * ((r * r * r) * dot))
            return dh, jnp.sum(g * (x * r), axis=0, keepdims=True)

        @pl.when(i == 0)
        def _():
            dh, part = piece(0)
            dmeta_ref[...] = dh[PAD_ROWS:LEAD, :]
            dw_ref[...] = jnp.broadcast_to(part, dw_ref.shape)

        @pl.when(i > 0)
        def _():
            for q in range(per):
                dh, part = piece(q)
                gx_ref[q * sub:(q + 1) * sub, :] = dh
                dw_ref[...] += jnp.broadcast_to(part, dw_ref.shape)

    def piece_spec(q):
        return pl.BlockSpec((sub, d), lambda i: (jnp.clip(1 + per * (i - 1) + q, 0, n_blocks - 1), 0))

    pieces = [piece_spec(q) for q in range(per)]
    return pl.pallas_call(
        body, name=name, grid=(n_out + 1,),
        in_specs=[pl.BlockSpec((1, d), lambda i: (0, 0))] + pieces * 3,
        out_specs=[pl.BlockSpec((tmo, d), lambda i: (jnp.maximum(i - 1, 0), 0)),
                   pl.BlockSpec((N_META, d), lambda i: (0, 0)), pl.BlockSpec((8, d), lambda i: (0, 0))],
        out_shape=[jax.ShapeDtypeStruct((t - LEAD, d), F32), jax.ShapeDtypeStruct((N_META, d), F32),
                   jax.ShapeDtypeStruct((8, d), F32)],
        compiler_params=_params(),
    )(w, *([h] * per), *([du] * per), *([dres] * per))


def _final_loss(h, w, tgt, name):
    t, d = h.shape
    sub = LEAD
    per = _pick(t // sub, 5, 1)
    tm = per * sub

    def body(h_ref, w_ref, *rest):
        t_refs, (dh_ref, dw_ref, loss_ref) = rest[:per], rest[per:]
        i = pl.program_id(0)

        @pl.when(i == 0)
        def _():
            dw_ref[...] = jnp.zeros_like(dw_ref)
            loss_ref[...] = jnp.zeros_like(loss_ref)

        for q in range(per):
            sl = slice(q * sub, (q + 1) * sub)
            x = h_ref[sl, :]
            r = lax.rsqrt(jnp.mean(x * x, axis=-1, keepdims=True) + RMS_EPS)
            yn = x * r
            y = yn * w_ref[...]
            rows = i * tm + q * sub + lax.broadcasted_iota(jnp.int32, (sub, 1), 0)
            diff = jnp.where(rows >= LEAD, y - t_refs[q][...], 0.0)
            tile_loss = 0.5 * jnp.sum(jnp.mean(diff * diff, axis=-1, keepdims=True), axis=0, keepdims=True)
            dy = diff / d
            gw = dy * w_ref[...]
            dot = jnp.mean(gw * x, axis=-1, keepdims=True)
            dh_ref[sl, :] = r * gw - x * ((r * r * r) * dot)
            dw_ref[...] += jnp.broadcast_to(jnp.sum(dy * yn, axis=0, keepdims=True), dw_ref.shape)
            loss_ref[...] += jnp.broadcast_to(tile_loss, loss_ref.shape)

    row = pl.BlockSpec((tm, d), lambda i: (i, 0))
    piece = [pl.BlockSpec((sub, d), functools.partial(lambda i, q: (jnp.maximum(i * per + q - 1, 0), 0), q=q))
             for q in range(per)]
    return pl.pallas_call(
        body, name=name, grid=(t // tm,),
        in_specs=[row, pl.BlockSpec((1, d), lambda i: (0, 0))] + piece,
        out_specs=[row, pl.BlockSpec((8, d), lambda i: (0, 0)), pl.BlockSpec((8, LANES), lambda i: (0, 0))],
        out_shape=[jax.ShapeDtypeStruct((t, d), F32), jax.ShapeDtypeStruct((8, d), F32),
                   jax.ShapeDtypeStruct((8, LANES), F32)],
        compiler_params=_params(),
    )(h, w, *([tgt] * per))


def _conv_gate_fwd(p0, conv_w, name, exchange=None):
    t = p0.shape[0]
    ec = p0.shape[1] // 8
    tm = _pick(t, 832, 16)
    nt = t // tm

    def body(p_ref, w_ref, g_ref, y_ref, ext_ref):
        i = pl.program_id(1)

        @pl.when(i == 0)
        def _():
            ext_ref[0:8, :] = jnp.zeros((8, ec), F32)

        for rows, cols in _strips(tm, ec):
            cg = p_ref[rows, _shift(cols, ec)].astype(F32)
            xin = p_ref[rows, _shift(cols, 2 * ec)].astype(F32)
            ext_ref[_shift(rows, 8), cols] = cg * xin
        for rows, cols in _strips(tm, ec):
            y = (w_ref[0:1, cols] * ext_ref[_shift(rows, 6), cols] + w_ref[1:2, cols] * ext_ref[_shift(rows, 7), cols]
                 + w_ref[2:3, cols] * ext_ref[_shift(rows, 8), cols])
            bg = p_ref[rows, cols].astype(F32)
            z = p_ref[rows, _shift(cols, 3 * ec)].astype(F32)
            y_ref[rows, cols] = y.astype(BF16)
            g_ref[rows, cols] = ((z * _sigmoid(z)) * bg * y).astype(BF16)
        ext_ref[0:8, :] = ext_ref[tm:tm + 8, :]

    out = pl.BlockSpec((tm, ec), lambda j, i: (i, j))
    return _call(
        body, name=name, grid=(2, nt),
        in_specs=[pl.BlockSpec((tm, 4 * ec), lambda j, i: (i, j)), pl.BlockSpec((3, ec), lambda j, i: (0, j))],
        out_specs=[out, out],
        out_shape=[jax.ShapeDtypeStruct((t, 2 * ec), BF16)] * 2,
        scratch_shapes=[pltpu.VMEM((tm + 8, ec), F32)], args=(p0, conv_w), exchange=exchange)


def _conv_gate_bwd(p0, y, dg, conv_w, name, exchange=None):
    t = p0.shape[0]
    ec = p0.shape[1] // 8
    tm = _pick(t, 416, 16)
    nt = t // tm

    def body(p_ref, y_ref, dg_ref, w_ref, dp_ref, dw_ref, ext_ref):
        i = pl.program_id(1)

        @pl.when(i == 0)
        def _():
            ext_ref[tm:tm + 8, :] = jnp.zeros((8, ec), F32)
            dw_ref[...] = jnp.zeros_like(dw_ref)

        for rows, cols in _strips(tm, ec):
            bg = p_ref[rows, cols].astype(F32)
            z = p_ref[rows, _shift(cols, 3 * ec)].astype(F32)
            yv = y_ref[rows, cols].astype(F32)
            dgv = dg_ref[rows, cols].astype(F32)
            sig = _sigmoid(z)
            sz = z * sig
            dp_ref[rows, _shift(cols, 3 * ec)] = (dgv * bg * yv * (sig * (1.0 + z * (1.0 - sig)))).astype(BF16)
            dp_ref[rows, cols] = (dgv * sz * yv).astype(BF16)
            ext_ref[rows, cols] = dgv * sz * bg
        acc = None
        for rows, cols in _strips(tm, ec):
            if rows.start == 0:
                acc = [jnp.zeros((STRIP_ROWS, cols.stop - cols.start), F32) for _ in range(3)]
            dy = ext_ref[rows, cols]
            dy1 = ext_ref[_shift(rows, 1), cols]
            dy2 = ext_ref[_shift(rows, 2), cols]
            cg = p_ref[rows, _shift(cols, ec)].astype(F32)
            xin = p_ref[rows, _shift(cols, 2 * ec)].astype(F32)
            da = w_ref[0:1, cols] * dy2 + w_ref[1:2, cols] * dy1 + w_ref[2:3, cols] * dy
            dp_ref[rows, _shift(cols, ec)] = (da * xin).astype(BF16)
            dp_ref[rows, _shift(cols, 2 * ec)] = (da * cg).astype(BF16)
            a = cg * xin
            acc = [acc[0] + a * dy2, acc[1] + a * dy1, acc[2] + a * dy]
            if rows.stop == tm:
                for tap in range(3):
                    dw_ref[tap:tap + 1, cols] += jnp.sum(acc[tap], axis=0, keepdims=True)
        ext_ref[tm:tm + 8, :] = ext_ref[0:8, :]

    rev = lambda j, i: (nt - 1 - i, j)
    return _call(
        body, name=name, grid=(2, nt),
        in_specs=[pl.BlockSpec((tm, 4 * ec), rev), pl.BlockSpec((tm, ec), rev), pl.BlockSpec((tm, ec), rev),
                  pl.BlockSpec((3, ec), lambda j, i: (0, j))],
        out_specs=[pl.BlockSpec((tm, 4 * ec), rev), pl.BlockSpec((8, ec), lambda j, i: (0, j))],
        out_shape=[jax.ShapeDtypeStruct((t, 8 * ec), BF16), jax.ShapeDtypeStruct((8, 2 * ec), F32)],
        scratch_shapes=[pltpu.VMEM((tm + 8, ec), F32)], args=(p0, y, dg, conv_w), exchange=exchange)


def _log_sigmoid(x):
    return jnp.minimum(x, 0.0) - jnp.log(1.0 + jnp.exp(-jnp.abs(x)))


def _gates_fwd(graw, graw_t, bias_row, bias_col, name):
    t = graw.shape[0]
    tm = _pick(t, 640, 128)
    cpt = tm // CHUNK
    nh = N_HEADS

    def body(g_ref, gt_ref, br_ref, bc_ref, colf_ref, rowf_ref):
        i = pl.program_id(0)
        gc = g_ref[...] + br_ref[...]
        rows = i * tm + lax.broadcasted_iota(jnp.int32, (tm, 1), 0)
        live = rows >= PAD_ROWS
        lf = jnp.where(live, _log_sigmoid(gc), 0.0)
        li = jnp.where(live, gc, NEG)
        gt = gt_ref[...] + bc_ref[...]
        cols = i * tm + lax.broadcasted_iota(jnp.int32, (1, tm), 1)
        live_t = cols >= PAD_ROWS
        lf_t = jnp.where(live_t, _log_sigmoid(gt), 0.0)
        li_t = jnp.where(live_t, gt, NEG)
        ri = lax.broadcasted_iota(jnp.int32, (CHUNK, CHUNK), 0)
        ci = lax.broadcasted_iota(jnp.int32, (CHUNK, CHUNK), 1)
        lower = (ri >= ci).astype(F32)
        upper = (ri <= ci).astype(F32)
        lane = lax.broadcasted_iota(jnp.int32, (CHUNK, LANES), 1)
        sub = lax.broadcasted_iota(jnp.int32, (8, CHUNK), 0)
        for c in range(cpt):
            sl = slice(c * CHUNK, (c + 1) * CHUNK)
            b_all = jnp.dot(lower, lf[sl, :], precision=HIGHEST, preferred_element_type=F32)
            bt_all = jnp.dot(lf_t[:, sl], upper, precision=HIGHEST, preferred_element_type=F32)
            for h in range(nh):
                b = b_all[:, nh + h:nh + h + 1]
                r = li[sl, h:h + 1] - b
                pre = gc[sl, nh + h:nh + h + 1]
                colf_ref[h, sl, :] = jnp.where(lane == 0, b, jnp.where(lane == 1, r, jnp.where(lane == 2, pre, 0.0)))
                b_row = bt_all[nh + h:nh + h + 1, :]
                r_row = li_t[h:h + 1, sl] - b_row
                rowf_ref[h, c, :, 0:CHUNK] = jnp.where(sub == 0, r_row, jnp.where(sub == 1, b_row, 0.0))
                rowf_ref[h, c, :, CHUNK:LANES] = jnp.zeros((8, LANES - CHUNK), F32)

    return pl.pallas_call(
        body, name=name, grid=(t // tm,),
        in_specs=[pl.BlockSpec((tm, LANES), lambda i: (i, 0)), pl.BlockSpec((16, tm), lambda i: (0, i)),
                  pl.BlockSpec((1, LANES), lambda i: (0, 0)), pl.BlockSpec((16, 1), lambda i: (0, 0))],
        out_specs=[pl.BlockSpec((nh, tm, LANES), lambda i: (0, i, 0)),
                   pl.BlockSpec((nh, cpt, 8, LANES), lambda i: (0, i, 0, 0))],
        out_shape=[jax.ShapeDtypeStruct((nh, t, LANES), F32),
                   jax.ShapeDtypeStruct((nh, t // CHUNK, 8, LANES), F32)],
        compiler_params=_params(),
    )(graw, graw_t, bias_row, bias_col)


def _gates_bwd(gstat, colf, dp1, name):
    nh, t, _ = gstat.shape
    tm = _pick(t, 640, 128)
    cpt = tm // CHUNK
    nt = t // tm
    nc = t // CHUNK
    gate_block = dp1.shape[1] // LANES - 1

    def body(gs_ref, nx_ref, colf_ref, dp_any, dg_ref, db_ref):
        i = pl.program_id(0)
        ri = lax.broadcasted_iota(jnp.int32, (CHUNK, CHUNK), 0)
        ci = lax.broadcasted_iota(jnp.int32, (CHUNK, CHUNK), 1)
        upper = (ri <= ci).astype(F32)
        lane = lax.broadcasted_iota(jnp.int32, (CHUNK, LANES), 1)
        total = jnp.zeros((1, LANES), F32)
        for c in range(cpt):
            sl = slice(c * CHUNK, (c + 1) * CHUNK)
            rows = i * tm + c * CHUNK + lax.broadcasted_iota(jnp.int32, (CHUNK, 1), 0)
            live = rows >= PAD_ROWS
            acc = jnp.zeros((CHUNK, LANES), F32)
            for h in range(nh):
                blk = gs_ref[h, sl, :]
                rev = jnp.dot(upper, blk, precision=HIGHEST, preferred_element_type=F32)
                if c + 1 < cpt:
                    carry = gs_ref[h, (c + 1) * CHUNK:(c + 1) * CHUNK + 1, 2:3]
                else:
                    carry = jnp.where(i == nt - 1, 0.0, nx_ref[h, 0:1, 2:3])
                dlogf = rev[:, 0:1] + carry
                pre = colf_ref[h, sl, 2:3]
                dgf = jnp.where(live, dlogf * (1.0 - _sigmoid(pre)), 0.0)
                dgi = jnp.where(live, blk[:, 1:2], 0.0)
                acc = acc + jnp.where(lane == h, dgi, 0.0) + jnp.where(lane == nh + h, dgf, 0.0)
            dg_ref[sl, :] = acc.astype(BF16)
            total = total + jnp.sum(acc, axis=0, keepdims=True)

        @pl.when(i == 0)
        def _():
            db_ref[...] = jnp.zeros_like(db_ref)

        db_ref[...] += jnp.broadcast_to(total, db_ref.shape)

    return pl.pallas_call(
        body, name=name, grid=(nt,),
        in_specs=[pl.BlockSpec((nh, tm, LANES), lambda i: (0, i, 0)),
                  pl.BlockSpec((nh, CHUNK, LANES), lambda i: (0, jnp.minimum((i + 1) * cpt, nc - 1), 0)),
                  pl.BlockSpec((nh, tm, LANES), lambda i: (0, i, 0)),
                  pl.BlockSpec(memory_space=pl.ANY)],
        out_specs=[pl.BlockSpec((tm, LANES), lambda i: (i, gate_block)), pl.BlockSpec((8, LANES), lambda i: (0, 0))],
        out_shape=[jax.ShapeDtypeStruct(dp1.shape, dp1.dtype), jax.ShapeDtypeStruct((8, LANES), F32)],
        input_output_aliases={3: 0},
        compiler_params=_params(),
    )(gstat, gstat, colf, dp1)


NT_DIMS = (((1,), (1,)), ((), ()))
TN_DIMS = (((0,), (0,)), ((), ()))


def _dot(a, b):
    return jnp.dot(a.astype(BF16), b.astype(BF16), preferred_element_type=F32)


def _dot_nt(a, b):
    return lax.dot_general(a.astype(BF16), b.astype(BF16), NT_DIMS, preferred_element_type=F32)


def _dot_tn(a, b):
    return lax.dot_general(a.astype(BF16), b.astype(BF16), TN_DIMS, preferred_element_type=F32)


def _chunk_gates(colf_ref, rowf_ref, m_prev, width=CHUNK):
    b = colf_ref[:, 0:1]
    rcol = colf_ref[:, 1:2]
    rrow = rowf_ref[0:1, 0:width]
    ri = lax.broadcasted_iota(jnp.int32, (CHUNK, width), 0)
    ci = lax.broadcasted_iota(jnp.int32, (CHUNK, width), 1)
    log_d = jnp.where(ri >= ci, b + rrow, NEG)
    m_row = jnp.maximum(b + m_prev, jnp.max(log_d, axis=-1, keepdims=True))
    dmat = jnp.exp(log_d - m_row)
    inter = jnp.exp(b + m_prev - m_row)
    b_last = b[CHUNK - 1:CHUNK, :]
    log_w = rcol + b_last
    m_new = jnp.maximum(b_last + m_prev, jnp.max(log_w, axis=0, keepdims=True))
    decay = jnp.exp(b_last + m_prev - m_new)
    w = jnp.exp(log_w - m_new)
    return m_row, dmat, inter, m_new, decay, w


def _mlstm_fwd(p1, colf, rowf, dk, dv, name):
    t = p1.shape[0]
    nh = N_HEADS
    nc = t // CHUNK
    hw = 2 * dk + dv
    cps = CHUNKS_PER_STEP if nc % CHUNKS_PER_STEP == 0 else 1
    rows_per_step = cps * CHUNK

    def body(p_ref, colf_ref, rowf_ref, hh_ref, stat_ref, cs_ref, ns_ref, *state):
        c_refs, n_refs, m_refs = state[:nh], state[nh:2 * nh], state[2 * nh:]
        c = pl.program_id(0)

        @pl.when(c == 0)
        def _():
            for ref in state:
                ref[...] = jnp.zeros_like(ref)

        for cc in range(cps):
            rows = pl.ds(cc * CHUNK, CHUNK)
            for h in range(nh):
                head(p_ref.at[rows, pl.ds(h * hw, hw)], colf_ref.at[h, rows], rowf_ref.at[h, cc],
                     hh_ref.at[rows, pl.ds(h * dv, dv)], stat_ref.at[h, rows], cs_ref.at[h, cc],
                     ns_ref.at[h, cc], c_refs[h], n_refs[h], m_refs[h])

    def head(p_ref, colf_ref, rowf_ref, hh_ref, stat_ref, cs_ref, ns_ref, c_ref, n_ref, m_ref):
        m_prev = m_ref[...]
        n_prev = n_ref[...]
        c_prev = c_ref[...]
        cs_ref[...] = c_prev.astype(BF16)
        sub = lax.broadcasted_iota(jnp.int32, (8, dk), 0)
        ns_ref[...] = jnp.where(sub == 0, n_prev, jnp.where(sub == 1, m_prev, 0.0))

        q = p_ref[:, 0:dk]
        k = p_ref[:, dk:2 * dk]
        v = p_ref[:, 2 * dk:2 * dk + dv]
        m_row, dmat, inter, m_new, decay, w = _chunk_gates(colf_ref, rowf_ref, m_prev)
        s = _dot_nt(q, k) * dmat
        num = _dot(s, v) + inter * _dot(q, c_prev)
        den = jnp.sum(s, axis=-1, keepdims=True) + inter * jnp.sum(q.astype(F32) * n_prev, axis=-1, keepdims=True)
        denom = jnp.maximum(jnp.abs(den), jnp.exp(-m_row))
        hh_ref[...] = num * (1.0 / denom)
        lane = lax.broadcasted_iota(jnp.int32, (CHUNK, LANES), 1)
        stat_ref[...] = jnp.where(lane == 0, den, 0.0)

        wk = w * k.astype(F32)
        c_ref[...] = decay * c_prev + _dot_tn(wk, v)
        n_ref[...] = decay * n_prev + jnp.sum(wk, axis=0, keepdims=True)
        m_ref[...] = m_new

    return pl.pallas_call(
        body, name=name, grid=(nc // cps,),
        in_specs=[pl.BlockSpec((rows_per_step, nh * hw), lambda c: (c, 0)),
                  pl.BlockSpec((nh, rows_per_step, LANES), lambda c: (0, c, 0)),
                  pl.BlockSpec((nh, cps, 8, LANES), lambda c: (0, c, 0, 0))],
        out_specs=[pl.BlockSpec((rows_per_step, nh * dv), lambda c: (c, 0)),
                   pl.BlockSpec((nh, rows_per_step, LANES), lambda c: (0, c, 0)),
                   pl.BlockSpec((nh, cps, dk, dv), lambda c: (0, c, 0, 0)),
                   pl.BlockSpec((nh, cps, 8, dk), lambda c: (0, c, 0, 0))],
        out_shape=[jax.ShapeDtypeStruct((t, nh * dv), F32), jax.ShapeDtypeStruct((nh, t, LANES), F32),
                   jax.ShapeDtypeStruct((nh, nc, dk, dv), BF16), jax.ShapeDtypeStruct((nh, nc, 8, dk), F32)],
        scratch_shapes=([pltpu.VMEM((dk, dv), F32)] * nh + [pltpu.VMEM((1, dk), F32)] * nh
                        + [pltpu.VMEM((1, 1), F32)] * nh),
        compiler_params=_params(),
    )(p1, colf, rowf)


def _head_gate_fwd(hh, p1, head_w, dv, name):
    t = hh.shape[0]
    nh = N_HEADS
    e = nh * dv
    tm = _pick(t, 416, 16)

    def body(hh_ref, oz_ref, w_ref, hm_ref):
        for h in range(nh):
            cols = slice(h * dv, (h + 1) * dv)
            x = hh_ref[:, cols]
            o = oz_ref[:, 2 * h * dv:(2 * h + 1) * dv].astype(F32)
            z = oz_ref[:, (2 * h + 1) * dv:(2 * h + 2) * dv].astype(F32)
            r = lax.rsqrt(jnp.mean(x * x, axis=-1, keepdims=True) + RMS_EPS)
            hn = (x * r) * w_ref[:, cols]
            hm_ref[:, cols] = (hn * _sigmoid(o) * (z * _sigmoid(z))).astype(BF16)

    return pl.pallas_call(
        body, name=name, grid=(t // tm,),
        in_specs=[pl.BlockSpec((tm, e), lambda i: (i, 0)), pl.BlockSpec((tm, 2 * e), lambda i: (i, 1)),
                  pl.BlockSpec((1, e), lambda i: (0, 0))],
        out_specs=pl.BlockSpec((tm, e), lambda i: (i, 0)),
        out_shape=jax.ShapeDtypeStruct((t, e), BF16),
        compiler_params=_params(),
    )(hh, p1, head_w)


def _mlstm_bwd(p1, colf, rowf, head_w, hh, stat, csave, nsave, dhm, n_cols, dk, dv, name, exchange=None):
    t = p1.shape[0]
    nh = N_HEADS
    nc = t // CHUNK
    hw = 2 * dk + dv
    cps = next(n for n in (CHUNKS_PER_STEP_BWD, CHUNKS_PER_STEP, 1) if nc % n == 0)
    rows_per_step = cps * CHUNK

    def body(p_ref, colf_ref, rowf_ref, w_ref, hh_ref, stat_ref, cs_ref, ns_ref, dhm_ref,
             dp_ref, gs_ref, dw_ref, *state):
        dc_refs, dn_refs, dwacc_refs = state[:nh], state[nh:2 * nh], state[2 * nh:]
        c = pl.program_id(0)

        @pl.when(c == 0)
        def _():
            for ref in state:
                ref[...] = jnp.zeros_like(ref)

        for cc in reversed(range(cps)):
            rows = pl.ds(cc * CHUNK, CHUNK)
            for h in range(nh):
                cols = pl.ds(h * dv, dv)
                oz = pl.ds(nh * hw + h * 2 * dv, 2 * dv)
                head(p_ref.at[rows, pl.ds(h * hw, hw)], p_ref.at[rows, oz], colf_ref.at[h, rows], rowf_ref.at[h, cc],
                     w_ref.at[:, cols], hh_ref.at[rows, cols], stat_ref.at[h, rows], cs_ref.at[h, cc],
                     ns_ref.at[h, cc], dhm_ref.at[rows, cols], dp_ref.at[rows, pl.ds(h * hw, hw)],
                     dp_ref.at[rows, oz], gs_ref.at[h, rows], dwacc_refs[h], dc_refs[h], dn_refs[h])

        @pl.when(c == nc // cps - 1)
        def _():
            for h in range(nh):
                dw_ref[h] = dwacc_refs[h][...]

    def head(p_ref, oz_ref, colf_ref, rowf_ref, w_ref, hh_ref, stat_ref, cs_ref, ns_ref, dhm_ref,
             dp_ref, doz_ref, gs_ref, dw_ref, dc_ref, dn_ref):
        q = p_ref[:, 0:dk]
        k = p_ref[:, dk:2 * dk]
        v = p_ref[:, 2 * dk:2 * dk + dv]
        o = oz_ref[:, 0:dv].astype(F32)
        z = oz_ref[:, dv:2 * dv].astype(F32)
        qf = q.astype(F32)
        kf = k.astype(F32)
        n_prev = ns_ref[0:1, :]
        m_prev = ns_ref[1:2, 0:1]
        c_prev = cs_ref[...]
        m_row, dmat, inter, m_new, decay, w = _chunk_gates(colf_ref, rowf_ref, m_prev)

        hh = hh_ref[...]
        dhm_v = dhm_ref[...].astype(F32)
        so = _sigmoid(o)
        sg = _sigmoid(z)
        sz = z * sg
        r = lax.rsqrt(jnp.mean(hh * hh, axis=-1, keepdims=True) + RMS_EPS)
        hn = (hh * r) * w_ref[...]
        dhn = dhm_v * so * sz
        doz_ref[:, 0:dv] = (dhm_v * hn * sz * (so * (1.0 - so))).astype(BF16)
        doz_ref[:, dv:2 * dv] = (dhm_v * hn * so * (sg * (1.0 + z * (1.0 - sg)))).astype(BF16)
        dw_ref[...] += jnp.broadcast_to(jnp.sum(dhn * (hh * r), axis=0, keepdims=True), dw_ref.shape)
        gwn = dhn * w_ref[...]
        dhh = r * gwn - hh * ((r * r * r) * jnp.mean(gwn * hh, axis=-1, keepdims=True))

        den = stat_ref[:, 0:1]
        floor = jnp.exp(-m_row)
        denom = jnp.maximum(jnp.abs(den), floor)
        inv = 1.0 / denom
        dnum = dhh * inv
        hdot = jnp.sum(dhh * hh, axis=-1, keepdims=True)
        dden = jnp.where(jnp.abs(den) > floor, -(hdot * inv) * jnp.sign(den), 0.0)
        s = _dot_nt(q, k) * dmat
        dqk = (_dot_nt(dnum, v) + dden) * dmat
        dc_new = dc_ref[...]
        dn_new = dn_ref[...]
        idd = inter * dden
        dq = _dot(dqk, k) + inter * _dot_nt(dnum, c_prev) + idd * n_prev
        dkv = _dot_tn(dqk, q) + w * (_dot_nt(v, dc_new) + dn_new)
        dvv = _dot_tn(s, dnum) + w * _dot(k, dc_new)
        dp_ref[:, 0:dk] = dq.astype(BF16)
        dp_ref[:, dk:2 * dk] = dkv.astype(BF16)
        dp_ref[:, 2 * dk:2 * dk + dv] = dvv.astype(BF16)
        qdq = jnp.sum(qf * dq, axis=-1, keepdims=True)
        kdk = jnp.sum(kf * dkv, axis=-1, keepdims=True)
        dc_prev = decay * dc_new + _dot_tn(inter * qf, dnum)
        dn_prev = decay * dn_new + jnp.sum(idd * qf, axis=0, keepdims=True)
        dc_ref[...] = dc_prev
        dn_ref[...] = dn_prev
        cross = (jnp.sum(jnp.sum(c_prev.astype(F32) * dc_prev, axis=-1, keepdims=True), axis=0, keepdims=True)
                 + jnp.sum(n_prev * dn_prev, axis=-1, keepdims=True))
        lane = lax.broadcasted_iota(jnp.int32, (CHUNK, LANES), 1)
        gs_ref[...] = jnp.where(lane == 0, qdq - kdk, jnp.where(lane == 1, kdk, jnp.where(lane == 2, cross, 0.0)))

    ns = nc // cps
    rc = lambda c: (ns - 1 - c, 0)
    rc3 = lambda c: (0, ns - 1 - c, 0)
    rc4 = lambda c: (0, ns - 1 - c, 0, 0)
    return _call(
        body, name=name, grid=(ns,),
        in_specs=[pl.BlockSpec((rows_per_step, nh * (hw + 2 * dv)), rc),
                  pl.BlockSpec((nh, rows_per_step, LANES), rc3),
                  pl.BlockSpec((nh, cps, 8, LANES), rc4),
                  pl.BlockSpec((1, nh * dv), lambda c: (0, 0)),
                  pl.BlockSpec((rows_per_step, nh * dv), rc),
                  pl.BlockSpec((nh, rows_per_step, LANES), rc3),
                  pl.BlockSpec((nh, cps, dk, dv), rc4),
                  pl.BlockSpec((nh, cps, 8, dk), rc4),
                  pl.BlockSpec((rows_per_step, nh * dv), rc)],
        out_specs=[pl.BlockSpec((rows_per_step, nh * (hw + 2 * dv)), rc),
                   pl.BlockSpec((nh, rows_per_step, LANES), rc3),
                   pl.BlockSpec((nh, 8, dv), lambda c: (0, 0, 0))],
        out_shape=[jax.ShapeDtypeStruct((t, n_cols), BF16), jax.ShapeDtypeStruct((nh, t, LANES), F32),
                   jax.ShapeDtypeStruct((nh, 8, dv), F32)],
        scratch_shapes=([pltpu.VMEM((dk, dv), F32)] * nh + [pltpu.VMEM((1, dk), F32)] * nh
                        + [pltpu.VMEM((8, dv), F32)] * nh),
        args=(p1, colf, rowf, head_w, hh, stat, csave, nsave, dhm), exchange=exchange)


def _position():
    return lax.axis_index("x"), lax.axis_index("y"), lax.axis_index("c")


class _Exchange:
    def __init__(self, ins, out_shapes, start, mid, finish):
        n = len(ins)
        self.ins, self.out_shapes = list(ins), list(out_shapes)
        self.start, self.mid, self.finish = start, mid, finish
        self.scratch = [pltpu.SemaphoreType.DMA((n, 7)), pltpu.SemaphoreType.DMA((n, 7)),
                        pltpu.SemaphoreType.DMA((n,))]


def _gather_exchange(shards):
    n = len(shards)

    def plan(ins, outs, sems):
        send_sems, recv_sems, local_sems = sems
        x, y, c = _position()
        me, sibling = (x, y, c), (x, y, 1 - c)
        chips = [(1 - x, y), (x, 1 - y), (1 - x, 1 - y)]

        def copy(a, k, block, to, src=None):
            px, py, pc = block
            dst = outs[a].at[4 * px + 2 * py + pc]
            return pltpu.make_async_remote_copy(
                src_ref=dst if src is None else src, dst_ref=dst,
                send_sem=send_sems.at[a, k], recv_sem=recv_sems.at[a, k],
                device_id=to, device_id_type=MESH)

        def mine():
            return [pltpu.make_async_copy(ins[a], outs[a].at[4 * x + 2 * y + c], local_sems.at[a])
                    for a in range(n)]

        def first():
            out = []
            for a in range(n):
                out.append(copy(a, 0, me, sibling, src=ins[a]))
                out += [copy(a, 1 + j, me, (*chip, c), src=ins[a]) for j, chip in enumerate(chips)]
            return out

        def ici_in():
            return [copy(a, 1 + j, (*chip, c), me) for j, chip in enumerate(chips) for a in range(n)]

        def passed():
            return [copy(a, 4 + j, (*chip, c), sibling) for j, chip in enumerate(chips) for a in range(n)]

        def d2d_in():
            return ([copy(a, 0, sibling, me) for a in range(n)]
                    + [copy(a, 4 + j, (*chip, 1 - c), me) for j, chip in enumerate(chips) for a in range(n)])

        return mine, first, ici_in, passed, d2d_in

    def start(ins, outs, sems):
        mine, first, _, _, _ = plan(ins, outs, sems)
        for cp in mine() + first():
            cp.start()

    def mid(ins, outs, sems):
        _, _, ici_in, passed, _ = plan(ins, outs, sems)
        for arrived, onward in zip(ici_in(), passed()):
            arrived.wait_recv()
            onward.start()

    def finish(ins, outs, sems):
        mine, first, _, passed, d2d_in = plan(ins, outs, sems)
        for cp in d2d_in():
            cp.wait_recv()
        for cp in first() + passed():
            cp.wait_send()
        for cp in mine():
            cp.wait()

    shapes = [jax.ShapeDtypeStruct((N_DEV,) + s.shape, s.dtype) for s in shards]
    return _Exchange(shards, shapes, start, mid, finish)


def _scatter_exchange(fulls):
    n = len(fulls)

    def plan(ins, outs, sems):
        send_sems, recv_sems, local_sems = sems
        x, y, c = _position()
        my_slot = 4 * x + 2 * y + c

        def mine():
            return [pltpu.make_async_copy(ins[a].at[my_slot], outs[a].at[my_slot], local_sems.at[a])
                    for a in range(n)]

        def remote(arriving):
            out = []
            for kk in (1, 2, 4, 6, 3, 5, 7):
                kx, ky, kc = (kk >> 2) & 1, (kk >> 1) & 1, kk & 1
                px = 1 - x if kx else x
                py = 1 - y if ky else y
                pc = 1 - c if kc else c
                peer_slot = 4 * px + 2 * py + pc
                for a in range(n):
                    out.append(pltpu.make_async_remote_copy(
                        src_ref=ins[a].at[peer_slot], dst_ref=outs[a].at[peer_slot if arriving else my_slot],
                        send_sem=send_sems.at[a, kk - 1], recv_sem=recv_sems.at[a, kk - 1],
                        device_id=(px, py, pc), device_id_type=MESH))
            return out

        return mine, remote

    def start(ins, outs, sems):
        mine, remote = plan(ins, outs, sems)
        for cp in mine() + remote(False):
            cp.start()

    def finish(ins, outs, sems):
        mine, remote = plan(ins, outs, sems)
        for cp in remote(True):
            cp.wait_recv()
        for cp in remote(False):
            cp.wait_send()
        for cp in mine():
            cp.wait()

    shapes = [jax.ShapeDtypeStruct(f.shape, f.dtype) for f in fulls]
    return _Exchange(fulls, shapes, start, None, finish)


def _run_exchange(ex, name):
    n_in, n_out = len(ex.ins), len(ex.out_shapes)

    def body(*refs):
        ins, outs, sems = refs[:n_in], refs[n_in:n_in + n_out], refs[n_in + n_out:]
        ex.start(ins, outs, sems)
        if ex.mid is not None:
            ex.mid(ins, outs, sems)
        ex.finish(ins, outs, sems)

    any_spec = pl.BlockSpec(memory_space=pl.ANY)
    return pl.pallas_call(
        body, name=name,
        in_specs=[any_spec] * n_in, out_specs=[any_spec] * n_out,
        out_shape=ex.out_shapes, scratch_shapes=ex.scratch,
    )(*ex.ins)


def _natural_to_head_major(n0, dk, dv):
    nh = N_HEADS
    qk, e, hw = nh * dk, nh * dv, 2 * dk + dv
    if n0 < qk:
        h, off = divmod(n0, dk)
        return h * hw + off, True
    n1 = n0 - qk
    if n1 < qk:
        h, off = divmod(n1, dk)
        return h * hw + dk + off, False
    part, n3 = divmod(n1 - qk, e)
    h, off = divmod(n3, dv)
    if part == 0:
        return h * hw + 2 * dk + off, False
    return nh * hw + h * 2 * dv + (part - 1) * dv + off, False


def _mlstm_weight_layout(w_blocks, dk, dv, name):
    _, d, blk = w_blocks.shape
    nh = N_HEADS
    hw = 2 * dk + 3 * dv
    n_tiles = nh * hw // LANES
    tiles_per_block = d // LANES
    tr = _pick(d, 128, 16)
    scale = dk ** -0.5

    def body(w_ref, p_ref, g_ref):
        lane = lax.broadcasted_iota(jnp.int32, (tr, LANES), 1)

        def tile(s, r):
            return w_ref[s, :, r * LANES:(r + 1) * LANES].astype(F32)

        def last_col(s):
            return w_ref[s, :, d:d + 1].astype(F32)

        for tn in range(n_tiles):
            s, r = divmod(tn, tiles_per_block)
            if s == 0:
                val = tile(0, r)
            elif r == 0:
                from_prev = pltpu.roll(tile(s - 1, tiles_per_block - 1), s - 1, 1)
                from_here = pltpu.roll(tile(s, 0), s, 1)
                val = jnp.where(lane < s - 1, from_prev, jnp.where(lane == s - 1, last_col(s - 1), from_here))
            else:
                slab = jnp.concatenate([tile(s, r - 1), tile(s, r)], axis=1)
                val = pltpu.roll(slab, s, 1)[:, LANES:]
            at, is_q = _natural_to_head_major(tn * LANES, dk, dv)
            if is_q:
                val = val * scale
            p_ref[:, at:at + LANES] = val.astype(BF16)
        n_gate = 2 * nh
        s = N_DEV - 1
        gates = pltpu.roll(tile(s, tiles_per_block - 1), n_gate - 1, 1)
        gates = jnp.where(lane < n_gate - 1, gates, jnp.where(lane == n_gate - 1, last_col(s), 0.0)).astype(BF16)
        g_ref[...] = gates
        p_ref[:, nh * hw:nh * hw + LANES] = gates

    return pl.pallas_call(
        body, name=name, grid=(d // tr,),
        in_specs=[pl.BlockSpec((N_DEV, tr, blk), lambda i: (0, i, 0))],
        out_specs=[pl.BlockSpec((tr, nh * hw + LANES), lambda i: (i, 0)), pl.BlockSpec((tr, LANES), lambda i: (i, 0))],
        out_shape=[jax.ShapeDtypeStruct((d, nh * hw + LANES), BF16), jax.ShapeDtypeStruct((d, LANES), BF16)],
        compiler_params=_params(),
    )(w_blocks)


def _mlstm_grad_layout(g_p1, row0, n_rows, dk, dv, name):
    d = g_p1.shape[0]
    nh = N_HEADS
    hw = 2 * dk + 3 * dv
    n_tiles = nh * hw // LANES
    tiles_per_block = d // LANES
    tr = _pick(d, 128, 16)
    scale = dk ** -0.5

    def body(p_ref, o_ref):
        def natural(tn):
            if tn == n_tiles:
                return p_ref[:, nh * hw:nh * hw + LANES].astype(F32)
            at, is_q = _natural_to_head_major(tn * LANES, dk, dv)
            val = p_ref[:, at:at + LANES].astype(F32)
            return val * scale if is_q else val

        for s in range(N_DEV):
            for r in range(tiles_per_block):
                tn = s * tiles_per_block + r
                if s == 0:
                    val = natural(tn)
                else:
                    slab = jnp.concatenate([natural(tn), natural(tn + 1)], axis=1)
                    val = pltpu.roll(slab, 2 * LANES - s, 1)[:, :LANES]
                o_ref[s, :, r * LANES:(r + 1) * LANES] = val.astype(BF16)
            o_ref[s, :, d:d + 1] = natural((s + 1) * tiles_per_block)[:, s:s + 1].astype(BF16)

    first = row0 // tr
    return pl.pallas_call(
        body, name=name, grid=(n_rows // tr,),
        in_specs=[pl.BlockSpec((tr, nh * hw + LANES), lambda i: (i + first, 0))],
        out_specs=pl.BlockSpec((N_DEV, tr, d + 1), lambda i: (0, i, 0)),
        out_shape=jax.ShapeDtypeStruct((N_DEV, n_rows, d + 1), BF16),
        compiler_params=_params(),
    )(g_p1)


def _adamw_math(w, g, m, v):
    m = ADAM_B1 * m + (1.0 - ADAM_B1) * g
    v = ADAM_B2 * v + (1.0 - ADAM_B2) * (g * g)
    m_hat = m / (1.0 - ADAM_B1 ** ADAM_STEP)
    v_hat = v / (1.0 - ADAM_B2 ** ADAM_STEP)
    delta = -ADAM_LR * (m_hat / (jnp.sqrt(v_hat) + ADAM_EPS) + ADAM_WD * w)
    return delta, m, v


def _adamw_sharded(parts, w, m, v, name):
    n_parts = len(parts)
    _, rk, c = parts[0].shape
    r = n_parts * rk
    tr = _pick(rk, 128, 8)
    per = rk // tr

    def body(*refs):
        p_refs = refs[:n_parts]
        w_ref, m_ref, v_ref, g_ref, d_ref, nm_ref, nv_ref, gsum_ref = refs[n_parts:]
        i = pl.program_id(0)
        for k, p_ref in enumerate(p_refs):
            @pl.when(i // per == k)
            def _(p_ref=p_ref):
                g = p_ref[0].astype(F32)
                for s in range(1, N_DEV):
                    g = g + p_ref[s].astype(F32)
                gsum_ref[...] = g
        g = gsum_ref[...]
        delta, m_new, v_new = _adamw_math(w_ref[...], g, m_ref[...], v_ref[...])
        g_ref[...] = g
        d_ref[...] = delta
        nm_ref[...] = m_new
        nv_ref[...] = v_new

    def part_spec(k):
        return pl.BlockSpec((N_DEV, tr, c), lambda i: (0, jnp.clip(i - k * per, 0, per - 1), 0))

    blk = pl.BlockSpec((tr, c), lambda i: (i, 0))
    return pl.pallas_call(
        body, name=name, grid=(r // tr,),
        in_specs=[part_spec(k) for k in range(n_parts)] + [blk, blk, blk],
        out_specs=[blk] * 4,
        out_shape=[jax.ShapeDtypeStruct((r, c), F32)] * 4,
        scratch_shapes=[pltpu.VMEM((tr, c), F32)],
        compiler_params=_params(),
    )(*parts, w, m, v)


def _sum_devices(parts, name):
    _, r, c = parts.shape

    def body(p_ref, o_ref):
        g = p_ref[0]
        for s in range(1, N_DEV):
            g = g + p_ref[s]
        o_ref[...] = g

    return pl.pallas_call(
        body, name=name, out_shape=jax.ShapeDtypeStruct((r, c), F32), compiler_params=_params(),
    )(parts)


def _adamw_small(gs, ws, ms, vs, name):
    n = len(gs)

    def body(*refs):
        g_refs, w_refs, m_refs, v_refs = refs[:n], refs[n:2 * n], refs[2 * n:3 * n], refs[3 * n:4 * n]
        d_refs, nm_refs, nv_refs = refs[4 * n:5 * n], refs[5 * n:6 * n], refs[6 * n:7 * n]
        for a in range(n):
            delta, m_new, v_new = _adamw_math(w_refs[a][...], g_refs[a][...], m_refs[a][...], v_refs[a][...])
            d_refs[a][...] = delta
            nm_refs[a][...] = m_new
            nv_refs[a][...] = v_new

    shapes = [jax.ShapeDtypeStruct(w.shape, F32) for w in ws]
    outs = pl.pallas_call(
        body, name=name, out_shape=shapes * 3, compiler_params=_params(),
    )(*gs, *ws, *ms, *vs)
    return outs[:n], outs[n:2 * n], outs[2 * n:]


def _pad_rows(a, rows):
    return jnp.pad(a, ((0, rows - a.shape[0]), (0, 0)))


def kernel(x, meta_tokens, norm_w, conv_in_w, conv_w, conv_out_w, mlstm_in_w, mlstm_gate_b, mlstm_head_norm_w, mlstm_out_w, final_norm_w, loss_target, m_meta_tokens, m_norm_w, m_conv_in_w, m_conv_w, m_conv_out_w, m_mlstm_in_w, m_mlstm_gate_b, m_mlstm_head_norm_w, m_mlstm_out_w, m_final_norm_w, v_meta_tokens, v_norm_w, v_conv_in_w, v_conv_w, v_conv_out_w, v_mlstm_in_w, v_mlstm_gate_b, v_mlstm_head_norm_w, v_mlstm_out_w, v_final_norm_w):
    seq, d = x.shape[1], x.shape[2]
    t = seq + LEAD
    e = 2 * d
    ec = e // 2
    nh = N_HEADS
    dv = e // nh
    dk = dv // 2
    qk = nh * dk
    hw = 2 * dk + 3 * dv
    n_in = 2 * qk + 3 * e + 2 * nh
    n_in_s = n_in // N_DEV
    me = 4 * lax.axis_index("x") + 2 * lax.axis_index("y") + lax.axis_index("c")
    tm = _pick(t, 832, 16)
    tm_in = _pick(t, 1664, 16)
    tkw = _pick(t, 2080, 16)

    small = jnp.concatenate([
        meta_tokens,
        _pad_rows(conv_w[0].reshape(3 * (e // N_DEV) // LANES, LANES), 8),
        _pad_rows(mlstm_head_norm_w[0].reshape((e // N_DEV) // LANES, LANES), 8),
    ], axis=0) if d // N_DEV == LANES else None
    assert small is not None, "the packed small-weight block assumes d_model / 8 == 128"
    (small_g,) = _run_exchange(_gather_exchange([small]), "gather_small_weights")
    meta_full = jnp.transpose(small_g[:, 0:N_META, :], (1, 0, 2)).reshape(N_META, d)
    cw_rows = 3 * (e // N_DEV) // LANES
    conv_w_full = jnp.transpose(
        small_g[:, N_META:N_META + cw_rows, :].reshape(N_DEV, 3, e // N_DEV), (1, 0, 2)).reshape(3, e)
    hn_rows = (e // N_DEV) // LANES
    head_w_full = small_g[:, N_META + 8:N_META + 8 + hn_rows, :].reshape(1, e)

    tgt = loss_target[0]

    ci_map = lambda blk: 2 * (blk % 4) + blk // 4
    (h0, u0), (w_ci,) = _rms_first(x[0], meta_full, norm_w[0:1], "rms0",
                                   exchange=_gather_exchange([conv_in_w[0].astype(BF16)]))
    p0, (w_co, w_mi) = _matmul(
        u0, w_ci, form="nn", m=t, n=8 * ec, kdim=d, tm=tm_in, tn=ec, tk=d, out_dtype=BF16, name="conv_in",
        b_spec=pl.BlockSpec((None, d, ec), lambda i, j, k: (ci_map(j), 0, 0)),
        exchange=_gather_exchange([conv_out_w[0].astype(BF16), mlstm_in_w[0].astype(BF16)]))
    w_co = w_co.reshape(e, d)
    (g0, y0), (w_mo,) = _conv_gate_fwd(p0, conv_w_full, "conv_gate",
                                       exchange=_gather_exchange([mlstm_out_w[0].astype(BF16)]))
    w_mo = w_mo.reshape(e, d)
    h1 = _matmul(g0, w_co, form="nn", m=t, n=d, kdim=e, tm=tm, tn=d, tk=e, out_dtype=F32, name="conv_out",
                 residual=h0)

    assert n_in_s == d + 1 and w_mi.shape == (N_DEV, d, d + 1)
    w_p1, w_gate = _mlstm_weight_layout(w_mi, dk, dv, "mlstm_w_layout")
    w_gate_t = jnp.transpose(w_gate[:, 0:16])
    bias_row = jnp.pad(mlstm_gate_b, ((0, 0), (0, LANES - 2 * nh)))
    bias_col = jnp.pad(mlstm_gate_b.T, ((0, 16 - 2 * nh), (0, 0)))

    u1 = _rms_fwd(h1, norm_w[1:2], "rms1")
    p1 = _matmul(u1, w_p1, form="nn", m=t, n=nh * hw, kdim=d, tm=tm_in, tn=_pick(nh * hw, 2048, LANES), tk=d,
                 out_dtype=BF16, name="mlstm_in")
    graw = _matmul(u1, w_gate, form="nn", m=t, n=LANES, kdim=d, tm=tm, tn=LANES, tk=d, out_dtype=F32,
                   name="gates_col")
    graw_t = _matmul(w_gate_t, u1, form="nt", m=16, n=t, kdim=d, tm=16, tn=_pick(t, 1664, LANES), tk=d,
                     out_dtype=F32, name="gates_row")
    colf, rowf = _gates_fwd(graw, graw_t, bias_row, bias_col, "gates_fwd")
    hh, stat, csave, nsave = _mlstm_fwd(p1, colf, rowf, dk, dv, "mlstm_fwd")
    hm = _head_gate_fwd(hh, p1, head_w_full, dv, "head_gate")
    h2 =_matmul(hm, w_mo, form="nn", m=t, n=d, kdim=e, tm=tm, tn=d, tk=e, out_dtype=F32, name="mlstm_out",
                 residual=h1)

    dh2, dwf, loss_part = _final_loss(h2, final_norm_w.reshape(1, d), tgt, "final_loss")

    dhm = _matmul(dh2, w_mo, form="nt", m=t, n=e, kdim=d, tm=tm, tn=_pick(e, 2048, LANES), tk=d, out_dtype=BF16,
                  name="mlstm_out_dx")
    g_mo = _matmul(hm, dh2, form="tn", m=e, n=d, kdim=t, tm=_pick(e, 1024, LANES), tn=d, tk=tkw, out_dtype=BF16,
                   name="mlstm_out_dw")
    n_p1 = nh * hw + LANES
    (dp1, gstat, dhead), (r_mo,) = _mlstm_bwd(
        p1, colf, rowf, head_w_full, hh, stat, csave, nsave, dhm, n_p1, dk, dv, "mlstm_bwd",
        exchange=_scatter_exchange([g_mo.reshape(N_DEV, e // N_DEV, d)]))
    dp1, dbias = _gates_bwd(gstat, colf, dp1, "gates_bwd")
    du1 = _matmul(dp1, w_p1, form="nt", m=t, n=d, kdim=n_p1, tm=tm_in, tn=d, tk=_pick(n_p1, 2048, LANES),
                  out_dtype=F32, name="mlstm_in_dx")
    g_p1 = _matmul(u1, dp1, form="tn", m=d, n=n_p1, kdim=t, tm=d, tn=_pick(n_p1, 2048, LANES), tk=tkw,
                   out_dtype=BF16, name="mlstm_in_dw")
    dh1, dnw1 = _rms_bwd(h1, norm_w[1:2], du1, dh2, "rms1_bwd")

    g_mi_a = _mlstm_grad_layout(g_p1, 0, d // 2, dk, dv, "mlstm_g_layout_a")
    g_mi_b = _mlstm_grad_layout(g_p1, d // 2, d // 2, dk, dv, "mlstm_g_layout_b")

    dg0 =_matmul(dh1, w_co, form="nt", m=t, n=e, kdim=d, tm=tm, tn=_pick(e, 2048, LANES), tk=d, out_dtype=BF16,
                  name="conv_out_dx")
    g_co = _matmul(g0, dh1, form="tn", m=e, n=d, kdim=t, tm=_pick(e, 1024, LANES), tn=d, tk=tkw, out_dtype=BF16,
                   name="conv_out_dw")
    (dp0, dconv), (r_co, r_mi_a) = _conv_gate_bwd(
        p0, y0, dg0, conv_w_full, "conv_gate_bwd",
        exchange=_scatter_exchange([g_co.reshape(N_DEV, e // N_DEV, d), g_mi_a]))
    g_ci, (r_mi_b,) = _matmul(
        u0, dp0, form="tn", m=d, n=8 * ec, kdim=t, tm=d, tn=ec, tk=tkw, out_dtype=BF16, name="conv_in_dw",
        out_shape=(N_DEV, d, ec), out_spec=pl.BlockSpec((None, d, ec), lambda i, j, k: (ci_map(j), 0, 0)),
        exchange=_scatter_exchange([g_mi_b]))
    du0, (r_ci,) = _matmul(
        dp0, w_ci, form="nt", m=t, n=d, kdim=8 * ec, tm=tm_in, tn=d, tk=ec, out_dtype=F32, name="conv_in_dx",
        b_spec=pl.BlockSpec((None, d, ec), lambda i, j, k: (ci_map(k), 0, 0)),
        exchange=_scatter_exchange([g_ci]))
    grad_x, dmeta, dnw0 = _rms_bwd_first(h0, norm_w[0:1], du0, dh1, "rms0_bwd")
    grad_x = grad_x[None]

    row8 = lax.broadcasted_iota(jnp.int32, (8, 1), 0)
    loss_wide = jnp.pad(loss_part, ((0, 0), (0, d - LANES)))
    payload = jnp.concatenate([
        jnp.where(row8 == 0, dnw0, jnp.where(row8 == 1, dnw1, 0.0)),
        jnp.where(row8 == 0, dwf, jnp.where(row8 == 1, loss_wide, 0.0)),
        jnp.where(row8 == 0, jnp.pad(dbias, ((0, 0), (0, d - LANES))), 0.0),
        dmeta,
        _pad_rows(dconv[0:3].reshape(3 * e // d, d), 8),
        _pad_rows(dhead[:, 0, :].reshape(e // d, d), 8),
    ], axis=0)
    (payload_g,) = _run_exchange(_gather_exchange([payload]), "gather_small_grads")
    tot = _sum_devices(payload_g, "sum_small_grads")

    loss = tot[9, 0]
    g_norm = tot[0:2]
    g_final = tot[8]
    g_gate_b = tot[16:17, 0:2 * nh]
    g_meta = lax.dynamic_slice(tot[24:24 + N_META], (0, me * (d // N_DEV)), (N_META, d // N_DEV))
    g_conv_w = lax.dynamic_slice(tot[40:40 + 3 * e // d].reshape(3, e), (0, me * (e // N_DEV)), (3, e // N_DEV))
    g_head = lax.dynamic_slice(tot[48:48 + e // d].reshape(1, e), (0, me * (e // N_DEV)), (1, e // N_DEV))

    g1, d1, nm1, nv1 = _adamw_sharded([r_ci], conv_in_w[0], m_conv_in_w[0], v_conv_in_w[0], "adamw_conv_in")
    g2, d2, nm2, nv2 = _adamw_sharded([r_co], conv_out_w[0], m_conv_out_w[0], v_conv_out_w[0], "adamw_conv_out")
    g3, d3, nm3, nv3 = _adamw_sharded([r_mi_a, r_mi_b], mlstm_in_w[0], m_mlstm_in_w[0], v_mlstm_in_w[0],
                                      "adamw_mlstm_in")
    g4, d4, nm4, nv4 = _adamw_sharded([r_mo], mlstm_out_w[0], m_mlstm_out_w[0], v_mlstm_out_w[0],
                                      "adamw_mlstm_out")

    small_g = [g_meta, g_norm, g_conv_w, g_gate_b, g_head, g_final.reshape(1, d)]
    small_w = [meta_tokens, norm_w, conv_w[0], mlstm_gate_b, mlstm_head_norm_w, final_norm_w.reshape(1, d)]
    small_m = [m_meta_tokens, m_norm_w, m_conv_w[0], m_mlstm_gate_b, m_mlstm_head_norm_w, m_final_norm_w.reshape(1, d)]
    small_v = [v_meta_tokens, v_norm_w, v_conv_w[0], v_mlstm_gate_b, v_mlstm_head_norm_w, v_final_norm_w.reshape(1, d)]
    sd, snm, snv = _adamw_small(small_g, small_w, small_m, small_v, "adamw_small")

    def order(meta, norm, cin, cw, cout, min_, gb, hn, mout, fin):
        return (meta, norm, cin[None], cw[None], cout[None], min_[None], gb, hn, mout[None], fin.reshape(d))

    grads = order(g_meta, g_norm, g1, g_conv_w, g2, g3, g_gate_b, g_head, g4, g_final)
    deltas = order(sd[0], sd[1], d1, sd[2], d2, d3, sd[3], sd[4], d4, sd[5])
    new_m = order(snm[0], snm[1], nm1, snm[2], nm2, nm3, snm[3], snm[4], nm4, snm[5])
    new_v = order(snv[0], snv[1], nv1, snv[2], nv2, nv3, snv[3], snv[4], nv4, snv[5])
    return (loss, grad_x, *grads, *deltas, *new_m, *new_v)
```

```python
import functools

import jax
import jax.numpy as jnp
from jax import lax
from jax.experimental import pallas as pl
from jax.experimental.pallas import tpu as pltpu

F32 = jnp.float32
BF16 = jnp.bfloat16
MESH = pl.DeviceIdType.MESH

N_DEV = 8
N_META = 16
N_HEADS = 4
CHUNK = 64
CHUNKS_PER_STEP = 2
LEAD = 128
PAD_ROWS = LEAD - N_META
RMS_EPS = 1e-6
NEG = -1e30
LANES = 128
VMEM_LIMIT = 48 * 1024 * 1024

ADAM_LR = 0.001
ADAM_B1 = 0.9
ADAM_B2 = 0.999
ADAM_EPS = 1e-08
ADAM_WD = 0.01
ADAM_STEP = 10

HIGHEST = lax.Precision.HIGHEST


def _pick(n, target, mult):
    best = None
    for d in range(mult, min(n, target) + 1, mult):
        if n % d == 0:
            best = d
    return n if best is None else best


def _params(**kw):
    return pltpu.CompilerParams(vmem_limit_bytes=VMEM_LIMIT, **kw)


def _sigmoid(x):
    return 0.5 * jnp.tanh(0.5 * x) + 0.5


STRIP_ROWS = 16
STRIP_COLS = 256


def _strips(n_rows, n_cols):
    for c0 in range(0, n_cols, STRIP_COLS):
        for r0 in range(0, n_rows, STRIP_ROWS):
            yield slice(r0, r0 + STRIP_ROWS), slice(c0, min(c0 + STRIP_COLS, n_cols))


def _shift(sl, by):
    return slice(sl.start + by, sl.stop + by)


def _call(body, *, name, grid, in_specs, out_specs, out_shape, args, scratch_shapes=(), aliases=None,
          exchange=None):
    aliases = {} if aliases is None else aliases
    if exchange is None:
        return pl.pallas_call(
            body, name=name, grid=grid, in_specs=list(in_specs), out_specs=out_specs, out_shape=out_shape,
            scratch_shapes=list(scratch_shapes), input_output_aliases=aliases,
            compiler_params=_params())(*args)
    ex = exchange
    single = not isinstance(out_shape, (list, tuple))
    shapes = [out_shape] if single else list(out_shape)
    specs = [out_specs] if single else list(out_specs)
    n_in, n_out, n_scr = len(in_specs), len(shapes), len(scratch_shapes)
    n_ex_in, n_ex_out = len(ex.ins), len(ex.out_shapes)
    steps = 1
    for size in grid:
        steps *= size

    def wrapped(*refs):
        own_in, ex_in = refs[:n_in], refs[n_in:n_in + n_ex_in]
        at = n_in + n_ex_in
        own_out, ex_out = refs[at:at + n_out], refs[at + n_out:at + n_out + n_ex_out]
        at += n_out + n_ex_out
        own_scr, ex_scr = refs[at:at + n_scr], refs[at + n_scr:]
        step = 0
        for axis, size in enumerate(grid):
            step = step * size + pl.program_id(axis)

        @pl.when(step == 0)
        def _():
            ex.start(ex_in, ex_out, ex_scr)

        if ex.mid is not None:
            @pl.when(step == (3 * steps) // 4)
            def _():
                ex.mid(ex_in, ex_out, ex_scr)

        body(*own_in, *own_out, *own_scr)

        @pl.when(step == steps - 1)
        def _():
            ex.finish(ex_in, ex_out, ex_scr)

    any_spec = pl.BlockSpec(memory_space=pl.ANY)
    res = pl.pallas_call(
        wrapped, name=name, grid=grid,
        in_specs=list(in_specs) + [any_spec] * n_ex_in, out_specs=specs + [any_spec] * n_ex_out,
        out_shape=shapes + ex.out_shapes, scratch_shapes=list(scratch_shapes) + ex.scratch,
        input_output_aliases=aliases, compiler_params=_params())(*args, *ex.ins)
    return (res[0] if single else res[:n_out]), res[n_out:]


def _matmul(a, b, *, form, m, n, kdim, tm, tn, tk, out_dtype, name,
            a_spec=None, b_spec=None, out_spec=None, out_shape=None, residual=None, exchange=None):
    ni, nj, nk = m // tm, n // tn, kdim // tk
    assert ni * tm == m and nj * tn == n and nk * tk == kdim, (name, m, n, kdim, tm, tn, tk)
    if form == "nn":
        dn = (((1,), (0,)), ((), ()))
        a_def = pl.BlockSpec((tm, tk), lambda i, j, k: (i, k))
        b_def = pl.BlockSpec((tk, tn), lambda i, j, k: (k, j))
    elif form == "nt":
        dn = (((1,), (1,)), ((), ()))
        a_def = pl.BlockSpec((tm, tk), lambda i, j, k: (i, k))
        b_def = pl.BlockSpec((tn, tk), lambda i, j, k: (j, k))
    else:
        dn = (((0,), (0,)), ((), ()))
        a_def = pl.BlockSpec((tk, tm), lambda i, j, k: (k, i))
        b_def = pl.BlockSpec((tk, tn), lambda i, j, k: (k, j))
    a_spec = a_def if a_spec is None else a_spec
    b_spec = b_def if b_spec is None else b_spec
    o_spec = pl.BlockSpec((tm, tn), lambda i, j, k: (i, j)) if out_spec is None else out_spec
    has_res = residual is not None

    def body(*refs):
        a_ref, b_ref = refs[:2]
        r_ref = refs[2] if has_res else None
        o_ref = refs[2 + has_res]

        def product():
            return lax.dot_general(a_ref[...].astype(BF16), b_ref[...].astype(BF16), dn,
                                   preferred_element_type=F32)

        def finish(acc):
            if has_res:
                acc = acc + r_ref[...].astype(F32)
            o_ref[...] = acc.astype(o_ref.dtype)

        if nk == 1:
            finish(product())
        else:
            acc_ref = refs[3 + has_res]
            k = pl.program_id(2)

            @pl.when(k == 0)
            def _():
                acc_ref[...] = jnp.zeros_like(acc_ref)

            acc_ref[...] += product()

            @pl.when(k == nk - 1)
            def _():
                finish(acc_ref[...])

    in_specs = [a_spec, b_spec]
    args = [a, b]
    if has_res:
        in_specs.append(pl.BlockSpec((tm, tn), lambda i, j, k: (i, j)))
        args.append(residual)
    return _call(
        body, name=name, grid=(ni, nj, nk), in_specs=in_specs, out_specs=o_spec,
        out_shape=jax.ShapeDtypeStruct((m, n) if out_shape is None else out_shape, out_dtype),
        scratch_shapes=[] if nk == 1 else [pltpu.VMEM((tm, tn), F32)], args=args, exchange=exchange)


def _rms_fwd(h, w, name, exchange=None):
    t, d = h.shape
    tm = _pick(t, 832, 16)

    def body(h_ref, w_ref, u_ref):
        x = h_ref[...]
        r = lax.rsqrt(jnp.mean(x * x, axis=-1, keepdims=True) + RMS_EPS)
        u_ref[...] = ((x * r) * w_ref[...]).astype(BF16)

    return _call(
        body, name=name, grid=(t // tm,),
        in_specs=[pl.BlockSpec((tm, d), lambda i: (i, 0)), pl.BlockSpec((1, d), lambda i: (0, 0))],
        out_specs=pl.BlockSpec((tm, d), lambda i: (i, 0)),
        out_shape=jax.ShapeDtypeStruct((t, d), BF16), args=(h, w), exchange=exchange)


def _rms_first(x, meta, w, name, exchange=None):
    seq, d = x.shape
    t = seq + LEAD
    tm = LEAD

    def body(x_ref, meta_ref, w_ref, h_ref, u_ref):
        i = pl.program_id(0)

        @pl.when(i == 0)
        def _():
            h_ref[0:PAD_ROWS, :] = jnp.zeros((PAD_ROWS, d), F32)
            h_ref[PAD_ROWS:LEAD, :] = meta_ref[...]

        @pl.when(i > 0)
        def _():
            h_ref[...] = x_ref[...]

        hv = h_ref[...]
        r = lax.rsqrt(jnp.mean(hv * hv, axis=-1, keepdims=True) + RMS_EPS)
        u_ref[...] = ((hv * r) * w_ref[...]).astype(BF16)

    row = pl.BlockSpec((tm, d), lambda i: (i, 0))
    return _call(
        body, name=name, grid=(t // tm,),
        in_specs=[pl.BlockSpec((tm, d), lambda i: (jnp.maximum(i - 1, 0), 0)),
                  pl.BlockSpec((N_META, d), lambda i: (0, 0)), pl.BlockSpec((1, d), lambda i: (0, 0))],
        out_specs=[row, row],
        out_shape=[jax.ShapeDtypeStruct((t, d), F32), jax.ShapeDtypeStruct((t, d), BF16)],
        args=(x, meta, w), exchange=exchange)


def _rms_bwd(h, w, du, dres, name):
    t, d = h.shape
    tm = _pick(t, 832, 16)

    def body(h_ref, w_ref, du_ref, dres_ref, dh_ref, dw_ref):
        i = pl.program_id(0)
        x = h_ref[...]
        g = du_ref[...].astype(F32)
        r = lax.rsqrt(jnp.mean(x * x, axis=-1, keepdims=True) + RMS_EPS)
        gw = g * w_ref[...]
        dot = jnp.mean(gw * x, axis=-1, keepdims=True)
        dh_ref[...] = dres_ref[...] + (r * gw - x * ((r * r * r) * dot))
        part = jnp.sum(g * (x * r), axis=0, keepdims=True)

        @pl.when(i == 0)
        def _():
            dw_ref[...] = jnp.zeros_like(dw_ref)

        dw_ref[...] += jnp.broadcast_to(part, dw_ref.shape)

    row = pl.BlockSpec((tm, d), lambda i: (i, 0))
    return pl.pallas_call(
        body, name=name, grid=(t // tm,),
        in_specs=[row, pl.BlockSpec((1, d), lambda i: (0, 0)), row, row],
        out_specs=[row, pl.BlockSpec((8, d), lambda i: (0, 0))],
        out_shape=[jax.ShapeDtypeStruct((t, d), F32), jax.ShapeDtypeStruct((8, d), F32)],
        compiler_params=_params(),
    )(h, w, du, dres)


def _rms_bwd_first(h, w, du, dres, name):
    t, d = h.shape
    sub = LEAD
    per = _pick((t - LEAD) // sub, 8, 1)
    tmo = per * sub
    n_out = (t - LEAD) // tmo
    n_blocks = t // sub

    def body(*refs):
        w_ref = refs[0]
        h_refs, du_refs, dres_refs = refs[1:1 + per], refs[1 + per:1 + 2 * per], refs[1 + 2 * per:1 + 3 * per]
        gx_ref, dmeta_ref, dw_ref = refs[1 + 3 * per:]
        i = pl.program_id(0)

        def piece(q):
            x = h_refs[q][...]
            g = du_refs[q][...].astype(F32)
            r = lax.rsqrt(jnp.mean(x * x, axis=-1, keepdims=True) + RMS_EPS)
            gw = g * w_ref[...]
            dot = jnp.mean(gw * x, axis=-1, keepdims=True)
            dh = dres_refs[q][...] + (r * gw - x * ((r * r * r) * dot))
            return dh, jnp.sum(g * (x * r), axis=0, keepdims=True)

        @pl.when(i == 0)
        def _():
            dh, part = piece(0)
            dmeta_ref[...] = dh[PAD_ROWS:LEAD, :]
            dw_ref[...] = jnp.broadcast_to(part, dw_ref.shape)

        @pl.when(i > 0)
        def _():
            for q in range(per):
                dh, part = piece(q)
                gx_ref[q * sub:(q + 1) * sub, :] = dh
                dw_ref[...] += jnp.broadcast_to(part, dw_ref.shape)

    def piece_spec(q):
        return pl.BlockSpec((sub, d), lambda i: (jnp.clip(1 + per * (i - 1) + q, 0, n_blocks - 1), 0))

    pieces = [piece_spec(q) for q in range(per)]
    return pl.pallas_call(
        body, name=name, grid=(n_out + 1,),
        in_specs=[pl.BlockSpec((1, d), lambda i: (0, 0))] + pieces * 3,
        out_specs=[pl.BlockSpec((tmo, d), lambda i: (jnp.maximum(i - 1, 0), 0)),
                   pl.BlockSpec((N_META, d), lambda i: (0, 0)), pl.BlockSpec((8, d), lambda i: (0, 0))],
        out_shape=[jax.ShapeDtypeStruct((t - LEAD, d), F32), jax.ShapeDtypeStruct((N_META, d), F32),
                   jax.ShapeDtypeStruct((8, d), F32)],
        compiler_params=_params(),
    )(w, *([h] * per), *([du] * per), *([dres] * per))


def _final_loss(h, w, tgt, name):
    t, d = h.shape
    sub = LEAD
    per = _pick(t // sub, 5, 1)
    tm = per * sub

    def body(h_ref, w_ref, *rest):
        t_refs, (dh_ref, dw_ref, loss_ref) = rest[:per], rest[per:]
        i = pl.program_id(0)

        @pl.when(i == 0)
        def _():
            dw_ref[...] = jnp.zeros_like(dw_ref)
            loss_ref[...] = jnp.zeros_like(loss_ref)

        for q in range(per):
            sl = slice(q * sub, (q + 1) * sub)
            x = h_ref[sl, :]
            r = lax.rsqrt(jnp.mean(x * x, axis=-1, keepdims=True) + RMS_EPS)
            yn = x * r
            y = yn * w_ref[...]
            rows = i * tm + q * sub + lax.broadcasted_iota(jnp.int32, (sub, 1), 0)
            diff = jnp.where(rows >= LEAD, y - t_refs[q][...], 0.0)
            tile_loss = 0.5 * jnp.sum(jnp.mean(diff * diff, axis=-1, keepdims=True), axis=0, keepdims=True)
            dy = diff / d
            gw = dy * w_ref[...]
            dot = jnp.mean(gw * x, axis=-1, keepdims=True)
            dh_ref[sl, :] = r * gw - x * ((r * r * r) * dot)
            dw_ref[...] += jnp.broadcast_to(jnp.sum(dy * yn, axis=0, keepdims=True), dw_ref.shape)
            loss_ref[...] += jnp.broadcast_to(tile_loss, loss_ref.shape)

    row = pl.BlockSpec((tm, d), lambda i: (i, 0))
    piece = [pl.BlockSpec((sub, d), functools.partial(lambda i, q: (jnp.maximum(i * per + q - 1, 0), 0), q=q))
             for q in range(per)]
    return pl.pallas_call(
        body, name=name, grid=(t // tm,),
        in_specs=[row, pl.BlockSpec((1, d), lambda i: (0, 0))] + piece,
        out_specs=[row, pl.BlockSpec((8, d), lambda i: (0, 0)), pl.BlockSpec((8, LANES), lambda i: (0, 0))],
        out_shape=[jax.ShapeDtypeStruct((t, d), F32), jax.ShapeDtypeStruct((8, d), F32),
                   jax.ShapeDtypeStruct((8, LANES), F32)],
        compiler_params=_params(),
    )(h, w, *([tgt] * per))


def _conv_gate_fwd(p0, conv_w, name, exchange=None):
    t = p0.shape[0]
    ec = p0.shape[1] // 8
    tm = _pick(t, 832, 16)
    nt = t // tm

    def body(p_ref, w_ref, g_ref, y_ref, ext_ref):
        i = pl.program_id(1)

        @pl.when(i == 0)
        def _():
            ext_ref[0:8, :] = jnp.zeros((8, ec), F32)

        for rows, cols in _strips(tm, ec):
            cg = p_ref[rows, _shift(cols, ec)].astype(F32)
            xin = p_ref[rows, _shift(cols, 2 * ec)].astype(F32)
            ext_ref[_shift(rows, 8), cols] = cg * xin
        for rows, cols in _strips(tm, ec):
            y = (w_ref[0:1, cols] * ext_ref[_shift(rows, 6), cols] + w_ref[1:2, cols] * ext_ref[_shift(rows, 7), cols]
                 + w_ref[2:3, cols] * ext_ref[_shift(rows, 8), cols])
            bg = p_ref[rows, cols].astype(F32)
            z = p_ref[rows, _shift(cols, 3 * ec)].astype(F32)
            y_ref[rows, cols] = y.astype(BF16)
            g_ref[rows, cols] = ((z * _sigmoid(z)) * bg * y).astype(BF16)
        ext_ref[0:8, :] = ext_ref[tm:tm + 8, :]

    out = pl.BlockSpec((tm, ec), lambda j, i: (i, j))
    return _call(
        body, name=name, grid=(2, nt),
        in_specs=[pl.BlockSpec((tm, 4 * ec), lambda j, i: (i, j)), pl.BlockSpec((3, ec), lambda j, i: (0, j))],
        out_specs=[out, out],
        out_shape=[jax.ShapeDtypeStruct((t, 2 * ec), BF16)] * 2,
        scratch_shapes=[pltpu.VMEM((tm + 8, ec), F32)], args=(p0, conv_w), exchange=exchange)


def _conv_gate_bwd(p0, y, dg, conv_w, name, exchange=None):
    t = p0.shape[0]
    ec = p0.shape[1] // 8
    tm = _pick(t, 416, 16)
    nt = t // tm

    def body(p_ref, y_ref, dg_ref, w_ref, dp_ref, dw_ref, ext_ref):
        i = pl.program_id(1)

        @pl.when(i == 0)
        def _():
            ext_ref[tm:tm + 8, :] = jnp.zeros((8, ec), F32)
            dw_ref[...] = jnp.zeros_like(dw_ref)

        for rows, cols in _strips(tm, ec):
            bg = p_ref[rows, cols].astype(F32)
            z = p_ref[rows, _shift(cols, 3 * ec)].astype(F32)
            yv = y_ref[rows, cols].astype(F32)
            dgv = dg_ref[rows, cols].astype(F32)
            sig = _sigmoid(z)
            sz = z * sig
            dp_ref[rows, _shift(cols, 3 * ec)] = (dgv * bg * yv * (sig * (1.0 + z * (1.0 - sig)))).astype(BF16)
            dp_ref[rows, cols] = (dgv * sz * yv).astype(BF16)
            ext_ref[rows, cols] = dgv * sz * bg
        acc = None
        for rows, cols in _strips(tm, ec):
            if rows.start == 0:
                acc = [jnp.zeros((STRIP_ROWS, cols.stop - cols.start), F32) for _ in range(3)]
            dy = ext_ref[rows, cols]
            dy1 = ext_ref[_shift(rows, 1), cols]
            dy2 = ext_ref[_shift(rows, 2), cols]
            cg = p_ref[rows, _shift(cols, ec)].astype(F32)
            xin = p_ref[rows, _shift(cols, 2 * ec)].astype(F32)
            da = w_ref[0:1, cols] * dy2 + w_ref[1:2, cols] * dy1 + w_ref[2:3, cols] * dy
            dp_ref[rows, _shift(cols, ec)] = (da * xin).astype(BF16)
            dp_ref[rows, _shift(cols, 2 * ec)] = (da * cg).astype(BF16)
            a = cg * xin
            acc = [acc[0] + a * dy2, acc[1] + a * dy1, acc[2] + a * dy]
            if rows.stop == tm:
                for tap in range(3):
                    dw_ref[tap:tap + 1, cols] += jnp.sum(acc[tap], axis=0, keepdims=True)
        ext_ref[tm:tm + 8, :] = ext_ref[0:8, :]

    rev = lambda j, i: (nt - 1 - i, j)
    return _call(
        body, name=name, grid=(2, nt),
        in_specs=[pl.BlockSpec((tm, 4 * ec), rev), pl.BlockSpec((tm, ec), rev), pl.BlockSpec((tm, ec), rev),
                  pl.BlockSpec((3, ec), lambda j, i: (0, j))],
        out_specs=[pl.BlockSpec((tm, 4 * ec), rev), pl.BlockSpec((8, ec), lambda j, i: (0, j))],
        out_shape=[jax.ShapeDtypeStruct((t, 8 * ec), BF16), jax.ShapeDtypeStruct((8, 2 * ec), F32)],
        scratch_shapes=[pltpu.VMEM((tm + 8, ec), F32)], args=(p0, y, dg, conv_w), exchange=exchange)


def _log_sigmoid(x):
    return jnp.minimum(x, 0.0) - jnp.log(1.0 + jnp.exp(-jnp.abs(x)))


def _gates_fwd(graw, graw_t, bias_row, bias_col, name):
    t = graw.shape[0]
    tm = _pick(t, 640, 128)
    cpt = tm // CHUNK
    nh = N_HEADS

    def body(g_ref, gt_ref, br_ref, bc_ref, colf_ref, rowf_ref):
        i = pl.program_id(0)
        gc = g_ref[...] + br_ref[...]
        rows = i * tm + lax.broadcasted_iota(jnp.int32, (tm, 1), 0)
        live = rows >= PAD_ROWS
        lf = jnp.where(live, _log_sigmoid(gc), 0.0)
        li = jnp.where(live, gc, NEG)
        gt = gt_ref[...] + bc_ref[...]
        cols = i * tm + lax.broadcasted_iota(jnp.int32, (1, tm), 1)
        live_t = cols >= PAD_ROWS
        lf_t = jnp.where(live_t, _log_sigmoid(gt), 0.0)
        li_t = jnp.where(live_t, gt, NEG)
        ri = lax.broadcasted_iota(jnp.int32, (CHUNK, CHUNK), 0)
        ci = lax.broadcasted_iota(jnp.int32, (CHUNK, CHUNK), 1)
        lower = (ri >= ci).astype(F32)
        upper = (ri <= ci).astype(F32)
        lane = lax.broadcasted_iota(jnp.int32, (CHUNK, LANES), 1)
        sub = lax.broadcasted_iota(jnp.int32, (8, CHUNK), 0)
        for c in range(cpt):
            sl = slice(c * CHUNK, (c + 1) * CHUNK)
            b_all = jnp.dot(lower, lf[sl, :], precision=HIGHEST, preferred_element_type=F32)
            bt_all = jnp.dot(lf_t[:, sl], upper, precision=HIGHEST, preferred_element_type=F32)
            for h in range(nh):
                b = b_all[:, nh + h:nh + h + 1]
                r = li[sl, h:h + 1] - b
                pre = gc[sl, nh + h:nh + h + 1]
                colf_ref[h, sl, :] = jnp.where(lane == 0, b, jnp.where(lane == 1, r, jnp.where(lane == 2, pre, 0.0)))
                b_row = bt_all[nh + h:nh + h + 1, :]
                r_row = li_t[h:h + 1, sl] - b_row
                rowf_ref[h, c, :, 0:CHUNK] = jnp.where(sub == 0, r_row, jnp.where(sub == 1, b_row, 0.0))
                rowf_ref[h, c, :, CHUNK:LANES] = jnp.zeros((8, LANES - CHUNK), F32)

    return pl.pallas_call(
        body, name=name, grid=(t // tm,),
        in_specs=[pl.BlockSpec((tm, LANES), lambda i: (i, 0)), pl.BlockSpec((16, tm), lambda i: (0, i)),
                  pl.BlockSpec((1, LANES), lambda i: (0, 0)), pl.BlockSpec((16, 1), lambda i: (0, 0))],
        out_specs=[pl.BlockSpec((nh, tm, LANES), lambda i: (0, i, 0)),
                   pl.BlockSpec((nh, cpt, 8, LANES), lambda i: (0, i, 0, 0))],
        out_shape=[jax.ShapeDtypeStruct((nh, t, LANES), F32),
                   jax.ShapeDtypeStruct((nh, t // CHUNK, 8, LANES), F32)],
        compiler_params=_params(),
    )(graw, graw_t, bias_row, bias_col)


def _gates_bwd(gstat, colf, dp1, name):
    nh, t, _ = gstat.shape
    tm = _pick(t, 640, 128)
    cpt = tm // CHUNK
    nt = t // tm
    nc = t // CHUNK
    gate_block = dp1.shape[1] // LANES - 1

    def body(gs_ref, nx_ref, colf_ref, dp_any, dg_ref, db_ref):
        i = pl.program_id(0)
        ri = lax.broadcasted_iota(jnp.int32, (CHUNK, CHUNK), 0)
        ci = lax.broadcasted_iota(jnp.int32, (CHUNK, CHUNK), 1)
        upper = (ri <= ci).astype(F32)
        lane = lax.broadcasted_iota(jnp.int32, (CHUNK, LANES), 1)
        total = jnp.zeros((1, LANES), F32)
        for c in range(cpt):
            sl = slice(c * CHUNK, (c + 1) * CHUNK)
            rows = i * tm + c * CHUNK + lax.broadcasted_iota(jnp.int32, (CHUNK, 1), 0)
            live = rows >= PAD_ROWS
            acc = jnp.zeros((CHUNK, LANES), F32)
            for h in range(nh):
                blk = gs_ref[h, sl, :]
                rev = jnp.dot(upper, blk, precision=HIGHEST, preferred_element_type=F32)
                if c + 1 < cpt:
                    carry = gs_ref[h, (c + 1) * CHUNK:(c + 1) * CHUNK + 1, 2:3]
                else:
                    carry = jnp.where(i == nt - 1, 0.0, nx_ref[h, 0:1, 2:3])
                dlogf = rev[:, 0:1] + carry
                pre = colf_ref[h, sl, 2:3]
                dgf = jnp.where(live, dlogf * (1.0 - _sigmoid(pre)), 0.0)
                dgi = jnp.where(live, blk[:, 1:2], 0.0)
                acc = acc + jnp.where(lane == h, dgi, 0.0) + jnp.where(lane == nh + h, dgf, 0.0)
            dg_ref[sl, :] = acc.astype(BF16)
            total = total + jnp.sum(acc, axis=0, keepdims=True)

        @pl.when(i == 0)
        def _():
            db_ref[...] = jnp.zeros_like(db_ref)

        db_ref[...] += jnp.broadcast_to(total, db_ref.shape)

    return pl.pallas_call(
        body, name=name, grid=(nt,),
        in_specs=[pl.BlockSpec((nh, tm, LANES), lambda i: (0, i, 0)),
                  pl.BlockSpec((nh, CHUNK, LANES), lambda i: (0, jnp.minimum((i + 1) * cpt, nc - 1), 0)),
                  pl.BlockSpec((nh, tm, LANES), lambda i: (0, i, 0)),
                  pl.BlockSpec(memory_space=pl.ANY)],
        out_specs=[pl.BlockSpec((tm, LANES), lambda i: (i, gate_block)), pl.BlockSpec((8, LANES), lambda i: (0, 0))],
        out_shape=[jax.ShapeDtypeStruct(dp1.shape, dp1.dtype), jax.ShapeDtypeStruct((8, LANES), F32)],
        input_output_aliases={3: 0},
        compiler_params=_params(),
    )(gstat, gstat, colf, dp1)


NT_DIMS = (((1,), (1,)), ((), ()))
TN_DIMS = (((0,), (0,)), ((), ()))


def _dot(a, b):
    return jnp.dot(a.astype(BF16), b.astype(BF16), preferred_element_type=F32)


def _dot_nt(a, b):
    return lax.dot_general(a.astype(BF16), b.astype(BF16), NT_DIMS, preferred_element_type=F32)


def _dot_tn(a, b):
    return lax.dot_general(a.astype(BF16), b.astype(BF16), TN_DIMS, preferred_element_type=F32)


def _chunk_gates(colf_ref, rowf_ref, m_prev, width=CHUNK):
    b = colf_ref[:, 0:1]
    rcol = colf_ref[:, 1:2]
    rrow = rowf_ref[0:1, 0:width]
    ri = lax.broadcasted_iota(jnp.int32, (CHUNK, width), 0)
    ci = lax.broadcasted_iota(jnp.int32, (CHUNK, width), 1)
    log_d = jnp.where(ri >= ci, b + rrow, NEG)
    m_row = jnp.maximum(b + m_prev, jnp.max(log_d, axis=-1, keepdims=True))
    dmat = jnp.exp(log_d - m_row)
    inter = jnp.exp(b + m_prev - m_row)
    b_last = b[CHUNK - 1:CHUNK, :]
    log_w = rcol + b_last
    m_new = jnp.maximum(b_last + m_prev, jnp.max(log_w, axis=0, keepdims=True))
    decay = jnp.exp(b_last + m_prev - m_new)
    w = jnp.exp(log_w - m_new)
    return m_row, dmat, inter, m_new, decay, w


def _mlstm_fwd(p1, colf, rowf, dk, dv, name):
    t = p1.shape[0]
    nh = N_HEADS
    nc = t // CHUNK
    hw = 2 * dk + dv
    cps = CHUNKS_PER_STEP if nc % CHUNKS_PER_STEP == 0 else 1
    rows_per_step = cps * CHUNK

    def body(p_ref, colf_ref, rowf_ref, hh_ref, stat_ref, cs_ref, ns_ref, *state):
        c_refs, n_refs, m_refs = state[:nh], state[nh:2 * nh], state[2 * nh:]
        c = pl.program_id(0)

        @pl.when(c == 0)
        def _():
            for ref in state:
                ref[...] = jnp.zeros_like(ref)

        for cc in range(cps):
            rows = pl.ds(cc * CHUNK, CHUNK)
            for h in range(nh):
                head(p_ref.at[rows, pl.ds(h * hw, hw)], colf_ref.at[h, rows], rowf_ref.at[h, cc],
                     hh_ref.at[rows, pl.ds(h * dv, dv)], stat_ref.at[h, rows], cs_ref.at[h, cc],
                     ns_ref.at[h, cc], c_refs[h], n_refs[h], m_refs[h])

    def head(p_ref, colf_ref, rowf_ref, hh_ref, stat_ref, cs_ref, ns_ref, c_ref, n_ref, m_ref):
        m_prev = m_ref[...]
        n_prev = n_ref[...]
        c_prev = c_ref[...]
        cs_ref[...] = c_prev.astype(BF16)
        sub = lax.broadcasted_iota(jnp.int32, (8, dk), 0)
        ns_ref[...] = jnp.where(sub == 0, n_prev, jnp.where(sub == 1, m_prev, 0.0))

        q = p_ref[:, 0:dk]
        k = p_ref[:, dk:2 * dk]
        v = p_ref[:, 2 * dk:2 * dk + dv]
        m_row, dmat, inter, m_new, decay, w = _chunk_gates(colf_ref, rowf_ref, m_prev)
        s = _dot_nt(q, k) * dmat
        num = _dot(s, v) + inter * _dot(q, c_prev)
        den = jnp.sum(s, axis=-1, keepdims=True) + inter * jnp.sum(q.astype(F32) * n_prev, axis=-1, keepdims=True)
        denom = jnp.maximum(jnp.abs(den), jnp.exp(-m_row))
        hh_ref[...] = num * (1.0 / denom)
        lane = lax.broadcasted_iota(jnp.int32, (CHUNK, LANES), 1)
        stat_ref[...] = jnp.where(lane == 0, den, 0.0)

        wk = w * k.astype(F32)
        c_ref[...] = decay * c_prev + _dot_tn(wk, v)
        n_ref[...] = decay * n_prev + jnp.sum(wk, axis=0, keepdims=True)
        m_ref[...] = m_new

    return pl.pallas_call(
        body, name=name, grid=(nc // cps,),
        in_specs=[pl.BlockSpec((rows_per_step, nh * hw), lambda c: (c, 0)),
                  pl.BlockSpec((nh, rows_per_step, LANES), lambda c: (0, c, 0)),
                  pl.BlockSpec((nh, cps, 8, LANES), lambda c: (0, c, 0, 0))],
        out_specs=[pl.BlockSpec((rows_per_step, nh * dv), lambda c: (c, 0)),
                   pl.BlockSpec((nh, rows_per_step, LANES), lambda c: (0, c, 0)),
                   pl.BlockSpec((nh, cps, dk, dv), lambda c: (0, c, 0, 0)),
                   pl.BlockSpec((nh, cps, 8, dk), lambda c: (0, c, 0, 0))],
        out_shape=[jax.ShapeDtypeStruct((t, nh * dv), F32), jax.ShapeDtypeStruct((nh, t, LANES), F32),
                   jax.ShapeDtypeStruct((nh, nc, dk, dv), BF16), jax.ShapeDtypeStruct((nh, nc, 8, dk), F32)],
        scratch_shapes=([pltpu.VMEM((dk, dv), F32)] * nh + [pltpu.VMEM((1, dk), F32)] * nh
                        + [pltpu.VMEM((1, 1), F32)] * nh),
        compiler_params=_params(),
    )(p1, colf, rowf)


def _head_gate_fwd(hh, p1, head_w, dv, name):
    t = hh.shape[0]
    nh = N_HEADS
    e = nh * dv
    tm = _pick(t, 416, 16)

    def body(hh_ref, oz_ref, w_ref, hm_ref):
        for h in range(nh):
            cols = slice(h * dv, (h + 1) * dv)
            x = hh_ref[:, cols]
            o = oz_ref[:, 2 * h * dv:(2 * h + 1) * dv].astype(F32)
            z = oz_ref[:, (2 * h + 1) * dv:(2 * h + 2) * dv].astype(F32)
            r = lax.rsqrt(jnp.mean(x * x, axis=-1, keepdims=True) + RMS_EPS)
            hn = (x * r) * w_ref[:, cols]
            hm_ref[:, cols] = (hn * _sigmoid(o) * (z * _sigmoid(z))).astype(BF16)

    return pl.pallas_call(
        body, name=name, grid=(t // tm,),
        in_specs=[pl.BlockSpec((tm, e), lambda i: (i, 0)), pl.BlockSpec((tm, 2 * e), lambda i: (i, 1)),
                  pl.BlockSpec((1, e), lambda i: (0, 0))],
        out_specs=pl.BlockSpec((tm, e), lambda i: (i, 0)),
        out_shape=jax.ShapeDtypeStruct((t, e), BF16),
        compiler_params=_params(),
    )(hh, p1, head_w)


def _mlstm_bwd(p1, colf, rowf, head_w, hh, stat, csave, nsave, dhm, n_cols, dk, dv, name, exchange=None):
    t = p1.shape[0]
    nh = N_HEADS
    nc = t // CHUNK
    hw = 2 * dk + dv
    cps = CHUNKS_PER_STEP if nc % CHUNKS_PER_STEP == 0 else 1
    rows_per_step = cps * CHUNK

    def body(p_ref, colf_ref, rowf_ref, w_ref, hh_ref, stat_ref, cs_ref, ns_ref, dhm_ref,
             dp_ref, gs_ref, dw_ref, *state):
        dc_refs, dn_refs, dwacc_refs = state[:nh], state[nh:2 * nh], state[2 * nh:]
        c = pl.program_id(0)

        @pl.when(c == 0)
        def _():
            for ref in state:
                ref[...] = jnp.zeros_like(ref)

        for cc in reversed(range(cps)):
            rows = pl.ds(cc * CHUNK, CHUNK)
            for h in range(nh):
                cols = pl.ds(h * dv, dv)
                oz = pl.ds(nh * hw + h * 2 * dv, 2 * dv)
                head(p_ref.at[rows, pl.ds(h * hw, hw)], p_ref.at[rows, oz], colf_ref.at[h, rows], rowf_ref.at[h, cc],
                     w_ref.at[:, cols], hh_ref.at[rows, cols], stat_ref.at[h, rows], cs_ref.at[h, cc],
                     ns_ref.at[h, cc], dhm_ref.at[rows, cols], dp_ref.at[rows, pl.ds(h * hw, hw)],
                     dp_ref.at[rows, oz], gs_ref.at[h, rows], dwacc_refs[h], dc_refs[h], dn_refs[h])

        @pl.when(c == nc // cps - 1)
        def _():
            for h in range(nh):
                dw_ref[h] = dwacc_refs[h][...]

    def head(p_ref, oz_ref, colf_ref, rowf_ref, w_ref, hh_ref, stat_ref, cs_ref, ns_ref, dhm_ref,
             dp_ref, doz_ref, gs_ref, dw_ref, dc_ref, dn_ref):
        q = p_ref[:, 0:dk]
        k = p_ref[:, dk:2 * dk]
        v = p_ref[:, 2 * dk:2 * dk + dv]
        o = oz_ref[:, 0:dv].astype(F32)
        z = oz_ref[:, dv:2 * dv].astype(F32)
        qf = q.astype(F32)
        kf = k.astype(F32)
        n_prev = ns_ref[0:1, :]
        m_prev = ns_ref[1:2, 0:1]
        c_prev = cs_ref[...]
        m_row, dmat, inter, m_new, decay, w = _chunk_gates(colf_ref, rowf_ref, m_prev)

        hh = hh_ref[...]
        dhm_v = dhm_ref[...].astype(F32)
        so = _sigmoid(o)
        sg = _sigmoid(z)
        sz = z * sg
        r = lax.rsqrt(jnp.mean(hh * hh, axis=-1, keepdims=True) + RMS_EPS)
        hn = (hh * r) * w_ref[...]
        dhn = dhm_v * so * sz
        doz_ref[:, 0:dv] = (dhm_v * hn * sz * (so * (1.0 - so))).astype(BF16)
        doz_ref[:, dv:2 * dv] = (dhm_v * hn * so * (sg * (1.0 + z * (1.0 - sg)))).astype(BF16)
        dw_ref[...] += jnp.broadcast_to(jnp.sum(dhn * (hh * r), axis=0, keepdims=True), dw_ref.shape)
        gwn = dhn * w_ref[...]
        dhh = r * gwn - hh * ((r * r * r) * jnp.mean(gwn * hh, axis=-1, keepdims=True))

        den = stat_ref[:, 0:1]
        floor = jnp.exp(-m_row)
        denom = jnp.maximum(jnp.abs(den), floor)
        inv = 1.0 / denom
        dnum = dhh * inv
        hdot = jnp.sum(dhh * hh, axis=-1, keepdims=True)
        dden = jnp.where(jnp.abs(den) > floor, -(hdot * inv) * jnp.sign(den), 0.0)
        qk = _dot_nt(q, k)
        s = qk * dmat
        dqk = (_dot_nt(dnum, v) + dden) * dmat
        dc_new = dc_ref[...]
        dn_new = dn_ref[...]
        idd = inter * dden
        dq_far = inter * _dot_nt(dnum, c_prev) + idd * n_prev
        dk_far = w * (_dot_nt(v, dc_new) + dn_new)
        dq = _dot(dqk, k) + dq_far
        dkv = _dot_tn(dqk, q) + dk_far
        dvv = _dot_tn(s, dnum) + w * _dot(k, dc_new)
        dp_ref[:, 0:dk] = dq.astype(BF16)
        dp_ref[:, dk:2 * dk] = dkv.astype(BF16)
        dp_ref[:, 2 * dk:2 * dk + dv] = dvv.astype(BF16)
        pair = dqk * qk
        ri = lax.broadcasted_iota(jnp.int32, (CHUNK, CHUNK), 0)
        ci = lax.broadcasted_iota(jnp.int32, (CHUNK, CHUNK), 1)
        near_q = jnp.sum(pair, axis=-1, keepdims=True)
        near_k = jnp.sum(jnp.where(ri == ci, jnp.sum(pair, axis=0, keepdims=True), 0.0), axis=-1, keepdims=True)
        qdq = near_q + jnp.sum(qf * dq_far, axis=-1, keepdims=True)
        kdk = near_k + jnp.sum(kf * dk_far, axis=-1, keepdims=True)
        dc_prev = decay * dc_new + _dot_tn(inter * qf, dnum)
        dn_prev = decay * dn_new + jnp.sum(idd * qf, axis=0, keepdims=True)
        dc_ref[...] = dc_prev
        dn_ref[...] = dn_prev
        cross = (jnp.sum(jnp.sum(c_prev.astype(F32) * dc_prev, axis=-1, keepdims=True), axis=0, keepdims=True)
                 + jnp.sum(n_prev * dn_prev, axis=-1, keepdims=True))
        lane = lax.broadcasted_iota(jnp.int32, (CHUNK, LANES), 1)
        gs_ref[...] = jnp.where(lane == 0, qdq - kdk, jnp.where(lane == 1, kdk, jnp.where(lane == 2, cross, 0.0)))

    ns = nc // cps
    rc = lambda c: (ns - 1 - c, 0)
    rc3 = lambda c: (0, ns - 1 - c, 0)
    rc4 = lambda c: (0, ns - 1 - c, 0, 0)
    return _call(
        body, name=name, grid=(ns,),
        in_specs=[pl.BlockSpec((rows_per_step, nh * (hw + 2 * dv)), rc),
                  pl.BlockSpec((nh, rows_per_step, LANES), rc3),
                  pl.BlockSpec((nh, cps, 8, LANES), rc4),
                  pl.BlockSpec((1, nh * dv), lambda c: (0, 0)),
                  pl.BlockSpec((rows_per_step, nh * dv), rc),
                  pl.BlockSpec((nh, rows_per_step, LANES), rc3),
                  pl.BlockSpec((nh, cps, dk, dv), rc4),
                  pl.BlockSpec((nh, cps, 8, dk), rc4),
                  pl.BlockSpec((rows_per_step, nh * dv), rc)],
        out_specs=[pl.BlockSpec((rows_per_step, nh * (hw + 2 * dv)), rc),
                   pl.BlockSpec((nh, rows_per_step, LANES), rc3),
                   pl.BlockSpec((nh, 8, dv), lambda c: (0, 0, 0))],
        out_shape=[jax.ShapeDtypeStruct((t, n_cols), BF16), jax.ShapeDtypeStruct((nh, t, LANES), F32),
                   jax.ShapeDtypeStruct((nh, 8, dv), F32)],
        scratch_shapes=([pltpu.VMEM((dk, dv), F32)] * nh + [pltpu.VMEM((1, dk), F32)] * nh
                        + [pltpu.VMEM((8, dv), F32)] * nh),
        args=(p1, colf, rowf, head_w, hh, stat, csave, nsave, dhm), exchange=exchange)


def _position():
    return lax.axis_index("x"), lax.axis_index("y"), lax.axis_index("c")


class _Exchange:
    def __init__(self, ins, out_shapes, start, mid, finish):
        n = len(ins)
        self.ins, self.out_shapes = list(ins), list(out_shapes)
        self.start, self.mid, self.finish = start, mid, finish
        self.scratch = [pltpu.SemaphoreType.DMA((n, 7)), pltpu.SemaphoreType.DMA((n, 7)),
                        pltpu.SemaphoreType.DMA((n,))]


def _gather_exchange(shards):
    n = len(shards)

    def plan(ins, outs, sems):
        send_sems, recv_sems, local_sems = sems
        x, y, c = _position()
        me, sibling = (x, y, c), (x, y, 1 - c)
        chips = [(1 - x, y), (x, 1 - y), (1 - x, 1 - y)]

        def copy(a, k, block, to, src=None):
            px, py, pc = block
            dst = outs[a].at[4 * px + 2 * py + pc]
            return pltpu.make_async_remote_copy(
                src_ref=dst if src is None else src, dst_ref=dst,
                send_sem=send_sems.at[a, k], recv_sem=recv_sems.at[a, k],
                device_id=to, device_id_type=MESH)

        def mine():
            return [pltpu.make_async_copy(ins[a], outs[a].at[4 * x + 2 * y + c], local_sems.at[a])
                    for a in range(n)]

        def first():
            out = []
            for a in range(n):
                out.append(copy(a, 0, me, sibling, src=ins[a]))
                out += [copy(a, 1 + j, me, (*chip, c), src=ins[a]) for j, chip in enumerate(chips)]
            return out

        def ici_in():
            return [copy(a, 1 + j, (*chip, c), me) for j, chip in enumerate(chips) for a in range(n)]

        def passed():
            return [copy(a, 4 + j, (*chip, c), sibling) for j, chip in enumerate(chips) for a in range(n)]

        def d2d_in():
            return ([copy(a, 0, sibling, me) for a in range(n)]
                    + [copy(a, 4 + j, (*chip, 1 - c), me) for j, chip in enumerate(chips) for a in range(n)])

        return mine, first, ici_in, passed, d2d_in

    def start(ins, outs, sems):
        mine, first, _, _, _ = plan(ins, outs, sems)
        for cp in mine() + first():
            cp.start()

    def mid(ins, outs, sems):
        _, _, ici_in, passed, _ = plan(ins, outs, sems)
        for arrived, onward in zip(ici_in(), passed()):
            arrived.wait_recv()
            onward.start()

    def finish(ins, outs, sems):
        mine, first, _, passed, d2d_in = plan(ins, outs, sems)
        for cp in d2d_in():
            cp.wait_recv()
        for cp in first() + passed():
            cp.wait_send()
        for cp in mine():
            cp.wait()

    shapes = [jax.ShapeDtypeStruct((N_DEV,) + s.shape, s.dtype) for s in shards]
    return _Exchange(shards, shapes, start, mid, finish)


def _scatter_exchange(fulls):
    n = len(fulls)

    def plan(ins, outs, sems):
        send_sems, recv_sems, local_sems = sems
        x, y, c = _position()
        my_slot = 4 * x + 2 * y + c

        def mine():
            return [pltpu.make_async_copy(ins[a].at[my_slot], outs[a].at[my_slot], local_sems.at[a])
                    for a in range(n)]

        def remote(arriving):
            out = []
            for kk in (1, 2, 4, 6, 3, 5, 7):
                kx, ky, kc = (kk >> 2) & 1, (kk >> 1) & 1, kk & 1
                px = 1 - x if kx else x
                py = 1 - y if ky else y
                pc = 1 - c if kc else c
                peer_slot = 4 * px + 2 * py + pc
                for a in range(n):
                    out.append(pltpu.make_async_remote_copy(
                        src_ref=ins[a].at[peer_slot], dst_ref=outs[a].at[peer_slot if arriving else my_slot],
                        send_sem=send_sems.at[a, kk - 1], recv_sem=recv_sems.at[a, kk - 1],
                        device_id=(px, py, pc), device_id_type=MESH))
            return out

        return mine, remote

    def start(ins, outs, sems):
        mine, remote = plan(ins, outs, sems)
        for cp in mine() + remote(False):
            cp.start()

    def finish(ins, outs, sems):
        mine, remote = plan(ins, outs, sems)
        for cp in remote(True):
            cp.wait_recv()
        for cp in remote(False):
            cp.wait_send()
        for cp in mine():
            cp.wait()

    shapes = [jax.ShapeDtypeStruct(f.shape, f.dtype) for f in fulls]
    return _Exchange(fulls, shapes, start, None, finish)


def _run_exchange(ex, name):
    n_in, n_out = len(ex.ins), len(ex.out_shapes)

    def body(*refs):
        ins, outs, sems = refs[:n_in], refs[n_in:n_in + n_out], refs[n_in + n_out:]
        ex.start(ins, outs, sems)
        if ex.mid is not None:
            ex.mid(ins, outs, sems)
        ex.finish(ins, outs, sems)

    any_spec = pl.BlockSpec(memory_space=pl.ANY)
    return pl.pallas_call(
        body, name=name,
        in_specs=[any_spec] * n_in, out_specs=[any_spec] * n_out,
        out_shape=ex.out_shapes, scratch_shapes=ex.scratch,
    )(*ex.ins)


def _natural_to_head_major(n0, dk, dv):
    nh = N_HEADS
    qk, e, hw = nh * dk, nh * dv, 2 * dk + dv
    if n0 < qk:
        h, off = divmod(n0, dk)
        return h * hw + off, True
    n1 = n0 - qk
    if n1 < qk:
        h, off = divmod(n1, dk)
        return h * hw + dk + off, False
    part, n3 = divmod(n1 - qk, e)
    h, off = divmod(n3, dv)
    if part == 0:
        return h * hw + 2 * dk + off, False
    return nh * hw + h * 2 * dv + (part - 1) * dv + off, False


def _mlstm_weight_layout(w_blocks, dk, dv, name):
    _, d, blk = w_blocks.shape
    nh = N_HEADS
    hw = 2 * dk + 3 * dv
    n_tiles = nh * hw // LANES
    tiles_per_block = d // LANES
    tr = _pick(d, 128, 16)
    scale = dk ** -0.5

    def body(w_ref, p_ref, g_ref):
        lane = lax.broadcasted_iota(jnp.int32, (tr, LANES), 1)

        def tile(s, r):
            return w_ref[s, :, r * LANES:(r + 1) * LANES].astype(F32)

        def last_col(s):
            return w_ref[s, :, d:d + 1].astype(F32)

        for tn in range(n_tiles):
            s, r = divmod(tn, tiles_per_block)
            if s == 0:
                val = tile(0, r)
            elif r == 0:
                from_prev = pltpu.roll(tile(s - 1, tiles_per_block - 1), s - 1, 1)
                from_here = pltpu.roll(tile(s, 0), s, 1)
                val = jnp.where(lane < s - 1, from_prev, jnp.where(lane == s - 1, last_col(s - 1), from_here))
            else:
                slab = jnp.concatenate([tile(s, r - 1), tile(s, r)], axis=1)
                val = pltpu.roll(slab, s, 1)[:, LANES:]
            at, is_q = _natural_to_head_major(tn * LANES, dk, dv)
            if is_q:
                val = val * scale
            p_ref[:, at:at + LANES] = val.astype(BF16)
        n_gate = 2 * nh
        s = N_DEV - 1
        gates = pltpu.roll(tile(s, tiles_per_block - 1), n_gate - 1, 1)
        gates = jnp.where(lane < n_gate - 1, gates, jnp.where(lane == n_gate - 1, last_col(s), 0.0)).astype(BF16)
        g_ref[...] = gates
        p_ref[:, nh * hw:nh * hw + LANES] = gates

    return pl.pallas_call(
        body, name=name, grid=(d // tr,),
        in_specs=[pl.BlockSpec((N_DEV, tr, blk), lambda i: (0, i, 0))],
        out_specs=[pl.BlockSpec((tr, nh * hw + LANES), lambda i: (i, 0)), pl.BlockSpec((tr, LANES), lambda i: (i, 0))],
        out_shape=[jax.ShapeDtypeStruct((d, nh * hw + LANES), BF16), jax.ShapeDtypeStruct((d, LANES), BF16)],
        compiler_params=_params(),
    )(w_blocks)


def _mlstm_grad_layout(g_p1, row0, n_rows, dk, dv, name):
    d = g_p1.shape[0]
    nh = N_HEADS
    hw = 2 * dk + 3 * dv
    n_tiles = nh * hw // LANES
    tiles_per_block = d // LANES
    tr = _pick(d, 128, 16)
    scale = dk ** -0.5

    def body(p_ref, o_ref):
        def natural(tn):
            if tn == n_tiles:
                return p_ref[:, nh * hw:nh * hw + LANES].astype(F32)
            at, is_q = _natural_to_head_major(tn * LANES, dk, dv)
            val = p_ref[:, at:at + LANES].astype(F32)
            return val * scale if is_q else val

        for s in range(N_DEV):
            for r in range(tiles_per_block):
                tn = s * tiles_per_block + r
                if s == 0:
                    val = natural(tn)
                else:
                    slab = jnp.concatenate([natural(tn), natural(tn + 1)], axis=1)
                    val = pltpu.roll(slab, 2 * LANES - s, 1)[:, :LANES]
                o_ref[s, :, r * LANES:(r + 1) * LANES] = val.astype(BF16)
            o_ref[s, :, d:d + 1] = natural((s + 1) * tiles_per_block)[:, s:s + 1].astype(BF16)

    first = row0 // tr
    return pl.pallas_call(
        body, name=name, grid=(n_rows // tr,),
        in_specs=[pl.BlockSpec((tr, nh * hw + LANES), lambda i: (i + first, 0))],
        out_specs=pl.BlockSpec((N_DEV, tr, d + 1), lambda i: (0, i, 0)),
        out_shape=jax.ShapeDtypeStruct((N_DEV, n_rows, d + 1), BF16),
        compiler_params=_params(),
    )(g_p1)


def _adamw_math(w, g, m, v):
    m = ADAM_B1 * m + (1.0 - ADAM_B1) * g
    v = ADAM_B2 * v + (1.0 - ADAM_B2) * (g * g)
    m_hat = m / (1.0 - ADAM_B1 ** ADAM_STEP)
    v_hat = v / (1.0 - ADAM_B2 ** ADAM_STEP)
    delta = -ADAM_LR * (m_hat / (jnp.sqrt(v_hat) + ADAM_EPS) + ADAM_WD * w)
    return delta, m, v


def _adamw_sharded(parts, w, m, v, name):
    n_parts = len(parts)
    _, rk, c = parts[0].shape
    r = n_parts * rk
    tr = _pick(rk, 128, 8)
    per = rk // tr

    def body(*refs):
        p_refs = refs[:n_parts]
        w_ref, m_ref, v_ref, g_ref, d_ref, nm_ref, nv_ref, gsum_ref = refs[n_parts:]
        i = pl.program_id(0)
        for k, p_ref in enumerate(p_refs):
            @pl.when(i // per == k)
            def _(p_ref=p_ref):
                g = p_ref[0].astype(F32)
                for s in range(1, N_DEV):
                    g = g + p_ref[s].astype(F32)
                gsum_ref[...] = g
        g = gsum_ref[...]
        delta, m_new, v_new = _adamw_math(w_ref[...], g, m_ref[...], v_ref[...])
        g_ref[...] = g
        d_ref[...] = delta
        nm_ref[...] = m_new
        nv_ref[...] = v_new

    def part_spec(k):
        return pl.BlockSpec((N_DEV, tr, c), lambda i: (0, jnp.clip(i - k * per, 0, per - 1), 0))

    blk = pl.BlockSpec((tr, c), lambda i: (i, 0))
    return pl.pallas_call(
        body, name=name, grid=(r // tr,),
        in_specs=[part_spec(k) for k in range(n_parts)] + [blk, blk, blk],
        out_specs=[blk] * 4,
        out_shape=[jax.ShapeDtypeStruct((r, c), F32)] * 4,
        scratch_shapes=[pltpu.VMEM((tr, c), F32)],
        compiler_params=_params(),
    )(*parts, w, m, v)


def _sum_devices(parts, name):
    _, r, c = parts.shape

    def body(p_ref, o_ref):
        g = p_ref[0]
        for s in range(1, N_DEV):
            g = g + p_ref[s]
        o_ref[...] = g

    return pl.pallas_call(
        body, name=name, out_shape=jax.ShapeDtypeStruct((r, c), F32), compiler_params=_params(),
    )(parts)


def _adamw_small(gs, ws, ms, vs, name):
    n = len(gs)

    def body(*refs):
        g_refs, w_refs, m_refs, v_refs = refs[:n], refs[n:2 * n], refs[2 * n:3 * n], refs[3 * n:4 * n]
        d_refs, nm_refs, nv_refs = refs[4 * n:5 * n], refs[5 * n:6 * n], refs[6 * n:7 * n]
        for a in range(n):
            delta, m_new, v_new = _adamw_math(w_refs[a][...], g_refs[a][...], m_refs[a][...], v_refs[a][...])
            d_refs[a][...] = delta
            nm_refs[a][...] = m_new
            nv_refs[a][...] = v_new

    shapes = [jax.ShapeDtypeStruct(w.shape, F32) for w in ws]
    outs = pl.pallas_call(
        body, name=name, out_shape=shapes * 3, compiler_params=_params(),
    )(*gs, *ws, *ms, *vs)
    return outs[:n], outs[n:2 * n], outs[2 * n:]


def _pad_rows(a, rows):
    return jnp.pad(a, ((0, rows - a.shape[0]), (0, 0)))


def kernel(x, meta_tokens, norm_w, conv_in_w, conv_w, conv_out_w, mlstm_in_w, mlstm_gate_b, mlstm_head_norm_w, mlstm_out_w, final_norm_w, loss_target, m_meta_tokens, m_norm_w, m_conv_in_w, m_conv_w, m_conv_out_w, m_mlstm_in_w, m_mlstm_gate_b, m_mlstm_head_norm_w, m_mlstm_out_w, m_final_norm_w, v_meta_tokens, v_norm_w, v_conv_in_w, v_conv_w, v_conv_out_w, v_mlstm_in_w, v_mlstm_gate_b, v_mlstm_head_norm_w, v_mlstm_out_w, v_final_norm_w):
    seq, d = x.shape[1], x.shape[2]
    t = seq + LEAD
    e = 2 * d
    ec = e // 2
    nh = N_HEADS
    dv = e // nh
    dk = dv // 2
    qk = nh * dk
    hw = 2 * dk + 3 * dv
    n_in = 2 * qk + 3 * e + 2 * nh
    n_in_s = n_in // N_DEV
    me = 4 * lax.axis_index("x") + 2 * lax.axis_index("y") + lax.axis_index("c")
    tm = _pick(t, 832, 16)
    tm_in = _pick(t, 1664, 16)
    tkw = _pick(t, 2080, 16)

    small = jnp.concatenate([
        meta_tokens,
        _pad_rows(conv_w[0].reshape(3 * (e // N_DEV) // LANES, LANES), 8),
        _pad_rows(mlstm_head_norm_w[0].reshape((e // N_DEV) // LANES, LANES), 8),
    ], axis=0) if d // N_DEV == LANES else None
    assert small is not None, "the packed small-weight block assumes d_model / 8 == 128"
    (small_g,) = _run_exchange(_gather_exchange([small]), "gather_small_weights")
    meta_full = jnp.transpose(small_g[:, 0:N_META, :], (1, 0, 2)).reshape(N_META, d)
    cw_rows = 3 * (e // N_DEV) // LANES
    conv_w_full = jnp.transpose(
        small_g[:, N_META:N_META + cw_rows, :].reshape(N_DEV, 3, e // N_DEV), (1, 0, 2)).reshape(3, e)
    hn_rows = (e // N_DEV) // LANES
    head_w_full = small_g[:, N_META + 8:N_META + 8 + hn_rows, :].reshape(1, e)

    tgt = loss_target[0]

    ci_map = lambda blk: 2 * (blk % 4) + blk // 4
    (h0, u0), (w_ci,) = _rms_first(x[0], meta_full, norm_w[0:1], "rms0",
                                   exchange=_gather_exchange([conv_in_w[0].astype(BF16)]))
    p0, (w_co, w_mi) = _matmul(
        u0, w_ci, form="nn", m=t, n=8 * ec, kdim=d, tm=tm_in, tn=ec, tk=d, out_dtype=BF16, name="conv_in",
        b_spec=pl.BlockSpec((None, d, ec), lambda i, j, k: (ci_map(j), 0, 0)),
        exchange=_gather_exchange([conv_out_w[0].astype(BF16), mlstm_in_w[0].astype(BF16)]))
    w_co = w_co.reshape(e, d)
    (g0, y0), (w_mo,) = _conv_gate_fwd(p0, conv_w_full, "conv_gate",
                                       exchange=_gather_exchange([mlstm_out_w[0].astype(BF16)]))
    w_mo = w_mo.reshape(e, d)
    h1 = _matmul(g0, w_co, form="nn", m=t, n=d, kdim=e, tm=tm, tn=d, tk=e, out_dtype=F32, name="conv_out",
                 residual=h0)

    assert n_in_s == d + 1 and w_mi.shape == (N_DEV, d, d + 1)
    w_p1, w_gate = _mlstm_weight_layout(w_mi, dk, dv, "mlstm_w_layout")
    w_gate_t = jnp.transpose(w_gate[:, 0:16])
    bias_row = jnp.pad(mlstm_gate_b, ((0, 0), (0, LANES - 2 * nh)))
    bias_col = jnp.pad(mlstm_gate_b.T, ((0, 16 - 2 * nh), (0, 0)))

    u1 = _rms_fwd(h1, norm_w[1:2], "rms1")
    p1 = _matmul(u1, w_p1, form="nn", m=t, n=nh * hw, kdim=d, tm=tm_in, tn=_pick(nh * hw, 2048, LANES), tk=d,
                 out_dtype=BF16, name="mlstm_in")
    graw = _matmul(u1, w_gate, form="nn", m=t, n=LANES, kdim=d, tm=tm, tn=LANES, tk=d, out_dtype=F32,
                   name="gates_col")
    graw_t = _matmul(w_gate_t, u1, form="nt", m=16, n=t, kdim=d, tm=16, tn=_pick(t, 1664, LANES), tk=d,
                     out_dtype=F32, name="gates_row")
    colf, rowf = _gates_fwd(graw, graw_t, bias_row, bias_col, "gates_fwd")
    hh, stat, csave, nsave = _mlstm_fwd(p1, colf, rowf, dk, dv, "mlstm_fwd")
    hm = _head_gate_fwd(hh, p1, head_w_full, dv, "head_gate")
    h2 =_matmul(hm, w_mo, form="nn", m=t, n=d, kdim=e, tm=tm, tn=d, tk=e, out_dtype=F32, name="mlstm_out",
                 residual=h1)

    dh2, dwf, loss_part = _final_loss(h2, final_norm_w.reshape(1, d), tgt, "final_loss")

    dhm = _matmul(dh2, w_mo, form="nt", m=t, n=e, kdim=d, tm=tm, tn=_pick(e, 2048, LANES), tk=d, out_dtype=BF16,
                  name="mlstm_out_dx")
    g_mo = _matmul(hm, dh2, form="tn", m=e, n=d, kdim=t, tm=_pick(e, 1024, LANES), tn=d, tk=tkw, out_dtype=BF16,
                   name="mlstm_out_dw")
    n_p1 = nh * hw + LANES
    (dp1, gstat, dhead), (r_mo,) = _mlstm_bwd(
        p1, colf, rowf, head_w_full, hh, stat, csave, nsave, dhm, n_p1, dk, dv, "mlstm_bwd",
        exchange=_scatter_exchange([g_mo.reshape(N_DEV, e // N_DEV, d)]))
    dp1, dbias = _gates_bwd(gstat, colf, dp1, "gates_bwd")
    du1 = _matmul(dp1, w_p1, form="nt", m=t, n=d, kdim=n_p1, tm=tm_in, tn=d, tk=_pick(n_p1, 2048, LANES),
                  out_dtype=F32, name="mlstm_in_dx")
    g_p1 = _matmul(u1, dp1, form="tn", m=d, n=n_p1, kdim=t, tm=d, tn=_pick(n_p1, 2048, LANES), tk=tkw,
                   out_dtype=BF16, name="mlstm_in_dw")
    dh1, dnw1 = _rms_bwd(h1, norm_w[1:2], du1, dh2, "rms1_bwd")

    g_mi_a = _mlstm_grad_layout(g_p1, 0, d // 2, dk, dv, "mlstm_g_layout_a")
    g_mi_b = _mlstm_grad_layout(g_p1, d // 2, d // 2, dk, dv, "mlstm_g_layout_b")

    dg0 =_matmul(dh1, w_co, form="nt", m=t, n=e, kdim=d, tm=tm, tn=_pick(e, 2048, LANES), tk=d, out_dtype=BF16,
                  name="conv_out_dx")
    g_co = _matmul(g0, dh1, form="tn", m=e, n=d, kdim=t, tm=_pick(e, 1024, LANES), tn=d, tk=tkw, out_dtype=BF16,
                   name="conv_out_dw")
    (dp0, dconv), (r_co, r_mi_a) = _conv_gate_bwd(
        p0, y0, dg0, conv_w_full, "conv_gate_bwd",
        exchange=_scatter_exchange([g_co.reshape(N_DEV, e // N_DEV, d), g_mi_a]))
    g_ci, (r_mi_b,) = _matmul(
        u0, dp0, form="tn", m=d, n=8 * ec, kdim=t, tm=d, tn=ec, tk=tkw, out_dtype=BF16, name="conv_in_dw",
        out_shape=(N_DEV, d, ec), out_spec=pl.BlockSpec((None, d, ec), lambda i, j, k: (ci_map(j), 0, 0)),
        exchange=_scatter_exchange([g_mi_b]))
    du0, (r_ci,) = _matmul(
        dp0, w_ci, form="nt", m=t, n=d, kdim=8 * ec, tm=tm_in, tn=d, tk=ec, out_dtype=F32, name="conv_in_dx",
        b_spec=pl.BlockSpec((None, d, ec), lambda i, j, k: (ci_map(k), 0, 0)),
        exchange=_scatter_exchange([g_ci]))
    grad_x, dmeta, dnw0 = _rms_bwd_first(h0, norm_w[0:1], du0, dh1, "rms0_bwd")
    grad_x = grad_x[None]

    row8 = lax.broadcasted_iota(jnp.int32, (8, 1), 0)
    loss_wide = jnp.pad(loss_part, ((0, 0), (0, d - LANES)))
    payload = jnp.concatenate([
        jnp.where(row8 == 0, dnw0, jnp.where(row8 == 1, dnw1, 0.0)),
        jnp.where(row8 == 0, dwf, jnp.where(row8 == 1, loss_wide, 0.0)),
        jnp.where(row8 == 0, jnp.pad(dbias, ((0, 0), (0, d - LANES))), 0.0),
        dmeta,
        _pad_rows(dconv[0:3].reshape(3 * e // d, d), 8),
        _pad_rows(dhead[:, 0, :].reshape(e // d, d), 8),
    ], axis=0)
    (payload_g,) = _run_exchange(_gather_exchange([payload]), "gather_small_grads")
    tot = _sum_devices(payload_g, "sum_small_grads")

    loss = tot[9, 0]
    g_norm = tot[0:2]
    g_final = tot[8]
    g_gate_b = tot[16:17, 0:2 * nh]
    g_meta = lax.dynamic_slice(tot[24:24 + N_META], (0, me * (d // N_DEV)), (N_META, d // N_DEV))
    g_conv_w = lax.dynamic_slice(tot[40:40 + 3 * e // d].reshape(3, e), (0, me * (e // N_DEV)), (3, e // N_DEV))
    g_head = lax.dynamic_slice(tot[48:48 + e // d].reshape(1, e), (0, me * (e // N_DEV)), (1, e // N_DEV))

    g1, d1, nm1, nv1 = _adamw_sharded([r_ci], conv_in_w[0], m_conv_in_w[0], v_conv_in_w[0], "adamw_conv_in")
    g2, d2, nm2, nv2 = _adamw_sharded([r_co], conv_out_w[0], m_conv_out_w[0], v_conv_out_w[0], "adamw_conv_out")
    g3, d3, nm3, nv3 = _adamw_sharded([r_mi_a, r_mi_b], mlstm_in_w[0], m_mlstm_in_w[0], v_mlstm_in_w[0],
                                      "adamw_mlstm_in")
    g4, d4, nm4, nv4 = _adamw_sharded([r_mo], mlstm_out_w[0], m_mlstm_out_w[0], v_mlstm_out_w[0],
                                      "adamw_mlstm_out")

    small_g = [g_meta, g_norm, g_conv_w, g_gate_b, g_head, g_final.reshape(1, d)]
    small_w = [meta_tokens, norm_w, conv_w[0], mlstm_gate_b, mlstm_head_norm_w, final_norm_w.reshape(1, d)]
    small_m = [m_meta_tokens, m_norm_w, m_conv_w[0], m_mlstm_gate_b, m_mlstm_head_norm_w, m_final_norm_w.reshape(1, d)]
    small_v = [v_meta_tokens, v_norm_w, v_conv_w[0], v_mlstm_gate_b, v_mlstm_head_norm_w, v_final_norm_w.reshape(1, d)]
    sd, snm, snv = _adamw_small(small_g, small_w, small_m, small_v, "adamw_small")

    def order(meta, norm, cin, cw, cout, min_, gb, hn, mout, fin):
        return (meta, norm, cin[None], cw[None], cout[None], min_[None], gb, hn, mout[None], fin.reshape(d))

    grads = order(g_meta, g_norm, g1, g_conv_w, g2, g3, g_gate_b, g_head, g4, g_final)
    deltas = order(sd[0], sd[1], d1, sd[2], d2, d3, sd[3], sd[4], d4, sd[5])
    new_m = order(snm[0], snm[1], nm1, snm[2], nm2, nm3, snm[3], snm[4], nm4, snm[5])
    new_v = order(snv[0], snv[1], nv1, snv[2], nv2, nv3, snv[3], snv[4], nv4, snv[5])
    return (loss, grad_x, *grads, *deltas, *new_m, *new_v)
```

```python
import functools

import jax
import jax.numpy as jnp
from jax import lax
from jax.experimental import pallas as pl
from jax.experimental.pallas import tpu as pltpu

F32 = jnp.float32
BF16 = jnp.bfloat16
MESH = pl.DeviceIdType.MESH

N_DEV = 8
N_META = 16
N_HEADS = 4
CHUNK = 64
CHUNKS_PER_STEP = 2
LEAD = 128
PAD_ROWS = LEAD - N_META
RMS_EPS = 1e-6
NEG = -1e30
LANES = 128
VMEM_LIMIT = 48 * 1024 * 1024

ADAM_LR = 0.001
ADAM_B1 = 0.9
ADAM_B2 = 0.999
ADAM_EPS = 1e-08
ADAM_WD = 0.01
ADAM_STEP = 10

HIGHEST = lax.Precision.HIGHEST


def _pick(n, target, mult):
    best = None
    for d in range(mult, min(n, target) + 1, mult):
        if n % d == 0:
            best = d
    return n if best is None else best


def _params(**kw):
    return pltpu.CompilerParams(vmem_limit_bytes=VMEM_LIMIT, **kw)


def _sigmoid(x):
    return 0.5 * jnp.tanh(0.5 * x) + 0.5


STRIP_ROWS = 16
STRIP_COLS = 256


def _strips(n_rows, n_cols):
    for c0 in range(0, n_cols, STRIP_COLS):
        for r0 in range(0, n_rows, STRIP_ROWS):
            yield slice(r0, r0 + STRIP_ROWS), slice(c0, min(c0 + STRIP_COLS, n_cols))


def _shift(sl, by):
    return slice(sl.start + by, sl.stop + by)


def _call(body, *, name, grid, in_specs, out_specs, out_shape, args, scratch_shapes=(), aliases=None,
          exchange=None):
    aliases = {} if aliases is None else aliases
    if exchange is None:
        return pl.pallas_call(
            body, name=name, grid=grid, in_specs=list(in_specs), out_specs=out_specs, out_shape=out_shape,
            scratch_shapes=list(scratch_shapes), input_output_aliases=aliases,
            compiler_params=_params())(*args)
    ex = exchange
    single = not isinstance(out_shape, (list, tuple))
    shapes = [out_shape] if single else list(out_shape)
    specs = [out_specs] if single else list(out_specs)
    n_in, n_out, n_scr = len(in_specs), len(shapes), len(scratch_shapes)
    n_ex_in, n_ex_out = len(ex.ins), len(ex.out_shapes)
    steps = 1
    for size in grid:
        steps *= size

    def wrapped(*refs):
        own_in, ex_in = refs[:n_in], refs[n_in:n_in + n_ex_in]
        at = n_in + n_ex_in
        own_out, ex_out = refs[at:at + n_out], refs[at + n_out:at + n_out + n_ex_out]
        at += n_out + n_ex_out
        own_scr, ex_scr = refs[at:at + n_scr], refs[at + n_scr:]
        step = 0
        for axis, size in enumerate(grid):
            step = step * size + pl.program_id(axis)

        @pl.when(step == 0)
        def _():
            ex.start(ex_in, ex_out, ex_scr)

        if ex.mid is not None:
            @pl.when(step == (3 * steps) // 4)
            def _():
                ex.mid(ex_in, ex_out, ex_scr)

        body(*own_in, *own_out, *own_scr)

        @pl.when(step == steps - 1)
        def _():
            ex.finish(ex_in, ex_out, ex_scr)

    any_spec = pl.BlockSpec(memory_space=pl.ANY)
    res = pl.pallas_call(
        wrapped, name=name, grid=grid,
        in_specs=list(in_specs) + [any_spec] * n_ex_in, out_specs=specs + [any_spec] * n_ex_out,
        out_shape=shapes + ex.out_shapes, scratch_shapes=list(scratch_shapes) + ex.scratch,
        input_output_aliases=aliases, compiler_params=_params())(*args, *ex.ins)
    return (res[0] if single else res[:n_out]), res[n_out:]


def _matmul(a, b, *, form, m, n, kdim, tm, tn, tk, out_dtype, name,
            a_spec=None, b_spec=None, out_spec=None, out_shape=None, residual=None, norm_w=None, exchange=None):
    ni, nj, nk = m // tm, n // tn, kdim // tk
    assert ni * tm == m and nj * tn == n and nk * tk == kdim, (name, m, n, kdim, tm, tn, tk)
    if form == "nn":
        dn = (((1,), (0,)), ((), ()))
        a_def = pl.BlockSpec((tm, tk), lambda i, j, k: (i, k))
        b_def = pl.BlockSpec((tk, tn), lambda i, j, k: (k, j))
    elif form == "nt":
        dn = (((1,), (1,)), ((), ()))
        a_def = pl.BlockSpec((tm, tk), lambda i, j, k: (i, k))
        b_def = pl.BlockSpec((tn, tk), lambda i, j, k: (j, k))
    else:
        dn = (((0,), (0,)), ((), ()))
        a_def = pl.BlockSpec((tk, tm), lambda i, j, k: (k, i))
        b_def = pl.BlockSpec((tk, tn), lambda i, j, k: (k, j))
    a_spec = a_def if a_spec is None else a_spec
    b_spec = b_def if b_spec is None else b_spec
    o_spec = pl.BlockSpec((tm, tn), lambda i, j, k: (i, j)) if out_spec is None else out_spec
    has_res = residual is not None
    has_norm = norm_w is not None
    assert not has_norm or (tn == n and out_dtype == F32)
    n_in = 2 + has_res + has_norm

    def body(*refs):
        a_ref, b_ref = refs[:2]
        r_ref = refs[2] if has_res else None
        nw_ref = refs[2 + has_res] if has_norm else None
        o_ref = refs[n_in]
        u_ref = refs[n_in + 1] if has_norm else None

        def product():
            return lax.dot_general(a_ref[...].astype(BF16), b_ref[...].astype(BF16), dn,
                                   preferred_element_type=F32)

        def finish(acc):
            if has_res:
                acc = acc + r_ref[...].astype(F32)
            o_ref[...] = acc.astype(o_ref.dtype)
            if has_norm:
                r = lax.rsqrt(jnp.mean(acc * acc, axis=-1, keepdims=True) + RMS_EPS)
                u_ref[...] = ((acc * r) * nw_ref[...]).astype(BF16)

        if nk == 1:
            finish(product())
        else:
            acc_ref = refs[n_in + 1 + has_norm]
            k = pl.program_id(2)

            @pl.when(k == 0)
            def _():
                acc_ref[...] = jnp.zeros_like(acc_ref)

            acc_ref[...] += product()

            @pl.when(k == nk - 1)
            def _():
                finish(acc_ref[...])

    in_specs = [a_spec, b_spec]
    args = [a, b]
    if has_res:
        in_specs.append(pl.BlockSpec((tm, tn), lambda i, j, k: (i, j)))
        args.append(residual)
    out_specs = o_spec
    out_shapes = jax.ShapeDtypeStruct((m, n) if out_shape is None else out_shape, out_dtype)
    if has_norm:
        in_specs.append(pl.BlockSpec((1, tn), lambda i, j, k: (0, j)))
        args.append(norm_w)
        out_specs = [o_spec, o_spec]
        out_shapes = [out_shapes, jax.ShapeDtypeStruct((m, n), BF16)]
    return _call(
        body, name=name, grid=(ni, nj, nk), in_specs=in_specs, out_specs=out_specs, out_shape=out_shapes,
        scratch_shapes=[] if nk == 1 else [pltpu.VMEM((tm, tn), F32)], args=args, exchange=exchange)


def _rms_fwd(h, w, name, exchange=None):
    t, d = h.shape
    tm = _pick(t, 832, 16)

    def body(h_ref, w_ref, u_ref):
        x = h_ref[...]
        r = lax.rsqrt(jnp.mean(x * x, axis=-1, keepdims=True) + RMS_EPS)
        u_ref[...] = ((x * r) * w_ref[...]).astype(BF16)

    return _call(
        body, name=name, grid=(t // tm,),
        in_specs=[pl.BlockSpec((tm, d), lambda i: (i, 0)), pl.BlockSpec((1, d), lambda i: (0, 0))],
        out_specs=pl.BlockSpec((tm, d), lambda i: (i, 0)),
        out_shape=jax.ShapeDtypeStruct((t, d), BF16), args=(h, w), exchange=exchange)


def _rms_first(x, meta, w, name, exchange=None):
    seq, d = x.shape
    t = seq + LEAD
    tm = LEAD

    def body(x_ref, meta_ref, w_ref, h_ref, u_ref):
        i = pl.program_id(0)

        @pl.when(i == 0)
        def _():
            h_ref[0:PAD_ROWS, :] = jnp.zeros((PAD_ROWS, d), F32)
            h_ref[PAD_ROWS:LEAD, :] = meta_ref[...]

        @pl.when(i > 0)
        def _():
            h_ref[...] = x_ref[...]

        hv = h_ref[...]
        r = lax.rsqrt(jnp.mean(hv * hv, axis=-1, keepdims=True) + RMS_EPS)
        u_ref[...] = ((hv * r) * w_ref[...]).astype(BF16)

    row = pl.BlockSpec((tm, d), lambda i: (i, 0))
    return _call(
        body, name=name, grid=(t // tm,),
        in_specs=[pl.BlockSpec((tm, d), lambda i: (jnp.maximum(i - 1, 0), 0)),
                  pl.BlockSpec((N_META, d), lambda i: (0, 0)), pl.BlockSpec((1, d), lambda i: (0, 0))],
        out_specs=[row, row],
        out_shape=[jax.ShapeDtypeStruct((t, d), F32), jax.ShapeDtypeStruct((t, d), BF16)],
        args=(x, meta, w), exchange=exchange)


def _rms_bwd(h, w, du, dres, name):
    t, d = h.shape
    tm = _pick(t, 832, 16)

    def body(h_ref, w_ref, du_ref, dres_ref, dh_ref, dw_ref):
        i = pl.program_id(0)
        x = h_ref[...]
        g = du_ref[...].astype(F32)
        r = lax.rsqrt(jnp.mean(x * x, axis=-1, keepdims=True) + RMS_EPS)
        gw = g * w_ref[...]
        dot = jnp.mean(gw * x, axis=-1, keepdims=True)
        dh_ref[...] = dres_ref[...] + (r * gw - x * ((r * r * r) * dot))
        part = jnp.sum(g * (x * r), axis=0, keepdims=True)

        @pl.when(i == 0)
        def _():
            dw_ref[...] = jnp.zeros_like(dw_ref)

        dw_ref[...] += jnp.broadcast_to(part, dw_ref.shape)

    row = pl.BlockSpec((tm, d), lambda i: (i, 0))
    return pl.pallas_call(
        body, name=name, grid=(t // tm,),
        in_specs=[row, pl.BlockSpec((1, d), lambda i: (0, 0)), row, row],
        out_specs=[row, pl.BlockSpec((8, d), lambda i: (0, 0))],
        out_shape=[jax.ShapeDtypeStruct((t, d), F32), jax.ShapeDtypeStruct((8, d), F32)],
        compiler_params=_params(),
    )(h, w, du, dres)


def _rms_bwd_first(h, w, du, dres, name):
    t, d = h.shape
    sub = LEAD
    per = _pick((t - LEAD) // sub, 8, 1)
    tmo = per * sub
    n_out = (t - LEAD) // tmo
    n_blocks = t // sub

    def body(*refs):
        w_ref = refs[0]
        h_refs, du_refs, dres_refs = refs[1:1 + per], refs[1 + per:1 + 2 * per], refs[1 + 2 * per:1 + 3 * per]
        gx_ref, dmeta_ref, dw_ref = refs[1 + 3 * per:]
        i = pl.program_id(0)

        def piece(q):
            x = h_refs[q][...]
            g = du_refs[q][...].astype(F32)
            r = lax.rsqrt(jnp.mean(x * x, axis=-1, keepdims=True) + RMS_EPS)
            gw = g * w_ref[...]
            dot = jnp.mean(gw * x, axis=-1, keepdims=True)
            dh = dres_refs[q][...] + (r * gw - x * ((r * r * r) * dot))
            return dh, jnp.sum(g * (x * r), axis=0, keepdims=True)

        @pl.when(i == 0)
        def _():
            dh, part = piece(0)
            dmeta_ref[...] = dh[PAD_ROWS:LEAD, :]
            dw_ref[...] = jnp.broadcast_to(part, dw_ref.shape)

        @pl.when(i > 0)
        def _():
            for q in range(per):
                dh, part = piece(q)
                gx_ref[q * sub:(q + 1) * sub, :] = dh
                dw_ref[...] += jnp.broadcast_to(part, dw_ref.shape)

    def piece_spec(q):
        return pl.BlockSpec((sub, d), lambda i: (jnp.clip(1 + per * (i - 1) + q, 0, n_blocks - 1), 0))

    pieces = [piece_spec(q) for q in range(per)]
    return pl.pallas_call(
        body, name=name, grid=(n_out + 1,),
        in_specs=[pl.BlockSpec((1, d), lambda i: (0, 0))] + pieces * 3,
        out_specs=[pl.BlockSpec((tmo, d), lambda i: (jnp.maximum(i - 1, 0), 0)),
                   pl.BlockSpec((N_META, d), lambda i: (0, 0)), pl.BlockSpec((8, d), lambda i: (0, 0))],
        out_shape=[jax.ShapeDtypeStruct((t - LEAD, d), F32), jax.ShapeDtypeStruct((N_META, d), F32),
                   jax.ShapeDtypeStruct((8, d), F32)],
        compiler_params=_params(),
    )(w, *([h] * per), *([du] * per), *([dres] * per))


def _final_loss(h, w, tgt, name):
    t, d = h.shape
    sub = LEAD
    per = _pick(t // sub, 5, 1)
    tm = per * sub

    def body(h_ref, w_ref, *rest):
        t_refs, (dh_ref, dw_ref, loss_ref) = rest[:per], rest[per:]
        i = pl.program_id(0)

        @pl.when(i == 0)
        def _():
            dw_ref[...] = jnp.zeros_like(dw_ref)
            loss_ref[...] = jnp.zeros_like(loss_ref)

        for q in range(per):
            sl = slice(q * sub, (q + 1) * sub)
            x = h_ref[sl, :]
            r = lax.rsqrt(jnp.mean(x * x, axis=-1, keepdims=True) + RMS_EPS)
            yn = x * r
            y = yn * w_ref[...]
            rows = i * tm + q * sub + lax.broadcasted_iota(jnp.int32, (sub, 1), 0)
            diff = jnp.where(rows >= LEAD, y - t_refs[q][...], 0.0)
            tile_loss = 0.5 * jnp.sum(jnp.mean(diff * diff, axis=-1, keepdims=True), axis=0, keepdims=True)
            dy = diff / d
            gw = dy * w_ref[...]
            dot = jnp.mean(gw * x, axis=-1, keepdims=True)
            dh_ref[sl, :] = r * gw - x * ((r * r * r) * dot)
            dw_ref[...] += jnp.broadcast_to(jnp.sum(dy * yn, axis=0, keepdims=True), dw_ref.shape)
            loss_ref[...] += jnp.broadcast_to(tile_loss, loss_ref.shape)

    row = pl.BlockSpec((tm, d), lambda i: (i, 0))
    piece = [pl.BlockSpec((sub, d), functools.partial(lambda i, q: (jnp.maximum(i * per + q - 1, 0), 0), q=q))
             for q in range(per)]
    return pl.pallas_call(
        body, name=name, grid=(t // tm,),
        in_specs=[row, pl.BlockSpec((1, d), lambda i: (0, 0))] + piece,
        out_specs=[row, pl.BlockSpec((8, d), lambda i: (0, 0)), pl.BlockSpec((8, LANES), lambda i: (0, 0))],
        out_shape=[jax.ShapeDtypeStruct((t, d), F32), jax.ShapeDtypeStruct((8, d), F32),
                   jax.ShapeDtypeStruct((8, LANES), F32)],
        compiler_params=_params(),
    )(h, w, *([tgt] * per))


def _conv_gate_fwd(p0, conv_w, name, exchange=None):
    t = p0.shape[0]
    ec = p0.shape[1] // 8
    tm = _pick(t, 832, 16)
    nt = t // tm

    def body(p_ref, w_ref, g_ref, y_ref, ext_ref):
        i = pl.program_id(1)

        @pl.when(i == 0)
        def _():
            ext_ref[0:8, :] = jnp.zeros((8, ec), F32)

        for rows, cols in _strips(tm, ec):
            cg = p_ref[rows, _shift(cols, ec)].astype(F32)
            xin = p_ref[rows, _shift(cols, 2 * ec)].astype(F32)
            ext_ref[_shift(rows, 8), cols] = cg * xin
        for rows, cols in _strips(tm, ec):
            y = (w_ref[0:1, cols] * ext_ref[_shift(rows, 6), cols] + w_ref[1:2, cols] * ext_ref[_shift(rows, 7), cols]
                 + w_ref[2:3, cols] * ext_ref[_shift(rows, 8), cols])
            bg = p_ref[rows, cols].astype(F32)
            z = p_ref[rows, _shift(cols, 3 * ec)].astype(F32)
            y_ref[rows, cols] = y.astype(BF16)
            g_ref[rows, cols] = ((z * _sigmoid(z)) * bg * y).astype(BF16)
        ext_ref[0:8, :] = ext_ref[tm:tm + 8, :]

    out = pl.BlockSpec((tm, ec), lambda j, i: (i, j))
    return _call(
        body, name=name, grid=(2, nt),
        in_specs=[pl.BlockSpec((tm, 4 * ec), lambda j, i: (i, j)), pl.BlockSpec((3, ec), lambda j, i: (0, j))],
        out_specs=[out, out],
        out_shape=[jax.ShapeDtypeStruct((t, 2 * ec), BF16)] * 2,
        scratch_shapes=[pltpu.VMEM((tm + 8, ec), F32)], args=(p0, conv_w), exchange=exchange)


def _conv_gate_bwd(p0, y, dg, conv_w, name, exchange=None):
    t = p0.shape[0]
    ec = p0.shape[1] // 8
    tm = _pick(t, 416, 16)
    nt = t // tm

    def body(p_ref, y_ref, dg_ref, w_ref, dp_ref, dw_ref, ext_ref):
        i = pl.program_id(1)

        @pl.when(i == 0)
        def _():
            ext_ref[tm:tm + 8, :] = jnp.zeros((8, ec), F32)
            dw_ref[...] = jnp.zeros_like(dw_ref)

        for rows, cols in _strips(tm, ec):
            bg = p_ref[rows, cols].astype(F32)
            z = p_ref[rows, _shift(cols, 3 * ec)].astype(F32)
            yv = y_ref[rows, cols].astype(F32)
            dgv = dg_ref[rows, cols].astype(F32)
            sig = _sigmoid(z)
            sz = z * sig
            dp_ref[rows, _shift(cols, 3 * ec)] = (dgv * bg * yv * (sig * (1.0 + z * (1.0 - sig)))).astype(BF16)
            dp_ref[rows, cols] = (dgv * sz * yv).astype(BF16)
            ext_ref[rows, cols] = dgv * sz * bg
        acc = None
        for rows, cols in _strips(tm, ec):
            if rows.start == 0:
                acc = [jnp.zeros((STRIP_ROWS, cols.stop - cols.start), F32) for _ in range(3)]
            dy = ext_ref[rows, cols]
            dy1 = ext_ref[_shift(rows, 1), cols]
            dy2 = ext_ref[_shift(rows, 2), cols]
            cg = p_ref[rows, _shift(cols, ec)].astype(F32)
            xin = p_ref[rows, _shift(cols, 2 * ec)].astype(F32)
            da = w_ref[0:1, cols] * dy2 + w_ref[1:2, cols] * dy1 + w_ref[2:3, cols] * dy
            dp_ref[rows, _shift(cols, ec)] = (da * xin).astype(BF16)
            dp_ref[rows, _shift(cols, 2 * ec)] = (da * cg).astype(BF16)
            a = cg * xin
            acc = [acc[0] + a * dy2, acc[1] + a * dy1, acc[2] + a * dy]
            if rows.stop == tm:
                for tap in range(3):
                    dw_ref[tap:tap + 1, cols] += jnp.sum(acc[tap], axis=0, keepdims=True)
        ext_ref[tm:tm + 8, :] = ext_ref[0:8, :]

    rev = lambda j, i: (nt - 1 - i, j)
    return _call(
        body, name=name, grid=(2, nt),
        in_specs=[pl.BlockSpec((tm, 4 * ec), rev), pl.BlockSpec((tm, ec), rev), pl.BlockSpec((tm, ec), rev),
                  pl.BlockSpec((3, ec), lambda j, i: (0, j))],
        out_specs=[pl.BlockSpec((tm, 4 * ec), rev), pl.BlockSpec((8, ec), lambda j, i: (0, j))],
        out_shape=[jax.ShapeDtypeStruct((t, 8 * ec), BF16), jax.ShapeDtypeStruct((8, 2 * ec), F32)],
        scratch_shapes=[pltpu.VMEM((tm + 8, ec), F32)], args=(p0, y, dg, conv_w), exchange=exchange)


def _log_sigmoid(x):
    return jnp.minimum(x, 0.0) - jnp.log(1.0 + jnp.exp(-jnp.abs(x)))


def _gates_fwd(graw, graw_t, bias_row, bias_col, name):
    t = graw.shape[0]
    tm = _pick(t, 640, 128)
    cpt = tm // CHUNK
    nh = N_HEADS

    def body(g_ref, gt_ref, br_ref, bc_ref, colf_ref, rowf_ref):
        i = pl.program_id(0)
        gc = g_ref[...] + br_ref[...]
        rows = i * tm + lax.broadcasted_iota(jnp.int32, (tm, 1), 0)
        live = rows >= PAD_ROWS
        lf = jnp.where(live, _log_sigmoid(gc), 0.0)
        li = jnp.where(live, gc, NEG)
        gt = gt_ref[...] + bc_ref[...]
        cols = i * tm + lax.broadcasted_iota(jnp.int32, (1, tm), 1)
        live_t = cols >= PAD_ROWS
        lf_t = jnp.where(live_t, _log_sigmoid(gt), 0.0)
        li_t = jnp.where(live_t, gt, NEG)
        ri = lax.broadcasted_iota(jnp.int32, (CHUNK, CHUNK), 0)
        ci = lax.broadcasted_iota(jnp.int32, (CHUNK, CHUNK), 1)
        lower = (ri >= ci).astype(F32)
        upper = (ri <= ci).astype(F32)
        lane = lax.broadcasted_iota(jnp.int32, (CHUNK, LANES), 1)
        sub = lax.broadcasted_iota(jnp.int32, (8, CHUNK), 0)
        for c in range(cpt):
            sl = slice(c * CHUNK, (c + 1) * CHUNK)
            b_all = jnp.dot(lower, lf[sl, :], precision=HIGHEST, preferred_element_type=F32)
            bt_all = jnp.dot(lf_t[:, sl], upper, precision=HIGHEST, preferred_element_type=F32)
            for h in range(nh):
                b = b_all[:, nh + h:nh + h + 1]
                r = li[sl, h:h + 1] - b
                pre = gc[sl, nh + h:nh + h + 1]
                colf_ref[h, sl, :] = jnp.where(lane == 0, b, jnp.where(lane == 1, r, jnp.where(lane == 2, pre, 0.0)))
                b_row = bt_all[nh + h:nh + h + 1, :]
                r_row = li_t[h:h + 1, sl] - b_row
                rowf_ref[h, c, :, 0:CHUNK] = jnp.where(sub == 0, r_row, jnp.where(sub == 1, b_row, 0.0))
                rowf_ref[h, c, :, CHUNK:LANES] = jnp.zeros((8, LANES - CHUNK), F32)

    return pl.pallas_call(
        body, name=name, grid=(t // tm,),
        in_specs=[pl.BlockSpec((tm, LANES), lambda i: (i, 0)), pl.BlockSpec((16, tm), lambda i: (0, i)),
                  pl.BlockSpec((1, LANES), lambda i: (0, 0)), pl.BlockSpec((16, 1), lambda i: (0, 0))],
        out_specs=[pl.BlockSpec((nh, tm, LANES), lambda i: (0, i, 0)),
                   pl.BlockSpec((nh, cpt, 8, LANES), lambda i: (0, i, 0, 0))],
        out_shape=[jax.ShapeDtypeStruct((nh, t, LANES), F32),
                   jax.ShapeDtypeStruct((nh, t // CHUNK, 8, LANES), F32)],
        compiler_params=_params(),
    )(graw, graw_t, bias_row, bias_col)


def _gates_bwd(gstat, colf, dp1, name):
    nh, t, _ = gstat.shape
    tm = _pick(t, 640, 128)
    cpt = tm // CHUNK
    nt = t // tm
    nc = t // CHUNK
    gate_block = dp1.shape[1] // LANES - 1

    def body(gs_ref, nx_ref, colf_ref, dp_any, dg_ref, db_ref):
        i = pl.program_id(0)
        ri = lax.broadcasted_iota(jnp.int32, (CHUNK, CHUNK), 0)
        ci = lax.broadcasted_iota(jnp.int32, (CHUNK, CHUNK), 1)
        upper = (ri <= ci).astype(F32)
        lane = lax.broadcasted_iota(jnp.int32, (CHUNK, LANES), 1)
        total = jnp.zeros((1, LANES), F32)
        for c in range(cpt):
            sl = slice(c * CHUNK, (c + 1) * CHUNK)
            rows = i * tm + c * CHUNK + lax.broadcasted_iota(jnp.int32, (CHUNK, 1), 0)
            live = rows >= PAD_ROWS
            acc = jnp.zeros((CHUNK, LANES), F32)
            for h in range(nh):
                blk = gs_ref[h, sl, :]
                rev = jnp.dot(upper, blk, precision=HIGHEST, preferred_element_type=F32)
                if c + 1 < cpt:
                    carry = gs_ref[h, (c + 1) * CHUNK:(c + 1) * CHUNK + 1, 2:3]
                else:
                    carry = jnp.where(i == nt - 1, 0.0, nx_ref[h, 0:1, 2:3])
                dlogf = rev[:, 0:1] + carry
                pre = colf_ref[h, sl, 2:3]
                dgf = jnp.where(live, dlogf * (1.0 - _sigmoid(pre)), 0.0)
                dgi = jnp.where(live, blk[:, 1:2], 0.0)
                acc = acc + jnp.where(lane == h, dgi, 0.0) + jnp.where(lane == nh + h, dgf, 0.0)
            dg_ref[sl, :] = acc.astype(BF16)
            total = total + jnp.sum(acc, axis=0, keepdims=True)

        @pl.when(i == 0)
        def _():
            db_ref[...] = jnp.zeros_like(db_ref)

        db_ref[...] += jnp.broadcast_to(total, db_ref.shape)

    return pl.pallas_call(
        body, name=name, grid=(nt,),
        in_specs=[pl.BlockSpec((nh, tm, LANES), lambda i: (0, i, 0)),
                  pl.BlockSpec((nh, CHUNK, LANES), lambda i: (0, jnp.minimum((i + 1) * cpt, nc - 1), 0)),
                  pl.BlockSpec((nh, tm, LANES), lambda i: (0, i, 0)),
                  pl.BlockSpec(memory_space=pl.ANY)],
        out_specs=[pl.BlockSpec((tm, LANES), lambda i: (i, gate_block)), pl.BlockSpec((8, LANES), lambda i: (0, 0))],
        out_shape=[jax.ShapeDtypeStruct(dp1.shape, dp1.dtype), jax.ShapeDtypeStruct((8, LANES), F32)],
        input_output_aliases={3: 0},
        compiler_params=_params(),
    )(gstat, gstat, colf, dp1)


NT_DIMS = (((1,), (1,)), ((), ()))
TN_DIMS = (((0,), (0,)), ((), ()))


def _dot(a, b):
    return jnp.dot(a.astype(BF16), b.astype(BF16), preferred_element_type=F32)


def _dot_nt(a, b):
    return lax.dot_general(a.astype(BF16), b.astype(BF16), NT_DIMS, preferred_element_type=F32)


def _dot_tn(a, b):
    return lax.dot_general(a.astype(BF16), b.astype(BF16), TN_DIMS, preferred_element_type=F32)


def _chunk_gates(colf_ref, rowf_ref, m_prev, width=CHUNK):
    b = colf_ref[:, 0:1]
    rcol = colf_ref[:, 1:2]
    rrow = rowf_ref[0:1, 0:width]
    ri = lax.broadcasted_iota(jnp.int32, (CHUNK, width), 0)
    ci = lax.broadcasted_iota(jnp.int32, (CHUNK, width), 1)
    log_d = jnp.where(ri >= ci, b + rrow, NEG)
    m_row = jnp.maximum(b + m_prev, jnp.max(log_d, axis=-1, keepdims=True))
    dmat = jnp.exp(log_d - m_row)
    inter = jnp.exp(b + m_prev - m_row)
    b_last = b[CHUNK - 1:CHUNK, :]
    log_w = rcol + b_last
    m_new = jnp.maximum(b_last + m_prev, jnp.max(log_w, axis=0, keepdims=True))
    decay = jnp.exp(b_last + m_prev - m_new)
    w = jnp.exp(log_w - m_new)
    return m_row, dmat, inter, m_new, decay, w


def _mlstm_fwd(p1, colf, rowf, dk, dv, name):
    t = p1.shape[0]
    nh = N_HEADS
    nc = t // CHUNK
    hw = 2 * dk + dv
    cps = CHUNKS_PER_STEP if nc % CHUNKS_PER_STEP == 0 else 1
    rows_per_step = cps * CHUNK

    def body(p_ref, colf_ref, rowf_ref, hh_ref, stat_ref, cs_ref, ns_ref, *state):
        c_refs, n_refs, m_refs = state[:nh], state[nh:2 * nh], state[2 * nh:]
        c = pl.program_id(0)

        @pl.when(c == 0)
        def _():
            for ref in state:
                ref[...] = jnp.zeros_like(ref)

        for cc in range(cps):
            rows = pl.ds(cc * CHUNK, CHUNK)
            for h in range(nh):
                head(p_ref.at[rows, pl.ds(h * hw, hw)], colf_ref.at[h, rows], rowf_ref.at[h, cc],
                     hh_ref.at[rows, pl.ds(h * dv, dv)], stat_ref.at[h, rows], cs_ref.at[h, cc],
                     ns_ref.at[h, cc], c_refs[h], n_refs[h], m_refs[h])

    def head(p_ref, colf_ref, rowf_ref, hh_ref, stat_ref, cs_ref, ns_ref, c_ref, n_ref, m_ref):
        m_prev = m_ref[...]
        n_prev = n_ref[...]
        c_prev = c_ref[...]
        cs_ref[...] = c_prev.astype(BF16)
        sub = lax.broadcasted_iota(jnp.int32, (8, dk), 0)
        ns_ref[...] = jnp.where(sub == 0, n_prev, jnp.where(sub == 1, m_prev, 0.0))

        q = p_ref[:, 0:dk]
        k = p_ref[:, dk:2 * dk]
        v = p_ref[:, 2 * dk:2 * dk + dv]
        m_row, dmat, inter, m_new, decay, w = _chunk_gates(colf_ref, rowf_ref, m_prev)
        s = _dot_nt(q, k) * dmat
        num = _dot(s, v) + inter * _dot(q, c_prev)
        den = jnp.sum(s, axis=-1, keepdims=True) + inter * jnp.sum(q.astype(F32) * n_prev, axis=-1, keepdims=True)
        denom = jnp.maximum(jnp.abs(den), jnp.exp(-m_row))
        hh_ref[...] = num * (1.0 / denom)
        lane = lax.broadcasted_iota(jnp.int32, (CHUNK, LANES), 1)
        stat_ref[...] = jnp.where(lane == 0, den, 0.0)

        wk = w * k.astype(F32)
        c_ref[...] = decay * c_prev + _dot_tn(wk, v)
        n_ref[...] = decay * n_prev + jnp.sum(wk, axis=0, keepdims=True)
        m_ref[...] = m_new

    return pl.pallas_call(
        body, name=name, grid=(nc // cps,),
        in_specs=[pl.BlockSpec((rows_per_step, nh * hw), lambda c: (c, 0)),
                  pl.BlockSpec((nh, rows_per_step, LANES), lambda c: (0, c, 0)),
                  pl.BlockSpec((nh, cps, 8, LANES), lambda c: (0, c, 0, 0))],
        out_specs=[pl.BlockSpec((rows_per_step, nh * dv), lambda c: (c, 0)),
                   pl.BlockSpec((nh, rows_per_step, LANES), lambda c: (0, c, 0)),
                   pl.BlockSpec((nh, cps, dk, dv), lambda c: (0, c, 0, 0)),
                   pl.BlockSpec((nh, cps, 8, dk), lambda c: (0, c, 0, 0))],
        out_shape=[jax.ShapeDtypeStruct((t, nh * dv), F32), jax.ShapeDtypeStruct((nh, t, LANES), F32),
                   jax.ShapeDtypeStruct((nh, nc, dk, dv), BF16), jax.ShapeDtypeStruct((nh, nc, 8, dk), F32)],
        scratch_shapes=([pltpu.VMEM((dk, dv), F32)] * nh + [pltpu.VMEM((1, dk), F32)] * nh
                        + [pltpu.VMEM((1, 1), F32)] * nh),
        compiler_params=_params(),
    )(p1, colf, rowf)


def _head_gate_fwd(hh, p1, head_w, dv, name):
    t = hh.shape[0]
    nh = N_HEADS
    e = nh * dv
    tm = _pick(t, 416, 16)

    def body(hh_ref, oz_ref, w_ref, hm_ref):
        for h in range(nh):
            cols = slice(h * dv, (h + 1) * dv)
            x = hh_ref[:, cols]
            o = oz_ref[:, 2 * h * dv:(2 * h + 1) * dv].astype(F32)
            z = oz_ref[:, (2 * h + 1) * dv:(2 * h + 2) * dv].astype(F32)
            r = lax.rsqrt(jnp.mean(x * x, axis=-1, keepdims=True) + RMS_EPS)
            hn = (x * r) * w_ref[:, cols]
            hm_ref[:, cols] = (hn * _sigmoid(o) * (z * _sigmoid(z))).astype(BF16)

    return pl.pallas_call(
        body, name=name, grid=(t // tm,),
        in_specs=[pl.BlockSpec((tm, e), lambda i: (i, 0)), pl.BlockSpec((tm, 2 * e), lambda i: (i, 1)),
                  pl.BlockSpec((1, e), lambda i: (0, 0))],
        out_specs=pl.BlockSpec((tm, e), lambda i: (i, 0)),
        out_shape=jax.ShapeDtypeStruct((t, e), BF16),
        compiler_params=_params(),
    )(hh, p1, head_w)


def _mlstm_bwd(p1, colf, rowf, head_w, hh, stat, csave, nsave, dhm, n_cols, dk, dv, name, exchange=None):
    t = p1.shape[0]
    nh = N_HEADS
    nc = t // CHUNK
    hw = 2 * dk + dv
    cps = CHUNKS_PER_STEP if nc % CHUNKS_PER_STEP == 0 else 1
    rows_per_step = cps * CHUNK

    def body(p_ref, colf_ref, rowf_ref, w_ref, hh_ref, stat_ref, cs_ref, ns_ref, dhm_ref,
             dp_ref, gs_ref, dw_ref, *state):
        dc_refs, dn_refs, dwacc_refs = state[:nh], state[nh:2 * nh], state[2 * nh:]
        c = pl.program_id(0)

        @pl.when(c == 0)
        def _():
            for ref in state:
                ref[...] = jnp.zeros_like(ref)

        for cc in reversed(range(cps)):
            rows = pl.ds(cc * CHUNK, CHUNK)
            for h in range(nh):
                cols = pl.ds(h * dv, dv)
                oz = pl.ds(nh * hw + h * 2 * dv, 2 * dv)
                head(p_ref.at[rows, pl.ds(h * hw, hw)], p_ref.at[rows, oz], colf_ref.at[h, rows], rowf_ref.at[h, cc],
                     w_ref.at[:, cols], hh_ref.at[rows, cols], stat_ref.at[h, rows], cs_ref.at[h, cc],
                     ns_ref.at[h, cc], dhm_ref.at[rows, cols], dp_ref.at[rows, pl.ds(h * hw, hw)],
                     dp_ref.at[rows, oz], gs_ref.at[h, rows], dwacc_refs[h], dc_refs[h], dn_refs[h])

        @pl.when(c == nc // cps - 1)
        def _():
            for h in range(nh):
                dw_ref[h] = dwacc_refs[h][...]

    def head(p_ref, oz_ref, colf_ref, rowf_ref, w_ref, hh_ref, stat_ref, cs_ref, ns_ref, dhm_ref,
             dp_ref, doz_ref, gs_ref, dw_ref, dc_ref, dn_ref):
        q = p_ref[:, 0:dk]
        k = p_ref[:, dk:2 * dk]
        v = p_ref[:, 2 * dk:2 * dk + dv]
        o = oz_ref[:, 0:dv].astype(F32)
        z = oz_ref[:, dv:2 * dv].astype(F32)
        qf = q.astype(F32)
        kf = k.astype(F32)
        n_prev = ns_ref[0:1, :]
        m_prev = ns_ref[1:2, 0:1]
        c_prev = cs_ref[...]
        m_row, dmat, inter, m_new, decay, w = _chunk_gates(colf_ref, rowf_ref, m_prev)

        hh = hh_ref[...]
        dhm_v = dhm_ref[...].astype(F32)
        so = _sigmoid(o)
        sg = _sigmoid(z)
        sz = z * sg
        r = lax.rsqrt(jnp.mean(hh * hh, axis=-1, keepdims=True) + RMS_EPS)
        hn = (hh * r) * w_ref[...]
        dhn = dhm_v * so * sz
        doz_ref[:, 0:dv] = (dhm_v * hn * sz * (so * (1.0 - so))).astype(BF16)
        doz_ref[:, dv:2 * dv] = (dhm_v * hn * so * (sg * (1.0 + z * (1.0 - sg)))).astype(BF16)
        dw_ref[...] += jnp.broadcast_to(jnp.sum(dhn * (hh * r), axis=0, keepdims=True), dw_ref.shape)
        gwn = dhn * w_ref[...]
        dhh = r * gwn - hh * ((r * r * r) * jnp.mean(gwn * hh, axis=-1, keepdims=True))

        den = stat_ref[:, 0:1]
        floor = jnp.exp(-m_row)
        denom = jnp.maximum(jnp.abs(den), floor)
        inv = 1.0 / denom
        dnum = dhh * inv
        hdot = jnp.sum(dhh * hh, axis=-1, keepdims=True)
        dden = jnp.where(jnp.abs(den) > floor, -(hdot * inv) * jnp.sign(den), 0.0)
        s = _dot_nt(q, k) * dmat
        dqk = (_dot_nt(dnum, v) + dden) * dmat
        dc_new = dc_ref[...]
        dn_new = dn_ref[...]
        idd = inter * dden
        dq = _dot(dqk, k) + inter * _dot_nt(dnum, c_prev) + idd * n_prev
        dkv = _dot_tn(dqk, q) + w * (_dot_nt(v, dc_new) + dn_new)
        dvv = _dot_tn(s, dnum) + w * _dot(k, dc_new)
        dp_ref[:, 0:dk] = dq.astype(BF16)
        dp_ref[:, dk:2 * dk] = dkv.astype(BF16)
        dp_ref[:, 2 * dk:2 * dk + dv] = dvv.astype(BF16)
        qdq = jnp.sum(qf * dq, axis=-1, keepdims=True)
        kdk = jnp.sum(kf * dkv, axis=-1, keepdims=True)
        dc_prev = decay * dc_new + _dot_tn(inter * qf, dnum)
        dn_prev = decay * dn_new + jnp.sum(idd * qf, axis=0, keepdims=True)
        dc_ref[...] = dc_prev
        dn_ref[...] = dn_prev
        cross = (jnp.sum(jnp.sum(c_prev.astype(F32) * dc_prev, axis=-1, keepdims=True), axis=0, keepdims=True)
                 + jnp.sum(n_prev * dn_prev, axis=-1, keepdims=True))
        lane = lax.broadcasted_iota(jnp.int32, (CHUNK, LANES), 1)
        gs_ref[...] = jnp.where(lane == 0, qdq - kdk, jnp.where(lane == 1, kdk, jnp.where(lane == 2, cross, 0.0)))

    ns = nc // cps
    rc = lambda c: (ns - 1 - c, 0)
    rc3 = lambda c: (0, ns - 1 - c, 0)
    rc4 = lambda c: (0, ns - 1 - c, 0, 0)
    return _call(
        body, name=name, grid=(ns,),
        in_specs=[pl.BlockSpec((rows_per_step, nh * (hw + 2 * dv)), rc),
                  pl.BlockSpec((nh, rows_per_step, LANES), rc3),
                  pl.BlockSpec((nh, cps, 8, LANES), rc4),
                  pl.BlockSpec((1, nh * dv), lambda c: (0, 0)),
                  pl.BlockSpec((rows_per_step, nh * dv), rc),
                  pl.BlockSpec((nh, rows_per_step, LANES), rc3),
                  pl.BlockSpec((nh, cps, dk, dv), rc4),
                  pl.BlockSpec((nh, cps, 8, dk), rc4),
                  pl.BlockSpec((rows_per_step, nh * dv), rc)],
        out_specs=[pl.BlockSpec((rows_per_step, nh * (hw + 2 * dv)), rc),
                   pl.BlockSpec((nh, rows_per_step, LANES), rc3),
                   pl.BlockSpec((nh, 8, dv), lambda c: (0, 0, 0))],
        out_shape=[jax.ShapeDtypeStruct((t, n_cols), BF16), jax.ShapeDtypeStruct((nh, t, LANES), F32),
                   jax.ShapeDtypeStruct((nh, 8, dv), F32)],
        scratch_shapes=([pltpu.VMEM((dk, dv), F32)] * nh + [pltpu.VMEM((1, dk), F32)] * nh
                        + [pltpu.VMEM((8, dv), F32)] * nh),
        args=(p1, colf, rowf, head_w, hh, stat, csave, nsave, dhm), exchange=exchange)


def _position():
    return lax.axis_index("x"), lax.axis_index("y"), lax.axis_index("c")


class _Exchange:
    def __init__(self, ins, out_shapes, start, mid, finish):
        n = len(ins)
        self.ins, self.out_shapes = list(ins), list(out_shapes)
        self.start, self.mid, self.finish = start, mid, finish
        self.scratch = [pltpu.SemaphoreType.DMA((n, 7)), pltpu.SemaphoreType.DMA((n, 7)),
                        pltpu.SemaphoreType.DMA((n,))]


def _gather_exchange(shards):
    n = len(shards)

    def plan(ins, outs, sems):
        send_sems, recv_sems, local_sems = sems
        x, y, c = _position()
        me, sibling = (x, y, c), (x, y, 1 - c)
        chips = [(1 - x, y), (x, 1 - y), (1 - x, 1 - y)]

        def copy(a, k, block, to, src=None):
            px, py, pc = block
            dst = outs[a].at[4 * px + 2 * py + pc]
            return pltpu.make_async_remote_copy(
                src_ref=dst if src is None else src, dst_ref=dst,
                send_sem=send_sems.at[a, k], recv_sem=recv_sems.at[a, k],
                device_id=to, device_id_type=MESH)

        def mine():
            return [pltpu.make_async_copy(ins[a], outs[a].at[4 * x + 2 * y + c], local_sems.at[a])
                    for a in range(n)]

        def first():
            out = []
            for a in range(n):
                out.append(copy(a, 0, me, sibling, src=ins[a]))
                out += [copy(a, 1 + j, me, (*chip, c), src=ins[a]) for j, chip in enumerate(chips)]
            return out

        def ici_in():
            return [copy(a, 1 + j, (*chip, c), me) for j, chip in enumerate(chips) for a in range(n)]

        def passed():
            return [copy(a, 4 + j, (*chip, c), sibling) for j, chip in enumerate(chips) for a in range(n)]

        def d2d_in():
            return ([copy(a, 0, sibling, me) for a in range(n)]
                    + [copy(a, 4 + j, (*chip, 1 - c), me) for j, chip in enumerate(chips) for a in range(n)])

        return mine, first, ici_in, passed, d2d_in

    def start(ins, outs, sems):
        mine, first, _, _, _ = plan(ins, outs, sems)
        for cp in mine() + first():
            cp.start()

    def mid(ins, outs, sems):
        _, _, ici_in, passed, _ = plan(ins, outs, sems)
        for arrived, onward in zip(ici_in(), passed()):
            arrived.wait_recv()
            onward.start()

    def finish(ins, outs, sems):
        mine, first, _, passed, d2d_in = plan(ins, outs, sems)
        for cp in d2d_in():
            cp.wait_recv()
        for cp in first() + passed():
            cp.wait_send()
        for cp in mine():
            cp.wait()

    shapes = [jax.ShapeDtypeStruct((N_DEV,) + s.shape, s.dtype) for s in shards]
    return _Exchange(shards, shapes, start, mid, finish)


def _scatter_exchange(fulls):
    n = len(fulls)

    def plan(ins, outs, sems):
        send_sems, recv_sems, local_sems = sems
        x, y, c = _position()
        my_slot = 4 * x + 2 * y + c

        def mine():
            return [pltpu.make_async_copy(ins[a].at[my_slot], outs[a].at[my_slot], local_sems.at[a])
                    for a in range(n)]

        def remote(arriving):
            out = []
            for kk in (1, 2, 4, 6, 3, 5, 7):
                kx, ky, kc = (kk >> 2) & 1, (kk >> 1) & 1, kk & 1
                px = 1 - x if kx else x
                py = 1 - y if ky else y
                pc = 1 - c if kc else c
                peer_slot = 4 * px + 2 * py + pc
                for a in range(n):
                    out.append(pltpu.make_async_remote_copy(
                        src_ref=ins[a].at[peer_slot], dst_ref=outs[a].at[peer_slot if arriving else my_slot],
                        send_sem=send_sems.at[a, kk - 1], recv_sem=recv_sems.at[a, kk - 1],
                        device_id=(px, py, pc), device_id_type=MESH))
            return out

        return mine, remote

    def start(ins, outs, sems):
        mine, remote = plan(ins, outs, sems)
        for cp in mine() + remote(False):
            cp.start()

    def finish(ins, outs, sems):
        mine, remote = plan(ins, outs, sems)
        for cp in remote(True):
            cp.wait_recv()
        for cp in remote(False):
            cp.wait_send()
        for cp in mine():
            cp.wait()

    shapes = [jax.ShapeDtypeStruct(f.shape, f.dtype) for f in fulls]
    return _Exchange(fulls, shapes, start, None, finish)


def _run_exchange(ex, name):
    n_in, n_out = len(ex.ins), len(ex.out_shapes)

    def body(*refs):
        ins, outs, sems = refs[:n_in], refs[n_in:n_in + n_out], refs[n_in + n_out:]
        ex.start(ins, outs, sems)
        if ex.mid is not None:
            ex.mid(ins, outs, sems)
        ex.finish(ins, outs, sems)

    any_spec = pl.BlockSpec(memory_space=pl.ANY)
    return pl.pallas_call(
        body, name=name,
        in_specs=[any_spec] * n_in, out_specs=[any_spec] * n_out,
        out_shape=ex.out_shapes, scratch_shapes=ex.scratch,
    )(*ex.ins)


def _natural_to_head_major(n0, dk, dv):
    nh = N_HEADS
    qk, e, hw = nh * dk, nh * dv, 2 * dk + dv
    if n0 < qk:
        h, off = divmod(n0, dk)
        return h * hw + off, True
    n1 = n0 - qk
    if n1 < qk:
        h, off = divmod(n1, dk)
        return h * hw + dk + off, False
    part, n3 = divmod(n1 - qk, e)
    h, off = divmod(n3, dv)
    if part == 0:
        return h * hw + 2 * dk + off, False
    return nh * hw + h * 2 * dv + (part - 1) * dv + off, False


def _mlstm_weight_layout(w_blocks, dk, dv, name):
    _, d, blk = w_blocks.shape
    nh = N_HEADS
    hw = 2 * dk + 3 * dv
    n_tiles = nh * hw // LANES
    tiles_per_block = d // LANES
    tr = _pick(d, 128, 16)
    scale = dk ** -0.5

    def body(w_ref, p_ref, g_ref):
        lane = lax.broadcasted_iota(jnp.int32, (tr, LANES), 1)

        def tile(s, r):
            return w_ref[s, :, r * LANES:(r + 1) * LANES].astype(F32)

        def last_col(s):
            return w_ref[s, :, d:d + 1].astype(F32)

        for tn in range(n_tiles):
            s, r = divmod(tn, tiles_per_block)
            if s == 0:
                val = tile(0, r)
            elif r == 0:
                from_prev = pltpu.roll(tile(s - 1, tiles_per_block - 1), s - 1, 1)
                from_here = pltpu.roll(tile(s, 0), s, 1)
                val = jnp.where(lane < s - 1, from_prev, jnp.where(lane == s - 1, last_col(s - 1), from_here))
            else:
                slab = jnp.concatenate([tile(s, r - 1), tile(s, r)], axis=1)
                val = pltpu.roll(slab, s, 1)[:, LANES:]
            at, is_q = _natural_to_head_major(tn * LANES, dk, dv)
            if is_q:
                val = val * scale
            p_ref[:, at:at + LANES] = val.astype(BF16)
        n_gate = 2 * nh
        s = N_DEV - 1
        gates = pltpu.roll(tile(s, tiles_per_block - 1), n_gate - 1, 1)
        gates = jnp.where(lane < n_gate - 1, gates, jnp.where(lane == n_gate - 1, last_col(s), 0.0)).astype(BF16)
        g_ref[...] = gates
        p_ref[:, nh * hw:nh * hw + LANES] = gates

    return pl.pallas_call(
        body, name=name, grid=(d // tr,),
        in_specs=[pl.BlockSpec((N_DEV, tr, blk), lambda i: (0, i, 0))],
        out_specs=[pl.BlockSpec((tr, nh * hw + LANES), lambda i: (i, 0)), pl.BlockSpec((tr, LANES), lambda i: (i, 0))],
        out_shape=[jax.ShapeDtypeStruct((d, nh * hw + LANES), BF16), jax.ShapeDtypeStruct((d, LANES), BF16)],
        compiler_params=_params(),
    )(w_blocks)


def _mlstm_grad_layout(g_p1, row0, n_rows, dk, dv, name):
    d = g_p1.shape[0]
    nh = N_HEADS
    hw = 2 * dk + 3 * dv
    n_tiles = nh * hw // LANES
    tiles_per_block = d // LANES
    tr = _pick(d, 128, 16)
    scale = dk ** -0.5

    def body(p_ref, o_ref):
        def natural(tn):
            if tn == n_tiles:
                return p_ref[:, nh * hw:nh * hw + LANES].astype(F32)
            at, is_q = _natural_to_head_major(tn * LANES, dk, dv)
            val = p_ref[:, at:at + LANES].astype(F32)
            return val * scale if is_q else val

        for s in range(N_DEV):
            for r in range(tiles_per_block):
                tn = s * tiles_per_block + r
                if s == 0:
                    val = natural(tn)
                else:
                    slab = jnp.concatenate([natural(tn), natural(tn + 1)], axis=1)
                    val = pltpu.roll(slab, 2 * LANES - s, 1)[:, :LANES]
                o_ref[s, :, r * LANES:(r + 1) * LANES] = val.astype(BF16)
            o_ref[s, :, d:d + 1] = natural((s + 1) * tiles_per_block)[:, s:s + 1].astype(BF16)

    first = row0 // tr
    return pl.pallas_call(
        body, name=name, grid=(n_rows // tr,),
        in_specs=[pl.BlockSpec((tr, nh * hw + LANES), lambda i: (i + first, 0))],
        out_specs=pl.BlockSpec((N_DEV, tr, d + 1), lambda i: (0, i, 0)),
        out_shape=jax.ShapeDtypeStruct((N_DEV, n_rows, d + 1), BF16),
        compiler_params=_params(),
    )(g_p1)


def _adamw_math(w, g, m, v):
    m = ADAM_B1 * m + (1.0 - ADAM_B1) * g
    v = ADAM_B2 * v + (1.0 - ADAM_B2) * (g * g)
    m_hat = m / (1.0 - ADAM_B1 ** ADAM_STEP)
    v_hat = v / (1.0 - ADAM_B2 ** ADAM_STEP)
    delta = -ADAM_LR * (m_hat / (jnp.sqrt(v_hat) + ADAM_EPS) + ADAM_WD * w)
    return delta, m, v


def _adamw_sharded(parts, w, m, v, name):
    n_parts = len(parts)
    _, rk, c = parts[0].shape
    r = n_parts * rk
    tr = _pick(rk, 128, 8)
    per = rk // tr

    def body(*refs):
        p_refs = refs[:n_parts]
        w_ref, m_ref, v_ref, g_ref, d_ref, nm_ref, nv_ref, gsum_ref = refs[n_parts:]
        i = pl.program_id(0)
        for k, p_ref in enumerate(p_refs):
            @pl.when(i // per == k)
            def _(p_ref=p_ref):
                g = p_ref[0].astype(F32)
                for s in range(1, N_DEV):
                    g = g + p_ref[s].astype(F32)
                gsum_ref[...] = g
        g = gsum_ref[...]
        delta, m_new, v_new = _adamw_math(w_ref[...], g, m_ref[...], v_ref[...])
        g_ref[...] = g
        d_ref[...] = delta
        nm_ref[...] = m_new
        nv_ref[...] = v_new

    def part_spec(k):
        return pl.BlockSpec((N_DEV, tr, c), lambda i: (0, jnp.clip(i - k * per, 0, per - 1), 0))

    blk = pl.BlockSpec((tr, c), lambda i: (i, 0))
    return pl.pallas_call(
        body, name=name, grid=(r // tr,),
        in_specs=[part_spec(k) for k in range(n_parts)] + [blk, blk, blk],
        out_specs=[blk] * 4,
        out_shape=[jax.ShapeDtypeStruct((r, c), F32)] * 4,
        scratch_shapes=[pltpu.VMEM((tr, c), F32)],
        compiler_params=_params(),
    )(*parts, w, m, v)


def _sum_devices(parts, name):
    _, r, c = parts.shape

    def body(p_ref, o_ref):
        g = p_ref[0]
        for s in range(1, N_DEV):
            g = g + p_ref[s]
        o_ref[...] = g

    return pl.pallas_call(
        body, name=name, out_shape=jax.ShapeDtypeStruct((r, c), F32), compiler_params=_params(),
    )(parts)


def _adamw_small(gs, ws, ms, vs, name):
    n = len(gs)

    def body(*refs):
        g_refs, w_refs, m_refs, v_refs = refs[:n], refs[n:2 * n], refs[2 * n:3 * n], refs[3 * n:4 * n]
        d_refs, nm_refs, nv_refs = refs[4 * n:5 * n], refs[5 * n:6 * n], refs[6 * n:7 * n]
        for a in range(n):
            delta, m_new, v_new = _adamw_math(w_refs[a][...], g_refs[a][...], m_refs[a][...], v_refs[a][...])
            d_refs[a][...] = delta
            nm_refs[a][...] = m_new
            nv_refs[a][...] = v_new

    shapes = [jax.ShapeDtypeStruct(w.shape, F32) for w in ws]
    outs = pl.pallas_call(
        body, name=name, out_shape=shapes * 3, compiler_params=_params(),
    )(*gs, *ws, *ms, *vs)
    return outs[:n], outs[n:2 * n], outs[2 * n:]


def _pad_rows(a, rows):
    return jnp.pad(a, ((0, rows - a.shape[0]), (0, 0)))


def kernel(x, meta_tokens, norm_w, conv_in_w, conv_w, conv_out_w, mlstm_in_w, mlstm_gate_b, mlstm_head_norm_w, mlstm_out_w, final_norm_w, loss_target, m_meta_tokens, m_norm_w, m_conv_in_w, m_conv_w, m_conv_out_w, m_mlstm_in_w, m_mlstm_gate_b, m_mlstm_head_norm_w, m_mlstm_out_w, m_final_norm_w, v_meta_tokens, v_norm_w, v_conv_in_w, v_conv_w, v_conv_out_w, v_mlstm_in_w, v_mlstm_gate_b, v_mlstm_head_norm_w, v_mlstm_out_w, v_final_norm_w):
    seq, d = x.shape[1], x.shape[2]
    t = seq + LEAD
    e = 2 * d
    ec = e // 2
    nh = N_HEADS
    dv = e // nh
    dk = dv // 2
    qk = nh * dk
    hw = 2 * dk + 3 * dv
    n_in = 2 * qk + 3 * e + 2 * nh
    n_in_s = n_in // N_DEV
    me = 4 * lax.axis_index("x") + 2 * lax.axis_index("y") + lax.axis_index("c")
    tm = _pick(t, 832, 16)
    tm_in = _pick(t, 1664, 16)
    tkw = _pick(t, 2080, 16)

    small = jnp.concatenate([
        meta_tokens,
        _pad_rows(conv_w[0].reshape(3 * (e // N_DEV) // LANES, LANES), 8),
        _pad_rows(mlstm_head_norm_w[0].reshape((e // N_DEV) // LANES, LANES), 8),
    ], axis=0) if d // N_DEV == LANES else None
    assert small is not None, "the packed small-weight block assumes d_model / 8 == 128"
    (small_g,) = _run_exchange(_gather_exchange([small]), "gather_small_weights")
    meta_full = jnp.transpose(small_g[:, 0:N_META, :], (1, 0, 2)).reshape(N_META, d)
    cw_rows = 3 * (e // N_DEV) // LANES
    conv_w_full = jnp.transpose(
        small_g[:, N_META:N_META + cw_rows, :].reshape(N_DEV, 3, e // N_DEV), (1, 0, 2)).reshape(3, e)
    hn_rows = (e // N_DEV) // LANES
    head_w_full = small_g[:, N_META + 8:N_META + 8 + hn_rows, :].reshape(1, e)

    tgt = loss_target[0]

    ci_map = lambda blk: 2 * (blk % 4) + blk // 4
    (h0, u0), (w_ci,) = _rms_first(x[0], meta_full, norm_w[0:1], "rms0",
                                   exchange=_gather_exchange([conv_in_w[0].astype(BF16)]))
    p0, (w_co, w_mi) = _matmul(
        u0, w_ci, form="nn", m=t, n=8 * ec, kdim=d, tm=tm_in, tn=ec, tk=d, out_dtype=BF16, name="conv_in",
        b_spec=pl.BlockSpec((None, d, ec), lambda i, j, k: (ci_map(j), 0, 0)),
        exchange=_gather_exchange([conv_out_w[0].astype(BF16), mlstm_in_w[0].astype(BF16)]))
    w_co = w_co.reshape(e, d)
    (g0, y0), (w_mo,) = _conv_gate_fwd(p0, conv_w_full, "conv_gate",
                                       exchange=_gather_exchange([mlstm_out_w[0].astype(BF16)]))
    w_mo = w_mo.reshape(e, d)
    h1, u1 = _matmul(g0, w_co, form="nn", m=t, n=d, kdim=e, tm=tm, tn=d, tk=e, out_dtype=F32, name="conv_out",
                     residual=h0, norm_w=norm_w[1:2])

    assert n_in_s == d + 1 and w_mi.shape == (N_DEV, d, d + 1)
    w_p1, w_gate = _mlstm_weight_layout(w_mi, dk, dv, "mlstm_w_layout")
    w_gate_t = jnp.transpose(w_gate[:, 0:16])
    bias_row = jnp.pad(mlstm_gate_b, ((0, 0), (0, LANES - 2 * nh)))
    bias_col = jnp.pad(mlstm_gate_b.T, ((0, 16 - 2 * nh), (0, 0)))

    p1 = _matmul(u1, w_p1, form="nn", m=t, n=nh * hw, kdim=d, tm=tm_in, tn=_pick(nh * hw, 2048, LANES), tk=d,
                 out_dtype=BF16, name="mlstm_in")
    graw = _matmul(u1, w_gate, form="nn", m=t, n=LANES, kdim=d, tm=tm, tn=LANES, tk=d, out_dtype=F32,
                   name="gates_col")
    graw_t = _matmul(w_gate_t, u1, form="nt", m=16, n=t, kdim=d, tm=16, tn=_pick(t, 1664, LANES), tk=d,
                     out_dtype=F32, name="gates_row")
    colf, rowf = _gates_fwd(graw, graw_t, bias_row, bias_col, "gates_fwd")
    hh, stat, csave, nsave = _mlstm_fwd(p1, colf, rowf, dk, dv, "mlstm_fwd")
    hm = _head_gate_fwd(hh, p1, head_w_full, dv, "head_gate")
    h2 =_matmul(hm, w_mo, form="nn", m=t, n=d, kdim=e, tm=tm, tn=d, tk=e, out_dtype=F32, name="mlstm_out",
                 residual=h1)

    dh2, dwf, loss_part = _final_loss(h2, final_norm_w.reshape(1, d), tgt, "final_loss")

    dhm = _matmul(dh2, w_mo, form="nt", m=t, n=e, kdim=d, tm=tm, tn=_pick(e, 2048, LANES), tk=d, out_dtype=BF16,
                  name="mlstm_out_dx")
    g_mo = _matmul(hm, dh2, form="tn", m=e, n=d, kdim=t, tm=_pick(e, 1024, LANES), tn=d, tk=tkw, out_dtype=BF16,
                   name="mlstm_out_dw")
    n_p1 = nh * hw + LANES
    (dp1, gstat, dhead), (r_mo,) = _mlstm_bwd(
        p1, colf, rowf, head_w_full, hh, stat, csave, nsave, dhm, n_p1, dk, dv, "mlstm_bwd",
        exchange=_scatter_exchange([g_mo.reshape(N_DEV, e // N_DEV, d)]))
    dp1, dbias = _gates_bwd(gstat, colf, dp1, "gates_bwd")
    du1 = _matmul(dp1, w_p1, form="nt", m=t, n=d, kdim=n_p1, tm=tm_in, tn=d, tk=_pick(n_p1, 2048, LANES),
                  out_dtype=F32, name="mlstm_in_dx")
    g_p1 = _matmul(u1, dp1, form="tn", m=d, n=n_p1, kdim=t, tm=d, tn=_pick(n_p1, 2048, LANES), tk=tkw,
                   out_dtype=BF16, name="mlstm_in_dw")
    dh1, dnw1 = _rms_bwd(h1, norm_w[1:2], du1, dh2, "rms1_bwd")

    g_mi_a = _mlstm_grad_layout(g_p1, 0, d // 2, dk, dv, "mlstm_g_layout_a")
    g_mi_b = _mlstm_grad_layout(g_p1, d // 2, d // 2, dk, dv, "mlstm_g_layout_b")

    dg0 =_matmul(dh1, w_co, form="nt", m=t, n=e, kdim=d, tm=tm, tn=_pick(e, 2048, LANES), tk=d, out_dtype=BF16,
                  name="conv_out_dx")
    g_co = _matmul(g0, dh1, form="tn", m=e, n=d, kdim=t, tm=_pick(e, 1024, LANES), tn=d, tk=tkw, out_dtype=BF16,
                   name="conv_out_dw")
    (dp0, dconv), (r_co, r_mi_a) = _conv_gate_bwd(
        p0, y0, dg0, conv_w_full, "conv_gate_bwd",
        exchange=_scatter_exchange([g_co.reshape(N_DEV, e // N_DEV, d), g_mi_a]))
    g_ci, (r_mi_b,) = _matmul(
        u0, dp0, form="tn", m=d, n=8 * ec, kdim=t, tm=d, tn=ec, tk=tkw, out_dtype=BF16, name="conv_in_dw",
        out_shape=(N_DEV, d, ec), out_spec=pl.BlockSpec((None, d, ec), lambda i, j, k: (ci_map(j), 0, 0)),
        exchange=_scatter_exchange([g_mi_b]))
    du0, (r_ci,) = _matmul(
        dp0, w_ci, form="nt", m=t, n=d, kdim=8 * ec, tm=tm_in, tn=d, tk=ec, out_dtype=F32, name="conv_in_dx",
        b_spec=pl.BlockSpec((None, d, ec), lambda i, j, k: (ci_map(k), 0, 0)),
        exchange=_scatter_exchange([g_ci]))
    grad_x, dmeta, dnw0 = _rms_bwd_first(h0, norm_w[0:1], du0, dh1, "rms0_bwd")
    grad_x = grad_x[None]

    row8 = lax.broadcasted_iota(jnp.int32, (8, 1), 0)
    loss_wide = jnp.pad(loss_part, ((0, 0), (0, d - LANES)))
    payload = jnp.concatenate([
        jnp.where(row8 == 0, dnw0, jnp.where(row8 == 1, dnw1, 0.0)),
        jnp.where(row8 == 0, dwf, jnp.where(row8 == 1, loss_wide, 0.0)),
        jnp.where(row8 == 0, jnp.pad(dbias, ((0, 0), (0, d - LANES))), 0.0),
        dmeta,
        _pad_rows(dconv[0:3].reshape(3 * e // d, d), 8),
        _pad_rows(dhead[:, 0, :].reshape(e // d, d), 8),
    ], axis=0)
    (payload_g,) = _run_exchange(_gather_exchange([payload]), "gather_small_grads")
    tot = _sum_devices(payload_g, "sum_small_grads")

    loss = tot[9, 0]
    g_norm = tot[0:2]
    g_final = tot[8]
    g_gate_b = tot[16:17, 0:2 * nh]
    g_meta = lax.dynamic_slice(tot[24:24 + N_META], (0, me * (d // N_DEV)), (N_META, d // N_DEV))
    g_conv_w = lax.dynamic_slice(tot[40:40 + 3 * e // d].reshape(3, e), (0, me * (e // N_DEV)), (3, e // N_DEV))
    g_head = lax.dynamic_slice(tot[48:48 + e // d].reshape(1, e), (0, me * (e // N_DEV)), (1, e // N_DEV))

    g1, d1, nm1, nv1 = _adamw_sharded([r_ci], conv_in_w[0], m_conv_in_w[0], v_conv_in_w[0], "adamw_conv_in")
    g2, d2, nm2, nv2 = _adamw_sharded([r_co], conv_out_w[0], m_conv_out_w[0], v_conv_out_w[0], "adamw_conv_out")
    g3, d3, nm3, nv3 = _adamw_sharded([r_mi_a, r_mi_b], mlstm_in_w[0], m_mlstm_in_w[0], v_mlstm_in_w[0],
                                      "adamw_mlstm_in")
    g4, d4, nm4, nv4 = _adamw_sharded([r_mo], mlstm_out_w[0], m_mlstm_out_w[0], v_mlstm_out_w[0],
                                      "adamw_mlstm_out")

    small_g = [g_meta, g_norm, g_conv_w, g_gate_b, g_head, g_final.reshape(1, d)]
    small_w = [meta_tokens, norm_w, conv_w[0], mlstm_gate_b, mlstm_head_norm_w, final_norm_w.reshape(1, d)]
    small_m = [m_meta_tokens, m_norm_w, m_conv_w[0], m_mlstm_gate_b, m_mlstm_head_norm_w, m_final_norm_w.reshape(1, d)]
    small_v = [v_meta_tokens, v_norm_w, v_conv_w[0], v_mlstm_gate_b, v_mlstm_head_norm_w, v_final_norm_w.reshape(1, d)]
    sd, snm, snv = _adamw_small(small_g, small_w, small_m, small_v, "adamw_small")

    def order(meta, norm, cin, cw, cout, min_, gb, hn, mout, fin):
        return (meta, norm, cin[None], cw[None], cout[None], min_[None], gb, hn, mout[None], fin.reshape(d))

    grads = order(g_meta, g_norm, g1, g_conv_w, g2, g3, g_gate_b, g_head, g4, g_final)
    deltas = order(sd[0], sd[1], d1, sd[2], d2, d3, sd[3], sd[4], d4, sd[5])
    new_m = order(snm[0], snm[1], nm1, snm[2], nm2, nm3, snm[3], snm[4], nm4, snm[5])
    new_v = order(snv[0], snv[1], nv1, snv[2], nv2, nv3, snv[3], snv[4], nv4, snv[5])
    return (loss, grad_x, *grads, *deltas, *new_m, *new_v)
```
